```python
import math
import jax, jax.numpy as jnp
from jax import lax
import numpy as np

D_MODEL = 1024
BATCH = 8
SEQ = 2048
DEPTH = 2
DEC_BATCH = 128
DEC_SEQ = 1
PAST_LEN = 16384
PAGE_SIZE = 128

N_MIXERS = 4
W_GROUP = D_MODEL // N_MIXERS
HEAD_DIM = 64
N_HEADS = W_GROUP // HEAD_DIM
S5_CH = 16
S5_GROUPS = W_GROUP // S5_CH
S5_STATE = 64
GLA_RANK = 16
GLA_GATE_TEMP = 16.0
RWKV_W_RANK = 32
RWKV_A_RANK = 32
RWKV_G_RANK = 64
RWKV_COLS = 3 * W_GROUP + RWKV_W_RANK + RWKV_A_RANK + RWKV_G_RANK
RWKV_DECAY_SCALE = 0.6065306597126334
D_FF = 2816
CONV_W = 3
CHUNK = 64
EPS = 1e-6
GN_EPS = 64e-5
IN_SIZES = (W_GROUP,
            W_GROUP, W_GROUP, W_GROUP, N_HEADS, N_HEADS, W_GROUP,
            W_GROUP, W_GROUP, W_GROUP, GLA_RANK, W_GROUP,
            RWKV_COLS)
D_IN = 9 * W_GROUP + 2 * N_HEADS + GLA_RANK + RWKV_COLS
F32 = jnp.float32

kernel_name = 'hybrid_s5_mlstm_gla_rwkv7_step'


def _split(a, sizes):
    idx, acc = [], 0
    for s in sizes[:-1]:
        acc += s
        idx.append(acc)
    return jnp.split(a, idx, axis=-1)


def _heads(a):
    return a.reshape(a.shape[0], a.shape[1], N_HEADS, HEAD_DIM)


def rmsnorm(x, g):
    x32 = x.astype(F32)
    y = x32 * lax.rsqrt(jnp.mean(x32 * x32, axis=-1, keepdims=True) + EPS)
    return (y * g.astype(F32)).astype(x.dtype)


def head_rmsnorm(y, g):
    y = y * lax.rsqrt(jnp.mean(y * y, axis=-1, keepdims=True) + EPS)
    return y.reshape(y.shape[0], y.shape[1], -1) * g


def head_layernorm(y, g):
    mu = jnp.mean(y, axis=-1, keepdims=True)
    yc = y - mu
    y = yc * lax.rsqrt(jnp.mean(yc * yc, axis=-1, keepdims=True) + GN_EPS)
    return y.reshape(y.shape[0], y.shape[1], -1) * g


def s5_mixer(u, h0_re, h0_im, lam_re, lam_im, log_dt, b_re, b_im, c_re, c_im, d_skip, w_glu):
    Bsz, T, _ = u.shape
    uc = u.reshape(Bsz, T, S5_GROUPS, S5_CH).astype(jnp.complex64)
    lam = lax.complex(lam_re.astype(F32), lam_im.astype(F32))
    dt = jnp.exp(log_dt.astype(F32))[:, None]
    lam_bar = jnp.exp(lam * dt)
    b = lax.complex(b_re.astype(F32), b_im.astype(F32))
    b_bar = ((lam_bar - 1.0) / lam)[..., None] * b
    bu = jnp.einsum('gph,btgh->btgp', b_bar, uc)
    a = jnp.broadcast_to(lam_bar, bu.shape)

    def combine(e1, e2):
        a1, x1 = e1
        a2, x2 = e2
        return a1 * a2, a2 * x1 + x2

    a_cum, hs = lax.associative_scan(combine, (a, bu), axis=1)
    h0 = lax.complex(h0_re.astype(F32), h0_im.astype(F32))
    hs = hs + a_cum * h0[:, None]
    c = lax.complex(c_re.astype(F32), c_im.astype(F32))
    y = jnp.einsum('ghp,btgp->btgh', c, hs).real.reshape(Bsz, T, W_GROUP) + d_skip * u
    z = jax.nn.gelu(y)
    out = z * jax.nn.sigmoid(z @ w_glu)
    h_last = hs[:, -1]
    return out, h_last.real, h_last.imag


def _to_chunks(a, L):
    nc = a.shape[1] // L
    return jnp.moveaxis(a.reshape(a.shape[0], nc, L, *a.shape[2:]), 1, 0)


def _from_chunks(a):
    a = jnp.moveaxis(a, 0, 1)
    return a.reshape(a.shape[0], a.shape[1] * a.shape[2], *a.shape[3:])


def mlstm_chunked(q, k, v, ig, logf, C0, n0, m0):
    T = q.shape[1]
    L = math.gcd(T, CHUNK)
    causal = jnp.tril(jnp.ones((L, L), dtype=bool))[None, :, :, None]

    def step(carry, inp):
        C, n, m = carry
        qc, kc, vc, ic, fc = inp
        bc = jnp.cumsum(fc, axis=1)
        dmat = bc[:, :, None, :] - bc[:, None, :, :] + ic[:, None, :, :]
        dmat = jnp.where(causal, dmat, -jnp.inf)
        g = bc + m[:, None, :]
        m_row = jnp.maximum(g, jnp.max(dmat, axis=2))
        w = jnp.exp(dmat - m_row[:, :, None, :])
        w0 = jnp.exp(g - m_row)
        ws = w * jnp.einsum('bthd,bshd->btsh', qc, kc)
        num = w0[..., None] * jnp.einsum('bhvd,bthd->bthv', C, qc) + jnp.einsum('btsh,bshv->bthv', ws, vc)
        den = w0 * jnp.einsum('bhd,bthd->bth', n, qc) + jnp.sum(ws, axis=2)
        hc = num / jnp.maximum(jnp.abs(den), jnp.exp(-m_row))[..., None]
        m_new = m_row[:, -1]
        wl = jnp.exp(bc[:, -1:, :] - bc + ic - m_new[:, None, :])
        w0l = jnp.exp(bc[:, -1] + m - m_new)
        C_new = w0l[..., None, None] * C + jnp.einsum('bsh,bshv,bshd->bhvd', wl, vc, kc)
        n_new = w0l[..., None] * n + jnp.einsum('bsh,bshd->bhd', wl, kc)
        return (C_new, n_new, m_new), hc

    xs = (_to_chunks(q, L), _to_chunks(k, L), _to_chunks(v, L), _to_chunks(ig, L), _to_chunks(logf, L))
    (C, n, m), hs = lax.scan(step, (C0, n0, m0), xs)
    return _from_chunks(hs), C, n, m


def gla_chunked(q, k, v, log_alpha, S0):
    T = q.shape[1]
    L = math.gcd(T, CHUNK)
    causal = jnp.tril(jnp.ones((L, L), dtype=bool))[None, :, :, None, None]

    def step(S, inp):
        qc, kc, vc, lc = inp
        bc = jnp.cumsum(lc, axis=1)
        diff = bc[:, :, None] - bc[:, None, :]
        decay = jnp.where(causal, jnp.exp(jnp.where(causal, diff, 0.0)), 0.0)
        att = jnp.einsum('bthd,btshd,bshd->btsh', qc, decay, kc)
        o = jnp.einsum('btsh,bshv->bthv', att, vc) + jnp.einsum('bthd,bhdv->bthv', qc * jnp.exp(bc), S)
        b_last = bc[:, -1]
        S_new = jnp.exp(b_last)[..., None] * S + jnp.einsum('bshd,bshv->bhdv', kc * jnp.exp(b_last[:, None] - bc), vc)
        return S_new, o

    xs = (_to_chunks(q, L), _to_chunks(k, L), _to_chunks(v, L), _to_chunks(log_alpha, L))
    S, os = lax.scan(step, S0, xs)
    return _from_chunks(os), S


def rwkv7_scan(r, w, k, v, kk, a, S0):
    def step(S, inp):
        r_t, w_t, k_t, v_t, kk_t, a_t = inp
        sk = jnp.einsum('bhvk,bhk->bhv', S, kk_t)
        S = S * w_t[:, :, None, :] - sk[..., None] * (a_t * kk_t)[:, :, None, :] + v_t[..., None] * k_t[:, :, None, :]
        return S, jnp.einsum('bhvk,bhk->bhv', S, r_t)

    xs = tuple(jnp.moveaxis(t, 1, 0) for t in (r, w, k, v, kk, a))
    S, o = lax.scan(step, S0, xs)
    return jnp.moveaxis(o, 0, 1), S


def hybrid_layer(x, s5_re, s5_im, ml_C, ml_n, ml_m, gla_S, rw_S, rw_shift, ffn_buf,
                 norm_mix, w_in, s5_lam_re, s5_lam_im, s5_log_dt, s5_b_re, s5_b_im,
                 s5_c_re, s5_c_im, s5_d, s5_w_glu, ml_gate_bias, ml_norm,
                 gla_w_alpha, gla_b_alpha, gla_norm, rw_mu, rw_w0, rw_w2, rw_a0,
                 rw_a2, rw_g2, rw_k_k, rw_k_a, rw_r_k, rw_norm, w_out, norm_ffn,
                 ffn_w_up, ffn_conv_w, ffn_conv_b, ffn_w_down):
    B, T, _ = x.shape
    h = rmsnorm(x, norm_mix)
    proj = (h @ w_in).astype(F32)
    (u, mq, mk, mv, mi, mf, mo, gq, gk, gv, ga, gg, rcols) = _split(proj, IN_SIZES)

    y_s5, s5_re_new, s5_im_new = s5_mixer(u, s5_re, s5_im, s5_lam_re, s5_lam_im, s5_log_dt,
                                          s5_b_re, s5_b_im, s5_c_re, s5_c_im, s5_d, s5_w_glu)

    i_pre = mi + ml_gate_bias[:N_HEADS]
    logf = jax.nn.log_sigmoid(mf + ml_gate_bias[N_HEADS:])
    h_ml, ml_C_new, ml_n_new, ml_m_new = mlstm_chunked(
        _heads(mq), _heads(mk) * HEAD_DIM ** -0.5, _heads(mv), i_pre, logf,
        ml_C.astype(F32), ml_n.astype(F32), ml_m.astype(F32))
    y_ml = head_rmsnorm(h_ml, ml_norm) * jax.nn.sigmoid(mo)

    log_alpha = jax.nn.log_sigmoid(ga @ gla_w_alpha + gla_b_alpha) / GLA_GATE_TEMP
    o_gla, gla_S_new = gla_chunked(_heads(gq) * HEAD_DIM ** -0.5, _heads(gk), _heads(gv),
                                   _heads(log_alpha), gla_S.astype(F32))
    y_gla = head_rmsnorm(o_gla, gla_norm) * jax.nn.silu(gg)

    prev = jnp.concatenate([rw_shift.astype(F32)[:, None], rcols[:, :-1]], axis=1)
    xm = rcols + rw_mu * (prev - rcols)
    rr, rk, rv, rwd, rad, rgd = _split(xm, (W_GROUP, W_GROUP, W_GROUP, RWKV_W_RANK, RWKV_A_RANK, RWKV_G_RANK))
    decay = jnp.exp(-RWKV_DECAY_SCALE * jax.nn.sigmoid(rw_w0 + jnp.tanh(rwd) @ rw_w2))
    a = jax.nn.sigmoid(rw_a0 + rad @ rw_a2)
    g = jax.nn.sigmoid(rgd) @ rw_g2
    kk = _heads(rk * rw_k_k)
    kk = kk * lax.rsqrt(jnp.maximum(jnp.sum(kk * kk, axis=-1, keepdims=True), 1e-24))
    kt = rk * (1.0 + (a - 1.0) * rw_k_a)
    rh, kth, vh = _heads(rr), _heads(kt), _heads(rv)
    o_rw, rw_S_new = rwkv7_scan(rh, _heads(decay), kth, vh, kk, _heads(a), rw_S.astype(F32))
    bonus = jnp.sum(rh * kth * rw_r_k, axis=-1, keepdims=True) * vh
    y_rw = (head_layernorm(o_rw, rw_norm) + bonus.reshape(B, T, W_GROUP)) * g

    mix = jnp.concatenate([y_s5, y_ml, y_gla, y_rw], axis=-1) @ w_out
    x = x + mix.astype(x.dtype)

    h2 = rmsnorm(x, norm_ffn)
    up = (h2 @ ffn_w_up).astype(F32)
    ug, uv = jnp.split(up, 2, axis=-1)
    padded = jnp.concatenate([ffn_buf.astype(F32), ug], axis=1)
    conv = ffn_conv_b.astype(F32)
    for j in range(CONV_W):
        conv = conv + ffn_conv_w[j] * padded[:, j:j + T]
    ffn_buf_new = padded[:, -(CONV_W - 1):]
    ffn = (jax.nn.gelu(conv) * uv) @ ffn_w_down
    x = x + ffn.astype(x.dtype)
    new_states = (s5_re_new, s5_im_new, ml_C_new, ml_n_new, ml_m_new, gla_S_new, rw_S_new,
                  rcols[:, -1], ffn_buf_new)
    return x, new_states


def run_trunk(x, states, layer_params, norm_final):
    outs = [[] for _ in states]
    for l in range(DEPTH):
        x, new = hybrid_layer(x, *[s[l] for s in states], *[p[l] for p in layer_params])
        for lst, s in zip(outs, new):
            lst.append(s)
    return rmsnorm(x, norm_final), tuple(jnp.stack(lst) for lst in outs)


def setup_inputs(seed: int = 0) -> dict:
    key = jax.random.key(seed)
    keys = jax.random.split(key, 64)
    counter = [0]

    def nk():
        counter[0] += 1
        return keys[counter[0] - 1]

    def nrm(shape, scale):
        return scale * jax.random.normal(nk(), shape, F32)

    def uni(shape, lo, hi):
        return jax.random.uniform(nk(), shape, F32, lo, hi)

    L, H, dh = DEPTH, N_HEADS, HEAD_DIM
    lam_im0 = jnp.pi * jnp.arange(S5_STATE, dtype=F32)
    return {
        'x_prompt': nrm((BATCH, SEQ, D_MODEL), 1.0),
        'x_sample': nrm((DEC_BATCH, DEC_SEQ, D_MODEL), 1.0),
        'state_s5_re': nrm((L, DEC_BATCH, S5_GROUPS, S5_STATE), 0.5),
        'state_s5_im': nrm((L, DEC_BATCH, S5_GROUPS, S5_STATE), 0.5),
        'state_mlstm_C': nrm((L, DEC_BATCH, H, dh, dh), 0.5),
        'state_mlstm_n': nrm((L, DEC_BATCH, H, dh), 0.5),
        'state_mlstm_m': nrm((L, DEC_BATCH, H), 1.0),
        'state_gla_S': nrm((L, DEC_BATCH, H, dh, dh), 0.5),
        'state_rwkv_S': nrm((L, DEC_BATCH, H, dh, dh), 0.5),
        'state_rwkv_shift': nrm((L, DEC_BATCH, RWKV_COLS), 1.0),
        'state_ffn_conv': nrm((L, DEC_BATCH, CONV_W - 1, D_FF), 1.0),
        'norm_mix': 1.0 + nrm((L, D_MODEL), 0.02),
        'w_in': nrm((L, D_MODEL, D_IN), D_MODEL ** -0.5),
        's5_lam_re': -0.5 + nrm((L, S5_GROUPS, S5_STATE), 0.01),
        's5_lam_im': lam_im0 + nrm((L, S5_GROUPS, S5_STATE), 0.01),
        's5_log_dt': uni((L, S5_GROUPS), math.log(1e-3), math.log(1e-1)),
        's5_b_re': nrm((L, S5_GROUPS, S5_STATE, S5_CH), (2.0 * S5_CH) ** -0.5),
        's5_b_im': nrm((L, S5_GROUPS, S5_STATE, S5_CH), (2.0 * S5_CH) ** -0.5),
        's5_c_re': nrm((L, S5_GROUPS, S5_CH, S5_STATE), (2.0 * S5_STATE) ** -0.5),
        's5_c_im': nrm((L, S5_GROUPS, S5_CH, S5_STATE), (2.0 * S5_STATE) ** -0.5),
        's5_d': nrm((L, W_GROUP), 1.0),
        's5_w_glu': nrm((L, W_GROUP, W_GROUP), W_GROUP ** -0.5),
        'ml_gate_bias': jnp.concatenate([nrm((L, H), 0.1), 3.0 + nrm((L, H), 0.5)], axis=-1),
        'ml_norm': 1.0 + nrm((L, W_GROUP), 0.02),
        'gla_w_alpha': nrm((L, GLA_RANK, W_GROUP), GLA_RANK ** -0.5),
        'gla_b_alpha': nrm((L, W_GROUP), 0.1),
        'gla_norm': 1.0 + nrm((L, W_GROUP), 0.02),
        'rw_mu': uni((L, RWKV_COLS), 0.0, 1.0),
        'rw_w0': nrm((L, W_GROUP), 0.5),
        'rw_w2': nrm((L, RWKV_W_RANK, W_GROUP), 0.1 * RWKV_W_RANK ** -0.5),
        'rw_a0': nrm((L, W_GROUP), 0.1),
        'rw_a2': nrm((L, RWKV_A_RANK, W_GROUP), 0.1 * RWKV_A_RANK ** -0.5),
        'rw_g2': nrm((L, RWKV_G_RANK, W_GROUP), RWKV_G_RANK ** -0.5),
        'rw_k_k': 0.85 + nrm((L, W_GROUP), 0.02),
        'rw_k_a': 1.0 + nrm((L, W_GROUP), 0.02),
        'rw_r_k': nrm((L, H, dh), 0.1),
        'rw_norm': 1.0 + nrm((L, W_GROUP), 0.02),
        'w_out': nrm((L, D_MODEL, D_MODEL), D_MODEL ** -0.5),
        'norm_ffn': 1.0 + nrm((L, D_MODEL), 0.02),
        'ffn_w_up': nrm((L, D_MODEL, 2 * D_FF), D_MODEL ** -0.5),
        'ffn_conv_w': nrm((L, CONV_W, D_FF), CONV_W ** -0.5),
        'ffn_conv_b': nrm((L, D_FF), 0.02),
        'ffn_w_down': nrm((L, D_FF, D_MODEL), D_FF ** -0.5),
        'norm_final': 1.0 + nrm((D_MODEL,), 0.02),
    }


def reference(x_prompt, x_sample, state_s5_re, state_s5_im, state_mlstm_C, state_mlstm_n,
              state_mlstm_m, state_gla_S, state_rwkv_S, state_rwkv_shift, state_ffn_conv,
              norm_mix, w_in, s5_lam_re, s5_lam_im, s5_log_dt, s5_b_re, s5_b_im,
              s5_c_re, s5_c_im, s5_d, s5_w_glu, ml_gate_bias, ml_norm,
              gla_w_alpha, gla_b_alpha, gla_norm, rw_mu, rw_w0, rw_w2, rw_a0,
              rw_a2, rw_g2, rw_k_k, rw_k_a, rw_r_k, rw_norm, w_out, norm_ffn,
              ffn_w_up, ffn_conv_w, ffn_conv_b, ffn_w_down, norm_final):
    layer_params = (norm_mix, w_in, s5_lam_re, s5_lam_im, s5_log_dt, s5_b_re, s5_b_im,
                    s5_c_re, s5_c_im, s5_d, s5_w_glu, ml_gate_bias, ml_norm,
                    gla_w_alpha, gla_b_alpha, gla_norm, rw_mu, rw_w0, rw_w2, rw_a0,
                    rw_a2, rw_g2, rw_k_k, rw_k_a, rw_r_k, rw_norm, w_out, norm_ffn,
                    ffn_w_up, ffn_conv_w, ffn_conv_b, ffn_w_down)
    L, H, dh = DEPTH, N_HEADS, HEAD_DIM
    prompt_states = (jnp.zeros((L, BATCH, S5_GROUPS, S5_STATE), F32),
                     jnp.zeros((L, BATCH, S5_GROUPS, S5_STATE), F32),
                     jnp.zeros((L, BATCH, H, dh, dh), F32),
                     jnp.zeros((L, BATCH, H, dh), F32),
                     jnp.zeros((L, BATCH, H), F32),
                     jnp.zeros((L, BATCH, H, dh, dh), F32),
                     jnp.zeros((L, BATCH, H, dh, dh), F32),
                     jnp.zeros((L, BATCH, RWKV_COLS), F32),
                     jnp.zeros((L, BATCH, CONV_W - 1, D_FF), F32))
    sample_states = (state_s5_re, state_s5_im, state_mlstm_C, state_mlstm_n, state_mlstm_m,
                     state_gla_S, state_rwkv_S, state_rwkv_shift, state_ffn_conv)
    y_prompt, new_p = run_trunk(x_prompt, prompt_states, layer_params, norm_final)
    y_sample, new_s = run_trunk(x_sample, sample_states, layer_params, norm_final)
    (p_s5_re, p_s5_im, p_ml_C, p_ml_n, p_ml_m, p_gla_S, p_rw_S, p_rw_shift, p_ffn_conv) = new_p
    (s_s5_re, s_s5_im, s_ml_C, s_ml_n, s_ml_m, s_gla_S, s_rw_S, s_rw_shift, s_ffn_conv) = new_s
    return (y_prompt, y_sample,
            p_s5_re, p_s5_im, p_ml_C, p_ml_n, p_ml_m, p_gla_S, p_rw_S, p_rw_shift, p_ffn_conv,
            s_s5_re, s_s5_im, s_ml_C, s_ml_n, s_ml_m, s_gla_S, s_rw_S, s_rw_shift, s_ffn_conv)
```

```python
import functools
import math

import jax
import jax.numpy as jnp
from jax import lax
from jax.experimental import pallas as pl
from jax.experimental.pallas import tpu as pltpu

F32 = jnp.float32
BF16 = jnp.bfloat16

LANES = 128
W_GROUP = 256
HEAD_DIM = 64
HEAD_SHIFT = 6
N_HEADS = 4
S5_CH = 16
S5_GROUPS = 16
S5_STATE = 64
S5_WIDTH = S5_GROUPS * S5_STATE
GLA_RANK = 16
GLA_GATE_TEMP = 16.0
RWKV_W_RANK = 32
RWKV_A_RANK = 32
RWKV_G_RANK = 64
RWKV_COLS = 3 * W_GROUP + RWKV_W_RANK + RWKV_A_RANK + RWKV_G_RANK
RWKV_DECAY_SCALE = 0.6065306597126334
CONV_W = 3
EPS = 1e-6
GN_EPS = 64e-5
NEG_BIG = -1e30
EXP_CLAMP = 80.0

SMALL_W = LANES
OFF_SMALL = RWKV_COLS
OFF_MAIN = RWKV_COLS + SMALL_W
D_PROJ = OFF_MAIN + 9 * W_GROUP
LANE_MI = GLA_RANK
LANE_MF = GLA_RANK + N_HEADS
(BLK_U, BLK_MQ, BLK_MK, BLK_MV, BLK_MO, BLK_GQ, BLK_GK, BLK_GV, BLK_GG) = range(
    OFF_MAIN // W_GROUP, OFF_MAIN // W_GROUP + 9)
BLK_SMALL = OFF_SMALL // SMALL_W

CHUNK_ML = 128
CHUNK_RW = 64
VMEM_LIMIT = 56 * 1024 * 1024


def _cparams(*sem):
    return pltpu.CompilerParams(dimension_semantics=sem, vmem_limit_bytes=VMEM_LIMIT)


def _dot(a, b):
    return jnp.dot(a.astype(BF16), b.astype(BF16), preferred_element_type=F32)


def _dot_nt(a, b):
    return lax.dot_general(a.astype(BF16), b.astype(BF16), (((1,), (1,)), ((), ())),
                           preferred_element_type=F32)


def _dot_tn(a, b):
    return lax.dot_general(a.astype(BF16), b.astype(BF16), (((0,), (0,)), ((), ())),
                           preferred_element_type=F32)


def _split3(x):
    hi = x.astype(BF16)
    r1 = x - hi.astype(F32)
    mid = r1.astype(BF16)
    lo = (r1 - mid.astype(F32)).astype(BF16)
    return hi, mid, lo


def _dot_exact_rhs(a01, x):
    hi, mid, lo = _split3(x)
    f = lambda p: jnp.dot(a01, p, preferred_element_type=F32)
    return f(hi) + f(mid) + f(lo)


def _dot_exact_lhs(x, b01):
    hi, mid, lo = _split3(x)
    f = lambda p: jnp.dot(p, b01, preferred_element_type=F32)
    return f(hi) + f(mid) + f(lo)


def _sigmoid(x):
    return 1.0 / (1.0 + jnp.exp(-x))


def _log_sigmoid(x):
    return jnp.minimum(x, 0.0) - jnp.log(1.0 + jnp.exp(-jnp.abs(x)))


def _gelu(x):
    return 0.5 * x * (1.0 + jnp.tanh(math.sqrt(2.0 / math.pi) * (x + 0.044715 * (x * x * x))))


def _iota(shape, dim):
    return lax.broadcasted_iota(jnp.int32, shape, dim)


def _head_ones():
    r = _iota((W_GROUP, W_GROUP), 0) >> HEAD_SHIFT
    c = _iota((W_GROUP, W_GROUP), 1) >> HEAD_SHIFT
    return jnp.where(r == c, 1.0, 0.0).astype(BF16)


def _head_sum(x, ones):
    return _dot_exact_lhs(x, ones)


def _head_lane_mask(h, rows=1):
    lane = _iota((rows, W_GROUP), 1) >> HEAD_SHIFT
    return jnp.where(lane == h, 1.0, 0.0).astype(F32)


def _rmsnorm_rows(x, g):
    ms = jnp.mean(x * x, axis=-1, keepdims=True)
    return x * lax.rsqrt(ms + EPS) * g


def _tril_ones(n, strict=False):
    r = _iota((n, n), 0)
    c = _iota((n, n), 1)
    return jnp.where((r > c) if strict else (r >= c), 1.0, 0.0).astype(BF16)


def _triu_ones(n):
    r = _iota((n, n), 0)
    c = _iota((n, n), 1)
    return jnp.where(r <= c, 1.0, 0.0).astype(BF16)


def _inproj_kernel(x_ref, g_ref, w_ref, wg_ref, proj_ref, gt_ref):
    hn = _rmsnorm_rows(x_ref[...], g_ref[...]).astype(BF16)
    proj_ref[...] = jnp.dot(hn, w_ref[...], preferred_element_type=F32)
    gt_ref[0] = lax.dot_general(wg_ref[...], hn, (((1,), (1,)), ((), ())),
                                preferred_element_type=F32)


def _inproj(x2d, g, w, wg, n_seq, t_len, tm):
    n, d = x2d.shape
    tiles_per_seq = t_len // tm
    return pl.pallas_call(
        _inproj_kernel,
        grid=(n // tm,),
        in_specs=[
            pl.BlockSpec((tm, d), lambda i: (i, 0)),
            pl.BlockSpec((1, d), lambda i: (0, 0)),
            pl.BlockSpec((d, D_PROJ), lambda i: (0, 0)),
            pl.BlockSpec((8, d), lambda i: (0, 0)),
        ],
        out_specs=[
            pl.BlockSpec((tm, D_PROJ), lambda i: (i, 0)),
            pl.BlockSpec((1, 8, tm), lambda i: (i // tiles_per_seq, 0, i % tiles_per_seq)),
        ],
        out_shape=[
            jax.ShapeDtypeStruct((n, D_PROJ), F32),
            jax.ShapeDtypeStruct((n_seq, 8, t_len), F32),
        ],
        compiler_params=_cparams("arbitrary"),
        name="inproj",
    )(x2d, g, w, wg)


def _s5_kernel(u_ref, h0re_ref, h0im_ref, bmat_ref, lre_ref, lim_ref, cmat_ref, d_ref, wglu_ref,
               y_ref, hre_out, him_out, bu_scr, hs_scr, hre_scr, him_scr, *, bp, tt):
    i = pl.program_id(0)

    @pl.when(i == 0)
    def _():
        hre_scr[...] = h0re_ref[...]
        him_scr[...] = h0im_ref[...]

    u = u_ref[...]
    bu_scr[...] = jnp.dot(u.astype(BF16), bmat_ref[...], preferred_element_type=F32)
    lre = lre_ref[...]
    lim = lim_ref[...]

    def step(t, carry):
        hre, him = carry
        r0 = pl.multiple_of(t * bp, bp)
        bre = bu_scr[pl.ds(r0, bp), 0:S5_WIDTH]
        bim = bu_scr[pl.ds(r0, bp), S5_WIDTH:2 * S5_WIDTH]
        nre = lre * hre - lim * him + bre
        nim = lre * him + lim * hre + bim
        hs_scr[pl.ds(r0, bp), 0:S5_WIDTH] = nre
        hs_scr[pl.ds(r0, bp), S5_WIDTH:2 * S5_WIDTH] = nim
        return nre, nim

    carry = (hre_scr[...], him_scr[...])
    if tt == 1:
        hre, him = step(0, carry)
    else:
        hre, him = lax.fori_loop(0, tt, step, carry)
    hre_scr[...] = hre
    him_scr[...] = him
    hre_out[...] = hre
    him_out[...] = him
    y = jnp.dot(hs_scr[...].astype(BF16), cmat_ref[...], preferred_element_type=F32) + d_ref[...] * u
    z = _gelu(y)
    y_ref[...] = (z * _sigmoid(_dot(z, wglu_ref[...]))).astype(y_ref.dtype)


def _s5(u_tb, h0re, h0im, sp, bp, t_len, tt):
    n = u_tb.shape[0]
    full = lambda shape: pl.BlockSpec(shape, lambda i: (0,) * len(shape))
    return pl.pallas_call(
        functools.partial(_s5_kernel, bp=bp, tt=tt),
        grid=(t_len // tt,),
        in_specs=[
            pl.BlockSpec((tt * bp, W_GROUP), lambda i: (i, 0)),
            full((bp, S5_WIDTH)), full((bp, S5_WIDTH)),
            full((W_GROUP, 2 * S5_WIDTH)),
            full((bp, S5_WIDTH)), full((bp, S5_WIDTH)),
            full((2 * S5_WIDTH, W_GROUP)),
            full((1, W_GROUP)),
            full((W_GROUP, W_GROUP)),
        ],
        out_specs=[
            pl.BlockSpec((tt * bp, W_GROUP), lambda i: (i, 0)),
            full((bp, S5_WIDTH)), full((bp, S5_WIDTH)),
        ],
        out_shape=[
            jax.ShapeDtypeStruct((n, W_GROUP), BF16),
            jax.ShapeDtypeStruct((bp, S5_WIDTH), F32),
            jax.ShapeDtypeStruct((bp, S5_WIDTH), F32),
        ],
        scratch_shapes=[
            pltpu.VMEM((tt * bp, 2 * S5_WIDTH), F32),
            pltpu.VMEM((tt * bp, 2 * S5_WIDTH), F32),
            pltpu.VMEM((bp, S5_WIDTH), F32),
            pltpu.VMEM((bp, S5_WIDTH), F32),
        ],
        compiler_params=_cparams("arbitrary"),
        name="s5_scan",
    )(u_tb, h0re, h0im, sp["bmat"], jnp.broadcast_to(sp["lre"], (bp, S5_WIDTH)),
      jnp.broadcast_to(sp["lim"], (bp, S5_WIDTH)), sp["cmat"], sp["d"], sp["wglu"])


def _mlstm_kernel(q_ref, k_ref, v_ref, og_ref, sm_ref, gt_ref, brow_ref, bcol_ref, norm_ref,
                  y_ref, c_out, n_out, m_out, c_scr, n_scr, m_scr, *, L, nc):
    c = pl.program_id(1)

    @pl.when(c == 0)
    def _():
        c_scr[...] = jnp.zeros_like(c_scr)
        n_scr[...] = jnp.zeros_like(n_scr)
        m_scr[...] = jnp.zeros_like(m_scr)

    q = q_ref[...]
    k = k_ref[...] * (HEAD_DIM ** -0.5)
    v = v_ref[...]
    pre = sm_ref[...] + brow_ref[...]
    lf_col = _log_sigmoid(pre)
    gtb = gt_ref[0] + bcol_ref[...]
    lf_row = _log_sigmoid(gtb)
    bc_col = _dot_exact_rhs(_tril_ones(L), lf_col)
    bc_row = _dot_exact_lhs(lf_row, _triu_ones(L))
    C = c_scr[...]
    n = n_scr[0:1, :]
    qC = _dot_nt(q, C)
    qn = q * n
    causal = _iota((L, L), 0) >= _iota((L, L), 1)
    ones = _head_ones()

    hc = jnp.zeros((L, W_GROUP), F32)
    wl_all = jnp.zeros((L, W_GROUP), F32)
    w0l_all = jnp.zeros((1, W_GROUP), F32)
    for h in range(N_HEADS):
        mask = _head_lane_mask(h)
        bcc = bc_col[:, LANE_MF + h:LANE_MF + h + 1]
        bcr = bc_row[N_HEADS + h:N_HEADS + h + 1, :]
        i_row = gtb[h:h + 1, :]
        i_col = pre[:, LANE_MI + h:LANE_MI + h + 1]
        m0 = m_scr[h:h + 1, 0:1]
        dmat = jnp.where(causal, bcc - bcr + i_row, NEG_BIG)
        g = bcc + m0
        m_row = jnp.maximum(g, jnp.max(dmat, axis=1, keepdims=True))
        w = jnp.exp(dmat - m_row)
        w0 = jnp.exp(g - m_row)
        ws = w * _dot_nt(q * mask, k)
        num = w0 * qC + _dot(ws, v * mask)
        den = (w0 * jnp.sum(qn * mask, axis=1, keepdims=True)
               + jnp.sum(ws, axis=1, keepdims=True))
        hc = hc + mask * (num / jnp.maximum(jnp.abs(den), jnp.exp(-m_row)))
        m_new = m_row[L - 1:L, :]
        b_last = bcc[L - 1:L, :]
        wl_all = wl_all + mask * jnp.exp(b_last - bcc + i_col - m_new)
        w0l_all = w0l_all + mask * jnp.exp(b_last + m0 - m_new)
        m_scr[h:h + 1, :] = jnp.broadcast_to(m_new, (1, LANES))

    kw = k * wl_all
    c_new = w0l_all * C + ones.astype(F32) * _dot_tn(v, kw)
    n_new = w0l_all * n + jnp.sum(kw, axis=0, keepdims=True)
    c_scr[...] = c_new
    n_scr[0:1, :] = n_new

    ms = _head_sum(hc * hc, ones) * (1.0 / HEAD_DIM)
    y = hc * lax.rsqrt(ms + EPS) * norm_ref[...] * _sigmoid(og_ref[...])
    y_ref[...] = y.astype(y_ref.dtype)

    @pl.when(c == nc - 1)
    def _():
        c_out[0] = c_new
        n_out[0] = jnp.broadcast_to(n_new, (8, W_GROUP))
        m_out[0] = m_scr[...]


def _proj_spec(L, nc, blk, width=W_GROUP):
    return pl.BlockSpec((L, width), lambda b, c: (b * nc + c, blk))


def _mlstm(proj, gt, p, n_seq, t_len):
    L = CHUNK_ML
    nc = t_len // L
    n = proj.shape[0]
    const = lambda shape: pl.BlockSpec(shape, lambda b, c: (0,) * len(shape))
    return pl.pallas_call(
        functools.partial(_mlstm_kernel, L=L, nc=nc),
        grid=(n_seq, nc),
        in_specs=[
            _proj_spec(L, nc, BLK_MQ), _proj_spec(L, nc, BLK_MK), _proj_spec(L, nc, BLK_MV),
            _proj_spec(L, nc, BLK_MO), _proj_spec(L, nc, BLK_SMALL, SMALL_W),
            pl.BlockSpec((1, 8, L), lambda b, c: (b, 0, c)),
            const((1, SMALL_W)), const((8, L)), const((1, W_GROUP)),
        ],
        out_specs=[
            pl.BlockSpec((L, W_GROUP), lambda b, c: (b * nc + c, 0)),
            pl.BlockSpec((1, W_GROUP, W_GROUP), lambda b, c: (b, 0, 0)),
            pl.BlockSpec((1, 8, W_GROUP), lambda b, c: (b, 0, 0)),
            pl.BlockSpec((1, 8, LANES), lambda b, c: (b, 0, 0)),
        ],
        out_shape=[
            jax.ShapeDtypeStruct((n, W_GROUP), BF16),
            jax.ShapeDtypeStruct((n_seq, W_GROUP, W_GROUP), F32),
            jax.ShapeDtypeStruct((n_seq, 8, W_GROUP), F32),
            jax.ShapeDtypeStruct((n_seq, 8, LANES), F32),
        ],
        scratch_shapes=[
            pltpu.VMEM((W_GROUP, W_GROUP), F32),
            pltpu.VMEM((8, W_GROUP), F32),
            pltpu.VMEM((8, LANES), F32),
        ],
        compiler_params=_cparams("arbitrary", "arbitrary"),
        name="mlstm_chunk",
    )(proj, proj, proj, proj, proj, gt, p["ml_brow"], p["ml_bcol"], p["ml_norm"])


def _gla_kernel(q_ref, k_ref, v_ref, gg_ref, sm_ref, wa_ref, ba_ref, norm_ref,
                y_ref, s_out, s_scr, *, L, nc):
    c = pl.program_id(1)

    @pl.when(c == 0)
    def _():
        s_scr[...] = jnp.zeros_like(s_scr)

    q = q_ref[...] * (HEAD_DIM ** -0.5)
    k = k_ref[...]
    v = v_ref[...]
    la = _log_sigmoid(_dot(sm_ref[...], wa_ref[...]) + ba_ref[...]) * (1.0 / GLA_GATE_TEMP)
    bc = _dot_exact_rhs(_tril_ones(L), la)
    qs = q * jnp.exp(bc)
    kh = k * jnp.exp(jnp.minimum(-bc, EXP_CLAMP))
    ST = s_scr[...]
    o = _dot_nt(qs, ST)
    causal = _iota((L, L), 0) >= _iota((L, L), 1)
    for h in range(N_HEADS):
        mask = _head_lane_mask(h)
        att = jnp.where(causal, _dot_nt(qs * mask, kh), 0.0)
        o = o + _dot(att, v * mask)
    b_last = bc[L - 1:L, :]
    kbar = k * jnp.exp(b_last - bc)
    ones = _head_ones()
    s_new = ST * jnp.exp(b_last) + ones.astype(F32) * _dot_tn(v, kbar)
    s_scr[...] = s_new

    ms = _head_sum(o * o, ones) * (1.0 / HEAD_DIM)
    gg = gg_ref[...]
    y = o * lax.rsqrt(ms + EPS) * norm_ref[...] * (gg * _sigmoid(gg))
    y_ref[...] = y.astype(y_ref.dtype)

    @pl.when(c == nc - 1)
    def _():
        s_out[0] = s_new


def _gla(proj, p, n_seq, t_len):
    L = CHUNK_ML
    nc = t_len // L
    n = proj.shape[0]
    const = lambda shape: pl.BlockSpec(shape, lambda b, c: (0,) * len(shape))
    return pl.pallas_call(
        functools.partial(_gla_kernel, L=L, nc=nc),
        grid=(n_seq, nc),
        in_specs=[
            _proj_spec(L, nc, BLK_GQ), _proj_spec(L, nc, BLK_GK), _proj_spec(L, nc, BLK_GV),
            _proj_spec(L, nc, BLK_GG), _proj_spec(L, nc, BLK_SMALL, SMALL_W),
            const((SMALL_W, W_GROUP)), const((1, W_GROUP)), const((1, W_GROUP)),
        ],
        out_specs=[
            pl.BlockSpec((L, W_GROUP), lambda b, c: (b * nc + c, 0)),
            pl.BlockSpec((1, W_GROUP, W_GROUP), lambda b, c: (b, 0, 0)),
        ],
        out_shape=[
            jax.ShapeDtypeStruct((n, W_GROUP), BF16),
            jax.ShapeDtypeStruct((n_seq, W_GROUP, W_GROUP), F32),
        ],
        scratch_shapes=[pltpu.VMEM((W_GROUP, W_GROUP), F32)],
        compiler_params=_cparams("arbitrary", "arbitrary"),
        name="gla_chunk",
    )(proj, proj, proj, proj, proj, p["gla_wa"], p["gla_ba"], p["gla_norm"])


def _rwkv_vectors(rc, prev, p_mu, p_w0, p_a0, p_kk, p_ka, p_rk, w2, a2, g2, ones):
    xm = rc + p_mu * (prev - rc)
    rr = xm[:, 0:W_GROUP]
    rk = xm[:, W_GROUP:2 * W_GROUP]
    rv = xm[:, 2 * W_GROUP:3 * W_GROUP]
    tail = xm[:, 3 * W_GROUP:RWKV_COLS]
    lw = -RWKV_DECAY_SCALE * _sigmoid(p_w0 + _dot(jnp.tanh(tail), w2))
    a = _sigmoid(p_a0 + _dot(tail, a2))
    g = _dot(_sigmoid(tail), g2)
    kk = rk * p_kk
    kk = kk * lax.rsqrt(jnp.maximum(_head_sum(kk * kk, ones), 1e-24))
    kt = rk * (1.0 + (a - 1.0) * p_ka)
    bonus = _head_sum(rr * kt * p_rk, ones) * rv
    return rr, lw, kt, rv, kk, a, g, bonus


def _head_layernorm(o, g, ones):
    mu = _head_sum(o, ones) * (1.0 / HEAD_DIM)
    oc = o - mu
    var = _head_sum(oc * oc, ones) * (1.0 / HEAD_DIM)
    return oc * lax.rsqrt(var + GN_EPS) * g


def _stack_heads(x):
    return jnp.concatenate([x * _head_lane_mask(h) for h in range(N_HEADS)], axis=0)


def _unit_lower_inverse(a_strict):
    n = a_strict.shape[0]
    r = _iota((n, n), 0)
    c = _iota((n, n), 1)
    eye = jnp.where(r == c, 1.0, 0.0).astype(F32)
    x = -a_strict
    x16 = jnp.where((r >> 4) == (c >> 4), x, 0.0)
    x2 = _dot(x16, x16)
    x4 = _dot(x2, x2)
    x8 = _dot(x4, x4)
    t = eye + x16
    t = t + _dot(t, x2)
    t = t + _dot(t, x4)
    t = t + _dot(t, x8)
    off32 = jnp.where(((r >> 5) == (c >> 5)) & ((r >> 4) > (c >> 4)), a_strict, 0.0)
    t = t - _dot(t, _dot(off32, t))
    off64 = jnp.where(((r >> 6) == (c >> 6)) & ((r >> 5) > (c >> 5)), a_strict, 0.0)
    t = t - _dot(t, _dot(off64, t))
    return t


def _rwkv_kernel(rc_ref, mu_ref, w0_ref, a0_ref, kk_ref, ka_ref, rk_ref, w2_ref, a2_ref, g2_ref,
                 norm_ref, y_ref, s_out, shift_out, s_scr, prev_scr, *, L, nc):
    c = pl.program_id(1)

    @pl.when(c == 0)
    def _():
        s_scr[...] = jnp.zeros_like(s_scr)
        prev_scr[...] = jnp.zeros_like(prev_scr)

    ones = _head_ones()
    rc = rc_ref[...]
    rolled = pltpu.roll(rc, 1, 0)
    prev = jnp.where(_iota((L, RWKV_COLS), 0) == 0, prev_scr[0:1, :], rolled)
    prev_scr[0:1, :] = rc[L - 1:L, :]
    r, lw, kx, v, kk, a, g, bonus = _rwkv_vectors(
        rc, prev, mu_ref[...], w0_ref[...], a0_ref[...], kk_ref[...], ka_ref[...], rk_ref[...],
        w2_ref[...], a2_ref[...], g2_ref[...], ones)

    lc = _dot_exact_rhs(_tril_ones(L), lw)
    akk = a * kk
    e_neg = jnp.exp(-lc)
    l_last = lc[L - 1:L, :]
    e_end = jnp.exp(l_last - lc)
    Ks = _stack_heads(kk * jnp.exp(lc - lw))
    As = _stack_heads(akk * e_neg)
    Khs = _stack_heads(kx * e_neg)
    Rs = _stack_heads(r * jnp.exp(lc))
    Vs = _stack_heads(v)
    Kbs = _stack_heads(kx * e_end)
    Abs = _stack_heads(akk * e_end)

    n4 = N_HEADS * L
    rr_ = _iota((n4, n4), 0)
    cc_ = _iota((n4, n4), 1)
    strict = rr_ > cc_
    incl = rr_ >= cc_
    a_ua = jnp.where(strict, _dot_nt(Ks, As), 0.0)
    a_vk = jnp.where(strict, _dot_nt(Ks, Khs), 0.0)
    b_rk = jnp.where(incl, _dot_nt(Rs, Khs), 0.0)
    b_ra = jnp.where(incl, _dot_nt(Rs, As), 0.0)
    T = _unit_lower_inverse(a_ua)

    S = s_scr[...]
    Us = _dot(T, _dot_nt(Ks, S) + _dot(a_vk, Vs))
    Os = _dot_nt(Rs, S) + _dot(b_rk, Vs) - _dot(b_ra, Us)
    o = Os[0:L] + Os[L:2 * L] + Os[2 * L:3 * L] + Os[3 * L:4 * L]
    s_new = S * jnp.exp(l_last) + _dot_tn(Vs, Kbs) - _dot_tn(Us, Abs)
    s_scr[...] = s_new

    y = (_head_layernorm(o, norm_ref[...], ones) + bonus) * g
    y_ref[...] = y.astype(y_ref.dtype)

    @pl.when(c == nc - 1)
    def _():
        s_out[0] = s_new
        shift_out[0] = jnp.broadcast_to(rc[L - 1:L, :], (8, RWKV_COLS))


def _rwkv(proj, p, n_seq, t_len):
    L = CHUNK_RW
    nc = t_len // L
    n = proj.shape[0]
    const = lambda shape: pl.BlockSpec(shape, lambda b, c: (0,) * len(shape))
    row = const((1, W_GROUP))
    pad = const((LANES, W_GROUP))
    return pl.pallas_call(
        functools.partial(_rwkv_kernel, L=L, nc=nc),
        grid=(n_seq, nc),
        in_specs=[
            pl.BlockSpec((L, RWKV_COLS), lambda b, c: (b * nc + c, 0)),
            const((1, RWKV_COLS)), row, row, row, row, row, pad, pad, pad, row,
        ],
        out_specs=[
            pl.BlockSpec((L, W_GROUP), lambda b, c: (b * nc + c, 0)),
            pl.BlockSpec((1, W_GROUP, W_GROUP), lambda b, c: (b, 0, 0)),
            pl.BlockSpec((1, 8, RWKV_COLS), lambda b, c: (b, 0, 0)),
        ],
        out_shape=[
            jax.ShapeDtypeStruct((n, W_GROUP), BF16),
            jax.ShapeDtypeStruct((n_seq, W_GROUP, W_GROUP), F32),
            jax.ShapeDtypeStruct((n_seq, 8, RWKV_COLS), F32),
        ],
        scratch_shapes=[
            pltpu.VMEM((W_GROUP, W_GROUP), F32),
            pltpu.VMEM((8, RWKV_COLS), F32),
        ],
        compiler_params=_cparams("arbitrary", "arbitrary"),
        name="rwkv_chunk",
    )(proj, p["rw_mu"], p["rw_w0"], p["rw_a0"], p["rw_kk"], p["rw_ka"], p["rw_rk"],
      p["rw_w2"], p["rw_a2"], p["rw_g2"], p["rw_norm"])


FF_CHUNK = 256


def _mix_residual(x_ref, ys, wout_ref):
    acc = x_ref[...]
    for j, y in enumerate(ys):
        acc = acc + jnp.dot(y.astype(BF16), wout_ref[j * W_GROUP:(j + 1) * W_GROUP, :],
                            preferred_element_type=F32)
    return acc


def _ffn_body(x1, nrm_ref, wup_ref, cw_ref, cb_ref, wdn_ref, prev_rows, d_ff, act_scr):
    h2 = _rmsnorm_rows(x1, nrm_ref[...]).astype(BF16)
    for j in range(d_ff // FF_CHUNK):
        lo, hi = j * FF_CHUNK, (j + 1) * FF_CHUNK
        ug = jnp.dot(h2, wup_ref[:, lo:hi], preferred_element_type=F32)
        uv = jnp.dot(h2, wup_ref[:, d_ff + lo:d_ff + hi], preferred_element_type=F32)
        p2, p1 = prev_rows(j, ug)
        conv = (cb_ref[:, lo:hi] + cw_ref[0:1, lo:hi] * p2 + cw_ref[1:2, lo:hi] * p1
                + cw_ref[2:3, lo:hi] * ug)
        act_scr[:, lo:hi] = (_gelu(conv) * uv).astype(BF16)
    return x1 + jnp.dot(act_scr[...], wdn_ref[...], preferred_element_type=F32)


def _ffn_seq_kernel(x_ref, y0_ref, y1_ref, y2_ref, y3_ref, wout_ref, nrm_ref, wup_ref, cw_ref, cb_ref,
                    wdn_ref, nf_ref, o_ref, buf_out, carry_scr, act_scr, *, tm, d_ff, final):
    t = pl.program_id(1)

    @pl.when(t == 0)
    def _():
        carry_scr[...] = jnp.zeros_like(carry_scr)

    x1 = _mix_residual(x_ref, (y0_ref[...], y1_ref[...], y2_ref[...], y3_ref[...]), wout_ref)
    row = _iota((tm, FF_CHUNK), 0)

    def prev_rows(j, ug):
        lo, hi = j * FF_CHUNK, (j + 1) * FF_CHUNK
        c0 = carry_scr[0:1, lo:hi]
        c1 = carry_scr[1:2, lo:hi]
        p1 = jnp.where(row == 0, c1, pltpu.roll(ug, 1, 0))
        p2 = jnp.where(row == 0, c0, jnp.where(row == 1, c1, pltpu.roll(ug, 2, 0)))
        carry_scr[0:2, lo:hi] = ug[tm - 2:tm, :]
        return p2, p1

    out = _ffn_body(x1, nrm_ref, wup_ref, cw_ref, cb_ref, wdn_ref, prev_rows, d_ff, act_scr)
    if final:
        out = _rmsnorm_rows(out, nf_ref[...])
    o_ref[...] = out
    buf_out[0] = carry_scr[...]


def _ffn_step_kernel(x_ref, y0_ref, y1_ref, y2_ref, y3_ref, wout_ref, nrm_ref, wup_ref, cw_ref, cb_ref,
                     wdn_ref, nf_ref, p2_ref, p1_ref, o_ref, ug_out, act_scr, *, d_ff, final):
    x1 = _mix_residual(x_ref, (y0_ref[...], y1_ref[...], y2_ref[...], y3_ref[...]), wout_ref)

    def prev_rows(j, ug):
        lo, hi = j * FF_CHUNK, (j + 1) * FF_CHUNK
        ug_out[:, lo:hi] = ug
        return p2_ref[:, lo:hi], p1_ref[:, lo:hi]

    out = _ffn_body(x1, nrm_ref, wup_ref, cw_ref, cb_ref, wdn_ref, prev_rows, d_ff, act_scr)
    if final:
        out = _rmsnorm_rows(out, nf_ref[...])
    o_ref[...] = out


def _ffn_seq(x2d, ys, p, norm_final, n_seq, t_len, tm, final):
    n, d = x2d.shape
    d_ff = p["ffn_wdn"].shape[0]
    nt = t_len // tm
    rows = lambda w: pl.BlockSpec((tm, w), lambda b, t: (b * nt + t, 0))
    const = lambda shape: pl.BlockSpec(shape, lambda b, t: (0,) * len(shape))
    return pl.pallas_call(
        functools.partial(_ffn_seq_kernel, tm=tm, d_ff=d_ff, final=final),
        grid=(n_seq, nt),
        in_specs=[rows(d)] + [rows(W_GROUP)] * 4 + [
            const((d, d)), const((1, d)), const((d, 2 * d_ff)), const((CONV_W, d_ff)),
            const((1, d_ff)), const((d_ff, d)), const((1, d))],
        out_specs=[rows(d), pl.BlockSpec((1, CONV_W - 1, d_ff), lambda b, t: (b, 0, 0))],
        out_shape=[jax.ShapeDtypeStruct((n, d), F32),
                   jax.ShapeDtypeStruct((n_seq, CONV_W - 1, d_ff), F32)],
        scratch_shapes=[pltpu.VMEM((CONV_W - 1, d_ff), F32), pltpu.VMEM((tm, d_ff), BF16)],
        compiler_params=_cparams("arbitrary", "arbitrary"),
        name="wout_ffn_seq",
    )(x2d, *ys, p["w_out"], p["norm_ffn"], p["ffn_wup"], p["ffn_cw"], p["ffn_cb"], p["ffn_wdn"],
      norm_final)


def _ffn_step(x2d, ys, p, norm_final, prev2, prev1, final):
    n, d = x2d.shape
    d_ff = p["ffn_wdn"].shape[0]
    full = lambda shape: pl.BlockSpec(shape, lambda i: (0,) * len(shape))
    return pl.pallas_call(
        functools.partial(_ffn_step_kernel, d_ff=d_ff, final=final),
        grid=(1,),
        in_specs=[full((n, d))] + [full((n, W_GROUP))] * 4 + [
            full((d, d)), full((1, d)), full((d, 2 * d_ff)), full((CONV_W, d_ff)),
            full((1, d_ff)), full((d_ff, d)), full((1, d)), full((n, d_ff)), full((n, d_ff))],
        out_specs=[full((n, d)), full((n, d_ff))],
        out_shape=[jax.ShapeDtypeStruct((n, d), F32), jax.ShapeDtypeStruct((n, d_ff), F32)],
        scratch_shapes=[pltpu.VMEM((n, d_ff), BF16)],
        compiler_params=_cparams("arbitrary"),
        name="wout_ffn_step",
    )(x2d, *ys, p["w_out"], p["norm_ffn"], p["ffn_wup"], p["ffn_cw"], p["ffn_cb"], p["ffn_wdn"],
      norm_final, prev2, prev1)


def _step_vec_kernel(proj_ref, shift_ref, m_ref, brow_ref, wa_ref, ba_ref, mu_ref, w0_ref, a0_ref,
                     kk_ref, ka_ref, rk_ref, w2_ref, a2_ref, g2_ref,
                     iw_out, fw_out, m_out, alpha_out, rr_out, kt_out, kk_out, akk_out, w_out, g_out,
                     bonus_out, rv_out):
    ones = _head_ones()
    sm = proj_ref[:, OFF_SMALL:OFF_SMALL + SMALL_W]
    pre = sm + brow_ref[...]
    logf = pltpu.roll(_log_sigmoid(pre), LANES - N_HEADS, 1)
    m_old = m_ref[...]
    m_new = jnp.maximum(logf + m_old, pre)
    iw_out[...] = jnp.exp(pre - m_new)
    fw_out[...] = jnp.exp(logf + m_old - m_new)
    m_out[...] = m_new
    la = _log_sigmoid(_dot(sm, wa_ref[...]) + ba_ref[...]) * (1.0 / GLA_GATE_TEMP)
    alpha_out[...] = jnp.exp(la)
    rc = proj_ref[:, 0:RWKV_COLS]
    rr, lw, kt, rv, kk, a, g, bonus = _rwkv_vectors(
        rc, shift_ref[...], mu_ref[...], w0_ref[...], a0_ref[...], kk_ref[...], ka_ref[...],
        rk_ref[...], w2_ref[...], a2_ref[...], g2_ref[...], ones)
    rr_out[...] = rr
    kt_out[...] = kt
    kk_out[...] = kk
    akk_out[...] = a * kk
    w_out[...] = jnp.exp(lw)
    g_out[...] = g
    bonus_out[...] = bonus
    rv_out[...] = rv


def _step_vectors(proj, shift, m_pad, p):
    n = proj.shape[0]
    full = lambda a: pl.BlockSpec(a.shape, lambda i: (0,) * a.ndim)
    args = (proj, shift, m_pad, p["ml_brow"], p["gla_wa"], p["gla_ba"], p["rw_mu"], p["rw_w0"],
            p["rw_a0"], p["rw_kk"], p["rw_ka"], p["rw_rk"], p["rw_w2"], p["rw_a2"], p["rw_g2"])
    small = jax.ShapeDtypeStruct((n, SMALL_W), F32)
    wide = jax.ShapeDtypeStruct((n, W_GROUP), F32)
    outs = [small, small, small] + [wide] * 9
    return pl.pallas_call(
        _step_vec_kernel,
        grid=(1,),
        in_specs=[full(a) for a in args],
        out_specs=[pl.BlockSpec(o.shape, lambda i: (0, 0)) for o in outs],
        out_shape=outs,
        compiler_params=_cparams("arbitrary"),
        name="step_vectors",
    )(*args)


PAIRS_PER_BLOCK = 64


def _mlstm_step_kernel(c_ref, n_ref, q_ref, k_ref, v_ref, iw_ref, fw_ref, m_ref, c_out, n_out, h_out):
    k = k_ref[...] * (HEAD_DIM ** -0.5)
    q = q_ref[...]
    iw = iw_ref[...]
    fw = fw_ref[...]
    c_new = fw * c_ref[...] + iw * (v_ref[...] * k)
    n_new = fw * n_ref[...] + iw * k
    num = jnp.sum(c_new * q, axis=2, keepdims=True)
    den = jnp.sum(n_new * q, axis=2, keepdims=True)
    c_out[...] = c_new
    n_out[...] = n_new
    h_out[...] = num / jnp.maximum(jnp.abs(den), jnp.exp(-m_ref[...]))


def _gla_step_kernel(s_ref, q_ref, k_ref, v_ref, a_ref, s_out, o_out):
    s_new = a_ref[...] * s_ref[...] + k_ref[...] * v_ref[...]
    s_out[...] = s_new
    o_out[...] = jnp.sum((q_ref[...] * (HEAD_DIM ** -0.5)) * s_new, axis=1, keepdims=True)


def _rwkv_step_kernel(s_ref, kk_ref, akk_ref, w_ref, kt_ref, rr_ref, v_ref, s_out, o_out):
    s = s_ref[...]
    sk = jnp.sum(s * kk_ref[...], axis=2, keepdims=True)
    s_new = s * w_ref[...] - sk * akk_ref[...] + v_ref[...] * kt_ref[...]
    s_out[...] = s_new
    o_out[...] = jnp.sum(s_new * rr_ref[...], axis=2, keepdims=True)


def _pairs_call(kernel, name, args, outs):
    npairs = args[0].shape[0]
    P = min(PAIRS_PER_BLOCK, npairs)
    spec = lambda a: pl.BlockSpec((P,) + tuple(a.shape[1:]), lambda i: (i, 0, 0))
    return pl.pallas_call(
        kernel,
        grid=(npairs // P,),
        in_specs=[spec(a) for a in args],
        out_specs=[spec(o) for o in outs],
        out_shape=outs,
        compiler_params=_cparams("arbitrary"),
        name=name,
    )(*args)


def _step_post_kernel(hml_ref, mo_ref, ogla_ref, gg_ref, orw_ref, g_ref, bonus_ref,
                      mln_ref, gln_ref, rwn_ref, yml_out, ygla_out, yrw_out):
    ones = _head_ones()
    hml = hml_ref[...]
    ms = _head_sum(hml * hml, ones) * (1.0 / HEAD_DIM)
    yml_out[...] = (hml * lax.rsqrt(ms + EPS) * mln_ref[...] * _sigmoid(mo_ref[...])).astype(BF16)
    og = ogla_ref[...]
    ms = _head_sum(og * og, ones) * (1.0 / HEAD_DIM)
    gg = gg_ref[...]
    ygla_out[...] = (og * lax.rsqrt(ms + EPS) * gln_ref[...] * (gg * _sigmoid(gg))).astype(BF16)
    yrw_out[...] = ((_head_layernorm(orw_ref[...], rwn_ref[...], ones) + bonus_ref[...])
                    * g_ref[...]).astype(BF16)


def _step_post(hml, mo, ogla, gg, orw, g, bonus, p):
    n = hml.shape[0]
    args = (hml, mo, ogla, gg, orw, g, bonus, p["ml_norm"], p["gla_norm"], p["rw_norm"])
    full = lambda a: pl.BlockSpec(a.shape, lambda i: (0,) * a.ndim)
    out = jax.ShapeDtypeStruct((n, W_GROUP), BF16)
    return pl.pallas_call(
        _step_post_kernel,
        grid=(1,),
        in_specs=[full(a) for a in args],
        out_specs=[pl.BlockSpec((n, W_GROUP), lambda i: (0, 0))] * 3,
        out_shape=[out] * 3,
        compiler_params=_cparams("arbitrary"),
        name="step_post",
    )(*args)


def _layer_params(l, prm):
    p = {}
    w_in = prm["w_in"][l]
    d = w_in.shape[0]
    sizes = (W_GROUP, W_GROUP, W_GROUP, W_GROUP, N_HEADS, N_HEADS, W_GROUP, W_GROUP, W_GROUP, W_GROUP,
             GLA_RANK, W_GROUP, RWKV_COLS)
    offs = [0]
    for s in sizes:
        offs.append(offs[-1] + s)
    col = lambda i: w_in[:, offs[i]:offs[i + 1]]
    (u, mq, mk, mv, mi, mf, mo, gq, gk, gv, ga, gg, rcols) = [col(i) for i in range(13)]
    small = jnp.concatenate([ga, mi, mf, jnp.zeros((d, SMALL_W - GLA_RANK - 2 * N_HEADS), F32)], axis=1)
    p["w_in"] = jnp.concatenate([rcols, small, u, mq, mk, mv, mo, gq, gk, gv, gg], axis=1).astype(BF16)
    p["w_gate_t"] = jnp.concatenate([mi, mf], axis=1).T.astype(BF16)
    p["norm_mix"] = prm["norm_mix"][l][None, :]

    lam = lax.complex(prm["s5_lam_re"][l], prm["s5_lam_im"][l])
    dt = jnp.exp(prm["s5_log_dt"][l])[:, None]
    lam_bar = jnp.exp(lam * dt)
    b_bar = ((lam_bar - 1.0) / lam)[..., None] * lax.complex(prm["s5_b_re"][l], prm["s5_b_im"][l])
    eye = jnp.eye(S5_GROUPS, dtype=F32)
    bm = lambda b: jnp.einsum("gph,gk->ghkp", b, eye).reshape(W_GROUP, S5_WIDTH)
    p["bmat"] = jnp.concatenate([bm(b_bar.real), bm(b_bar.imag)], axis=1).astype(BF16)
    cm = lambda c: jnp.einsum("ghp,gk->kpgh", c, eye).reshape(S5_WIDTH, W_GROUP)
    p["cmat"] = jnp.concatenate([cm(prm["s5_c_re"][l]), -cm(prm["s5_c_im"][l])], axis=0).astype(BF16)
    p["lre"] = lam_bar.real.reshape(1, S5_WIDTH)
    p["lim"] = lam_bar.imag.reshape(1, S5_WIDTH)
    p["d"] = prm["s5_d"][l][None, :]
    p["wglu"] = prm["s5_w_glu"][l].astype(BF16)

    bias = prm["ml_gate_bias"][l]
    p["ml_brow"] = jnp.zeros((1, SMALL_W), F32).at[0, LANE_MI:LANE_MI + 2 * N_HEADS].set(bias)
    p["ml_bcol"] = jnp.broadcast_to(bias[:, None], (2 * N_HEADS, CHUNK_ML))
    p["ml_norm"] = prm["ml_norm"][l][None, :]

    p["gla_wa"] = jnp.zeros((SMALL_W, W_GROUP), F32).at[0:GLA_RANK].set(prm["gla_w_alpha"][l]).astype(BF16)
    p["gla_ba"] = prm["gla_b_alpha"][l][None, :]
    p["gla_norm"] = prm["gla_norm"][l][None, :]

    p["rw_mu"] = prm["rw_mu"][l][None, :]
    p["rw_w0"] = prm["rw_w0"][l][None, :]
    p["rw_a0"] = prm["rw_a0"][l][None, :]
    p["rw_kk"] = prm["rw_k_k"][l][None, :]
    p["rw_ka"] = prm["rw_k_a"][l][None, :]
    p["rw_rk"] = prm["rw_r_k"][l].reshape(1, W_GROUP)
    z = jnp.zeros((LANES, W_GROUP), F32)
    p["rw_w2"] = z.at[0:RWKV_W_RANK].set(prm["rw_w2"][l]).astype(BF16)
    p["rw_a2"] = z.at[RWKV_W_RANK:RWKV_W_RANK + RWKV_A_RANK].set(prm["rw_a2"][l]).astype(BF16)
    p["rw_g2"] = z.at[RWKV_W_RANK + RWKV_A_RANK:].set(prm["rw_g2"][l]).astype(BF16)
    p["rw_norm"] = prm["rw_norm"][l][None, :]

    p["w_out"] = prm["w_out"][l].astype(BF16)
    p["norm_ffn"] = prm["norm_ffn"][l][None, :]
    p["ffn_wup"] = prm["ffn_w_up"][l].astype(BF16)
    p["ffn_cw"] = prm["ffn_conv_w"][l]
    p["ffn_cb"] = prm["ffn_conv_b"][l][None, :]
    p["ffn_wdn"] = prm["ffn_w_down"][l].astype(BF16)
    return p


def _diag_blocks(m):
    b = m.shape[0]
    m = m.reshape(b, N_HEADS, HEAD_DIM, N_HEADS, HEAD_DIM)
    return jnp.stack([m[:, h, :, h, :] for h in range(N_HEADS)], axis=1)


def _prompt_layer(x2d, p, norm_final, n_seq, t_len, final):
    proj, gt = _inproj(x2d, p["norm_mix"], p["w_in"], p["w_gate_t"], n_seq, t_len, tm=512)

    u_tb = proj[:, OFF_MAIN:OFF_MAIN + W_GROUP].reshape(n_seq, t_len, W_GROUP).transpose(1, 0, 2)
    zeros = jnp.zeros((n_seq, S5_WIDTH), F32)
    y_s5_tb, s5_re, s5_im = _s5(u_tb.reshape(t_len * n_seq, W_GROUP), zeros, zeros, p, n_seq, t_len, tt=128)
    y_s5 = y_s5_tb.reshape(t_len, n_seq, W_GROUP).transpose(1, 0, 2).reshape(n_seq * t_len, W_GROUP)

    y_ml, ml_c, ml_n, ml_m = _mlstm(proj, gt, p, n_seq, t_len)
    y_gla, gla_st = _gla(proj, p, n_seq, t_len)
    y_rw, rw_s, rw_shift = _rwkv(proj, p, n_seq, t_len)

    x_new, ffn_buf = _ffn_seq(x2d, (y_s5, y_ml, y_gla, y_rw), p, norm_final, n_seq, t_len, tm=256,
                              final=final)
    states = (
        s5_re.reshape(n_seq, S5_GROUPS, S5_STATE),
        s5_im.reshape(n_seq, S5_GROUPS, S5_STATE),
        _diag_blocks(ml_c),
        ml_n[:, 0, :].reshape(n_seq, N_HEADS, HEAD_DIM),
        ml_m[:, 0:N_HEADS, 0],
        jnp.swapaxes(_diag_blocks(gla_st), -1, -2),
        _diag_blocks(rw_s),
        rw_shift[:, 0, :],
        ffn_buf,
    )
    return x_new, states


def _sample_layer(x2d, st, p, norm_final, final):
    (s5_re, s5_im, ml_c, ml_n, ml_m, gla_s, rw_s, rw_shift, ffn_buf) = st
    n = x2d.shape[0]
    npairs = n * N_HEADS
    proj, _ = _inproj(x2d, p["norm_mix"], p["w_in"], p["w_gate_t"], 1, n, tm=n)

    y_s5, s5_re_new, s5_im_new = _s5(proj[:, OFF_MAIN:OFF_MAIN + W_GROUP],
                                     s5_re.reshape(n, S5_WIDTH), s5_im.reshape(n, S5_WIDTH),
                                     p, n, 1, tt=1)

    m_pad = jnp.zeros((n, SMALL_W), F32).at[:, LANE_MI:LANE_MI + N_HEADS].set(ml_m)
    (iw, fw, m_new, alpha, rr, kt, kk, akk, wdec, g, bonus, rv) = _step_vectors(proj, rw_shift, m_pad, p)

    main = lambda blk: proj[:, blk * W_GROUP:(blk + 1) * W_GROUP]
    as_row = lambda a: a.reshape(npairs, 1, HEAD_DIM)
    as_col = lambda a: a.reshape(npairs, HEAD_DIM, 1)
    as_scl = lambda a: a[:, LANE_MI:LANE_MI + N_HEADS].reshape(npairs, 1, 1)
    sds = lambda *s: jax.ShapeDtypeStruct(s, F32)
    sq = (npairs, HEAD_DIM, HEAD_DIM)

    ml_c_new, ml_n_new, h_ml = _pairs_call(
        _mlstm_step_kernel, "mlstm_step",
        (ml_c.reshape(sq), as_row(ml_n), as_row(main(BLK_MQ)), as_row(main(BLK_MK)), as_col(main(BLK_MV)),
         as_scl(iw), as_scl(fw), as_scl(m_new)),
        [sds(*sq), sds(npairs, 1, HEAD_DIM), sds(npairs, HEAD_DIM, 1)])
    gla_s_new, o_gla = _pairs_call(
        _gla_step_kernel, "gla_step",
        (gla_s.reshape(sq), as_col(main(BLK_GQ)), as_col(main(BLK_GK)), as_row(main(BLK_GV)), as_col(alpha)),
        [sds(*sq), sds(npairs, 1, HEAD_DIM)])
    rw_s_new, o_rw = _pairs_call(
        _rwkv_step_kernel, "rwkv_step",
        (rw_s.reshape(sq), as_row(kk), as_row(akk), as_row(wdec), as_row(kt), as_row(rr), as_col(rv)),
        [sds(*sq), sds(npairs, HEAD_DIM, 1)])

    y_ml, y_gla, y_rw = _step_post(h_ml.reshape(n, W_GROUP), main(BLK_MO), o_gla.reshape(n, W_GROUP),
                                   main(BLK_GG), o_rw.reshape(n, W_GROUP), g, bonus, p)
    x_new, ug = _ffn_step(x2d, (y_s5, y_ml, y_gla, y_rw), p, norm_final, ffn_buf[:, 0, :], ffn_buf[:, 1, :],
                          final)
    states = (
        s5_re_new.reshape(n, S5_GROUPS, S5_STATE),
        s5_im_new.reshape(n, S5_GROUPS, S5_STATE),
        ml_c_new.reshape(n, N_HEADS, HEAD_DIM, HEAD_DIM),
        ml_n_new.reshape(n, N_HEADS, HEAD_DIM),
        m_new[:, LANE_MI:LANE_MI + N_HEADS],
        gla_s_new.reshape(n, N_HEADS, HEAD_DIM, HEAD_DIM),
        rw_s_new.reshape(n, N_HEADS, HEAD_DIM, HEAD_DIM),
        proj[:, 0:RWKV_COLS],
        jnp.stack([ffn_buf[:, 1, :], ug], axis=1),
    )
    return x_new, states


def kernel(x_prompt, x_sample, state_s5_re, state_s5_im, state_mlstm_C, state_mlstm_n, state_mlstm_m, state_gla_S, state_rwkv_S, state_rwkv_shift, state_ffn_conv, norm_mix, w_in, s5_lam_re, s5_lam_im, s5_log_dt, s5_b_re, s5_b_im, s5_c_re, s5_c_im, s5_d, s5_w_glu, ml_gate_bias, ml_norm, gla_w_alpha, gla_b_alpha, gla_norm, rw_mu, rw_w0, rw_w2, rw_a0, rw_a2, rw_g2, rw_k_k, rw_k_a, rw_r_k, rw_norm, w_out, norm_ffn, ffn_w_up, ffn_conv_w, ffn_conv_b, ffn_w_down, norm_final):
    prm = dict(norm_mix=norm_mix, w_in=w_in, s5_lam_re=s5_lam_re, s5_lam_im=s5_lam_im, s5_log_dt=s5_log_dt,
               s5_b_re=s5_b_re, s5_b_im=s5_b_im, s5_c_re=s5_c_re, s5_c_im=s5_c_im, s5_d=s5_d,
               s5_w_glu=s5_w_glu, ml_gate_bias=ml_gate_bias, ml_norm=ml_norm, gla_w_alpha=gla_w_alpha,
               gla_b_alpha=gla_b_alpha, gla_norm=gla_norm, rw_mu=rw_mu, rw_w0=rw_w0, rw_w2=rw_w2,
               rw_a0=rw_a0, rw_a2=rw_a2, rw_g2=rw_g2, rw_k_k=rw_k_k, rw_k_a=rw_k_a, rw_r_k=rw_r_k,
               rw_norm=rw_norm, w_out=w_out, norm_ffn=norm_ffn, ffn_w_up=ffn_w_up, ffn_conv_w=ffn_conv_w,
               ffn_conv_b=ffn_conv_b, ffn_w_down=ffn_w_down)
    depth = w_in.shape[0]
    n_seq, t_len, d = x_prompt.shape
    n_smp = x_sample.shape[0]
    assert t_len % 512 == 0 and x_sample.shape[1] == 1
    nf = norm_final[None, :]
    sample_states = (state_s5_re, state_s5_im, state_mlstm_C, state_mlstm_n, state_mlstm_m, state_gla_S,
                     state_rwkv_S, state_rwkv_shift, state_ffn_conv)

    xp = x_prompt.reshape(n_seq * t_len, d)
    xs = x_sample.reshape(n_smp, d)
    p_states, s_states = [], []
    for l in range(depth):
        p = _layer_params(l, prm)
        final = l == depth - 1
        xp, ps = _prompt_layer(xp, p, nf, n_seq, t_len, final)
        xs, ss = _sample_layer(xs, tuple(s[l] for s in sample_states), p, nf, final)
        p_states.append(ps)
        s_states.append(ss)
    new_p = tuple(jnp.stack([st[i] for st in p_states]) for i in range(9))
    new_s = tuple(jnp.stack([st[i] for st in s_states]) for i in range(9))
    return (xp.reshape(n_seq, t_len, d), xs.reshape(n_smp, 1, d)) + new_p + new_s
```

```python
import functools
import math

import jax
import jax.numpy as jnp
from jax import lax
from jax.experimental import pallas as pl
from jax.experimental.pallas import tpu as pltpu

F32 = jnp.float32
BF16 = jnp.bfloat16

LANES = 128
W_GROUP = 256
HEAD_DIM = 64
HEAD_SHIFT = 6
N_HEADS = 4
S5_CH = 16
S5_GROUPS = 16
S5_STATE = 64
S5_WIDTH = S5_GROUPS * S5_STATE
GLA_RANK = 16
GLA_GATE_TEMP = 16.0
RWKV_W_RANK = 32
RWKV_A_RANK = 32
RWKV_G_RANK = 64
RWKV_COLS = 3 * W_GROUP + RWKV_W_RANK + RWKV_A_RANK + RWKV_G_RANK
RWKV_DECAY_SCALE = 0.6065306597126334
CONV_W = 3
EPS = 1e-6
GN_EPS = 64e-5
NEG_BIG = -1e30
EXP_CLAMP = 80.0

SMALL_W = LANES
OFF_SMALL = RWKV_COLS
OFF_MAIN = RWKV_COLS + SMALL_W
D_PROJ = OFF_MAIN + 9 * W_GROUP
LANE_MI = GLA_RANK
LANE_MF = GLA_RANK + N_HEADS
(BLK_U, BLK_MQ, BLK_MK, BLK_MV, BLK_MO, BLK_GQ, BLK_GK, BLK_GV, BLK_GG) = range(
    OFF_MAIN // W_GROUP, OFF_MAIN // W_GROUP + 9)
BLK_SMALL = OFF_SMALL // SMALL_W

CHUNK_ML = 128
CHUNK_RW = 64
SEQS_PER_STEP = 4
VMEM_LIMIT = 56 * 1024 * 1024


def _cparams(*sem):
    return pltpu.CompilerParams(dimension_semantics=sem, vmem_limit_bytes=VMEM_LIMIT)


def _dot(a, b):
    return jnp.dot(a.astype(BF16), b.astype(BF16), preferred_element_type=F32)


def _dot_nt(a, b):
    return lax.dot_general(a.astype(BF16), b.astype(BF16), (((1,), (1,)), ((), ())),
                           preferred_element_type=F32)


def _dot_tn(a, b):
    return lax.dot_general(a.astype(BF16), b.astype(BF16), (((0,), (0,)), ((), ())),
                           preferred_element_type=F32)


def _split3(x):
    hi = x.astype(BF16)
    r1 = x - hi.astype(F32)
    mid = r1.astype(BF16)
    lo = (r1 - mid.astype(F32)).astype(BF16)
    return hi, mid, lo


def _dot_exact_rhs(a01, x):
    hi, mid, lo = _split3(x)
    f = lambda p: jnp.dot(a01, p, preferred_element_type=F32)
    return f(hi) + f(mid) + f(lo)


def _dot_exact_lhs(x, b01):
    hi, mid, lo = _split3(x)
    f = lambda p: jnp.dot(p, b01, preferred_element_type=F32)
    return f(hi) + f(mid) + f(lo)


def _sigmoid(x):
    return 1.0 / (1.0 + jnp.exp(-x))


def _log_sigmoid(x):
    return jnp.minimum(x, 0.0) - jnp.log(1.0 + jnp.exp(-jnp.abs(x)))


def _gelu(x):
    return 0.5 * x * (1.0 + jnp.tanh(math.sqrt(2.0 / math.pi) * (x + 0.044715 * (x * x * x))))


def _iota(shape, dim):
    return lax.broadcasted_iota(jnp.int32, shape, dim)


def _head_ones():
    r = _iota((W_GROUP, W_GROUP), 0) >> HEAD_SHIFT
    c = _iota((W_GROUP, W_GROUP), 1) >> HEAD_SHIFT
    return jnp.where(r == c, 1.0, 0.0).astype(BF16)


def _head_sum(x, ones):
    return _dot_exact_lhs(x, ones)


def _head_lane_mask(h, rows=1):
    lane = _iota((rows, W_GROUP), 1) >> HEAD_SHIFT
    return jnp.where(lane == h, 1.0, 0.0).astype(F32)


def _rmsnorm_rows(x, g):
    ms = jnp.mean(x * x, axis=-1, keepdims=True)
    return x * lax.rsqrt(ms + EPS) * g


def _tril_ones(n, strict=False):
    r = _iota((n, n), 0)
    c = _iota((n, n), 1)
    return jnp.where((r > c) if strict else (r >= c), 1.0, 0.0).astype(BF16)


def _triu_ones(n):
    r = _iota((n, n), 0)
    c = _iota((n, n), 1)
    return jnp.where(r <= c, 1.0, 0.0).astype(BF16)


def _inproj_kernel(x_ref, g_ref, w_ref, wg_ref, proj_ref, gt_ref):
    hn = _rmsnorm_rows(x_ref[...], g_ref[...]).astype(BF16)
    proj_ref[...] = jnp.dot(hn, w_ref[...], preferred_element_type=F32)
    gt_ref[0] = lax.dot_general(wg_ref[...], hn, (((1,), (1,)), ((), ())),
                                preferred_element_type=F32)


def _inproj(x2d, g, w, wg, n_seq, t_len, tm):
    n, d = x2d.shape
    tiles_per_seq = t_len // tm
    return pl.pallas_call(
        _inproj_kernel,
        grid=(n // tm,),
        in_specs=[
            pl.BlockSpec((tm, d), lambda i: (i, 0)),
            pl.BlockSpec((1, d), lambda i: (0, 0)),
            pl.BlockSpec((d, D_PROJ), lambda i: (0, 0)),
            pl.BlockSpec((8, d), lambda i: (0, 0)),
        ],
        out_specs=[
            pl.BlockSpec((tm, D_PROJ), lambda i: (i, 0)),
            pl.BlockSpec((1, 8, tm), lambda i: (i // tiles_per_seq, 0, i % tiles_per_seq)),
        ],
        out_shape=[
            jax.ShapeDtypeStruct((n, D_PROJ), F32),
            jax.ShapeDtypeStruct((n_seq, 8, t_len), F32),
        ],
        compiler_params=_cparams("arbitrary"),
        name="inproj",
    )(x2d, g, w, wg)


def _s5_kernel(u_ref, h0re_ref, h0im_ref, bmat_ref, lre_ref, lim_ref, cmat_ref, d_ref, wglu_ref,
               y_ref, hre_out, him_out, bu_scr, hs_scr, hre_scr, him_scr, *, bp, tt):
    i = pl.program_id(0)

    @pl.when(i == 0)
    def _():
        hre_scr[...] = h0re_ref[...]
        him_scr[...] = h0im_ref[...]

    u = u_ref[...]
    bu_scr[...] = jnp.dot(u.astype(BF16), bmat_ref[...], preferred_element_type=F32)
    lre = lre_ref[...]
    lim = lim_ref[...]

    def step(t, carry):
        hre, him = carry
        r0 = pl.multiple_of(t * bp, bp)
        bre = bu_scr[pl.ds(r0, bp), 0:S5_WIDTH]
        bim = bu_scr[pl.ds(r0, bp), S5_WIDTH:2 * S5_WIDTH]
        nre = lre * hre - lim * him + bre
        nim = lre * him + lim * hre + bim
        hs_scr[pl.ds(r0, bp), 0:S5_WIDTH] = nre
        hs_scr[pl.ds(r0, bp), S5_WIDTH:2 * S5_WIDTH] = nim
        return nre, nim

    carry = (hre_scr[...], him_scr[...])
    if tt == 1:
        hre, him = step(0, carry)
    else:
        hre, him = lax.fori_loop(0, tt, step, carry)
    hre_scr[...] = hre
    him_scr[...] = him
    hre_out[...] = hre
    him_out[...] = him
    y = jnp.dot(hs_scr[...].astype(BF16), cmat_ref[...], preferred_element_type=F32) + d_ref[...] * u
    z = _gelu(y)
    y_ref[...] = (z * _sigmoid(_dot(z, wglu_ref[...]))).astype(y_ref.dtype)


def _s5(u_tb, h0re, h0im, sp, bp, t_len, tt):
    n = u_tb.shape[0]
    full = lambda shape: pl.BlockSpec(shape, lambda i: (0,) * len(shape))
    return pl.pallas_call(
        functools.partial(_s5_kernel, bp=bp, tt=tt),
        grid=(t_len // tt,),
        in_specs=[
            pl.BlockSpec((tt * bp, W_GROUP), lambda i: (i, 0)),
            full((bp, S5_WIDTH)), full((bp, S5_WIDTH)),
            full((W_GROUP, 2 * S5_WIDTH)),
            full((bp, S5_WIDTH)), full((bp, S5_WIDTH)),
            full((2 * S5_WIDTH, W_GROUP)),
            full((1, W_GROUP)),
            full((W_GROUP, W_GROUP)),
        ],
        out_specs=[
            pl.BlockSpec((tt * bp, W_GROUP), lambda i: (i, 0)),
            full((bp, S5_WIDTH)), full((bp, S5_WIDTH)),
        ],
        out_shape=[
            jax.ShapeDtypeStruct((n, W_GROUP), BF16),
            jax.ShapeDtypeStruct((bp, S5_WIDTH), F32),
            jax.ShapeDtypeStruct((bp, S5_WIDTH), F32),
        ],
        scratch_shapes=[
            pltpu.VMEM((tt * bp, 2 * S5_WIDTH), F32),
            pltpu.VMEM((tt * bp, 2 * S5_WIDTH), F32),
            pltpu.VMEM((bp, S5_WIDTH), F32),
            pltpu.VMEM((bp, S5_WIDTH), F32),
        ],
        compiler_params=_cparams("arbitrary"),
        name="s5_scan",
    )(u_tb, h0re, h0im, sp["bmat"], jnp.broadcast_to(sp["lre"], (bp, S5_WIDTH)),
      jnp.broadcast_to(sp["lim"], (bp, S5_WIDTH)), sp["cmat"], sp["d"], sp["wglu"])


def _mlstm_kernel(q_ref, k_ref, v_ref, og_ref, sm_ref, gt_ref, brow_ref, bcol_ref, norm_ref,
                  y_ref, c_out, n_out, m_out, c_scr, n_scr, m_scr, *, L, nc):
    c = pl.program_id(1)

    @pl.when(c == 0)
    def _():
        c_scr[...] = jnp.zeros_like(c_scr)
        n_scr[...] = jnp.zeros_like(n_scr)
        m_scr[...] = jnp.zeros_like(m_scr)

    q = q_ref[...]
    k = k_ref[...] * (HEAD_DIM ** -0.5)
    v = v_ref[...]
    pre = sm_ref[...] + brow_ref[...]
    lf_col = _log_sigmoid(pre)
    gtb = gt_ref[0] + bcol_ref[...]
    lf_row = _log_sigmoid(gtb)
    bc_col = _dot_exact_rhs(_tril_ones(L), lf_col)
    bc_row = _dot_exact_lhs(lf_row, _triu_ones(L))
    C = c_scr[...]
    n = n_scr[0:1, :]
    qC = _dot_nt(q, C)
    qn = q * n
    causal = _iota((L, L), 0) >= _iota((L, L), 1)
    ones = _head_ones()

    hc = jnp.zeros((L, W_GROUP), F32)
    wl_all = jnp.zeros((L, W_GROUP), F32)
    w0l_all = jnp.zeros((1, W_GROUP), F32)
    for h in range(N_HEADS):
        mask = _head_lane_mask(h)
        bcc = bc_col[:, LANE_MF + h:LANE_MF + h + 1]
        bcr = bc_row[N_HEADS + h:N_HEADS + h + 1, :]
        i_row = gtb[h:h + 1, :]
        i_col = pre[:, LANE_MI + h:LANE_MI + h + 1]
        m0 = m_scr[h:h + 1, 0:1]
        dmat = jnp.where(causal, bcc - bcr + i_row, NEG_BIG)
        g = bcc + m0
        m_row = jnp.maximum(g, jnp.max(dmat, axis=1, keepdims=True))
        w = jnp.exp(dmat - m_row)
        w0 = jnp.exp(g - m_row)
        ws = w * _dot_nt(q * mask, k)
        num = w0 * qC + _dot(ws, v * mask)
        den = (w0 * jnp.sum(qn * mask, axis=1, keepdims=True)
               + jnp.sum(ws, axis=1, keepdims=True))
        hc = hc + mask * (num / jnp.maximum(jnp.abs(den), jnp.exp(-m_row)))
        m_new = m_row[L - 1:L, :]
        b_last = bcc[L - 1:L, :]
        wl_all = wl_all + mask * jnp.exp(b_last - bcc + i_col - m_new)
        w0l_all = w0l_all + mask * jnp.exp(b_last + m0 - m_new)
        m_scr[h:h + 1, :] = jnp.broadcast_to(m_new, (1, LANES))

    kw = k * wl_all
    c_new = w0l_all * C + ones.astype(F32) * _dot_tn(v, kw)
    n_new = w0l_all * n + jnp.sum(kw, axis=0, keepdims=True)
    c_scr[...] = c_new
    n_scr[0:1, :] = n_new

    ms = _head_sum(hc * hc, ones) * (1.0 / HEAD_DIM)
    y = hc * lax.rsqrt(ms + EPS) * norm_ref[...] * _sigmoid(og_ref[...])
    y_ref[...] = y.astype(y_ref.dtype)

    @pl.when(c == nc - 1)
    def _():
        c_out[0] = c_new
        n_out[0] = jnp.broadcast_to(n_new, (8, W_GROUP))
        m_out[0] = m_scr[...]


def _proj_spec(L, nc, blk, width=W_GROUP):
    return pl.BlockSpec((L, width), lambda b, c: (b * nc + c, blk))


def _mlstm(proj, gt, p, n_seq, t_len):
    L = CHUNK_ML
    nc = t_len // L
    n = proj.shape[0]
    const = lambda shape: pl.BlockSpec(shape, lambda b, c: (0,) * len(shape))
    return pl.pallas_call(
        functools.partial(_mlstm_kernel, L=L, nc=nc),
        grid=(n_seq, nc),
        in_specs=[
            _proj_spec(L, nc, BLK_MQ), _proj_spec(L, nc, BLK_MK), _proj_spec(L, nc, BLK_MV),
            _proj_spec(L, nc, BLK_MO), _proj_spec(L, nc, BLK_SMALL, SMALL_W),
            pl.BlockSpec((1, 8, L), lambda b, c: (b, 0, c)),
            const((1, SMALL_W)), const((8, L)), const((1, W_GROUP)),
        ],
        out_specs=[
            pl.BlockSpec((L, W_GROUP), lambda b, c: (b * nc + c, 0)),
            pl.BlockSpec((1, W_GROUP, W_GROUP), lambda b, c: (b, 0, 0)),
            pl.BlockSpec((1, 8, W_GROUP), lambda b, c: (b, 0, 0)),
            pl.BlockSpec((1, 8, LANES), lambda b, c: (b, 0, 0)),
        ],
        out_shape=[
            jax.ShapeDtypeStruct((n, W_GROUP), BF16),
            jax.ShapeDtypeStruct((n_seq, W_GROUP, W_GROUP), F32),
            jax.ShapeDtypeStruct((n_seq, 8, W_GROUP), F32),
            jax.ShapeDtypeStruct((n_seq, 8, LANES), F32),
        ],
        scratch_shapes=[
            pltpu.VMEM((W_GROUP, W_GROUP), F32),
            pltpu.VMEM((8, W_GROUP), F32),
            pltpu.VMEM((8, LANES), F32),
        ],
        compiler_params=_cparams("arbitrary", "arbitrary"),
        name="mlstm_chunk",
    )(proj, proj, proj, proj, proj, gt, p["ml_brow"], p["ml_bcol"], p["ml_norm"])


def _gla_kernel(q_ref, k_ref, v_ref, gg_ref, sm_ref, wa_ref, ba_ref, norm_ref,
                y_ref, s_out, s_scr, *, L, nc):
    c = pl.program_id(1)

    @pl.when(c == 0)
    def _():
        s_scr[...] = jnp.zeros_like(s_scr)

    q = q_ref[...] * (HEAD_DIM ** -0.5)
    k = k_ref[...]
    v = v_ref[...]
    la = _log_sigmoid(_dot(sm_ref[...], wa_ref[...]) + ba_ref[...]) * (1.0 / GLA_GATE_TEMP)
    bc = _dot_exact_rhs(_tril_ones(L), la)
    qs = q * jnp.exp(bc)
    kh = k * jnp.exp(jnp.minimum(-bc, EXP_CLAMP))
    ST = s_scr[...]
    o = _dot_nt(qs, ST)
    causal = _iota((L, L), 0) >= _iota((L, L), 1)
    for h in range(N_HEADS):
        mask = _head_lane_mask(h)
        att = jnp.where(causal, _dot_nt(qs * mask, kh), 0.0)
        o = o + _dot(att, v * mask)
    b_last = bc[L - 1:L, :]
    kbar = k * jnp.exp(b_last - bc)
    ones = _head_ones()
    s_new = ST * jnp.exp(b_last) + ones.astype(F32) * _dot_tn(v, kbar)
    s_scr[...] = s_new

    ms = _head_sum(o * o, ones) * (1.0 / HEAD_DIM)
    gg = gg_ref[...]
    y = o * lax.rsqrt(ms + EPS) * norm_ref[...] * (gg * _sigmoid(gg))
    y_ref[...] = y.astype(y_ref.dtype)

    @pl.when(c == nc - 1)
    def _():
        s_out[0] = s_new


def _gla(proj, p, n_seq, t_len):
    L = CHUNK_ML
    nc = t_len // L
    n = proj.shape[0]
    const = lambda shape: pl.BlockSpec(shape, lambda b, c: (0,) * len(shape))
    return pl.pallas_call(
        functools.partial(_gla_kernel, L=L, nc=nc),
        grid=(n_seq, nc),
        in_specs=[
            _proj_spec(L, nc, BLK_GQ), _proj_spec(L, nc, BLK_GK), _proj_spec(L, nc, BLK_GV),
            _proj_spec(L, nc, BLK_GG), _proj_spec(L, nc, BLK_SMALL, SMALL_W),
            const((SMALL_W, W_GROUP)), const((1, W_GROUP)), const((1, W_GROUP)),
        ],
        out_specs=[
            pl.BlockSpec((L, W_GROUP), lambda b, c: (b * nc + c, 0)),
            pl.BlockSpec((1, W_GROUP, W_GROUP), lambda b, c: (b, 0, 0)),
        ],
        out_shape=[
            jax.ShapeDtypeStruct((n, W_GROUP), BF16),
            jax.ShapeDtypeStruct((n_seq, W_GROUP, W_GROUP), F32),
        ],
        scratch_shapes=[pltpu.VMEM((W_GROUP, W_GROUP), F32)],
        compiler_params=_cparams("arbitrary", "arbitrary"),
        name="gla_chunk",
    )(proj, proj, proj, proj, proj, p["gla_wa"], p["gla_ba"], p["gla_norm"])


def _rwkv_vectors(rc, prev, p_mu, p_w0, p_a0, p_kk, p_ka, p_rk, w2, a2, g2, ones):
    xm = rc + p_mu * (prev - rc)
    rr = xm[:, 0:W_GROUP]
    rk = xm[:, W_GROUP:2 * W_GROUP]
    rv = xm[:, 2 * W_GROUP:3 * W_GROUP]
    tail = xm[:, 3 * W_GROUP:RWKV_COLS]
    lw = -RWKV_DECAY_SCALE * _sigmoid(p_w0 + _dot(jnp.tanh(tail), w2))
    a = _sigmoid(p_a0 + _dot(tail, a2))
    g = _dot(_sigmoid(tail), g2)
    kk = rk * p_kk
    kk = kk * lax.rsqrt(jnp.maximum(_head_sum(kk * kk, ones), 1e-24))
    kt = rk * (1.0 + (a - 1.0) * p_ka)
    bonus = _head_sum(rr * kt * p_rk, ones) * rv
    return rr, lw, kt, rv, kk, a, g, bonus


def _head_layernorm(o, g, ones):
    mu = _head_sum(o, ones) * (1.0 / HEAD_DIM)
    oc = o - mu
    var = _head_sum(oc * oc, ones) * (1.0 / HEAD_DIM)
    return oc * lax.rsqrt(var + GN_EPS) * g


def _stack_heads(x):
    return jnp.concatenate([x * _head_lane_mask(h) for h in range(N_HEADS)], axis=0)


def _each(f, *seqs):
    return [f(*args) for args in zip(*seqs)]


def _unit_lower_inverse(a_list):
    n = a_list[0].shape[0]
    r = _iota((n, n), 0)
    c = _iota((n, n), 1)
    eye = jnp.where(r == c, 1.0, 0.0).astype(F32)
    in16 = (r >> 4) == (c >> 4)
    in32 = ((r >> 5) == (c >> 5)) & ((r >> 4) > (c >> 4))
    in64 = ((r >> 6) == (c >> 6)) & ((r >> 5) > (c >> 5))
    x16 = _each(lambda a: jnp.where(in16, -a, 0.0), a_list)
    x2 = _each(lambda x: _dot(x, x), x16)
    x4 = _each(lambda x: _dot(x, x), x2)
    x8 = _each(lambda x: _dot(x, x), x4)
    t = _each(lambda x: eye + x, x16)
    t = _each(lambda t_, x: t_ + _dot(t_, x), t, x2)
    t = _each(lambda t_, x: t_ + _dot(t_, x), t, x4)
    t = _each(lambda t_, x: t_ + _dot(t_, x), t, x8)
    for sel in (in32, in64):
        w = _each(lambda a, t_: _dot(jnp.where(sel, a, 0.0), t_), a_list, t)
        t = _each(lambda t_, w_: t_ - _dot(t_, w_), t, w)
    return t


def _rwkv_chunks(rcs, prev_rows, states, prm, ones, L):
    (mu, w0, a0, pkk, pka, prk, w2, a2, g2, norm) = prm
    first_row = _iota((L, RWKV_COLS), 0) == 0
    prevs = _each(lambda rc, pr: jnp.where(first_row, pr, pltpu.roll(rc, 1, 0)), rcs, prev_rows)
    vecs = _each(lambda rc, pv: _rwkv_vectors(rc, pv, mu, w0, a0, pkk, pka, prk, w2, a2, g2, ones),
                 rcs, prevs)
    r, lw, kx, v, kk, a, g, bonus = [list(col) for col in zip(*vecs)]

    tril = _tril_ones(L)
    lc = _each(lambda x: _dot_exact_rhs(tril, x), lw)
    akk = _each(lambda a_, k_: a_ * k_, a, kk)
    e_neg = _each(lambda x: jnp.exp(-x), lc)
    l_last = _each(lambda x: x[L - 1:L, :], lc)
    e_end = _each(lambda ll, x: jnp.exp(ll - x), l_last, lc)
    Ks = _each(lambda k_, c_, w_: _stack_heads(k_ * jnp.exp(c_ - w_)), kk, lc, lw)
    As = _each(lambda x, e: _stack_heads(x * e), akk, e_neg)
    Khs = _each(lambda x, e: _stack_heads(x * e), kx, e_neg)
    Rs = _each(lambda x, c_: _stack_heads(x * jnp.exp(c_)), r, lc)
    Vs = _each(_stack_heads, v)
    Kbs = _each(lambda x, e: _stack_heads(x * e), kx, e_end)
    Abs = _each(lambda x, e: _stack_heads(x * e), akk, e_end)

    n4 = N_HEADS * L
    strict = _iota((n4, n4), 0) > _iota((n4, n4), 1)
    incl = _iota((n4, n4), 0) >= _iota((n4, n4), 1)
    a_ua = _each(lambda x, y_: jnp.where(strict, _dot_nt(x, y_), 0.0), Ks, As)
    a_vk = _each(lambda x, y_: jnp.where(strict, _dot_nt(x, y_), 0.0), Ks, Khs)
    b_rk = _each(lambda x, y_: jnp.where(incl, _dot_nt(x, y_), 0.0), Rs, Khs)
    b_ra = _each(lambda x, y_: jnp.where(incl, _dot_nt(x, y_), 0.0), Rs, As)
    T = _unit_lower_inverse(a_ua)

    rhs = _each(lambda k_, s_, a_, v_: _dot_nt(k_, s_) + _dot(a_, v_), Ks, states, a_vk, Vs)
    Us = _each(_dot, T, rhs)
    Os = _each(lambda r_, s_, b_, v_: _dot_nt(r_, s_) + _dot(b_, v_), Rs, states, b_rk, Vs)
    Os = _each(lambda o_, b_, u_: o_ - _dot(b_, u_), Os, b_ra, Us)
    o = _each(lambda x: x[0:L] + x[L:2 * L] + x[2 * L:3 * L] + x[3 * L:4 * L], Os)
    s_new = _each(lambda s_, ll, v_, k_: s_ * jnp.exp(ll) + _dot_tn(v_, k_), states, l_last, Vs, Kbs)
    s_new = _each(lambda s_, u_, a_: s_ - _dot_tn(u_, a_), s_new, Us, Abs)
    y = _each(lambda o_, b_, g_: (_head_layernorm(o_, norm, ones) + b_) * g_, o, bonus, g)
    return y, s_new


def _rwkv_kernel(rc_ref, mu_ref, w0_ref, a0_ref, kk_ref, ka_ref, rk_ref, w2_ref, a2_ref, g2_ref,
                 norm_ref, y_ref, s_out, shift_out, s_scr, prev_scr, *, L, nc, G):
    c = pl.program_id(1)

    @pl.when(c == 0)
    def _():
        s_scr[...] = jnp.zeros_like(s_scr)
        prev_scr[...] = jnp.zeros_like(prev_scr)

    ones = _head_ones()
    prm = (mu_ref[...], w0_ref[...], a0_ref[...], kk_ref[...], ka_ref[...], rk_ref[...],
           w2_ref[...], a2_ref[...], g2_ref[...], norm_ref[...])
    rcs = [rc_ref[gi] for gi in range(G)]
    ys, s_new = _rwkv_chunks(rcs, [prev_scr[gi, 0:1, :] for gi in range(G)],
                             [s_scr[gi] for gi in range(G)], prm, ones, L)
    for gi in range(G):
        prev_scr[gi, 0:1, :] = rcs[gi][L - 1:L, :]
        s_scr[gi] = s_new[gi]
        y_ref[gi] = ys[gi].astype(y_ref.dtype)

    @pl.when(c == nc - 1)
    def _():
        s_out[...] = s_scr[...]
        shift_out[...] = jnp.broadcast_to(prev_scr[:, 0:1, :], shift_out.shape)


def _rwkv(proj, p, n_seq, t_len):
    L = CHUNK_RW
    G = SEQS_PER_STEP
    nc = t_len // L
    const = lambda shape: pl.BlockSpec(shape, lambda b, c: (0,) * len(shape))
    row = const((1, W_GROUP))
    pad = const((LANES, W_GROUP))
    y, s_new, shift = pl.pallas_call(
        functools.partial(_rwkv_kernel, L=L, nc=nc, G=G),
        grid=(n_seq // G, nc),
        in_specs=[
            pl.BlockSpec((G, L, RWKV_COLS), lambda b, c: (b, c, 0)),
            const((1, RWKV_COLS)), row, row, row, row, row, pad, pad, pad, row,
        ],
        out_specs=[
            pl.BlockSpec((G, L, W_GROUP), lambda b, c: (b, c, 0)),
            pl.BlockSpec((G, W_GROUP, W_GROUP), lambda b, c: (b, 0, 0)),
            pl.BlockSpec((G, 8, RWKV_COLS), lambda b, c: (b, 0, 0)),
        ],
        out_shape=[
            jax.ShapeDtypeStruct((n_seq, t_len, W_GROUP), BF16),
            jax.ShapeDtypeStruct((n_seq, W_GROUP, W_GROUP), F32),
            jax.ShapeDtypeStruct((n_seq, 8, RWKV_COLS), F32),
        ],
        scratch_shapes=[
            pltpu.VMEM((G, W_GROUP, W_GROUP), F32),
            pltpu.VMEM((G, 8, RWKV_COLS), F32),
        ],
        compiler_params=_cparams("arbitrary", "arbitrary"),
        name="rwkv_chunk",
    )(proj.reshape(n_seq, t_len, D_PROJ), p["rw_mu"], p["rw_w0"], p["rw_a0"], p["rw_kk"], p["rw_ka"],
      p["rw_rk"], p["rw_w2"], p["rw_a2"], p["rw_g2"], p["rw_norm"])
    return y.reshape(n_seq * t_len, W_GROUP), s_new, shift


FF_CHUNK = 256


def _mix_residual(x_ref, ys, wout_ref):
    acc = x_ref[...]
    for j, y in enumerate(ys):
        acc = acc + jnp.dot(y.astype(BF16), wout_ref[j * W_GROUP:(j + 1) * W_GROUP, :],
                            preferred_element_type=F32)
    return acc


def _ffn_body(x1, nrm_ref, wup_ref, cw_ref, cb_ref, wdn_ref, prev_rows, d_ff, act_scr):
    h2 = _rmsnorm_rows(x1, nrm_ref[...]).astype(BF16)
    for j in range(d_ff // FF_CHUNK):
        lo, hi = j * FF_CHUNK, (j + 1) * FF_CHUNK
        ug = jnp.dot(h2, wup_ref[:, lo:hi], preferred_element_type=F32)
        uv = jnp.dot(h2, wup_ref[:, d_ff + lo:d_ff + hi], preferred_element_type=F32)
        p2, p1 = prev_rows(j, ug)
        conv = (cb_ref[:, lo:hi] + cw_ref[0:1, lo:hi] * p2 + cw_ref[1:2, lo:hi] * p1
                + cw_ref[2:3, lo:hi] * ug)
        act_scr[:, lo:hi] = (_gelu(conv) * uv).astype(BF16)
    return x1 + jnp.dot(act_scr[...], wdn_ref[...], preferred_element_type=F32)


def _ffn_seq_kernel(x_ref, y0_ref, y1_ref, y2_ref, y3_ref, wout_ref, nrm_ref, wup_ref, cw_ref, cb_ref,
                    wdn_ref, nf_ref, o_ref, buf_out, carry_scr, act_scr, *, tm, d_ff, final):
    t = pl.program_id(1)

    @pl.when(t == 0)
    def _():
        carry_scr[...] = jnp.zeros_like(carry_scr)

    x1 = _mix_residual(x_ref, (y0_ref[...], y1_ref[...], y2_ref[...], y3_ref[...]), wout_ref)
    row = _iota((tm, FF_CHUNK), 0)

    def prev_rows(j, ug):
        lo, hi = j * FF_CHUNK, (j + 1) * FF_CHUNK
        c0 = carry_scr[0:1, lo:hi]
        c1 = carry_scr[1:2, lo:hi]
        p1 = jnp.where(row == 0, c1, pltpu.roll(ug, 1, 0))
        p2 = jnp.where(row == 0, c0, jnp.where(row == 1, c1, pltpu.roll(ug, 2, 0)))
        carry_scr[0:2, lo:hi] = ug[tm - 2:tm, :]
        return p2, p1

    out = _ffn_body(x1, nrm_ref, wup_ref, cw_ref, cb_ref, wdn_ref, prev_rows, d_ff, act_scr)
    if final:
        out = _rmsnorm_rows(out, nf_ref[...])
    o_ref[...] = out
    buf_out[0] = carry_scr[...]


def _ffn_step_kernel(x_ref, y0_ref, y1_ref, y2_ref, y3_ref, wout_ref, nrm_ref, wup_ref, cw_ref, cb_ref,
                     wdn_ref, nf_ref, p2_ref, p1_ref, o_ref, ug_out, act_scr, *, d_ff, final):
    x1 = _mix_residual(x_ref, (y0_ref[...], y1_ref[...], y2_ref[...], y3_ref[...]), wout_ref)

    def prev_rows(j, ug):
        lo, hi = j * FF_CHUNK, (j + 1) * FF_CHUNK
        ug_out[:, lo:hi] = ug
        return p2_ref[:, lo:hi], p1_ref[:, lo:hi]

    out = _ffn_body(x1, nrm_ref, wup_ref, cw_ref, cb_ref, wdn_ref, prev_rows, d_ff, act_scr)
    if final:
        out = _rmsnorm_rows(out, nf_ref[...])
    o_ref[...] = out


def _ffn_seq(x2d, ys, p, norm_final, n_seq, t_len, tm, final):
    n, d = x2d.shape
    d_ff = p["ffn_wdn"].shape[0]
    nt = t_len // tm
    rows = lambda w: pl.BlockSpec((tm, w), lambda b, t: (b * nt + t, 0))
    const = lambda shape: pl.BlockSpec(shape, lambda b, t: (0,) * len(shape))
    return pl.pallas_call(
        functools.partial(_ffn_seq_kernel, tm=tm, d_ff=d_ff, final=final),
        grid=(n_seq, nt),
        in_specs=[rows(d)] + [rows(W_GROUP)] * 4 + [
            const((d, d)), const((1, d)), const((d, 2 * d_ff)), const((CONV_W, d_ff)),
            const((1, d_ff)), const((d_ff, d)), const((1, d))],
        out_specs=[rows(d), pl.BlockSpec((1, CONV_W - 1, d_ff), lambda b, t: (b, 0, 0))],
        out_shape=[jax.ShapeDtypeStruct((n, d), F32),
                   jax.ShapeDtypeStruct((n_seq, CONV_W - 1, d_ff), F32)],
        scratch_shapes=[pltpu.VMEM((CONV_W - 1, d_ff), F32), pltpu.VMEM((tm, d_ff), BF16)],
        compiler_params=_cparams("arbitrary", "arbitrary"),
        name="wout_ffn_seq",
    )(x2d, *ys, p["w_out"], p["norm_ffn"], p["ffn_wup"], p["ffn_cw"], p["ffn_cb"], p["ffn_wdn"],
      norm_final)


def _ffn_step(x2d, ys, p, norm_final, prev2, prev1, final):
    n, d = x2d.shape
    d_ff = p["ffn_wdn"].shape[0]
    full = lambda shape: pl.BlockSpec(shape, lambda i: (0,) * len(shape))
    return pl.pallas_call(
        functools.partial(_ffn_step_kernel, d_ff=d_ff, final=final),
        grid=(1,),
        in_specs=[full((n, d))] + [full((n, W_GROUP))] * 4 + [
            full((d, d)), full((1, d)), full((d, 2 * d_ff)), full((CONV_W, d_ff)),
            full((1, d_ff)), full((d_ff, d)), full((1, d)), full((n, d_ff)), full((n, d_ff))],
        out_specs=[full((n, d)), full((n, d_ff))],
        out_shape=[jax.ShapeDtypeStruct((n, d), F32), jax.ShapeDtypeStruct((n, d_ff), F32)],
        scratch_shapes=[pltpu.VMEM((n, d_ff), BF16)],
        compiler_params=_cparams("arbitrary"),
        name="wout_ffn_step",
    )(x2d, *ys, p["w_out"], p["norm_ffn"], p["ffn_wup"], p["ffn_cw"], p["ffn_cb"], p["ffn_wdn"],
      norm_final, prev2, prev1)


def _step_vec_kernel(proj_ref, shift_ref, m_ref, brow_ref, wa_ref, ba_ref, mu_ref, w0_ref, a0_ref,
                     kk_ref, ka_ref, rk_ref, w2_ref, a2_ref, g2_ref,
                     iw_out, fw_out, m_out, alpha_out, rr_out, kt_out, kk_out, akk_out, w_out, g_out,
                     bonus_out, rv_out):
    ones = _head_ones()
    sm = proj_ref[:, OFF_SMALL:OFF_SMALL + SMALL_W]
    pre = sm + brow_ref[...]
    logf = pltpu.roll(_log_sigmoid(pre), LANES - N_HEADS, 1)
    m_old = m_ref[...]
    m_new = jnp.maximum(logf + m_old, pre)
    iw_out[...] = jnp.exp(pre - m_new)
    fw_out[...] = jnp.exp(logf + m_old - m_new)
    m_out[...] = m_new
    la = _log_sigmoid(_dot(sm, wa_ref[...]) + ba_ref[...]) * (1.0 / GLA_GATE_TEMP)
    alpha_out[...] = jnp.exp(la)
    rc = proj_ref[:, 0:RWKV_COLS]
    rr, lw, kt, rv, kk, a, g, bonus = _rwkv_vectors(
        rc, shift_ref[...], mu_ref[...], w0_ref[...], a0_ref[...], kk_ref[...], ka_ref[...],
        rk_ref[...], w2_ref[...], a2_ref[...], g2_ref[...], ones)
    rr_out[...] = rr
    kt_out[...] = kt
    kk_out[...] = kk
    akk_out[...] = a * kk
    w_out[...] = jnp.exp(lw)
    g_out[...] = g
    bonus_out[...] = bonus
    rv_out[...] = rv


def _step_vectors(proj, shift, m_pad, p):
    n = proj.shape[0]
    full = lambda a: pl.BlockSpec(a.shape, lambda i: (0,) * a.ndim)
    args = (proj, shift, m_pad, p["ml_brow"], p["gla_wa"], p["gla_ba"], p["rw_mu"], p["rw_w0"],
            p["rw_a0"], p["rw_kk"], p["rw_ka"], p["rw_rk"], p["rw_w2"], p["rw_a2"], p["rw_g2"])
    small = jax.ShapeDtypeStruct((n, SMALL_W), F32)
    wide = jax.ShapeDtypeStruct((n, W_GROUP), F32)
    outs = [small, small, small] + [wide] * 9
    return pl.pallas_call(
        _step_vec_kernel,
        grid=(1,),
        in_specs=[full(a) for a in args],
        out_specs=[pl.BlockSpec(o.shape, lambda i: (0, 0)) for o in outs],
        out_shape=outs,
        compiler_params=_cparams("arbitrary"),
        name="step_vectors",
    )(*args)


PAIRS_PER_BLOCK = 64


def _mlstm_step_kernel(c_ref, n_ref, q_ref, k_ref, v_ref, iw_ref, fw_ref, m_ref, c_out, n_out, h_out):
    k = k_ref[...] * (HEAD_DIM ** -0.5)
    q = q_ref[...]
    iw = iw_ref[...]
    fw = fw_ref[...]
    c_new = fw * c_ref[...] + iw * (v_ref[...] * k)
    n_new = fw * n_ref[...] + iw * k
    num = jnp.sum(c_new * q, axis=2, keepdims=True)
    den = jnp.sum(n_new * q, axis=2, keepdims=True)
    c_out[...] = c_new
    n_out[...] = n_new
    h_out[...] = num / jnp.maximum(jnp.abs(den), jnp.exp(-m_ref[...]))


def _gla_step_kernel(s_ref, q_ref, k_ref, v_ref, a_ref, s_out, o_out):
    s_new = a_ref[...] * s_ref[...] + k_ref[...] * v_ref[...]
    s_out[...] = s_new
    o_out[...] = jnp.sum((q_ref[...] * (HEAD_DIM ** -0.5)) * s_new, axis=1, keepdims=True)


def _rwkv_step_kernel(s_ref, kk_ref, akk_ref, w_ref, kt_ref, rr_ref, v_ref, s_out, o_out):
    s = s_ref[...]
    sk = jnp.sum(s * kk_ref[...], axis=2, keepdims=True)
    s_new = s * w_ref[...] - sk * akk_ref[...] + v_ref[...] * kt_ref[...]
    s_out[...] = s_new
    o_out[...] = jnp.sum(s_new * rr_ref[...], axis=2, keepdims=True)


def _pairs_call(kernel, name, args, outs):
    npairs = args[0].shape[0]
    P = min(PAIRS_PER_BLOCK, npairs)
    spec = lambda a: pl.BlockSpec((P,) + tuple(a.shape[1:]), lambda i: (i, 0, 0))
    return pl.pallas_call(
        kernel,
        grid=(npairs // P,),
        in_specs=[spec(a) for a in args],
        out_specs=[spec(o) for o in outs],
        out_shape=outs,
        compiler_params=_cparams("arbitrary"),
        name=name,
    )(*args)


def _step_post_kernel(hml_ref, mo_ref, ogla_ref, gg_ref, orw_ref, g_ref, bonus_ref,
                      mln_ref, gln_ref, rwn_ref, yml_out, ygla_out, yrw_out):
    ones = _head_ones()
    hml = hml_ref[...]
    ms = _head_sum(hml * hml, ones) * (1.0 / HEAD_DIM)
    yml_out[...] = (hml * lax.rsqrt(ms + EPS) * mln_ref[...] * _sigmoid(mo_ref[...])).astype(BF16)
    og = ogla_ref[...]
    ms = _head_sum(og * og, ones) * (1.0 / HEAD_DIM)
    gg = gg_ref[...]
    ygla_out[...] = (og * lax.rsqrt(ms + EPS) * gln_ref[...] * (gg * _sigmoid(gg))).astype(BF16)
    yrw_out[...] = ((_head_layernorm(orw_ref[...], rwn_ref[...], ones) + bonus_ref[...])
                    * g_ref[...]).astype(BF16)


def _step_post(hml, mo, ogla, gg, orw, g, bonus, p):
    n = hml.shape[0]
    args = (hml, mo, ogla, gg, orw, g, bonus, p["ml_norm"], p["gla_norm"], p["rw_norm"])
    full = lambda a: pl.BlockSpec(a.shape, lambda i: (0,) * a.ndim)
    out = jax.ShapeDtypeStruct((n, W_GROUP), BF16)
    return pl.pallas_call(
        _step_post_kernel,
        grid=(1,),
        in_specs=[full(a) for a in args],
        out_specs=[pl.BlockSpec((n, W_GROUP), lambda i: (0, 0))] * 3,
        out_shape=[out] * 3,
        compiler_params=_cparams("arbitrary"),
        name="step_post",
    )(*args)


def _layer_params(l, prm):
    p = {}
    w_in = prm["w_in"][l]
    d = w_in.shape[0]
    sizes = (W_GROUP, W_GROUP, W_GROUP, W_GROUP, N_HEADS, N_HEADS, W_GROUP, W_GROUP, W_GROUP, W_GROUP,
             GLA_RANK, W_GROUP, RWKV_COLS)
    offs = [0]
    for s in sizes:
        offs.append(offs[-1] + s)
    col = lambda i: w_in[:, offs[i]:offs[i + 1]]
    (u, mq, mk, mv, mi, mf, mo, gq, gk, gv, ga, gg, rcols) = [col(i) for i in range(13)]
    small = jnp.concatenate([ga, mi, mf, jnp.zeros((d, SMALL_W - GLA_RANK - 2 * N_HEADS), F32)], axis=1)
    p["w_in"] = jnp.concatenate([rcols, small, u, mq, mk, mv, mo, gq, gk, gv, gg], axis=1).astype(BF16)
    p["w_gate_t"] = jnp.concatenate([mi, mf], axis=1).T.astype(BF16)
    p["norm_mix"] = prm["norm_mix"][l][None, :]

    lam = lax.complex(prm["s5_lam_re"][l], prm["s5_lam_im"][l])
    dt = jnp.exp(prm["s5_log_dt"][l])[:, None]
    lam_bar = jnp.exp(lam * dt)
    b_bar = ((lam_bar - 1.0) / lam)[..., None] * lax.complex(prm["s5_b_re"][l], prm["s5_b_im"][l])
    eye = jnp.eye(S5_GROUPS, dtype=F32)
    bm = lambda b: jnp.einsum("gph,gk->ghkp", b, eye).reshape(W_GROUP, S5_WIDTH)
    p["bmat"] = jnp.concatenate([bm(b_bar.real), bm(b_bar.imag)], axis=1).astype(BF16)
    cm = lambda c: jnp.einsum("ghp,gk->kpgh", c, eye).reshape(S5_WIDTH, W_GROUP)
    p["cmat"] = jnp.concatenate([cm(prm["s5_c_re"][l]), -cm(prm["s5_c_im"][l])], axis=0).astype(BF16)
    p["lre"] = lam_bar.real.reshape(1, S5_WIDTH)
    p["lim"] = lam_bar.imag.reshape(1, S5_WIDTH)
    p["d"] = prm["s5_d"][l][None, :]
    p["wglu"] = prm["s5_w_glu"][l].astype(BF16)

    bias = prm["ml_gate_bias"][l]
    p["ml_brow"] = jnp.zeros((1, SMALL_W), F32).at[0, LANE_MI:LANE_MI + 2 * N_HEADS].set(bias)
    p["ml_bcol"] = jnp.broadcast_to(bias[:, None], (2 * N_HEADS, CHUNK_ML))
    p["ml_norm"] = prm["ml_norm"][l][None, :]

    p["gla_wa"] = jnp.zeros((SMALL_W, W_GROUP), F32).at[0:GLA_RANK].set(prm["gla_w_alpha"][l]).astype(BF16)
    p["gla_ba"] = prm["gla_b_alpha"][l][None, :]
    p["gla_norm"] = prm["gla_norm"][l][None, :]

    p["rw_mu"] = prm["rw_mu"][l][None, :]
    p["rw_w0"] = prm["rw_w0"][l][None, :]
    p["rw_a0"] = prm["rw_a0"][l][None, :]
    p["rw_kk"] = prm["rw_k_k"][l][None, :]
    p["rw_ka"] = prm["rw_k_a"][l][None, :]
    p["rw_rk"] = prm["rw_r_k"][l].reshape(1, W_GROUP)
    z = jnp.zeros((LANES, W_GROUP), F32)
    p["rw_w2"] = z.at[0:RWKV_W_RANK].set(prm["rw_w2"][l]).astype(BF16)
    p["rw_a2"] = z.at[RWKV_W_RANK:RWKV_W_RANK + RWKV_A_RANK].set(prm["rw_a2"][l]).astype(BF16)
    p["rw_g2"] = z.at[RWKV_W_RANK + RWKV_A_RANK:].set(prm["rw_g2"][l]).astype(BF16)
    p["rw_norm"] = prm["rw_norm"][l][None, :]

    p["w_out"] = prm["w_out"][l].astype(BF16)
    p["norm_ffn"] = prm["norm_ffn"][l][None, :]
    p["ffn_wup"] = prm["ffn_w_up"][l].astype(BF16)
    p["ffn_cw"] = prm["ffn_conv_w"][l]
    p["ffn_cb"] = prm["ffn_conv_b"][l][None, :]
    p["ffn_wdn"] = prm["ffn_w_down"][l].astype(BF16)
    return p


def _diag_blocks(m):
    b = m.shape[0]
    m = m.reshape(b, N_HEADS, HEAD_DIM, N_HEADS, HEAD_DIM)
    return jnp.stack([m[:, h, :, h, :] for h in range(N_HEADS)], axis=1)


def _prompt_layer(x2d, p, norm_final, n_seq, t_len, final):
    proj, gt = _inproj(x2d, p["norm_mix"], p["w_in"], p["w_gate_t"], n_seq, t_len, tm=512)

    u_tb = proj[:, OFF_MAIN:OFF_MAIN + W_GROUP].reshape(n_seq, t_len, W_GROUP).transpose(1, 0, 2)
    zeros = jnp.zeros((n_seq, S5_WIDTH), F32)
    y_s5_tb, s5_re, s5_im = _s5(u_tb.reshape(t_len * n_seq, W_GROUP), zeros, zeros, p, n_seq, t_len, tt=128)
    y_s5 = y_s5_tb.reshape(t_len, n_seq, W_GROUP).transpose(1, 0, 2).reshape(n_seq * t_len, W_GROUP)

    y_ml, ml_c, ml_n, ml_m = _mlstm(proj, gt, p, n_seq, t_len)
    y_gla, gla_st = _gla(proj, p, n_seq, t_len)
    y_rw, rw_s, rw_shift = _rwkv(proj, p, n_seq, t_len)

    x_new, ffn_buf = _ffn_seq(x2d, (y_s5, y_ml, y_gla, y_rw), p, norm_final, n_seq, t_len, tm=256,
                              final=final)
    states = (
        s5_re.reshape(n_seq, S5_GROUPS, S5_STATE),
        s5_im.reshape(n_seq, S5_GROUPS, S5_STATE),
        _diag_blocks(ml_c),
        ml_n[:, 0, :].reshape(n_seq, N_HEADS, HEAD_DIM),
        ml_m[:, 0:N_HEADS, 0],
        jnp.swapaxes(_diag_blocks(gla_st), -1, -2),
        _diag_blocks(rw_s),
        rw_shift[:, 0, :],
        ffn_buf,
    )
    return x_new, states


def _sample_layer(x2d, st, p, norm_final, final):
    (s5_re, s5_im, ml_c, ml_n, ml_m, gla_s, rw_s, rw_shift, ffn_buf) = st
    n = x2d.shape[0]
    npairs = n * N_HEADS
    proj, _ = _inproj(x2d, p["norm_mix"], p["w_in"], p["w_gate_t"], 1, n, tm=n)

    y_s5, s5_re_new, s5_im_new = _s5(proj[:, OFF_MAIN:OFF_MAIN + W_GROUP],
                                     s5_re.reshape(n, S5_WIDTH), s5_im.reshape(n, S5_WIDTH),
                                     p, n, 1, tt=1)

    m_pad = jnp.zeros((n, SMALL_W), F32).at[:, LANE_MI:LANE_MI + N_HEADS].set(ml_m)
    (iw, fw, m_new, alpha, rr, kt, kk, akk, wdec, g, bonus, rv) = _step_vectors(proj, rw_shift, m_pad, p)

    main = lambda blk: proj[:, blk * W_GROUP:(blk + 1) * W_GROUP]
    as_row = lambda a: a.reshape(npairs, 1, HEAD_DIM)
    as_col = lambda a: a.reshape(npairs, HEAD_DIM, 1)
    as_scl = lambda a: a[:, LANE_MI:LANE_MI + N_HEADS].reshape(npairs, 1, 1)
    sds = lambda *s: jax.ShapeDtypeStruct(s, F32)
    sq = (npairs, HEAD_DIM, HEAD_DIM)

    ml_c_new, ml_n_new, h_ml = _pairs_call(
        _mlstm_step_kernel, "mlstm_step",
        (ml_c.reshape(sq), as_row(ml_n), as_row(main(BLK_MQ)), as_row(main(BLK_MK)), as_col(main(BLK_MV)),
         as_scl(iw), as_scl(fw), as_scl(m_new)),
        [sds(*sq), sds(npairs, 1, HEAD_DIM), sds(npairs, HEAD_DIM, 1)])
    gla_s_new, o_gla = _pairs_call(
        _gla_step_kernel, "gla_step",
        (gla_s.reshape(sq), as_col(main(BLK_GQ)), as_col(main(BLK_GK)), as_row(main(BLK_GV)), as_col(alpha)),
        [sds(*sq), sds(npairs, 1, HEAD_DIM)])
    rw_s_new, o_rw = _pairs_call(
        _rwkv_step_kernel, "rwkv_step",
        (rw_s.reshape(sq), as_row(kk), as_row(akk), as_row(wdec), as_row(kt), as_row(rr), as_col(rv)),
        [sds(*sq), sds(npairs, HEAD_DIM, 1)])

    y_ml, y_gla, y_rw = _step_post(h_ml.reshape(n, W_GROUP), main(BLK_MO), o_gla.reshape(n, W_GROUP),
                                   main(BLK_GG), o_rw.reshape(n, W_GROUP), g, bonus, p)
    x_new, ug = _ffn_step(x2d, (y_s5, y_ml, y_gla, y_rw), p, norm_final, ffn_buf[:, 0, :], ffn_buf[:, 1, :],
                          final)
    states = (
        s5_re_new.reshape(n, S5_GROUPS, S5_STATE),
        s5_im_new.reshape(n, S5_GROUPS, S5_STATE),
        ml_c_new.reshape(n, N_HEADS, HEAD_DIM, HEAD_DIM),
        ml_n_new.reshape(n, N_HEADS, HEAD_DIM),
        m_new[:, LANE_MI:LANE_MI + N_HEADS],
        gla_s_new.reshape(n, N_HEADS, HEAD_DIM, HEAD_DIM),
        rw_s_new.reshape(n, N_HEADS, HEAD_DIM, HEAD_DIM),
        proj[:, 0:RWKV_COLS],
        jnp.stack([ffn_buf[:, 1, :], ug], axis=1),
    )
    return x_new, states


def kernel(x_prompt, x_sample, state_s5_re, state_s5_im, state_mlstm_C, state_mlstm_n, state_mlstm_m, state_gla_S, state_rwkv_S, state_rwkv_shift, state_ffn_conv, norm_mix, w_in, s5_lam_re, s5_lam_im, s5_log_dt, s5_b_re, s5_b_im, s5_c_re, s5_c_im, s5_d, s5_w_glu, ml_gate_bias, ml_norm, gla_w_alpha, gla_b_alpha, gla_norm, rw_mu, rw_w0, rw_w2, rw_a0, rw_a2, rw_g2, rw_k_k, rw_k_a, rw_r_k, rw_norm, w_out, norm_ffn, ffn_w_up, ffn_conv_w, ffn_conv_b, ffn_w_down, norm_final):
    prm = dict(norm_mix=norm_mix, w_in=w_in, s5_lam_re=s5_lam_re, s5_lam_im=s5_lam_im, s5_log_dt=s5_log_dt,
               s5_b_re=s5_b_re, s5_b_im=s5_b_im, s5_c_re=s5_c_re, s5_c_im=s5_c_im, s5_d=s5_d,
               s5_w_glu=s5_w_glu, ml_gate_bias=ml_gate_bias, ml_norm=ml_norm, gla_w_alpha=gla_w_alpha,
               gla_b_alpha=gla_b_alpha, gla_norm=gla_norm, rw_mu=rw_mu, rw_w0=rw_w0, rw_w2=rw_w2,
               rw_a0=rw_a0, rw_a2=rw_a2, rw_g2=rw_g2, rw_k_k=rw_k_k, rw_k_a=rw_k_a, rw_r_k=rw_r_k,
               rw_norm=rw_norm, w_out=w_out, norm_ffn=norm_ffn, ffn_w_up=ffn_w_up, ffn_conv_w=ffn_conv_w,
               ffn_conv_b=ffn_conv_b, ffn_w_down=ffn_w_down)
    depth = w_in.shape[0]
    n_seq, t_len, d = x_prompt.shape
    n_smp = x_sample.shape[0]
    assert t_len % 512 == 0 and x_sample.shape[1] == 1
    nf = norm_final[None, :]
    sample_states = (state_s5_re, state_s5_im, state_mlstm_C, state_mlstm_n, state_mlstm_m, state_gla_S,
                     state_rwkv_S, state_rwkv_shift, state_ffn_conv)

    xp = x_prompt.reshape(n_seq * t_len, d)
    xs = x_sample.reshape(n_smp, d)
    p_states, s_states = [], []
    for l in range(depth):
        p = _layer_params(l, prm)
        final = l == depth - 1
        xp, ps = _prompt_layer(xp, p, nf, n_seq, t_len, final)
        xs, ss = _sample_layer(xs, tuple(s[l] for s in sample_states), p, nf, final)
        p_states.append(ps)
        s_states.append(ss)
    new_p = tuple(jnp.stack([st[i] for st in p_states]) for i in range(9))
    new_s = tuple(jnp.stack([st[i] for st in s_states]) for i in range(9))
    return (xp.reshape(n_seq, t_len, d), xs.reshape(n_smp, 1, d)) + new_p + new_s
```

```python
import functools
import math

import jax
import jax.numpy as jnp
from jax import lax
from jax.experimental import pallas as pl
from jax.experimental.pallas import tpu as pltpu

F32 = jnp.float32
BF16 = jnp.bfloat16

LANES = 128
W_GROUP = 256
HEAD_DIM = 64
HEAD_SHIFT = 6
N_HEADS = 4
S5_CH = 16
S5_GROUPS = 16
S5_STATE = 64
S5_WIDTH = S5_GROUPS * S5_STATE
GLA_RANK = 16
GLA_GATE_TEMP = 16.0
RWKV_W_RANK = 32
RWKV_A_RANK = 32
RWKV_G_RANK = 64
RWKV_COLS = 3 * W_GROUP + RWKV_W_RANK + RWKV_A_RANK + RWKV_G_RANK
RWKV_DECAY_SCALE = 0.6065306597126334
CONV_W = 3
EPS = 1e-6
GN_EPS = 64e-5
NEG_BIG = -1e30
EXP_CLAMP = 80.0

SMALL_W = LANES
OFF_SMALL = RWKV_COLS
OFF_MAIN = RWKV_COLS + SMALL_W
D_PROJ = OFF_MAIN + 9 * W_GROUP
LANE_MI = GLA_RANK
LANE_MF = GLA_RANK + N_HEADS
(BLK_U, BLK_MQ, BLK_MK, BLK_MV, BLK_MO, BLK_GQ, BLK_GK, BLK_GV, BLK_GG) = range(
    OFF_MAIN // W_GROUP, OFF_MAIN // W_GROUP + 9)
BLK_SMALL = OFF_SMALL // SMALL_W

CHUNK_ML = 128
CHUNK_RW = 64
SEQS_PER_STEP = 4
VMEM_LIMIT = 56 * 1024 * 1024


def _cparams(*sem):
    return pltpu.CompilerParams(dimension_semantics=sem, vmem_limit_bytes=VMEM_LIMIT)


def _dot(a, b):
    return jnp.dot(a.astype(BF16), b.astype(BF16), preferred_element_type=F32)


def _dot_nt(a, b):
    return lax.dot_general(a.astype(BF16), b.astype(BF16), (((1,), (1,)), ((), ())),
                           preferred_element_type=F32)


def _dot_tn(a, b):
    return lax.dot_general(a.astype(BF16), b.astype(BF16), (((0,), (0,)), ((), ())),
                           preferred_element_type=F32)


def _split3(x):
    hi = x.astype(BF16)
    r1 = x - hi.astype(F32)
    mid = r1.astype(BF16)
    lo = (r1 - mid.astype(F32)).astype(BF16)
    return hi, mid, lo


def _dot_exact_rhs(a01, x):
    hi, mid, lo = _split3(x)
    f = lambda p: jnp.dot(a01, p, preferred_element_type=F32)
    return f(hi) + f(mid) + f(lo)


def _dot_exact_lhs(x, b01):
    hi, mid, lo = _split3(x)
    f = lambda p: jnp.dot(p, b01, preferred_element_type=F32)
    return f(hi) + f(mid) + f(lo)


def _sigmoid(x):
    return 1.0 / (1.0 + jnp.exp(-x))


def _log_sigmoid(x):
    return jnp.minimum(x, 0.0) - jnp.log(1.0 + jnp.exp(-jnp.abs(x)))


def _gelu(x):
    return 0.5 * x * (1.0 + jnp.tanh(math.sqrt(2.0 / math.pi) * (x + 0.044715 * (x * x * x))))


def _iota(shape, dim):
    return lax.broadcasted_iota(jnp.int32, shape, dim)


def _head_ones():
    r = _iota((W_GROUP, W_GROUP), 0) >> HEAD_SHIFT
    c = _iota((W_GROUP, W_GROUP), 1) >> HEAD_SHIFT
    return jnp.where(r == c, 1.0, 0.0).astype(BF16)


def _head_sum(x, ones):
    return _dot_exact_lhs(x, ones)


def _head_lane_mask(h, rows=1):
    lane = _iota((rows, W_GROUP), 1) >> HEAD_SHIFT
    return jnp.where(lane == h, 1.0, 0.0).astype(F32)


def _rmsnorm_rows(x, g):
    ms = jnp.mean(x * x, axis=-1, keepdims=True)
    return x * lax.rsqrt(ms + EPS) * g


def _tril_ones(n, strict=False):
    r = _iota((n, n), 0)
    c = _iota((n, n), 1)
    return jnp.where((r > c) if strict else (r >= c), 1.0, 0.0).astype(BF16)


def _triu_ones(n):
    r = _iota((n, n), 0)
    c = _iota((n, n), 1)
    return jnp.where(r <= c, 1.0, 0.0).astype(BF16)


def _inproj_kernel(x_ref, g_ref, w_ref, wg_ref, proj_ref, gt_ref):
    hn = _rmsnorm_rows(x_ref[...], g_ref[...]).astype(BF16)
    proj_ref[...] = jnp.dot(hn, w_ref[...], preferred_element_type=F32)
    gt_ref[0] = lax.dot_general(wg_ref[...], hn, (((1,), (1,)), ((), ())),
                                preferred_element_type=F32)


def _inproj(x2d, g, w, wg, n_seq, t_len, tm):
    n, d = x2d.shape
    tiles_per_seq = t_len // tm
    return pl.pallas_call(
        _inproj_kernel,
        grid=(n // tm,),
        in_specs=[
            pl.BlockSpec((tm, d), lambda i: (i, 0)),
            pl.BlockSpec((1, d), lambda i: (0, 0)),
            pl.BlockSpec((d, D_PROJ), lambda i: (0, 0)),
            pl.BlockSpec((8, d), lambda i: (0, 0)),
        ],
        out_specs=[
            pl.BlockSpec((tm, D_PROJ), lambda i: (i, 0)),
            pl.BlockSpec((1, 8, tm), lambda i: (i // tiles_per_seq, 0, i % tiles_per_seq)),
        ],
        out_shape=[
            jax.ShapeDtypeStruct((n, D_PROJ), F32),
            jax.ShapeDtypeStruct((n_seq, 8, t_len), F32),
        ],
        compiler_params=_cparams("arbitrary"),
        name="inproj",
    )(x2d, g, w, wg)


def _s5_kernel(u_ref, h0re_ref, h0im_ref, bmat_ref, lre_ref, lim_ref, cmat_ref, d_ref, wglu_ref,
               y_ref, hre_out, him_out, bu_scr, hs_scr, hre_scr, him_scr, *, bp, tt):
    i = pl.program_id(0)

    @pl.when(i == 0)
    def _():
        hre_scr[...] = h0re_ref[...]
        him_scr[...] = h0im_ref[...]

    u = u_ref[...]
    bu_scr[...] = jnp.dot(u.astype(BF16), bmat_ref[...], preferred_element_type=F32)
    lre = lre_ref[...]
    lim = lim_ref[...]

    def step(t, carry):
        hre, him = carry
        r0 = pl.multiple_of(t * bp, bp)
        bre = bu_scr[pl.ds(r0, bp), 0:S5_WIDTH]
        bim = bu_scr[pl.ds(r0, bp), S5_WIDTH:2 * S5_WIDTH]
        nre = lre * hre - lim * him + bre
        nim = lre * him + lim * hre + bim
        hs_scr[pl.ds(r0, bp), 0:S5_WIDTH] = nre
        hs_scr[pl.ds(r0, bp), S5_WIDTH:2 * S5_WIDTH] = nim
        return nre, nim

    carry = (hre_scr[...], him_scr[...])
    if tt == 1:
        hre, him = step(0, carry)
    else:
        hre, him = lax.fori_loop(0, tt, step, carry)
    hre_scr[...] = hre
    him_scr[...] = him
    hre_out[...] = hre
    him_out[...] = him
    y = jnp.dot(hs_scr[...].astype(BF16), cmat_ref[...], preferred_element_type=F32) + d_ref[...] * u
    z = _gelu(y)
    y_ref[...] = (z * _sigmoid(_dot(z, wglu_ref[...]))).astype(y_ref.dtype)


def _s5(u_tb, h0re, h0im, sp, bp, t_len, tt):
    n = u_tb.shape[0]
    full = lambda shape: pl.BlockSpec(shape, lambda i: (0,) * len(shape))
    return pl.pallas_call(
        functools.partial(_s5_kernel, bp=bp, tt=tt),
        grid=(t_len // tt,),
        in_specs=[
            pl.BlockSpec((tt * bp, W_GROUP), lambda i: (i, 0)),
            full((bp, S5_WIDTH)), full((bp, S5_WIDTH)),
            full((W_GROUP, 2 * S5_WIDTH)),
            full((bp, S5_WIDTH)), full((bp, S5_WIDTH)),
            full((2 * S5_WIDTH, W_GROUP)),
            full((1, W_GROUP)),
            full((W_GROUP, W_GROUP)),
        ],
        out_specs=[
            pl.BlockSpec((tt * bp, W_GROUP), lambda i: (i, 0)),
            full((bp, S5_WIDTH)), full((bp, S5_WIDTH)),
        ],
        out_shape=[
            jax.ShapeDtypeStruct((n, W_GROUP), BF16),
            jax.ShapeDtypeStruct((bp, S5_WIDTH), F32),
            jax.ShapeDtypeStruct((bp, S5_WIDTH), F32),
        ],
        scratch_shapes=[
            pltpu.VMEM((tt * bp, 2 * S5_WIDTH), F32),
            pltpu.VMEM((tt * bp, 2 * S5_WIDTH), F32),
            pltpu.VMEM((bp, S5_WIDTH), F32),
            pltpu.VMEM((bp, S5_WIDTH), F32),
        ],
        compiler_params=_cparams("arbitrary"),
        name="s5_scan",
    )(u_tb, h0re, h0im, sp["bmat"], jnp.broadcast_to(sp["lre"], (bp, S5_WIDTH)),
      jnp.broadcast_to(sp["lim"], (bp, S5_WIDTH)), sp["cmat"], sp["d"], sp["wglu"])


def _mlstm_kernel(q_ref, k_ref, v_ref, og_ref, sm_ref, gt_ref, brow_ref, bcol_ref, norm_ref,
                  y_ref, c_out, n_out, m_out, c_scr, n_scr, m_scr, *, L, nc):
    c = pl.program_id(1)

    @pl.when(c == 0)
    def _():
        c_scr[...] = jnp.zeros_like(c_scr)
        n_scr[...] = jnp.zeros_like(n_scr)
        m_scr[...] = jnp.zeros_like(m_scr)

    q = q_ref[...]
    k = k_ref[...] * (HEAD_DIM ** -0.5)
    v = v_ref[...]
    pre = sm_ref[...] + brow_ref[...]
    lf_col = _log_sigmoid(pre)
    gtb = gt_ref[0] + bcol_ref[...]
    lf_row = _log_sigmoid(gtb)
    bc_col = _dot_exact_rhs(_tril_ones(L), lf_col)
    bc_row = _dot_exact_lhs(lf_row, _triu_ones(L))
    C = c_scr[...]
    n = n_scr[0:1, :]
    qC = _dot_nt(q, C)
    qn = q * n
    causal = _iota((L, L), 0) >= _iota((L, L), 1)
    ones = _head_ones()

    hc = jnp.zeros((L, W_GROUP), F32)
    wl_all = jnp.zeros((L, W_GROUP), F32)
    w0l_all = jnp.zeros((1, W_GROUP), F32)
    for h in range(N_HEADS):
        mask = _head_lane_mask(h)
        bcc = bc_col[:, LANE_MF + h:LANE_MF + h + 1]
        bcr = bc_row[N_HEADS + h:N_HEADS + h + 1, :]
        i_row = gtb[h:h + 1, :]
        i_col = pre[:, LANE_MI + h:LANE_MI + h + 1]
        m0 = m_scr[h:h + 1, 0:1]
        dmat = jnp.where(causal, bcc - bcr + i_row, NEG_BIG)
        g = bcc + m0
        m_row = jnp.maximum(g, jnp.max(dmat, axis=1, keepdims=True))
        w = jnp.exp(dmat - m_row)
        w0 = jnp.exp(g - m_row)
        ws = w * _dot_nt(q * mask, k)
        num = w0 * qC + _dot(ws, v * mask)
        den = (w0 * jnp.sum(qn * mask, axis=1, keepdims=True)
               + jnp.sum(ws, axis=1, keepdims=True))
        hc = hc + mask * (num / jnp.maximum(jnp.abs(den), jnp.exp(-m_row)))
        m_new = m_row[L - 1:L, :]
        b_last = bcc[L - 1:L, :]
        wl_all = wl_all + mask * jnp.exp(b_last - bcc + i_col - m_new)
        w0l_all = w0l_all + mask * jnp.exp(b_last + m0 - m_new)
        m_scr[h:h + 1, :] = jnp.broadcast_to(m_new, (1, LANES))

    kw = k * wl_all
    c_new = w0l_all * C + ones.astype(F32) * _dot_tn(v, kw)
    n_new = w0l_all * n + jnp.sum(kw, axis=0, keepdims=True)
    c_scr[...] = c_new
    n_scr[0:1, :] = n_new

    ms = _head_sum(hc * hc, ones) * (1.0 / HEAD_DIM)
    y = hc * lax.rsqrt(ms + EPS) * norm_ref[...] * _sigmoid(og_ref[...])
    y_ref[...] = y.astype(y_ref.dtype)

    @pl.when(c == nc - 1)
    def _():
        c_out[0] = c_new
        n_out[0] = jnp.broadcast_to(n_new, (8, W_GROUP))
        m_out[0] = m_scr[...]


def _proj_spec(L, nc, blk, width=W_GROUP):
    return pl.BlockSpec((L, width), lambda b, c: (b * nc + c, blk))


def _mlstm(proj, gt, p, n_seq, t_len):
    L = CHUNK_ML
    nc = t_len // L
    n = proj.shape[0]
    const = lambda shape: pl.BlockSpec(shape, lambda b, c: (0,) * len(shape))
    return pl.pallas_call(
        functools.partial(_mlstm_kernel, L=L, nc=nc),
        grid=(n_seq, nc),
        in_specs=[
            _proj_spec(L, nc, BLK_MQ), _proj_spec(L, nc, BLK_MK), _proj_spec(L, nc, BLK_MV),
            _proj_spec(L, nc, BLK_MO), _proj_spec(L, nc, BLK_SMALL, SMALL_W),
            pl.BlockSpec((1, 8, L), lambda b, c: (b, 0, c)),
            const((1, SMALL_W)), const((8, L)), const((1, W_GROUP)),
        ],
        out_specs=[
            pl.BlockSpec((L, W_GROUP), lambda b, c: (b * nc + c, 0)),
            pl.BlockSpec((1, W_GROUP, W_GROUP), lambda b, c: (b, 0, 0)),
            pl.BlockSpec((1, 8, W_GROUP), lambda b, c: (b, 0, 0)),
            pl.BlockSpec((1, 8, LANES), lambda b, c: (b, 0, 0)),
        ],
        out_shape=[
            jax.ShapeDtypeStruct((n, W_GROUP), BF16),
            jax.ShapeDtypeStruct((n_seq, W_GROUP, W_GROUP), F32),
            jax.ShapeDtypeStruct((n_seq, 8, W_GROUP), F32),
            jax.ShapeDtypeStruct((n_seq, 8, LANES), F32),
        ],
        scratch_shapes=[
            pltpu.VMEM((W_GROUP, W_GROUP), F32),
            pltpu.VMEM((8, W_GROUP), F32),
            pltpu.VMEM((8, LANES), F32),
        ],
        compiler_params=_cparams("arbitrary", "arbitrary"),
        name="mlstm_chunk",
    )(proj, proj, proj, proj, proj, gt, p["ml_brow"], p["ml_bcol"], p["ml_norm"])


def _gla_kernel(q_ref, k_ref, v_ref, gg_ref, sm_ref, wa_ref, ba_ref, norm_ref,
                y_ref, s_out, s_scr, *, L, nc):
    c = pl.program_id(1)

    @pl.when(c == 0)
    def _():
        s_scr[...] = jnp.zeros_like(s_scr)

    q = q_ref[...] * (HEAD_DIM ** -0.5)
    k = k_ref[...]
    v = v_ref[...]
    la = _log_sigmoid(_dot(sm_ref[...], wa_ref[...]) + ba_ref[...]) * (1.0 / GLA_GATE_TEMP)
    bc = _dot_exact_rhs(_tril_ones(L), la)
    qs = q * jnp.exp(bc)
    kh = k * jnp.exp(jnp.minimum(-bc, EXP_CLAMP))
    ST = s_scr[...]
    o = _dot_nt(qs, ST)
    causal = _iota((L, L), 0) >= _iota((L, L), 1)
    for h in range(N_HEADS):
        mask = _head_lane_mask(h)
        att = jnp.where(causal, _dot_nt(qs * mask, kh), 0.0)
        o = o + _dot(att, v * mask)
    b_last = bc[L - 1:L, :]
    kbar = k * jnp.exp(b_last - bc)
    ones = _head_ones()
    s_new = ST * jnp.exp(b_last) + ones.astype(F32) * _dot_tn(v, kbar)
    s_scr[...] = s_new

    ms = _head_sum(o * o, ones) * (1.0 / HEAD_DIM)
    gg = gg_ref[...]
    y = o * lax.rsqrt(ms + EPS) * norm_ref[...] * (gg * _sigmoid(gg))
    y_ref[...] = y.astype(y_ref.dtype)

    @pl.when(c == nc - 1)
    def _():
        s_out[0] = s_new


def _gla(proj, p, n_seq, t_len):
    L = CHUNK_ML
    nc = t_len // L
    n = proj.shape[0]
    const = lambda shape: pl.BlockSpec(shape, lambda b, c: (0,) * len(shape))
    return pl.pallas_call(
        functools.partial(_gla_kernel, L=L, nc=nc),
        grid=(n_seq, nc),
        in_specs=[
            _proj_spec(L, nc, BLK_GQ), _proj_spec(L, nc, BLK_GK), _proj_spec(L, nc, BLK_GV),
            _proj_spec(L, nc, BLK_GG), _proj_spec(L, nc, BLK_SMALL, SMALL_W),
            const((SMALL_W, W_GROUP)), const((1, W_GROUP)), const((1, W_GROUP)),
        ],
        out_specs=[
            pl.BlockSpec((L, W_GROUP), lambda b, c: (b * nc + c, 0)),
            pl.BlockSpec((1, W_GROUP, W_GROUP), lambda b, c: (b, 0, 0)),
        ],
        out_shape=[
            jax.ShapeDtypeStruct((n, W_GROUP), BF16),
            jax.ShapeDtypeStruct((n_seq, W_GROUP, W_GROUP), F32),
        ],
        scratch_shapes=[pltpu.VMEM((W_GROUP, W_GROUP), F32)],
        compiler_params=_cparams("arbitrary", "arbitrary"),
        name="gla_chunk",
    )(proj, proj, proj, proj, proj, p["gla_wa"], p["gla_ba"], p["gla_norm"])


def _rwkv_vectors(rc, prev, p_mu, p_w0, p_a0, p_kk, p_ka, p_rk, w2, a2, g2, ones):
    xm = rc + p_mu * (prev - rc)
    rr = xm[:, 0:W_GROUP]
    rk = xm[:, W_GROUP:2 * W_GROUP]
    rv = xm[:, 2 * W_GROUP:3 * W_GROUP]
    tail = xm[:, 3 * W_GROUP:RWKV_COLS]
    lw = -RWKV_DECAY_SCALE * _sigmoid(p_w0 + _dot(jnp.tanh(tail), w2))
    a = _sigmoid(p_a0 + _dot(tail, a2))
    g = _dot(_sigmoid(tail), g2)
    kk = rk * p_kk
    kk = kk * lax.rsqrt(jnp.maximum(_head_sum(kk * kk, ones), 1e-24))
    kt = rk * (1.0 + (a - 1.0) * p_ka)
    bonus = _head_sum(rr * kt * p_rk, ones) * rv
    return rr, lw, kt, rv, kk, a, g, bonus


def _head_layernorm(o, g, ones):
    mu = _head_sum(o, ones) * (1.0 / HEAD_DIM)
    oc = o - mu
    var = _head_sum(oc * oc, ones) * (1.0 / HEAD_DIM)
    return oc * lax.rsqrt(var + GN_EPS) * g


def _stack_heads(x):
    return jnp.concatenate([x * _head_lane_mask(h) for h in range(N_HEADS)], axis=0)


def _each(f, *seqs):
    return [f(*args) for args in zip(*seqs)]


def _unit_lower_inverse(a_list):
    n = a_list[0].shape[0]
    r = _iota((n, n), 0)
    c = _iota((n, n), 1)
    eye = jnp.where(r == c, 1.0, 0.0).astype(F32)
    in16 = (r >> 4) == (c >> 4)
    in32 = ((r >> 5) == (c >> 5)) & ((r >> 4) > (c >> 4))
    in64 = ((r >> 6) == (c >> 6)) & ((r >> 5) > (c >> 5))
    x16 = _each(lambda a: jnp.where(in16, -a, 0.0), a_list)
    x2 = _each(lambda x: _dot(x, x), x16)
    x4 = _each(lambda x: _dot(x, x), x2)
    x8 = _each(lambda x: _dot(x, x), x4)
    t = _each(lambda x: eye + x, x16)
    t = _each(lambda t_, x: t_ + _dot(t_, x), t, x2)
    t = _each(lambda t_, x: t_ + _dot(t_, x), t, x4)
    t = _each(lambda t_, x: t_ + _dot(t_, x), t, x8)
    for sel in (in32, in64):
        w = _each(lambda a, t_: _dot(jnp.where(sel, a, 0.0), t_), a_list, t)
        t = _each(lambda t_, w_: t_ - _dot(t_, w_), t, w)
    return t


def _rwkv_chunks(rcs, prev_rows, states, prm, ones, L):
    (mu, w0, a0, pkk, pka, prk, w2, a2, g2, norm) = prm
    first_row = _iota((L, RWKV_COLS), 0) == 0
    prevs = _each(lambda rc, pr: jnp.where(first_row, pr, pltpu.roll(rc, 1, 0)), rcs, prev_rows)
    vecs = _each(lambda rc, pv: _rwkv_vectors(rc, pv, mu, w0, a0, pkk, pka, prk, w2, a2, g2, ones),
                 rcs, prevs)
    r, lw, kx, v, kk, a, g, bonus = [list(col) for col in zip(*vecs)]

    tril = _tril_ones(L)
    lc = _each(lambda x: _dot_exact_rhs(tril, x), lw)
    akk = _each(lambda a_, k_: a_ * k_, a, kk)
    e_neg = _each(lambda x: jnp.exp(-x), lc)
    l_last = _each(lambda x: x[L - 1:L, :], lc)
    e_end = _each(lambda ll, x: jnp.exp(ll - x), l_last, lc)
    Ks = _each(lambda k_, c_, w_: _stack_heads(k_ * jnp.exp(c_ - w_)), kk, lc, lw)
    As = _each(lambda x, e: _stack_heads(x * e), akk, e_neg)
    Khs = _each(lambda x, e: _stack_heads(x * e), kx, e_neg)
    Rs = _each(lambda x, c_: _stack_heads(x * jnp.exp(c_)), r, lc)
    Vs = _each(_stack_heads, v)
    Kbs = _each(lambda x, e: _stack_heads(x * e), kx, e_end)
    Abs = _each(lambda x, e: _stack_heads(x * e), akk, e_end)

    n4 = N_HEADS * L
    strict = _iota((n4, n4), 0) > _iota((n4, n4), 1)
    incl = _iota((n4, n4), 0) >= _iota((n4, n4), 1)
    a_ua = _each(lambda x, y_: jnp.where(strict, _dot_nt(x, y_), 0.0), Ks, As)
    a_vk = _each(lambda x, y_: jnp.where(strict, _dot_nt(x, y_), 0.0), Ks, Khs)
    b_rk = _each(lambda x, y_: jnp.where(incl, _dot_nt(x, y_), 0.0), Rs, Khs)
    b_ra = _each(lambda x, y_: jnp.where(incl, _dot_nt(x, y_), 0.0), Rs, As)
    T = _unit_lower_inverse(a_ua)

    rhs = _each(lambda k_, s_, a_, v_: _dot_nt(k_, s_) + _dot(a_, v_), Ks, states, a_vk, Vs)
    Us = _each(_dot, T, rhs)
    Os = _each(lambda r_, s_, b_, v_: _dot_nt(r_, s_) + _dot(b_, v_), Rs, states, b_rk, Vs)
    Os = _each(lambda o_, b_, u_: o_ - _dot(b_, u_), Os, b_ra, Us)
    o = _each(lambda x: x[0:L] + x[L:2 * L] + x[2 * L:3 * L] + x[3 * L:4 * L], Os)
    s_new = _each(lambda s_, ll, v_, k_: s_ * jnp.exp(ll) + _dot_tn(v_, k_), states, l_last, Vs, Kbs)
    s_new = _each(lambda s_, u_, a_: s_ - _dot_tn(u_, a_), s_new, Us, Abs)
    y = _each(lambda o_, b_, g_: (_head_layernorm(o_, norm, ones) + b_) * g_, o, bonus, g)
    return y, s_new


def _rwkv_kernel(rc_ref, mu_ref, w0_ref, a0_ref, kk_ref, ka_ref, rk_ref, w2_ref, a2_ref, g2_ref,
                 norm_ref, y_ref, s_out, shift_out, s_scr, prev_scr, *, L, nc, G):
    c = pl.program_id(1)

    @pl.when(c == 0)
    def _():
        s_scr[...] = jnp.zeros_like(s_scr)
        prev_scr[...] = jnp.zeros_like(prev_scr)

    ones = _head_ones()
    prm = (mu_ref[...], w0_ref[...], a0_ref[...], kk_ref[...], ka_ref[...], rk_ref[...],
           w2_ref[...], a2_ref[...], g2_ref[...], norm_ref[...])
    rcs = [rc_ref[gi] for gi in range(G)]
    ys, s_new = _rwkv_chunks(rcs, [prev_scr[gi, 0:1, :] for gi in range(G)],
                             [s_scr[gi] for gi in range(G)], prm, ones, L)
    for gi in range(G):
        prev_scr[gi, 0:1, :] = rcs[gi][L - 1:L, :]
        s_scr[gi] = s_new[gi]
        y_ref[gi] = ys[gi].astype(y_ref.dtype)

    @pl.when(c == nc - 1)
    def _():
        s_out[...] = s_scr[...]
        shift_out[...] = jnp.broadcast_to(prev_scr[:, 0:1, :], shift_out.shape)


def _rwkv(proj, p, n_seq, t_len):
    L = CHUNK_RW
    G = SEQS_PER_STEP
    nc = t_len // L
    const = lambda shape: pl.BlockSpec(shape, lambda b, c: (0,) * len(shape))
    row = const((1, W_GROUP))
    pad = const((LANES, W_GROUP))
    y, s_new, shift = pl.pallas_call(
        functools.partial(_rwkv_kernel, L=L, nc=nc, G=G),
        grid=(n_seq // G, nc),
        in_specs=[
            pl.BlockSpec((G, L, RWKV_COLS), lambda b, c: (b, c, 0)),
            const((1, RWKV_COLS)), row, row, row, row, row, pad, pad, pad, row,
        ],
        out_specs=[
            pl.BlockSpec((G, L, W_GROUP), lambda b, c: (b, c, 0)),
            pl.BlockSpec((G, W_GROUP, W_GROUP), lambda b, c: (b, 0, 0)),
            pl.BlockSpec((G, 8, RWKV_COLS), lambda b, c: (b, 0, 0)),
        ],
        out_shape=[
            jax.ShapeDtypeStruct((n_seq, t_len, W_GROUP), BF16),
            jax.ShapeDtypeStruct((n_seq, W_GROUP, W_GROUP), F32),
            jax.ShapeDtypeStruct((n_seq, 8, RWKV_COLS), F32),
        ],
        scratch_shapes=[
            pltpu.VMEM((G, W_GROUP, W_GROUP), F32),
            pltpu.VMEM((G, 8, RWKV_COLS), F32),
        ],
        compiler_params=_cparams("arbitrary", "arbitrary"),
        name="rwkv_chunk",
    )(proj.reshape(n_seq, t_len, D_PROJ), p["rw_mu"], p["rw_w0"], p["rw_a0"], p["rw_kk"], p["rw_ka"],
      p["rw_rk"], p["rw_w2"], p["rw_a2"], p["rw_g2"], p["rw_norm"])
    return y.reshape(n_seq * t_len, W_GROUP), s_new, shift


FF_CHUNK = 256


def _mix_residual(x_ref, ys, wout_ref):
    acc = x_ref[...]
    for j, y in enumerate(ys):
        acc = acc + jnp.dot(y.astype(BF16), wout_ref[j * W_GROUP:(j + 1) * W_GROUP, :],
                            preferred_element_type=F32)
    return acc


def _ffn_body(x1, nrm_ref, wup_ref, cw_ref, cb_ref, wdn_ref, prev_rows, d_ff, act_scr):
    h2 = _rmsnorm_rows(x1, nrm_ref[...]).astype(BF16)
    for j in range(d_ff // FF_CHUNK):
        lo, hi = j * FF_CHUNK, (j + 1) * FF_CHUNK
        ug = jnp.dot(h2, wup_ref[:, lo:hi], preferred_element_type=F32)
        uv = jnp.dot(h2, wup_ref[:, d_ff + lo:d_ff + hi], preferred_element_type=F32)
        p2, p1 = prev_rows(j, ug)
        conv = (cb_ref[:, lo:hi] + cw_ref[0:1, lo:hi] * p2 + cw_ref[1:2, lo:hi] * p1
                + cw_ref[2:3, lo:hi] * ug)
        act_scr[:, lo:hi] = (_gelu(conv) * uv).astype(BF16)
    return x1 + jnp.dot(act_scr[...], wdn_ref[...], preferred_element_type=F32)


def _ffn_seq_kernel(x_ref, y0_ref, y1_ref, y2_ref, y3_ref, wout_ref, nrm_ref, wup_ref, cw_ref, cb_ref,
                    wdn_ref, nf_ref, o_ref, buf_out, carry_scr, act_scr, *, tm, d_ff, final):
    t = pl.program_id(1)

    @pl.when(t == 0)
    def _():
        carry_scr[...] = jnp.zeros_like(carry_scr)

    x1 = _mix_residual(x_ref, (y0_ref[...], y1_ref[...], y2_ref[...], y3_ref[...]), wout_ref)
    row = _iota((tm, FF_CHUNK), 0)

    def prev_rows(j, ug):
        lo, hi = j * FF_CHUNK, (j + 1) * FF_CHUNK
        c0 = carry_scr[0:1, lo:hi]
        c1 = carry_scr[1:2, lo:hi]
        p1 = jnp.where(row == 0, c1, pltpu.roll(ug, 1, 0))
        p2 = jnp.where(row == 0, c0, jnp.where(row == 1, c1, pltpu.roll(ug, 2, 0)))
        carry_scr[0:2, lo:hi] = ug[tm - 2:tm, :]
        return p2, p1

    out = _ffn_body(x1, nrm_ref, wup_ref, cw_ref, cb_ref, wdn_ref, prev_rows, d_ff, act_scr)
    if final:
        out = _rmsnorm_rows(out, nf_ref[...])
    o_ref[...] = out
    buf_out[0] = carry_scr[...]


def _ffn_step_kernel(x_ref, y0_ref, y1_ref, y2_ref, y3_ref, wout_ref, nrm_ref, wup_ref, cw_ref, cb_ref,
                     wdn_ref, nf_ref, p2_ref, p1_ref, o_ref, ug_out, act_scr, *, d_ff, final):
    x1 = _mix_residual(x_ref, (y0_ref[...], y1_ref[...], y2_ref[...], y3_ref[...]), wout_ref)

    def prev_rows(j, ug):
        lo, hi = j * FF_CHUNK, (j + 1) * FF_CHUNK
        ug_out[:, lo:hi] = ug
        return p2_ref[:, lo:hi], p1_ref[:, lo:hi]

    out = _ffn_body(x1, nrm_ref, wup_ref, cw_ref, cb_ref, wdn_ref, prev_rows, d_ff, act_scr)
    if final:
        out = _rmsnorm_rows(out, nf_ref[...])
    o_ref[...] = out


def _ffn_seq(x2d, ys, p, norm_final, n_seq, t_len, tm, final):
    n, d = x2d.shape
    d_ff = p["ffn_wdn"].shape[1]
    lay = p["layer"]
    nt = t_len // tm
    rows = lambda w: pl.BlockSpec((tm, w), lambda b, t: (b * nt + t, 0))
    const = lambda shape: pl.BlockSpec(shape, lambda b, t: (0,) * len(shape))
    layered = lambda shape: pl.BlockSpec((None,) + shape, lambda b, t: (lay, 0, 0))
    return pl.pallas_call(
        functools.partial(_ffn_seq_kernel, tm=tm, d_ff=d_ff, final=final),
        grid=(n_seq, nt),
        in_specs=[rows(d)] + [rows(W_GROUP)] * 4 + [
            layered((d, d)), const((1, d)), layered((d, 2 * d_ff)), const((CONV_W, d_ff)),
            const((1, d_ff)), layered((d_ff, d)), const((1, d))],
        out_specs=[rows(d), pl.BlockSpec((1, CONV_W - 1, d_ff), lambda b, t: (b, 0, 0))],
        out_shape=[jax.ShapeDtypeStruct((n, d), F32),
                   jax.ShapeDtypeStruct((n_seq, CONV_W - 1, d_ff), F32)],
        scratch_shapes=[pltpu.VMEM((CONV_W - 1, d_ff), F32), pltpu.VMEM((tm, d_ff), BF16)],
        compiler_params=_cparams("arbitrary", "arbitrary"),
        name="wout_ffn_seq",
    )(x2d, *ys, p["w_out"], p["norm_ffn"], p["ffn_wup"], p["ffn_cw"], p["ffn_cb"], p["ffn_wdn"],
      norm_final)


def _ffn_step(x2d, ys, p, norm_final, prev2, prev1, final):
    n, d = x2d.shape
    d_ff = p["ffn_wdn"].shape[1]
    lay = p["layer"]
    full = lambda shape: pl.BlockSpec(shape, lambda i: (0,) * len(shape))
    layered = lambda shape: pl.BlockSpec((None,) + shape, lambda i: (lay, 0, 0))
    return pl.pallas_call(
        functools.partial(_ffn_step_kernel, d_ff=d_ff, final=final),
        grid=(1,),
        in_specs=[full((n, d))] + [full((n, W_GROUP))] * 4 + [
            layered((d, d)), full((1, d)), layered((d, 2 * d_ff)), full((CONV_W, d_ff)),
            full((1, d_ff)), layered((d_ff, d)), full((1, d)), full((n, d_ff)), full((n, d_ff))],
        out_specs=[full((n, d)), full((n, d_ff))],
        out_shape=[jax.ShapeDtypeStruct((n, d), F32), jax.ShapeDtypeStruct((n, d_ff), F32)],
        scratch_shapes=[pltpu.VMEM((n, d_ff), BF16)],
        compiler_params=_cparams("arbitrary"),
        name="wout_ffn_step",
    )(x2d, *ys, p["w_out"], p["norm_ffn"], p["ffn_wup"], p["ffn_cw"], p["ffn_cb"], p["ffn_wdn"],
      norm_final, prev2, prev1)


def _step_vec_kernel(proj_ref, shift_ref, m_ref, brow_ref, wa_ref, ba_ref, mu_ref, w0_ref, a0_ref,
                     kk_ref, ka_ref, rk_ref, w2_ref, a2_ref, g2_ref,
                     pack_out, m_out, g_out, bonus_out):
    ones = _head_ones()
    sm = proj_ref[:, OFF_SMALL:OFF_SMALL + SMALL_W]
    pre = sm + brow_ref[...]
    logf = pltpu.roll(_log_sigmoid(pre), LANES - N_HEADS, 1)
    m_old = m_ref[...]
    m_new = jnp.maximum(logf + m_old, pre)
    m_out[...] = m_new
    lane = _iota((SMALL_W, W_GROUP), 0)
    spread_mat = jnp.where(lane == LANE_MI + (_iota((SMALL_W, W_GROUP), 1) >> HEAD_SHIFT), 1.0, 0.0).astype(BF16)
    gate_lanes = (_iota(pre.shape, 1) >= LANE_MI) & (_iota(pre.shape, 1) < LANE_MI + N_HEADS)
    spread = lambda x: _dot_exact_lhs(jnp.where(gate_lanes, x, 0.0), spread_mat)
    la = _log_sigmoid(_dot(sm, wa_ref[...]) + ba_ref[...]) * (1.0 / GLA_GATE_TEMP)
    rc = proj_ref[:, 0:RWKV_COLS]
    rr, lw, kt, rv, kk, a, g, bonus = _rwkv_vectors(
        rc, shift_ref[...], mu_ref[...], w0_ref[...], a0_ref[...], kk_ref[...], ka_ref[...],
        rk_ref[...], w2_ref[...], a2_ref[...], g2_ref[...], ones)
    main = lambda blk: proj_ref[:, blk * W_GROUP:(blk + 1) * W_GROUP]
    rows = {
        ROW_MQ: main(BLK_MQ), ROW_MK: main(BLK_MK), ROW_MV: main(BLK_MV),
        ROW_GQ: main(BLK_GQ), ROW_GK: main(BLK_GK), ROW_GV: main(BLK_GV), ROW_GA: jnp.exp(la),
        ROW_KK: kk, ROW_AKK: a * kk, ROW_W: jnp.exp(lw), ROW_KT: kt, ROW_RR: rr, ROW_RV: rv,
        ROW_IW: spread(jnp.exp(pre - m_new)), ROW_FW: spread(jnp.exp(logf + m_old - m_new)),
        ROW_M: spread(m_new),
    }
    for i, val in rows.items():
        pack_out[i] = val
    g_out[...] = g
    bonus_out[...] = bonus


def _step_vectors(proj, shift, m_pad, p):
    n = proj.shape[0]
    full = lambda a: pl.BlockSpec(a.shape, lambda i: (0,) * a.ndim)
    args = (proj, shift, m_pad, p["ml_brow"], p["gla_wa"], p["gla_ba"], p["rw_mu"], p["rw_w0"],
            p["rw_a0"], p["rw_kk"], p["rw_ka"], p["rw_rk"], p["rw_w2"], p["rw_a2"], p["rw_g2"])
    wide = jax.ShapeDtypeStruct((n, W_GROUP), F32)
    outs = [jax.ShapeDtypeStruct((N_PACK, n, W_GROUP), F32), jax.ShapeDtypeStruct((n, SMALL_W), F32),
            wide, wide]
    return pl.pallas_call(
        _step_vec_kernel,
        grid=(1,),
        in_specs=[full(a) for a in args],
        out_specs=[pl.BlockSpec(o.shape, lambda i, nd=len(o.shape): (0,) * nd) for o in outs],
        out_shape=outs,
        compiler_params=_cparams("arbitrary"),
        name="step_vectors",
    )(*args)


PAIRS_PER_BLOCK = 32
(ROW_MQ, ROW_MK, ROW_MV, ROW_GQ, ROW_GK, ROW_GV, ROW_GA, ROW_KK, ROW_AKK, ROW_W, ROW_KT, ROW_RR, ROW_RV,
 ROW_IW, ROW_FW, ROW_M) = range(16)
N_PACK = 16
(OROW_ML, OROW_GLA, OROW_RW, OROW_N) = range(4)
N_OPACK = 8


def _pairs_step_kernel(pack_ref, c_ref, n_ref, gs_ref, rs_ref, c_out, gs_out, rs_out, opack_out):
    eye = jnp.where(_iota((HEAD_DIM, HEAD_DIM), 0) == _iota((HEAD_DIM, HEAD_DIM), 1), 1.0, 0.0).astype(F32)
    row = lambda i: pack_ref[:, i:i + 1, :]
    col = lambda x: jnp.sum(x * eye, axis=2, keepdims=True)
    to_row = lambda x: jnp.sum(x * eye, axis=1, keepdims=True)

    k = row(ROW_MK) * (HEAD_DIM ** -0.5)
    q = row(ROW_MQ)
    iw = row(ROW_IW)
    fw = row(ROW_FW)
    c_new = fw * c_ref[...] + iw * (col(row(ROW_MV)) * k)
    n_new = fw * n_ref[...] + iw * k
    num = jnp.sum(c_new * q, axis=2, keepdims=True)
    den = jnp.sum(n_new * q, axis=2, keepdims=True)
    m_new = row(ROW_M)[:, :, 0:1]
    c_out[...] = c_new
    opack_out[:, OROW_ML:OROW_ML + 1, :] = to_row(num / jnp.maximum(jnp.abs(den), jnp.exp(-m_new)))
    opack_out[:, OROW_N:OROW_N + 1, :] = n_new

    s_new = col(row(ROW_GA)) * gs_ref[...] + col(row(ROW_GK)) * row(ROW_GV)
    gs_out[...] = s_new
    opack_out[:, OROW_GLA:OROW_GLA + 1, :] = jnp.sum(
        (col(row(ROW_GQ)) * (HEAD_DIM ** -0.5)) * s_new, axis=1, keepdims=True)

    s = rs_ref[...]
    sk = jnp.sum(s * row(ROW_KK), axis=2, keepdims=True)
    s_new = s * row(ROW_W) - sk * row(ROW_AKK) + col(row(ROW_RV)) * row(ROW_KT)
    rs_out[...] = s_new
    opack_out[:, OROW_RW:OROW_RW + 1, :] = to_row(jnp.sum(s_new * row(ROW_RR), axis=2, keepdims=True))
    opack_out[:, OROW_N + 1:N_OPACK, :] = jnp.zeros((pack_ref.shape[0], N_OPACK - OROW_N - 1, HEAD_DIM), F32)


def _pairs_step(pack, ml_c, ml_n, gla_s, rw_s, layer):
    npairs = pack.shape[0]
    P = min(PAIRS_PER_BLOCK, npairs)
    nblk = npairs // P
    here = lambda shape: pl.BlockSpec((P,) + shape, lambda i: (i, 0, 0))
    at_layer = lambda shape: pl.BlockSpec((P,) + shape, lambda i: (layer * nblk + i, 0, 0))
    sq = (HEAD_DIM, HEAD_DIM)
    state = jax.ShapeDtypeStruct((npairs,) + sq, F32)
    return pl.pallas_call(
        _pairs_step_kernel,
        grid=(nblk,),
        in_specs=[here((N_PACK, HEAD_DIM)), at_layer(sq), at_layer((1, HEAD_DIM)), at_layer(sq), at_layer(sq)],
        out_specs=[here(sq), here(sq), here(sq), here((N_OPACK, HEAD_DIM))],
        out_shape=[state, state, state, jax.ShapeDtypeStruct((npairs, N_OPACK, HEAD_DIM), F32)],
        compiler_params=_cparams("arbitrary"),
        name="pairs_step",
    )(pack, ml_c, ml_n, gla_s, rw_s)


def _step_post_kernel(hml_ref, mo_ref, ogla_ref, gg_ref, orw_ref, g_ref, bonus_ref,
                      mln_ref, gln_ref, rwn_ref, yml_out, ygla_out, yrw_out):
    ones = _head_ones()
    hml = hml_ref[...]
    ms = _head_sum(hml * hml, ones) * (1.0 / HEAD_DIM)
    yml_out[...] = (hml * lax.rsqrt(ms + EPS) * mln_ref[...] * _sigmoid(mo_ref[...])).astype(BF16)
    og = ogla_ref[...]
    ms = _head_sum(og * og, ones) * (1.0 / HEAD_DIM)
    gg = gg_ref[...]
    ygla_out[...] = (og * lax.rsqrt(ms + EPS) * gln_ref[...] * (gg * _sigmoid(gg))).astype(BF16)
    yrw_out[...] = ((_head_layernorm(orw_ref[...], rwn_ref[...], ones) + bonus_ref[...])
                    * g_ref[...]).astype(BF16)


def _step_post(hml, mo, ogla, gg, orw, g, bonus, p):
    n = hml.shape[0]
    args = (hml, mo, ogla, gg, orw, g, bonus, p["ml_norm"], p["gla_norm"], p["rw_norm"])
    full = lambda a: pl.BlockSpec(a.shape, lambda i: (0,) * a.ndim)
    out = jax.ShapeDtypeStruct((n, W_GROUP), BF16)
    return pl.pallas_call(
        _step_post_kernel,
        grid=(1,),
        in_specs=[full(a) for a in args],
        out_specs=[pl.BlockSpec((n, W_GROUP), lambda i: (0, 0))] * 3,
        out_shape=[out] * 3,
        compiler_params=_cparams("arbitrary"),
        name="step_post",
    )(*args)


def _layer_params(l, prm):
    p = {}
    w_in = prm["w_in"][l]
    d = w_in.shape[0]
    sizes = (W_GROUP, W_GROUP, W_GROUP, W_GROUP, N_HEADS, N_HEADS, W_GROUP, W_GROUP, W_GROUP, W_GROUP,
             GLA_RANK, W_GROUP, RWKV_COLS)
    offs = [0]
    for s in sizes:
        offs.append(offs[-1] + s)
    col = lambda i: w_in[:, offs[i]:offs[i + 1]]
    (u, mq, mk, mv, mi, mf, mo, gq, gk, gv, ga, gg, rcols) = [col(i) for i in range(13)]
    small = jnp.concatenate([ga, mi, mf, jnp.zeros((d, SMALL_W - GLA_RANK - 2 * N_HEADS), F32)], axis=1)
    p["w_in"] = jnp.concatenate([rcols, small, u, mq, mk, mv, mo, gq, gk, gv, gg], axis=1).astype(BF16)
    p["w_gate_t"] = jnp.concatenate([mi, mf], axis=1).T.astype(BF16)
    p["norm_mix"] = prm["norm_mix"][l][None, :]

    lam = lax.complex(prm["s5_lam_re"][l], prm["s5_lam_im"][l])
    dt = jnp.exp(prm["s5_log_dt"][l])[:, None]
    lam_bar = jnp.exp(lam * dt)
    b_bar = ((lam_bar - 1.0) / lam)[..., None] * lax.complex(prm["s5_b_re"][l], prm["s5_b_im"][l])
    eye = jnp.eye(S5_GROUPS, dtype=F32)
    bm = lambda b: jnp.einsum("gph,gk->ghkp", b, eye).reshape(W_GROUP, S5_WIDTH)
    p["bmat"] = jnp.concatenate([bm(b_bar.real), bm(b_bar.imag)], axis=1).astype(BF16)
    cm = lambda c: jnp.einsum("ghp,gk->kpgh", c, eye).reshape(S5_WIDTH, W_GROUP)
    p["cmat"] = jnp.concatenate([cm(prm["s5_c_re"][l]), -cm(prm["s5_c_im"][l])], axis=0).astype(BF16)
    p["lre"] = lam_bar.real.reshape(1, S5_WIDTH)
    p["lim"] = lam_bar.imag.reshape(1, S5_WIDTH)
    p["d"] = prm["s5_d"][l][None, :]
    p["wglu"] = prm["s5_w_glu"][l].astype(BF16)

    bias = prm["ml_gate_bias"][l]
    p["ml_brow"] = jnp.zeros((1, SMALL_W), F32).at[0, LANE_MI:LANE_MI + 2 * N_HEADS].set(bias)
    p["ml_bcol"] = jnp.broadcast_to(bias[:, None], (2 * N_HEADS, CHUNK_ML))
    p["ml_norm"] = prm["ml_norm"][l][None, :]

    p["gla_wa"] = jnp.zeros((SMALL_W, W_GROUP), F32).at[0:GLA_RANK].set(prm["gla_w_alpha"][l]).astype(BF16)
    p["gla_ba"] = prm["gla_b_alpha"][l][None, :]
    p["gla_norm"] = prm["gla_norm"][l][None, :]

    p["rw_mu"] = prm["rw_mu"][l][None, :]
    p["rw_w0"] = prm["rw_w0"][l][None, :]
    p["rw_a0"] = prm["rw_a0"][l][None, :]
    p["rw_kk"] = prm["rw_k_k"][l][None, :]
    p["rw_ka"] = prm["rw_k_a"][l][None, :]
    p["rw_rk"] = prm["rw_r_k"][l].reshape(1, W_GROUP)
    z = jnp.zeros((LANES, W_GROUP), F32)
    p["rw_w2"] = z.at[0:RWKV_W_RANK].set(prm["rw_w2"][l]).astype(BF16)
    p["rw_a2"] = z.at[RWKV_W_RANK:RWKV_W_RANK + RWKV_A_RANK].set(prm["rw_a2"][l]).astype(BF16)
    p["rw_g2"] = z.at[RWKV_W_RANK + RWKV_A_RANK:].set(prm["rw_g2"][l]).astype(BF16)
    p["rw_norm"] = prm["rw_norm"][l][None, :]

    p["layer"] = l
    p["w_out"] = prm["w_out"].astype(BF16)
    p["norm_ffn"] = prm["norm_ffn"][l][None, :]
    p["ffn_wup"] = prm["ffn_w_up"].astype(BF16)
    p["ffn_cw"] = prm["ffn_conv_w"][l]
    p["ffn_cb"] = prm["ffn_conv_b"][l][None, :]
    p["ffn_wdn"] = prm["ffn_w_down"].astype(BF16)
    return p


def _diag_blocks(m):
    b = m.shape[0]
    m = m.reshape(b, N_HEADS, HEAD_DIM, N_HEADS, HEAD_DIM)
    return jnp.stack([m[:, h, :, h, :] for h in range(N_HEADS)], axis=1)


def _prompt_layer(x2d, p, norm_final, n_seq, t_len, final):
    proj, gt = _inproj(x2d, p["norm_mix"], p["w_in"], p["w_gate_t"], n_seq, t_len, tm=512)

    u_tb = proj[:, OFF_MAIN:OFF_MAIN + W_GROUP].reshape(n_seq, t_len, W_GROUP).transpose(1, 0, 2)
    zeros = jnp.zeros((n_seq, S5_WIDTH), F32)
    y_s5_tb, s5_re, s5_im = _s5(u_tb.reshape(t_len * n_seq, W_GROUP), zeros, zeros, p, n_seq, t_len, tt=128)
    y_s5 = y_s5_tb.reshape(t_len, n_seq, W_GROUP).transpose(1, 0, 2).reshape(n_seq * t_len, W_GROUP)

    y_ml, ml_c, ml_n, ml_m = _mlstm(proj, gt, p, n_seq, t_len)
    y_gla, gla_st = _gla(proj, p, n_seq, t_len)
    y_rw, rw_s, rw_shift = _rwkv(proj, p, n_seq, t_len)

    x_new, ffn_buf = _ffn_seq(x2d, (y_s5, y_ml, y_gla, y_rw), p, norm_final, n_seq, t_len, tm=256,
                              final=final)
    states = (
        s5_re.reshape(n_seq, S5_GROUPS, S5_STATE),
        s5_im.reshape(n_seq, S5_GROUPS, S5_STATE),
        _diag_blocks(ml_c),
        ml_n[:, 0, :].reshape(n_seq, N_HEADS, HEAD_DIM),
        ml_m[:, 0:N_HEADS, 0],
        jnp.swapaxes(_diag_blocks(gla_st), -1, -2),
        _diag_blocks(rw_s),
        rw_shift[:, 0, :],
        ffn_buf,
    )
    return x_new, states


def _sample_layer(x2d, st, pair_states, layer, p, norm_final, final):
    (s5_re, s5_im, ml_m, rw_shift, ffn_buf) = st
    n = x2d.shape[0]
    npairs = n * N_HEADS
    proj, _ = _inproj(x2d, p["norm_mix"], p["w_in"], p["w_gate_t"], 1, n, tm=n)

    y_s5, s5_re_new, s5_im_new = _s5(proj[:, OFF_MAIN:OFF_MAIN + W_GROUP],
                                     s5_re.reshape(n, S5_WIDTH), s5_im.reshape(n, S5_WIDTH),
                                     p, n, 1, tt=1)

    m_pad = jnp.zeros((n, SMALL_W), F32).at[:, LANE_MI:LANE_MI + N_HEADS].set(ml_m)
    pack, m_new, g, bonus = _step_vectors(proj, rw_shift, m_pad, p)
    pack = pack.reshape(N_PACK, n, N_HEADS, HEAD_DIM).transpose(1, 2, 0, 3).reshape(npairs, N_PACK, HEAD_DIM)
    ml_c_new, gla_s_new, rw_s_new, opack = _pairs_step(pack, *pair_states, layer)
    opack = opack.reshape(n, N_HEADS, N_OPACK, HEAD_DIM).transpose(2, 0, 1, 3).reshape(N_OPACK, n, W_GROUP)

    main = lambda blk: proj[:, blk * W_GROUP:(blk + 1) * W_GROUP]
    y_ml, y_gla, y_rw = _step_post(opack[OROW_ML], main(BLK_MO), opack[OROW_GLA], main(BLK_GG),
                                   opack[OROW_RW], g, bonus, p)
    x_new, ug = _ffn_step(x2d, (y_s5, y_ml, y_gla, y_rw), p, norm_final, ffn_buf[:, 0, :], ffn_buf[:, 1, :],
                          final)
    states = (
        s5_re_new.reshape(n, S5_GROUPS, S5_STATE),
        s5_im_new.reshape(n, S5_GROUPS, S5_STATE),
        ml_c_new.reshape(n, N_HEADS, HEAD_DIM, HEAD_DIM),
        opack[OROW_N].reshape(n, N_HEADS, HEAD_DIM),
        m_new[:, LANE_MI:LANE_MI + N_HEADS],
        gla_s_new.reshape(n, N_HEADS, HEAD_DIM, HEAD_DIM),
        rw_s_new.reshape(n, N_HEADS, HEAD_DIM, HEAD_DIM),
        proj[:, 0:RWKV_COLS],
        jnp.stack([ffn_buf[:, 1, :], ug], axis=1),
    )
    return x_new, states


def kernel(x_prompt, x_sample, state_s5_re, state_s5_im, state_mlstm_C, state_mlstm_n, state_mlstm_m, state_gla_S, state_rwkv_S, state_rwkv_shift, state_ffn_conv, norm_mix, w_in, s5_lam_re, s5_lam_im, s5_log_dt, s5_b_re, s5_b_im, s5_c_re, s5_c_im, s5_d, s5_w_glu, ml_gate_bias, ml_norm, gla_w_alpha, gla_b_alpha, gla_norm, rw_mu, rw_w0, rw_w2, rw_a0, rw_a2, rw_g2, rw_k_k, rw_k_a, rw_r_k, rw_norm, w_out, norm_ffn, ffn_w_up, ffn_conv_w, ffn_conv_b, ffn_w_down, norm_final):
    prm = dict(norm_mix=norm_mix, w_in=w_in, s5_lam_re=s5_lam_re, s5_lam_im=s5_lam_im, s5_log_dt=s5_log_dt,
               s5_b_re=s5_b_re, s5_b_im=s5_b_im, s5_c_re=s5_c_re, s5_c_im=s5_c_im, s5_d=s5_d,
               s5_w_glu=s5_w_glu, ml_gate_bias=ml_gate_bias, ml_norm=ml_norm, gla_w_alpha=gla_w_alpha,
               gla_b_alpha=gla_b_alpha, gla_norm=gla_norm, rw_mu=rw_mu, rw_w0=rw_w0, rw_w2=rw_w2,
               rw_a0=rw_a0, rw_a2=rw_a2, rw_g2=rw_g2, rw_k_k=rw_k_k, rw_k_a=rw_k_a, rw_r_k=rw_r_k,
               rw_norm=rw_norm, w_out=w_out, norm_ffn=norm_ffn, ffn_w_up=ffn_w_up, ffn_conv_w=ffn_conv_w,
               ffn_conv_b=ffn_conv_b, ffn_w_down=ffn_w_down)
    depth = w_in.shape[0]
    n_seq, t_len, d = x_prompt.shape
    n_smp = x_sample.shape[0]
    assert t_len % 512 == 0 and x_sample.shape[1] == 1
    nf = norm_final[None, :]
    small_states = (state_s5_re, state_s5_im, state_mlstm_m, state_rwkv_shift, state_ffn_conv)
    pairs = depth * n_smp * N_HEADS
    pair_states = (state_mlstm_C.reshape(pairs, HEAD_DIM, HEAD_DIM),
                   state_mlstm_n.reshape(pairs, 1, HEAD_DIM),
                   state_gla_S.reshape(pairs, HEAD_DIM, HEAD_DIM),
                   state_rwkv_S.reshape(pairs, HEAD_DIM, HEAD_DIM))

    xp = x_prompt.reshape(n_seq * t_len, d)
    xs = x_sample.reshape(n_smp, d)
    p_states, s_states = [], []
    for l in range(depth):
        p = _layer_params(l, prm)
        final = l == depth - 1
        xp, ps = _prompt_layer(xp, p, nf, n_seq, t_len, final)
        xs, ss = _sample_layer(xs, tuple(s[l] for s in small_states), pair_states, l, p, nf, final)
        p_states.append(ps)
        s_states.append(ss)
    new_p = tuple(jnp.stack([st[i] for st in p_states]) for i in range(9))
    new_s = tuple(jnp.stack([st[i] for st in s_states]) for i in range(9))
    return (xp.reshape(n_seq, t_len, d), xs.reshape(n_smp, 1, d)) + new_p + new_s
```

```python
import functools
import math

import jax
import jax.numpy as jnp
from jax import lax
from jax.experimental import pallas as pl
from jax.experimental.pallas import tpu as pltpu

F32 = jnp.float32
BF16 = jnp.bfloat16

LANES = 128
W_GROUP = 256
HEAD_DIM = 64
HEAD_SHIFT = 6
N_HEADS = 4
S5_CH = 16
S5_GROUPS = 16
S5_STATE = 64
S5_WIDTH = S5_GROUPS * S5_STATE
GLA_RANK = 16
GLA_GATE_TEMP = 16.0
RWKV_W_RANK = 32
RWKV_A_RANK = 32
RWKV_G_RANK = 64
RWKV_COLS = 3 * W_GROUP + RWKV_W_RANK + RWKV_A_RANK + RWKV_G_RANK
RWKV_DECAY_SCALE = 0.6065306597126334
CONV_W = 3
EPS = 1e-6
GN_EPS = 64e-5
NEG_BIG = -1e30
EXP_CLAMP = 80.0

SMALL_W = LANES
OFF_SMALL = RWKV_COLS
OFF_MAIN = RWKV_COLS + SMALL_W
D_PROJ = OFF_MAIN + 9 * W_GROUP
LANE_MI = GLA_RANK
LANE_MF = GLA_RANK + N_HEADS
(BLK_U, BLK_MQ, BLK_MK, BLK_MV, BLK_MO, BLK_GQ, BLK_GK, BLK_GV, BLK_GG) = range(
    OFF_MAIN // W_GROUP, OFF_MAIN // W_GROUP + 9)
BLK_SMALL = OFF_SMALL // SMALL_W

CHUNK_ML = 128
CHUNK_RW = 64
SEQS_PER_STEP = 8
SEQS_PER_STEP_ML = 4
VMEM_LIMIT = 56 * 1024 * 1024


def _cparams(*sem):
    return pltpu.CompilerParams(dimension_semantics=sem, vmem_limit_bytes=VMEM_LIMIT)


def _dot(a, b):
    return jnp.dot(a.astype(BF16), b.astype(BF16), preferred_element_type=F32)


def _dot_nt(a, b):
    return lax.dot_general(a.astype(BF16), b.astype(BF16), (((1,), (1,)), ((), ())),
                           preferred_element_type=F32)


def _dot_tn(a, b):
    return lax.dot_general(a.astype(BF16), b.astype(BF16), (((0,), (0,)), ((), ())),
                           preferred_element_type=F32)


def _split3(x):
    hi = x.astype(BF16)
    r1 = x - hi.astype(F32)
    mid = r1.astype(BF16)
    lo = (r1 - mid.astype(F32)).astype(BF16)
    return hi, mid, lo


def _dot_exact_rhs(a01, x):
    hi, mid, lo = _split3(x)
    f = lambda p: jnp.dot(a01, p, preferred_element_type=F32)
    return f(hi) + f(mid) + f(lo)


def _dot_exact_lhs(x, b01):
    hi, mid, lo = _split3(x)
    f = lambda p: jnp.dot(p, b01, preferred_element_type=F32)
    return f(hi) + f(mid) + f(lo)


def _sigmoid(x):
    return 1.0 / (1.0 + jnp.exp(-x))


def _log_sigmoid(x):
    return jnp.minimum(x, 0.0) - jnp.log(1.0 + jnp.exp(-jnp.abs(x)))


def _gelu(x):
    return 0.5 * x * (1.0 + jnp.tanh(math.sqrt(2.0 / math.pi) * (x + 0.044715 * (x * x * x))))


def _iota(shape, dim):
    return lax.broadcasted_iota(jnp.int32, shape, dim)


def _head_ones():
    r = _iota((W_GROUP, W_GROUP), 0) >> HEAD_SHIFT
    c = _iota((W_GROUP, W_GROUP), 1) >> HEAD_SHIFT
    return jnp.where(r == c, 1.0, 0.0).astype(BF16)


def _head_sum(x, ones):
    return _dot_exact_lhs(x, ones)


def _head_lane_mask(h, rows=1):
    lane = _iota((rows, W_GROUP), 1) >> HEAD_SHIFT
    return jnp.where(lane == h, 1.0, 0.0).astype(F32)


def _rmsnorm_rows(x, g):
    ms = jnp.mean(x * x, axis=-1, keepdims=True)
    return x * lax.rsqrt(ms + EPS) * g


def _tril_ones(n, strict=False):
    r = _iota((n, n), 0)
    c = _iota((n, n), 1)
    return jnp.where((r > c) if strict else (r >= c), 1.0, 0.0).astype(BF16)


def _triu_ones(n):
    r = _iota((n, n), 0)
    c = _iota((n, n), 1)
    return jnp.where(r <= c, 1.0, 0.0).astype(BF16)


def _inproj_kernel(x_ref, g_ref, w_ref, wg_ref, proj_ref, gt_ref):
    hn = _rmsnorm_rows(x_ref[...], g_ref[...]).astype(BF16)
    proj_ref[...] = jnp.dot(hn, w_ref[...], preferred_element_type=F32)
    gt_ref[0] = lax.dot_general(wg_ref[...], hn, (((1,), (1,)), ((), ())),
                                preferred_element_type=F32)


def _inproj(x2d, g, w, wg, n_seq, t_len, tm):
    n, d = x2d.shape
    tiles_per_seq = t_len // tm
    return pl.pallas_call(
        _inproj_kernel,
        grid=(n // tm,),
        in_specs=[
            pl.BlockSpec((tm, d), lambda i: (i, 0)),
            pl.BlockSpec((1, d), lambda i: (0, 0)),
            pl.BlockSpec((d, D_PROJ), lambda i: (0, 0)),
            pl.BlockSpec((8, d), lambda i: (0, 0)),
        ],
        out_specs=[
            pl.BlockSpec((tm, D_PROJ), lambda i: (i, 0)),
            pl.BlockSpec((1, 8, tm), lambda i: (i // tiles_per_seq, 0, i % tiles_per_seq)),
        ],
        out_shape=[
            jax.ShapeDtypeStruct((n, D_PROJ), F32),
            jax.ShapeDtypeStruct((n_seq, 8, t_len), F32),
        ],
        compiler_params=_cparams("arbitrary"),
        name="inproj",
    )(x2d, g, w, wg)


def _s5_kernel(u_ref, h0re_ref, h0im_ref, bmat_ref, lre_ref, lim_ref, cmat_ref, d_ref, wglu_ref,
               y_ref, hre_out, him_out, bu_scr, hs_scr, hre_scr, him_scr, *, bp, tt):
    i = pl.program_id(0)

    @pl.when(i == 0)
    def _():
        hre_scr[...] = h0re_ref[...]
        him_scr[...] = h0im_ref[...]

    u = u_ref[...]
    bu_scr[...] = jnp.dot(u.astype(BF16), bmat_ref[...], preferred_element_type=F32)
    lre = lre_ref[...]
    lim = lim_ref[...]

    def step(t, carry):
        hre, him = carry
        r0 = pl.multiple_of(t * bp, bp)
        bre = bu_scr[pl.ds(r0, bp), 0:S5_WIDTH]
        bim = bu_scr[pl.ds(r0, bp), S5_WIDTH:2 * S5_WIDTH]
        nre = lre * hre - lim * him + bre
        nim = lre * him + lim * hre + bim
        hs_scr[pl.ds(r0, bp), 0:S5_WIDTH] = nre
        hs_scr[pl.ds(r0, bp), S5_WIDTH:2 * S5_WIDTH] = nim
        return nre, nim

    carry = (hre_scr[...], him_scr[...])
    if tt == 1:
        hre, him = step(0, carry)
    else:
        hre, him = lax.fori_loop(0, tt, step, carry)
    hre_scr[...] = hre
    him_scr[...] = him
    hre_out[...] = hre
    him_out[...] = him
    y = jnp.dot(hs_scr[...].astype(BF16), cmat_ref[...], preferred_element_type=F32) + d_ref[...] * u
    z = _gelu(y)
    y_ref[...] = (z * _sigmoid(_dot(z, wglu_ref[...]))).astype(y_ref.dtype)


def _s5(u_tb, h0re, h0im, sp, bp, t_len, tt):
    n = u_tb.shape[0]
    full = lambda shape: pl.BlockSpec(shape, lambda i: (0,) * len(shape))
    return pl.pallas_call(
        functools.partial(_s5_kernel, bp=bp, tt=tt),
        grid=(t_len // tt,),
        in_specs=[
            pl.BlockSpec((tt * bp, W_GROUP), lambda i: (i, 0)),
            full((bp, S5_WIDTH)), full((bp, S5_WIDTH)),
            full((W_GROUP, 2 * S5_WIDTH)),
            full((bp, S5_WIDTH)), full((bp, S5_WIDTH)),
            full((2 * S5_WIDTH, W_GROUP)),
            full((1, W_GROUP)),
            full((W_GROUP, W_GROUP)),
        ],
        out_specs=[
            pl.BlockSpec((tt * bp, W_GROUP), lambda i: (i, 0)),
            full((bp, S5_WIDTH)), full((bp, S5_WIDTH)),
        ],
        out_shape=[
            jax.ShapeDtypeStruct((n, W_GROUP), BF16),
            jax.ShapeDtypeStruct((bp, S5_WIDTH), F32),
            jax.ShapeDtypeStruct((bp, S5_WIDTH), F32),
        ],
        scratch_shapes=[
            pltpu.VMEM((tt * bp, 2 * S5_WIDTH), F32),
            pltpu.VMEM((tt * bp, 2 * S5_WIDTH), F32),
            pltpu.VMEM((bp, S5_WIDTH), F32),
            pltpu.VMEM((bp, S5_WIDTH), F32),
        ],
        compiler_params=_cparams("arbitrary"),
        name="s5_scan",
    )(u_tb, h0re, h0im, sp["bmat"], jnp.broadcast_to(sp["lre"], (bp, S5_WIDTH)),
      jnp.broadcast_to(sp["lim"], (bp, S5_WIDTH)), sp["cmat"], sp["d"], sp["wglu"])


def _run_interleaved(chains):
    live = list(chains)
    while live:
        still = []
        for ch in live:
            try:
                next(ch)
                still.append(ch)
            except StopIteration:
                pass
        live = still


def _mlstm_kernel(q_ref, k_ref, v_ref, og_ref, sm_ref, gt_ref, brow_ref, bcol_ref, norm_ref,
                  y_ref, c_out, n_out, m_out, c_scr, n_scr, m_scr, *, L, nc, G):
    c = pl.program_id(1)

    @pl.when(c == 0)
    def _():
        c_scr[...] = jnp.zeros_like(c_scr)
        n_scr[...] = jnp.zeros_like(n_scr)
        m_scr[...] = jnp.zeros_like(m_scr)

    ones = _head_ones()
    causal = _iota((L, L), 0) >= _iota((L, L), 1)
    lane_head = _iota((1, W_GROUP), 1) >> HEAD_SHIFT
    tril = _tril_ones(L)
    triu = _triu_ones(L)

    def chain(gi):
        q = q_ref[gi]
        k = k_ref[gi] * (HEAD_DIM ** -0.5)
        v = v_ref[gi]
        pre = sm_ref[gi] + brow_ref[...]
        gtb = gt_ref[gi] + bcol_ref[...]
        bc_col = _dot_exact_rhs(tril, _log_sigmoid(pre))
        bc_row = _dot_exact_lhs(_log_sigmoid(gtb), triu)
        yield
        C = c_scr[gi]
        n = n_scr[gi, 0:1, :]
        qC = _dot_nt(q, C)
        qn = q * n
        yield

        hc = jnp.zeros((L, W_GROUP), F32)
        wl_all = jnp.zeros((L, W_GROUP), F32)
        w0l_all = jnp.zeros((1, W_GROUP), F32)
        for h in range(N_HEADS):
            mask = _head_lane_mask(h)
            bcc = bc_col[:, LANE_MF + h:LANE_MF + h + 1]
            bcr = bc_row[N_HEADS + h:N_HEADS + h + 1, :]
            i_row = gtb[h:h + 1, :]
            i_col = pre[:, LANE_MI + h:LANE_MI + h + 1]
            m0 = m_scr[gi, h:h + 1, 0:1]
            qk = _dot_nt(q * mask, k)
            dmat = jnp.where(causal, bcc - bcr + i_row, NEG_BIG)
            g = bcc + m0
            m_row = jnp.maximum(g, jnp.max(dmat, axis=1, keepdims=True))
            w0 = jnp.exp(g - m_row)
            ws = jnp.exp(dmat - m_row) * qk
            yield
            num = w0 * qC + _dot(ws, v * mask)
            den = (w0 * jnp.sum(qn * mask, axis=1, keepdims=True)
                   + jnp.sum(ws, axis=1, keepdims=True))
            hc = hc + mask * (num / jnp.maximum(jnp.abs(den), jnp.exp(-m_row)))
            m_new = m_row[L - 1:L, :]
            b_last = bcc[L - 1:L, :]
            wl_all = wl_all + mask * jnp.exp(b_last - bcc + i_col - m_new)
            w0l_all = w0l_all + mask * jnp.exp(b_last + m0 - m_new)
            m_scr[gi, h:h + 1, :] = jnp.broadcast_to(m_new, (1, LANES))
            yield

        kw = k * wl_all
        c_scr[gi] = w0l_all * C + ones.astype(F32) * _dot_tn(v, kw)
        n_scr[gi, 0:1, :] = w0l_all * n + jnp.sum(kw, axis=0, keepdims=True)
        yield
        ms = _head_sum(hc * hc, ones) * (1.0 / HEAD_DIM)
        y = hc * lax.rsqrt(ms + EPS) * norm_ref[...] * _sigmoid(og_ref[gi])
        y_ref[gi] = y.astype(y_ref.dtype)

    _run_interleaved([chain(gi) for gi in range(G)])

    @pl.when(c == nc - 1)
    def _():
        c_out[...] = c_scr[...]
        n_out[...] = jnp.broadcast_to(n_scr[:, 0:1, :], n_out.shape)
        m_out[...] = m_scr[...]


def _seq_spec(G, L, blk, width=W_GROUP):
    return pl.BlockSpec((G, L, width), lambda b, c: (b, c, blk))


def _mlstm(proj, gt, p, n_seq, t_len):
    L = CHUNK_ML
    G = SEQS_PER_STEP_ML
    nc = t_len // L
    const = lambda shape: pl.BlockSpec(shape, lambda b, c: (0,) * len(shape))
    per_seq = lambda rows, width: pl.BlockSpec((G, rows, width), lambda b, c: (b, 0, 0))
    proj3 = proj.reshape(n_seq, t_len, D_PROJ)
    y, c_new, n_new, m_new = pl.pallas_call(
        functools.partial(_mlstm_kernel, L=L, nc=nc, G=G),
        grid=(n_seq // G, nc),
        in_specs=[
            _seq_spec(G, L, BLK_MQ), _seq_spec(G, L, BLK_MK), _seq_spec(G, L, BLK_MV),
            _seq_spec(G, L, BLK_MO), _seq_spec(G, L, BLK_SMALL, SMALL_W),
            pl.BlockSpec((G, 8, L), lambda b, c: (b, 0, c)),
            const((1, SMALL_W)), const((8, L)), const((1, W_GROUP)),
        ],
        out_specs=[
            pl.BlockSpec((G, L, W_GROUP), lambda b, c: (b, c, 0)),
            per_seq(W_GROUP, W_GROUP), per_seq(8, W_GROUP), per_seq(8, LANES),
        ],
        out_shape=[
            jax.ShapeDtypeStruct((n_seq, t_len, W_GROUP), BF16),
            jax.ShapeDtypeStruct((n_seq, W_GROUP, W_GROUP), F32),
            jax.ShapeDtypeStruct((n_seq, 8, W_GROUP), F32),
            jax.ShapeDtypeStruct((n_seq, 8, LANES), F32),
        ],
        scratch_shapes=[
            pltpu.VMEM((G, W_GROUP, W_GROUP), F32),
            pltpu.VMEM((G, 8, W_GROUP), F32),
            pltpu.VMEM((G, 8, LANES), F32),
        ],
        compiler_params=_cparams("arbitrary", "arbitrary"),
        name="mlstm_chunk",
    )(proj3, proj3, proj3, proj3, proj3, gt, p["ml_brow"], p["ml_bcol"], p["ml_norm"])
    return y.reshape(n_seq * t_len, W_GROUP), c_new, n_new, m_new


def _gla_kernel(q_ref, k_ref, v_ref, gg_ref, sm_ref, wa_ref, ba_ref, norm_ref,
                y_ref, s_out, s_scr, *, L, nc, G):
    c = pl.program_id(1)

    @pl.when(c == 0)
    def _():
        s_scr[...] = jnp.zeros_like(s_scr)

    ones = _head_ones()
    tril = _tril_ones(L)
    causal = _iota((L, N_HEADS * L), 0) >= (_iota((L, N_HEADS * L), 1) & (L - 1))

    def chain(gi):
        q = q_ref[gi] * (HEAD_DIM ** -0.5)
        k = k_ref[gi]
        v = v_ref[gi]
        la = _log_sigmoid(_dot(sm_ref[gi], wa_ref[...]) + ba_ref[...]) * (1.0 / GLA_GATE_TEMP)
        yield
        bc = _dot_exact_rhs(tril, la)
        yield
        qs = q * jnp.exp(bc)
        kh = k * jnp.exp(jnp.minimum(-bc, EXP_CLAMP))
        ST = s_scr[gi]
        att = jnp.where(causal, _dot_nt(qs, _stack_heads(kh)), 0.0)
        from_state = _dot_nt(qs, ST)
        yield
        o = from_state + _dot(att, _stack_heads(v))
        b_last = bc[L - 1:L, :]
        kbar = k * jnp.exp(b_last - bc)
        s_scr[gi] = ST * jnp.exp(b_last) + ones.astype(F32) * _dot_tn(v, kbar)
        yield
        ms = _head_sum(o * o, ones) * (1.0 / HEAD_DIM)
        gg = gg_ref[gi]
        y = o * lax.rsqrt(ms + EPS) * norm_ref[...] * (gg * _sigmoid(gg))
        y_ref[gi] = y.astype(y_ref.dtype)

    _run_interleaved([chain(gi) for gi in range(G)])

    @pl.when(c == nc - 1)
    def _():
        s_out[...] = s_scr[...]


def _gla(proj, p, n_seq, t_len):
    L = CHUNK_ML
    G = SEQS_PER_STEP_ML
    nc = t_len // L
    const = lambda shape: pl.BlockSpec(shape, lambda b, c: (0,) * len(shape))
    proj3 = proj.reshape(n_seq, t_len, D_PROJ)
    y, s_new = pl.pallas_call(
        functools.partial(_gla_kernel, L=L, nc=nc, G=G),
        grid=(n_seq // G, nc),
        in_specs=[
            _seq_spec(G, L, BLK_GQ), _seq_spec(G, L, BLK_GK), _seq_spec(G, L, BLK_GV),
            _seq_spec(G, L, BLK_GG), _seq_spec(G, L, BLK_SMALL, SMALL_W),
            const((SMALL_W, W_GROUP)), const((1, W_GROUP)), const((1, W_GROUP)),
        ],
        out_specs=[
            pl.BlockSpec((G, L, W_GROUP), lambda b, c: (b, c, 0)),
            pl.BlockSpec((G, W_GROUP, W_GROUP), lambda b, c: (b, 0, 0)),
        ],
        out_shape=[
            jax.ShapeDtypeStruct((n_seq, t_len, W_GROUP), BF16),
            jax.ShapeDtypeStruct((n_seq, W_GROUP, W_GROUP), F32),
        ],
        scratch_shapes=[pltpu.VMEM((G, W_GROUP, W_GROUP), F32)],
        compiler_params=_cparams("arbitrary", "arbitrary"),
        name="gla_chunk",
    )(proj3, proj3, proj3, proj3, proj3, p["gla_wa"], p["gla_ba"], p["gla_norm"])
    return y.reshape(n_seq * t_len, W_GROUP), s_new


def _rwkv_vectors(rc, prev, p_mu, p_w0, p_a0, p_kk, p_ka, p_rk, w2, a2, g2, ones):
    xm = rc + p_mu * (prev - rc)
    rr = xm[:, 0:W_GROUP]
    rk = xm[:, W_GROUP:2 * W_GROUP]
    rv = xm[:, 2 * W_GROUP:3 * W_GROUP]
    tail = xm[:, 3 * W_GROUP:RWKV_COLS]
    lw = -RWKV_DECAY_SCALE * _sigmoid(p_w0 + _dot(jnp.tanh(tail), w2))
    a = _sigmoid(p_a0 + _dot(tail, a2))
    g = _dot(_sigmoid(tail), g2)
    kk = rk * p_kk
    kk = kk * lax.rsqrt(jnp.maximum(_head_sum(kk * kk, ones), 1e-24))
    kt = rk * (1.0 + (a - 1.0) * p_ka)
    bonus = _head_sum(rr * kt * p_rk, ones) * rv
    return rr, lw, kt, rv, kk, a, g, bonus


def _head_layernorm(o, g, ones):
    mu = _head_sum(o, ones) * (1.0 / HEAD_DIM)
    oc = o - mu
    var = _head_sum(oc * oc, ones) * (1.0 / HEAD_DIM)
    return oc * lax.rsqrt(var + GN_EPS) * g


def _stack_heads(x):
    xb = x.astype(BF16)
    lane_head = _iota((1, W_GROUP), 1) >> HEAD_SHIFT
    return jnp.concatenate([jnp.where(lane_head == h, xb, jnp.zeros_like(xb)) for h in range(N_HEADS)],
                           axis=0)


def _each(f, *seqs):
    return [f(*args) for args in zip(*seqs)]


def _block_mm(x_list, y_list):
    return _each(lambda x, y: _dot(x, _stack_heads(y)), x_list, y_list)


def _unit_lower_inverse(a_list, L):
    t_idx = _iota((L, N_HEADS * L), 0)
    s_idx = _iota((L, N_HEADS * L), 1) & (L - 1)
    eye = jnp.where(t_idx == s_idx, 1.0, 0.0).astype(F32)
    in16 = (t_idx >> 4) == (s_idx >> 4)
    in32 = ((t_idx >> 5) == (s_idx >> 5)) & ((t_idx >> 4) > (s_idx >> 4))
    in64 = (t_idx >> 5) > (s_idx >> 5)
    x16 = _each(lambda a: jnp.where(in16, -a, 0.0), a_list)
    x2 = _block_mm(x16, x16)
    x4 = _block_mm(x2, x2)
    x8 = _block_mm(x4, x4)
    t = _each(lambda x: eye + x, x16)
    for xp in (x2, x4, x8):
        t = _each(lambda t_, d: t_ + d, t, _block_mm(t, xp))
    for sel in (in32, in64):
        w = _block_mm(_each(lambda a: jnp.where(sel, a, 0.0), a_list), t)
        t = _each(lambda t_, d: t_ - d, t, _block_mm(t, w))
    return t


def _rwkv_chunks(rcs, prev_rows, states, prm, ones, L):
    (mu, w0, a0, pkk, pka, prk, w2, a2, g2, norm) = prm
    first_row = _iota((L, RWKV_COLS), 0) == 0
    prevs = _each(lambda rc, pr: jnp.where(first_row, pr, pltpu.roll(rc, 1, 0)), rcs, prev_rows)
    vecs = _each(lambda rc, pv: _rwkv_vectors(rc, pv, mu, w0, a0, pkk, pka, prk, w2, a2, g2, ones),
                 rcs, prevs)
    r, lw, kx, v, kk, a, g, bonus = [list(col) for col in zip(*vecs)]

    tril = _tril_ones(L)
    lc = _each(lambda x: _dot_exact_rhs(tril, x), lw)
    akk = _each(lambda a_, k_: a_ * k_, a, kk)
    e_neg = _each(lambda x: jnp.exp(-x), lc)
    l_last = _each(lambda x: x[L - 1:L, :], lc)
    e_end = _each(lambda ll, x: jnp.exp(ll - x), l_last, lc)
    kr = _each(lambda k_, r_, c_, w_: jnp.concatenate([k_ * jnp.exp(c_ - w_), r_ * jnp.exp(c_)], axis=0),
               kk, r, lc, lw)
    a_hat = _each(lambda x, e: _stack_heads(x * e), akk, e_neg)
    k_hat = _each(lambda x, e: _stack_heads(x * e), kx, e_neg)
    v_bd = _each(_stack_heads, v)

    t_idx = _iota((L, N_HEADS * L), 0)
    s_idx = _iota((L, N_HEADS * L), 1) & (L - 1)
    strict = t_idx > s_idx
    incl = t_idx >= s_idx
    pa = _each(_dot_nt, kr, a_hat)
    pk = _each(_dot_nt, kr, k_hat)
    a_ua = _each(lambda p_: jnp.where(strict, p_[0:L], 0.0), pa)
    b_ra = _each(lambda p_: jnp.where(incl, p_[L:2 * L], 0.0), pa)
    ab_k = _each(lambda p_: jnp.concatenate([jnp.where(strict, p_[0:L], 0.0),
                                             jnp.where(incl, p_[L:2 * L], 0.0)], axis=0), pk)
    T = _unit_lower_inverse(a_ua, L)

    from_state = _each(_dot_nt, kr, states)
    from_v = _each(_dot, ab_k, v_bd)
    rhs = _each(lambda s_, v_: s_[0:L] + v_[0:L], from_state, from_v)
    U = _block_mm(T, rhs)
    corr = _block_mm(b_ra, U)
    o = _each(lambda s_, v_, c_: s_[L:2 * L] + v_[L:2 * L] - c_, from_state, from_v, corr)
    upd = _each(lambda v_, u_, k_, a_, e: _dot_tn(jnp.concatenate([v_, u_], axis=0),
                                                  jnp.concatenate([k_ * e, -(a_ * e)], axis=0)),
                v, U, kx, akk, e_end)
    s_new = _each(lambda s_, ll, d: s_ * jnp.exp(ll) + ones.astype(F32) * d, states, l_last, upd)
    y = _each(lambda o_, b_, g_: (_head_layernorm(o_, norm, ones) + b_) * g_, o, bonus, g)
    return y, s_new


def _rwkv_kernel(rc_ref, mu_ref, w0_ref, a0_ref, kk_ref, ka_ref, rk_ref, w2_ref, a2_ref, g2_ref,
                 norm_ref, y_ref, s_out, shift_out, s_scr, prev_scr, *, L, nc, G):
    c = pl.program_id(1)

    @pl.when(c == 0)
    def _():
        s_scr[...] = jnp.zeros_like(s_scr)
        prev_scr[...] = jnp.zeros_like(prev_scr)

    ones = _head_ones()
    prm = (mu_ref[...], w0_ref[...], a0_ref[...], kk_ref[...], ka_ref[...], rk_ref[...],
           w2_ref[...], a2_ref[...], g2_ref[...], norm_ref[...])
    rcs = [rc_ref[gi] for gi in range(G)]
    ys, s_new = _rwkv_chunks(rcs, [prev_scr[gi, 0:1, :] for gi in range(G)],
                             [s_scr[gi] for gi in range(G)], prm, ones, L)
    for gi in range(G):
        prev_scr[gi, 0:1, :] = rcs[gi][L - 1:L, :]
        s_scr[gi] = s_new[gi]
        y_ref[gi] = ys[gi].astype(y_ref.dtype)

    @pl.when(c == nc - 1)
    def _():
        s_out[...] = s_scr[...]
        shift_out[...] = jnp.broadcast_to(prev_scr[:, 0:1, :], shift_out.shape)


def _rwkv(proj, p, n_seq, t_len):
    L = CHUNK_RW
    G = SEQS_PER_STEP
    nc = t_len // L
    const = lambda shape: pl.BlockSpec(shape, lambda b, c: (0,) * len(shape))
    row = const((1, W_GROUP))
    pad = const((LANES, W_GROUP))
    y, s_new, shift = pl.pallas_call(
        functools.partial(_rwkv_kernel, L=L, nc=nc, G=G),
        grid=(n_seq // G, nc),
        in_specs=[
            pl.BlockSpec((G, L, RWKV_COLS), lambda b, c: (b, c, 0)),
            const((1, RWKV_COLS)), row, row, row, row, row, pad, pad, pad, row,
        ],
        out_specs=[
            pl.BlockSpec((G, L, W_GROUP), lambda b, c: (b, c, 0)),
            pl.BlockSpec((G, W_GROUP, W_GROUP), lambda b, c: (b, 0, 0)),
            pl.BlockSpec((G, 8, RWKV_COLS), lambda b, c: (b, 0, 0)),
        ],
        out_shape=[
            jax.ShapeDtypeStruct((n_seq, t_len, W_GROUP), BF16),
            jax.ShapeDtypeStruct((n_seq, W_GROUP, W_GROUP), F32),
            jax.ShapeDtypeStruct((n_seq, 8, RWKV_COLS), F32),
        ],
        scratch_shapes=[
            pltpu.VMEM((G, W_GROUP, W_GROUP), F32),
            pltpu.VMEM((G, 8, RWKV_COLS), F32),
        ],
        compiler_params=_cparams("arbitrary", "arbitrary"),
        name="rwkv_chunk",
    )(proj.reshape(n_seq, t_len, D_PROJ), p["rw_mu"], p["rw_w0"], p["rw_a0"], p["rw_kk"], p["rw_ka"],
      p["rw_rk"], p["rw_w2"], p["rw_a2"], p["rw_g2"], p["rw_norm"])
    return y.reshape(n_seq * t_len, W_GROUP), s_new, shift


FF_CHUNK = 256


def _mix_residual(x_ref, ys, wout_ref):
    acc = x_ref[...]
    for j, y in enumerate(ys):
        acc = acc + jnp.dot(y.astype(BF16), wout_ref[j * W_GROUP:(j + 1) * W_GROUP, :],
                            preferred_element_type=F32)
    return acc


def _ffn_body(x1, nrm_ref, wup_ref, cw_ref, cb_ref, wdn_ref, prev_rows, d_ff, act_scr):
    h2 = _rmsnorm_rows(x1, nrm_ref[...]).astype(BF16)
    for j in range(d_ff // FF_CHUNK):
        lo, hi = j * FF_CHUNK, (j + 1) * FF_CHUNK
        ug = jnp.dot(h2, wup_ref[:, lo:hi], preferred_element_type=F32)
        uv = jnp.dot(h2, wup_ref[:, d_ff + lo:d_ff + hi], preferred_element_type=F32)
        p2, p1 = prev_rows(j, ug)
        conv = (cb_ref[:, lo:hi] + cw_ref[0:1, lo:hi] * p2 + cw_ref[1:2, lo:hi] * p1
                + cw_ref[2:3, lo:hi] * ug)
        act_scr[:, lo:hi] = (_gelu(conv) * uv).astype(BF16)
    return x1 + jnp.dot(act_scr[...], wdn_ref[...], preferred_element_type=F32)


def _ffn_seq_kernel(x_ref, y0_ref, y1_ref, y2_ref, y3_ref, wout_ref, nrm_ref, wup_ref, cw_ref, cb_ref,
                    wdn_ref, nf_ref, o_ref, buf_out, carry_scr, act_scr, *, tm, d_ff, final):
    t = pl.program_id(1)

    @pl.when(t == 0)
    def _():
        carry_scr[...] = jnp.zeros_like(carry_scr)

    x1 = _mix_residual(x_ref, (y0_ref[...], y1_ref[...], y2_ref[...], y3_ref[...]), wout_ref)
    row = _iota((tm, FF_CHUNK), 0)

    def prev_rows(j, ug):
        lo, hi = j * FF_CHUNK, (j + 1) * FF_CHUNK
        c0 = carry_scr[0:1, lo:hi]
        c1 = carry_scr[1:2, lo:hi]
        p1 = jnp.where(row == 0, c1, pltpu.roll(ug, 1, 0))
        p2 = jnp.where(row == 0, c0, jnp.where(row == 1, c1, pltpu.roll(ug, 2, 0)))
        carry_scr[0:2, lo:hi] = ug[tm - 2:tm, :]
        return p2, p1

    out = _ffn_body(x1, nrm_ref, wup_ref, cw_ref, cb_ref, wdn_ref, prev_rows, d_ff, act_scr)
    if final:
        out = _rmsnorm_rows(out, nf_ref[...])
    o_ref[...] = out
    buf_out[0] = carry_scr[...]


def _ffn_step_kernel(x_ref, y0_ref, y1_ref, y2_ref, y3_ref, wout_ref, nrm_ref, wup_ref, cw_ref, cb_ref,
                     wdn_ref, nf_ref, p2_ref, p1_ref, o_ref, ug_out, act_scr, *, d_ff, final):
    x1 = _mix_residual(x_ref, (y0_ref[...], y1_ref[...], y2_ref[...], y3_ref[...]), wout_ref)

    def prev_rows(j, ug):
        lo, hi = j * FF_CHUNK, (j + 1) * FF_CHUNK
        ug_out[:, lo:hi] = ug
        return p2_ref[:, lo:hi], p1_ref[:, lo:hi]

    out = _ffn_body(x1, nrm_ref, wup_ref, cw_ref, cb_ref, wdn_ref, prev_rows, d_ff, act_scr)
    if final:
        out = _rmsnorm_rows(out, nf_ref[...])
    o_ref[...] = out


def _ffn_seq(x2d, ys, p, norm_final, n_seq, t_len, tm, final):
    n, d = x2d.shape
    d_ff = p["ffn_wdn"].shape[1]
    lay = p["layer"]
    nt = t_len // tm
    rows = lambda w: pl.BlockSpec((tm, w), lambda b, t: (b * nt + t, 0))
    const = lambda shape: pl.BlockSpec(shape, lambda b, t: (0,) * len(shape))
    layered = lambda shape: pl.BlockSpec((None,) + shape, lambda b, t: (lay, 0, 0),
                                         pipeline_mode=pl.Buffered(1))
    return pl.pallas_call(
        functools.partial(_ffn_seq_kernel, tm=tm, d_ff=d_ff, final=final),
        grid=(n_seq, nt),
        in_specs=[rows(d)] + [rows(W_GROUP)] * 4 + [
            layered((d, d)), const((1, d)), layered((d, 2 * d_ff)), const((CONV_W, d_ff)),
            const((1, d_ff)), layered((d_ff, d)), const((1, d))],
        out_specs=[rows(d), pl.BlockSpec((1, CONV_W - 1, d_ff), lambda b, t: (b, 0, 0))],
        out_shape=[jax.ShapeDtypeStruct((n, d), F32),
                   jax.ShapeDtypeStruct((n_seq, CONV_W - 1, d_ff), F32)],
        scratch_shapes=[pltpu.VMEM((CONV_W - 1, d_ff), F32), pltpu.VMEM((tm, d_ff), BF16)],
        compiler_params=_cparams("arbitrary", "arbitrary"),
        name="wout_ffn_seq",
    )(x2d, *ys, p["w_out"], p["norm_ffn"], p["ffn_wup"], p["ffn_cw"], p["ffn_cb"], p["ffn_wdn"],
      norm_final)


def _ffn_step(x2d, ys, p, norm_final, prev2, prev1, final):
    n, d = x2d.shape
    d_ff = p["ffn_wdn"].shape[1]
    lay = p["layer"]
    full = lambda shape: pl.BlockSpec(shape, lambda i: (0,) * len(shape))
    layered = lambda shape: pl.BlockSpec((None,) + shape, lambda i: (lay, 0, 0))
    return pl.pallas_call(
        functools.partial(_ffn_step_kernel, d_ff=d_ff, final=final),
        grid=(1,),
        in_specs=[full((n, d))] + [full((n, W_GROUP))] * 4 + [
            layered((d, d)), full((1, d)), layered((d, 2 * d_ff)), full((CONV_W, d_ff)),
            full((1, d_ff)), layered((d_ff, d)), full((1, d)), full((n, d_ff)), full((n, d_ff))],
        out_specs=[full((n, d)), full((n, d_ff))],
        out_shape=[jax.ShapeDtypeStruct((n, d), F32), jax.ShapeDtypeStruct((n, d_ff), F32)],
        scratch_shapes=[pltpu.VMEM((n, d_ff), BF16)],
        compiler_params=_cparams("arbitrary"),
        name="wout_ffn_step",
    )(x2d, *ys, p["w_out"], p["norm_ffn"], p["ffn_wup"], p["ffn_cw"], p["ffn_cb"], p["ffn_wdn"],
      norm_final, prev2, prev1)


def _step_vec_kernel(proj_ref, shift_ref, m_ref, brow_ref, wa_ref, ba_ref, mu_ref, w0_ref, a0_ref,
                     kk_ref, ka_ref, rk_ref, w2_ref, a2_ref, g2_ref,
                     pack_out, m_out, g_out, bonus_out):
    ones = _head_ones()
    sm = proj_ref[:, OFF_SMALL:OFF_SMALL + SMALL_W]
    pre = sm + brow_ref[...]
    logf = pltpu.roll(_log_sigmoid(pre), LANES - N_HEADS, 1)
    m_old = m_ref[...]
    m_new = jnp.maximum(logf + m_old, pre)
    m_out[...] = m_new
    lane = _iota((SMALL_W, W_GROUP), 0)
    spread_mat = jnp.where(lane == LANE_MI + (_iota((SMALL_W, W_GROUP), 1) >> HEAD_SHIFT), 1.0, 0.0).astype(BF16)
    gate_lanes = (_iota(pre.shape, 1) >= LANE_MI) & (_iota(pre.shape, 1) < LANE_MI + N_HEADS)
    spread = lambda x: _dot_exact_lhs(jnp.where(gate_lanes, x, 0.0), spread_mat)
    la = _log_sigmoid(_dot(sm, wa_ref[...]) + ba_ref[...]) * (1.0 / GLA_GATE_TEMP)
    rc = proj_ref[:, 0:RWKV_COLS]
    rr, lw, kt, rv, kk, a, g, bonus = _rwkv_vectors(
        rc, shift_ref[...], mu_ref[...], w0_ref[...], a0_ref[...], kk_ref[...], ka_ref[...],
        rk_ref[...], w2_ref[...], a2_ref[...], g2_ref[...], ones)
    main = lambda blk: proj_ref[:, blk * W_GROUP:(blk + 1) * W_GROUP]
    rows = {
        ROW_MQ: main(BLK_MQ), ROW_MK: main(BLK_MK), ROW_MV: main(BLK_MV),
        ROW_GQ: main(BLK_GQ), ROW_GK: main(BLK_GK), ROW_GV: main(BLK_GV), ROW_GA: jnp.exp(la),
        ROW_KK: kk, ROW_AKK: a * kk, ROW_W: jnp.exp(lw), ROW_KT: kt, ROW_RR: rr, ROW_RV: rv,
        ROW_IW: spread(jnp.exp(pre - m_new)), ROW_FW: spread(jnp.exp(logf + m_old - m_new)),
        ROW_M: spread(m_new),
    }
    for i, val in rows.items():
        pack_out[i] = val
    g_out[...] = g
    bonus_out[...] = bonus


def _step_vectors(proj, shift, m_pad, p):
    n = proj.shape[0]
    full = lambda a: pl.BlockSpec(a.shape, lambda i: (0,) * a.ndim)
    args = (proj, shift, m_pad, p["ml_brow"], p["gla_wa"], p["gla_ba"], p["rw_mu"], p["rw_w0"],
            p["rw_a0"], p["rw_kk"], p["rw_ka"], p["rw_rk"], p["rw_w2"], p["rw_a2"], p["rw_g2"])
    wide = jax.ShapeDtypeStruct((n, W_GROUP), F32)
    outs = [jax.ShapeDtypeStruct((N_PACK, n, W_GROUP), F32), jax.ShapeDtypeStruct((n, SMALL_W), F32),
            wide, wide]
    return pl.pallas_call(
        _step_vec_kernel,
        grid=(1,),
        in_specs=[full(a) for a in args],
        out_specs=[pl.BlockSpec(o.shape, lambda i, nd=len(o.shape): (0,) * nd) for o in outs],
        out_shape=outs,
        compiler_params=_cparams("arbitrary"),
        name="step_vectors",
    )(*args)


PAIRS_PER_BLOCK = 32
(ROW_MQ, ROW_MK, ROW_MV, ROW_GQ, ROW_GK, ROW_GV, ROW_GA, ROW_KK, ROW_AKK, ROW_W, ROW_KT, ROW_RR, ROW_RV,
 ROW_IW, ROW_FW, ROW_M) = range(16)
N_PACK = 16
(OROW_ML, OROW_GLA, OROW_RW, OROW_N) = range(4)
N_OPACK = 8


def _pairs_step_kernel(pack_ref, c_ref, n_ref, gs_ref, rs_ref, c_out, gs_out, rs_out, opack_out):
    eye = jnp.where(_iota((HEAD_DIM, HEAD_DIM), 0) == _iota((HEAD_DIM, HEAD_DIM), 1), 1.0, 0.0).astype(F32)
    row = lambda i: pack_ref[:, i:i + 1, :]
    col = lambda x: jnp.sum(x * eye, axis=2, keepdims=True)
    to_row = lambda x: jnp.sum(x * eye, axis=1, keepdims=True)

    k = row(ROW_MK) * (HEAD_DIM ** -0.5)
    q = row(ROW_MQ)
    iw = row(ROW_IW)
    fw = row(ROW_FW)
    c_new = fw * c_ref[...] + iw * (col(row(ROW_MV)) * k)
    n_new = fw * n_ref[...] + iw * k
    num = jnp.sum(c_new * q, axis=2, keepdims=True)
    den = jnp.sum(n_new * q, axis=2, keepdims=True)
    m_new = row(ROW_M)[:, :, 0:1]
    c_out[...] = c_new
    opack_out[:, OROW_ML:OROW_ML + 1, :] = to_row(num / jnp.maximum(jnp.abs(den), jnp.exp(-m_new)))
    opack_out[:, OROW_N:OROW_N + 1, :] = n_new

    s_new = col(row(ROW_GA)) * gs_ref[...] + col(row(ROW_GK)) * row(ROW_GV)
    gs_out[...] = s_new
    opack_out[:, OROW_GLA:OROW_GLA + 1, :] = jnp.sum(
        (col(row(ROW_GQ)) * (HEAD_DIM ** -0.5)) * s_new, axis=1, keepdims=True)

    s = rs_ref[...]
    sk = jnp.sum(s * row(ROW_KK), axis=2, keepdims=True)
    s_new = s * row(ROW_W) - sk * row(ROW_AKK) + col(row(ROW_RV)) * row(ROW_KT)
    rs_out[...] = s_new
    opack_out[:, OROW_RW:OROW_RW + 1, :] = to_row(jnp.sum(s_new * row(ROW_RR), axis=2, keepdims=True))
    opack_out[:, OROW_N + 1:N_OPACK, :] = jnp.zeros((pack_ref.shape[0], N_OPACK - OROW_N - 1, HEAD_DIM), F32)


def _pairs_step(pack, ml_c, ml_n, gla_s, rw_s, layer):
    npairs = pack.shape[0]
    P = min(PAIRS_PER_BLOCK, npairs)
    nblk = npairs // P
    here = lambda shape: pl.BlockSpec((P,) + shape, lambda i: (i, 0, 0))
    at_layer = lambda shape: pl.BlockSpec((P,) + shape, lambda i: (layer * nblk + i, 0, 0))
    sq = (HEAD_DIM, HEAD_DIM)
    state = jax.ShapeDtypeStruct((npairs,) + sq, F32)
    return pl.pallas_call(
        _pairs_step_kernel,
        grid=(nblk,),
        in_specs=[here((N_PACK, HEAD_DIM)), at_layer(sq), at_layer((1, HEAD_DIM)), at_layer(sq), at_layer(sq)],
        out_specs=[here(sq), here(sq), here(sq), here((N_OPACK, HEAD_DIM))],
        out_shape=[state, state, state, jax.ShapeDtypeStruct((npairs, N_OPACK, HEAD_DIM), F32)],
        compiler_params=_cparams("arbitrary"),
        name="pairs_step",
    )(pack, ml_c, ml_n, gla_s, rw_s)


def _step_post_kernel(hml_ref, mo_ref, ogla_ref, gg_ref, orw_ref, g_ref, bonus_ref,
                      mln_ref, gln_ref, rwn_ref, yml_out, ygla_out, yrw_out):
    ones = _head_ones()
    hml = hml_ref[...]
    ms = _head_sum(hml * hml, ones) * (1.0 / HEAD_DIM)
    yml_out[...] = (hml * lax.rsqrt(ms + EPS) * mln_ref[...] * _sigmoid(mo_ref[...])).astype(BF16)
    og = ogla_ref[...]
    ms = _head_sum(og * og, ones) * (1.0 / HEAD_DIM)
    gg = gg_ref[...]
    ygla_out[...] = (og * lax.rsqrt(ms + EPS) * gln_ref[...] * (gg * _sigmoid(gg))).astype(BF16)
    yrw_out[...] = ((_head_layernorm(orw_ref[...], rwn_ref[...], ones) + bonus_ref[...])
                    * g_ref[...]).astype(BF16)


def _step_post(hml, mo, ogla, gg, orw, g, bonus, p):
    n = hml.shape[0]
    args = (hml, mo, ogla, gg, orw, g, bonus, p["ml_norm"], p["gla_norm"], p["rw_norm"])
    full = lambda a: pl.BlockSpec(a.shape, lambda i: (0,) * a.ndim)
    out = jax.ShapeDtypeStruct((n, W_GROUP), BF16)
    return pl.pallas_call(
        _step_post_kernel,
        grid=(1,),
        in_specs=[full(a) for a in args],
        out_specs=[pl.BlockSpec((n, W_GROUP), lambda i: (0, 0))] * 3,
        out_shape=[out] * 3,
        compiler_params=_cparams("arbitrary"),
        name="step_post",
    )(*args)


def _layer_params(l, prm):
    p = {}
    w_in = prm["w_in"][l]
    d = w_in.shape[0]
    sizes = (W_GROUP, W_GROUP, W_GROUP, W_GROUP, N_HEADS, N_HEADS, W_GROUP, W_GROUP, W_GROUP, W_GROUP,
             GLA_RANK, W_GROUP, RWKV_COLS)
    offs = [0]
    for s in sizes:
        offs.append(offs[-1] + s)
    col = lambda i: w_in[:, offs[i]:offs[i + 1]]
    (u, mq, mk, mv, mi, mf, mo, gq, gk, gv, ga, gg, rcols) = [col(i) for i in range(13)]
    small = jnp.concatenate([ga, mi, mf, jnp.zeros((d, SMALL_W - GLA_RANK - 2 * N_HEADS), F32)], axis=1)
    p["w_in"] = jnp.concatenate([rcols, small, u, mq, mk, mv, mo, gq, gk, gv, gg], axis=1).astype(BF16)
    p["w_gate_t"] = jnp.concatenate([mi, mf], axis=1).T.astype(BF16)
    p["norm_mix"] = prm["norm_mix"][l][None, :]

    lam = lax.complex(prm["s5_lam_re"][l], prm["s5_lam_im"][l])
    dt = jnp.exp(prm["s5_log_dt"][l])[:, None]
    lam_bar = jnp.exp(lam * dt)
    b_bar = ((lam_bar - 1.0) / lam)[..., None] * lax.complex(prm["s5_b_re"][l], prm["s5_b_im"][l])
    eye = jnp.eye(S5_GROUPS, dtype=F32)
    bm = lambda b: jnp.einsum("gph,gk->ghkp", b, eye).reshape(W_GROUP, S5_WIDTH)
    p["bmat"] = jnp.concatenate([bm(b_bar.real), bm(b_bar.imag)], axis=1).astype(BF16)
    cm = lambda c: jnp.einsum("ghp,gk->kpgh", c, eye).reshape(S5_WIDTH, W_GROUP)
    p["cmat"] = jnp.concatenate([cm(prm["s5_c_re"][l]), -cm(prm["s5_c_im"][l])], axis=0).astype(BF16)
    p["lre"] = lam_bar.real.reshape(1, S5_WIDTH)
    p["lim"] = lam_bar.imag.reshape(1, S5_WIDTH)
    p["d"] = prm["s5_d"][l][None, :]
    p["wglu"] = prm["s5_w_glu"][l].astype(BF16)

    bias = prm["ml_gate_bias"][l]
    p["ml_brow"] = jnp.zeros((1, SMALL_W), F32).at[0, LANE_MI:LANE_MI + 2 * N_HEADS].set(bias)
    p["ml_bcol"] = jnp.broadcast_to(bias[:, None], (2 * N_HEADS, CHUNK_ML))
    p["ml_norm"] = prm["ml_norm"][l][None, :]

    p["gla_wa"] = jnp.zeros((SMALL_W, W_GROUP), F32).at[0:GLA_RANK].set(prm["gla_w_alpha"][l]).astype(BF16)
    p["gla_ba"] = prm["gla_b_alpha"][l][None, :]
    p["gla_norm"] = prm["gla_norm"][l][None, :]

    p["rw_mu"] = prm["rw_mu"][l][None, :]
    p["rw_w0"] = prm["rw_w0"][l][None, :]
    p["rw_a0"] = prm["rw_a0"][l][None, :]
    p["rw_kk"] = prm["rw_k_k"][l][None, :]
    p["rw_ka"] = prm["rw_k_a"][l][None, :]
    p["rw_rk"] = prm["rw_r_k"][l].reshape(1, W_GROUP)
    z = jnp.zeros((LANES, W_GROUP), F32)
    p["rw_w2"] = z.at[0:RWKV_W_RANK].set(prm["rw_w2"][l]).astype(BF16)
    p["rw_a2"] = z.at[RWKV_W_RANK:RWKV_W_RANK + RWKV_A_RANK].set(prm["rw_a2"][l]).astype(BF16)
    p["rw_g2"] = z.at[RWKV_W_RANK + RWKV_A_RANK:].set(prm["rw_g2"][l]).astype(BF16)
    p["rw_norm"] = prm["rw_norm"][l][None, :]

    p["layer"] = l
    p["w_out"] = prm["w_out"].astype(BF16)
    p["norm_ffn"] = prm["norm_ffn"][l][None, :]
    p["ffn_wup"] = prm["ffn_w_up"].astype(BF16)
    p["ffn_cw"] = prm["ffn_conv_w"][l]
    p["ffn_cb"] = prm["ffn_conv_b"][l][None, :]
    p["ffn_wdn"] = prm["ffn_w_down"].astype(BF16)
    return p


def _diag_blocks(m):
    b = m.shape[0]
    m = m.reshape(b, N_HEADS, HEAD_DIM, N_HEADS, HEAD_DIM)
    return jnp.stack([m[:, h, :, h, :] for h in range(N_HEADS)], axis=1)


def _prompt_layer(x2d, p, norm_final, n_seq, t_len, final):
    proj, gt = _inproj(x2d, p["norm_mix"], p["w_in"], p["w_gate_t"], n_seq, t_len, tm=512)

    u_tb = proj[:, OFF_MAIN:OFF_MAIN + W_GROUP].reshape(n_seq, t_len, W_GROUP).transpose(1, 0, 2)
    zeros = jnp.zeros((n_seq, S5_WIDTH), F32)
    y_s5_tb, s5_re, s5_im = _s5(u_tb.reshape(t_len * n_seq, W_GROUP), zeros, zeros, p, n_seq, t_len, tt=128)
    y_s5 = y_s5_tb.reshape(t_len, n_seq, W_GROUP).transpose(1, 0, 2).reshape(n_seq * t_len, W_GROUP)

    y_ml, ml_c, ml_n, ml_m = _mlstm(proj, gt, p, n_seq, t_len)
    y_gla, gla_st = _gla(proj, p, n_seq, t_len)
    y_rw, rw_s, rw_shift = _rwkv(proj, p, n_seq, t_len)

    x_new, ffn_buf = _ffn_seq(x2d, (y_s5, y_ml, y_gla, y_rw), p, norm_final, n_seq, t_len, tm=512,
                              final=final)
    states = (
        s5_re.reshape(n_seq, S5_GROUPS, S5_STATE),
        s5_im.reshape(n_seq, S5_GROUPS, S5_STATE),
        _diag_blocks(ml_c),
        ml_n[:, 0, :].reshape(n_seq, N_HEADS, HEAD_DIM),
        ml_m[:, 0:N_HEADS, 0],
        jnp.swapaxes(_diag_blocks(gla_st), -1, -2),
        _diag_blocks(rw_s),
        rw_shift[:, 0, :],
        ffn_buf,
    )
    return x_new, states


def _sample_layer(x2d, st, pair_states, layer, p, norm_final, final):
    (s5_re, s5_im, ml_m, rw_shift, ffn_buf) = st
    n = x2d.shape[0]
    npairs = n * N_HEADS
    proj, _ = _inproj(x2d, p["norm_mix"], p["w_in"], p["w_gate_t"], 1, n, tm=n)

    y_s5, s5_re_new, s5_im_new = _s5(proj[:, OFF_MAIN:OFF_MAIN + W_GROUP],
                                     s5_re.reshape(n, S5_WIDTH), s5_im.reshape(n, S5_WIDTH),
                                     p, n, 1, tt=1)

    m_pad = jnp.zeros((n, SMALL_W), F32).at[:, LANE_MI:LANE_MI + N_HEADS].set(ml_m)
    pack, m_new, g, bonus = _step_vectors(proj, rw_shift, m_pad, p)
    pack = pack.reshape(N_PACK, n, N_HEADS, HEAD_DIM).transpose(1, 2, 0, 3).reshape(npairs, N_PACK, HEAD_DIM)
    ml_c_new, gla_s_new, rw_s_new, opack = _pairs_step(pack, *pair_states, layer)
    opack = opack.reshape(n, N_HEADS, N_OPACK, HEAD_DIM).transpose(2, 0, 1, 3).reshape(N_OPACK, n, W_GROUP)

    main = lambda blk: proj[:, blk * W_GROUP:(blk + 1) * W_GROUP]
    y_ml, y_gla, y_rw = _step_post(opack[OROW_ML], main(BLK_MO), opack[OROW_GLA], main(BLK_GG),
                                   opack[OROW_RW], g, bonus, p)
    x_new, ug = _ffn_step(x2d, (y_s5, y_ml, y_gla, y_rw), p, norm_final, ffn_buf[:, 0, :], ffn_buf[:, 1, :],
                          final)
    states = (
        s5_re_new.reshape(n, S5_GROUPS, S5_STATE),
        s5_im_new.reshape(n, S5_GROUPS, S5_STATE),
        ml_c_new.reshape(n, N_HEADS, HEAD_DIM, HEAD_DIM),
        opack[OROW_N].reshape(n, N_HEADS, HEAD_DIM),
        m_new[:, LANE_MI:LANE_MI + N_HEADS],
        gla_s_new.reshape(n, N_HEADS, HEAD_DIM, HEAD_DIM),
        rw_s_new.reshape(n, N_HEADS, HEAD_DIM, HEAD_DIM),
        proj[:, 0:RWKV_COLS],
        jnp.stack([ffn_buf[:, 1, :], ug], axis=1),
    )
    return x_new, states


def kernel(x_prompt, x_sample, state_s5_re, state_s5_im, state_mlstm_C, state_mlstm_n, state_mlstm_m, state_gla_S, state_rwkv_S, state_rwkv_shift, state_ffn_conv, norm_mix, w_in, s5_lam_re, s5_lam_im, s5_log_dt, s5_b_re, s5_b_im, s5_c_re, s5_c_im, s5_d, s5_w_glu, ml_gate_bias, ml_norm, gla_w_alpha, gla_b_alpha, gla_norm, rw_mu, rw_w0, rw_w2, rw_a0, rw_a2, rw_g2, rw_k_k, rw_k_a, rw_r_k, rw_norm, w_out, norm_ffn, ffn_w_up, ffn_conv_w, ffn_conv_b, ffn_w_down, norm_final):
    prm = dict(norm_mix=norm_mix, w_in=w_in, s5_lam_re=s5_lam_re, s5_lam_im=s5_lam_im, s5_log_dt=s5_log_dt,
               s5_b_re=s5_b_re, s5_b_im=s5_b_im, s5_c_re=s5_c_re, s5_c_im=s5_c_im, s5_d=s5_d,
               s5_w_glu=s5_w_glu, ml_gate_bias=ml_gate_bias, ml_norm=ml_norm, gla_w_alpha=gla_w_alpha,
               gla_b_alpha=gla_b_alpha, gla_norm=gla_norm, rw_mu=rw_mu, rw_w0=rw_w0, rw_w2=rw_w2,
               rw_a0=rw_a0, rw_a2=rw_a2, rw_g2=rw_g2, rw_k_k=rw_k_k, rw_k_a=rw_k_a, rw_r_k=rw_r_k,
               rw_norm=rw_norm, w_out=w_out, norm_ffn=norm_ffn, ffn_w_up=ffn_w_up, ffn_conv_w=ffn_conv_w,
               ffn_conv_b=ffn_conv_b, ffn_w_down=ffn_w_down)
    depth = w_in.shape[0]
    n_seq, t_len, d = x_prompt.shape
    n_smp = x_sample.shape[0]
    assert t_len % 512 == 0 and x_sample.shape[1] == 1
    nf = norm_final[None, :]
    small_states = (state_s5_re, state_s5_im, state_mlstm_m, state_rwkv_shift, state_ffn_conv)
    pairs = depth * n_smp * N_HEADS
    pair_states = (state_mlstm_C.reshape(pairs, HEAD_DIM, HEAD_DIM),
                   state_mlstm_n.reshape(pairs, 1, HEAD_DIM),
                   state_gla_S.reshape(pairs, HEAD_DIM, HEAD_DIM),
                   state_rwkv_S.reshape(pairs, HEAD_DIM, HEAD_DIM))

    xp = x_prompt.reshape(n_seq * t_len, d)
    xs = x_sample.reshape(n_smp, d)
    p_states, s_states = [], []
    for l in range(depth):
        p = _layer_params(l, prm)
        final = l == depth - 1
        xp, ps = _prompt_layer(xp, p, nf, n_seq, t_len, final)
        xs, ss = _sample_layer(xs, tuple(s[l] for s in small_states), pair_states, l, p, nf, final)
        p_states.append(ps)
        s_states.append(ss)
    new_p = tuple(jnp.stack([st[i] for st in p_states]) for i in range(9))
    new_s = tuple(jnp.stack([st[i] for st in s_states]) for i in range(9))
    return (xp.reshape(n_seq, t_len, d), xs.reshape(n_smp, 1, d)) + new_p + new_s
```

```python
import functools
import math

import jax
import jax.numpy as jnp
from jax import lax
from jax.experimental import pallas as pl
from jax.experimental.pallas import tpu as pltpu

F32 = jnp.float32
BF16 = jnp.bfloat16

LANES = 128
W_GROUP = 256
HEAD_DIM = 64
HEAD_SHIFT = 6
N_HEADS = 4
S5_CH = 16
S5_GROUPS = 16
S5_STATE = 64
S5_WIDTH = S5_GROUPS * S5_STATE
GLA_RANK = 16
GLA_GATE_TEMP = 16.0
RWKV_W_RANK = 32
RWKV_A_RANK = 32
RWKV_G_RANK = 64
RWKV_COLS = 3 * W_GROUP + RWKV_W_RANK + RWKV_A_RANK + RWKV_G_RANK
RWKV_DECAY_SCALE = 0.6065306597126334
CONV_W = 3
EPS = 1e-6
GN_EPS = 64e-5
NEG_BIG = -1e30
EXP_CLAMP = 80.0

SMALL_W = LANES
OFF_SMALL = RWKV_COLS
OFF_MAIN = RWKV_COLS + SMALL_W
D_PROJ = OFF_MAIN + 9 * W_GROUP
LANE_MI = GLA_RANK
LANE_MF = GLA_RANK + N_HEADS
(BLK_U, BLK_MQ, BLK_MK, BLK_MV, BLK_MO, BLK_GQ, BLK_GK, BLK_GV, BLK_GG) = range(
    OFF_MAIN // W_GROUP, OFF_MAIN // W_GROUP + 9)
BLK_SMALL = OFF_SMALL // SMALL_W

CHUNK_ML = 128
CHUNK_RW = 64
SEQS_PER_STEP = 8
SEQS_PER_STEP_ML = 4
VMEM_LIMIT = 56 * 1024 * 1024


def _cparams(*sem):
    return pltpu.CompilerParams(dimension_semantics=sem, vmem_limit_bytes=VMEM_LIMIT)


def _dot(a, b):
    return jnp.dot(a.astype(BF16), b.astype(BF16), preferred_element_type=F32)


def _dot_nt(a, b):
    return lax.dot_general(a.astype(BF16), b.astype(BF16), (((1,), (1,)), ((), ())),
                           preferred_element_type=F32)


def _dot_tn(a, b):
    return lax.dot_general(a.astype(BF16), b.astype(BF16), (((0,), (0,)), ((), ())),
                           preferred_element_type=F32)


def _split3(x):
    hi = x.astype(BF16)
    r1 = x - hi.astype(F32)
    mid = r1.astype(BF16)
    lo = (r1 - mid.astype(F32)).astype(BF16)
    return hi, mid, lo


def _dot_exact_rhs(a01, x):
    hi, mid, lo = _split3(x)
    f = lambda p: jnp.dot(a01, p, preferred_element_type=F32)
    return f(hi) + f(mid) + f(lo)


def _dot_exact_lhs(x, b01):
    hi, mid, lo = _split3(x)
    f = lambda p: jnp.dot(p, b01, preferred_element_type=F32)
    return f(hi) + f(mid) + f(lo)


def _sigmoid(x):
    return 1.0 / (1.0 + jnp.exp(-x))


def _log_sigmoid(x):
    return jnp.minimum(x, 0.0) - jnp.log(1.0 + jnp.exp(-jnp.abs(x)))


def _gelu(x):
    return 0.5 * x * (1.0 + jnp.tanh(math.sqrt(2.0 / math.pi) * (x + 0.044715 * (x * x * x))))


def _iota(shape, dim):
    return lax.broadcasted_iota(jnp.int32, shape, dim)


def _head_ones():
    r = _iota((W_GROUP, W_GROUP), 0) >> HEAD_SHIFT
    c = _iota((W_GROUP, W_GROUP), 1) >> HEAD_SHIFT
    return jnp.where(r == c, 1.0, 0.0).astype(BF16)


def _head_sum(x, ones):
    return _dot_exact_lhs(x, ones)


def _head_lane_mask(h, rows=1):
    lane = _iota((rows, W_GROUP), 1) >> HEAD_SHIFT
    return jnp.where(lane == h, 1.0, 0.0).astype(F32)


def _rmsnorm_rows(x, g):
    ms = jnp.mean(x * x, axis=-1, keepdims=True)
    return x * lax.rsqrt(ms + EPS) * g


def _tril_ones(n, strict=False):
    r = _iota((n, n), 0)
    c = _iota((n, n), 1)
    return jnp.where((r > c) if strict else (r >= c), 1.0, 0.0).astype(BF16)


def _triu_ones(n):
    r = _iota((n, n), 0)
    c = _iota((n, n), 1)
    return jnp.where(r <= c, 1.0, 0.0).astype(BF16)


def _inproj_kernel(x_ref, g_ref, w_ref, wg_ref, proj_ref, gt_ref):
    hn = _rmsnorm_rows(x_ref[...], g_ref[...]).astype(BF16)
    proj_ref[...] = jnp.dot(hn, w_ref[...], preferred_element_type=F32)
    gt_ref[0] = lax.dot_general(wg_ref[...], hn, (((1,), (1,)), ((), ())),
                                preferred_element_type=F32)


def _inproj(x2d, g, w, wg, n_seq, t_len, tm):
    n, d = x2d.shape
    tiles_per_seq = t_len // tm
    return pl.pallas_call(
        _inproj_kernel,
        grid=(n // tm,),
        in_specs=[
            pl.BlockSpec((tm, d), lambda i: (i, 0)),
            pl.BlockSpec((1, d), lambda i: (0, 0)),
            pl.BlockSpec((d, D_PROJ), lambda i: (0, 0)),
            pl.BlockSpec((8, d), lambda i: (0, 0)),
        ],
        out_specs=[
            pl.BlockSpec((tm, D_PROJ), lambda i: (i, 0)),
            pl.BlockSpec((1, 8, tm), lambda i: (i // tiles_per_seq, 0, i % tiles_per_seq)),
        ],
        out_shape=[
            jax.ShapeDtypeStruct((n, D_PROJ), F32),
            jax.ShapeDtypeStruct((n_seq, 8, t_len), F32),
        ],
        compiler_params=_cparams("arbitrary"),
        name="inproj",
    )(x2d, g, w, wg)


def _s5_kernel(u_ref, h0re_ref, h0im_ref, bmat_ref, lre_ref, lim_ref, cmat_ref, d_ref, wglu_ref,
               y_ref, hre_out, him_out, bu_scr, hs_scr, hre_scr, him_scr, *, bp, tt):
    i = pl.program_id(0)

    @pl.when(i == 0)
    def _():
        hre_scr[...] = h0re_ref[...]
        him_scr[...] = h0im_ref[...]

    u = u_ref[...]
    bu_scr[...] = jnp.dot(u.astype(BF16), bmat_ref[...], preferred_element_type=F32)
    lre = lre_ref[...]
    lim = lim_ref[...]

    def step(t, carry):
        hre, him = carry
        r0 = pl.multiple_of(t * bp, bp)
        bre = bu_scr[pl.ds(r0, bp), 0:S5_WIDTH]
        bim = bu_scr[pl.ds(r0, bp), S5_WIDTH:2 * S5_WIDTH]
        nre = lre * hre - lim * him + bre
        nim = lre * him + lim * hre + bim
        hs_scr[pl.ds(r0, bp), 0:S5_WIDTH] = nre
        hs_scr[pl.ds(r0, bp), S5_WIDTH:2 * S5_WIDTH] = nim
        return nre, nim

    carry = (hre_scr[...], him_scr[...])
    if tt == 1:
        hre, him = step(0, carry)
    else:
        hre, him = lax.fori_loop(0, tt, step, carry)
    hre_scr[...] = hre
    him_scr[...] = him
    hre_out[...] = hre
    him_out[...] = him
    y = jnp.dot(hs_scr[...].astype(BF16), cmat_ref[...], preferred_element_type=F32) + d_ref[...] * u
    z = _gelu(y)
    y_ref[...] = (z * _sigmoid(_dot(z, wglu_ref[...]))).astype(y_ref.dtype)


def _s5(u_tb, h0re, h0im, sp, bp, t_len, tt):
    n = u_tb.shape[0]
    full = lambda shape: pl.BlockSpec(shape, lambda i: (0,) * len(shape))
    return pl.pallas_call(
        functools.partial(_s5_kernel, bp=bp, tt=tt),
        grid=(t_len // tt,),
        in_specs=[
            pl.BlockSpec((tt * bp, W_GROUP), lambda i: (i, 0)),
            full((bp, S5_WIDTH)), full((bp, S5_WIDTH)),
            full((W_GROUP, 2 * S5_WIDTH)),
            full((bp, S5_WIDTH)), full((bp, S5_WIDTH)),
            full((2 * S5_WIDTH, W_GROUP)),
            full((1, W_GROUP)),
            full((W_GROUP, W_GROUP)),
        ],
        out_specs=[
            pl.BlockSpec((tt * bp, W_GROUP), lambda i: (i, 0)),
            full((bp, S5_WIDTH)), full((bp, S5_WIDTH)),
        ],
        out_shape=[
            jax.ShapeDtypeStruct((n, W_GROUP), BF16),
            jax.ShapeDtypeStruct((bp, S5_WIDTH), F32),
            jax.ShapeDtypeStruct((bp, S5_WIDTH), F32),
        ],
        scratch_shapes=[
            pltpu.VMEM((tt * bp, 2 * S5_WIDTH), F32),
            pltpu.VMEM((tt * bp, 2 * S5_WIDTH), F32),
            pltpu.VMEM((bp, S5_WIDTH), F32),
            pltpu.VMEM((bp, S5_WIDTH), F32),
        ],
        compiler_params=_cparams("arbitrary"),
        name="s5_scan",
    )(u_tb, h0re, h0im, sp["bmat"], jnp.broadcast_to(sp["lre"], (bp, S5_WIDTH)),
      jnp.broadcast_to(sp["lim"], (bp, S5_WIDTH)), sp["cmat"], sp["d"], sp["wglu"])


def _run_interleaved(chains):
    live = list(chains)
    while live:
        still = []
        for ch in live:
            try:
                next(ch)
                still.append(ch)
            except StopIteration:
                pass
        live = still


def _mlstm_kernel(q_ref, k_ref, v_ref, og_ref, sm_ref, gt_ref, brow_ref, bcol_ref, norm_ref,
                  y_ref, c_out, n_out, m_out, c_scr, n_scr, m_scr, *, L, nc, G):
    c = pl.program_id(1)

    @pl.when(c == 0)
    def _():
        c_scr[...] = jnp.zeros_like(c_scr)
        n_scr[...] = jnp.zeros_like(n_scr)
        m_scr[...] = jnp.zeros_like(m_scr)

    ones = _head_ones()
    causal = _iota((L, L), 0) >= _iota((L, L), 1)
    lane_head = _iota((1, W_GROUP), 1) >> HEAD_SHIFT
    tril = _tril_ones(L)
    triu = _triu_ones(L)

    def chain(gi):
        q = q_ref[gi]
        k = k_ref[gi] * (HEAD_DIM ** -0.5)
        v = v_ref[gi]
        pre = sm_ref[gi] + brow_ref[...]
        gtb = gt_ref[gi] + bcol_ref[...]
        bc_col = _dot_exact_rhs(tril, _log_sigmoid(pre))
        bc_row = _dot_exact_lhs(_log_sigmoid(gtb), triu)
        yield
        C = c_scr[gi]
        n = n_scr[gi, 0:1, :]
        qC = _dot_nt(q, C)
        qn = q * n
        yield

        hc = jnp.zeros((L, W_GROUP), F32)
        wl_all = jnp.zeros((L, W_GROUP), F32)
        w0l_all = jnp.zeros((1, W_GROUP), F32)
        for h in range(N_HEADS):
            mask = _head_lane_mask(h)
            bcc = bc_col[:, LANE_MF + h:LANE_MF + h + 1]
            bcr = bc_row[N_HEADS + h:N_HEADS + h + 1, :]
            i_row = gtb[h:h + 1, :]
            i_col = pre[:, LANE_MI + h:LANE_MI + h + 1]
            m0 = m_scr[gi, h:h + 1, 0:1]
            qk = _dot_nt(q * mask, k)
            dmat = jnp.where(causal, bcc - bcr + i_row, NEG_BIG)
            g = bcc + m0
            m_row = jnp.maximum(g, jnp.max(dmat, axis=1, keepdims=True))
            w0 = jnp.exp(g - m_row)
            ws = jnp.exp(dmat - m_row) * qk
            yield
            num = w0 * qC + _dot(ws, v * mask)
            den = (w0 * jnp.sum(qn * mask, axis=1, keepdims=True)
                   + jnp.sum(ws, axis=1, keepdims=True))
            hc = hc + mask * (num / jnp.maximum(jnp.abs(den), jnp.exp(-m_row)))
            m_new = m_row[L - 1:L, :]
            b_last = bcc[L - 1:L, :]
            wl_all = wl_all + mask * jnp.exp(b_last - bcc + i_col - m_new)
            w0l_all = w0l_all + mask * jnp.exp(b_last + m0 - m_new)
            m_scr[gi, h:h + 1, :] = jnp.broadcast_to(m_new, (1, LANES))
            yield

        kw = k * wl_all
        c_scr[gi] = w0l_all * C + ones.astype(F32) * _dot_tn(v, kw)
        n_scr[gi, 0:1, :] = w0l_all * n + jnp.sum(kw, axis=0, keepdims=True)
        yield
        ms = _head_sum(hc * hc, ones) * (1.0 / HEAD_DIM)
        y = hc * lax.rsqrt(ms + EPS) * norm_ref[...] * _sigmoid(og_ref[gi])
        y_ref[gi] = y.astype(y_ref.dtype)

    _run_interleaved([chain(gi) for gi in range(G)])

    @pl.when(c == nc - 1)
    def _():
        c_out[...] = c_scr[...]
        n_out[...] = jnp.broadcast_to(n_scr[:, 0:1, :], n_out.shape)
        m_out[...] = m_scr[...]


def _seq_spec(G, L, blk, width=W_GROUP):
    return pl.BlockSpec((G, L, width), lambda b, c: (b, c, blk))


def _mlstm(proj, gt, p, n_seq, t_len):
    L = CHUNK_ML
    G = SEQS_PER_STEP_ML
    nc = t_len // L
    const = lambda shape: pl.BlockSpec(shape, lambda b, c: (0,) * len(shape))
    per_seq = lambda rows, width: pl.BlockSpec((G, rows, width), lambda b, c: (b, 0, 0))
    proj3 = proj.reshape(n_seq, t_len, D_PROJ)
    y, c_new, n_new, m_new = pl.pallas_call(
        functools.partial(_mlstm_kernel, L=L, nc=nc, G=G),
        grid=(n_seq // G, nc),
        in_specs=[
            _seq_spec(G, L, BLK_MQ), _seq_spec(G, L, BLK_MK), _seq_spec(G, L, BLK_MV),
            _seq_spec(G, L, BLK_MO), _seq_spec(G, L, BLK_SMALL, SMALL_W),
            pl.BlockSpec((G, 8, L), lambda b, c: (b, 0, c)),
            const((1, SMALL_W)), const((8, L)), const((1, W_GROUP)),
        ],
        out_specs=[
            pl.BlockSpec((G, L, W_GROUP), lambda b, c: (b, c, 0)),
            per_seq(W_GROUP, W_GROUP), per_seq(8, W_GROUP), per_seq(8, LANES),
        ],
        out_shape=[
            jax.ShapeDtypeStruct((n_seq, t_len, W_GROUP), BF16),
            jax.ShapeDtypeStruct((n_seq, W_GROUP, W_GROUP), F32),
            jax.ShapeDtypeStruct((n_seq, 8, W_GROUP), F32),
            jax.ShapeDtypeStruct((n_seq, 8, LANES), F32),
        ],
        scratch_shapes=[
            pltpu.VMEM((G, W_GROUP, W_GROUP), F32),
            pltpu.VMEM((G, 8, W_GROUP), F32),
            pltpu.VMEM((G, 8, LANES), F32),
        ],
        compiler_params=_cparams("arbitrary", "arbitrary"),
        name="mlstm_chunk",
    )(proj3, proj3, proj3, proj3, proj3, gt, p["ml_brow"], p["ml_bcol"], p["ml_norm"])
    return y.reshape(n_seq * t_len, W_GROUP), c_new, n_new, m_new


def _gla_kernel(q_ref, k_ref, v_ref, gg_ref, sm_ref, wa_ref, ba_ref, norm_ref,
                y_ref, s_out, s_scr, *, L, nc, G):
    c = pl.program_id(1)

    @pl.when(c == 0)
    def _():
        s_scr[...] = jnp.zeros_like(s_scr)

    ones = _head_ones()
    tril = _tril_ones(L)
    causal = _iota((L, N_HEADS * L), 0) >= (_iota((L, N_HEADS * L), 1) & (L - 1))

    def chain(gi):
        q = q_ref[gi] * (HEAD_DIM ** -0.5)
        k = k_ref[gi]
        v = v_ref[gi]
        la = _log_sigmoid(_dot(sm_ref[gi], wa_ref[...]) + ba_ref[...]) * (1.0 / GLA_GATE_TEMP)
        yield
        bc = _dot_exact_rhs(tril, la)
        yield
        qs = q * jnp.exp(bc)
        kh = k * jnp.exp(jnp.minimum(-bc, EXP_CLAMP))
        ST = s_scr[gi]
        att = jnp.where(causal, _dot_nt(qs, _stack_heads(kh)), 0.0)
        from_state = _dot_nt(qs, ST)
        yield
        o = from_state + _dot(att, _stack_heads(v))
        b_last = bc[L - 1:L, :]
        kbar = k * jnp.exp(b_last - bc)
        s_scr[gi] = ST * jnp.exp(b_last) + ones.astype(F32) * _dot_tn(v, kbar)
        yield
        ms = _head_sum(o * o, ones) * (1.0 / HEAD_DIM)
        gg = gg_ref[gi]
        y = o * lax.rsqrt(ms + EPS) * norm_ref[...] * (gg * _sigmoid(gg))
        y_ref[gi] = y.astype(y_ref.dtype)

    _run_interleaved([chain(gi) for gi in range(G)])

    @pl.when(c == nc - 1)
    def _():
        s_out[...] = s_scr[...]


def _gla(proj, p, n_seq, t_len):
    L = CHUNK_ML
    G = SEQS_PER_STEP_ML
    nc = t_len // L
    const = lambda shape: pl.BlockSpec(shape, lambda b, c: (0,) * len(shape))
    proj3 = proj.reshape(n_seq, t_len, D_PROJ)
    y, s_new = pl.pallas_call(
        functools.partial(_gla_kernel, L=L, nc=nc, G=G),
        grid=(n_seq // G, nc),
        in_specs=[
            _seq_spec(G, L, BLK_GQ), _seq_spec(G, L, BLK_GK), _seq_spec(G, L, BLK_GV),
            _seq_spec(G, L, BLK_GG), _seq_spec(G, L, BLK_SMALL, SMALL_W),
            const((SMALL_W, W_GROUP)), const((1, W_GROUP)), const((1, W_GROUP)),
        ],
        out_specs=[
            pl.BlockSpec((G, L, W_GROUP), lambda b, c: (b, c, 0)),
            pl.BlockSpec((G, W_GROUP, W_GROUP), lambda b, c: (b, 0, 0)),
        ],
        out_shape=[
            jax.ShapeDtypeStruct((n_seq, t_len, W_GROUP), BF16),
            jax.ShapeDtypeStruct((n_seq, W_GROUP, W_GROUP), F32),
        ],
        scratch_shapes=[pltpu.VMEM((G, W_GROUP, W_GROUP), F32)],
        compiler_params=_cparams("arbitrary", "arbitrary"),
        name="gla_chunk",
    )(proj3, proj3, proj3, proj3, proj3, p["gla_wa"], p["gla_ba"], p["gla_norm"])
    return y.reshape(n_seq * t_len, W_GROUP), s_new


def _rwkv_vectors(rc, prev, p_mu, p_w0, p_a0, p_kk, p_ka, p_rk, w2, a2, g2, ones):
    xm = rc + p_mu * (prev - rc)
    rr = xm[:, 0:W_GROUP]
    rk = xm[:, W_GROUP:2 * W_GROUP]
    rv = xm[:, 2 * W_GROUP:3 * W_GROUP]
    tail = xm[:, 3 * W_GROUP:RWKV_COLS]
    lw = -RWKV_DECAY_SCALE * _sigmoid(p_w0 + _dot(jnp.tanh(tail), w2))
    a = _sigmoid(p_a0 + _dot(tail, a2))
    g = _dot(_sigmoid(tail), g2)
    kk = rk * p_kk
    kk = kk * lax.rsqrt(jnp.maximum(_head_sum(kk * kk, ones), 1e-24))
    kt = rk * (1.0 + (a - 1.0) * p_ka)
    bonus = _head_sum(rr * kt * p_rk, ones) * rv
    return rr, lw, kt, rv, kk, a, g, bonus


def _head_layernorm(o, g, ones):
    mu = _head_sum(o, ones) * (1.0 / HEAD_DIM)
    oc = o - mu
    var = _head_sum(oc * oc, ones) * (1.0 / HEAD_DIM)
    return oc * lax.rsqrt(var + GN_EPS) * g


def _stack_heads(x):
    xb = x.astype(BF16)
    lane_head = _iota((1, W_GROUP), 1) >> HEAD_SHIFT
    return jnp.concatenate([jnp.where(lane_head == h, xb, jnp.zeros_like(xb)) for h in range(N_HEADS)],
                           axis=0)


def _each(f, *seqs):
    return [f(*args) for args in zip(*seqs)]


def _block_mm(x_list, y_list):
    return _each(lambda x, y: _dot(x, _stack_heads(y)), x_list, y_list)


def _unit_lower_inverse(a_list, L):
    t_idx = _iota((L, N_HEADS * L), 0)
    s_idx = _iota((L, N_HEADS * L), 1) & (L - 1)
    eye = jnp.where(t_idx == s_idx, 1.0, 0.0).astype(F32)
    in16 = (t_idx >> 4) == (s_idx >> 4)
    in32 = ((t_idx >> 5) == (s_idx >> 5)) & ((t_idx >> 4) > (s_idx >> 4))
    in64 = (t_idx >> 5) > (s_idx >> 5)
    x16 = _each(lambda a: jnp.where(in16, -a, 0.0), a_list)
    x2 = _block_mm(x16, x16)
    x4 = _block_mm(x2, x2)
    x8 = _block_mm(x4, x4)
    t = _each(lambda x: eye + x, x16)
    for xp in (x2, x4, x8):
        t = _each(lambda t_, d: t_ + d, t, _block_mm(t, xp))
    for sel in (in32, in64):
        w = _block_mm(_each(lambda a: jnp.where(sel, a, 0.0), a_list), t)
        t = _each(lambda t_, d: t_ - d, t, _block_mm(t, w))
    return t


def _rwkv_chunks(rcs, prev_rows, states, prm, ones, L):
    (mu, w0, a0, pkk, pka, prk, w2, a2, g2, norm) = prm
    first_row = _iota((L, RWKV_COLS), 0) == 0
    prevs = _each(lambda rc, pr: jnp.where(first_row, pr, pltpu.roll(rc, 1, 0)), rcs, prev_rows)
    vecs = _each(lambda rc, pv: _rwkv_vectors(rc, pv, mu, w0, a0, pkk, pka, prk, w2, a2, g2, ones),
                 rcs, prevs)
    r, lw, kx, v, kk, a, g, bonus = [list(col) for col in zip(*vecs)]

    tril = _tril_ones(L)
    lc = _each(lambda x: _dot_exact_rhs(tril, x), lw)
    akk = _each(lambda a_, k_: a_ * k_, a, kk)
    e_neg = _each(lambda x: jnp.exp(-x), lc)
    l_last = _each(lambda x: x[L - 1:L, :], lc)
    e_end = _each(lambda ll, x: jnp.exp(ll - x), l_last, lc)
    kr = _each(lambda k_, r_, c_, w_: jnp.concatenate([k_ * jnp.exp(c_ - w_), r_ * jnp.exp(c_)], axis=0),
               kk, r, lc, lw)
    a_hat = _each(lambda x, e: _stack_heads(x * e), akk, e_neg)
    k_hat = _each(lambda x, e: _stack_heads(x * e), kx, e_neg)
    v_bd = _each(_stack_heads, v)

    t_idx = _iota((L, N_HEADS * L), 0)
    s_idx = _iota((L, N_HEADS * L), 1) & (L - 1)
    strict = t_idx > s_idx
    incl = t_idx >= s_idx
    pa = _each(_dot_nt, kr, a_hat)
    pk = _each(_dot_nt, kr, k_hat)
    a_ua = _each(lambda p_: jnp.where(strict, p_[0:L], 0.0), pa)
    b_ra = _each(lambda p_: jnp.where(incl, p_[L:2 * L], 0.0), pa)
    ab_k = _each(lambda p_: jnp.concatenate([jnp.where(strict, p_[0:L], 0.0),
                                             jnp.where(incl, p_[L:2 * L], 0.0)], axis=0), pk)
    T = _unit_lower_inverse(a_ua, L)

    from_state = _each(_dot_nt, kr, states)
    from_v = _each(_dot, ab_k, v_bd)
    rhs = _each(lambda s_, v_: s_[0:L] + v_[0:L], from_state, from_v)
    U = _block_mm(T, rhs)
    corr = _block_mm(b_ra, U)
    o = _each(lambda s_, v_, c_: s_[L:2 * L] + v_[L:2 * L] - c_, from_state, from_v, corr)
    upd = _each(lambda v_, u_, k_, a_, e: _dot_tn(jnp.concatenate([v_, u_], axis=0),
                                                  jnp.concatenate([k_ * e, -(a_ * e)], axis=0)),
                v, U, kx, akk, e_end)
    s_new = _each(lambda s_, ll, d: s_ * jnp.exp(ll) + ones.astype(F32) * d, states, l_last, upd)
    y = _each(lambda o_, b_, g_: (_head_layernorm(o_, norm, ones) + b_) * g_, o, bonus, g)
    return y, s_new


def _rwkv_kernel(rc_ref, mu_ref, w0_ref, a0_ref, kk_ref, ka_ref, rk_ref, w2_ref, a2_ref, g2_ref,
                 norm_ref, y_ref, s_out, shift_out, s_scr, prev_scr, *, L, nc, G):
    c = pl.program_id(1)

    @pl.when(c == 0)
    def _():
        s_scr[...] = jnp.zeros_like(s_scr)
        prev_scr[...] = jnp.zeros_like(prev_scr)

    ones = _head_ones()
    prm = (mu_ref[...], w0_ref[...], a0_ref[...], kk_ref[...], ka_ref[...], rk_ref[...],
           w2_ref[...], a2_ref[...], g2_ref[...], norm_ref[...])
    rcs = [rc_ref[gi] for gi in range(G)]
    ys, s_new = _rwkv_chunks(rcs, [prev_scr[gi, 0:1, :] for gi in range(G)],
                             [s_scr[gi] for gi in range(G)], prm, ones, L)
    for gi in range(G):
        prev_scr[gi, 0:1, :] = rcs[gi][L - 1:L, :]
        s_scr[gi] = s_new[gi]
        y_ref[gi] = ys[gi].astype(y_ref.dtype)

    @pl.when(c == nc - 1)
    def _():
        s_out[...] = s_scr[...]
        shift_out[...] = jnp.broadcast_to(prev_scr[:, 0:1, :], shift_out.shape)


def _rwkv(proj, p, n_seq, t_len):
    L = CHUNK_RW
    G = SEQS_PER_STEP
    nc = t_len // L
    const = lambda shape: pl.BlockSpec(shape, lambda b, c: (0,) * len(shape))
    row = const((1, W_GROUP))
    pad = const((LANES, W_GROUP))
    y, s_new, shift = pl.pallas_call(
        functools.partial(_rwkv_kernel, L=L, nc=nc, G=G),
        grid=(n_seq // G, nc),
        in_specs=[
            pl.BlockSpec((G, L, RWKV_COLS), lambda b, c: (b, c, 0)),
            const((1, RWKV_COLS)), row, row, row, row, row, pad, pad, pad, row,
        ],
        out_specs=[
            pl.BlockSpec((G, L, W_GROUP), lambda b, c: (b, c, 0)),
            pl.BlockSpec((G, W_GROUP, W_GROUP), lambda b, c: (b, 0, 0)),
            pl.BlockSpec((G, 8, RWKV_COLS), lambda b, c: (b, 0, 0)),
        ],
        out_shape=[
            jax.ShapeDtypeStruct((n_seq, t_len, W_GROUP), BF16),
            jax.ShapeDtypeStruct((n_seq, W_GROUP, W_GROUP), F32),
            jax.ShapeDtypeStruct((n_seq, 8, RWKV_COLS), F32),
        ],
        scratch_shapes=[
            pltpu.VMEM((G, W_GROUP, W_GROUP), F32),
            pltpu.VMEM((G, 8, RWKV_COLS), F32),
        ],
        compiler_params=_cparams("arbitrary", "arbitrary"),
        name="rwkv_chunk",
    )(proj.reshape(n_seq, t_len, D_PROJ), p["rw_mu"], p["rw_w0"], p["rw_a0"], p["rw_kk"], p["rw_ka"],
      p["rw_rk"], p["rw_w2"], p["rw_a2"], p["rw_g2"], p["rw_norm"])
    return y.reshape(n_seq * t_len, W_GROUP), s_new, shift


FF_CHUNK = 256


def _mix_residual(x_ref, ys, wout_ref):
    acc = x_ref[...]
    for j, y in enumerate(ys):
        acc = acc + jnp.dot(y.astype(BF16), wout_ref[j * W_GROUP:(j + 1) * W_GROUP, :],
                            preferred_element_type=F32)
    return acc


def _ffn_body(x1, nrm_ref, wup_ref, cw_ref, cb_ref, wdn_ref, prev_rows, d_ff, act_scr):
    h2 = _rmsnorm_rows(x1, nrm_ref[...]).astype(BF16)
    for j in range(d_ff // FF_CHUNK):
        lo, hi = j * FF_CHUNK, (j + 1) * FF_CHUNK
        ug = jnp.dot(h2, wup_ref[:, lo:hi], preferred_element_type=F32)
        uv = jnp.dot(h2, wup_ref[:, d_ff + lo:d_ff + hi], preferred_element_type=F32)
        p2, p1 = prev_rows(j, ug)
        conv = (cb_ref[:, lo:hi] + cw_ref[0:1, lo:hi] * p2 + cw_ref[1:2, lo:hi] * p1
                + cw_ref[2:3, lo:hi] * ug)
        act_scr[:, lo:hi] = (_gelu(conv) * uv).astype(BF16)
    return x1 + jnp.dot(act_scr[...], wdn_ref[...], preferred_element_type=F32)


def _ffn_seq_kernel(x_ref, y0_ref, y1_ref, y2_ref, y3_ref, wout_ref, nrm_ref, wup_ref, cw_ref, cb_ref,
                    wdn_ref, nf_ref, o_ref, buf_out, carry_scr, act_scr, *, tm, d_ff, final):
    t = pl.program_id(1)

    @pl.when(t == 0)
    def _():
        carry_scr[...] = jnp.zeros_like(carry_scr)

    x1 = _mix_residual(x_ref, (y0_ref[...], y1_ref[...], y2_ref[...], y3_ref[...]), wout_ref)
    row = _iota((tm, FF_CHUNK), 0)

    def prev_rows(j, ug):
        lo, hi = j * FF_CHUNK, (j + 1) * FF_CHUNK
        c0 = carry_scr[0:1, lo:hi]
        c1 = carry_scr[1:2, lo:hi]
        p1 = jnp.where(row == 0, c1, pltpu.roll(ug, 1, 0))
        p2 = jnp.where(row == 0, c0, jnp.where(row == 1, c1, pltpu.roll(ug, 2, 0)))
        carry_scr[0:2, lo:hi] = ug[tm - 2:tm, :]
        return p2, p1

    out = _ffn_body(x1, nrm_ref, wup_ref, cw_ref, cb_ref, wdn_ref, prev_rows, d_ff, act_scr)
    if final:
        out = _rmsnorm_rows(out, nf_ref[...])
    o_ref[...] = out
    buf_out[0] = carry_scr[...]


def _ffn_step_kernel(x_ref, y0_ref, y1_ref, y2_ref, y3_ref, wout_ref, nrm_ref, wup_ref, cw_ref, cb_ref,
                     wdn_ref, nf_ref, p2_ref, p1_ref, o_ref, ug_out, act_scr, *, d_ff, final):
    x1 = _mix_residual(x_ref, (y0_ref[...], y1_ref[...], y2_ref[...], y3_ref[...]), wout_ref)

    def prev_rows(j, ug):
        lo, hi = j * FF_CHUNK, (j + 1) * FF_CHUNK
        ug_out[:, lo:hi] = ug
        return p2_ref[:, lo:hi], p1_ref[:, lo:hi]

    out = _ffn_body(x1, nrm_ref, wup_ref, cw_ref, cb_ref, wdn_ref, prev_rows, d_ff, act_scr)
    if final:
        out = _rmsnorm_rows(out, nf_ref[...])
    o_ref[...] = out


def _ffn_seq(x2d, ys, p, norm_final, n_seq, t_len, tm, final):
    n, d = x2d.shape
    d_ff = p["ffn_wdn"].shape[1]
    lay = p["layer"]
    nt = t_len // tm
    rows = lambda w: pl.BlockSpec((tm, w), lambda b, t: (b * nt + t, 0))
    const = lambda shape: pl.BlockSpec(shape, lambda b, t: (0,) * len(shape))
    layered = lambda shape: pl.BlockSpec((None,) + shape, lambda b, t: (lay, 0, 0),
                                         pipeline_mode=pl.Buffered(1))
    return pl.pallas_call(
        functools.partial(_ffn_seq_kernel, tm=tm, d_ff=d_ff, final=final),
        grid=(n_seq, nt),
        in_specs=[rows(d)] + [rows(W_GROUP)] * 4 + [
            layered((d, d)), const((1, d)), layered((d, 2 * d_ff)), const((CONV_W, d_ff)),
            const((1, d_ff)), layered((d_ff, d)), const((1, d))],
        out_specs=[rows(d), pl.BlockSpec((1, CONV_W - 1, d_ff), lambda b, t: (b, 0, 0))],
        out_shape=[jax.ShapeDtypeStruct((n, d), F32),
                   jax.ShapeDtypeStruct((n_seq, CONV_W - 1, d_ff), F32)],
        scratch_shapes=[pltpu.VMEM((CONV_W - 1, d_ff), F32), pltpu.VMEM((tm, d_ff), BF16)],
        compiler_params=_cparams("arbitrary", "arbitrary"),
        name="wout_ffn_seq",
    )(x2d, *ys, p["w_out"], p["norm_ffn"], p["ffn_wup"], p["ffn_cw"], p["ffn_cb"], p["ffn_wdn"],
      norm_final)


def _ffn_step(x2d, ys, p, norm_final, prev2, prev1, final):
    n, d = x2d.shape
    d_ff = p["ffn_wdn"].shape[1]
    lay = p["layer"]
    full = lambda shape: pl.BlockSpec(shape, lambda i: (0,) * len(shape))
    layered = lambda shape: pl.BlockSpec((None,) + shape, lambda i: (lay, 0, 0))
    return pl.pallas_call(
        functools.partial(_ffn_step_kernel, d_ff=d_ff, final=final),
        grid=(1,),
        in_specs=[full((n, d))] + [full((n, W_GROUP))] * 4 + [
            layered((d, d)), full((1, d)), layered((d, 2 * d_ff)), full((CONV_W, d_ff)),
            full((1, d_ff)), layered((d_ff, d)), full((1, d)), full((n, d_ff)), full((n, d_ff))],
        out_specs=[full((n, d)), full((n, d_ff))],
        out_shape=[jax.ShapeDtypeStruct((n, d), F32), jax.ShapeDtypeStruct((n, d_ff), F32)],
        scratch_shapes=[pltpu.VMEM((n, d_ff), BF16)],
        compiler_params=_cparams("arbitrary"),
        name="wout_ffn_step",
    )(x2d, *ys, p["w_out"], p["norm_ffn"], p["ffn_wup"], p["ffn_cw"], p["ffn_cb"], p["ffn_wdn"],
      norm_final, prev2, prev1)


def _step_vec_kernel(proj_ref, shift_ref, m_ref, brow_ref, wa_ref, ba_ref, mu_ref, w0_ref, a0_ref,
                     kk_ref, ka_ref, rk_ref, w2_ref, a2_ref, g2_ref,
                     pack_out, m_out, g_out, bonus_out):
    ones = _head_ones()
    sm = proj_ref[:, OFF_SMALL:OFF_SMALL + SMALL_W]
    pre = sm + brow_ref[...]
    logf = pltpu.roll(_log_sigmoid(pre), LANES - N_HEADS, 1)
    m_old = m_ref[...]
    m_new = jnp.maximum(logf + m_old, pre)
    m_out[...] = m_new
    lane = _iota((SMALL_W, W_GROUP), 0)
    spread_mat = jnp.where(lane == LANE_MI + (_iota((SMALL_W, W_GROUP), 1) >> HEAD_SHIFT), 1.0, 0.0).astype(BF16)
    gate_lanes = (_iota(pre.shape, 1) >= LANE_MI) & (_iota(pre.shape, 1) < LANE_MI + N_HEADS)
    spread = lambda x: _dot_exact_lhs(jnp.where(gate_lanes, x, 0.0), spread_mat)
    la = _log_sigmoid(_dot(sm, wa_ref[...]) + ba_ref[...]) * (1.0 / GLA_GATE_TEMP)
    rc = proj_ref[:, 0:RWKV_COLS]
    rr, lw, kt, rv, kk, a, g, bonus = _rwkv_vectors(
        rc, shift_ref[...], mu_ref[...], w0_ref[...], a0_ref[...], kk_ref[...], ka_ref[...],
        rk_ref[...], w2_ref[...], a2_ref[...], g2_ref[...], ones)
    main = lambda blk: proj_ref[:, blk * W_GROUP:(blk + 1) * W_GROUP]
    rows = {
        ROW_MQ: main(BLK_MQ), ROW_MK: main(BLK_MK), ROW_MV: main(BLK_MV),
        ROW_GQ: main(BLK_GQ), ROW_GK: main(BLK_GK), ROW_GV: main(BLK_GV), ROW_GA: jnp.exp(la),
        ROW_KK: kk, ROW_AKK: a * kk, ROW_W: jnp.exp(lw), ROW_KT: kt, ROW_RR: rr, ROW_RV: rv,
        ROW_IW: spread(jnp.exp(pre - m_new)), ROW_FW: spread(jnp.exp(logf + m_old - m_new)),
        ROW_M: spread(m_new),
    }
    for i, val in rows.items():
        pack_out[i] = val.T
    g_out[...] = g
    bonus_out[...] = bonus


def _step_vectors(proj, shift, m_pad, p):
    n = proj.shape[0]
    full = lambda a: pl.BlockSpec(a.shape, lambda i: (0,) * a.ndim)
    args = (proj, shift, m_pad, p["ml_brow"], p["gla_wa"], p["gla_ba"], p["rw_mu"], p["rw_w0"],
            p["rw_a0"], p["rw_kk"], p["rw_ka"], p["rw_rk"], p["rw_w2"], p["rw_a2"], p["rw_g2"])
    wide = jax.ShapeDtypeStruct((n, W_GROUP), F32)
    outs = [jax.ShapeDtypeStruct((N_PACK, W_GROUP, n), F32), jax.ShapeDtypeStruct((n, SMALL_W), F32),
            wide, wide]
    return pl.pallas_call(
        _step_vec_kernel,
        grid=(1,),
        in_specs=[full(a) for a in args],
        out_specs=[pl.BlockSpec(o.shape, lambda i, nd=len(o.shape): (0,) * nd) for o in outs],
        out_shape=outs,
        compiler_params=_cparams("arbitrary"),
        name="step_vectors",
    )(*args)


(ROW_MQ, ROW_MK, ROW_MV, ROW_GQ, ROW_GK, ROW_GV, ROW_GA, ROW_KK, ROW_AKK, ROW_W, ROW_KT, ROW_RR, ROW_RV,
 ROW_IW, ROW_FW, ROW_M) = range(16)
N_PACK = 16
(OROW_ML, OROW_GLA, OROW_RW, OROW_N) = range(4)
N_OPACK = 4
STEP_UNROLL = 4


def _heads_step_kernel(pack_ref, c_ref, n_ref, gs_ref, rs_ref, c_out, gs_out, rs_out, opack_out):
    vec = lambda i: pack_ref[i]
    one = lambda i, j: pack_ref[i, pl.ds(j, 1), :]
    colsum = lambda x: jnp.sum(x, axis=0, keepdims=True)

    k_ml = vec(ROW_MK) * (HEAD_DIM ** -0.5)
    q_ml = vec(ROW_MQ)
    iw = pack_ref[ROW_IW, 0:1, :]
    fw = pack_ref[ROW_FW, 0:1, :]
    n_new = fw * n_ref[...] + iw * k_ml
    opack_out[OROW_N] = n_new
    r_den = 1.0 / jnp.maximum(jnp.abs(colsum(n_new * q_ml)), jnp.exp(-pack_ref[ROW_M, 0:1, :]))

    def ml_body(v, carry):
        c_new = fw * c_ref[v] + (iw * one(ROW_MV, v)) * k_ml
        c_out[v] = c_new
        opack_out[OROW_ML, pl.ds(v, 1), :] = colsum(c_new * q_ml) * r_den
        return carry

    lax.fori_loop(0, HEAD_DIM, ml_body, 0, unroll=STEP_UNROLL)

    v_gla = vec(ROW_GV)

    def gla_body(k, acc):
        s_new = one(ROW_GA, k) * gs_ref[k] + one(ROW_GK, k) * v_gla
        gs_out[k] = s_new
        return acc + (one(ROW_GQ, k) * (HEAD_DIM ** -0.5)) * s_new

    opack_out[OROW_GLA] = lax.fori_loop(0, HEAD_DIM, gla_body, jnp.zeros_like(v_gla), unroll=STEP_UNROLL)

    kk, akk, w, kt, rr = vec(ROW_KK), vec(ROW_AKK), vec(ROW_W), vec(ROW_KT), vec(ROW_RR)

    def rw_body(v, carry):
        s = rs_ref[v]
        s_new = s * w - colsum(s * kk) * akk + one(ROW_RV, v) * kt
        rs_out[v] = s_new
        opack_out[OROW_RW, pl.ds(v, 1), :] = colsum(s_new * rr)
        return carry

    lax.fori_loop(0, HEAD_DIM, rw_body, 0, unroll=STEP_UNROLL)


def _heads_step(pack, ml_c, ml_n, gla_s, rw_s, layer):
    n = pack.shape[-1]
    sq = (HEAD_DIM, HEAD_DIM, n)
    head = lambda lead: pl.BlockSpec((lead, HEAD_DIM, n), lambda h: (0, h, 0))
    at_layer = lambda shape: pl.BlockSpec((None,) + shape, lambda h: (layer * N_HEADS + h,) + (0,) * len(shape))
    out_state = pl.BlockSpec((None,) + sq, lambda h: (h, 0, 0, 0))
    state = jax.ShapeDtypeStruct((N_HEADS,) + sq, F32)
    return pl.pallas_call(
        _heads_step_kernel,
        grid=(N_HEADS,),
        in_specs=[head(N_PACK), at_layer(sq), at_layer((HEAD_DIM, n)), at_layer(sq), at_layer(sq)],
        out_specs=[out_state, out_state, out_state, head(N_OPACK)],
        out_shape=[state, state, state, jax.ShapeDtypeStruct((N_OPACK, W_GROUP, n), F32)],
        compiler_params=_cparams("arbitrary"),
        name="heads_step",
    )(pack, ml_c, ml_n, gla_s, rw_s)


def _step_post_kernel(hml_ref, mo_ref, ogla_ref, gg_ref, orw_ref, g_ref, bonus_ref,
                      mln_ref, gln_ref, rwn_ref, yml_out, ygla_out, yrw_out):
    ones = _head_ones()
    hml = hml_ref[...].T
    ms = _head_sum(hml * hml, ones) * (1.0 / HEAD_DIM)
    yml_out[...] = (hml * lax.rsqrt(ms + EPS) * mln_ref[...] * _sigmoid(mo_ref[...])).astype(BF16)
    og = ogla_ref[...].T
    ms = _head_sum(og * og, ones) * (1.0 / HEAD_DIM)
    gg = gg_ref[...]
    ygla_out[...] = (og * lax.rsqrt(ms + EPS) * gln_ref[...] * (gg * _sigmoid(gg))).astype(BF16)
    yrw_out[...] = ((_head_layernorm(orw_ref[...].T, rwn_ref[...], ones) + bonus_ref[...])
                    * g_ref[...]).astype(BF16)


def _step_post(hml, mo, ogla, gg, orw, g, bonus, p):
    n = mo.shape[0]
    args = (hml, mo, ogla, gg, orw, g, bonus, p["ml_norm"], p["gla_norm"], p["rw_norm"])
    full = lambda a: pl.BlockSpec(a.shape, lambda i: (0,) * a.ndim)
    out = jax.ShapeDtypeStruct((n, W_GROUP), BF16)
    return pl.pallas_call(
        _step_post_kernel,
        grid=(1,),
        in_specs=[full(a) for a in args],
        out_specs=[pl.BlockSpec((n, W_GROUP), lambda i: (0, 0))] * 3,
        out_shape=[out] * 3,
        compiler_params=_cparams("arbitrary"),
        name="step_post",
    )(*args)


def _layer_params(l, prm):
    p = {}
    w_in = prm["w_in"][l]
    d = w_in.shape[0]
    sizes = (W_GROUP, W_GROUP, W_GROUP, W_GROUP, N_HEADS, N_HEADS, W_GROUP, W_GROUP, W_GROUP, W_GROUP,
             GLA_RANK, W_GROUP, RWKV_COLS)
    offs = [0]
    for s in sizes:
        offs.append(offs[-1] + s)
    col = lambda i: w_in[:, offs[i]:offs[i + 1]]
    (u, mq, mk, mv, mi, mf, mo, gq, gk, gv, ga, gg, rcols) = [col(i) for i in range(13)]
    small = jnp.concatenate([ga, mi, mf, jnp.zeros((d, SMALL_W - GLA_RANK - 2 * N_HEADS), F32)], axis=1)
    p["w_in"] = jnp.concatenate([rcols, small, u, mq, mk, mv, mo, gq, gk, gv, gg], axis=1).astype(BF16)
    p["w_gate_t"] = jnp.concatenate([mi, mf], axis=1).T.astype(BF16)
    p["norm_mix"] = prm["norm_mix"][l][None, :]

    lam = lax.complex(prm["s5_lam_re"][l], prm["s5_lam_im"][l])
    dt = jnp.exp(prm["s5_log_dt"][l])[:, None]
    lam_bar = jnp.exp(lam * dt)
    b_bar = ((lam_bar - 1.0) / lam)[..., None] * lax.complex(prm["s5_b_re"][l], prm["s5_b_im"][l])
    eye = jnp.eye(S5_GROUPS, dtype=F32)
    bm = lambda b: jnp.einsum("gph,gk->ghkp", b, eye).reshape(W_GROUP, S5_WIDTH)
    p["bmat"] = jnp.concatenate([bm(b_bar.real), bm(b_bar.imag)], axis=1).astype(BF16)
    cm = lambda c: jnp.einsum("ghp,gk->kpgh", c, eye).reshape(S5_WIDTH, W_GROUP)
    p["cmat"] = jnp.concatenate([cm(prm["s5_c_re"][l]), -cm(prm["s5_c_im"][l])], axis=0).astype(BF16)
    p["lre"] = lam_bar.real.reshape(1, S5_WIDTH)
    p["lim"] = lam_bar.imag.reshape(1, S5_WIDTH)
    p["d"] = prm["s5_d"][l][None, :]
    p["wglu"] = prm["s5_w_glu"][l].astype(BF16)

    bias = prm["ml_gate_bias"][l]
    p["ml_brow"] = jnp.zeros((1, SMALL_W), F32).at[0, LANE_MI:LANE_MI + 2 * N_HEADS].set(bias)
    p["ml_bcol"] = jnp.broadcast_to(bias[:, None], (2 * N_HEADS, CHUNK_ML))
    p["ml_norm"] = prm["ml_norm"][l][None, :]

    p["gla_wa"] = jnp.zeros((SMALL_W, W_GROUP), F32).at[0:GLA_RANK].set(prm["gla_w_alpha"][l]).astype(BF16)
    p["gla_ba"] = prm["gla_b_alpha"][l][None, :]
    p["gla_norm"] = prm["gla_norm"][l][None, :]

    p["rw_mu"] = prm["rw_mu"][l][None, :]
    p["rw_w0"] = prm["rw_w0"][l][None, :]
    p["rw_a0"] = prm["rw_a0"][l][None, :]
    p["rw_kk"] = prm["rw_k_k"][l][None, :]
    p["rw_ka"] = prm["rw_k_a"][l][None, :]
    p["rw_rk"] = prm["rw_r_k"][l].reshape(1, W_GROUP)
    z = jnp.zeros((LANES, W_GROUP), F32)
    p["rw_w2"] = z.at[0:RWKV_W_RANK].set(prm["rw_w2"][l]).astype(BF16)
    p["rw_a2"] = z.at[RWKV_W_RANK:RWKV_W_RANK + RWKV_A_RANK].set(prm["rw_a2"][l]).astype(BF16)
    p["rw_g2"] = z.at[RWKV_W_RANK + RWKV_A_RANK:].set(prm["rw_g2"][l]).astype(BF16)
    p["rw_norm"] = prm["rw_norm"][l][None, :]

    p["layer"] = l
    p["w_out"] = prm["w_out"].astype(BF16)
    p["norm_ffn"] = prm["norm_ffn"][l][None, :]
    p["ffn_wup"] = prm["ffn_w_up"].astype(BF16)
    p["ffn_cw"] = prm["ffn_conv_w"][l]
    p["ffn_cb"] = prm["ffn_conv_b"][l][None, :]
    p["ffn_wdn"] = prm["ffn_w_down"].astype(BF16)
    return p


def _diag_blocks(m):
    b = m.shape[0]
    m = m.reshape(b, N_HEADS, HEAD_DIM, N_HEADS, HEAD_DIM)
    return jnp.stack([m[:, h, :, h, :] for h in range(N_HEADS)], axis=1)


def _prompt_layer(x2d, p, norm_final, n_seq, t_len, final):
    proj, gt = _inproj(x2d, p["norm_mix"], p["w_in"], p["w_gate_t"], n_seq, t_len, tm=512)

    u_tb = proj[:, OFF_MAIN:OFF_MAIN + W_GROUP].reshape(n_seq, t_len, W_GROUP).transpose(1, 0, 2)
    zeros = jnp.zeros((n_seq, S5_WIDTH), F32)
    y_s5_tb, s5_re, s5_im = _s5(u_tb.reshape(t_len * n_seq, W_GROUP), zeros, zeros, p, n_seq, t_len, tt=128)
    y_s5 = y_s5_tb.reshape(t_len, n_seq, W_GROUP).transpose(1, 0, 2).reshape(n_seq * t_len, W_GROUP)

    y_ml, ml_c, ml_n, ml_m = _mlstm(proj, gt, p, n_seq, t_len)
    y_gla, gla_st = _gla(proj, p, n_seq, t_len)
    y_rw, rw_s, rw_shift = _rwkv(proj, p, n_seq, t_len)

    x_new, ffn_buf = _ffn_seq(x2d, (y_s5, y_ml, y_gla, y_rw), p, norm_final, n_seq, t_len, tm=512,
                              final=final)
    states = (
        s5_re.reshape(n_seq, S5_GROUPS, S5_STATE),
        s5_im.reshape(n_seq, S5_GROUPS, S5_STATE),
        _diag_blocks(ml_c),
        ml_n[:, 0, :].reshape(n_seq, N_HEADS, HEAD_DIM),
        ml_m[:, 0:N_HEADS, 0],
        jnp.swapaxes(_diag_blocks(gla_st), -1, -2),
        _diag_blocks(rw_s),
        rw_shift[:, 0, :],
        ffn_buf,
    )
    return x_new, states


def _sample_layer(x2d, st, pair_states, layer, p, norm_final, final):
    (s5_re, s5_im, ml_m, rw_shift, ffn_buf) = st
    n = x2d.shape[0]
    npairs = n * N_HEADS
    proj, _ = _inproj(x2d, p["norm_mix"], p["w_in"], p["w_gate_t"], 1, n, tm=n)

    y_s5, s5_re_new, s5_im_new = _s5(proj[:, OFF_MAIN:OFF_MAIN + W_GROUP],
                                     s5_re.reshape(n, S5_WIDTH), s5_im.reshape(n, S5_WIDTH),
                                     p, n, 1, tt=1)

    m_pad = jnp.zeros((n, SMALL_W), F32).at[:, LANE_MI:LANE_MI + N_HEADS].set(ml_m)
    pack, m_new, g, bonus = _step_vectors(proj, rw_shift, m_pad, p)
    ml_c_new, gla_s_new, rw_s_new, opack = _heads_step(pack, *pair_states, layer)

    main = lambda blk: proj[:, blk * W_GROUP:(blk + 1) * W_GROUP]
    y_ml, y_gla, y_rw = _step_post(opack[OROW_ML], main(BLK_MO), opack[OROW_GLA], main(BLK_GG),
                                   opack[OROW_RW], g, bonus, p)
    x_new, ug = _ffn_step(x2d, (y_s5, y_ml, y_gla, y_rw), p, norm_final, ffn_buf[:, 0, :], ffn_buf[:, 1, :],
                          final)
    states = (
        s5_re_new.reshape(n, S5_GROUPS, S5_STATE),
        s5_im_new.reshape(n, S5_GROUPS, S5_STATE),
        ml_c_new.transpose(3, 0, 1, 2),
        opack[OROW_N].reshape(N_HEADS, HEAD_DIM, n).transpose(2, 0, 1),
        m_new[:, LANE_MI:LANE_MI + N_HEADS],
        gla_s_new.transpose(3, 0, 1, 2),
        rw_s_new.transpose(3, 0, 1, 2),
        proj[:, 0:RWKV_COLS],
        jnp.stack([ffn_buf[:, 1, :], ug], axis=1),
    )
    return x_new, states


def kernel(x_prompt, x_sample, state_s5_re, state_s5_im, state_mlstm_C, state_mlstm_n, state_mlstm_m, state_gla_S, state_rwkv_S, state_rwkv_shift, state_ffn_conv, norm_mix, w_in, s5_lam_re, s5_lam_im, s5_log_dt, s5_b_re, s5_b_im, s5_c_re, s5_c_im, s5_d, s5_w_glu, ml_gate_bias, ml_norm, gla_w_alpha, gla_b_alpha, gla_norm, rw_mu, rw_w0, rw_w2, rw_a0, rw_a2, rw_g2, rw_k_k, rw_k_a, rw_r_k, rw_norm, w_out, norm_ffn, ffn_w_up, ffn_conv_w, ffn_conv_b, ffn_w_down, norm_final):
    prm = dict(norm_mix=norm_mix, w_in=w_in, s5_lam_re=s5_lam_re, s5_lam_im=s5_lam_im, s5_log_dt=s5_log_dt,
               s5_b_re=s5_b_re, s5_b_im=s5_b_im, s5_c_re=s5_c_re, s5_c_im=s5_c_im, s5_d=s5_d,
               s5_w_glu=s5_w_glu, ml_gate_bias=ml_gate_bias, ml_norm=ml_norm, gla_w_alpha=gla_w_alpha,
               gla_b_alpha=gla_b_alpha, gla_norm=gla_norm, rw_mu=rw_mu, rw_w0=rw_w0, rw_w2=rw_w2,
               rw_a0=rw_a0, rw_a2=rw_a2, rw_g2=rw_g2, rw_k_k=rw_k_k, rw_k_a=rw_k_a, rw_r_k=rw_r_k,
               rw_norm=rw_norm, w_out=w_out, norm_ffn=norm_ffn, ffn_w_up=ffn_w_up, ffn_conv_w=ffn_conv_w,
               ffn_conv_b=ffn_conv_b, ffn_w_down=ffn_w_down)
    depth = w_in.shape[0]
    n_seq, t_len, d = x_prompt.shape
    n_smp = x_sample.shape[0]
    assert t_len % 512 == 0 and x_sample.shape[1] == 1
    nf = norm_final[None, :]
    small_states = (state_s5_re, state_s5_im, state_mlstm_m, state_rwkv_shift, state_ffn_conv)
    lead = depth * N_HEADS
    to_lanes = lambda st: jnp.moveaxis(st, 1, -1)
    pair_states = (to_lanes(state_mlstm_C).reshape(lead, HEAD_DIM, HEAD_DIM, n_smp),
                   to_lanes(state_mlstm_n).reshape(lead, HEAD_DIM, n_smp),
                   to_lanes(state_gla_S).reshape(lead, HEAD_DIM, HEAD_DIM, n_smp),
                   to_lanes(state_rwkv_S).reshape(lead, HEAD_DIM, HEAD_DIM, n_smp))

    xp = x_prompt.reshape(n_seq * t_len, d)
    xs = x_sample.reshape(n_smp, d)
    p_states, s_states = [], []
    for l in range(depth):
        p = _layer_params(l, prm)
        final = l == depth - 1
        xp, ps = _prompt_layer(xp, p, nf, n_seq, t_len, final)
        xs, ss = _sample_layer(xs, tuple(s[l] for s in small_states), pair_states, l, p, nf, final)
        p_states.append(ps)
        s_states.append(ss)
    new_p = tuple(jnp.stack([st[i] for st in p_states]) for i in range(9))
    new_s = tuple(jnp.stack([st[i] for st in s_states]) for i in range(9))
    return (xp.reshape(n_seq, t_len, d), xs.reshape(n_smp, 1, d)) + new_p + new_s
```

```python
import functools
import math

import jax
import jax.numpy as jnp
from jax import lax
from jax.experimental import pallas as pl
from jax.experimental.pallas import tpu as pltpu

F32 = jnp.float32
BF16 = jnp.bfloat16

LANES = 128
W_GROUP = 256
HEAD_DIM = 64
HEAD_SHIFT = 6
N_HEADS = 4
S5_CH = 16
S5_GROUPS = 16
S5_STATE = 64
S5_WIDTH = S5_GROUPS * S5_STATE
GLA_RANK = 16
GLA_GATE_TEMP = 16.0
RWKV_W_RANK = 32
RWKV_A_RANK = 32
RWKV_G_RANK = 64
RWKV_COLS = 3 * W_GROUP + RWKV_W_RANK + RWKV_A_RANK + RWKV_G_RANK
RWKV_DECAY_SCALE = 0.6065306597126334
CONV_W = 3
EPS = 1e-6
GN_EPS = 64e-5
NEG_BIG = -1e30
EXP_CLAMP = 80.0

SMALL_W = LANES
OFF_SMALL = RWKV_COLS
OFF_MAIN = RWKV_COLS + SMALL_W
D_PROJ = OFF_MAIN + 9 * W_GROUP
LANE_MI = GLA_RANK
LANE_MF = GLA_RANK + N_HEADS
(BLK_U, BLK_MQ, BLK_MK, BLK_MV, BLK_MO, BLK_GQ, BLK_GK, BLK_GV, BLK_GG) = range(
    OFF_MAIN // W_GROUP, OFF_MAIN // W_GROUP + 9)
BLK_SMALL = OFF_SMALL // SMALL_W

CHUNK_ML = 128
CHUNK_RW = 64
SEQS_PER_STEP = 8
SEQS_PER_STEP_ML = 4
VMEM_LIMIT = 56 * 1024 * 1024


def _cparams(*sem):
    return pltpu.CompilerParams(dimension_semantics=sem, vmem_limit_bytes=VMEM_LIMIT)


def _dot(a, b):
    return jnp.dot(a.astype(BF16), b.astype(BF16), preferred_element_type=F32)


def _dot_nt(a, b):
    return lax.dot_general(a.astype(BF16), b.astype(BF16), (((1,), (1,)), ((), ())),
                           preferred_element_type=F32)


def _dot_tn(a, b):
    return lax.dot_general(a.astype(BF16), b.astype(BF16), (((0,), (0,)), ((), ())),
                           preferred_element_type=F32)


def _split2(x):
    hi = x.astype(BF16)
    lo = (x - hi.astype(F32)).astype(BF16)
    return hi, lo


def _dot_exact_rhs(a01, x):
    hi, lo = _split2(x)
    f = lambda p: jnp.dot(a01, p, preferred_element_type=F32)
    return f(hi) + f(lo)


def _dot_exact_lhs(x, b01):
    hi, lo = _split2(x)
    f = lambda p: jnp.dot(p, b01, preferred_element_type=F32)
    return f(hi) + f(lo)


def _sigmoid(x):
    return 1.0 / (1.0 + jnp.exp(-x))


def _log_sigmoid(x):
    return jnp.minimum(x, 0.0) - jnp.log(1.0 + jnp.exp(-jnp.abs(x)))


def _gelu(x):
    return 0.5 * x * (1.0 + jnp.tanh(math.sqrt(2.0 / math.pi) * (x + 0.044715 * (x * x * x))))


def _iota(shape, dim):
    return lax.broadcasted_iota(jnp.int32, shape, dim)


def _head_ones():
    r = _iota((W_GROUP, W_GROUP), 0) >> HEAD_SHIFT
    c = _iota((W_GROUP, W_GROUP), 1) >> HEAD_SHIFT
    return jnp.where(r == c, 1.0, 0.0).astype(BF16)


def _head_sum(x, ones):
    return _dot_exact_lhs(x, ones)


def _head_lane_mask(h, rows=1):
    lane = _iota((rows, W_GROUP), 1) >> HEAD_SHIFT
    return jnp.where(lane == h, 1.0, 0.0).astype(F32)


def _rmsnorm_rows(x, g):
    ms = jnp.mean(x * x, axis=-1, keepdims=True)
    return x * lax.rsqrt(ms + EPS) * g


def _tril_ones(n, strict=False):
    r = _iota((n, n), 0)
    c = _iota((n, n), 1)
    return jnp.where((r > c) if strict else (r >= c), 1.0, 0.0).astype(BF16)


def _triu_ones(n):
    r = _iota((n, n), 0)
    c = _iota((n, n), 1)
    return jnp.where(r <= c, 1.0, 0.0).astype(BF16)


def _inproj_kernel(x_ref, g_ref, w_ref, wg_ref, proj_ref, gt_ref):
    hn = _rmsnorm_rows(x_ref[...], g_ref[...]).astype(BF16)
    proj_ref[...] = jnp.dot(hn, w_ref[...], preferred_element_type=F32)
    gt_ref[0] = lax.dot_general(wg_ref[...], hn, (((1,), (1,)), ((), ())),
                                preferred_element_type=F32)


def _inproj(x2d, g, w, wg, n_seq, t_len, tm):
    n, d = x2d.shape
    tiles_per_seq = t_len // tm
    return pl.pallas_call(
        _inproj_kernel,
        grid=(n // tm,),
        in_specs=[
            pl.BlockSpec((tm, d), lambda i: (i, 0)),
            pl.BlockSpec((1, d), lambda i: (0, 0)),
            pl.BlockSpec((d, D_PROJ), lambda i: (0, 0)),
            pl.BlockSpec((8, d), lambda i: (0, 0)),
        ],
        out_specs=[
            pl.BlockSpec((tm, D_PROJ), lambda i: (i, 0)),
            pl.BlockSpec((1, 8, tm), lambda i: (i // tiles_per_seq, 0, i % tiles_per_seq)),
        ],
        out_shape=[
            jax.ShapeDtypeStruct((n, D_PROJ), F32),
            jax.ShapeDtypeStruct((n_seq, 8, t_len), F32),
        ],
        compiler_params=_cparams("arbitrary"),
        name="inproj",
    )(x2d, g, w, wg)


def _s5_kernel(u_ref, h0re_ref, h0im_ref, bmat_ref, lre_ref, lim_ref, cmat_ref, d_ref, wglu_ref,
               y_ref, hre_out, him_out, bu_scr, hs_scr, y_scr, hre_scr, him_scr, *, bp, tt, seq_major):
    i = pl.program_id(0)

    @pl.when(i == 0)
    def _():
        hre_scr[...] = h0re_ref[...]
        him_scr[...] = h0im_ref[...]

    nt = S5_WIDTH // LANES
    u = u_ref[...].reshape(tt * bp, W_GROUP)
    bu = jnp.dot(u.astype(BF16), bmat_ref[...], preferred_element_type=F32)
    for j in range(2 * nt):
        bu_scr[j] = bu[:, j * LANES:(j + 1) * LANES]
    lre = [lre_ref[:, j * LANES:(j + 1) * LANES] for j in range(nt)]
    lim = [lim_ref[:, j * LANES:(j + 1) * LANES] for j in range(nt)]

    def step(t, carry):
        dst = pl.ds(pl.multiple_of(t * bp, bp), bp)
        src = pl.ds(t, bp, stride=tt) if seq_major else dst
        new = []
        for j in range(nt):
            hre, him = carry[j], carry[nt + j]
            new.append((lre[j] * hre - lim[j] * him + bu_scr[j, src, :],
                        lre[j] * him + lim[j] * hre + bu_scr[nt + j, src, :]))
        out = tuple(x[0] for x in new) + tuple(x[1] for x in new)
        for j in range(2 * nt):
            hs_scr[j, dst, :] = out[j]
        return out

    carry = tuple(hre_scr[:, j * LANES:(j + 1) * LANES] for j in range(nt)) + tuple(
        him_scr[:, j * LANES:(j + 1) * LANES] for j in range(nt))
    carry = step(0, carry) if tt == 1 else lax.fori_loop(0, tt, step, carry)
    hre = jnp.concatenate(carry[0:nt], axis=1)
    him = jnp.concatenate(carry[nt:2 * nt], axis=1)
    hre_scr[...] = hre
    him_scr[...] = him
    hre_out[...] = hre
    him_out[...] = him
    hs = jnp.concatenate([hs_scr[j] for j in range(2 * nt)], axis=1)
    y = jnp.dot(hs.astype(BF16), cmat_ref[...], preferred_element_type=F32)
    if seq_major:
        for j in range(W_GROUP // LANES):
            y_scr[j] = y[:, j * LANES:(j + 1) * LANES]
        y = jnp.concatenate(
            [jnp.concatenate([y_scr[j, pl.ds(b, tt, stride=bp), :] for j in range(W_GROUP // LANES)], axis=1)
             for b in range(bp)], axis=0)
    y = y + d_ref[...] * u
    z = _gelu(y)
    y_ref[...] = (z * _sigmoid(_dot(z, wglu_ref[...]))).astype(y_ref.dtype).reshape(y_ref.shape)


def _s5(u, h0re, h0im, sp, bp, t_len, tt, seq_major):
    full = lambda shape: pl.BlockSpec(shape, lambda i: (0,) * len(shape))
    if seq_major:
        u_spec = pl.BlockSpec((bp, tt, W_GROUP), lambda i: (0, i, BLK_U))
        y_spec = pl.BlockSpec((bp, tt, W_GROUP), lambda i: (0, i, 0))
        y_shape = jax.ShapeDtypeStruct((bp, t_len, W_GROUP), BF16)
    else:
        u_spec = y_spec = pl.BlockSpec((tt * bp, W_GROUP), lambda i: (i, 0))
        y_shape = jax.ShapeDtypeStruct((t_len * bp, W_GROUP), BF16)
    return pl.pallas_call(
        functools.partial(_s5_kernel, bp=bp, tt=tt, seq_major=seq_major),
        grid=(t_len // tt,),
        in_specs=[
            u_spec,
            full((bp, S5_WIDTH)), full((bp, S5_WIDTH)),
            full((W_GROUP, 2 * S5_WIDTH)),
            full((bp, S5_WIDTH)), full((bp, S5_WIDTH)),
            full((2 * S5_WIDTH, W_GROUP)),
            full((1, W_GROUP)),
            full((W_GROUP, W_GROUP)),
        ],
        out_specs=[
            y_spec,
            full((bp, S5_WIDTH)), full((bp, S5_WIDTH)),
        ],
        out_shape=[
            y_shape,
            jax.ShapeDtypeStruct((bp, S5_WIDTH), F32),
            jax.ShapeDtypeStruct((bp, S5_WIDTH), F32),
        ],
        scratch_shapes=[
            pltpu.VMEM((2 * S5_WIDTH // LANES, tt * bp, LANES), F32),
            pltpu.VMEM((2 * S5_WIDTH // LANES, tt * bp, LANES), F32),
            pltpu.VMEM((W_GROUP // LANES, tt * bp, LANES), F32),
            pltpu.VMEM((bp, S5_WIDTH), F32),
            pltpu.VMEM((bp, S5_WIDTH), F32),
        ],
        compiler_params=_cparams("arbitrary"),
        name="s5_scan",
    )(u, h0re, h0im, sp["bmat"], jnp.broadcast_to(sp["lre"], (bp, S5_WIDTH)),
      jnp.broadcast_to(sp["lim"], (bp, S5_WIDTH)), sp["cmat"], sp["d"], sp["wglu"])


def _run_interleaved(chains):
    live = list(chains)
    while live:
        still = []
        for ch in live:
            try:
                next(ch)
                still.append(ch)
            except StopIteration:
                pass
        live = still


def _mlstm_kernel(q_ref, k_ref, v_ref, og_ref, sm_ref, gt_ref, brow_ref, bcol_ref, norm_ref,
                  y_ref, c_out, n_out, m_out, c_scr, n_scr, m_scr, *, L, nc, G):
    c = pl.program_id(1)

    @pl.when(c == 0)
    def _():
        c_scr[...] = jnp.zeros_like(c_scr)
        n_scr[...] = jnp.zeros_like(n_scr)
        m_scr[...] = jnp.zeros_like(m_scr)

    ones = _head_ones()
    tril = _tril_ones(L)
    triu = _triu_ones(L)
    l_shift = L.bit_length() - 1
    src = _iota((LANES, N_HEADS * L), 0)
    to_scores = jnp.where(src == LANE_MI + (_iota((LANES, N_HEADS * L), 1) >> l_shift), 1.0, 0.0).astype(BF16)
    src = _iota((LANES, W_GROUP), 0)
    to_feats = jnp.where(src == LANE_MI + (_iota((LANES, W_GROUP), 1) >> HEAD_SHIFT), 1.0, 0.0).astype(BF16)
    seg = _iota((N_HEADS * L, W_GROUP), 0) >> l_shift
    score_sum = jnp.where(seg == (_iota((N_HEADS * L, W_GROUP), 1) >> HEAD_SHIFT), 1.0, 0.0).astype(BF16)
    causal = _iota((L, N_HEADS * L), 0) >= (_iota((L, N_HEADS * L), 1) & (L - 1))
    t_idx = _iota((L, LANES), 0)

    def chain(gi):
        q = q_ref[gi]
        k = k_ref[gi] * (HEAD_DIM ** -0.5)
        v = v_ref[gi]
        pre = sm_ref[gi] + brow_ref[...]
        gtb = gt_ref[gi] + bcol_ref[...]
        bc_col = _dot_exact_rhs(tril, _log_sigmoid(pre))
        bc_row = _dot_exact_lhs(_log_sigmoid(gtb), triu)
        yield
        C = c_scr[gi]
        n = n_scr[gi, 0:1, :]
        qC = _dot_nt(q, C)
        qn = _head_sum(q * n, ones)
        qk = _dot_nt(q, _stack_heads(k))
        yield

        b_col = pltpu.roll(bc_col, LANES - N_HEADS, 1)
        r_col = pre - b_col
        mu = r_col
        shift = 1
        while shift < L:
            mu = jnp.maximum(mu, jnp.where(t_idx >= shift, pltpu.roll(mu, shift, 0), NEG_BIG))
            shift *= 2
        m0 = m_scr[gi, 0:1, :]
        mu = jnp.maximum(mu, m0)
        mu_last = mu[L - 1:L, :]
        r_rows = gtb[0:N_HEADS, :] - bc_row[N_HEADS:2 * N_HEADS, :]
        r_all = jnp.concatenate([r_rows[h:h + 1, :] for h in range(N_HEADS)], axis=1)
        w = jnp.exp(jnp.where(causal, r_all - _dot_exact_lhs(mu, to_scores), NEG_BIG))
        ws = w * qk
        yield
        w0 = _dot_exact_lhs(jnp.exp(m0 - mu), to_feats)
        m_row = _dot_exact_lhs(b_col + mu, to_feats)
        den = w0 * qn + _dot(ws, score_sum)
        hc = (w0 * qC + _dot(ws, _stack_heads(v))) / jnp.maximum(jnp.abs(den), jnp.exp(-m_row))
        yield
        kw = k * _dot_exact_lhs(jnp.exp(r_col - mu_last), to_feats)
        w0l = _dot_exact_lhs(jnp.broadcast_to(jnp.exp(m0 - mu_last), (8, LANES)), to_feats)[0:1, :]
        c_scr[gi] = w0l * C + ones.astype(F32) * _dot_tn(v, kw)
        n_scr[gi, 0:1, :] = w0l * n + jnp.sum(kw, axis=0, keepdims=True)
        m_scr[gi, 0:1, :] = b_col[L - 1:L, :] + mu_last
        yield
        ms = _head_sum(hc * hc, ones) * (1.0 / HEAD_DIM)
        y = hc * lax.rsqrt(ms + EPS) * norm_ref[...] * _sigmoid(og_ref[gi])
        y_ref[gi] = y.astype(y_ref.dtype)

    _run_interleaved([chain(gi) for gi in range(G)])

    @pl.when(c == nc - 1)
    def _():
        c_out[...] = c_scr[...]
        n_out[...] = jnp.broadcast_to(n_scr[:, 0:1, :], n_out.shape)
        m_out[...] = m_scr[...]


def _seq_spec(G, L, blk, width=W_GROUP):
    return pl.BlockSpec((G, L, width), lambda b, c: (b, c, blk))


def _mlstm(proj, gt, p, n_seq, t_len):
    L = CHUNK_ML
    G = SEQS_PER_STEP_ML
    nc = t_len // L
    const = lambda shape: pl.BlockSpec(shape, lambda b, c: (0,) * len(shape))
    per_seq = lambda rows, width: pl.BlockSpec((G, rows, width), lambda b, c: (b, 0, 0))
    proj3 = proj.reshape(n_seq, t_len, D_PROJ)
    y, c_new, n_new, m_new = pl.pallas_call(
        functools.partial(_mlstm_kernel, L=L, nc=nc, G=G),
        grid=(n_seq // G, nc),
        in_specs=[
            _seq_spec(G, L, BLK_MQ), _seq_spec(G, L, BLK_MK), _seq_spec(G, L, BLK_MV),
            _seq_spec(G, L, BLK_MO), _seq_spec(G, L, BLK_SMALL, SMALL_W),
            pl.BlockSpec((G, 8, L), lambda b, c: (b, 0, c)),
            const((1, SMALL_W)), const((8, L)), const((1, W_GROUP)),
        ],
        out_specs=[
            pl.BlockSpec((G, L, W_GROUP), lambda b, c: (b, c, 0)),
            per_seq(W_GROUP, W_GROUP), per_seq(8, W_GROUP), per_seq(8, LANES),
        ],
        out_shape=[
            jax.ShapeDtypeStruct((n_seq, t_len, W_GROUP), BF16),
            jax.ShapeDtypeStruct((n_seq, W_GROUP, W_GROUP), F32),
            jax.ShapeDtypeStruct((n_seq, 8, W_GROUP), F32),
            jax.ShapeDtypeStruct((n_seq, 8, LANES), F32),
        ],
        scratch_shapes=[
            pltpu.VMEM((G, W_GROUP, W_GROUP), F32),
            pltpu.VMEM((G, 8, W_GROUP), F32),
            pltpu.VMEM((G, 8, LANES), F32),
        ],
        compiler_params=_cparams("arbitrary", "arbitrary"),
        name="mlstm_chunk",
    )(proj3, proj3, proj3, proj3, proj3, gt, p["ml_brow"], p["ml_bcol"], p["ml_norm"])
    return y.reshape(n_seq * t_len, W_GROUP), c_new, n_new, m_new


def _gla_kernel(q_ref, k_ref, v_ref, gg_ref, sm_ref, wa_ref, ba_ref, norm_ref,
                y_ref, s_out, s_scr, *, L, nc, G):
    c = pl.program_id(1)

    @pl.when(c == 0)
    def _():
        s_scr[...] = jnp.zeros_like(s_scr)

    ones = _head_ones()
    tril = _tril_ones(L)
    causal = _iota((L, N_HEADS * L), 0) >= (_iota((L, N_HEADS * L), 1) & (L - 1))

    def chain(gi):
        q = q_ref[gi] * (HEAD_DIM ** -0.5)
        k = k_ref[gi]
        v = v_ref[gi]
        la = _log_sigmoid(_dot(sm_ref[gi], wa_ref[...]) + ba_ref[...]) * (1.0 / GLA_GATE_TEMP)
        yield
        bc = _dot_exact_rhs(tril, la)
        yield
        qs = q * jnp.exp(bc)
        kh = k * jnp.exp(jnp.minimum(-bc, EXP_CLAMP))
        ST = s_scr[gi]
        att = jnp.where(causal, _dot_nt(qs, _stack_heads(kh)), 0.0)
        from_state = _dot_nt(qs, ST)
        yield
        o = from_state + _dot(att, _stack_heads(v))
        b_last = bc[L - 1:L, :]
        kbar = k * jnp.exp(b_last - bc)
        s_scr[gi] = ST * jnp.exp(b_last) + ones.astype(F32) * _dot_tn(v, kbar)
        yield
        ms = _head_sum(o * o, ones) * (1.0 / HEAD_DIM)
        gg = gg_ref[gi]
        y = o * lax.rsqrt(ms + EPS) * norm_ref[...] * (gg * _sigmoid(gg))
        y_ref[gi] = y.astype(y_ref.dtype)

    _run_interleaved([chain(gi) for gi in range(G)])

    @pl.when(c == nc - 1)
    def _():
        s_out[...] = s_scr[...]


def _gla(proj, p, n_seq, t_len):
    L = CHUNK_ML
    G = SEQS_PER_STEP_ML
    nc = t_len // L
    const = lambda shape: pl.BlockSpec(shape, lambda b, c: (0,) * len(shape))
    proj3 = proj.reshape(n_seq, t_len, D_PROJ)
    y, s_new = pl.pallas_call(
        functools.partial(_gla_kernel, L=L, nc=nc, G=G),
        grid=(n_seq // G, nc),
        in_specs=[
            _seq_spec(G, L, BLK_GQ), _seq_spec(G, L, BLK_GK), _seq_spec(G, L, BLK_GV),
            _seq_spec(G, L, BLK_GG), _seq_spec(G, L, BLK_SMALL, SMALL_W),
            const((SMALL_W, W_GROUP)), const((1, W_GROUP)), const((1, W_GROUP)),
        ],
        out_specs=[
            pl.BlockSpec((G, L, W_GROUP), lambda b, c: (b, c, 0)),
            pl.BlockSpec((G, W_GROUP, W_GROUP), lambda b, c: (b, 0, 0)),
        ],
        out_shape=[
            jax.ShapeDtypeStruct((n_seq, t_len, W_GROUP), BF16),
            jax.ShapeDtypeStruct((n_seq, W_GROUP, W_GROUP), F32),
        ],
        scratch_shapes=[pltpu.VMEM((G, W_GROUP, W_GROUP), F32)],
        compiler_params=_cparams("arbitrary", "arbitrary"),
        name="gla_chunk",
    )(proj3, proj3, proj3, proj3, proj3, p["gla_wa"], p["gla_ba"], p["gla_norm"])
    return y.reshape(n_seq * t_len, W_GROUP), s_new


def _rwkv_vectors(rc, prev, p_mu, p_w0, p_a0, p_kk, p_ka, p_rk, w2, a2, g2, ones):
    xm = rc + p_mu * (prev - rc)
    rr = xm[:, 0:W_GROUP]
    rk = xm[:, W_GROUP:2 * W_GROUP]
    rv = xm[:, 2 * W_GROUP:3 * W_GROUP]
    tail = xm[:, 3 * W_GROUP:RWKV_COLS]
    lw = -RWKV_DECAY_SCALE * _sigmoid(p_w0 + _dot(jnp.tanh(tail), w2))
    a = _sigmoid(p_a0 + _dot(tail, a2))
    g = _dot(_sigmoid(tail), g2)
    kk = rk * p_kk
    kk = kk * lax.rsqrt(jnp.maximum(_head_sum(kk * kk, ones), 1e-24))
    kt = rk * (1.0 + (a - 1.0) * p_ka)
    bonus = _head_sum(rr * kt * p_rk, ones) * rv
    return rr, lw, kt, rv, kk, a, g, bonus


def _head_layernorm(o, g, ones):
    mu = _head_sum(o, ones) * (1.0 / HEAD_DIM)
    oc = o - mu
    var = _head_sum(oc * oc, ones) * (1.0 / HEAD_DIM)
    return oc * lax.rsqrt(var + GN_EPS) * g


def _stack_heads(x):
    xb = x.astype(BF16)
    lane_head = _iota((1, W_GROUP), 1) >> HEAD_SHIFT
    return jnp.concatenate([jnp.where(lane_head == h, xb, jnp.zeros_like(xb)) for h in range(N_HEADS)],
                           axis=0)


def _each(f, *seqs):
    return [f(*args) for args in zip(*seqs)]


def _block_mm(x_list, y_list):
    return _each(lambda x, y: _dot(x, _stack_heads(y)), x_list, y_list)


def _unit_lower_inverse(a_list, L):
    t_idx = _iota((L, N_HEADS * L), 0)
    s_idx = _iota((L, N_HEADS * L), 1) & (L - 1)
    eye = jnp.where(t_idx == s_idx, 1.0, 0.0).astype(F32)
    in16 = (t_idx >> 4) == (s_idx >> 4)
    in32 = ((t_idx >> 5) == (s_idx >> 5)) & ((t_idx >> 4) > (s_idx >> 4))
    in64 = (t_idx >> 5) > (s_idx >> 5)
    x16 = _each(lambda a: jnp.where(in16, -a, 0.0), a_list)
    x2 = _block_mm(x16, x16)
    x4 = _block_mm(x2, x2)
    x8 = _block_mm(x4, x4)
    t = _each(lambda x: eye + x, x16)
    for xp in (x2, x4, x8):
        t = _each(lambda t_, d: t_ + d, t, _block_mm(t, xp))
    for sel in (in32, in64):
        w = _block_mm(_each(lambda a: jnp.where(sel, a, 0.0), a_list), t)
        t = _each(lambda t_, d: t_ - d, t, _block_mm(t, w))
    return t


def _rwkv_chunks(rcs, prev_rows, states, prm, ones, L):
    (mu, w0, a0, pkk, pka, prk, w2, a2, g2, norm) = prm
    first_row = _iota((L, RWKV_COLS), 0) == 0
    prevs = _each(lambda rc, pr: jnp.where(first_row, pr, pltpu.roll(rc, 1, 0)), rcs, prev_rows)
    vecs = _each(lambda rc, pv: _rwkv_vectors(rc, pv, mu, w0, a0, pkk, pka, prk, w2, a2, g2, ones),
                 rcs, prevs)
    r, lw, kx, v, kk, a, g, bonus = [list(col) for col in zip(*vecs)]

    tril = _tril_ones(L)
    lc = _each(lambda x: _dot_exact_rhs(tril, x), lw)
    akk = _each(lambda a_, k_: a_ * k_, a, kk)
    e_neg = _each(lambda x: jnp.exp(-x), lc)
    l_last = _each(lambda x: x[L - 1:L, :], lc)
    e_end = _each(lambda ll, x: jnp.exp(ll - x), l_last, lc)
    kr = _each(lambda k_, r_, c_, w_: jnp.concatenate([k_ * jnp.exp(c_ - w_), r_ * jnp.exp(c_)], axis=0),
               kk, r, lc, lw)
    a_hat = _each(lambda x, e: _stack_heads(x * e), akk, e_neg)
    k_hat = _each(lambda x, e: _stack_heads(x * e), kx, e_neg)
    v_bd = _each(_stack_heads, v)

    t_idx = _iota((L, N_HEADS * L), 0)
    s_idx = _iota((L, N_HEADS * L), 1) & (L - 1)
    strict = t_idx > s_idx
    incl = t_idx >= s_idx
    pa = _each(_dot_nt, kr, a_hat)
    pk = _each(_dot_nt, kr, k_hat)
    a_ua = _each(lambda p_: jnp.where(strict, p_[0:L], 0.0), pa)
    b_ra = _each(lambda p_: jnp.where(incl, p_[L:2 * L], 0.0), pa)
    ab_k = _each(lambda p_: jnp.concatenate([jnp.where(strict, p_[0:L], 0.0),
                                             jnp.where(incl, p_[L:2 * L], 0.0)], axis=0), pk)
    T = _unit_lower_inverse(a_ua, L)

    from_state = _each(_dot_nt, kr, states)
    from_v = _each(_dot, ab_k, v_bd)
    rhs = _each(lambda s_, v_: s_[0:L] + v_[0:L], from_state, from_v)
    U = _block_mm(T, rhs)
    corr = _block_mm(b_ra, U)
    o = _each(lambda s_, v_, c_: s_[L:2 * L] + v_[L:2 * L] - c_, from_state, from_v, corr)
    upd = _each(lambda v_, u_, k_, a_, e: _dot_tn(jnp.concatenate([v_, u_], axis=0),
                                                  jnp.concatenate([k_ * e, -(a_ * e)], axis=0)),
                v, U, kx, akk, e_end)
    s_new = _each(lambda s_, ll, d: s_ * jnp.exp(ll) + ones.astype(F32) * d, states, l_last, upd)
    y = _each(lambda o_, b_, g_: (_head_layernorm(o_, norm, ones) + b_) * g_, o, bonus, g)
    return y, s_new


def _rwkv_kernel(rc_ref, mu_ref, w0_ref, a0_ref, kk_ref, ka_ref, rk_ref, w2_ref, a2_ref, g2_ref,
                 norm_ref, y_ref, s_out, shift_out, s_scr, prev_scr, *, L, nc, G):
    c = pl.program_id(1)

    @pl.when(c == 0)
    def _():
        s_scr[...] = jnp.zeros_like(s_scr)
        prev_scr[...] = jnp.zeros_like(prev_scr)

    ones = _head_ones()
    prm = (mu_ref[...], w0_ref[...], a0_ref[...], kk_ref[...], ka_ref[...], rk_ref[...],
           w2_ref[...], a2_ref[...], g2_ref[...], norm_ref[...])
    rcs = [rc_ref[gi] for gi in range(G)]
    ys, s_new = _rwkv_chunks(rcs, [prev_scr[gi, 0:1, :] for gi in range(G)],
                             [s_scr[gi] for gi in range(G)], prm, ones, L)
    for gi in range(G):
        prev_scr[gi, 0:1, :] = rcs[gi][L - 1:L, :]
        s_scr[gi] = s_new[gi]
        y_ref[gi] = ys[gi].astype(y_ref.dtype)

    @pl.when(c == nc - 1)
    def _():
        s_out[...] = s_scr[...]
        shift_out[...] = jnp.broadcast_to(prev_scr[:, 0:1, :], shift_out.shape)


def _rwkv(proj, p, n_seq, t_len):
    L = CHUNK_RW
    G = SEQS_PER_STEP
    nc = t_len // L
    const = lambda shape: pl.BlockSpec(shape, lambda b, c: (0,) * len(shape))
    row = const((1, W_GROUP))
    pad = const((LANES, W_GROUP))
    y, s_new, shift = pl.pallas_call(
        functools.partial(_rwkv_kernel, L=L, nc=nc, G=G),
        grid=(n_seq // G, nc),
        in_specs=[
            pl.BlockSpec((G, L, RWKV_COLS), lambda b, c: (b, c, 0)),
            const((1, RWKV_COLS)), row, row, row, row, row, pad, pad, pad, row,
        ],
        out_specs=[
            pl.BlockSpec((G, L, W_GROUP), lambda b, c: (b, c, 0)),
            pl.BlockSpec((G, W_GROUP, W_GROUP), lambda b, c: (b, 0, 0)),
            pl.BlockSpec((G, 8, RWKV_COLS), lambda b, c: (b, 0, 0)),
        ],
        out_shape=[
            jax.ShapeDtypeStruct((n_seq, t_len, W_GROUP), BF16),
            jax.ShapeDtypeStruct((n_seq, W_GROUP, W_GROUP), F32),
            jax.ShapeDtypeStruct((n_seq, 8, RWKV_COLS), F32),
        ],
        scratch_shapes=[
            pltpu.VMEM((G, W_GROUP, W_GROUP), F32),
            pltpu.VMEM((G, 8, RWKV_COLS), F32),
        ],
        compiler_params=_cparams("arbitrary", "arbitrary"),
        name="rwkv_chunk",
    )(proj.reshape(n_seq, t_len, D_PROJ), p["rw_mu"], p["rw_w0"], p["rw_a0"], p["rw_kk"], p["rw_ka"],
      p["rw_rk"], p["rw_w2"], p["rw_a2"], p["rw_g2"], p["rw_norm"])
    return y.reshape(n_seq * t_len, W_GROUP), s_new, shift


FF_CHUNK = 256


def _mix_residual(x_ref, ys, wout_ref):
    acc = x_ref[...]
    for j, y in enumerate(ys):
        acc = acc + jnp.dot(y.astype(BF16), wout_ref[j * W_GROUP:(j + 1) * W_GROUP, :],
                            preferred_element_type=F32)
    return acc


def _ffn_body(x1, nrm_ref, wup_ref, cw_ref, cb_ref, wdn_ref, prev_rows, d_ff, act_scr):
    h2 = _rmsnorm_rows(x1, nrm_ref[...]).astype(BF16)
    for j in range(d_ff // FF_CHUNK):
        lo, hi = j * FF_CHUNK, (j + 1) * FF_CHUNK
        ug = jnp.dot(h2, wup_ref[:, lo:hi], preferred_element_type=F32)
        uv = jnp.dot(h2, wup_ref[:, d_ff + lo:d_ff + hi], preferred_element_type=F32)
        p2, p1 = prev_rows(j, ug)
        conv = (cb_ref[:, lo:hi] + cw_ref[0:1, lo:hi] * p2 + cw_ref[1:2, lo:hi] * p1
                + cw_ref[2:3, lo:hi] * ug)
        act_scr[:, lo:hi] = (_gelu(conv) * uv).astype(BF16)
    return x1 + jnp.dot(act_scr[...], wdn_ref[...], preferred_element_type=F32)


def _ffn_seq_kernel(x_ref, y0_ref, y1_ref, y2_ref, y3_ref, wout_ref, nrm_ref, wup_ref, cw_ref, cb_ref,
                    wdn_ref, nf_ref, o_ref, buf_out, carry_scr, act_scr, *, tm, d_ff, final):
    t = pl.program_id(1)

    @pl.when(t == 0)
    def _():
        carry_scr[...] = jnp.zeros_like(carry_scr)

    x1 = _mix_residual(x_ref, (y0_ref[...], y1_ref[...], y2_ref[...], y3_ref[...]), wout_ref)
    row = _iota((tm, FF_CHUNK), 0)

    def prev_rows(j, ug):
        lo, hi = j * FF_CHUNK, (j + 1) * FF_CHUNK
        c0 = carry_scr[0:1, lo:hi]
        c1 = carry_scr[1:2, lo:hi]
        p1 = jnp.where(row == 0, c1, pltpu.roll(ug, 1, 0))
        p2 = jnp.where(row == 0, c0, jnp.where(row == 1, c1, pltpu.roll(ug, 2, 0)))
        carry_scr[0:2, lo:hi] = ug[tm - 2:tm, :]
        return p2, p1

    out = _ffn_body(x1, nrm_ref, wup_ref, cw_ref, cb_ref, wdn_ref, prev_rows, d_ff, act_scr)
    if final:
        out = _rmsnorm_rows(out, nf_ref[...])
    o_ref[...] = out
    buf_out[0] = carry_scr[...]


def _ffn_step_kernel(x_ref, y0_ref, y1_ref, y2_ref, y3_ref, wout_ref, nrm_ref, wup_ref, cw_ref, cb_ref,
                     wdn_ref, nf_ref, p2_ref, p1_ref, o_ref, ug_out, act_scr, *, d_ff, final):
    x1 = _mix_residual(x_ref, (y0_ref[...], y1_ref[...], y2_ref[...], y3_ref[...]), wout_ref)

    def prev_rows(j, ug):
        lo, hi = j * FF_CHUNK, (j + 1) * FF_CHUNK
        ug_out[:, lo:hi] = ug
        return p2_ref[:, lo:hi], p1_ref[:, lo:hi]

    out = _ffn_body(x1, nrm_ref, wup_ref, cw_ref, cb_ref, wdn_ref, prev_rows, d_ff, act_scr)
    if final:
        out = _rmsnorm_rows(out, nf_ref[...])
    o_ref[...] = out


def _ffn_seq(x2d, ys, p, norm_final, n_seq, t_len, tm, final):
    n, d = x2d.shape
    d_ff = p["ffn_wdn"].shape[1]
    lay = p["layer"]
    nt = t_len // tm
    rows = lambda w: pl.BlockSpec((tm, w), lambda b, t: (b * nt + t, 0))
    const = lambda shape: pl.BlockSpec(shape, lambda b, t: (0,) * len(shape))
    layered = lambda shape: pl.BlockSpec((None,) + shape, lambda b, t: (lay, 0, 0),
                                         pipeline_mode=pl.Buffered(1))
    return pl.pallas_call(
        functools.partial(_ffn_seq_kernel, tm=tm, d_ff=d_ff, final=final),
        grid=(n_seq, nt),
        in_specs=[rows(d)] + [rows(W_GROUP)] * 4 + [
            layered((d, d)), const((1, d)), layered((d, 2 * d_ff)), const((CONV_W, d_ff)),
            const((1, d_ff)), layered((d_ff, d)), const((1, d))],
        out_specs=[rows(d), pl.BlockSpec((1, CONV_W - 1, d_ff), lambda b, t: (b, 0, 0))],
        out_shape=[jax.ShapeDtypeStruct((n, d), F32),
                   jax.ShapeDtypeStruct((n_seq, CONV_W - 1, d_ff), F32)],
        scratch_shapes=[pltpu.VMEM((CONV_W - 1, d_ff), F32), pltpu.VMEM((tm, d_ff), BF16)],
        compiler_params=_cparams("arbitrary", "arbitrary"),
        name="wout_ffn_seq",
    )(x2d, *ys, p["w_out"], p["norm_ffn"], p["ffn_wup"], p["ffn_cw"], p["ffn_cb"], p["ffn_wdn"],
      norm_final)


def _ffn_step(x2d, ys, p, norm_final, prev2, prev1, final):
    n, d = x2d.shape
    d_ff = p["ffn_wdn"].shape[1]
    lay = p["layer"]
    full = lambda shape: pl.BlockSpec(shape, lambda i: (0,) * len(shape))
    layered = lambda shape: pl.BlockSpec((None,) + shape, lambda i: (lay, 0, 0))
    return pl.pallas_call(
        functools.partial(_ffn_step_kernel, d_ff=d_ff, final=final),
        grid=(1,),
        in_specs=[full((n, d))] + [full((n, W_GROUP))] * 4 + [
            layered((d, d)), full((1, d)), layered((d, 2 * d_ff)), full((CONV_W, d_ff)),
            full((1, d_ff)), layered((d_ff, d)), full((1, d)), full((n, d_ff)), full((n, d_ff))],
        out_specs=[full((n, d)), full((n, d_ff))],
        out_shape=[jax.ShapeDtypeStruct((n, d), F32), jax.ShapeDtypeStruct((n, d_ff), F32)],
        scratch_shapes=[pltpu.VMEM((n, d_ff), BF16)],
        compiler_params=_cparams("arbitrary"),
        name="wout_ffn_step",
    )(x2d, *ys, p["w_out"], p["norm_ffn"], p["ffn_wup"], p["ffn_cw"], p["ffn_cb"], p["ffn_wdn"],
      norm_final, prev2, prev1)


def _step_vec_kernel(proj_ref, shift_ref, m_ref, brow_ref, wa_ref, ba_ref, mu_ref, w0_ref, a0_ref,
                     kk_ref, ka_ref, rk_ref, w2_ref, a2_ref, g2_ref,
                     pack_out, m_out, g_out, bonus_out):
    ones = _head_ones()
    sm = proj_ref[:, OFF_SMALL:OFF_SMALL + SMALL_W]
    pre = sm + brow_ref[...]
    logf = pltpu.roll(_log_sigmoid(pre), LANES - N_HEADS, 1)
    m_old = m_ref[...]
    m_new = jnp.maximum(logf + m_old, pre)
    m_out[...] = m_new
    lane = _iota((SMALL_W, W_GROUP), 0)
    spread_mat = jnp.where(lane == LANE_MI + (_iota((SMALL_W, W_GROUP), 1) >> HEAD_SHIFT), 1.0, 0.0).astype(BF16)
    gate_lanes = (_iota(pre.shape, 1) >= LANE_MI) & (_iota(pre.shape, 1) < LANE_MI + N_HEADS)
    spread = lambda x: _dot_exact_lhs(jnp.where(gate_lanes, x, 0.0), spread_mat)
    la = _log_sigmoid(_dot(sm, wa_ref[...]) + ba_ref[...]) * (1.0 / GLA_GATE_TEMP)
    rc = proj_ref[:, 0:RWKV_COLS]
    rr, lw, kt, rv, kk, a, g, bonus = _rwkv_vectors(
        rc, shift_ref[...], mu_ref[...], w0_ref[...], a0_ref[...], kk_ref[...], ka_ref[...],
        rk_ref[...], w2_ref[...], a2_ref[...], g2_ref[...], ones)
    main = lambda blk: proj_ref[:, blk * W_GROUP:(blk + 1) * W_GROUP]
    rows = {
        ROW_MQ: main(BLK_MQ), ROW_MK: main(BLK_MK), ROW_MV: main(BLK_MV),
        ROW_GQ: main(BLK_GQ), ROW_GK: main(BLK_GK), ROW_GV: main(BLK_GV), ROW_GA: jnp.exp(la),
        ROW_KK: kk, ROW_AKK: a * kk, ROW_W: jnp.exp(lw), ROW_KT: kt, ROW_RR: rr, ROW_RV: rv,
        ROW_IW: spread(jnp.exp(pre - m_new)), ROW_FW: spread(jnp.exp(logf + m_old - m_new)),
        ROW_M: spread(m_new),
    }
    for i, val in rows.items():
        pack_out[i] = val.T
    g_out[...] = g
    bonus_out[...] = bonus


def _step_vectors(proj, shift, m_pad, p):
    n = proj.shape[0]
    full = lambda a: pl.BlockSpec(a.shape, lambda i: (0,) * a.ndim)
    args = (proj, shift, m_pad, p["ml_brow"], p["gla_wa"], p["gla_ba"], p["rw_mu"], p["rw_w0"],
            p["rw_a0"], p["rw_kk"], p["rw_ka"], p["rw_rk"], p["rw_w2"], p["rw_a2"], p["rw_g2"])
    wide = jax.ShapeDtypeStruct((n, W_GROUP), F32)
    outs = [jax.ShapeDtypeStruct((N_PACK, W_GROUP, n), F32), jax.ShapeDtypeStruct((n, SMALL_W), F32),
            wide, wide]
    return pl.pallas_call(
        _step_vec_kernel,
        grid=(1,),
        in_specs=[full(a) for a in args],
        out_specs=[pl.BlockSpec(o.shape, lambda i, nd=len(o.shape): (0,) * nd) for o in outs],
        out_shape=outs,
        compiler_params=_cparams("arbitrary"),
        name="step_vectors",
    )(*args)


(ROW_MQ, ROW_MK, ROW_MV, ROW_GQ, ROW_GK, ROW_GV, ROW_GA, ROW_KK, ROW_AKK, ROW_W, ROW_KT, ROW_RR, ROW_RV,
 ROW_IW, ROW_FW, ROW_M) = range(16)
N_PACK = 16
(OROW_ML, OROW_GLA, OROW_RW, OROW_N) = range(4)
N_OPACK = 4
STEP_UNROLL = 4


def _heads_step_kernel(pack_ref, c_ref, n_ref, gs_ref, rs_ref, c_out, gs_out, rs_out, opack_out):
    vec = lambda i: pack_ref[i]
    one = lambda i, j: pack_ref[i, pl.ds(j, 1), :]
    colsum = lambda x: jnp.sum(x, axis=0, keepdims=True)

    k_ml = vec(ROW_MK) * (HEAD_DIM ** -0.5)
    q_ml = vec(ROW_MQ)
    iw = pack_ref[ROW_IW, 0:1, :]
    fw = pack_ref[ROW_FW, 0:1, :]
    n_new = fw * n_ref[...] + iw * k_ml
    opack_out[OROW_N] = n_new
    r_den = 1.0 / jnp.maximum(jnp.abs(colsum(n_new * q_ml)), jnp.exp(-pack_ref[ROW_M, 0:1, :]))

    def ml_body(v, carry):
        c_new = fw * c_ref[v] + (iw * one(ROW_MV, v)) * k_ml
        c_out[v] = c_new
        opack_out[OROW_ML, pl.ds(v, 1), :] = colsum(c_new * q_ml) * r_den
        return carry

    lax.fori_loop(0, HEAD_DIM, ml_body, 0, unroll=STEP_UNROLL)

    v_gla = vec(ROW_GV)

    def gla_body(k, acc):
        s_new = one(ROW_GA, k) * gs_ref[k] + one(ROW_GK, k) * v_gla
        gs_out[k] = s_new
        return acc + (one(ROW_GQ, k) * (HEAD_DIM ** -0.5)) * s_new

    opack_out[OROW_GLA] = lax.fori_loop(0, HEAD_DIM, gla_body, jnp.zeros_like(v_gla), unroll=STEP_UNROLL)

    kk, akk, w, kt, rr = vec(ROW_KK), vec(ROW_AKK), vec(ROW_W), vec(ROW_KT), vec(ROW_RR)

    def rw_body(v, carry):
        s = rs_ref[v]
        s_new = s * w - colsum(s * kk) * akk + one(ROW_RV, v) * kt
        rs_out[v] = s_new
        opack_out[OROW_RW, pl.ds(v, 1), :] = colsum(s_new * rr)
        return carry

    lax.fori_loop(0, HEAD_DIM, rw_body, 0, unroll=STEP_UNROLL)


def _heads_step(pack, ml_c, ml_n, gla_s, rw_s, layer):
    n = pack.shape[-1]
    sq = (HEAD_DIM, HEAD_DIM, n)
    head = lambda lead: pl.BlockSpec((lead, HEAD_DIM, n), lambda h: (0, h, 0))
    at_layer = lambda shape: pl.BlockSpec((None,) + shape, lambda h: (layer * N_HEADS + h,) + (0,) * len(shape))
    out_state = pl.BlockSpec((None,) + sq, lambda h: (h, 0, 0, 0))
    state = jax.ShapeDtypeStruct((N_HEADS,) + sq, F32)
    return pl.pallas_call(
        _heads_step_kernel,
        grid=(N_HEADS,),
        in_specs=[head(N_PACK), at_layer(sq), at_layer((HEAD_DIM, n)), at_layer(sq), at_layer(sq)],
        out_specs=[out_state, out_state, out_state, head(N_OPACK)],
        out_shape=[state, state, state, jax.ShapeDtypeStruct((N_OPACK, W_GROUP, n), F32)],
        compiler_params=_cparams("arbitrary"),
        name="heads_step",
    )(pack, ml_c, ml_n, gla_s, rw_s)


def _step_post_kernel(hml_ref, mo_ref, ogla_ref, gg_ref, orw_ref, g_ref, bonus_ref,
                      mln_ref, gln_ref, rwn_ref, yml_out, ygla_out, yrw_out):
    ones = _head_ones()
    hml = hml_ref[...].T
    ms = _head_sum(hml * hml, ones) * (1.0 / HEAD_DIM)
    yml_out[...] = (hml * lax.rsqrt(ms + EPS) * mln_ref[...] * _sigmoid(mo_ref[...])).astype(BF16)
    og = ogla_ref[...].T
    ms = _head_sum(og * og, ones) * (1.0 / HEAD_DIM)
    gg = gg_ref[...]
    ygla_out[...] = (og * lax.rsqrt(ms + EPS) * gln_ref[...] * (gg * _sigmoid(gg))).astype(BF16)
    yrw_out[...] = ((_head_layernorm(orw_ref[...].T, rwn_ref[...], ones) + bonus_ref[...])
                    * g_ref[...]).astype(BF16)


def _step_post(hml, mo, ogla, gg, orw, g, bonus, p):
    n = mo.shape[0]
    args = (hml, mo, ogla, gg, orw, g, bonus, p["ml_norm"], p["gla_norm"], p["rw_norm"])
    full = lambda a: pl.BlockSpec(a.shape, lambda i: (0,) * a.ndim)
    out = jax.ShapeDtypeStruct((n, W_GROUP), BF16)
    return pl.pallas_call(
        _step_post_kernel,
        grid=(1,),
        in_specs=[full(a) for a in args],
        out_specs=[pl.BlockSpec((n, W_GROUP), lambda i: (0, 0))] * 3,
        out_shape=[out] * 3,
        compiler_params=_cparams("arbitrary"),
        name="step_post",
    )(*args)


def _layer_params(l, prm):
    p = {}
    w_in = prm["w_in"][l]
    d = w_in.shape[0]
    sizes = (W_GROUP, W_GROUP, W_GROUP, W_GROUP, N_HEADS, N_HEADS, W_GROUP, W_GROUP, W_GROUP, W_GROUP,
             GLA_RANK, W_GROUP, RWKV_COLS)
    offs = [0]
    for s in sizes:
        offs.append(offs[-1] + s)
    col = lambda i: w_in[:, offs[i]:offs[i + 1]]
    (u, mq, mk, mv, mi, mf, mo, gq, gk, gv, ga, gg, rcols) = [col(i) for i in range(13)]
    small = jnp.concatenate([ga, mi, mf, jnp.zeros((d, SMALL_W - GLA_RANK - 2 * N_HEADS), F32)], axis=1)
    p["w_in"] = jnp.concatenate([rcols, small, u, mq, mk, mv, mo, gq, gk, gv, gg], axis=1).astype(BF16)
    p["w_gate_t"] = jnp.concatenate([mi, mf], axis=1).T.astype(BF16)
    p["norm_mix"] = prm["norm_mix"][l][None, :]

    lam = lax.complex(prm["s5_lam_re"][l], prm["s5_lam_im"][l])
    dt = jnp.exp(prm["s5_log_dt"][l])[:, None]
    lam_bar = jnp.exp(lam * dt)
    b_bar = ((lam_bar - 1.0) / lam)[..., None] * lax.complex(prm["s5_b_re"][l], prm["s5_b_im"][l])
    eye = jnp.eye(S5_GROUPS, dtype=F32)
    bm = lambda b: jnp.einsum("gph,gk->ghkp", b, eye).reshape(W_GROUP, S5_WIDTH)
    p["bmat"] = jnp.concatenate([bm(b_bar.real), bm(b_bar.imag)], axis=1).astype(BF16)
    cm = lambda c: jnp.einsum("ghp,gk->kpgh", c, eye).reshape(S5_WIDTH, W_GROUP)
    p["cmat"] = jnp.concatenate([cm(prm["s5_c_re"][l]), -cm(prm["s5_c_im"][l])], axis=0).astype(BF16)
    p["lre"] = lam_bar.real.reshape(1, S5_WIDTH)
    p["lim"] = lam_bar.imag.reshape(1, S5_WIDTH)
    p["d"] = prm["s5_d"][l][None, :]
    p["wglu"] = prm["s5_w_glu"][l].astype(BF16)

    bias = prm["ml_gate_bias"][l]
    p["ml_brow"] = jnp.zeros((1, SMALL_W), F32).at[0, LANE_MI:LANE_MI + 2 * N_HEADS].set(bias)
    p["ml_bcol"] = jnp.broadcast_to(bias[:, None], (2 * N_HEADS, CHUNK_ML))
    p["ml_norm"] = prm["ml_norm"][l][None, :]

    p["gla_wa"] = jnp.zeros((SMALL_W, W_GROUP), F32).at[0:GLA_RANK].set(prm["gla_w_alpha"][l]).astype(BF16)
    p["gla_ba"] = prm["gla_b_alpha"][l][None, :]
    p["gla_norm"] = prm["gla_norm"][l][None, :]

    p["rw_mu"] = prm["rw_mu"][l][None, :]
    p["rw_w0"] = prm["rw_w0"][l][None, :]
    p["rw_a0"] = prm["rw_a0"][l][None, :]
    p["rw_kk"] = prm["rw_k_k"][l][None, :]
    p["rw_ka"] = prm["rw_k_a"][l][None, :]
    p["rw_rk"] = prm["rw_r_k"][l].reshape(1, W_GROUP)
    z = jnp.zeros((LANES, W_GROUP), F32)
    p["rw_w2"] = z.at[0:RWKV_W_RANK].set(prm["rw_w2"][l]).astype(BF16)
    p["rw_a2"] = z.at[RWKV_W_RANK:RWKV_W_RANK + RWKV_A_RANK].set(prm["rw_a2"][l]).astype(BF16)
    p["rw_g2"] = z.at[RWKV_W_RANK + RWKV_A_RANK:].set(prm["rw_g2"][l]).astype(BF16)
    p["rw_norm"] = prm["rw_norm"][l][None, :]

    p["layer"] = l
    p["w_out"] = prm["w_out"].astype(BF16)
    p["norm_ffn"] = prm["norm_ffn"][l][None, :]
    p["ffn_wup"] = prm["ffn_w_up"].astype(BF16)
    p["ffn_cw"] = prm["ffn_conv_w"][l]
    p["ffn_cb"] = prm["ffn_conv_b"][l][None, :]
    p["ffn_wdn"] = prm["ffn_w_down"].astype(BF16)
    return p


def _diag_blocks(m):
    b = m.shape[0]
    m = m.reshape(b, N_HEADS, HEAD_DIM, N_HEADS, HEAD_DIM)
    return jnp.stack([m[:, h, :, h, :] for h in range(N_HEADS)], axis=1)


def _prompt_layer(x2d, p, norm_final, n_seq, t_len, final):
    proj, gt = _inproj(x2d, p["norm_mix"], p["w_in"], p["w_gate_t"], n_seq, t_len, tm=512)

    u_tb = proj[:, OFF_MAIN:OFF_MAIN + W_GROUP].reshape(n_seq, t_len, W_GROUP).transpose(1, 0, 2)
    zeros = jnp.zeros((n_seq, S5_WIDTH), F32)
    y_s5_tb, s5_re, s5_im = _s5(u_tb.reshape(t_len * n_seq, W_GROUP), zeros, zeros, p, n_seq, t_len, tt=128,
                                seq_major=False)
    y_s5 = y_s5_tb.reshape(t_len, n_seq, W_GROUP).transpose(1, 0, 2).reshape(n_seq * t_len, W_GROUP)

    y_ml, ml_c, ml_n, ml_m = _mlstm(proj, gt, p, n_seq, t_len)
    y_gla, gla_st = _gla(proj, p, n_seq, t_len)
    y_rw, rw_s, rw_shift = _rwkv(proj, p, n_seq, t_len)

    x_new, ffn_buf = _ffn_seq(x2d, (y_s5, y_ml, y_gla, y_rw), p, norm_final, n_seq, t_len, tm=512,
                              final=final)
    states = (
        s5_re.reshape(n_seq, S5_GROUPS, S5_STATE),
        s5_im.reshape(n_seq, S5_GROUPS, S5_STATE),
        _diag_blocks(ml_c),
        ml_n[:, 0, :].reshape(n_seq, N_HEADS, HEAD_DIM),
        ml_m[:, 0, LANE_MI:LANE_MI + N_HEADS],
        jnp.swapaxes(_diag_blocks(gla_st), -1, -2),
        _diag_blocks(rw_s),
        rw_shift[:, 0, :],
        ffn_buf,
    )
    return x_new, states


def _sample_layer(x2d, st, pair_states, layer, p, norm_final, final):
    (s5_re, s5_im, ml_m, rw_shift, ffn_buf) = st
    n = x2d.shape[0]
    npairs = n * N_HEADS
    proj, _ = _inproj(x2d, p["norm_mix"], p["w_in"], p["w_gate_t"], 1, n, tm=n)

    y_s5, s5_re_new, s5_im_new = _s5(proj[:, OFF_MAIN:OFF_MAIN + W_GROUP],
                                     s5_re.reshape(n, S5_WIDTH), s5_im.reshape(n, S5_WIDTH),
                                     p, n, 1, tt=1, seq_major=False)

    m_pad = jnp.zeros((n, SMALL_W), F32).at[:, LANE_MI:LANE_MI + N_HEADS].set(ml_m)
    pack, m_new, g, bonus = _step_vectors(proj, rw_shift, m_pad, p)
    ml_c_new, gla_s_new, rw_s_new, opack = _heads_step(pack, *pair_states, layer)

    main = lambda blk: proj[:, blk * W_GROUP:(blk + 1) * W_GROUP]
    y_ml, y_gla, y_rw = _step_post(opack[OROW_ML], main(BLK_MO), opack[OROW_GLA], main(BLK_GG),
                                   opack[OROW_RW], g, bonus, p)
    x_new, ug = _ffn_step(x2d, (y_s5, y_ml, y_gla, y_rw), p, norm_final, ffn_buf[:, 0, :], ffn_buf[:, 1, :],
                          final)
    states = (
        s5_re_new.reshape(n, S5_GROUPS, S5_STATE),
        s5_im_new.reshape(n, S5_GROUPS, S5_STATE),
        ml_c_new.transpose(3, 0, 1, 2),
        opack[OROW_N].reshape(N_HEADS, HEAD_DIM, n).transpose(2, 0, 1),
        m_new[:, LANE_MI:LANE_MI + N_HEADS],
        gla_s_new.transpose(3, 0, 1, 2),
        rw_s_new.transpose(3, 0, 1, 2),
        proj[:, 0:RWKV_COLS],
        jnp.stack([ffn_buf[:, 1, :], ug], axis=1),
    )
    return x_new, states


def kernel(x_prompt, x_sample, state_s5_re, state_s5_im, state_mlstm_C, state_mlstm_n, state_mlstm_m, state_gla_S, state_rwkv_S, state_rwkv_shift, state_ffn_conv, norm_mix, w_in, s5_lam_re, s5_lam_im, s5_log_dt, s5_b_re, s5_b_im, s5_c_re, s5_c_im, s5_d, s5_w_glu, ml_gate_bias, ml_norm, gla_w_alpha, gla_b_alpha, gla_norm, rw_mu, rw_w0, rw_w2, rw_a0, rw_a2, rw_g2, rw_k_k, rw_k_a, rw_r_k, rw_norm, w_out, norm_ffn, ffn_w_up, ffn_conv_w, ffn_conv_b, ffn_w_down, norm_final):
    prm = dict(norm_mix=norm_mix, w_in=w_in, s5_lam_re=s5_lam_re, s5_lam_im=s5_lam_im, s5_log_dt=s5_log_dt,
               s5_b_re=s5_b_re, s5_b_im=s5_b_im, s5_c_re=s5_c_re, s5_c_im=s5_c_im, s5_d=s5_d,
               s5_w_glu=s5_w_glu, ml_gate_bias=ml_gate_bias, ml_norm=ml_norm, gla_w_alpha=gla_w_alpha,
               gla_b_alpha=gla_b_alpha, gla_norm=gla_norm, rw_mu=rw_mu, rw_w0=rw_w0, rw_w2=rw_w2,
               rw_a0=rw_a0, rw_a2=rw_a2, rw_g2=rw_g2, rw_k_k=rw_k_k, rw_k_a=rw_k_a, rw_r_k=rw_r_k,
               rw_norm=rw_norm, w_out=w_out, norm_ffn=norm_ffn, ffn_w_up=ffn_w_up, ffn_conv_w=ffn_conv_w,
               ffn_conv_b=ffn_conv_b, ffn_w_down=ffn_w_down)
    depth = w_in.shape[0]
    n_seq, t_len, d = x_prompt.shape
    n_smp = x_sample.shape[0]
    assert t_len % 512 == 0 and x_sample.shape[1] == 1
    nf = norm_final[None, :]
    small_states = (state_s5_re, state_s5_im, state_mlstm_m, state_rwkv_shift, state_ffn_conv)
    lead = depth * N_HEADS
    to_lanes = lambda st: jnp.moveaxis(st, 1, -1)
    pair_states = (to_lanes(state_mlstm_C).reshape(lead, HEAD_DIM, HEAD_DIM, n_smp),
                   to_lanes(state_mlstm_n).reshape(lead, HEAD_DIM, n_smp),
                   to_lanes(state_gla_S).reshape(lead, HEAD_DIM, HEAD_DIM, n_smp),
                   to_lanes(state_rwkv_S).reshape(lead, HEAD_DIM, HEAD_DIM, n_smp))

    xp = x_prompt.reshape(n_seq * t_len, d)
    xs = x_sample.reshape(n_smp, d)
    p_states, s_states = [], []
    for l in range(depth):
        p = _layer_params(l, prm)
        final = l == depth - 1
        xp, ps = _prompt_layer(xp, p, nf, n_seq, t_len, final)
        xs, ss = _sample_layer(xs, tuple(s[l] for s in small_states), pair_states, l, p, nf, final)
        p_states.append(ps)
        s_states.append(ss)
    new_p = tuple(jnp.stack([st[i] for st in p_states]) for i in range(9))
    new_s = tuple(jnp.stack([st[i] for st in s_states]) for i in range(9))
    return (xp.reshape(n_seq, t_len, d), xs.reshape(n_smp, 1, d)) + new_p + new_s
```

```python
import functools
import math

import jax
import jax.numpy as jnp
from jax import lax
from jax.experimental import pallas as pl
from jax.experimental.pallas import tpu as pltpu

F32 = jnp.float32
BF16 = jnp.bfloat16

LANES = 128
W_GROUP = 256
HEAD_DIM = 64
HEAD_SHIFT = 6
N_HEADS = 4
S5_CH = 16
S5_GROUPS = 16
S5_STATE = 64
S5_WIDTH = S5_GROUPS * S5_STATE
GLA_RANK = 16
GLA_GATE_TEMP = 16.0
RWKV_W_RANK = 32
RWKV_A_RANK = 32
RWKV_G_RANK = 64
RWKV_COLS = 3 * W_GROUP + RWKV_W_RANK + RWKV_A_RANK + RWKV_G_RANK
RWKV_DECAY_SCALE = 0.6065306597126334
CONV_W = 3
EPS = 1e-6
GN_EPS = 64e-5
NEG_BIG = -1e30
EXP_CLAMP = 80.0

SMALL_W = LANES
OFF_SMALL = RWKV_COLS
OFF_MAIN = RWKV_COLS + SMALL_W
D_PROJ = OFF_MAIN + 9 * W_GROUP
LANE_MI = GLA_RANK
LANE_MF = GLA_RANK + N_HEADS
(BLK_U, BLK_MQ, BLK_MK, BLK_MV, BLK_MO, BLK_GQ, BLK_GK, BLK_GV, BLK_GG) = range(
    OFF_MAIN // W_GROUP, OFF_MAIN // W_GROUP + 9)
BLK_SMALL = OFF_SMALL // SMALL_W

CHUNK_ML = 128
CHUNK_RW = 64
SEQS_PER_STEP = 8
SEQS_PER_STEP_ML = 4
VMEM_LIMIT = 56 * 1024 * 1024


def _cparams(*sem):
    return pltpu.CompilerParams(dimension_semantics=sem, vmem_limit_bytes=VMEM_LIMIT)


def _dot(a, b):
    return jnp.dot(a.astype(BF16), b.astype(BF16), preferred_element_type=F32)


def _dot_nt(a, b):
    return lax.dot_general(a.astype(BF16), b.astype(BF16), (((1,), (1,)), ((), ())),
                           preferred_element_type=F32)


def _dot_tn(a, b):
    return lax.dot_general(a.astype(BF16), b.astype(BF16), (((0,), (0,)), ((), ())),
                           preferred_element_type=F32)


def _split2(x):
    hi = x.astype(BF16)
    lo = (x - hi.astype(F32)).astype(BF16)
    return hi, lo


def _dot_exact_rhs(a01, x):
    hi, lo = _split2(x)
    f = lambda p: jnp.dot(a01, p, preferred_element_type=F32)
    return f(hi) + f(lo)


def _dot_exact_lhs(x, b01):
    hi, lo = _split2(x)
    f = lambda p: jnp.dot(p, b01, preferred_element_type=F32)
    return f(hi) + f(lo)


def _sigmoid(x):
    return 1.0 / (1.0 + jnp.exp(-x))


def _log_sigmoid(x):
    return jnp.minimum(x, 0.0) - jnp.log(1.0 + jnp.exp(-jnp.abs(x)))


def _gelu(x):
    return 0.5 * x * (1.0 + jnp.tanh(math.sqrt(2.0 / math.pi) * (x + 0.044715 * (x * x * x))))


def _iota(shape, dim):
    return lax.broadcasted_iota(jnp.int32, shape, dim)


def _head_ones():
    r = _iota((W_GROUP, W_GROUP), 0) >> HEAD_SHIFT
    c = _iota((W_GROUP, W_GROUP), 1) >> HEAD_SHIFT
    return jnp.where(r == c, 1.0, 0.0).astype(BF16)


def _head_sum(x, ones):
    return _dot_exact_lhs(x, ones)


def _head_lane_mask(h, rows=1):
    lane = _iota((rows, W_GROUP), 1) >> HEAD_SHIFT
    return jnp.where(lane == h, 1.0, 0.0).astype(F32)


def _rmsnorm_rows(x, g):
    ms = jnp.mean(x * x, axis=-1, keepdims=True)
    return x * lax.rsqrt(ms + EPS) * g


def _tril_ones(n, strict=False):
    r = _iota((n, n), 0)
    c = _iota((n, n), 1)
    return jnp.where((r > c) if strict else (r >= c), 1.0, 0.0).astype(BF16)


def _triu_ones(n):
    r = _iota((n, n), 0)
    c = _iota((n, n), 1)
    return jnp.where(r <= c, 1.0, 0.0).astype(BF16)


def _store_head_blocks(out_ref, scr_ref, transpose=False):
    for g in range(scr_ref.shape[0]):
        m = scr_ref[g].T if transpose else scr_ref[g]
        for h in range(N_HEADS):
            lo, hi = h * HEAD_DIM, (h + 1) * HEAD_DIM
            out_ref[g, h] = m[lo:hi, lo:hi]


def _HEAD_BLOCKS_SPEC(G):
    return pl.BlockSpec((G, N_HEADS, HEAD_DIM, HEAD_DIM), lambda b, c: (b, 0, 0, 0))


def _inproj_kernel(x_ref, g_ref, w_ref, wg_ref, proj_ref, gt_ref):
    hn = _rmsnorm_rows(x_ref[...], g_ref[...]).astype(BF16)
    proj_ref[...] = jnp.dot(hn, w_ref[...], preferred_element_type=F32)
    gt_ref[0] = lax.dot_general(wg_ref[...], hn, (((1,), (1,)), ((), ())),
                                preferred_element_type=F32)


def _inproj(x2d, g, w, wg, n_seq, t_len, tm):
    n, d = x2d.shape
    tiles_per_seq = t_len // tm
    return pl.pallas_call(
        _inproj_kernel,
        grid=(n // tm,),
        in_specs=[
            pl.BlockSpec((tm, d), lambda i: (i, 0)),
            pl.BlockSpec((1, d), lambda i: (0, 0)),
            pl.BlockSpec((d, D_PROJ), lambda i: (0, 0)),
            pl.BlockSpec((8, d), lambda i: (0, 0)),
        ],
        out_specs=[
            pl.BlockSpec((tm, D_PROJ), lambda i: (i, 0)),
            pl.BlockSpec((1, 8, tm), lambda i: (i // tiles_per_seq, 0, i % tiles_per_seq)),
        ],
        out_shape=[
            jax.ShapeDtypeStruct((n, D_PROJ), F32),
            jax.ShapeDtypeStruct((n_seq, 8, t_len), F32),
        ],
        compiler_params=_cparams("arbitrary"),
        name="inproj",
    )(x2d, g, w, wg)


def _s5_kernel(u_ref, h0re_ref, h0im_ref, bmat_ref, lre_ref, lim_ref, cmat_ref, d_ref, wglu_ref,
               y_ref, hre_out, him_out, bu_scr, hs_scr, y_scr, hre_scr, him_scr, *, bp, tt, seq_major):
    i = pl.program_id(0)

    @pl.when(i == 0)
    def _():
        hre_scr[...] = h0re_ref[...]
        him_scr[...] = h0im_ref[...]

    nt = S5_WIDTH // LANES
    u = u_ref[...].reshape(tt * bp, W_GROUP)
    bu = jnp.dot(u.astype(BF16), bmat_ref[...], preferred_element_type=F32)
    for j in range(2 * nt):
        bu_scr[j] = bu[:, j * LANES:(j + 1) * LANES]
    lre = [lre_ref[:, j * LANES:(j + 1) * LANES] for j in range(nt)]
    lim = [lim_ref[:, j * LANES:(j + 1) * LANES] for j in range(nt)]

    def step(t, carry):
        dst = pl.ds(pl.multiple_of(t * bp, bp), bp)
        src = pl.ds(t, bp, stride=tt) if seq_major else dst
        new = []
        for j in range(nt):
            hre, him = carry[j], carry[nt + j]
            new.append((lre[j] * hre - lim[j] * him + bu_scr[j, src, :],
                        lre[j] * him + lim[j] * hre + bu_scr[nt + j, src, :]))
        out = tuple(x[0] for x in new) + tuple(x[1] for x in new)
        for j in range(2 * nt):
            hs_scr[j, dst, :] = out[j]
        return out

    carry = tuple(hre_scr[:, j * LANES:(j + 1) * LANES] for j in range(nt)) + tuple(
        him_scr[:, j * LANES:(j + 1) * LANES] for j in range(nt))
    carry = step(0, carry) if tt == 1 else lax.fori_loop(0, tt, step, carry)
    hre = jnp.concatenate(carry[0:nt], axis=1)
    him = jnp.concatenate(carry[nt:2 * nt], axis=1)
    hre_scr[...] = hre
    him_scr[...] = him
    hre_out[...] = hre
    him_out[...] = him
    hs = jnp.concatenate([hs_scr[j] for j in range(2 * nt)], axis=1)
    y = jnp.dot(hs.astype(BF16), cmat_ref[...], preferred_element_type=F32)
    if seq_major:
        for j in range(W_GROUP // LANES):
            y_scr[j] = y[:, j * LANES:(j + 1) * LANES]
        y = jnp.concatenate(
            [jnp.concatenate([y_scr[j, pl.ds(b, tt, stride=bp), :] for j in range(W_GROUP // LANES)], axis=1)
             for b in range(bp)], axis=0)
    y = y + d_ref[...] * u
    z = _gelu(y)
    y_ref[...] = (z * _sigmoid(_dot(z, wglu_ref[...]))).astype(y_ref.dtype).reshape(y_ref.shape)


def _s5(u, h0re, h0im, sp, bp, t_len, tt, seq_major):
    full = lambda shape: pl.BlockSpec(shape, lambda i: (0,) * len(shape))
    if seq_major:
        u_spec = pl.BlockSpec((bp, tt, W_GROUP), lambda i: (0, i, BLK_U))
        y_spec = pl.BlockSpec((bp, tt, W_GROUP), lambda i: (0, i, 0))
        y_shape = jax.ShapeDtypeStruct((bp, t_len, W_GROUP), BF16)
    else:
        u_spec = y_spec = pl.BlockSpec((tt * bp, W_GROUP), lambda i: (i, 0))
        y_shape = jax.ShapeDtypeStruct((t_len * bp, W_GROUP), BF16)
    return pl.pallas_call(
        functools.partial(_s5_kernel, bp=bp, tt=tt, seq_major=seq_major),
        grid=(t_len // tt,),
        in_specs=[
            u_spec,
            full((bp, S5_WIDTH)), full((bp, S5_WIDTH)),
            full((W_GROUP, 2 * S5_WIDTH)),
            full((bp, S5_WIDTH)), full((bp, S5_WIDTH)),
            full((2 * S5_WIDTH, W_GROUP)),
            full((1, W_GROUP)),
            full((W_GROUP, W_GROUP)),
        ],
        out_specs=[
            y_spec,
            full((bp, S5_WIDTH)), full((bp, S5_WIDTH)),
        ],
        out_shape=[
            y_shape,
            jax.ShapeDtypeStruct((bp, S5_WIDTH), F32),
            jax.ShapeDtypeStruct((bp, S5_WIDTH), F32),
        ],
        scratch_shapes=[
            pltpu.VMEM((2 * S5_WIDTH // LANES, tt * bp, LANES), F32),
            pltpu.VMEM((2 * S5_WIDTH // LANES, tt * bp, LANES), F32),
            pltpu.VMEM((W_GROUP // LANES, tt * bp, LANES), F32),
            pltpu.VMEM((bp, S5_WIDTH), F32),
            pltpu.VMEM((bp, S5_WIDTH), F32),
        ],
        compiler_params=_cparams("arbitrary"),
        name="s5_scan",
    )(u, h0re, h0im, sp["bmat"], jnp.broadcast_to(sp["lre"], (bp, S5_WIDTH)),
      jnp.broadcast_to(sp["lim"], (bp, S5_WIDTH)), sp["cmat"], sp["d"], sp["wglu"])


def _run_interleaved(chains):
    live = list(chains)
    while live:
        still = []
        for ch in live:
            try:
                next(ch)
                still.append(ch)
            except StopIteration:
                pass
        live = still


def _mlstm_kernel(q_ref, k_ref, v_ref, og_ref, sm_ref, gt_ref, brow_ref, bcol_ref, norm_ref,
                  y_ref, c_out, n_out, m_out, c_scr, n_scr, m_scr, *, L, nc, G):
    c = pl.program_id(1)

    @pl.when(c == 0)
    def _():
        c_scr[...] = jnp.zeros_like(c_scr)
        n_scr[...] = jnp.zeros_like(n_scr)
        m_scr[...] = jnp.zeros_like(m_scr)

    ones = _head_ones()
    tril = _tril_ones(L)
    triu = _triu_ones(L)
    l_shift = L.bit_length() - 1
    src = _iota((LANES, N_HEADS * L), 0)
    to_scores = jnp.where(src == LANE_MI + (_iota((LANES, N_HEADS * L), 1) >> l_shift), 1.0, 0.0).astype(BF16)
    src = _iota((LANES, W_GROUP), 0)
    to_feats = jnp.where(src == LANE_MI + (_iota((LANES, W_GROUP), 1) >> HEAD_SHIFT), 1.0, 0.0).astype(BF16)
    seg = _iota((N_HEADS * L, W_GROUP), 0) >> l_shift
    score_sum = jnp.where(seg == (_iota((N_HEADS * L, W_GROUP), 1) >> HEAD_SHIFT), 1.0, 0.0).astype(BF16)
    causal = _iota((L, N_HEADS * L), 0) >= (_iota((L, N_HEADS * L), 1) & (L - 1))
    t_idx = _iota((L, LANES), 0)

    seqs = list(range(G))
    cat = lambda xs: jnp.concatenate(xs, axis=0)
    split = lambda x: [x[i * L:(i + 1) * L] for i in seqs]

    def running_max(x):
        shift = 1
        while shift < L:
            x = jnp.maximum(x, jnp.where(t_idx >= shift, pltpu.roll(x, shift, 0), NEG_BIG))
            shift *= 2
        return x

    q = [q_ref[i] for i in seqs]
    k = [k_ref[i] * (HEAD_DIM ** -0.5) for i in seqs]
    v = [v_ref[i] for i in seqs]
    C = [c_scr[i] for i in seqs]
    n = [n_scr[i, 0:1, :] for i in seqs]
    m0 = [m_scr[i, 0:1, :] for i in seqs]
    pre = [sm_ref[i] + brow_ref[...] for i in seqs]
    gtb = [gt_ref[i] + bcol_ref[...] for i in seqs]
    bc_col = _each(lambda x: _dot_exact_rhs(tril, _log_sigmoid(x)), pre)
    bc_rows = _dot_exact_lhs(_log_sigmoid(cat(gtb)), triu)
    qC = _each(_dot_nt, q, C)
    qk = _each(lambda q_, k_: _dot_nt(q_, _stack_heads(k_)), q, k)
    qn = _head_sum(cat(_each(lambda q_, n_: q_ * n_, q, n)), ones)

    b_col = _each(lambda x: pltpu.roll(x, LANES - N_HEADS, 1), bc_col)
    r_col = _each(lambda p_, b_: p_ - b_, pre, b_col)
    mu = _each(lambda r_, m_: jnp.maximum(running_max(r_), m_), r_col, m0)
    mu_last = _each(lambda x: x[L - 1:L, :], mu)
    mu_scores = split(_dot_exact_lhs(cat(mu), to_scores))

    def weights(i):
        r_rows = gtb[i][0:N_HEADS, :] - bc_rows[8 * i + N_HEADS:8 * i + 2 * N_HEADS, :]
        r_all = jnp.concatenate([r_rows[h:h + 1, :] for h in range(N_HEADS)], axis=1)
        return jnp.exp(jnp.where(causal, r_all - mu_scores[i], NEG_BIG)) * qk[i]

    ws = [weights(i) for i in seqs]
    wsv = _each(lambda w_, v_: _dot(w_, _stack_heads(v_)), ws, v)
    w0 = _dot(cat(_each(lambda m_, u_: jnp.exp(m_ - u_), m0, mu)), to_feats)
    m_row = _dot_exact_lhs(cat(_each(lambda b_, u_: b_ + u_, b_col, mu)), to_feats)
    den = w0 * qn + _dot(cat(ws), score_sum)
    hc = (w0 * cat(qC) + cat(wsv)) / jnp.maximum(jnp.abs(den), jnp.exp(-m_row))

    wl = split(_dot(cat(_each(lambda r_, u_: jnp.exp(r_ - u_), r_col, mu_last)), to_feats))
    w0l_rows = _each(lambda m_, u_: jnp.exp(m_ - u_), m0, mu_last)
    w0l_rows = w0l_rows + [jnp.zeros((1, LANES), F32)] * (-G % 8)
    w0l = _dot_exact_lhs(cat(w0l_rows), to_feats)
    kw = _each(lambda k_, w_: k_ * w_, k, wl)
    upd = _each(_dot_tn, v, kw)
    for i in seqs:
        c_scr[i] = w0l[i:i + 1, :] * C[i] + ones.astype(F32) * upd[i]
        n_scr[i, 0:1, :] = w0l[i:i + 1, :] * n[i] + jnp.sum(kw[i], axis=0, keepdims=True)
        m_scr[i, 0:1, :] = b_col[i][L - 1:L, :] + mu_last[i]

    ms = _head_sum(hc * hc, ones) * (1.0 / HEAD_DIM)
    y = split(hc * lax.rsqrt(ms + EPS) * norm_ref[...])
    for i in seqs:
        y_ref[i] = (y[i] * _sigmoid(og_ref[i])).astype(y_ref.dtype)

    @pl.when(c == nc - 1)
    def _():
        _store_head_blocks(c_out, c_scr)
        n_out[...] = jnp.broadcast_to(n_scr[:, 0:1, :], n_out.shape)
        m_out[...] = m_scr[...]


def _seq_spec(G, L, blk, width=W_GROUP):
    return pl.BlockSpec((G, L, width), lambda b, c: (b, c, blk))


def _mlstm(proj, gt, p, n_seq, t_len):
    L = CHUNK_ML
    G = SEQS_PER_STEP_ML
    nc = t_len // L
    const = lambda shape: pl.BlockSpec(shape, lambda b, c: (0,) * len(shape))
    per_seq = lambda rows, width: pl.BlockSpec((G, rows, width), lambda b, c: (b, 0, 0))
    proj3 = proj.reshape(n_seq, t_len, D_PROJ)
    y, c_new, n_new, m_new = pl.pallas_call(
        functools.partial(_mlstm_kernel, L=L, nc=nc, G=G),
        grid=(n_seq // G, nc),
        in_specs=[
            _seq_spec(G, L, BLK_MQ), _seq_spec(G, L, BLK_MK), _seq_spec(G, L, BLK_MV),
            _seq_spec(G, L, BLK_MO), _seq_spec(G, L, BLK_SMALL, SMALL_W),
            pl.BlockSpec((G, 8, L), lambda b, c: (b, 0, c)),
            const((1, SMALL_W)), const((8, L)), const((1, W_GROUP)),
        ],
        out_specs=[
            pl.BlockSpec((G, L, W_GROUP), lambda b, c: (b, c, 0)),
            _HEAD_BLOCKS_SPEC(G), per_seq(8, W_GROUP), per_seq(8, LANES),
        ],
        out_shape=[
            jax.ShapeDtypeStruct((n_seq, t_len, W_GROUP), BF16),
            jax.ShapeDtypeStruct((n_seq, N_HEADS, HEAD_DIM, HEAD_DIM), F32),
            jax.ShapeDtypeStruct((n_seq, 8, W_GROUP), F32),
            jax.ShapeDtypeStruct((n_seq, 8, LANES), F32),
        ],
        scratch_shapes=[
            pltpu.VMEM((G, W_GROUP, W_GROUP), F32),
            pltpu.VMEM((G, 8, W_GROUP), F32),
            pltpu.VMEM((G, 8, LANES), F32),
        ],
        compiler_params=_cparams("arbitrary", "arbitrary"),
        name="mlstm_chunk",
    )(proj3, proj3, proj3, proj3, proj3, gt, p["ml_brow"], p["ml_bcol"], p["ml_norm"])
    return y.reshape(n_seq * t_len, W_GROUP), c_new, n_new, m_new


def _gla_kernel(q_ref, k_ref, v_ref, gg_ref, sm_ref, wa_ref, ba_ref, norm_ref,
                y_ref, s_out, s_scr, *, L, nc, G):
    c = pl.program_id(1)

    @pl.when(c == 0)
    def _():
        s_scr[...] = jnp.zeros_like(s_scr)

    ones = _head_ones()
    tril = _tril_ones(L)
    causal = _iota((L, N_HEADS * L), 0) >= (_iota((L, N_HEADS * L), 1) & (L - 1))

    def chain(gi):
        q = q_ref[gi] * (HEAD_DIM ** -0.5)
        k = k_ref[gi]
        v = v_ref[gi]
        la = _log_sigmoid(_dot(sm_ref[gi], wa_ref[...]) + ba_ref[...]) * (1.0 / GLA_GATE_TEMP)
        yield
        bc = _dot_exact_rhs(tril, la)
        yield
        qs = q * jnp.exp(bc)
        kh = k * jnp.exp(jnp.minimum(-bc, EXP_CLAMP))
        ST = s_scr[gi]
        att = jnp.where(causal, _dot_nt(qs, _stack_heads(kh)), 0.0)
        from_state = _dot_nt(qs, ST)
        yield
        o = from_state + _dot(att, _stack_heads(v))
        b_last = bc[L - 1:L, :]
        kbar = k * jnp.exp(b_last - bc)
        s_scr[gi] = ST * jnp.exp(b_last) + ones.astype(F32) * _dot_tn(v, kbar)
        yield
        ms = _head_sum(o * o, ones) * (1.0 / HEAD_DIM)
        gg = gg_ref[gi]
        y = o * lax.rsqrt(ms + EPS) * norm_ref[...] * (gg * _sigmoid(gg))
        y_ref[gi] = y.astype(y_ref.dtype)

    _run_interleaved([chain(gi) for gi in range(G)])

    @pl.when(c == nc - 1)
    def _():
        _store_head_blocks(s_out, s_scr, transpose=True)


def _gla(proj, p, n_seq, t_len):
    L = CHUNK_ML
    G = SEQS_PER_STEP_ML
    nc = t_len // L
    const = lambda shape: pl.BlockSpec(shape, lambda b, c: (0,) * len(shape))
    proj3 = proj.reshape(n_seq, t_len, D_PROJ)
    y, s_new = pl.pallas_call(
        functools.partial(_gla_kernel, L=L, nc=nc, G=G),
        grid=(n_seq // G, nc),
        in_specs=[
            _seq_spec(G, L, BLK_GQ), _seq_spec(G, L, BLK_GK), _seq_spec(G, L, BLK_GV),
            _seq_spec(G, L, BLK_GG), _seq_spec(G, L, BLK_SMALL, SMALL_W),
            const((SMALL_W, W_GROUP)), const((1, W_GROUP)), const((1, W_GROUP)),
        ],
        out_specs=[
            pl.BlockSpec((G, L, W_GROUP), lambda b, c: (b, c, 0)),
            _HEAD_BLOCKS_SPEC(G),
        ],
        out_shape=[
            jax.ShapeDtypeStruct((n_seq, t_len, W_GROUP), BF16),
            jax.ShapeDtypeStruct((n_seq, N_HEADS, HEAD_DIM, HEAD_DIM), F32),
        ],
        scratch_shapes=[pltpu.VMEM((G, W_GROUP, W_GROUP), F32)],
        compiler_params=_cparams("arbitrary", "arbitrary"),
        name="gla_chunk",
    )(proj3, proj3, proj3, proj3, proj3, p["gla_wa"], p["gla_ba"], p["gla_norm"])
    return y.reshape(n_seq * t_len, W_GROUP), s_new


def _rwkv_vectors(rc, prev, p_mu, p_w0, p_a0, p_kk, p_ka, p_rk, w2, a2, g2, ones):
    xm = rc + p_mu * (prev - rc)
    rr = xm[:, 0:W_GROUP]
    rk = xm[:, W_GROUP:2 * W_GROUP]
    rv = xm[:, 2 * W_GROUP:3 * W_GROUP]
    tail = xm[:, 3 * W_GROUP:RWKV_COLS]
    lw = -RWKV_DECAY_SCALE * _sigmoid(p_w0 + _dot(jnp.tanh(tail), w2))
    a = _sigmoid(p_a0 + _dot(tail, a2))
    g = _dot(_sigmoid(tail), g2)
    kk = rk * p_kk
    kk = kk * lax.rsqrt(jnp.maximum(_head_sum(kk * kk, ones), 1e-24))
    kt = rk * (1.0 + (a - 1.0) * p_ka)
    bonus = _head_sum(rr * kt * p_rk, ones) * rv
    return rr, lw, kt, rv, kk, a, g, bonus


def _head_layernorm(o, g, ones):
    mu = _head_sum(o, ones) * (1.0 / HEAD_DIM)
    oc = o - mu
    var = _head_sum(oc * oc, ones) * (1.0 / HEAD_DIM)
    return oc * lax.rsqrt(var + GN_EPS) * g


def _stack_heads(x):
    xb = x.astype(BF16)
    lane_head = _iota((1, W_GROUP), 1) >> HEAD_SHIFT
    return jnp.concatenate([jnp.where(lane_head == h, xb, jnp.zeros_like(xb)) for h in range(N_HEADS)],
                           axis=0)


def _each(f, *seqs):
    return [f(*args) for args in zip(*seqs)]


def _block_mm(x_list, y_list):
    return _each(lambda x, y: _dot(x, _stack_heads(y)), x_list, y_list)


def _unit_lower_inverse(a_list, L):
    t_idx = _iota((L, N_HEADS * L), 0)
    s_idx = _iota((L, N_HEADS * L), 1) & (L - 1)
    eye = jnp.where(t_idx == s_idx, 1.0, 0.0).astype(F32)
    in16 = (t_idx >> 4) == (s_idx >> 4)
    in32 = ((t_idx >> 5) == (s_idx >> 5)) & ((t_idx >> 4) > (s_idx >> 4))
    in64 = (t_idx >> 5) > (s_idx >> 5)
    x16 = _each(lambda a: jnp.where(in16, -a, 0.0), a_list)
    x2 = _block_mm(x16, x16)
    x4 = _block_mm(x2, x2)
    x8 = _block_mm(x4, x4)
    t = _each(lambda x: eye + x, x16)
    for xp in (x2, x4, x8):
        t = _each(lambda t_, d: t_ + d, t, _block_mm(t, xp))
    for sel in (in32, in64):
        w = _block_mm(_each(lambda a: jnp.where(sel, a, 0.0), a_list), t)
        t = _each(lambda t_, d: t_ - d, t, _block_mm(t, w))
    return t


def _rwkv_chunks(rcs, prev_rows, states, prm, ones, L):
    (mu, w0, a0, pkk, pka, prk, w2, a2, g2, norm) = prm
    first_row = _iota((L, RWKV_COLS), 0) == 0
    prevs = _each(lambda rc, pr: jnp.where(first_row, pr, pltpu.roll(rc, 1, 0)), rcs, prev_rows)
    n_seq = len(rcs)
    stacked = _rwkv_vectors(jnp.concatenate(rcs, axis=0), jnp.concatenate(prevs, axis=0),
                            mu, w0, a0, pkk, pka, prk, w2, a2, g2, ones)
    r, lw, kx, v, kk, a, g, bonus = [[x[i * L:(i + 1) * L] for i in range(n_seq)] for x in stacked]

    tril = _tril_ones(L)
    lc = _each(lambda x: _dot_exact_rhs(tril, x), lw)
    akk = _each(lambda a_, k_: a_ * k_, a, kk)
    e_neg = _each(lambda x: jnp.exp(-x), lc)
    l_last = _each(lambda x: x[L - 1:L, :], lc)
    e_end = _each(lambda ll, x: jnp.exp(ll - x), l_last, lc)
    kr = _each(lambda k_, r_, c_, w_: jnp.concatenate([k_ * jnp.exp(c_ - w_), r_ * jnp.exp(c_)], axis=0),
               kk, r, lc, lw)
    a_hat = _each(lambda x, e: _stack_heads(x * e), akk, e_neg)
    k_hat = _each(lambda x, e: _stack_heads(x * e), kx, e_neg)
    v_bd = _each(_stack_heads, v)

    t_idx = _iota((L, N_HEADS * L), 0)
    s_idx = _iota((L, N_HEADS * L), 1) & (L - 1)
    strict = t_idx > s_idx
    incl = t_idx >= s_idx
    pa = _each(_dot_nt, kr, a_hat)
    pk = _each(_dot_nt, kr, k_hat)
    a_ua = _each(lambda p_: jnp.where(strict, p_[0:L], 0.0), pa)
    b_ra = _each(lambda p_: jnp.where(incl, p_[L:2 * L], 0.0), pa)
    ab_k = _each(lambda p_: jnp.concatenate([jnp.where(strict, p_[0:L], 0.0),
                                             jnp.where(incl, p_[L:2 * L], 0.0)], axis=0), pk)
    T = _unit_lower_inverse(a_ua, L)

    from_state = _each(_dot_nt, kr, states)
    from_v = _each(_dot, ab_k, v_bd)
    rhs = _each(lambda s_, v_: s_[0:L] + v_[0:L], from_state, from_v)
    U = _block_mm(T, rhs)
    corr = _block_mm(b_ra, U)
    o = _each(lambda s_, v_, c_: s_[L:2 * L] + v_[L:2 * L] - c_, from_state, from_v, corr)
    upd = _each(lambda v_, u_, k_, a_, e: _dot_tn(jnp.concatenate([v_, u_], axis=0),
                                                  jnp.concatenate([k_ * e, -(a_ * e)], axis=0)),
                v, U, kx, akk, e_end)
    s_new = _each(lambda s_, ll, d: s_ * jnp.exp(ll) + ones.astype(F32) * d, states, l_last, upd)
    y_all = ((_head_layernorm(jnp.concatenate(o, axis=0), norm, ones) + stacked[7]) * stacked[6])
    return [y_all[i * L:(i + 1) * L] for i in range(n_seq)], s_new


def _rwkv_kernel(rc_ref, mu_ref, w0_ref, a0_ref, kk_ref, ka_ref, rk_ref, w2_ref, a2_ref, g2_ref,
                 norm_ref, y_ref, s_out, shift_out, s_scr, prev_scr, *, L, nc, G):
    c = pl.program_id(1)

    @pl.when(c == 0)
    def _():
        s_scr[...] = jnp.zeros_like(s_scr)
        prev_scr[...] = jnp.zeros_like(prev_scr)

    ones = _head_ones()
    prm = (mu_ref[...], w0_ref[...], a0_ref[...], kk_ref[...], ka_ref[...], rk_ref[...],
           w2_ref[...], a2_ref[...], g2_ref[...], norm_ref[...])
    rcs = [rc_ref[gi] for gi in range(G)]
    ys, s_new = _rwkv_chunks(rcs, [prev_scr[gi, 0:1, :] for gi in range(G)],
                             [s_scr[gi] for gi in range(G)], prm, ones, L)
    for gi in range(G):
        prev_scr[gi, 0:1, :] = rcs[gi][L - 1:L, :]
        s_scr[gi] = s_new[gi]
        y_ref[gi] = ys[gi].astype(y_ref.dtype)

    @pl.when(c == nc - 1)
    def _():
        _store_head_blocks(s_out, s_scr)
        shift_out[...] = jnp.broadcast_to(prev_scr[:, 0:1, :], shift_out.shape)


def _rwkv(proj, p, n_seq, t_len):
    L = CHUNK_RW
    G = SEQS_PER_STEP
    nc = t_len // L
    const = lambda shape: pl.BlockSpec(shape, lambda b, c: (0,) * len(shape))
    row = const((1, W_GROUP))
    pad = const((LANES, W_GROUP))
    y, s_new, shift = pl.pallas_call(
        functools.partial(_rwkv_kernel, L=L, nc=nc, G=G),
        grid=(n_seq // G, nc),
        in_specs=[
            pl.BlockSpec((G, L, RWKV_COLS), lambda b, c: (b, c, 0)),
            const((1, RWKV_COLS)), row, row, row, row, row, pad, pad, pad, row,
        ],
        out_specs=[
            pl.BlockSpec((G, L, W_GROUP), lambda b, c: (b, c, 0)),
            _HEAD_BLOCKS_SPEC(G),
            pl.BlockSpec((G, 8, RWKV_COLS), lambda b, c: (b, 0, 0)),
        ],
        out_shape=[
            jax.ShapeDtypeStruct((n_seq, t_len, W_GROUP), BF16),
            jax.ShapeDtypeStruct((n_seq, N_HEADS, HEAD_DIM, HEAD_DIM), F32),
            jax.ShapeDtypeStruct((n_seq, 8, RWKV_COLS), F32),
        ],
        scratch_shapes=[
            pltpu.VMEM((G, W_GROUP, W_GROUP), F32),
            pltpu.VMEM((G, 8, RWKV_COLS), F32),
        ],
        compiler_params=_cparams("arbitrary", "arbitrary"),
        name="rwkv_chunk",
    )(proj.reshape(n_seq, t_len, D_PROJ), p["rw_mu"], p["rw_w0"], p["rw_a0"], p["rw_kk"], p["rw_ka"],
      p["rw_rk"], p["rw_w2"], p["rw_a2"], p["rw_g2"], p["rw_norm"])
    return y.reshape(n_seq * t_len, W_GROUP), s_new, shift


FF_CHUNK = 256


def _mix_residual(x_ref, ys, wout_ref):
    acc = x_ref[...]
    for j, y in enumerate(ys):
        acc = acc + jnp.dot(y.astype(BF16), wout_ref[j * W_GROUP:(j + 1) * W_GROUP, :],
                            preferred_element_type=F32)
    return acc


def _ffn_body(x1, nrm_ref, wup_ref, cw_ref, cb_ref, wdn_ref, prev_rows, d_ff, act_scr):
    h2 = _rmsnorm_rows(x1, nrm_ref[...]).astype(BF16)
    for j in range(d_ff // FF_CHUNK):
        lo, hi = j * FF_CHUNK, (j + 1) * FF_CHUNK
        ug = jnp.dot(h2, wup_ref[:, lo:hi], preferred_element_type=F32)
        uv = jnp.dot(h2, wup_ref[:, d_ff + lo:d_ff + hi], preferred_element_type=F32)
        p2, p1 = prev_rows(j, ug)
        conv = (cb_ref[:, lo:hi] + cw_ref[0:1, lo:hi] * p2 + cw_ref[1:2, lo:hi] * p1
                + cw_ref[2:3, lo:hi] * ug)
        act_scr[:, lo:hi] = (_gelu(conv) * uv).astype(BF16)
    return x1 + jnp.dot(act_scr[...], wdn_ref[...], preferred_element_type=F32)


def _ffn_seq_kernel(x_ref, y0_ref, y1_ref, y2_ref, y3_ref, wout_ref, nrm_ref, wup_ref, cw_ref, cb_ref,
                    wdn_ref, nf_ref, o_ref, buf_out, carry_scr, act_scr, *, tm, d_ff, final):
    t = pl.program_id(1)

    @pl.when(t == 0)
    def _():
        carry_scr[...] = jnp.zeros_like(carry_scr)

    x1 = _mix_residual(x_ref, (y0_ref[...], y1_ref[...], y2_ref[...], y3_ref[...]), wout_ref)
    row = _iota((tm, FF_CHUNK), 0)

    def prev_rows(j, ug):
        lo, hi = j * FF_CHUNK, (j + 1) * FF_CHUNK
        c0 = carry_scr[0:1, lo:hi]
        c1 = carry_scr[1:2, lo:hi]
        p1 = jnp.where(row == 0, c1, pltpu.roll(ug, 1, 0))
        p2 = jnp.where(row == 0, c0, jnp.where(row == 1, c1, pltpu.roll(ug, 2, 0)))
        carry_scr[0:2, lo:hi] = ug[tm - 2:tm, :]
        return p2, p1

    out = _ffn_body(x1, nrm_ref, wup_ref, cw_ref, cb_ref, wdn_ref, prev_rows, d_ff, act_scr)
    if final:
        out = _rmsnorm_rows(out, nf_ref[...])
    o_ref[...] = out
    buf_out[0] = carry_scr[...]


def _ffn_step_kernel(x_ref, y0_ref, y1_ref, y2_ref, y3_ref, wout_ref, nrm_ref, wup_ref, cw_ref, cb_ref,
                     wdn_ref, nf_ref, p2_ref, p1_ref, o_ref, ug_out, act_scr, *, d_ff, final):
    x1 = _mix_residual(x_ref, (y0_ref[...], y1_ref[...], y2_ref[...], y3_ref[...]), wout_ref)

    def prev_rows(j, ug):
        lo, hi = j * FF_CHUNK, (j + 1) * FF_CHUNK
        ug_out[:, lo:hi] = ug
        return p2_ref[:, lo:hi], p1_ref[:, lo:hi]

    out = _ffn_body(x1, nrm_ref, wup_ref, cw_ref, cb_ref, wdn_ref, prev_rows, d_ff, act_scr)
    if final:
        out = _rmsnorm_rows(out, nf_ref[...])
    o_ref[...] = out


def _ffn_seq(x2d, ys, p, norm_final, n_seq, t_len, tm, final):
    n, d = x2d.shape
    d_ff = p["ffn_wdn"].shape[1]
    lay = p["layer"]
    nt = t_len // tm
    rows = lambda w: pl.BlockSpec((tm, w), lambda b, t: (b * nt + t, 0))
    const = lambda shape: pl.BlockSpec(shape, lambda b, t: (0,) * len(shape))
    layered = lambda shape: pl.BlockSpec((None,) + shape, lambda b, t: (lay, 0, 0),
                                         pipeline_mode=pl.Buffered(1))
    return pl.pallas_call(
        functools.partial(_ffn_seq_kernel, tm=tm, d_ff=d_ff, final=final),
        grid=(n_seq, nt),
        in_specs=[rows(d)] + [rows(W_GROUP)] * 4 + [
            layered((d, d)), const((1, d)), layered((d, 2 * d_ff)), const((CONV_W, d_ff)),
            const((1, d_ff)), layered((d_ff, d)), const((1, d))],
        out_specs=[rows(d), pl.BlockSpec((1, CONV_W - 1, d_ff), lambda b, t: (b, 0, 0))],
        out_shape=[jax.ShapeDtypeStruct((n, d), F32),
                   jax.ShapeDtypeStruct((n_seq, CONV_W - 1, d_ff), F32)],
        scratch_shapes=[pltpu.VMEM((CONV_W - 1, d_ff), F32), pltpu.VMEM((tm, d_ff), BF16)],
        compiler_params=_cparams("arbitrary", "arbitrary"),
        name="wout_ffn_seq",
    )(x2d, *ys, p["w_out"], p["norm_ffn"], p["ffn_wup"], p["ffn_cw"], p["ffn_cb"], p["ffn_wdn"],
      norm_final)


def _ffn_step(x2d, ys, p, norm_final, prev2, prev1, final):
    n, d = x2d.shape
    d_ff = p["ffn_wdn"].shape[1]
    lay = p["layer"]
    full = lambda shape: pl.BlockSpec(shape, lambda i: (0,) * len(shape))
    layered = lambda shape: pl.BlockSpec((None,) + shape, lambda i: (lay, 0, 0))
    return pl.pallas_call(
        functools.partial(_ffn_step_kernel, d_ff=d_ff, final=final),
        grid=(1,),
        in_specs=[full((n, d))] + [full((n, W_GROUP))] * 4 + [
            layered((d, d)), full((1, d)), layered((d, 2 * d_ff)), full((CONV_W, d_ff)),
            full((1, d_ff)), layered((d_ff, d)), full((1, d)), full((n, d_ff)), full((n, d_ff))],
        out_specs=[full((n, d)), full((n, d_ff))],
        out_shape=[jax.ShapeDtypeStruct((n, d), F32), jax.ShapeDtypeStruct((n, d_ff), F32)],
        scratch_shapes=[pltpu.VMEM((n, d_ff), BF16)],
        compiler_params=_cparams("arbitrary"),
        name="wout_ffn_step",
    )(x2d, *ys, p["w_out"], p["norm_ffn"], p["ffn_wup"], p["ffn_cw"], p["ffn_cb"], p["ffn_wdn"],
      norm_final, prev2, prev1)


def _step_vec_kernel(proj_ref, shift_ref, m_ref, brow_ref, wa_ref, ba_ref, mu_ref, w0_ref, a0_ref,
                     kk_ref, ka_ref, rk_ref, w2_ref, a2_ref, g2_ref,
                     pack_out, m_out, g_out, bonus_out):
    ones = _head_ones()
    sm = proj_ref[:, OFF_SMALL:OFF_SMALL + SMALL_W]
    pre = sm + brow_ref[...]
    logf = pltpu.roll(_log_sigmoid(pre), LANES - N_HEADS, 1)
    m_old = m_ref[...]
    m_new = jnp.maximum(logf + m_old, pre)
    m_out[...] = m_new
    lane = _iota((SMALL_W, W_GROUP), 0)
    spread_mat = jnp.where(lane == LANE_MI + (_iota((SMALL_W, W_GROUP), 1) >> HEAD_SHIFT), 1.0, 0.0).astype(BF16)
    gate_lanes = (_iota(pre.shape, 1) >= LANE_MI) & (_iota(pre.shape, 1) < LANE_MI + N_HEADS)
    spread = lambda x: _dot_exact_lhs(jnp.where(gate_lanes, x, 0.0), spread_mat)
    la = _log_sigmoid(_dot(sm, wa_ref[...]) + ba_ref[...]) * (1.0 / GLA_GATE_TEMP)
    rc = proj_ref[:, 0:RWKV_COLS]
    rr, lw, kt, rv, kk, a, g, bonus = _rwkv_vectors(
        rc, shift_ref[...], mu_ref[...], w0_ref[...], a0_ref[...], kk_ref[...], ka_ref[...],
        rk_ref[...], w2_ref[...], a2_ref[...], g2_ref[...], ones)
    main = lambda blk: proj_ref[:, blk * W_GROUP:(blk + 1) * W_GROUP]
    rows = {
        ROW_MQ: main(BLK_MQ), ROW_MK: main(BLK_MK), ROW_MV: main(BLK_MV),
        ROW_GQ: main(BLK_GQ), ROW_GK: main(BLK_GK), ROW_GV: main(BLK_GV), ROW_GA: jnp.exp(la),
        ROW_KK: kk, ROW_AKK: a * kk, ROW_W: jnp.exp(lw), ROW_KT: kt, ROW_RR: rr, ROW_RV: rv,
        ROW_IW: spread(jnp.exp(pre - m_new)), ROW_FW: spread(jnp.exp(logf + m_old - m_new)),
        ROW_M: spread(m_new),
    }
    for i, val in rows.items():
        pack_out[i] = val.T
    g_out[...] = g
    bonus_out[...] = bonus


def _step_vectors(proj, shift, m_pad, p):
    n = proj.shape[0]
    full = lambda a: pl.BlockSpec(a.shape, lambda i: (0,) * a.ndim)
    args = (proj, shift, m_pad, p["ml_brow"], p["gla_wa"], p["gla_ba"], p["rw_mu"], p["rw_w0"],
            p["rw_a0"], p["rw_kk"], p["rw_ka"], p["rw_rk"], p["rw_w2"], p["rw_a2"], p["rw_g2"])
    wide = jax.ShapeDtypeStruct((n, W_GROUP), F32)
    outs = [jax.ShapeDtypeStruct((N_PACK, W_GROUP, n), F32), jax.ShapeDtypeStruct((n, SMALL_W), F32),
            wide, wide]
    return pl.pallas_call(
        _step_vec_kernel,
        grid=(1,),
        in_specs=[full(a) for a in args],
        out_specs=[pl.BlockSpec(o.shape, lambda i, nd=len(o.shape): (0,) * nd) for o in outs],
        out_shape=outs,
        compiler_params=_cparams("arbitrary"),
        name="step_vectors",
    )(*args)


(ROW_MQ, ROW_MK, ROW_MV, ROW_GQ, ROW_GK, ROW_GV, ROW_GA, ROW_KK, ROW_AKK, ROW_W, ROW_KT, ROW_RR, ROW_RV,
 ROW_IW, ROW_FW, ROW_M) = range(16)
N_PACK = 16
(OROW_ML, OROW_GLA, OROW_RW, OROW_N) = range(4)
N_OPACK = 4
STEP_UNROLL = 4


def _heads_step_kernel(pack_ref, c_ref, n_ref, gs_ref, rs_ref, c_out, gs_out, rs_out, opack_out):
    vec = lambda i: pack_ref[i]
    one = lambda i, j: pack_ref[i, pl.ds(j, 1), :]
    colsum = lambda x: jnp.sum(x, axis=0, keepdims=True)

    k_ml = vec(ROW_MK) * (HEAD_DIM ** -0.5)
    q_ml = vec(ROW_MQ)
    iw = pack_ref[ROW_IW, 0:1, :]
    fw = pack_ref[ROW_FW, 0:1, :]
    n_new = fw * n_ref[...] + iw * k_ml
    opack_out[OROW_N] = n_new
    r_den = 1.0 / jnp.maximum(jnp.abs(colsum(n_new * q_ml)), jnp.exp(-pack_ref[ROW_M, 0:1, :]))

    def ml_body(v, carry):
        c_new = fw * c_ref[v] + (iw * one(ROW_MV, v)) * k_ml
        c_out[v] = c_new
        opack_out[OROW_ML, pl.ds(v, 1), :] = colsum(c_new * q_ml) * r_den
        return carry

    lax.fori_loop(0, HEAD_DIM, ml_body, 0, unroll=STEP_UNROLL)

    v_gla = vec(ROW_GV)

    def gla_body(k, acc):
        s_new = one(ROW_GA, k) * gs_ref[k] + one(ROW_GK, k) * v_gla
        gs_out[k] = s_new
        return acc + (one(ROW_GQ, k) * (HEAD_DIM ** -0.5)) * s_new

    opack_out[OROW_GLA] = lax.fori_loop(0, HEAD_DIM, gla_body, jnp.zeros_like(v_gla), unroll=STEP_UNROLL)

    kk, akk, w, kt, rr = vec(ROW_KK), vec(ROW_AKK), vec(ROW_W), vec(ROW_KT), vec(ROW_RR)

    def rw_body(v, carry):
        s = rs_ref[v]
        s_new = s * w - colsum(s * kk) * akk + one(ROW_RV, v) * kt
        rs_out[v] = s_new
        opack_out[OROW_RW, pl.ds(v, 1), :] = colsum(s_new * rr)
        return carry

    lax.fori_loop(0, HEAD_DIM, rw_body, 0, unroll=STEP_UNROLL)


def _heads_step(pack, ml_c, ml_n, gla_s, rw_s, layer):
    n = pack.shape[-1]
    sq = (HEAD_DIM, HEAD_DIM, n)
    head = lambda lead: pl.BlockSpec((lead, HEAD_DIM, n), lambda h: (0, h, 0))
    at_layer = lambda shape: pl.BlockSpec((None,) + shape, lambda h: (layer * N_HEADS + h,) + (0,) * len(shape))
    out_state = pl.BlockSpec((None,) + sq, lambda h: (h, 0, 0, 0))
    state = jax.ShapeDtypeStruct((N_HEADS,) + sq, F32)
    return pl.pallas_call(
        _heads_step_kernel,
        grid=(N_HEADS,),
        in_specs=[head(N_PACK), at_layer(sq), at_layer((HEAD_DIM, n)), at_layer(sq), at_layer(sq)],
        out_specs=[out_state, out_state, out_state, head(N_OPACK)],
        out_shape=[state, state, state, jax.ShapeDtypeStruct((N_OPACK, W_GROUP, n), F32)],
        compiler_params=_cparams("arbitrary"),
        name="heads_step",
    )(pack, ml_c, ml_n, gla_s, rw_s)


def _step_post_kernel(hml_ref, mo_ref, ogla_ref, gg_ref, orw_ref, g_ref, bonus_ref,
                      mln_ref, gln_ref, rwn_ref, yml_out, ygla_out, yrw_out):
    ones = _head_ones()
    hml = hml_ref[...].T
    ms = _head_sum(hml * hml, ones) * (1.0 / HEAD_DIM)
    yml_out[...] = (hml * lax.rsqrt(ms + EPS) * mln_ref[...] * _sigmoid(mo_ref[...])).astype(BF16)
    og = ogla_ref[...].T
    ms = _head_sum(og * og, ones) * (1.0 / HEAD_DIM)
    gg = gg_ref[...]
    ygla_out[...] = (og * lax.rsqrt(ms + EPS) * gln_ref[...] * (gg * _sigmoid(gg))).astype(BF16)
    yrw_out[...] = ((_head_layernorm(orw_ref[...].T, rwn_ref[...], ones) + bonus_ref[...])
                    * g_ref[...]).astype(BF16)


def _step_post(hml, mo, ogla, gg, orw, g, bonus, p):
    n = mo.shape[0]
    args = (hml, mo, ogla, gg, orw, g, bonus, p["ml_norm"], p["gla_norm"], p["rw_norm"])
    full = lambda a: pl.BlockSpec(a.shape, lambda i: (0,) * a.ndim)
    out = jax.ShapeDtypeStruct((n, W_GROUP), BF16)
    return pl.pallas_call(
        _step_post_kernel,
        grid=(1,),
        in_specs=[full(a) for a in args],
        out_specs=[pl.BlockSpec((n, W_GROUP), lambda i: (0, 0))] * 3,
        out_shape=[out] * 3,
        compiler_params=_cparams("arbitrary"),
        name="step_post",
    )(*args)


def _layer_params(l, prm):
    p = {}
    w_in = prm["w_in"][l]
    d = w_in.shape[0]
    sizes = (W_GROUP, W_GROUP, W_GROUP, W_GROUP, N_HEADS, N_HEADS, W_GROUP, W_GROUP, W_GROUP, W_GROUP,
             GLA_RANK, W_GROUP, RWKV_COLS)
    offs = [0]
    for s in sizes:
        offs.append(offs[-1] + s)
    col = lambda i: w_in[:, offs[i]:offs[i + 1]]
    (u, mq, mk, mv, mi, mf, mo, gq, gk, gv, ga, gg, rcols) = [col(i) for i in range(13)]
    small = jnp.concatenate([ga, mi, mf, jnp.zeros((d, SMALL_W - GLA_RANK - 2 * N_HEADS), F32)], axis=1)
    p["w_in"] = jnp.concatenate([rcols, small, u, mq, mk, mv, mo, gq, gk, gv, gg], axis=1).astype(BF16)
    p["w_gate_t"] = jnp.concatenate([mi, mf], axis=1).T.astype(BF16)
    p["norm_mix"] = prm["norm_mix"][l][None, :]

    lam = lax.complex(prm["s5_lam_re"][l], prm["s5_lam_im"][l])
    dt = jnp.exp(prm["s5_log_dt"][l])[:, None]
    lam_bar = jnp.exp(lam * dt)
    b_bar = ((lam_bar - 1.0) / lam)[..., None] * lax.complex(prm["s5_b_re"][l], prm["s5_b_im"][l])
    eye = jnp.eye(S5_GROUPS, dtype=F32)
    bm = lambda b: jnp.einsum("gph,gk->ghkp", b, eye).reshape(W_GROUP, S5_WIDTH)
    p["bmat"] = jnp.concatenate([bm(b_bar.real), bm(b_bar.imag)], axis=1).astype(BF16)
    cm = lambda c: jnp.einsum("ghp,gk->kpgh", c, eye).reshape(S5_WIDTH, W_GROUP)
    p["cmat"] = jnp.concatenate([cm(prm["s5_c_re"][l]), -cm(prm["s5_c_im"][l])], axis=0).astype(BF16)
    p["lre"] = lam_bar.real.reshape(1, S5_WIDTH)
    p["lim"] = lam_bar.imag.reshape(1, S5_WIDTH)
    p["d"] = prm["s5_d"][l][None, :]
    p["wglu"] = prm["s5_w_glu"][l].astype(BF16)

    bias = prm["ml_gate_bias"][l]
    p["ml_brow"] = jnp.zeros((1, SMALL_W), F32).at[0, LANE_MI:LANE_MI + 2 * N_HEADS].set(bias)
    p["ml_bcol"] = jnp.broadcast_to(bias[:, None], (2 * N_HEADS, CHUNK_ML))
    p["ml_norm"] = prm["ml_norm"][l][None, :]

    p["gla_wa"] = jnp.zeros((SMALL_W, W_GROUP), F32).at[0:GLA_RANK].set(prm["gla_w_alpha"][l]).astype(BF16)
    p["gla_ba"] = prm["gla_b_alpha"][l][None, :]
    p["gla_norm"] = prm["gla_norm"][l][None, :]

    p["rw_mu"] = prm["rw_mu"][l][None, :]
    p["rw_w0"] = prm["rw_w0"][l][None, :]
    p["rw_a0"] = prm["rw_a0"][l][None, :]
    p["rw_kk"] = prm["rw_k_k"][l][None, :]
    p["rw_ka"] = prm["rw_k_a"][l][None, :]
    p["rw_rk"] = prm["rw_r_k"][l].reshape(1, W_GROUP)
    z = jnp.zeros((LANES, W_GROUP), F32)
    p["rw_w2"] = z.at[0:RWKV_W_RANK].set(prm["rw_w2"][l]).astype(BF16)
    p["rw_a2"] = z.at[RWKV_W_RANK:RWKV_W_RANK + RWKV_A_RANK].set(prm["rw_a2"][l]).astype(BF16)
    p["rw_g2"] = z.at[RWKV_W_RANK + RWKV_A_RANK:].set(prm["rw_g2"][l]).astype(BF16)
    p["rw_norm"] = prm["rw_norm"][l][None, :]

    p["layer"] = l
    p["w_out"] = prm["w_out"].astype(BF16)
    p["norm_ffn"] = prm["norm_ffn"][l][None, :]
    p["ffn_wup"] = prm["ffn_w_up"].astype(BF16)
    p["ffn_cw"] = prm["ffn_conv_w"][l]
    p["ffn_cb"] = prm["ffn_conv_b"][l][None, :]
    p["ffn_wdn"] = prm["ffn_w_down"].astype(BF16)
    return p


def _prompt_layer(x2d, p, norm_final, n_seq, t_len, final):
    proj, gt = _inproj(x2d, p["norm_mix"], p["w_in"], p["w_gate_t"], n_seq, t_len, tm=512)

    u_tb = proj[:, OFF_MAIN:OFF_MAIN + W_GROUP].reshape(n_seq, t_len, W_GROUP).transpose(1, 0, 2)
    zeros = jnp.zeros((n_seq, S5_WIDTH), F32)
    y_s5_tb, s5_re, s5_im = _s5(u_tb.reshape(t_len * n_seq, W_GROUP), zeros, zeros, p, n_seq, t_len, tt=128,
                                seq_major=False)
    y_s5 = y_s5_tb.reshape(t_len, n_seq, W_GROUP).transpose(1, 0, 2).reshape(n_seq * t_len, W_GROUP)

    y_ml, ml_c, ml_n, ml_m = _mlstm(proj, gt, p, n_seq, t_len)
    y_gla, gla_st = _gla(proj, p, n_seq, t_len)
    y_rw, rw_s, rw_shift = _rwkv(proj, p, n_seq, t_len)

    x_new, ffn_buf = _ffn_seq(x2d, (y_s5, y_ml, y_gla, y_rw), p, norm_final, n_seq, t_len, tm=512,
                              final=final)
    states = (
        s5_re.reshape(n_seq, S5_GROUPS, S5_STATE),
        s5_im.reshape(n_seq, S5_GROUPS, S5_STATE),
        ml_c,
        ml_n[:, 0, :].reshape(n_seq, N_HEADS, HEAD_DIM),
        ml_m[:, 0, LANE_MI:LANE_MI + N_HEADS],
        gla_st,
        rw_s,
        rw_shift[:, 0, :],
        ffn_buf,
    )
    return x_new, states


def _sample_layer(x2d, st, pair_states, layer, p, norm_final, final):
    (s5_re, s5_im, ml_m, rw_shift, ffn_buf) = st
    n = x2d.shape[0]
    npairs = n * N_HEADS
    proj, _ = _inproj(x2d, p["norm_mix"], p["w_in"], p["w_gate_t"], 1, n, tm=n)

    y_s5, s5_re_new, s5_im_new = _s5(proj[:, OFF_MAIN:OFF_MAIN + W_GROUP],
                                     s5_re.reshape(n, S5_WIDTH), s5_im.reshape(n, S5_WIDTH),
                                     p, n, 1, tt=1, seq_major=False)

    m_pad = jnp.zeros((n, SMALL_W), F32).at[:, LANE_MI:LANE_MI + N_HEADS].set(ml_m)
    pack, m_new, g, bonus = _step_vectors(proj, rw_shift, m_pad, p)
    ml_c_new, gla_s_new, rw_s_new, opack = _heads_step(pack, *pair_states, layer)

    main = lambda blk: proj[:, blk * W_GROUP:(blk + 1) * W_GROUP]
    y_ml, y_gla, y_rw = _step_post(opack[OROW_ML], main(BLK_MO), opack[OROW_GLA], main(BLK_GG),
                                   opack[OROW_RW], g, bonus, p)
    x_new, ug = _ffn_step(x2d, (y_s5, y_ml, y_gla, y_rw), p, norm_final, ffn_buf[:, 0, :], ffn_buf[:, 1, :],
                          final)
    states = (
        s5_re_new.reshape(n, S5_GROUPS, S5_STATE),
        s5_im_new.reshape(n, S5_GROUPS, S5_STATE),
        ml_c_new.transpose(3, 0, 1, 2),
        opack[OROW_N].reshape(N_HEADS, HEAD_DIM, n).transpose(2, 0, 1),
        m_new[:, LANE_MI:LANE_MI + N_HEADS],
        gla_s_new.transpose(3, 0, 1, 2),
        rw_s_new.transpose(3, 0, 1, 2),
        proj[:, 0:RWKV_COLS],
        jnp.stack([ffn_buf[:, 1, :], ug], axis=1),
    )
    return x_new, states


def kernel(x_prompt, x_sample, state_s5_re, state_s5_im, state_mlstm_C, state_mlstm_n, state_mlstm_m, state_gla_S, state_rwkv_S, state_rwkv_shift, state_ffn_conv, norm_mix, w_in, s5_lam_re, s5_lam_im, s5_log_dt, s5_b_re, s5_b_im, s5_c_re, s5_c_im, s5_d, s5_w_glu, ml_gate_bias, ml_norm, gla_w_alpha, gla_b_alpha, gla_norm, rw_mu, rw_w0, rw_w2, rw_a0, rw_a2, rw_g2, rw_k_k, rw_k_a, rw_r_k, rw_norm, w_out, norm_ffn, ffn_w_up, ffn_conv_w, ffn_conv_b, ffn_w_down, norm_final):
    prm = dict(norm_mix=norm_mix, w_in=w_in, s5_lam_re=s5_lam_re, s5_lam_im=s5_lam_im, s5_log_dt=s5_log_dt,
               s5_b_re=s5_b_re, s5_b_im=s5_b_im, s5_c_re=s5_c_re, s5_c_im=s5_c_im, s5_d=s5_d,
               s5_w_glu=s5_w_glu, ml_gate_bias=ml_gate_bias, ml_norm=ml_norm, gla_w_alpha=gla_w_alpha,
               gla_b_alpha=gla_b_alpha, gla_norm=gla_norm, rw_mu=rw_mu, rw_w0=rw_w0, rw_w2=rw_w2,
               rw_a0=rw_a0, rw_a2=rw_a2, rw_g2=rw_g2, rw_k_k=rw_k_k, rw_k_a=rw_k_a, rw_r_k=rw_r_k,
               rw_norm=rw_norm, w_out=w_out, norm_ffn=norm_ffn, ffn_w_up=ffn_w_up, ffn_conv_w=ffn_conv_w,
               ffn_conv_b=ffn_conv_b, ffn_w_down=ffn_w_down)
    depth = w_in.shape[0]
    n_seq, t_len, d = x_prompt.shape
    n_smp = x_sample.shape[0]
    assert t_len % 512 == 0 and x_sample.shape[1] == 1
    nf = norm_final[None, :]
    small_states = (state_s5_re, state_s5_im, state_mlstm_m, state_rwkv_shift, state_ffn_conv)
    lead = depth * N_HEADS
    to_lanes = lambda st: jnp.moveaxis(st, 1, -1)
    pair_states = (to_lanes(state_mlstm_C).reshape(lead, HEAD_DIM, HEAD_DIM, n_smp),
                   to_lanes(state_mlstm_n).reshape(lead, HEAD_DIM, n_smp),
                   to_lanes(state_gla_S).reshape(lead, HEAD_DIM, HEAD_DIM, n_smp),
                   to_lanes(state_rwkv_S).reshape(lead, HEAD_DIM, HEAD_DIM, n_smp))

    xp = x_prompt.reshape(n_seq * t_len, d)
    xs = x_sample.reshape(n_smp, d)
    p_states, s_states = [], []
    for l in range(depth):
        p = _layer_params(l, prm)
        final = l == depth - 1
        xp, ps = _prompt_layer(xp, p, nf, n_seq, t_len, final)
        xs, ss = _sample_layer(xs, tuple(s[l] for s in small_states), pair_states, l, p, nf, final)
        p_states.append(ps)
        s_states.append(ss)
    new_p = tuple(jnp.stack([st[i] for st in p_states]) for i in range(9))
    new_s = tuple(jnp.stack([st[i] for st in s_states]) for i in range(9))
    return (xp.reshape(n_seq, t_len, d), xs.reshape(n_smp, 1, d)) + new_p + new_s
```

```python
import functools
import math

import jax
import jax.numpy as jnp
from jax import lax
from jax.experimental import pallas as pl
from jax.experimental.pallas import tpu as pltpu

F32 = jnp.float32
BF16 = jnp.bfloat16

LANES = 128
W_GROUP = 256
HEAD_DIM = 64
HEAD_SHIFT = 6
N_HEADS = 4
S5_CH = 16
S5_GROUPS = 16
S5_STATE = 64
S5_WIDTH = S5_GROUPS * S5_STATE
GLA_RANK = 16
GLA_GATE_TEMP = 16.0
RWKV_W_RANK = 32
RWKV_A_RANK = 32
RWKV_G_RANK = 64
RWKV_COLS = 3 * W_GROUP + RWKV_W_RANK + RWKV_A_RANK + RWKV_G_RANK
RWKV_DECAY_SCALE = 0.6065306597126334
CONV_W = 3
EPS = 1e-6
GN_EPS = 64e-5
NEG_BIG = -1e30
EXP_CLAMP = 80.0

SMALL_W = LANES
OFF_SMALL = RWKV_COLS
OFF_MAIN = RWKV_COLS + SMALL_W
D_PROJ = OFF_MAIN + 9 * W_GROUP
LANE_MI = GLA_RANK
LANE_MF = GLA_RANK + N_HEADS
(BLK_U, BLK_MQ, BLK_MK, BLK_MV, BLK_MO, BLK_GQ, BLK_GK, BLK_GV, BLK_GG) = range(
    OFF_MAIN // W_GROUP, OFF_MAIN // W_GROUP + 9)
BLK_SMALL = OFF_SMALL // SMALL_W

CHUNK_ML = 128
CHUNK_RW = 64
SEQS_PER_STEP = 8
SEQS_PER_STEP_ML = 8
VMEM_LIMIT = 56 * 1024 * 1024


def _cparams(*sem):
    return pltpu.CompilerParams(dimension_semantics=sem, vmem_limit_bytes=VMEM_LIMIT)


def _dot(a, b):
    return jnp.dot(a.astype(BF16), b.astype(BF16), preferred_element_type=F32)


def _dot_nt(a, b):
    return lax.dot_general(a.astype(BF16), b.astype(BF16), (((1,), (1,)), ((), ())),
                           preferred_element_type=F32)


def _dot_tn(a, b):
    return lax.dot_general(a.astype(BF16), b.astype(BF16), (((0,), (0,)), ((), ())),
                           preferred_element_type=F32)


def _split2(x):
    hi = x.astype(BF16)
    lo = (x - hi.astype(F32)).astype(BF16)
    return hi, lo


def _dot_exact_rhs(a01, x):
    hi, lo = _split2(x)
    f = lambda p: jnp.dot(a01, p, preferred_element_type=F32)
    return f(hi) + f(lo)


def _dot_exact_lhs(x, b01):
    hi, lo = _split2(x)
    f = lambda p: jnp.dot(p, b01, preferred_element_type=F32)
    return f(hi) + f(lo)


def _sigmoid(x):
    return 1.0 / (1.0 + jnp.exp(-x))


def _log_sigmoid(x):
    return jnp.minimum(x, 0.0) - jnp.log(1.0 + jnp.exp(-jnp.abs(x)))


def _gelu(x):
    return 0.5 * x * (1.0 + jnp.tanh(math.sqrt(2.0 / math.pi) * (x + 0.044715 * (x * x * x))))


def _iota(shape, dim):
    return lax.broadcasted_iota(jnp.int32, shape, dim)


def _head_ones():
    r = _iota((W_GROUP, W_GROUP), 0) >> HEAD_SHIFT
    c = _iota((W_GROUP, W_GROUP), 1) >> HEAD_SHIFT
    return jnp.where(r == c, 1.0, 0.0).astype(BF16)


def _head_sum(x, ones):
    return _dot_exact_lhs(x, ones)


def _head_lane_mask(h, rows=1):
    lane = _iota((rows, W_GROUP), 1) >> HEAD_SHIFT
    return jnp.where(lane == h, 1.0, 0.0).astype(F32)


def _rmsnorm_rows(x, g):
    ms = jnp.mean(x * x, axis=-1, keepdims=True)
    return x * lax.rsqrt(ms + EPS) * g


def _tril_ones(n, strict=False):
    r = _iota((n, n), 0)
    c = _iota((n, n), 1)
    return jnp.where((r > c) if strict else (r >= c), 1.0, 0.0).astype(BF16)


def _triu_ones(n):
    r = _iota((n, n), 0)
    c = _iota((n, n), 1)
    return jnp.where(r <= c, 1.0, 0.0).astype(BF16)


def _store_head_blocks(out_ref, scr_ref, transpose=False):
    for g in range(scr_ref.shape[0]):
        m = scr_ref[g].T if transpose else scr_ref[g]
        for h in range(N_HEADS):
            lo, hi = h * HEAD_DIM, (h + 1) * HEAD_DIM
            out_ref[g, h] = m[lo:hi, lo:hi]


def _HEAD_BLOCKS_SPEC(G):
    return pl.BlockSpec((G, N_HEADS, HEAD_DIM, HEAD_DIM), lambda b, c: (b, 0, 0, 0))


def _inproj_kernel(x_ref, g_ref, w_ref, wg_ref, proj_ref, gt_ref):
    hn = _rmsnorm_rows(x_ref[...], g_ref[...]).astype(BF16)
    proj_ref[...] = jnp.dot(hn, w_ref[...], preferred_element_type=F32)
    gt_ref[0] = lax.dot_general(wg_ref[...], hn, (((1,), (1,)), ((), ())),
                                preferred_element_type=F32)


def _inproj(x2d, g, w, wg, n_seq, t_len, tm):
    n, d = x2d.shape
    tiles_per_seq = t_len // tm
    return pl.pallas_call(
        _inproj_kernel,
        grid=(n // tm,),
        in_specs=[
            pl.BlockSpec((tm, d), lambda i: (i, 0)),
            pl.BlockSpec((1, d), lambda i: (0, 0)),
            pl.BlockSpec((d, D_PROJ), lambda i: (0, 0), pipeline_mode=pl.Buffered(1)),
            pl.BlockSpec((8, d), lambda i: (0, 0)),
        ],
        out_specs=[
            pl.BlockSpec((tm, D_PROJ), lambda i: (i, 0)),
            pl.BlockSpec((1, 8, tm), lambda i: (i // tiles_per_seq, 0, i % tiles_per_seq)),
        ],
        out_shape=[
            jax.ShapeDtypeStruct((n, D_PROJ), F32),
            jax.ShapeDtypeStruct((n_seq, 8, t_len), F32),
        ],
        compiler_params=_cparams("arbitrary"),
        name="inproj",
    )(x2d, g, w, wg)


def _s5_kernel(u_ref, unext_ref, h0re_ref, h0im_ref, bmat_ref, lre_ref, lim_ref, cmat_ref, d_ref, wglu_ref,
               y_ref, hre_out, him_out, bu_scr, hs_scr, hre_scr, him_scr, *, bp, tt):
    i = pl.program_id(0)
    nt = S5_WIDTH // LANES

    def project(u_tile, slot):
        bu = jnp.dot(u_tile.astype(BF16), bmat_ref[...], preferred_element_type=F32)
        for j in range(2 * nt):
            bu_scr[slot, j] = bu[:, j * LANES:(j + 1) * LANES]

    @pl.when(i == 0)
    def _():
        hre_scr[...] = h0re_ref[...]
        him_scr[...] = h0im_ref[...]
        project(u_ref[...], 0)

    slot = lax.rem(i, 2)
    lre = [lre_ref[:, j * LANES:(j + 1) * LANES] for j in range(nt)]
    lim = [lim_ref[:, j * LANES:(j + 1) * LANES] for j in range(nt)]

    def step(t, carry):
        rows = pl.ds(pl.multiple_of(t * bp, bp), bp)
        new = []
        for j in range(nt):
            hre, him = carry[j], carry[nt + j]
            new.append((lre[j] * hre - lim[j] * him + bu_scr[slot, j, rows, :],
                        lre[j] * him + lim[j] * hre + bu_scr[slot, nt + j, rows, :]))
        out = tuple(x[0] for x in new) + tuple(x[1] for x in new)
        for j in range(2 * nt):
            hs_scr[j, rows, :] = out[j]
        return out

    carry = tuple(hre_scr[:, j * LANES:(j + 1) * LANES] for j in range(nt)) + tuple(
        him_scr[:, j * LANES:(j + 1) * LANES] for j in range(nt))
    carry = step(0, carry) if tt == 1 else lax.fori_loop(0, tt, step, carry)
    hre = jnp.concatenate(carry[0:nt], axis=1)
    him = jnp.concatenate(carry[nt:2 * nt], axis=1)
    hre_scr[...] = hre
    him_scr[...] = him
    hre_out[...] = hre
    him_out[...] = him

    project(unext_ref[...], lax.rem(i + 1, 2))
    u = u_ref[...]
    hs = jnp.concatenate([hs_scr[j] for j in range(2 * nt)], axis=1)
    y = jnp.dot(hs.astype(BF16), cmat_ref[...], preferred_element_type=F32) + d_ref[...] * u
    z = _gelu(y)
    y_ref[...] = (z * _sigmoid(_dot(z, wglu_ref[...]))).astype(y_ref.dtype)


def _s5(u_tb, h0re, h0im, sp, bp, t_len, tt):
    n = u_tb.shape[0]
    nsteps = t_len // tt
    full = lambda shape: pl.BlockSpec(shape, lambda i: (0,) * len(shape))
    tile = pl.BlockSpec((tt * bp, W_GROUP), lambda i: (i, 0))
    next_tile = pl.BlockSpec((tt * bp, W_GROUP), lambda i: (jnp.minimum(i + 1, nsteps - 1), 0))
    return pl.pallas_call(
        functools.partial(_s5_kernel, bp=bp, tt=tt),
        grid=(nsteps,),
        in_specs=[
            tile, next_tile,
            full((bp, S5_WIDTH)), full((bp, S5_WIDTH)),
            full((W_GROUP, 2 * S5_WIDTH)),
            full((bp, S5_WIDTH)), full((bp, S5_WIDTH)),
            full((2 * S5_WIDTH, W_GROUP)),
            full((1, W_GROUP)),
            full((W_GROUP, W_GROUP)),
        ],
        out_specs=[
            tile,
            full((bp, S5_WIDTH)), full((bp, S5_WIDTH)),
        ],
        out_shape=[
            jax.ShapeDtypeStruct((n, W_GROUP), BF16),
            jax.ShapeDtypeStruct((bp, S5_WIDTH), F32),
            jax.ShapeDtypeStruct((bp, S5_WIDTH), F32),
        ],
        scratch_shapes=[
            pltpu.VMEM((2, 2 * S5_WIDTH // LANES, tt * bp, LANES), F32),
            pltpu.VMEM((2 * S5_WIDTH // LANES, tt * bp, LANES), F32),
            pltpu.VMEM((bp, S5_WIDTH), F32),
            pltpu.VMEM((bp, S5_WIDTH), F32),
        ],
        compiler_params=_cparams("arbitrary"),
        name="s5_scan",
    )(u_tb, u_tb, h0re, h0im, sp["bmat"], jnp.broadcast_to(sp["lre"], (bp, S5_WIDTH)),
      jnp.broadcast_to(sp["lim"], (bp, S5_WIDTH)), sp["cmat"], sp["d"], sp["wglu"])


def _run_interleaved(chains):
    live = list(chains)
    while live:
        still = []
        for ch in live:
            try:
                next(ch)
                still.append(ch)
            except StopIteration:
                pass
        live = still


def _mlstm_kernel(q_ref, k_ref, v_ref, og_ref, sm_ref, gt_ref, brow_ref, bcol_ref, norm_ref,
                  y_ref, c_out, n_out, m_out, c_scr, n_scr, m_scr, *, L, nc, G):
    c = pl.program_id(1)

    @pl.when(c == 0)
    def _():
        c_scr[...] = jnp.zeros_like(c_scr)
        n_scr[...] = jnp.zeros_like(n_scr)
        m_scr[...] = jnp.zeros_like(m_scr)

    ones = _head_ones()
    tril = _tril_ones(L)
    triu = _triu_ones(L)
    l_shift = L.bit_length() - 1
    src = _iota((LANES, N_HEADS * L), 0)
    to_scores = jnp.where(src == LANE_MI + (_iota((LANES, N_HEADS * L), 1) >> l_shift), 1.0, 0.0).astype(BF16)
    src = _iota((LANES, W_GROUP), 0)
    to_feats = jnp.where(src == LANE_MI + (_iota((LANES, W_GROUP), 1) >> HEAD_SHIFT), 1.0, 0.0).astype(BF16)
    seg = _iota((N_HEADS * L, W_GROUP), 0) >> l_shift
    score_sum = jnp.where(seg == (_iota((N_HEADS * L, W_GROUP), 1) >> HEAD_SHIFT), 1.0, 0.0).astype(BF16)
    causal = _iota((L, N_HEADS * L), 0) >= (_iota((L, N_HEADS * L), 1) & (L - 1))
    t_idx = _iota((L, LANES), 0)

    seqs = list(range(G))
    cat = lambda xs: jnp.concatenate(xs, axis=0)
    split = lambda x: [x[i * L:(i + 1) * L] for i in seqs]

    def running_max(x):
        shift = 1
        while shift < L:
            x = jnp.maximum(x, jnp.where(t_idx >= shift, pltpu.roll(x, shift, 0), NEG_BIG))
            shift *= 2
        return x

    q = [q_ref[i] for i in seqs]
    k = [k_ref[i] * (HEAD_DIM ** -0.5) for i in seqs]
    v = [v_ref[i] for i in seqs]
    C = [c_scr[i] for i in seqs]
    n = [n_scr[i, 0:1, :] for i in seqs]
    m0 = [m_scr[i, 0:1, :] for i in seqs]
    pre = [sm_ref[i] + brow_ref[...] for i in seqs]
    gtb = [gt_ref[i] + bcol_ref[...] for i in seqs]
    bc_col = _each(lambda x: _dot_exact_rhs(tril, _log_sigmoid(x)), pre)
    bc_rows = _dot_exact_lhs(_log_sigmoid(cat(gtb)), triu)
    qC = _each(_dot_nt, q, C)
    qk = _each(lambda q_, k_: _dot_nt(q_, _stack_heads(k_)), q, k)
    qn = _head_sum(cat(_each(lambda q_, n_: q_ * n_, q, n)), ones)

    b_col = _each(lambda x: pltpu.roll(x, LANES - N_HEADS, 1), bc_col)
    r_col = _each(lambda p_, b_: p_ - b_, pre, b_col)
    mu = _each(lambda r_, m_: jnp.maximum(running_max(r_), m_), r_col, m0)
    mu_last = _each(lambda x: x[L - 1:L, :], mu)
    mu_scores = split(_dot_exact_lhs(cat(mu), to_scores))

    def weights(i):
        r_rows = gtb[i][0:N_HEADS, :] - bc_rows[8 * i + N_HEADS:8 * i + 2 * N_HEADS, :]
        r_all = jnp.concatenate([r_rows[h:h + 1, :] for h in range(N_HEADS)], axis=1)
        return jnp.exp(jnp.where(causal, r_all - mu_scores[i], NEG_BIG)) * qk[i]

    ws = [weights(i) for i in seqs]
    wsv = _each(lambda w_, v_: _dot(w_, _stack_heads(v_)), ws, v)
    w0 = _dot(cat(_each(lambda m_, u_: jnp.exp(m_ - u_), m0, mu)), to_feats)
    m_row = _dot_exact_lhs(cat(_each(lambda b_, u_: b_ + u_, b_col, mu)), to_feats)
    den = w0 * qn + _dot(cat(ws), score_sum)
    hc = (w0 * cat(qC) + cat(wsv)) / jnp.maximum(jnp.abs(den), jnp.exp(-m_row))

    wl = split(_dot(cat(_each(lambda r_, u_: jnp.exp(r_ - u_), r_col, mu_last)), to_feats))
    w0l_rows = _each(lambda m_, u_: jnp.exp(m_ - u_), m0, mu_last)
    w0l_rows = w0l_rows + [jnp.zeros((1, LANES), F32)] * (-G % 8)
    w0l = _dot_exact_lhs(cat(w0l_rows), to_feats)
    kw = _each(lambda k_, w_: k_ * w_, k, wl)
    upd = _each(_dot_tn, v, kw)
    for i in seqs:
        c_scr[i] = w0l[i:i + 1, :] * C[i] + ones.astype(F32) * upd[i]
        n_scr[i, 0:1, :] = w0l[i:i + 1, :] * n[i] + jnp.sum(kw[i], axis=0, keepdims=True)
        m_scr[i, 0:1, :] = b_col[i][L - 1:L, :] + mu_last[i]

    ms = _head_sum(hc * hc, ones) * (1.0 / HEAD_DIM)
    y = split(hc * lax.rsqrt(ms + EPS) * norm_ref[...])
    for i in seqs:
        y_ref[i] = (y[i] * _sigmoid(og_ref[i])).astype(y_ref.dtype)

    @pl.when(c == nc - 1)
    def _():
        _store_head_blocks(c_out, c_scr)
        n_out[...] = jnp.broadcast_to(n_scr[:, 0:1, :], n_out.shape)
        m_out[...] = m_scr[...]


def _seq_spec(G, L, blk, width=W_GROUP):
    return pl.BlockSpec((G, L, width), lambda b, c: (b, c, blk))


def _mlstm(proj, gt, p, n_seq, t_len):
    L = CHUNK_ML
    G = SEQS_PER_STEP_ML
    nc = t_len // L
    const = lambda shape: pl.BlockSpec(shape, lambda b, c: (0,) * len(shape))
    per_seq = lambda rows, width: pl.BlockSpec((G, rows, width), lambda b, c: (b, 0, 0))
    proj3 = proj.reshape(n_seq, t_len, D_PROJ)
    y, c_new, n_new, m_new = pl.pallas_call(
        functools.partial(_mlstm_kernel, L=L, nc=nc, G=G),
        grid=(n_seq // G, nc),
        in_specs=[
            _seq_spec(G, L, BLK_MQ), _seq_spec(G, L, BLK_MK), _seq_spec(G, L, BLK_MV),
            _seq_spec(G, L, BLK_MO), _seq_spec(G, L, BLK_SMALL, SMALL_W),
            pl.BlockSpec((G, 8, L), lambda b, c: (b, 0, c)),
            const((1, SMALL_W)), const((8, L)), const((1, W_GROUP)),
        ],
        out_specs=[
            pl.BlockSpec((G, L, W_GROUP), lambda b, c: (b, c, 0)),
            _HEAD_BLOCKS_SPEC(G), per_seq(8, W_GROUP), per_seq(8, LANES),
        ],
        out_shape=[
            jax.ShapeDtypeStruct((n_seq, t_len, W_GROUP), BF16),
            jax.ShapeDtypeStruct((n_seq, N_HEADS, HEAD_DIM, HEAD_DIM), F32),
            jax.ShapeDtypeStruct((n_seq, 8, W_GROUP), F32),
            jax.ShapeDtypeStruct((n_seq, 8, LANES), F32),
        ],
        scratch_shapes=[
            pltpu.VMEM((G, W_GROUP, W_GROUP), F32),
            pltpu.VMEM((G, 8, W_GROUP), F32),
            pltpu.VMEM((G, 8, LANES), F32),
        ],
        compiler_params=_cparams("arbitrary", "arbitrary"),
        name="mlstm_chunk",
    )(proj3, proj3, proj3, proj3, proj3, gt, p["ml_brow"], p["ml_bcol"], p["ml_norm"])
    return y.reshape(n_seq * t_len, W_GROUP), c_new, n_new, m_new


def _gla_kernel(q_ref, k_ref, v_ref, gg_ref, sm_ref, wa_ref, ba_ref, norm_ref,
                y_ref, s_out, s_scr, *, L, nc, G):
    c = pl.program_id(1)

    @pl.when(c == 0)
    def _():
        s_scr[...] = jnp.zeros_like(s_scr)

    ones = _head_ones()
    tril = _tril_ones(L)
    causal = _iota((L, N_HEADS * L), 0) >= (_iota((L, N_HEADS * L), 1) & (L - 1))

    def chain(gi):
        q = q_ref[gi] * (HEAD_DIM ** -0.5)
        k = k_ref[gi]
        v = v_ref[gi]
        la = _log_sigmoid(_dot(sm_ref[gi], wa_ref[...]) + ba_ref[...]) * (1.0 / GLA_GATE_TEMP)
        yield
        bc = _dot_exact_rhs(tril, la)
        yield
        qs = q * jnp.exp(bc)
        kh = k * jnp.exp(jnp.minimum(-bc, EXP_CLAMP))
        ST = s_scr[gi]
        att = jnp.where(causal, _dot_nt(qs, _stack_heads(kh)), 0.0)
        from_state = _dot_nt(qs, ST)
        yield
        o = from_state + _dot(att, _stack_heads(v))
        b_last = bc[L - 1:L, :]
        kbar = k * jnp.exp(b_last - bc)
        s_scr[gi] = ST * jnp.exp(b_last) + ones.astype(F32) * _dot_tn(v, kbar)
        yield
        ms = _head_sum(o * o, ones) * (1.0 / HEAD_DIM)
        gg = gg_ref[gi]
        y = o * lax.rsqrt(ms + EPS) * norm_ref[...] * (gg * _sigmoid(gg))
        y_ref[gi] = y.astype(y_ref.dtype)

    _run_interleaved([chain(gi) for gi in range(G)])

    @pl.when(c == nc - 1)
    def _():
        _store_head_blocks(s_out, s_scr, transpose=True)


def _gla(proj, p, n_seq, t_len):
    L = CHUNK_ML
    G = SEQS_PER_STEP_ML
    nc = t_len // L
    const = lambda shape: pl.BlockSpec(shape, lambda b, c: (0,) * len(shape))
    proj3 = proj.reshape(n_seq, t_len, D_PROJ)
    y, s_new = pl.pallas_call(
        functools.partial(_gla_kernel, L=L, nc=nc, G=G),
        grid=(n_seq // G, nc),
        in_specs=[
            _seq_spec(G, L, BLK_GQ), _seq_spec(G, L, BLK_GK), _seq_spec(G, L, BLK_GV),
            _seq_spec(G, L, BLK_GG), _seq_spec(G, L, BLK_SMALL, SMALL_W),
            const((SMALL_W, W_GROUP)), const((1, W_GROUP)), const((1, W_GROUP)),
        ],
        out_specs=[
            pl.BlockSpec((G, L, W_GROUP), lambda b, c: (b, c, 0)),
            _HEAD_BLOCKS_SPEC(G),
        ],
        out_shape=[
            jax.ShapeDtypeStruct((n_seq, t_len, W_GROUP), BF16),
            jax.ShapeDtypeStruct((n_seq, N_HEADS, HEAD_DIM, HEAD_DIM), F32),
        ],
        scratch_shapes=[pltpu.VMEM((G, W_GROUP, W_GROUP), F32)],
        compiler_params=_cparams("arbitrary", "arbitrary"),
        name="gla_chunk",
    )(proj3, proj3, proj3, proj3, proj3, p["gla_wa"], p["gla_ba"], p["gla_norm"])
    return y.reshape(n_seq * t_len, W_GROUP), s_new


def _rwkv_vectors(rc, prev, p_mu, p_w0, p_a0, p_kk, p_ka, p_rk, w2, a2, g2, ones):
    xm = rc + p_mu * (prev - rc)
    rr = xm[:, 0:W_GROUP]
    rk = xm[:, W_GROUP:2 * W_GROUP]
    rv = xm[:, 2 * W_GROUP:3 * W_GROUP]
    tail = xm[:, 3 * W_GROUP:RWKV_COLS]
    lw = -RWKV_DECAY_SCALE * _sigmoid(p_w0 + _dot(jnp.tanh(tail), w2))
    a = _sigmoid(p_a0 + _dot(tail, a2))
    g = _dot(_sigmoid(tail), g2)
    kk = rk * p_kk
    kk = kk * lax.rsqrt(jnp.maximum(_head_sum(kk * kk, ones), 1e-24))
    kt = rk * (1.0 + (a - 1.0) * p_ka)
    bonus = _head_sum(rr * kt * p_rk, ones) * rv
    return rr, lw, kt, rv, kk, a, g, bonus


def _head_layernorm(o, g, ones):
    mu = _head_sum(o, ones) * (1.0 / HEAD_DIM)
    oc = o - mu
    var = _head_sum(oc * oc, ones) * (1.0 / HEAD_DIM)
    return oc * lax.rsqrt(var + GN_EPS) * g


def _stack_heads(x):
    xb = x.astype(BF16)
    lane_head = _iota((1, W_GROUP), 1) >> HEAD_SHIFT
    return jnp.concatenate([jnp.where(lane_head == h, xb, jnp.zeros_like(xb)) for h in range(N_HEADS)],
                           axis=0)


def _each(f, *seqs):
    return [f(*args) for args in zip(*seqs)]


def _block_mm(x_list, y_list):
    return _each(lambda x, y: _dot(x, _stack_heads(y)), x_list, y_list)


def _unit_lower_inverse(a_list, L):
    t_idx = _iota((L, N_HEADS * L), 0)
    s_idx = _iota((L, N_HEADS * L), 1) & (L - 1)
    eye = jnp.where(t_idx == s_idx, 1.0, 0.0).astype(F32)
    in16 = (t_idx >> 4) == (s_idx >> 4)
    in32 = ((t_idx >> 5) == (s_idx >> 5)) & ((t_idx >> 4) > (s_idx >> 4))
    in64 = (t_idx >> 5) > (s_idx >> 5)
    x16 = _each(lambda a: jnp.where(in16, -a, 0.0), a_list)
    x2 = _block_mm(x16, x16)
    x4 = _block_mm(x2, x2)
    x8 = _block_mm(x4, x4)
    t = _each(lambda x: eye + x, x16)
    for xp in (x2, x4, x8):
        t = _each(lambda t_, d: t_ + d, t, _block_mm(t, xp))
    for sel in (in32, in64):
        w = _block_mm(_each(lambda a: jnp.where(sel, a, 0.0), a_list), t)
        t = _each(lambda t_, d: t_ - d, t, _block_mm(t, w))
    return t


def _rwkv_chunks(rcs, prev_rows, states, prm, ones, L):
    (mu, w0, a0, pkk, pka, prk, w2, a2, g2, norm) = prm
    first_row = _iota((L, RWKV_COLS), 0) == 0
    prevs = _each(lambda rc, pr: jnp.where(first_row, pr, pltpu.roll(rc, 1, 0)), rcs, prev_rows)
    n_seq = len(rcs)
    stacked = _rwkv_vectors(jnp.concatenate(rcs, axis=0), jnp.concatenate(prevs, axis=0),
                            mu, w0, a0, pkk, pka, prk, w2, a2, g2, ones)
    r, lw, kx, v, kk, a, g, bonus = [[x[i * L:(i + 1) * L] for i in range(n_seq)] for x in stacked]

    tril = _tril_ones(L)
    lc = _each(lambda x: _dot_exact_rhs(tril, x), lw)
    akk = _each(lambda a_, k_: a_ * k_, a, kk)
    e_neg = _each(lambda x: jnp.exp(-x), lc)
    l_last = _each(lambda x: x[L - 1:L, :], lc)
    e_end = _each(lambda ll, x: jnp.exp(ll - x), l_last, lc)
    kr = _each(lambda k_, r_, c_, w_: jnp.concatenate([k_ * jnp.exp(c_ - w_), r_ * jnp.exp(c_)], axis=0),
               kk, r, lc, lw)
    a_hat = _each(lambda x, e: _stack_heads(x * e), akk, e_neg)
    k_hat = _each(lambda x, e: _stack_heads(x * e), kx, e_neg)
    v_bd = _each(_stack_heads, v)

    t_idx = _iota((L, N_HEADS * L), 0)
    s_idx = _iota((L, N_HEADS * L), 1) & (L - 1)
    strict = t_idx > s_idx
    incl = t_idx >= s_idx
    pa = _each(_dot_nt, kr, a_hat)
    pk = _each(_dot_nt, kr, k_hat)
    a_ua = _each(lambda p_: jnp.where(strict, p_[0:L], 0.0), pa)
    b_ra = _each(lambda p_: jnp.where(incl, p_[L:2 * L], 0.0), pa)
    ab_k = _each(lambda p_: jnp.concatenate([jnp.where(strict, p_[0:L], 0.0),
                                             jnp.where(incl, p_[L:2 * L], 0.0)], axis=0), pk)
    T = _unit_lower_inverse(a_ua, L)

    from_state = _each(_dot_nt, kr, states)
    from_v = _each(_dot, ab_k, v_bd)
    rhs = _each(lambda s_, v_: s_[0:L] + v_[0:L], from_state, from_v)
    U = _block_mm(T, rhs)
    corr = _block_mm(b_ra, U)
    o = _each(lambda s_, v_, c_: s_[L:2 * L] + v_[L:2 * L] - c_, from_state, from_v, corr)
    upd = _each(lambda v_, u_, k_, a_, e: _dot_tn(jnp.concatenate([v_, u_], axis=0),
                                                  jnp.concatenate([k_ * e, -(a_ * e)], axis=0)),
                v, U, kx, akk, e_end)
    s_new = _each(lambda s_, ll, d: s_ * jnp.exp(ll) + ones.astype(F32) * d, states, l_last, upd)
    y_all = ((_head_layernorm(jnp.concatenate(o, axis=0), norm, ones) + stacked[7]) * stacked[6])
    return [y_all[i * L:(i + 1) * L] for i in range(n_seq)], s_new


def _rwkv_kernel(rc_ref, mu_ref, w0_ref, a0_ref, kk_ref, ka_ref, rk_ref, w2_ref, a2_ref, g2_ref,
                 norm_ref, y_ref, s_out, shift_out, s_scr, prev_scr, *, L, nc, G):
    c = pl.program_id(1)

    @pl.when(c == 0)
    def _():
        s_scr[...] = jnp.zeros_like(s_scr)
        prev_scr[...] = jnp.zeros_like(prev_scr)

    ones = _head_ones()
    prm = (mu_ref[...], w0_ref[...], a0_ref[...], kk_ref[...], ka_ref[...], rk_ref[...],
           w2_ref[...], a2_ref[...], g2_ref[...], norm_ref[...])
    rcs = [rc_ref[gi] for gi in range(G)]
    ys, s_new = _rwkv_chunks(rcs, [prev_scr[gi, 0:1, :] for gi in range(G)],
                             [s_scr[gi] for gi in range(G)], prm, ones, L)
    for gi in range(G):
        prev_scr[gi, 0:1, :] = rcs[gi][L - 1:L, :]
        s_scr[gi] = s_new[gi]
        y_ref[gi] = ys[gi].astype(y_ref.dtype)

    @pl.when(c == nc - 1)
    def _():
        _store_head_blocks(s_out, s_scr)
        shift_out[...] = jnp.broadcast_to(prev_scr[:, 0:1, :], shift_out.shape)


def _rwkv(proj, p, n_seq, t_len):
    L = CHUNK_RW
    G = SEQS_PER_STEP
    nc = t_len // L
    const = lambda shape: pl.BlockSpec(shape, lambda b, c: (0,) * len(shape))
    row = const((1, W_GROUP))
    pad = const((LANES, W_GROUP))
    y, s_new, shift = pl.pallas_call(
        functools.partial(_rwkv_kernel, L=L, nc=nc, G=G),
        grid=(n_seq // G, nc),
        in_specs=[
            pl.BlockSpec((G, L, RWKV_COLS), lambda b, c: (b, c, 0)),
            const((1, RWKV_COLS)), row, row, row, row, row, pad, pad, pad, row,
        ],
        out_specs=[
            pl.BlockSpec((G, L, W_GROUP), lambda b, c: (b, c, 0)),
            _HEAD_BLOCKS_SPEC(G),
            pl.BlockSpec((G, 8, RWKV_COLS), lambda b, c: (b, 0, 0)),
        ],
        out_shape=[
            jax.ShapeDtypeStruct((n_seq, t_len, W_GROUP), BF16),
            jax.ShapeDtypeStruct((n_seq, N_HEADS, HEAD_DIM, HEAD_DIM), F32),
            jax.ShapeDtypeStruct((n_seq, 8, RWKV_COLS), F32),
        ],
        scratch_shapes=[
            pltpu.VMEM((G, W_GROUP, W_GROUP), F32),
            pltpu.VMEM((G, 8, RWKV_COLS), F32),
        ],
        compiler_params=_cparams("arbitrary", "arbitrary"),
        name="rwkv_chunk",
    )(proj.reshape(n_seq, t_len, D_PROJ), p["rw_mu"], p["rw_w0"], p["rw_a0"], p["rw_kk"], p["rw_ka"],
      p["rw_rk"], p["rw_w2"], p["rw_a2"], p["rw_g2"], p["rw_norm"])
    return y.reshape(n_seq * t_len, W_GROUP), s_new, shift


FF_CHUNK = 256


def _mix_residual(x_ref, ys, wout_ref):
    acc = x_ref[...]
    for j, y in enumerate(ys):
        acc = acc + jnp.dot(y.astype(BF16), wout_ref[j * W_GROUP:(j + 1) * W_GROUP, :],
                            preferred_element_type=F32)
    return acc


def _ffn_body(x1, nrm_ref, wup_ref, cw_ref, cb_ref, wdn_ref, prev_rows, d_ff, act_scr):
    h2 = _rmsnorm_rows(x1, nrm_ref[...]).astype(BF16)
    for j in range(d_ff // FF_CHUNK):
        lo, hi = j * FF_CHUNK, (j + 1) * FF_CHUNK
        ug = jnp.dot(h2, wup_ref[:, lo:hi], preferred_element_type=F32)
        uv = jnp.dot(h2, wup_ref[:, d_ff + lo:d_ff + hi], preferred_element_type=F32)
        p2, p1 = prev_rows(j, ug)
        conv = (cb_ref[:, lo:hi] + cw_ref[0:1, lo:hi] * p2 + cw_ref[1:2, lo:hi] * p1
                + cw_ref[2:3, lo:hi] * ug)
        act_scr[:, lo:hi] = (_gelu(conv) * uv).astype(BF16)
    return x1 + jnp.dot(act_scr[...], wdn_ref[...], preferred_element_type=F32)


def _ffn_seq_kernel(x_ref, y0_ref, y1_ref, y2_ref, y3_ref, wout_ref, nrm_ref, wup_ref, cw_ref, cb_ref,
                    wdn_ref, nf_ref, o_ref, buf_out, carry_scr, act_scr, *, tm, d_ff, final):
    t = pl.program_id(1)

    @pl.when(t == 0)
    def _():
        carry_scr[...] = jnp.zeros_like(carry_scr)

    x1 = _mix_residual(x_ref, (y0_ref[...], y1_ref[...], y2_ref[...], y3_ref[...]), wout_ref)
    row = _iota((tm, FF_CHUNK), 0)

    def prev_rows(j, ug):
        lo, hi = j * FF_CHUNK, (j + 1) * FF_CHUNK
        c0 = carry_scr[0:1, lo:hi]
        c1 = carry_scr[1:2, lo:hi]
        p1 = jnp.where(row == 0, c1, pltpu.roll(ug, 1, 0))
        p2 = jnp.where(row == 0, c0, jnp.where(row == 1, c1, pltpu.roll(ug, 2, 0)))
        carry_scr[0:2, lo:hi] = ug[tm - 2:tm, :]
        return p2, p1

    out = _ffn_body(x1, nrm_ref, wup_ref, cw_ref, cb_ref, wdn_ref, prev_rows, d_ff, act_scr)
    if final:
        out = _rmsnorm_rows(out, nf_ref[...])
    o_ref[...] = out
    buf_out[0] = carry_scr[...]


def _ffn_step_kernel(x_ref, y0_ref, y1_ref, y2_ref, y3_ref, wout_ref, nrm_ref, wup_ref, cw_ref, cb_ref,
                     wdn_ref, nf_ref, p2_ref, p1_ref, o_ref, ug_out, act_scr, *, d_ff, final):
    x1 = _mix_residual(x_ref, (y0_ref[...], y1_ref[...], y2_ref[...], y3_ref[...]), wout_ref)

    def prev_rows(j, ug):
        lo, hi = j * FF_CHUNK, (j + 1) * FF_CHUNK
        ug_out[:, lo:hi] = ug
        return p2_ref[:, lo:hi], p1_ref[:, lo:hi]

    out = _ffn_body(x1, nrm_ref, wup_ref, cw_ref, cb_ref, wdn_ref, prev_rows, d_ff, act_scr)
    if final:
        out = _rmsnorm_rows(out, nf_ref[...])
    o_ref[...] = out


def _ffn_seq(x2d, ys, p, norm_final, n_seq, t_len, tm, final):
    n, d = x2d.shape
    d_ff = p["ffn_wdn"].shape[1]
    lay = p["layer"]
    nt = t_len // tm
    rows = lambda w: pl.BlockSpec((tm, w), lambda b, t: (b * nt + t, 0))
    const = lambda shape: pl.BlockSpec(shape, lambda b, t: (0,) * len(shape))
    layered = lambda shape: pl.BlockSpec((None,) + shape, lambda b, t: (lay, 0, 0),
                                         pipeline_mode=pl.Buffered(1))
    return pl.pallas_call(
        functools.partial(_ffn_seq_kernel, tm=tm, d_ff=d_ff, final=final),
        grid=(n_seq, nt),
        in_specs=[rows(d)] + [rows(W_GROUP)] * 4 + [
            layered((d, d)), const((1, d)), layered((d, 2 * d_ff)), const((CONV_W, d_ff)),
            const((1, d_ff)), layered((d_ff, d)), const((1, d))],
        out_specs=[rows(d), pl.BlockSpec((1, CONV_W - 1, d_ff), lambda b, t: (b, 0, 0))],
        out_shape=[jax.ShapeDtypeStruct((n, d), F32),
                   jax.ShapeDtypeStruct((n_seq, CONV_W - 1, d_ff), F32)],
        scratch_shapes=[pltpu.VMEM((CONV_W - 1, d_ff), F32), pltpu.VMEM((tm, d_ff), BF16)],
        compiler_params=_cparams("arbitrary", "arbitrary"),
        name="wout_ffn_seq",
    )(x2d, *ys, p["w_out"], p["norm_ffn"], p["ffn_wup"], p["ffn_cw"], p["ffn_cb"], p["ffn_wdn"],
      norm_final)


def _ffn_step(x2d, ys, p, norm_final, prev2, prev1, final):
    n, d = x2d.shape
    d_ff = p["ffn_wdn"].shape[1]
    lay = p["layer"]
    full = lambda shape: pl.BlockSpec(shape, lambda i: (0,) * len(shape))
    layered = lambda shape: pl.BlockSpec((None,) + shape, lambda i: (lay, 0, 0))
    return pl.pallas_call(
        functools.partial(_ffn_step_kernel, d_ff=d_ff, final=final),
        grid=(1,),
        in_specs=[full((n, d))] + [full((n, W_GROUP))] * 4 + [
            layered((d, d)), full((1, d)), layered((d, 2 * d_ff)), full((CONV_W, d_ff)),
            full((1, d_ff)), layered((d_ff, d)), full((1, d)), full((n, d_ff)), full((n, d_ff))],
        out_specs=[full((n, d)), full((n, d_ff))],
        out_shape=[jax.ShapeDtypeStruct((n, d), F32), jax.ShapeDtypeStruct((n, d_ff), F32)],
        scratch_shapes=[pltpu.VMEM((n, d_ff), BF16)],
        compiler_params=_cparams("arbitrary"),
        name="wout_ffn_step",
    )(x2d, *ys, p["w_out"], p["norm_ffn"], p["ffn_wup"], p["ffn_cw"], p["ffn_cb"], p["ffn_wdn"],
      norm_final, prev2, prev1)


def _step_vec_kernel(proj_ref, shift_ref, m_ref, brow_ref, wa_ref, ba_ref, mu_ref, w0_ref, a0_ref,
                     kk_ref, ka_ref, rk_ref, w2_ref, a2_ref, g2_ref,
                     pack_out, m_out, g_out, bonus_out):
    ones = _head_ones()
    sm = proj_ref[:, OFF_SMALL:OFF_SMALL + SMALL_W]
    pre = sm + brow_ref[...]
    logf = pltpu.roll(_log_sigmoid(pre), LANES - N_HEADS, 1)
    m_old = m_ref[...]
    m_new = jnp.maximum(logf + m_old, pre)
    m_out[...] = m_new
    lane = _iota((SMALL_W, W_GROUP), 0)
    spread_mat = jnp.where(lane == LANE_MI + (_iota((SMALL_W, W_GROUP), 1) >> HEAD_SHIFT), 1.0, 0.0).astype(BF16)
    gate_lanes = (_iota(pre.shape, 1) >= LANE_MI) & (_iota(pre.shape, 1) < LANE_MI + N_HEADS)
    spread = lambda x: _dot_exact_lhs(jnp.where(gate_lanes, x, 0.0), spread_mat)
    la = _log_sigmoid(_dot(sm, wa_ref[...]) + ba_ref[...]) * (1.0 / GLA_GATE_TEMP)
    rc = proj_ref[:, 0:RWKV_COLS]
    rr, lw, kt, rv, kk, a, g, bonus = _rwkv_vectors(
        rc, shift_ref[...], mu_ref[...], w0_ref[...], a0_ref[...], kk_ref[...], ka_ref[...],
        rk_ref[...], w2_ref[...], a2_ref[...], g2_ref[...], ones)
    main = lambda blk: proj_ref[:, blk * W_GROUP:(blk + 1) * W_GROUP]
    rows = {
        ROW_MQ: main(BLK_MQ), ROW_MK: main(BLK_MK), ROW_MV: main(BLK_MV),
        ROW_GQ: main(BLK_GQ), ROW_GK: main(BLK_GK), ROW_GV: main(BLK_GV), ROW_GA: jnp.exp(la),
        ROW_KK: kk, ROW_AKK: a * kk, ROW_W: jnp.exp(lw), ROW_KT: kt, ROW_RR: rr, ROW_RV: rv,
        ROW_IW: spread(jnp.exp(pre - m_new)), ROW_FW: spread(jnp.exp(logf + m_old - m_new)),
        ROW_M: spread(m_new),
    }
    for i, val in rows.items():
        pack_out[i] = val.T
    g_out[...] = g
    bonus_out[...] = bonus


def _step_vectors(proj, shift, m_pad, p):
    n = proj.shape[0]
    full = lambda a: pl.BlockSpec(a.shape, lambda i: (0,) * a.ndim)
    args = (proj, shift, m_pad, p["ml_brow"], p["gla_wa"], p["gla_ba"], p["rw_mu"], p["rw_w0"],
            p["rw_a0"], p["rw_kk"], p["rw_ka"], p["rw_rk"], p["rw_w2"], p["rw_a2"], p["rw_g2"])
    wide = jax.ShapeDtypeStruct((n, W_GROUP), F32)
    outs = [jax.ShapeDtypeStruct((N_PACK, W_GROUP, n), F32), jax.ShapeDtypeStruct((n, SMALL_W), F32),
            wide, wide]
    return pl.pallas_call(
        _step_vec_kernel,
        grid=(1,),
        in_specs=[full(a) for a in args],
        out_specs=[pl.BlockSpec(o.shape, lambda i, nd=len(o.shape): (0,) * nd) for o in outs],
        out_shape=outs,
        compiler_params=_cparams("arbitrary"),
        name="step_vectors",
    )(*args)


(ROW_MQ, ROW_MK, ROW_MV, ROW_GQ, ROW_GK, ROW_GV, ROW_GA, ROW_KK, ROW_AKK, ROW_W, ROW_KT, ROW_RR, ROW_RV,
 ROW_IW, ROW_FW, ROW_M) = range(16)
N_PACK = 16
(OROW_ML, OROW_GLA, OROW_RW, OROW_N) = range(4)
N_OPACK = 4
STEP_UNROLL = 4


def _heads_step_kernel(pack_ref, c_ref, n_ref, gs_ref, rs_ref, c_out, gs_out, rs_out, opack_out):
    vec = lambda i: pack_ref[i]
    one = lambda i, j: pack_ref[i, pl.ds(j, 1), :]
    colsum = lambda x: jnp.sum(x, axis=0, keepdims=True)

    k_ml = vec(ROW_MK) * (HEAD_DIM ** -0.5)
    q_ml = vec(ROW_MQ)
    iw = pack_ref[ROW_IW, 0:1, :]
    fw = pack_ref[ROW_FW, 0:1, :]
    n_new = fw * n_ref[...] + iw * k_ml
    opack_out[OROW_N] = n_new
    r_den = 1.0 / jnp.maximum(jnp.abs(colsum(n_new * q_ml)), jnp.exp(-pack_ref[ROW_M, 0:1, :]))

    def ml_body(v, carry):
        c_new = fw * c_ref[v] + (iw * one(ROW_MV, v)) * k_ml
        c_out[v] = c_new
        opack_out[OROW_ML, pl.ds(v, 1), :] = colsum(c_new * q_ml) * r_den
        return carry

    lax.fori_loop(0, HEAD_DIM, ml_body, 0, unroll=STEP_UNROLL)

    v_gla = vec(ROW_GV)

    def gla_body(k, acc):
        s_new = one(ROW_GA, k) * gs_ref[k] + one(ROW_GK, k) * v_gla
        gs_out[k] = s_new
        return acc + (one(ROW_GQ, k) * (HEAD_DIM ** -0.5)) * s_new

    opack_out[OROW_GLA] = lax.fori_loop(0, HEAD_DIM, gla_body, jnp.zeros_like(v_gla), unroll=STEP_UNROLL)

    kk, akk, w, kt, rr = vec(ROW_KK), vec(ROW_AKK), vec(ROW_W), vec(ROW_KT), vec(ROW_RR)

    def rw_body(v, carry):
        s = rs_ref[v]
        s_new = s * w - colsum(s * kk) * akk + one(ROW_RV, v) * kt
        rs_out[v] = s_new
        opack_out[OROW_RW, pl.ds(v, 1), :] = colsum(s_new * rr)
        return carry

    lax.fori_loop(0, HEAD_DIM, rw_body, 0, unroll=STEP_UNROLL)


def _heads_step(pack, ml_c, ml_n, gla_s, rw_s, layer):
    n = pack.shape[-1]
    sq = (HEAD_DIM, HEAD_DIM, n)
    head = lambda lead: pl.BlockSpec((lead, HEAD_DIM, n), lambda h: (0, h, 0))
    at_layer = lambda shape: pl.BlockSpec((None,) + shape, lambda h: (layer * N_HEADS + h,) + (0,) * len(shape))
    out_state = pl.BlockSpec((None,) + sq, lambda h: (h, 0, 0, 0))
    state = jax.ShapeDtypeStruct((N_HEADS,) + sq, F32)
    return pl.pallas_call(
        _heads_step_kernel,
        grid=(N_HEADS,),
        in_specs=[head(N_PACK), at_layer(sq), at_layer((HEAD_DIM, n)), at_layer(sq), at_layer(sq)],
        out_specs=[out_state, out_state, out_state, head(N_OPACK)],
        out_shape=[state, state, state, jax.ShapeDtypeStruct((N_OPACK, W_GROUP, n), F32)],
        compiler_params=_cparams("arbitrary"),
        name="heads_step",
    )(pack, ml_c, ml_n, gla_s, rw_s)


def _step_post_kernel(hml_ref, mo_ref, ogla_ref, gg_ref, orw_ref, g_ref, bonus_ref,
                      mln_ref, gln_ref, rwn_ref, yml_out, ygla_out, yrw_out):
    ones = _head_ones()
    hml = hml_ref[...].T
    ms = _head_sum(hml * hml, ones) * (1.0 / HEAD_DIM)
    yml_out[...] = (hml * lax.rsqrt(ms + EPS) * mln_ref[...] * _sigmoid(mo_ref[...])).astype(BF16)
    og = ogla_ref[...].T
    ms = _head_sum(og * og, ones) * (1.0 / HEAD_DIM)
    gg = gg_ref[...]
    ygla_out[...] = (og * lax.rsqrt(ms + EPS) * gln_ref[...] * (gg * _sigmoid(gg))).astype(BF16)
    yrw_out[...] = ((_head_layernorm(orw_ref[...].T, rwn_ref[...], ones) + bonus_ref[...])
                    * g_ref[...]).astype(BF16)


def _step_post(hml, mo, ogla, gg, orw, g, bonus, p):
    n = mo.shape[0]
    args = (hml, mo, ogla, gg, orw, g, bonus, p["ml_norm"], p["gla_norm"], p["rw_norm"])
    full = lambda a: pl.BlockSpec(a.shape, lambda i: (0,) * a.ndim)
    out = jax.ShapeDtypeStruct((n, W_GROUP), BF16)
    return pl.pallas_call(
        _step_post_kernel,
        grid=(1,),
        in_specs=[full(a) for a in args],
        out_specs=[pl.BlockSpec((n, W_GROUP), lambda i: (0, 0))] * 3,
        out_shape=[out] * 3,
        compiler_params=_cparams("arbitrary"),
        name="step_post",
    )(*args)


def _layer_params(l, prm):
    p = {}
    w_in = prm["w_in"][l]
    d = w_in.shape[0]
    sizes = (W_GROUP, W_GROUP, W_GROUP, W_GROUP, N_HEADS, N_HEADS, W_GROUP, W_GROUP, W_GROUP, W_GROUP,
             GLA_RANK, W_GROUP, RWKV_COLS)
    offs = [0]
    for s in sizes:
        offs.append(offs[-1] + s)
    col = lambda i: w_in[:, offs[i]:offs[i + 1]]
    (u, mq, mk, mv, mi, mf, mo, gq, gk, gv, ga, gg, rcols) = [col(i) for i in range(13)]
    small = jnp.concatenate([ga, mi, mf, jnp.zeros((d, SMALL_W - GLA_RANK - 2 * N_HEADS), F32)], axis=1)
    p["w_in"] = jnp.concatenate([rcols, small, u, mq, mk, mv, mo, gq, gk, gv, gg], axis=1).astype(BF16)
    p["w_gate_t"] = jnp.concatenate([mi, mf], axis=1).T.astype(BF16)
    p["norm_mix"] = prm["norm_mix"][l][None, :]

    lam = lax.complex(prm["s5_lam_re"][l], prm["s5_lam_im"][l])
    dt = jnp.exp(prm["s5_log_dt"][l])[:, None]
    lam_bar = jnp.exp(lam * dt)
    b_bar = ((lam_bar - 1.0) / lam)[..., None] * lax.complex(prm["s5_b_re"][l], prm["s5_b_im"][l])
    eye = jnp.eye(S5_GROUPS, dtype=F32)
    bm = lambda b: jnp.einsum("gph,gk->ghkp", b, eye).reshape(W_GROUP, S5_WIDTH)
    p["bmat"] = jnp.concatenate([bm(b_bar.real), bm(b_bar.imag)], axis=1).astype(BF16)
    cm = lambda c: jnp.einsum("ghp,gk->kpgh", c, eye).reshape(S5_WIDTH, W_GROUP)
    p["cmat"] = jnp.concatenate([cm(prm["s5_c_re"][l]), -cm(prm["s5_c_im"][l])], axis=0).astype(BF16)
    p["lre"] = lam_bar.real.reshape(1, S5_WIDTH)
    p["lim"] = lam_bar.imag.reshape(1, S5_WIDTH)
    p["d"] = prm["s5_d"][l][None, :]
    p["wglu"] = prm["s5_w_glu"][l].astype(BF16)

    bias = prm["ml_gate_bias"][l]
    p["ml_brow"] = jnp.zeros((1, SMALL_W), F32).at[0, LANE_MI:LANE_MI + 2 * N_HEADS].set(bias)
    p["ml_bcol"] = jnp.broadcast_to(bias[:, None], (2 * N_HEADS, CHUNK_ML))
    p["ml_norm"] = prm["ml_norm"][l][None, :]

    p["gla_wa"] = jnp.zeros((SMALL_W, W_GROUP), F32).at[0:GLA_RANK].set(prm["gla_w_alpha"][l]).astype(BF16)
    p["gla_ba"] = prm["gla_b_alpha"][l][None, :]
    p["gla_norm"] = prm["gla_norm"][l][None, :]

    p["rw_mu"] = prm["rw_mu"][l][None, :]
    p["rw_w0"] = prm["rw_w0"][l][None, :]
    p["rw_a0"] = prm["rw_a0"][l][None, :]
    p["rw_kk"] = prm["rw_k_k"][l][None, :]
    p["rw_ka"] = prm["rw_k_a"][l][None, :]
    p["rw_rk"] = prm["rw_r_k"][l].reshape(1, W_GROUP)
    z = jnp.zeros((LANES, W_GROUP), F32)
    p["rw_w2"] = z.at[0:RWKV_W_RANK].set(prm["rw_w2"][l]).astype(BF16)
    p["rw_a2"] = z.at[RWKV_W_RANK:RWKV_W_RANK + RWKV_A_RANK].set(prm["rw_a2"][l]).astype(BF16)
    p["rw_g2"] = z.at[RWKV_W_RANK + RWKV_A_RANK:].set(prm["rw_g2"][l]).astype(BF16)
    p["rw_norm"] = prm["rw_norm"][l][None, :]

    p["layer"] = l
    p["w_out"] = prm["w_out"].astype(BF16)
    p["norm_ffn"] = prm["norm_ffn"][l][None, :]
    p["ffn_wup"] = prm["ffn_w_up"].astype(BF16)
    p["ffn_cw"] = prm["ffn_conv_w"][l]
    p["ffn_cb"] = prm["ffn_conv_b"][l][None, :]
    p["ffn_wdn"] = prm["ffn_w_down"].astype(BF16)
    return p


def _prompt_layer(x2d, p, norm_final, n_seq, t_len, final):
    proj, gt = _inproj(x2d, p["norm_mix"], p["w_in"], p["w_gate_t"], n_seq, t_len, tm=1024)

    u_tb = proj[:, OFF_MAIN:OFF_MAIN + W_GROUP].reshape(n_seq, t_len, W_GROUP).transpose(1, 0, 2)
    zeros = jnp.zeros((n_seq, S5_WIDTH), F32)
    y_s5_tb, s5_re, s5_im = _s5(u_tb.reshape(t_len * n_seq, W_GROUP), zeros, zeros, p, n_seq, t_len, tt=128)
    y_s5 = y_s5_tb.reshape(t_len, n_seq, W_GROUP).transpose(1, 0, 2).reshape(n_seq * t_len, W_GROUP)

    y_ml, ml_c, ml_n, ml_m = _mlstm(proj, gt, p, n_seq, t_len)
    y_gla, gla_st = _gla(proj, p, n_seq, t_len)
    y_rw, rw_s, rw_shift = _rwkv(proj, p, n_seq, t_len)

    x_new, ffn_buf = _ffn_seq(x2d, (y_s5, y_ml, y_gla, y_rw), p, norm_final, n_seq, t_len, tm=512,
                              final=final)
    states = (
        s5_re.reshape(n_seq, S5_GROUPS, S5_STATE),
        s5_im.reshape(n_seq, S5_GROUPS, S5_STATE),
        ml_c,
        ml_n[:, 0, :].reshape(n_seq, N_HEADS, HEAD_DIM),
        ml_m[:, 0, LANE_MI:LANE_MI + N_HEADS],
        gla_st,
        rw_s,
        rw_shift[:, 0, :],
        ffn_buf,
    )
    return x_new, states


def _sample_layer(x2d, st, pair_states, layer, p, norm_final, final):
    (s5_re, s5_im, ml_m, rw_shift, ffn_buf) = st
    n = x2d.shape[0]
    npairs = n * N_HEADS
    proj, _ = _inproj(x2d, p["norm_mix"], p["w_in"], p["w_gate_t"], 1, n, tm=n)

    y_s5, s5_re_new, s5_im_new = _s5(proj[:, OFF_MAIN:OFF_MAIN + W_GROUP],
                                     s5_re.reshape(n, S5_WIDTH), s5_im.reshape(n, S5_WIDTH),
                                     p, n, 1, tt=1)

    m_pad = jnp.zeros((n, SMALL_W), F32).at[:, LANE_MI:LANE_MI + N_HEADS].set(ml_m)
    pack, m_new, g, bonus = _step_vectors(proj, rw_shift, m_pad, p)
    ml_c_new, gla_s_new, rw_s_new, opack = _heads_step(pack, *pair_states, layer)

    main = lambda blk: proj[:, blk * W_GROUP:(blk + 1) * W_GROUP]
    y_ml, y_gla, y_rw = _step_post(opack[OROW_ML], main(BLK_MO), opack[OROW_GLA], main(BLK_GG),
                                   opack[OROW_RW], g, bonus, p)
    x_new, ug = _ffn_step(x2d, (y_s5, y_ml, y_gla, y_rw), p, norm_final, ffn_buf[:, 0, :], ffn_buf[:, 1, :],
                          final)
    states = (
        s5_re_new.reshape(n, S5_GROUPS, S5_STATE),
        s5_im_new.reshape(n, S5_GROUPS, S5_STATE),
        ml_c_new.transpose(3, 0, 1, 2),
        opack[OROW_N].reshape(N_HEADS, HEAD_DIM, n).transpose(2, 0, 1),
        m_new[:, LANE_MI:LANE_MI + N_HEADS],
        gla_s_new.transpose(3, 0, 1, 2),
        rw_s_new.transpose(3, 0, 1, 2),
        proj[:, 0:RWKV_COLS],
        jnp.stack([ffn_buf[:, 1, :], ug], axis=1),
    )
    return x_new, states


def kernel(x_prompt, x_sample, state_s5_re, state_s5_im, state_mlstm_C, state_mlstm_n, state_mlstm_m, state_gla_S, state_rwkv_S, state_rwkv_shift, state_ffn_conv, norm_mix, w_in, s5_lam_re, s5_lam_im, s5_log_dt, s5_b_re, s5_b_im, s5_c_re, s5_c_im, s5_d, s5_w_glu, ml_gate_bias, ml_norm, gla_w_alpha, gla_b_alpha, gla_norm, rw_mu, rw_w0, rw_w2, rw_a0, rw_a2, rw_g2, rw_k_k, rw_k_a, rw_r_k, rw_norm, w_out, norm_ffn, ffn_w_up, ffn_conv_w, ffn_conv_b, ffn_w_down, norm_final):
    prm = dict(norm_mix=norm_mix, w_in=w_in, s5_lam_re=s5_lam_re, s5_lam_im=s5_lam_im, s5_log_dt=s5_log_dt,
               s5_b_re=s5_b_re, s5_b_im=s5_b_im, s5_c_re=s5_c_re, s5_c_im=s5_c_im, s5_d=s5_d,
               s5_w_glu=s5_w_glu, ml_gate_bias=ml_gate_bias, ml_norm=ml_norm, gla_w_alpha=gla_w_alpha,
               gla_b_alpha=gla_b_alpha, gla_norm=gla_norm, rw_mu=rw_mu, rw_w0=rw_w0, rw_w2=rw_w2,
               rw_a0=rw_a0, rw_a2=rw_a2, rw_g2=rw_g2, rw_k_k=rw_k_k, rw_k_a=rw_k_a, rw_r_k=rw_r_k,
               rw_norm=rw_norm, w_out=w_out, norm_ffn=norm_ffn, ffn_w_up=ffn_w_up, ffn_conv_w=ffn_conv_w,
               ffn_conv_b=ffn_conv_b, ffn_w_down=ffn_w_down)
    depth = w_in.shape[0]
    n_seq, t_len, d = x_prompt.shape
    n_smp = x_sample.shape[0]
    assert t_len % 1024 == 0 and n_seq % SEQS_PER_STEP == 0 and x_sample.shape[1] == 1
    nf = norm_final[None, :]
    small_states = (state_s5_re, state_s5_im, state_mlstm_m, state_rwkv_shift, state_ffn_conv)
    lead = depth * N_HEADS
    to_lanes = lambda st: jnp.moveaxis(st, 1, -1)
    pair_states = (to_lanes(state_mlstm_C).reshape(lead, HEAD_DIM, HEAD_DIM, n_smp),
                   to_lanes(state_mlstm_n).reshape(lead, HEAD_DIM, n_smp),
                   to_lanes(state_gla_S).reshape(lead, HEAD_DIM, HEAD_DIM, n_smp),
                   to_lanes(state_rwkv_S).reshape(lead, HEAD_DIM, HEAD_DIM, n_smp))

    xp = x_prompt.reshape(n_seq * t_len, d)
    xs = x_sample.reshape(n_smp, d)
    p_states, s_states = [], []
    for l in range(depth):
        p = _layer_params(l, prm)
        final = l == depth - 1
        xp, ps = _prompt_layer(xp, p, nf, n_seq, t_len, final)
        xs, ss = _sample_layer(xs, tuple(s[l] for s in small_states), pair_states, l, p, nf, final)
        p_states.append(ps)
        s_states.append(ss)
    new_p = tuple(jnp.stack([st[i] for st in p_states]) for i in range(9))
    new_s = tuple(jnp.stack([st[i] for st in s_states]) for i in range(9))
    return (xp.reshape(n_seq, t_len, d), xs.reshape(n_smp, 1, d)) + new_p + new_s
```

```python
import functools
import math

import jax
import jax.numpy as jnp
from jax import lax
from jax.experimental import pallas as pl
from jax.experimental.pallas import tpu as pltpu

F32 = jnp.float32
BF16 = jnp.bfloat16

LANES = 128
W_GROUP = 256
HEAD_DIM = 64
HEAD_SHIFT = 6
N_HEADS = 4
S5_CH = 16
S5_GROUPS = 16
S5_STATE = 64
S5_WIDTH = S5_GROUPS * S5_STATE
GLA_RANK = 16
GLA_GATE_TEMP = 16.0
RWKV_W_RANK = 32
RWKV_A_RANK = 32
RWKV_G_RANK = 64
RWKV_COLS = 3 * W_GROUP + RWKV_W_RANK + RWKV_A_RANK + RWKV_G_RANK
RWKV_DECAY_SCALE = 0.6065306597126334
CONV_W = 3
EPS = 1e-6
GN_EPS = 64e-5
NEG_BIG = -1e30
EXP_CLAMP = 80.0

SMALL_W = LANES
OFF_SMALL = RWKV_COLS
OFF_MAIN = RWKV_COLS + SMALL_W
D_PROJ = OFF_MAIN + 9 * W_GROUP
LANE_MI = GLA_RANK
LANE_MF = GLA_RANK + N_HEADS
(BLK_U, BLK_MQ, BLK_MK, BLK_MV, BLK_MO, BLK_GQ, BLK_GK, BLK_GV, BLK_GG) = range(
    OFF_MAIN // W_GROUP, OFF_MAIN // W_GROUP + 9)
BLK_SMALL = OFF_SMALL // SMALL_W

CHUNK_ML = 128
CHUNK_RW = 64
SEQS_PER_STEP = 8
SEQS_PER_STEP_ML = 8
VMEM_LIMIT = 56 * 1024 * 1024


def _cparams(*sem):
    return pltpu.CompilerParams(dimension_semantics=sem, vmem_limit_bytes=VMEM_LIMIT)


class _Param:
    def __init__(self, arr, *lead):
        self.arr, self.lead = arr, tuple(lead)

    @property
    def shape(self):
        return tuple(self.arr.shape[len(self.lead):])

    def spec(self, **kw):
        lead, tail = self.lead, (0,) * len(self.shape)
        return pl.BlockSpec((None,) * len(lead) + self.shape, lambda *grid_idx: lead + tail, **kw)


def _dot(a, b):
    return jnp.dot(a.astype(BF16), b.astype(BF16), preferred_element_type=F32)


def _dot_nt(a, b):
    return lax.dot_general(a.astype(BF16), b.astype(BF16), (((1,), (1,)), ((), ())),
                           preferred_element_type=F32)


def _dot_tn(a, b):
    return lax.dot_general(a.astype(BF16), b.astype(BF16), (((0,), (0,)), ((), ())),
                           preferred_element_type=F32)


def _split2(x):
    hi = x.astype(BF16)
    lo = (x - hi.astype(F32)).astype(BF16)
    return hi, lo


def _dot_exact_rhs(a01, x):
    hi, lo = _split2(x)
    f = lambda p: jnp.dot(a01, p, preferred_element_type=F32)
    return f(hi) + f(lo)


def _dot_exact_lhs(x, b01):
    hi, lo = _split2(x)
    f = lambda p: jnp.dot(p, b01, preferred_element_type=F32)
    return f(hi) + f(lo)


def _sigmoid(x):
    return 1.0 / (1.0 + jnp.exp(-x))


def _log_sigmoid(x):
    return jnp.minimum(x, 0.0) - jnp.log(1.0 + jnp.exp(-jnp.abs(x)))


def _gelu(x):
    return 0.5 * x * (1.0 + jnp.tanh(math.sqrt(2.0 / math.pi) * (x + 0.044715 * (x * x * x))))


def _iota(shape, dim):
    return lax.broadcasted_iota(jnp.int32, shape, dim)


def _head_ones():
    r = _iota((W_GROUP, W_GROUP), 0) >> HEAD_SHIFT
    c = _iota((W_GROUP, W_GROUP), 1) >> HEAD_SHIFT
    return jnp.where(r == c, 1.0, 0.0).astype(BF16)


def _head_sum(x, ones):
    return _dot_exact_lhs(x, ones)


def _head_lane_mask(h, rows=1):
    lane = _iota((rows, W_GROUP), 1) >> HEAD_SHIFT
    return jnp.where(lane == h, 1.0, 0.0).astype(F32)


def _rmsnorm_rows(x, g):
    ms = jnp.mean(x * x, axis=-1, keepdims=True)
    return x * lax.rsqrt(ms + EPS) * g


def _tril_ones(n, strict=False):
    r = _iota((n, n), 0)
    c = _iota((n, n), 1)
    return jnp.where((r > c) if strict else (r >= c), 1.0, 0.0).astype(BF16)


def _triu_ones(n):
    r = _iota((n, n), 0)
    c = _iota((n, n), 1)
    return jnp.where(r <= c, 1.0, 0.0).astype(BF16)


def _store_head_blocks(out_ref, scr_ref, transpose=False):
    for g in range(scr_ref.shape[0]):
        m = scr_ref[g].T if transpose else scr_ref[g]
        for h in range(N_HEADS):
            lo, hi = h * HEAD_DIM, (h + 1) * HEAD_DIM
            out_ref[g, h] = m[lo:hi, lo:hi]


def _HEAD_BLOCKS_SPEC(G):
    return pl.BlockSpec((G, N_HEADS, HEAD_DIM, HEAD_DIM), lambda b, c: (b, 0, 0, 0))


def _inproj_kernel(x_ref, g_ref, w_ref, wg_ref, proj_ref, gt_ref):
    hn = _rmsnorm_rows(x_ref[...], g_ref[...]).astype(BF16)
    proj_ref[...] = jnp.dot(hn, w_ref[...], preferred_element_type=F32)
    gt_ref[0] = lax.dot_general(wg_ref[...], hn, (((1,), (1,)), ((), ())),
                                preferred_element_type=F32)


def _inproj(x2d, g, w, wg, n_seq, t_len, tm):
    n, d = x2d.shape
    tiles_per_seq = t_len // tm
    return pl.pallas_call(
        _inproj_kernel,
        grid=(n // tm,),
        in_specs=[
            pl.BlockSpec((tm, d), lambda i: (i, 0)),
            g.spec(), w.spec(pipeline_mode=pl.Buffered(1)), wg.spec(),
        ],
        out_specs=[
            pl.BlockSpec((tm, D_PROJ), lambda i: (i, 0)),
            pl.BlockSpec((1, 8, tm), lambda i: (i // tiles_per_seq, 0, i % tiles_per_seq)),
        ],
        out_shape=[
            jax.ShapeDtypeStruct((n, D_PROJ), F32),
            jax.ShapeDtypeStruct((n_seq, 8, t_len), F32),
        ],
        compiler_params=_cparams("arbitrary"),
        name="inproj",
    )(x2d, g.arr, w.arr, wg.arr)


def _s5_kernel(u_ref, unext_ref, h0re_ref, h0im_ref, bmat_ref, lre_ref, lim_ref, cmat_ref, d_ref, wglu_ref,
               y_ref, hre_out, him_out, bu_scr, hs_scr, hre_scr, him_scr, *, bp, tt):
    i = pl.program_id(0)
    nt = S5_WIDTH // LANES

    def project(u_tile, slot):
        bu = jnp.dot(u_tile.astype(BF16), bmat_ref[...], preferred_element_type=F32)
        for j in range(2 * nt):
            bu_scr[slot, j] = bu[:, j * LANES:(j + 1) * LANES]

    @pl.when(i == 0)
    def _():
        hre_scr[...] = h0re_ref[...]
        him_scr[...] = h0im_ref[...]
        project(u_ref[...], 0)

    slot = lax.rem(i, 2)
    lre = [lre_ref[:, j * LANES:(j + 1) * LANES] for j in range(nt)]
    lim = [lim_ref[:, j * LANES:(j + 1) * LANES] for j in range(nt)]

    def step(t, carry):
        rows = pl.ds(pl.multiple_of(t * bp, bp), bp)
        new = []
        for j in range(nt):
            hre, him = carry[j], carry[nt + j]
            new.append((lre[j] * hre - lim[j] * him + bu_scr[slot, j, rows, :],
                        lre[j] * him + lim[j] * hre + bu_scr[slot, nt + j, rows, :]))
        out = tuple(x[0] for x in new) + tuple(x[1] for x in new)
        for j in range(2 * nt):
            hs_scr[j, rows, :] = out[j]
        return out

    carry = tuple(hre_scr[:, j * LANES:(j + 1) * LANES] for j in range(nt)) + tuple(
        him_scr[:, j * LANES:(j + 1) * LANES] for j in range(nt))
    carry = step(0, carry) if tt == 1 else lax.fori_loop(0, tt, step, carry)
    hre = jnp.concatenate(carry[0:nt], axis=1)
    him = jnp.concatenate(carry[nt:2 * nt], axis=1)
    hre_scr[...] = hre
    him_scr[...] = him
    hre_out[...] = hre
    him_out[...] = him

    project(unext_ref[...], lax.rem(i + 1, 2))
    u = u_ref[...]
    hs = jnp.concatenate([hs_scr[j] for j in range(2 * nt)], axis=1)
    y = jnp.dot(hs.astype(BF16), cmat_ref[...], preferred_element_type=F32) + d_ref[...] * u
    z = _gelu(y)
    y_ref[...] = (z * _sigmoid(_dot(z, wglu_ref[...]))).astype(y_ref.dtype)


def _s5(u_tb, h0re, h0im, sp, bp, t_len, tt):
    n = u_tb.shape[0]
    nsteps = t_len // tt
    full = lambda shape: pl.BlockSpec(shape, lambda i: (0,) * len(shape))
    tile = pl.BlockSpec((tt * bp, W_GROUP), lambda i: (i, 0))
    next_tile = pl.BlockSpec((tt * bp, W_GROUP), lambda i: (jnp.minimum(i + 1, nsteps - 1), 0))
    return pl.pallas_call(
        functools.partial(_s5_kernel, bp=bp, tt=tt),
        grid=(nsteps,),
        in_specs=[
            tile, next_tile,
            full((bp, S5_WIDTH)), full((bp, S5_WIDTH)),
            sp["bmat"].spec(), sp["lre"].spec(), sp["lim"].spec(), sp["cmat"].spec(), sp["d"].spec(),
            sp["wglu"].spec(),
        ],
        out_specs=[
            tile,
            full((bp, S5_WIDTH)), full((bp, S5_WIDTH)),
        ],
        out_shape=[
            jax.ShapeDtypeStruct((n, W_GROUP), BF16),
            jax.ShapeDtypeStruct((bp, S5_WIDTH), F32),
            jax.ShapeDtypeStruct((bp, S5_WIDTH), F32),
        ],
        scratch_shapes=[
            pltpu.VMEM((2, 2 * S5_WIDTH // LANES, tt * bp, LANES), F32),
            pltpu.VMEM((2 * S5_WIDTH // LANES, tt * bp, LANES), F32),
            pltpu.VMEM((bp, S5_WIDTH), F32),
            pltpu.VMEM((bp, S5_WIDTH), F32),
        ],
        compiler_params=_cparams("arbitrary"),
        name="s5_scan",
    )(u_tb, u_tb, h0re, h0im, sp["bmat"].arr, sp["lre"].arr, sp["lim"].arr, sp["cmat"].arr, sp["d"].arr,
      sp["wglu"].arr)


def _run_interleaved(chains):
    live = list(chains)
    while live:
        still = []
        for ch in live:
            try:
                next(ch)
                still.append(ch)
            except StopIteration:
                pass
        live = still


def _mlstm_kernel(q_ref, k_ref, v_ref, og_ref, sm_ref, gt_ref, brow_ref, bcol_ref, norm_ref,
                  y_ref, c_out, n_out, m_out, c_scr, n_scr, m_scr, *, L, nc, G):
    c = pl.program_id(1)

    @pl.when(c == 0)
    def _():
        c_scr[...] = jnp.zeros_like(c_scr)
        n_scr[...] = jnp.zeros_like(n_scr)
        m_scr[...] = jnp.zeros_like(m_scr)

    ones = _head_ones()
    tril = _tril_ones(L)
    triu = _triu_ones(L)
    l_shift = L.bit_length() - 1
    src = _iota((LANES, N_HEADS * L), 0)
    to_scores = jnp.where(src == LANE_MI + (_iota((LANES, N_HEADS * L), 1) >> l_shift), 1.0, 0.0).astype(BF16)
    src = _iota((LANES, W_GROUP), 0)
    to_feats = jnp.where(src == LANE_MI + (_iota((LANES, W_GROUP), 1) >> HEAD_SHIFT), 1.0, 0.0).astype(BF16)
    seg = _iota((N_HEADS * L, W_GROUP), 0) >> l_shift
    score_sum = jnp.where(seg == (_iota((N_HEADS * L, W_GROUP), 1) >> HEAD_SHIFT), 1.0, 0.0).astype(BF16)
    causal = _iota((L, N_HEADS * L), 0) >= (_iota((L, N_HEADS * L), 1) & (L - 1))
    t_idx = _iota((L, LANES), 0)

    seqs = list(range(G))
    cat = lambda xs: jnp.concatenate(xs, axis=0)
    split = lambda x: [x[i * L:(i + 1) * L] for i in seqs]

    def running_max(x):
        shift = 1
        while shift < L:
            x = jnp.maximum(x, jnp.where(t_idx >= shift, pltpu.roll(x, shift, 0), NEG_BIG))
            shift *= 2
        return x

    q = [q_ref[i] for i in seqs]
    k = [k_ref[i] * (HEAD_DIM ** -0.5) for i in seqs]
    v = [v_ref[i] for i in seqs]
    C = [c_scr[i] for i in seqs]
    n = [n_scr[i, 0:1, :] for i in seqs]
    m0 = [m_scr[i, 0:1, :] for i in seqs]
    pre = [sm_ref[i] + brow_ref[...] for i in seqs]
    gtb = [gt_ref[i] + bcol_ref[...] for i in seqs]
    bc_col = _each(lambda x: _dot_exact_rhs(tril, _log_sigmoid(x)), pre)
    bc_rows = _dot_exact_lhs(_log_sigmoid(cat(gtb)), triu)
    qC = _each(_dot_nt, q, C)
    qk = _each(lambda q_, k_: _dot_nt(q_, _stack_heads(k_)), q, k)
    qn = _head_sum(cat(_each(lambda q_, n_: q_ * n_, q, n)), ones)

    b_col = _each(lambda x: pltpu.roll(x, LANES - N_HEADS, 1), bc_col)
    r_col = _each(lambda p_, b_: p_ - b_, pre, b_col)
    mu = _each(lambda r_, m_: jnp.maximum(running_max(r_), m_), r_col, m0)
    mu_last = _each(lambda x: x[L - 1:L, :], mu)
    mu_scores = split(_dot_exact_lhs(cat(mu), to_scores))

    def weights(i):
        r_rows = gtb[i][0:N_HEADS, :] - bc_rows[8 * i + N_HEADS:8 * i + 2 * N_HEADS, :]
        r_all = jnp.concatenate([r_rows[h:h + 1, :] for h in range(N_HEADS)], axis=1)
        return jnp.exp(jnp.where(causal, r_all - mu_scores[i], NEG_BIG)) * qk[i]

    ws = [weights(i) for i in seqs]
    wsv = _each(lambda w_, v_: _dot(w_, _stack_heads(v_)), ws, v)
    w0 = _dot(cat(_each(lambda m_, u_: jnp.exp(m_ - u_), m0, mu)), to_feats)
    m_row = _dot_exact_lhs(cat(_each(lambda b_, u_: b_ + u_, b_col, mu)), to_feats)
    den = w0 * qn + _dot(cat(ws), score_sum)
    hc = (w0 * cat(qC) + cat(wsv)) / jnp.maximum(jnp.abs(den), jnp.exp(-m_row))

    wl = split(_dot(cat(_each(lambda r_, u_: jnp.exp(r_ - u_), r_col, mu_last)), to_feats))
    w0l_rows = _each(lambda m_, u_: jnp.exp(m_ - u_), m0, mu_last)
    w0l_rows = w0l_rows + [jnp.zeros((1, LANES), F32)] * (-G % 8)
    w0l = _dot_exact_lhs(cat(w0l_rows), to_feats)
    kw = _each(lambda k_, w_: k_ * w_, k, wl)
    upd = _each(_dot_tn, v, kw)
    for i in seqs:
        c_scr[i] = w0l[i:i + 1, :] * C[i] + ones.astype(F32) * upd[i]
        n_scr[i, 0:1, :] = w0l[i:i + 1, :] * n[i] + jnp.sum(kw[i], axis=0, keepdims=True)
        m_scr[i, 0:1, :] = b_col[i][L - 1:L, :] + mu_last[i]

    ms = _head_sum(hc * hc, ones) * (1.0 / HEAD_DIM)
    y = split(hc * lax.rsqrt(ms + EPS) * norm_ref[...])
    for i in seqs:
        y_ref[i] = (y[i] * _sigmoid(og_ref[i])).astype(y_ref.dtype)

    @pl.when(c == nc - 1)
    def _():
        _store_head_blocks(c_out, c_scr)
        n_out[...] = jnp.broadcast_to(n_scr[:, 0:1, :], n_out.shape)
        m_out[...] = m_scr[...]


def _seq_spec(G, L, blk, width=W_GROUP):
    return pl.BlockSpec((G, L, width), lambda b, c: (b, c, blk))


def _mlstm(proj, gt, p, n_seq, t_len):
    L = CHUNK_ML
    G = SEQS_PER_STEP_ML
    nc = t_len // L
    per_seq = lambda rows, width: pl.BlockSpec((G, rows, width), lambda b, c: (b, 0, 0))
    proj3 = proj.reshape(n_seq, t_len, D_PROJ)
    y, c_new, n_new, m_new = pl.pallas_call(
        functools.partial(_mlstm_kernel, L=L, nc=nc, G=G),
        grid=(n_seq // G, nc),
        in_specs=[
            _seq_spec(G, L, BLK_MQ), _seq_spec(G, L, BLK_MK), _seq_spec(G, L, BLK_MV),
            _seq_spec(G, L, BLK_MO), _seq_spec(G, L, BLK_SMALL, SMALL_W),
            pl.BlockSpec((G, 8, L), lambda b, c: (b, 0, c)),
            p["ml_brow"].spec(), p["ml_bcol"].spec(), p["ml_norm"].spec(),
        ],
        out_specs=[
            pl.BlockSpec((G, L, W_GROUP), lambda b, c: (b, c, 0)),
            _HEAD_BLOCKS_SPEC(G), per_seq(8, W_GROUP), per_seq(8, LANES),
        ],
        out_shape=[
            jax.ShapeDtypeStruct((n_seq, t_len, W_GROUP), BF16),
            jax.ShapeDtypeStruct((n_seq, N_HEADS, HEAD_DIM, HEAD_DIM), F32),
            jax.ShapeDtypeStruct((n_seq, 8, W_GROUP), F32),
            jax.ShapeDtypeStruct((n_seq, 8, LANES), F32),
        ],
        scratch_shapes=[
            pltpu.VMEM((G, W_GROUP, W_GROUP), F32),
            pltpu.VMEM((G, 8, W_GROUP), F32),
            pltpu.VMEM((G, 8, LANES), F32),
        ],
        compiler_params=_cparams("arbitrary", "arbitrary"),
        name="mlstm_chunk",
    )(proj3, proj3, proj3, proj3, proj3, gt, p["ml_brow"].arr, p["ml_bcol"].arr, p["ml_norm"].arr)
    return y.reshape(n_seq * t_len, W_GROUP), c_new, n_new, m_new


def _gla_kernel(q_ref, k_ref, v_ref, gg_ref, sm_ref, wa_ref, ba_ref, norm_ref,
                y_ref, s_out, s_scr, *, L, nc, G):
    c = pl.program_id(1)

    @pl.when(c == 0)
    def _():
        s_scr[...] = jnp.zeros_like(s_scr)

    ones = _head_ones()
    tril = _tril_ones(L)
    causal = _iota((L, N_HEADS * L), 0) >= (_iota((L, N_HEADS * L), 1) & (L - 1))

    def chain(gi):
        q = q_ref[gi] * (HEAD_DIM ** -0.5)
        k = k_ref[gi]
        v = v_ref[gi]
        la = _log_sigmoid(_dot(sm_ref[gi], wa_ref[...]) + ba_ref[...]) * (1.0 / GLA_GATE_TEMP)
        yield
        bc = _dot_exact_rhs(tril, la)
        yield
        qs = q * jnp.exp(bc)
        kh = k * jnp.exp(jnp.minimum(-bc, EXP_CLAMP))
        ST = s_scr[gi]
        att = jnp.where(causal, _dot_nt(qs, _stack_heads(kh)), 0.0)
        from_state = _dot_nt(qs, ST)
        yield
        o = from_state + _dot(att, _stack_heads(v))
        b_last = bc[L - 1:L, :]
        kbar = k * jnp.exp(b_last - bc)
        s_scr[gi] = ST * jnp.exp(b_last) + ones.astype(F32) * _dot_tn(v, kbar)
        yield
        ms = _head_sum(o * o, ones) * (1.0 / HEAD_DIM)
        gg = gg_ref[gi]
        y = o * lax.rsqrt(ms + EPS) * norm_ref[...] * (gg * _sigmoid(gg))
        y_ref[gi] = y.astype(y_ref.dtype)

    _run_interleaved([chain(gi) for gi in range(G)])

    @pl.when(c == nc - 1)
    def _():
        _store_head_blocks(s_out, s_scr, transpose=True)


def _gla(proj, p, n_seq, t_len):
    L = CHUNK_ML
    G = SEQS_PER_STEP_ML
    nc = t_len // L
    proj3 = proj.reshape(n_seq, t_len, D_PROJ)
    y, s_new = pl.pallas_call(
        functools.partial(_gla_kernel, L=L, nc=nc, G=G),
        grid=(n_seq // G, nc),
        in_specs=[
            _seq_spec(G, L, BLK_GQ), _seq_spec(G, L, BLK_GK), _seq_spec(G, L, BLK_GV),
            _seq_spec(G, L, BLK_GG), _seq_spec(G, L, BLK_SMALL, SMALL_W),
            p["gla_wa"].spec(), p["gla_ba"].spec(), p["gla_norm"].spec(),
        ],
        out_specs=[
            pl.BlockSpec((G, L, W_GROUP), lambda b, c: (b, c, 0)),
            _HEAD_BLOCKS_SPEC(G),
        ],
        out_shape=[
            jax.ShapeDtypeStruct((n_seq, t_len, W_GROUP), BF16),
            jax.ShapeDtypeStruct((n_seq, N_HEADS, HEAD_DIM, HEAD_DIM), F32),
        ],
        scratch_shapes=[pltpu.VMEM((G, W_GROUP, W_GROUP), F32)],
        compiler_params=_cparams("arbitrary", "arbitrary"),
        name="gla_chunk",
    )(proj3, proj3, proj3, proj3, proj3, p["gla_wa"].arr, p["gla_ba"].arr, p["gla_norm"].arr)
    return y.reshape(n_seq * t_len, W_GROUP), s_new


def _rwkv_vectors(rc, prev, p_mu, p_w0, p_a0, p_kk, p_ka, p_rk, w2, a2, g2, ones):
    xm = rc + p_mu * (prev - rc)
    rr = xm[:, 0:W_GROUP]
    rk = xm[:, W_GROUP:2 * W_GROUP]
    rv = xm[:, 2 * W_GROUP:3 * W_GROUP]
    tail = xm[:, 3 * W_GROUP:RWKV_COLS]
    lw = -RWKV_DECAY_SCALE * _sigmoid(p_w0 + _dot(jnp.tanh(tail), w2))
    a = _sigmoid(p_a0 + _dot(tail, a2))
    g = _dot(_sigmoid(tail), g2)
    kk = rk * p_kk
    kk = kk * lax.rsqrt(jnp.maximum(_head_sum(kk * kk, ones), 1e-24))
    kt = rk * (1.0 + (a - 1.0) * p_ka)
    bonus = _head_sum(rr * kt * p_rk, ones) * rv
    return rr, lw, kt, rv, kk, a, g, bonus


def _head_layernorm(o, g, ones):
    mu = _head_sum(o, ones) * (1.0 / HEAD_DIM)
    oc = o - mu
    var = _head_sum(oc * oc, ones) * (1.0 / HEAD_DIM)
    return oc * lax.rsqrt(var + GN_EPS) * g


def _stack_heads(x):
    xb = x.astype(BF16)
    lane_head = _iota((1, W_GROUP), 1) >> HEAD_SHIFT
    return jnp.concatenate([jnp.where(lane_head == h, xb, jnp.zeros_like(xb)) for h in range(N_HEADS)],
                           axis=0)


def _each(f, *seqs):
    return [f(*args) for args in zip(*seqs)]


def _block_mm(x_list, y_list):
    return _each(lambda x, y: _dot(x, _stack_heads(y)), x_list, y_list)


def _unit_lower_inverse(a_list, L):
    t_idx = _iota((L, N_HEADS * L), 0)
    s_idx = _iota((L, N_HEADS * L), 1) & (L - 1)
    eye = jnp.where(t_idx == s_idx, 1.0, 0.0).astype(F32)
    in16 = (t_idx >> 4) == (s_idx >> 4)
    in32 = ((t_idx >> 5) == (s_idx >> 5)) & ((t_idx >> 4) > (s_idx >> 4))
    in64 = (t_idx >> 5) > (s_idx >> 5)
    x16 = _each(lambda a: jnp.where(in16, -a, 0.0), a_list)
    x2 = _block_mm(x16, x16)
    x4 = _block_mm(x2, x2)
    x8 = _block_mm(x4, x4)
    t = _each(lambda x: eye + x, x16)
    for xp in (x2, x4, x8):
        t = _each(lambda t_, d: t_ + d, t, _block_mm(t, xp))
    for sel in (in32, in64):
        w = _block_mm(_each(lambda a: jnp.where(sel, a, 0.0), a_list), t)
        t = _each(lambda t_, d: t_ - d, t, _block_mm(t, w))
    return t


def _rwkv_chunks(rcs, prev_rows, states, prm, ones, L):
    (mu, w0, a0, pkk, pka, prk, w2, a2, g2, norm) = prm
    first_row = _iota((L, RWKV_COLS), 0) == 0
    prevs = _each(lambda rc, pr: jnp.where(first_row, pr, pltpu.roll(rc, 1, 0)), rcs, prev_rows)
    n_seq = len(rcs)
    stacked = _rwkv_vectors(jnp.concatenate(rcs, axis=0), jnp.concatenate(prevs, axis=0),
                            mu, w0, a0, pkk, pka, prk, w2, a2, g2, ones)
    r, lw, kx, v, kk, a, g, bonus = [[x[i * L:(i + 1) * L] for i in range(n_seq)] for x in stacked]

    tril = _tril_ones(L)
    lc = _each(lambda x: _dot_exact_rhs(tril, x), lw)
    akk = _each(lambda a_, k_: a_ * k_, a, kk)
    e_neg = _each(lambda x: jnp.exp(-x), lc)
    l_last = _each(lambda x: x[L - 1:L, :], lc)
    e_end = _each(lambda ll, x: jnp.exp(ll - x), l_last, lc)
    kr = _each(lambda k_, r_, c_, w_: jnp.concatenate([k_ * jnp.exp(c_ - w_), r_ * jnp.exp(c_)], axis=0),
               kk, r, lc, lw)
    a_hat = _each(lambda x, e: _stack_heads(x * e), akk, e_neg)
    k_hat = _each(lambda x, e: _stack_heads(x * e), kx, e_neg)
    v_bd = _each(_stack_heads, v)

    t_idx = _iota((L, N_HEADS * L), 0)
    s_idx = _iota((L, N_HEADS * L), 1) & (L - 1)
    strict = t_idx > s_idx
    incl = t_idx >= s_idx
    pa = _each(_dot_nt, kr, a_hat)
    pk = _each(_dot_nt, kr, k_hat)
    a_ua = _each(lambda p_: jnp.where(strict, p_[0:L], 0.0), pa)
    b_ra = _each(lambda p_: jnp.where(incl, p_[L:2 * L], 0.0), pa)
    ab_k = _each(lambda p_: jnp.concatenate([jnp.where(strict, p_[0:L], 0.0),
                                             jnp.where(incl, p_[L:2 * L], 0.0)], axis=0), pk)
    T = _unit_lower_inverse(a_ua, L)

    from_state = _each(_dot_nt, kr, states)
    from_v = _each(_dot, ab_k, v_bd)
    rhs = _each(lambda s_, v_: s_[0:L] + v_[0:L], from_state, from_v)
    U = _block_mm(T, rhs)
    corr = _block_mm(b_ra, U)
    o = _each(lambda s_, v_, c_: s_[L:2 * L] + v_[L:2 * L] - c_, from_state, from_v, corr)
    upd = _each(lambda v_, u_, k_, a_, e: _dot_tn(jnp.concatenate([v_, u_], axis=0),
                                                  jnp.concatenate([k_ * e, -(a_ * e)], axis=0)),
                v, U, kx, akk, e_end)
    s_new = _each(lambda s_, ll, d: s_ * jnp.exp(ll) + ones.astype(F32) * d, states, l_last, upd)
    y_all = ((_head_layernorm(jnp.concatenate(o, axis=0), norm, ones) + stacked[7]) * stacked[6])
    return [y_all[i * L:(i + 1) * L] for i in range(n_seq)], s_new


def _rwkv_kernel(rc_ref, mu_ref, w0_ref, a0_ref, kk_ref, ka_ref, rk_ref, w2_ref, a2_ref, g2_ref,
                 norm_ref, y_ref, s_out, shift_out, s_scr, prev_scr, *, L, nc, G):
    c = pl.program_id(1)

    @pl.when(c == 0)
    def _():
        s_scr[...] = jnp.zeros_like(s_scr)
        prev_scr[...] = jnp.zeros_like(prev_scr)

    ones = _head_ones()
    prm = (mu_ref[...], w0_ref[...], a0_ref[...], kk_ref[...], ka_ref[...], rk_ref[...],
           w2_ref[...], a2_ref[...], g2_ref[...], norm_ref[...])
    rcs = [rc_ref[gi] for gi in range(G)]
    ys, s_new = _rwkv_chunks(rcs, [prev_scr[gi, 0:1, :] for gi in range(G)],
                             [s_scr[gi] for gi in range(G)], prm, ones, L)
    for gi in range(G):
        prev_scr[gi, 0:1, :] = rcs[gi][L - 1:L, :]
        s_scr[gi] = s_new[gi]
        y_ref[gi] = ys[gi].astype(y_ref.dtype)

    @pl.when(c == nc - 1)
    def _():
        _store_head_blocks(s_out, s_scr)
        shift_out[...] = jnp.broadcast_to(prev_scr[:, 0:1, :], shift_out.shape)


def _rwkv(proj, p, n_seq, t_len):
    L = CHUNK_RW
    G = SEQS_PER_STEP
    nc = t_len // L
    names = ("rw_mu", "rw_w0", "rw_a0", "rw_kk", "rw_ka", "rw_rk", "rw_w2", "rw_a2", "rw_g2", "rw_norm")
    y, s_new, shift = pl.pallas_call(
        functools.partial(_rwkv_kernel, L=L, nc=nc, G=G),
        grid=(n_seq // G, nc),
        in_specs=[
            pl.BlockSpec((G, L, RWKV_COLS), lambda b, c: (b, c, 0)),
        ] + [p[k].spec() for k in names],
        out_specs=[
            pl.BlockSpec((G, L, W_GROUP), lambda b, c: (b, c, 0)),
            _HEAD_BLOCKS_SPEC(G),
            pl.BlockSpec((G, 8, RWKV_COLS), lambda b, c: (b, 0, 0)),
        ],
        out_shape=[
            jax.ShapeDtypeStruct((n_seq, t_len, W_GROUP), BF16),
            jax.ShapeDtypeStruct((n_seq, N_HEADS, HEAD_DIM, HEAD_DIM), F32),
            jax.ShapeDtypeStruct((n_seq, 8, RWKV_COLS), F32),
        ],
        scratch_shapes=[
            pltpu.VMEM((G, W_GROUP, W_GROUP), F32),
            pltpu.VMEM((G, 8, RWKV_COLS), F32),
        ],
        compiler_params=_cparams("arbitrary", "arbitrary"),
        name="rwkv_chunk",
    )(proj.reshape(n_seq, t_len, D_PROJ), *[p[k].arr for k in names])
    return y.reshape(n_seq * t_len, W_GROUP), s_new, shift


FF_CHUNK = 256


def _mix_residual(x_ref, ys, wout_ref):
    acc = x_ref[...]
    for j, y in enumerate(ys):
        acc = acc + jnp.dot(y.astype(BF16), wout_ref[j * W_GROUP:(j + 1) * W_GROUP, :],
                            preferred_element_type=F32)
    return acc


def _ffn_body(x1, nrm_ref, wup_ref, cw_ref, cb_ref, wdn_ref, prev_rows, d_ff, act_scr):
    h2 = _rmsnorm_rows(x1, nrm_ref[...]).astype(BF16)
    for j in range(d_ff // FF_CHUNK):
        lo, hi = j * FF_CHUNK, (j + 1) * FF_CHUNK
        ug = jnp.dot(h2, wup_ref[:, lo:hi], preferred_element_type=F32)
        uv = jnp.dot(h2, wup_ref[:, d_ff + lo:d_ff + hi], preferred_element_type=F32)
        p2, p1 = prev_rows(j, ug)
        conv = (cb_ref[:, lo:hi] + cw_ref[0:1, lo:hi] * p2 + cw_ref[1:2, lo:hi] * p1
                + cw_ref[2:3, lo:hi] * ug)
        act_scr[:, lo:hi] = (_gelu(conv) * uv).astype(BF16)
    return x1 + jnp.dot(act_scr[...], wdn_ref[...], preferred_element_type=F32)


def _ffn_seq_kernel(x_ref, y0_ref, y1_ref, y2_ref, y3_ref, wout_ref, nrm_ref, wup_ref, cw_ref, cb_ref,
                    wdn_ref, nf_ref, o_ref, buf_out, carry_scr, act_scr, *, tm, d_ff, final):
    t = pl.program_id(1)

    @pl.when(t == 0)
    def _():
        carry_scr[...] = jnp.zeros_like(carry_scr)

    x1 = _mix_residual(x_ref, (y0_ref[...], y1_ref[...], y2_ref[...], y3_ref[...]), wout_ref)
    row = _iota((tm, FF_CHUNK), 0)

    def prev_rows(j, ug):
        lo, hi = j * FF_CHUNK, (j + 1) * FF_CHUNK
        c0 = carry_scr[0:1, lo:hi]
        c1 = carry_scr[1:2, lo:hi]
        p1 = jnp.where(row == 0, c1, pltpu.roll(ug, 1, 0))
        p2 = jnp.where(row == 0, c0, jnp.where(row == 1, c1, pltpu.roll(ug, 2, 0)))
        carry_scr[0:2, lo:hi] = ug[tm - 2:tm, :]
        return p2, p1

    out = _ffn_body(x1, nrm_ref, wup_ref, cw_ref, cb_ref, wdn_ref, prev_rows, d_ff, act_scr)
    if final:
        out = _rmsnorm_rows(out, nf_ref[...])
    o_ref[...] = out
    buf_out[0] = carry_scr[...]


def _ffn_step_kernel(x_ref, y0_ref, y1_ref, y2_ref, y3_ref, wout_ref, nrm_ref, wup_ref, cw_ref, cb_ref,
                     wdn_ref, nf_ref, p2_ref, p1_ref, o_ref, ug_out, act_scr, *, d_ff, final):
    x1 = _mix_residual(x_ref, (y0_ref[...], y1_ref[...], y2_ref[...], y3_ref[...]), wout_ref)

    def prev_rows(j, ug):
        lo, hi = j * FF_CHUNK, (j + 1) * FF_CHUNK
        ug_out[:, lo:hi] = ug
        return p2_ref[:, lo:hi], p1_ref[:, lo:hi]

    out = _ffn_body(x1, nrm_ref, wup_ref, cw_ref, cb_ref, wdn_ref, prev_rows, d_ff, act_scr)
    if final:
        out = _rmsnorm_rows(out, nf_ref[...])
    o_ref[...] = out


def _ffn_seq(x2d, ys, p, norm_final, n_seq, t_len, tm, final):
    n, d = x2d.shape
    d_ff = p["ffn_wdn"].shape[0]
    nt = t_len // tm
    rows = lambda w: pl.BlockSpec((tm, w), lambda b, t: (b * nt + t, 0))
    once = dict(pipeline_mode=pl.Buffered(1))
    return pl.pallas_call(
        functools.partial(_ffn_seq_kernel, tm=tm, d_ff=d_ff, final=final),
        grid=(n_seq, nt),
        in_specs=[rows(d)] + [rows(W_GROUP)] * 4 + [
            p["w_out"].spec(**once), p["norm_ffn"].spec(), p["ffn_wup"].spec(**once), p["ffn_cw"].spec(),
            p["ffn_cb"].spec(), p["ffn_wdn"].spec(**once), norm_final.spec()],
        out_specs=[rows(d), pl.BlockSpec((1, CONV_W - 1, d_ff), lambda b, t: (b, 0, 0))],
        out_shape=[jax.ShapeDtypeStruct((n, d), F32),
                   jax.ShapeDtypeStruct((n_seq, CONV_W - 1, d_ff), F32)],
        scratch_shapes=[pltpu.VMEM((CONV_W - 1, d_ff), F32), pltpu.VMEM((tm, d_ff), BF16)],
        compiler_params=_cparams("arbitrary", "arbitrary"),
        name="wout_ffn_seq",
    )(x2d, *ys, p["w_out"].arr, p["norm_ffn"].arr, p["ffn_wup"].arr, p["ffn_cw"].arr, p["ffn_cb"].arr,
      p["ffn_wdn"].arr, norm_final.arr)


def _ffn_step(x2d, ys, p, norm_final, prev2, prev1, final):
    n, d = x2d.shape
    d_ff = p["ffn_wdn"].shape[0]
    full = lambda shape: pl.BlockSpec(shape, lambda i: (0,) * len(shape))
    return pl.pallas_call(
        functools.partial(_ffn_step_kernel, d_ff=d_ff, final=final),
        grid=(1,),
        in_specs=[full((n, d))] + [full((n, W_GROUP))] * 4 + [
            p["w_out"].spec(), p["norm_ffn"].spec(), p["ffn_wup"].spec(), p["ffn_cw"].spec(),
            p["ffn_cb"].spec(), p["ffn_wdn"].spec(), norm_final.spec(), full((n, d_ff)), full((n, d_ff))],
        out_specs=[full((n, d)), full((n, d_ff))],
        out_shape=[jax.ShapeDtypeStruct((n, d), F32), jax.ShapeDtypeStruct((n, d_ff), F32)],
        scratch_shapes=[pltpu.VMEM((n, d_ff), BF16)],
        compiler_params=_cparams("arbitrary"),
        name="wout_ffn_step",
    )(x2d, *ys, p["w_out"].arr, p["norm_ffn"].arr, p["ffn_wup"].arr, p["ffn_cw"].arr, p["ffn_cb"].arr,
      p["ffn_wdn"].arr, norm_final.arr, prev2, prev1)


def _step_vec_kernel(proj_ref, shift_ref, m_ref, brow_ref, wa_ref, ba_ref, mu_ref, w0_ref, a0_ref,
                     kk_ref, ka_ref, rk_ref, w2_ref, a2_ref, g2_ref,
                     pack_out, m_out, g_out, bonus_out):
    ones = _head_ones()
    sm = proj_ref[:, OFF_SMALL:OFF_SMALL + SMALL_W]
    pre = sm + brow_ref[...]
    logf = pltpu.roll(_log_sigmoid(pre), LANES - N_HEADS, 1)
    m_old = m_ref[...]
    m_new = jnp.maximum(logf + m_old, pre)
    m_out[...] = m_new
    lane = _iota((SMALL_W, W_GROUP), 0)
    spread_mat = jnp.where(lane == LANE_MI + (_iota((SMALL_W, W_GROUP), 1) >> HEAD_SHIFT), 1.0, 0.0).astype(BF16)
    gate_lanes = (_iota(pre.shape, 1) >= LANE_MI) & (_iota(pre.shape, 1) < LANE_MI + N_HEADS)
    spread = lambda x: _dot_exact_lhs(jnp.where(gate_lanes, x, 0.0), spread_mat)
    la = _log_sigmoid(_dot(sm, wa_ref[...]) + ba_ref[...]) * (1.0 / GLA_GATE_TEMP)
    rc = proj_ref[:, 0:RWKV_COLS]
    rr, lw, kt, rv, kk, a, g, bonus = _rwkv_vectors(
        rc, shift_ref[...], mu_ref[...], w0_ref[...], a0_ref[...], kk_ref[...], ka_ref[...],
        rk_ref[...], w2_ref[...], a2_ref[...], g2_ref[...], ones)
    main = lambda blk: proj_ref[:, blk * W_GROUP:(blk + 1) * W_GROUP]
    rows = {
        ROW_MQ: main(BLK_MQ), ROW_MK: main(BLK_MK), ROW_MV: main(BLK_MV),
        ROW_GQ: main(BLK_GQ), ROW_GK: main(BLK_GK), ROW_GV: main(BLK_GV), ROW_GA: jnp.exp(la),
        ROW_KK: kk, ROW_AKK: a * kk, ROW_W: jnp.exp(lw), ROW_KT: kt, ROW_RR: rr, ROW_RV: rv,
        ROW_IW: spread(jnp.exp(pre - m_new)), ROW_FW: spread(jnp.exp(logf + m_old - m_new)),
        ROW_M: spread(m_new),
    }
    for i, val in rows.items():
        pack_out[i] = val.T
    g_out[...] = g
    bonus_out[...] = bonus


def _step_vectors(proj, shift, m_pad, p):
    n = proj.shape[0]
    full = lambda a: pl.BlockSpec(a.shape, lambda i: (0,) * a.ndim)
    data = (proj, shift, m_pad)
    prms = [p[k] for k in ("ml_brow", "gla_wa", "gla_ba", "rw_mu", "rw_w0", "rw_a0", "rw_kk", "rw_ka",
                           "rw_rk", "rw_w2", "rw_a2", "rw_g2")]
    wide = jax.ShapeDtypeStruct((n, W_GROUP), F32)
    outs = [jax.ShapeDtypeStruct((N_PACK, W_GROUP, n), F32), jax.ShapeDtypeStruct((n, SMALL_W), F32),
            wide, wide]
    return pl.pallas_call(
        _step_vec_kernel,
        grid=(1,),
        in_specs=[full(a) for a in data] + [q.spec() for q in prms],
        out_specs=[pl.BlockSpec(o.shape, lambda i, nd=len(o.shape): (0,) * nd) for o in outs],
        out_shape=outs,
        compiler_params=_cparams("arbitrary"),
        name="step_vectors",
    )(*data, *[q.arr for q in prms])


(ROW_MQ, ROW_MK, ROW_MV, ROW_GQ, ROW_GK, ROW_GV, ROW_GA, ROW_KK, ROW_AKK, ROW_W, ROW_KT, ROW_RR, ROW_RV,
 ROW_IW, ROW_FW, ROW_M) = range(16)
N_PACK = 16
(OROW_ML, OROW_GLA, OROW_RW, OROW_N) = range(4)
N_OPACK = 4
STEP_UNROLL = 4


def _heads_step_kernel(pack_ref, c_ref, n_ref, gs_ref, rs_ref, c_out, gs_out, rs_out, opack_out):
    vec = lambda i: pack_ref[i]
    one = lambda i, j: pack_ref[i, pl.ds(j, 1), :]
    colsum = lambda x: jnp.sum(x, axis=0, keepdims=True)

    k_ml = vec(ROW_MK) * (HEAD_DIM ** -0.5)
    q_ml = vec(ROW_MQ)
    iw = pack_ref[ROW_IW, 0:1, :]
    fw = pack_ref[ROW_FW, 0:1, :]
    n_new = fw * n_ref[...] + iw * k_ml
    opack_out[OROW_N] = n_new
    r_den = 1.0 / jnp.maximum(jnp.abs(colsum(n_new * q_ml)), jnp.exp(-pack_ref[ROW_M, 0:1, :]))

    def ml_body(v, carry):
        c_new = fw * c_ref[v] + (iw * one(ROW_MV, v)) * k_ml
        c_out[v] = c_new
        opack_out[OROW_ML, pl.ds(v, 1), :] = colsum(c_new * q_ml) * r_den
        return carry

    lax.fori_loop(0, HEAD_DIM, ml_body, 0, unroll=STEP_UNROLL)

    v_gla = vec(ROW_GV)

    def gla_body(k, acc):
        s_new = one(ROW_GA, k) * gs_ref[k] + one(ROW_GK, k) * v_gla
        gs_out[k] = s_new
        return acc + (one(ROW_GQ, k) * (HEAD_DIM ** -0.5)) * s_new

    opack_out[OROW_GLA] = lax.fori_loop(0, HEAD_DIM, gla_body, jnp.zeros_like(v_gla), unroll=STEP_UNROLL)

    kk, akk, w, kt, rr = vec(ROW_KK), vec(ROW_AKK), vec(ROW_W), vec(ROW_KT), vec(ROW_RR)

    def rw_body(v, carry):
        s = rs_ref[v]
        s_new = s * w - colsum(s * kk) * akk + one(ROW_RV, v) * kt
        rs_out[v] = s_new
        opack_out[OROW_RW, pl.ds(v, 1), :] = colsum(s_new * rr)
        return carry

    lax.fori_loop(0, HEAD_DIM, rw_body, 0, unroll=STEP_UNROLL)


def _heads_step(pack, ml_c, ml_n, gla_s, rw_s, layer):
    n = pack.shape[-1]
    sq = (HEAD_DIM, HEAD_DIM, n)
    head = lambda lead: pl.BlockSpec((lead, HEAD_DIM, n), lambda h: (0, h, 0))
    at_layer = lambda shape: pl.BlockSpec((None,) + shape, lambda h: (layer * N_HEADS + h,) + (0,) * len(shape))
    out_state = pl.BlockSpec((None,) + sq, lambda h: (h, 0, 0, 0))
    state = jax.ShapeDtypeStruct((N_HEADS,) + sq, F32)
    return pl.pallas_call(
        _heads_step_kernel,
        grid=(N_HEADS,),
        in_specs=[head(N_PACK), at_layer(sq), at_layer((HEAD_DIM, n)), at_layer(sq), at_layer(sq)],
        out_specs=[out_state, out_state, out_state, head(N_OPACK)],
        out_shape=[state, state, state, jax.ShapeDtypeStruct((N_OPACK, W_GROUP, n), F32)],
        compiler_params=_cparams("arbitrary"),
        name="heads_step",
    )(pack, ml_c, ml_n, gla_s, rw_s)


def _step_post_kernel(hml_ref, mo_ref, ogla_ref, gg_ref, orw_ref, g_ref, bonus_ref,
                      mln_ref, gln_ref, rwn_ref, yml_out, ygla_out, yrw_out):
    ones = _head_ones()
    hml = hml_ref[...].T
    ms = _head_sum(hml * hml, ones) * (1.0 / HEAD_DIM)
    yml_out[...] = (hml * lax.rsqrt(ms + EPS) * mln_ref[...] * _sigmoid(mo_ref[...])).astype(BF16)
    og = ogla_ref[...].T
    ms = _head_sum(og * og, ones) * (1.0 / HEAD_DIM)
    gg = gg_ref[...]
    ygla_out[...] = (og * lax.rsqrt(ms + EPS) * gln_ref[...] * (gg * _sigmoid(gg))).astype(BF16)
    yrw_out[...] = ((_head_layernorm(orw_ref[...].T, rwn_ref[...], ones) + bonus_ref[...])
                    * g_ref[...]).astype(BF16)


def _step_post(hml, mo, ogla, gg, orw, g, bonus, p):
    n = mo.shape[0]
    data = (hml, mo, ogla, gg, orw, g, bonus)
    prms = [p[k] for k in ("ml_norm", "gla_norm", "rw_norm")]
    full = lambda a: pl.BlockSpec(a.shape, lambda i: (0,) * a.ndim)
    out = jax.ShapeDtypeStruct((n, W_GROUP), BF16)
    return pl.pallas_call(
        _step_post_kernel,
        grid=(1,),
        in_specs=[full(a) for a in data] + [q.spec() for q in prms],
        out_specs=[pl.BlockSpec((n, W_GROUP), lambda i: (0, 0))] * 3,
        out_shape=[out] * 3,
        compiler_params=_cparams("arbitrary"),
        name="step_post",
    )(*data, *[q.arr for q in prms])


def _prepare_params(prm):
    P = {}
    w_in = prm["w_in"]
    depth, d = w_in.shape[0], w_in.shape[1]
    sizes = (W_GROUP, W_GROUP, W_GROUP, W_GROUP, N_HEADS, N_HEADS, W_GROUP, W_GROUP, W_GROUP, W_GROUP,
             GLA_RANK, W_GROUP, RWKV_COLS)
    offs = [0]
    for sz in sizes:
        offs.append(offs[-1] + sz)
    col = lambda i: w_in[:, :, offs[i]:offs[i + 1]]
    (u, mq, mk, mv, mi, mf, mo, gq, gk, gv, ga, gg, rcols) = [col(i) for i in range(13)]
    small = jnp.concatenate([ga, mi, mf, jnp.zeros((depth, d, SMALL_W - GLA_RANK - 2 * N_HEADS), F32)], axis=2)
    P["w_in"] = jnp.concatenate([rcols, small, u, mq, mk, mv, mo, gq, gk, gv, gg], axis=2).astype(BF16)
    P["w_gate_t"] = jnp.swapaxes(jnp.concatenate([mi, mf], axis=2), 1, 2).astype(BF16)

    lam = lax.complex(prm["s5_lam_re"], prm["s5_lam_im"])
    dt = jnp.exp(prm["s5_log_dt"])[..., None]
    lam_bar = jnp.exp(lam * dt)
    b_bar = ((lam_bar - 1.0) / lam)[..., None] * lax.complex(prm["s5_b_re"], prm["s5_b_im"])
    eye = jnp.eye(S5_GROUPS, dtype=F32)
    bm = lambda b: jnp.einsum("lgph,gk->lghkp", b, eye).reshape(depth, W_GROUP, S5_WIDTH)
    P["bmat"] = jnp.concatenate([bm(b_bar.real), bm(b_bar.imag)], axis=2).astype(BF16)
    cm = lambda c: jnp.einsum("lghp,gk->lkpgh", c, eye).reshape(depth, S5_WIDTH, W_GROUP)
    P["cmat"] = jnp.concatenate([cm(prm["s5_c_re"]), -cm(prm["s5_c_im"])], axis=1).astype(BF16)
    P["lam"] = jnp.stack([lam_bar.real.reshape(depth, S5_WIDTH), lam_bar.imag.reshape(depth, S5_WIDTH)],
                         axis=1)[:, :, None, :]
    P["wglu"] = prm["s5_w_glu"].astype(BF16)

    bias = prm["ml_gate_bias"]
    P["ml_brow"] = jnp.pad(bias, ((0, 0), (LANE_MI, SMALL_W - LANE_MI - 2 * N_HEADS)))[:, None, :]
    P["ml_bcol"] = jnp.broadcast_to(bias[:, :, None], (depth, 2 * N_HEADS, CHUNK_ML))

    P["rows"] = jnp.stack([prm["s5_d"], prm["ml_norm"], prm["gla_b_alpha"], prm["gla_norm"], prm["rw_w0"],
                           prm["rw_a0"], prm["rw_k_k"], prm["rw_k_a"], prm["rw_r_k"].reshape(depth, W_GROUP),
                           prm["rw_norm"]], axis=1)[:, :, None, :]
    pad_rows = lambda w, lo: jnp.pad(w, ((0, 0), (lo, LANES - lo - w.shape[1]), (0, 0)))
    P["pads"] = jnp.stack([pad_rows(prm["gla_w_alpha"], 0), pad_rows(prm["rw_w2"], 0),
                           pad_rows(prm["rw_a2"], RWKV_W_RANK),
                           pad_rows(prm["rw_g2"], RWKV_W_RANK + RWKV_A_RANK)], axis=1).astype(BF16)
    P["rw_mu"] = prm["rw_mu"][:, None, :]
    P["norms"] = jnp.stack([prm["norm_mix"], prm["norm_ffn"]], axis=1)[:, :, None, :]
    P["w_out"] = prm["w_out"].astype(BF16)
    P["ffn_wup"] = prm["ffn_w_up"].astype(BF16)
    P["ffn_cw"] = prm["ffn_conv_w"]
    P["ffn_cb"] = prm["ffn_conv_b"][:, None, :]
    P["ffn_wdn"] = prm["ffn_w_down"].astype(BF16)
    return P


ROW_NAMES = ("d", "ml_norm", "gla_ba", "gla_norm", "rw_w0", "rw_a0", "rw_kk", "rw_ka", "rw_rk", "rw_norm")
PAD_NAMES = ("gla_wa", "rw_w2", "rw_a2", "rw_g2")


def _layer_params(P, l):
    p = {k: _Param(P[k], l) for k in ("w_in", "w_gate_t", "bmat", "cmat", "wglu", "ml_brow", "ml_bcol",
                                      "rw_mu", "w_out", "ffn_wup", "ffn_cw", "ffn_cb", "ffn_wdn")}
    p.update({k: _Param(P["rows"], l, i) for i, k in enumerate(ROW_NAMES)})
    p.update({k: _Param(P["pads"], l, i) for i, k in enumerate(PAD_NAMES)})
    p["lre"], p["lim"] = _Param(P["lam"], l, 0), _Param(P["lam"], l, 1)
    p["norm_mix"], p["norm_ffn"] = _Param(P["norms"], l, 0), _Param(P["norms"], l, 1)
    return p


def _prompt_layer(x2d, p, norm_final, n_seq, t_len, final):
    proj, gt = _inproj(x2d, p["norm_mix"], p["w_in"], p["w_gate_t"], n_seq, t_len, tm=1024)

    u_tb = proj[:, OFF_MAIN:OFF_MAIN + W_GROUP].reshape(n_seq, t_len, W_GROUP).transpose(1, 0, 2)
    zeros = jnp.zeros((n_seq, S5_WIDTH), F32)
    y_s5_tb, s5_re, s5_im = _s5(u_tb.reshape(t_len * n_seq, W_GROUP), zeros, zeros, p, n_seq, t_len, tt=128)
    y_s5 = y_s5_tb.reshape(t_len, n_seq, W_GROUP).transpose(1, 0, 2).reshape(n_seq * t_len, W_GROUP)

    y_ml, ml_c, ml_n, ml_m = _mlstm(proj, gt, p, n_seq, t_len)
    y_gla, gla_st = _gla(proj, p, n_seq, t_len)
    y_rw, rw_s, rw_shift = _rwkv(proj, p, n_seq, t_len)

    x_new, ffn_buf = _ffn_seq(x2d, (y_s5, y_ml, y_gla, y_rw), p, norm_final, n_seq, t_len, tm=512,
                              final=final)
    states = (
        s5_re.reshape(n_seq, S5_GROUPS, S5_STATE),
        s5_im.reshape(n_seq, S5_GROUPS, S5_STATE),
        ml_c,
        ml_n[:, 0, :].reshape(n_seq, N_HEADS, HEAD_DIM),
        ml_m[:, 0, LANE_MI:LANE_MI + N_HEADS],
        gla_st,
        rw_s,
        rw_shift[:, 0, :],
        ffn_buf,
    )
    return x_new, states


def _sample_layer(x2d, st, pair_states, layer, p, norm_final, final):
    (s5_re, s5_im, ml_m, rw_shift, ffn_buf) = st
    n = x2d.shape[0]
    npairs = n * N_HEADS
    proj, _ = _inproj(x2d, p["norm_mix"], p["w_in"], p["w_gate_t"], 1, n, tm=n)

    y_s5, s5_re_new, s5_im_new = _s5(proj[:, OFF_MAIN:OFF_MAIN + W_GROUP],
                                     s5_re.reshape(n, S5_WIDTH), s5_im.reshape(n, S5_WIDTH),
                                     p, n, 1, tt=1)

    m_pad = jnp.zeros((n, SMALL_W), F32).at[:, LANE_MI:LANE_MI + N_HEADS].set(ml_m)
    pack, m_new, g, bonus = _step_vectors(proj, rw_shift, m_pad, p)
    ml_c_new, gla_s_new, rw_s_new, opack = _heads_step(pack, *pair_states, layer)

    main = lambda blk: proj[:, blk * W_GROUP:(blk + 1) * W_GROUP]
    y_ml, y_gla, y_rw = _step_post(opack[OROW_ML], main(BLK_MO), opack[OROW_GLA], main(BLK_GG),
                                   opack[OROW_RW], g, bonus, p)
    x_new, ug = _ffn_step(x2d, (y_s5, y_ml, y_gla, y_rw), p, norm_final, ffn_buf[:, 0, :], ffn_buf[:, 1, :],
                          final)
    states = (
        s5_re_new.reshape(n, S5_GROUPS, S5_STATE),
        s5_im_new.reshape(n, S5_GROUPS, S5_STATE),
        ml_c_new.transpose(3, 0, 1, 2),
        opack[OROW_N].reshape(N_HEADS, HEAD_DIM, n).transpose(2, 0, 1),
        m_new[:, LANE_MI:LANE_MI + N_HEADS],
        gla_s_new.transpose(3, 0, 1, 2),
        rw_s_new.transpose(3, 0, 1, 2),
        proj[:, 0:RWKV_COLS],
        jnp.stack([ffn_buf[:, 1, :], ug], axis=1),
    )
    return x_new, states


def kernel(x_prompt, x_sample, state_s5_re, state_s5_im, state_mlstm_C, state_mlstm_n, state_mlstm_m, state_gla_S, state_rwkv_S, state_rwkv_shift, state_ffn_conv, norm_mix, w_in, s5_lam_re, s5_lam_im, s5_log_dt, s5_b_re, s5_b_im, s5_c_re, s5_c_im, s5_d, s5_w_glu, ml_gate_bias, ml_norm, gla_w_alpha, gla_b_alpha, gla_norm, rw_mu, rw_w0, rw_w2, rw_a0, rw_a2, rw_g2, rw_k_k, rw_k_a, rw_r_k, rw_norm, w_out, norm_ffn, ffn_w_up, ffn_conv_w, ffn_conv_b, ffn_w_down, norm_final):
    prm = dict(norm_mix=norm_mix, w_in=w_in, s5_lam_re=s5_lam_re, s5_lam_im=s5_lam_im, s5_log_dt=s5_log_dt,
               s5_b_re=s5_b_re, s5_b_im=s5_b_im, s5_c_re=s5_c_re, s5_c_im=s5_c_im, s5_d=s5_d,
               s5_w_glu=s5_w_glu, ml_gate_bias=ml_gate_bias, ml_norm=ml_norm, gla_w_alpha=gla_w_alpha,
               gla_b_alpha=gla_b_alpha, gla_norm=gla_norm, rw_mu=rw_mu, rw_w0=rw_w0, rw_w2=rw_w2,
               rw_a0=rw_a0, rw_a2=rw_a2, rw_g2=rw_g2, rw_k_k=rw_k_k, rw_k_a=rw_k_a, rw_r_k=rw_r_k,
               rw_norm=rw_norm, w_out=w_out, norm_ffn=norm_ffn, ffn_w_up=ffn_w_up, ffn_conv_w=ffn_conv_w,
               ffn_conv_b=ffn_conv_b, ffn_w_down=ffn_w_down)
    depth = w_in.shape[0]
    n_seq, t_len, d = x_prompt.shape
    n_smp = x_sample.shape[0]
    assert t_len % 1024 == 0 and n_seq % SEQS_PER_STEP == 0 and x_sample.shape[1] == 1
    nf = _Param(norm_final[None, :])
    small_states = (state_s5_re, state_s5_im, state_mlstm_m, state_rwkv_shift, state_ffn_conv)
    lead = depth * N_HEADS
    to_lanes = lambda st: jnp.moveaxis(st, 1, -1)
    pair_states = (to_lanes(state_mlstm_C).reshape(lead, HEAD_DIM, HEAD_DIM, n_smp),
                   to_lanes(state_mlstm_n).reshape(lead, HEAD_DIM, n_smp),
                   to_lanes(state_gla_S).reshape(lead, HEAD_DIM, HEAD_DIM, n_smp),
                   to_lanes(state_rwkv_S).reshape(lead, HEAD_DIM, HEAD_DIM, n_smp))

    P = _prepare_params(prm)
    xp = x_prompt.reshape(n_seq * t_len, d)
    xs = x_sample.reshape(n_smp, d)
    p_states, s_states = [], []
    for l in range(depth):
        p = _layer_params(P, l)
        final = l == depth - 1
        xp, ps = _prompt_layer(xp, p, nf, n_seq, t_len, final)
        xs, ss = _sample_layer(xs, tuple(s[l] for s in small_states), pair_states, l, p, nf, final)
        p_states.append(ps)
        s_states.append(ss)
    new_p = tuple(jnp.stack([st[i] for st in p_states]) for i in range(9))
    new_s = tuple(jnp.stack([st[i] for st in s_states]) for i in range(9))
    return (xp.reshape(n_seq, t_len, d), xs.reshape(n_smp, 1, d)) + new_p + new_s
```

```python
import functools
import math

import jax
import jax.numpy as jnp
from jax import lax
from jax.experimental import pallas as pl
from jax.experimental.pallas import tpu as pltpu

F32 = jnp.float32
BF16 = jnp.bfloat16

LANES = 128
W_GROUP = 256
HEAD_DIM = 64
HEAD_SHIFT = 6
N_HEADS = 4
S5_CH = 16
S5_GROUPS = 16
S5_STATE = 64
S5_WIDTH = S5_GROUPS * S5_STATE
GLA_RANK = 16
GLA_GATE_TEMP = 16.0
RWKV_W_RANK = 32
RWKV_A_RANK = 32
RWKV_G_RANK = 64
RWKV_COLS = 3 * W_GROUP + RWKV_W_RANK + RWKV_A_RANK + RWKV_G_RANK
RWKV_DECAY_SCALE = 0.6065306597126334
CONV_W = 3
EPS = 1e-6
GN_EPS = 64e-5
NEG_BIG = -1e30
EXP_CLAMP = 80.0

SMALL_W = LANES
OFF_SMALL = RWKV_COLS
OFF_MAIN = RWKV_COLS + SMALL_W
D_PROJ = OFF_MAIN + 9 * W_GROUP
LANE_MI = GLA_RANK
LANE_MF = GLA_RANK + N_HEADS
(BLK_U, BLK_MQ, BLK_MK, BLK_MV, BLK_MO, BLK_GQ, BLK_GK, BLK_GV, BLK_GG) = range(
    OFF_MAIN // W_GROUP, OFF_MAIN // W_GROUP + 9)
BLK_SMALL = OFF_SMALL // SMALL_W

CHUNK_ML = 128
CHUNK_RW = 64
SEQS_PER_STEP = 8
SEQS_PER_STEP_ML = 8
VMEM_LIMIT = 56 * 1024 * 1024


def _cparams(*sem):
    return pltpu.CompilerParams(dimension_semantics=sem, vmem_limit_bytes=VMEM_LIMIT)


class _Param:
    def __init__(self, arr, *lead):
        self.arr, self.lead = arr, tuple(lead)

    @property
    def shape(self):
        return tuple(self.arr.shape[len(self.lead):])

    def spec(self, **kw):
        lead, tail = self.lead, (0,) * len(self.shape)
        return pl.BlockSpec((None,) * len(lead) + self.shape, lambda *grid_idx: lead + tail, **kw)


def _dot(a, b):
    return jnp.dot(a.astype(BF16), b.astype(BF16), preferred_element_type=F32)


def _dot_nt(a, b):
    return lax.dot_general(a.astype(BF16), b.astype(BF16), (((1,), (1,)), ((), ())),
                           preferred_element_type=F32)


def _dot_tn(a, b):
    return lax.dot_general(a.astype(BF16), b.astype(BF16), (((0,), (0,)), ((), ())),
                           preferred_element_type=F32)


def _split2(x):
    hi = x.astype(BF16)
    lo = (x - hi.astype(F32)).astype(BF16)
    return hi, lo


def _dot_exact_rhs(a01, x):
    hi, lo = _split2(x)
    f = lambda p: jnp.dot(a01, p, preferred_element_type=F32)
    return f(hi) + f(lo)


def _dot_exact_lhs(x, b01):
    hi, lo = _split2(x)
    f = lambda p: jnp.dot(p, b01, preferred_element_type=F32)
    return f(hi) + f(lo)


def _sigmoid(x):
    return 1.0 / (1.0 + jnp.exp(-x))


def _log_sigmoid(x):
    return jnp.minimum(x, 0.0) - jnp.log(1.0 + jnp.exp(-jnp.abs(x)))


def _gelu(x):
    return 0.5 * x * (1.0 + jnp.tanh(math.sqrt(2.0 / math.pi) * (x + 0.044715 * (x * x * x))))


def _iota(shape, dim):
    return lax.broadcasted_iota(jnp.int32, shape, dim)


def _head_ones():
    r = _iota((W_GROUP, W_GROUP), 0) >> HEAD_SHIFT
    c = _iota((W_GROUP, W_GROUP), 1) >> HEAD_SHIFT
    return jnp.where(r == c, 1.0, 0.0).astype(BF16)


def _head_sum(x, ones):
    return _dot_exact_lhs(x, ones)


def _head_lane_mask(h, rows=1):
    lane = _iota((rows, W_GROUP), 1) >> HEAD_SHIFT
    return jnp.where(lane == h, 1.0, 0.0).astype(F32)


def _rmsnorm_rows(x, g):
    ms = jnp.mean(x * x, axis=-1, keepdims=True)
    return x * lax.rsqrt(ms + EPS) * g


def _tril_ones(n, strict=False):
    r = _iota((n, n), 0)
    c = _iota((n, n), 1)
    return jnp.where((r > c) if strict else (r >= c), 1.0, 0.0).astype(BF16)


def _triu_ones(n):
    r = _iota((n, n), 0)
    c = _iota((n, n), 1)
    return jnp.where(r <= c, 1.0, 0.0).astype(BF16)


def _store_head_blocks(out_ref, scr_ref, transpose=False):
    for g in range(scr_ref.shape[0]):
        m = scr_ref[g].T if transpose else scr_ref[g]
        for h in range(N_HEADS):
            lo, hi = h * HEAD_DIM, (h + 1) * HEAD_DIM
            out_ref[g, h] = m[lo:hi, lo:hi]


def _HEAD_BLOCKS_SPEC(G):
    return pl.BlockSpec((G, N_HEADS, HEAD_DIM, HEAD_DIM), lambda b, c: (b, 0, 0, 0))


def _inproj_kernel(x_ref, g_ref, w_ref, wg_ref, proj_ref, gt_ref, u_ref):
    hn = _rmsnorm_rows(x_ref[...], g_ref[...]).astype(BF16)
    proj = jnp.dot(hn, w_ref[...], preferred_element_type=F32)
    proj_ref[...] = proj
    u_ref[...] = proj[:, OFF_MAIN:OFF_MAIN + W_GROUP]
    gt_ref[0] = lax.dot_general(wg_ref[...], hn, (((1,), (1,)), ((), ())),
                                preferred_element_type=F32)


def _inproj(x2d, g, w, wg, n_seq, t_len, tm):
    n, d = x2d.shape
    tiles_per_seq = t_len // tm
    return pl.pallas_call(
        _inproj_kernel,
        grid=(n // tm,),
        in_specs=[
            pl.BlockSpec((tm, d), lambda i: (i, 0)),
            g.spec(), w.spec(pipeline_mode=pl.Buffered(1)), wg.spec(),
        ],
        out_specs=[
            pl.BlockSpec((tm, D_PROJ), lambda i: (i, 0)),
            pl.BlockSpec((1, 8, tm), lambda i: (i // tiles_per_seq, 0, i % tiles_per_seq)),
            pl.BlockSpec((tm, W_GROUP), lambda i: (i, 0)),
        ],
        out_shape=[
            jax.ShapeDtypeStruct((n, D_PROJ), F32),
            jax.ShapeDtypeStruct((n_seq, 8, t_len), F32),
            jax.ShapeDtypeStruct((n, W_GROUP), F32),
        ],
        compiler_params=_cparams("arbitrary"),
        name="inproj",
    )(x2d, g.arr, w.arr, wg.arr)


def _s5_kernel(u_ref, unext_ref, h0re_ref, h0im_ref, bmat_ref, lre_ref, lim_ref, cmat_ref, d_ref, wglu_ref,
               y_ref, hre_out, him_out, bu_scr, hs_scr, hre_scr, him_scr, *, bp, tt):
    i = pl.program_id(0)
    nt = S5_WIDTH // LANES

    def project(u_tile, slot):
        bu = jnp.dot(u_tile.astype(BF16), bmat_ref[...], preferred_element_type=F32)
        for j in range(2 * nt):
            bu_scr[slot, j] = bu[:, j * LANES:(j + 1) * LANES]

    @pl.when(i == 0)
    def _():
        hre_scr[...] = h0re_ref[...]
        him_scr[...] = h0im_ref[...]
        project(u_ref[...], 0)

    slot = lax.rem(i, 2)
    lre = [lre_ref[:, j * LANES:(j + 1) * LANES] for j in range(nt)]
    lim = [lim_ref[:, j * LANES:(j + 1) * LANES] for j in range(nt)]

    def step(t, carry):
        rows = pl.ds(pl.multiple_of(t * bp, bp), bp)
        new = []
        for j in range(nt):
            hre, him = carry[j], carry[nt + j]
            new.append((lre[j] * hre - lim[j] * him + bu_scr[slot, j, rows, :],
                        lre[j] * him + lim[j] * hre + bu_scr[slot, nt + j, rows, :]))
        out = tuple(x[0] for x in new) + tuple(x[1] for x in new)
        for j in range(2 * nt):
            hs_scr[j, rows, :] = out[j]
        return out

    carry = tuple(hre_scr[:, j * LANES:(j + 1) * LANES] for j in range(nt)) + tuple(
        him_scr[:, j * LANES:(j + 1) * LANES] for j in range(nt))
    carry = step(0, carry) if tt == 1 else lax.fori_loop(0, tt, step, carry)
    hre = jnp.concatenate(carry[0:nt], axis=1)
    him = jnp.concatenate(carry[nt:2 * nt], axis=1)
    hre_scr[...] = hre
    him_scr[...] = him
    hre_out[...] = hre
    him_out[...] = him

    project(unext_ref[...], lax.rem(i + 1, 2))
    u = u_ref[...]
    hs = jnp.concatenate([hs_scr[j] for j in range(2 * nt)], axis=1)
    y = jnp.dot(hs.astype(BF16), cmat_ref[...], preferred_element_type=F32) + d_ref[...] * u
    z = _gelu(y)
    y_ref[...] = (z * _sigmoid(_dot(z, wglu_ref[...]))).astype(y_ref.dtype)


def _s5(u_tb, h0re, h0im, sp, bp, t_len, tt):
    n = u_tb.shape[0]
    nsteps = t_len // tt
    full = lambda shape: pl.BlockSpec(shape, lambda i: (0,) * len(shape))
    tile = pl.BlockSpec((tt * bp, W_GROUP), lambda i: (i, 0))
    next_tile = pl.BlockSpec((tt * bp, W_GROUP), lambda i: (jnp.minimum(i + 1, nsteps - 1), 0))
    return pl.pallas_call(
        functools.partial(_s5_kernel, bp=bp, tt=tt),
        grid=(nsteps,),
        in_specs=[
            tile, next_tile,
            full((bp, S5_WIDTH)), full((bp, S5_WIDTH)),
            sp["bmat"].spec(), sp["lre"].spec(), sp["lim"].spec(), sp["cmat"].spec(), sp["d"].spec(),
            sp["wglu"].spec(),
        ],
        out_specs=[
            tile,
            full((bp, S5_WIDTH)), full((bp, S5_WIDTH)),
        ],
        out_shape=[
            jax.ShapeDtypeStruct((n, W_GROUP), BF16),
            jax.ShapeDtypeStruct((bp, S5_WIDTH), F32),
            jax.ShapeDtypeStruct((bp, S5_WIDTH), F32),
        ],
        scratch_shapes=[
            pltpu.VMEM((2, 2 * S5_WIDTH // LANES, tt * bp, LANES), F32),
            pltpu.VMEM((2 * S5_WIDTH // LANES, tt * bp, LANES), F32),
            pltpu.VMEM((bp, S5_WIDTH), F32),
            pltpu.VMEM((bp, S5_WIDTH), F32),
        ],
        compiler_params=_cparams("arbitrary"),
        name="s5_scan",
    )(u_tb, u_tb, h0re, h0im, sp["bmat"].arr, sp["lre"].arr, sp["lim"].arr, sp["cmat"].arr, sp["d"].arr,
      sp["wglu"].arr)


def _run_interleaved(chains):
    live = list(chains)
    while live:
        still = []
        for ch in live:
            try:
                next(ch)
                still.append(ch)
            except StopIteration:
                pass
        live = still


def _mlstm_kernel(q_ref, k_ref, v_ref, og_ref, sm_ref, gt_ref, brow_ref, bcol_ref, norm_ref,
                  y_ref, c_out, n_out, m_out, c_scr, n_scr, m_scr, *, L, nc, G):
    c = pl.program_id(1)

    @pl.when(c == 0)
    def _():
        c_scr[...] = jnp.zeros_like(c_scr)
        n_scr[...] = jnp.zeros_like(n_scr)
        m_scr[...] = jnp.zeros_like(m_scr)

    ones = _head_ones()
    tril = _tril_ones(L)
    triu = _triu_ones(L)
    l_shift = L.bit_length() - 1
    src = _iota((LANES, N_HEADS * L), 0)
    to_scores = jnp.where(src == LANE_MI + (_iota((LANES, N_HEADS * L), 1) >> l_shift), 1.0, 0.0).astype(BF16)
    src = _iota((LANES, W_GROUP), 0)
    to_feats = jnp.where(src == LANE_MI + (_iota((LANES, W_GROUP), 1) >> HEAD_SHIFT), 1.0, 0.0).astype(BF16)
    seg = _iota((N_HEADS * L, W_GROUP), 0) >> l_shift
    score_sum = jnp.where(seg == (_iota((N_HEADS * L, W_GROUP), 1) >> HEAD_SHIFT), 1.0, 0.0).astype(BF16)
    causal = _iota((L, N_HEADS * L), 0) >= (_iota((L, N_HEADS * L), 1) & (L - 1))
    t_idx = _iota((L, LANES), 0)

    seqs = list(range(G))
    cat = lambda xs: jnp.concatenate(xs, axis=0)
    split = lambda x: [x[i * L:(i + 1) * L] for i in seqs]

    def running_max(x):
        shift = 1
        while shift < L:
            x = jnp.maximum(x, jnp.where(t_idx >= shift, pltpu.roll(x, shift, 0), NEG_BIG))
            shift *= 2
        return x

    q = [q_ref[i] for i in seqs]
    k = [k_ref[i] * (HEAD_DIM ** -0.5) for i in seqs]
    v = [v_ref[i] for i in seqs]
    C = [c_scr[i] for i in seqs]
    n = [n_scr[i, 0:1, :] for i in seqs]
    m0 = [m_scr[i, 0:1, :] for i in seqs]
    pre = [sm_ref[i] + brow_ref[...] for i in seqs]
    gtb = [gt_ref[i] + bcol_ref[...] for i in seqs]
    bc_col = _each(lambda x: _dot_exact_rhs(tril, _log_sigmoid(x)), pre)
    bc_rows = _dot_exact_lhs(_log_sigmoid(cat(gtb)), triu)
    qC = _each(_dot_nt, q, C)
    qk = _each(lambda q_, k_: _dot_nt(q_, _stack_heads(k_)), q, k)
    qn = _head_sum(cat(_each(lambda q_, n_: q_ * n_, q, n)), ones)

    b_col = _each(lambda x: pltpu.roll(x, LANES - N_HEADS, 1), bc_col)
    r_col = _each(lambda p_, b_: p_ - b_, pre, b_col)
    mu = _each(lambda r_, m_: jnp.maximum(running_max(r_), m_), r_col, m0)
    mu_last = _each(lambda x: x[L - 1:L, :], mu)
    mu_scores = split(_dot_exact_lhs(cat(mu), to_scores))

    def weights(i):
        r_rows = gtb[i][0:N_HEADS, :] - bc_rows[8 * i + N_HEADS:8 * i + 2 * N_HEADS, :]
        r_all = jnp.concatenate([r_rows[h:h + 1, :] for h in range(N_HEADS)], axis=1)
        return jnp.exp(jnp.where(causal, r_all - mu_scores[i], NEG_BIG)) * qk[i]

    ws = [weights(i) for i in seqs]
    wsv = _each(lambda w_, v_: _dot(w_, _stack_heads(v_)), ws, v)
    w0 = _dot(cat(_each(lambda m_, u_: jnp.exp(m_ - u_), m0, mu)), to_feats)
    m_row = _dot_exact_lhs(cat(_each(lambda b_, u_: b_ + u_, b_col, mu)), to_feats)
    den = w0 * qn + _dot(cat(ws), score_sum)
    hc = (w0 * cat(qC) + cat(wsv)) / jnp.maximum(jnp.abs(den), jnp.exp(-m_row))

    wl = split(_dot(cat(_each(lambda r_, u_: jnp.exp(r_ - u_), r_col, mu_last)), to_feats))
    w0l_rows = _each(lambda m_, u_: jnp.exp(m_ - u_), m0, mu_last)
    w0l_rows = w0l_rows + [jnp.zeros((1, LANES), F32)] * (-G % 8)
    w0l = _dot_exact_lhs(cat(w0l_rows), to_feats)
    kw = _each(lambda k_, w_: k_ * w_, k, wl)
    upd = _each(_dot_tn, v, kw)
    for i in seqs:
        c_scr[i] = w0l[i:i + 1, :] * C[i] + ones.astype(F32) * upd[i]
        n_scr[i, 0:1, :] = w0l[i:i + 1, :] * n[i] + jnp.sum(kw[i], axis=0, keepdims=True)
        m_scr[i, 0:1, :] = b_col[i][L - 1:L, :] + mu_last[i]

    ms = _head_sum(hc * hc, ones) * (1.0 / HEAD_DIM)
    y = split(hc * lax.rsqrt(ms + EPS) * norm_ref[...])
    for i in seqs:
        y_ref[i] = (y[i] * _sigmoid(og_ref[i])).astype(y_ref.dtype)

    @pl.when(c == nc - 1)
    def _():
        _store_head_blocks(c_out, c_scr)
        n_out[...] = jnp.broadcast_to(n_scr[:, 0:1, :], n_out.shape)
        m_out[...] = m_scr[...]


def _seq_spec(G, L, blk, width=W_GROUP):
    return pl.BlockSpec((G, L, width), lambda b, c: (b, c, blk))


def _mlstm(proj, gt, p, n_seq, t_len):
    L = CHUNK_ML
    G = SEQS_PER_STEP_ML
    nc = t_len // L
    per_seq = lambda rows, width: pl.BlockSpec((G, rows, width), lambda b, c: (b, 0, 0))
    proj3 = proj.reshape(n_seq, t_len, D_PROJ)
    y, c_new, n_new, m_new = pl.pallas_call(
        functools.partial(_mlstm_kernel, L=L, nc=nc, G=G),
        grid=(n_seq // G, nc),
        in_specs=[
            _seq_spec(G, L, BLK_MQ), _seq_spec(G, L, BLK_MK), _seq_spec(G, L, BLK_MV),
            _seq_spec(G, L, BLK_MO), _seq_spec(G, L, BLK_SMALL, SMALL_W),
            pl.BlockSpec((G, 8, L), lambda b, c: (b, 0, c)),
            p["ml_brow"].spec(), p["ml_bcol"].spec(), p["ml_norm"].spec(),
        ],
        out_specs=[
            pl.BlockSpec((G, L, W_GROUP), lambda b, c: (b, c, 0)),
            _HEAD_BLOCKS_SPEC(G), per_seq(8, W_GROUP), per_seq(8, LANES),
        ],
        out_shape=[
            jax.ShapeDtypeStruct((n_seq, t_len, W_GROUP), BF16),
            jax.ShapeDtypeStruct((n_seq, N_HEADS, HEAD_DIM, HEAD_DIM), F32),
            jax.ShapeDtypeStruct((n_seq, 8, W_GROUP), F32),
            jax.ShapeDtypeStruct((n_seq, 8, LANES), F32),
        ],
        scratch_shapes=[
            pltpu.VMEM((G, W_GROUP, W_GROUP), F32),
            pltpu.VMEM((G, 8, W_GROUP), F32),
            pltpu.VMEM((G, 8, LANES), F32),
        ],
        compiler_params=_cparams("arbitrary", "arbitrary"),
        name="mlstm_chunk",
    )(proj3, proj3, proj3, proj3, proj3, gt, p["ml_brow"].arr, p["ml_bcol"].arr, p["ml_norm"].arr)
    return y.reshape(n_seq * t_len, W_GROUP), c_new, n_new, m_new


def _gla_kernel(q_ref, k_ref, v_ref, gg_ref, sm_ref, wa_ref, ba_ref, norm_ref,
                y_ref, s_out, s_scr, *, L, nc, G):
    c = pl.program_id(1)

    @pl.when(c == 0)
    def _():
        s_scr[...] = jnp.zeros_like(s_scr)

    ones = _head_ones()
    tril = _tril_ones(L)
    causal = _iota((L, N_HEADS * L), 0) >= (_iota((L, N_HEADS * L), 1) & (L - 1))

    def chain(gi):
        q = q_ref[gi] * (HEAD_DIM ** -0.5)
        k = k_ref[gi]
        v = v_ref[gi]
        la = _log_sigmoid(_dot(sm_ref[gi], wa_ref[...]) + ba_ref[...]) * (1.0 / GLA_GATE_TEMP)
        yield
        bc = _dot_exact_rhs(tril, la)
        yield
        qs = q * jnp.exp(bc)
        kh = k * jnp.exp(jnp.minimum(-bc, EXP_CLAMP))
        ST = s_scr[gi]
        att = jnp.where(causal, _dot_nt(qs, _stack_heads(kh)), 0.0)
        from_state = _dot_nt(qs, ST)
        yield
        o = from_state + _dot(att, _stack_heads(v))
        b_last = bc[L - 1:L, :]
        kbar = k * jnp.exp(b_last - bc)
        s_scr[gi] = ST * jnp.exp(b_last) + ones.astype(F32) * _dot_tn(v, kbar)
        yield
        ms = _head_sum(o * o, ones) * (1.0 / HEAD_DIM)
        gg = gg_ref[gi]
        y = o * lax.rsqrt(ms + EPS) * norm_ref[...] * (gg * _sigmoid(gg))
        y_ref[gi] = y.astype(y_ref.dtype)

    _run_interleaved([chain(gi) for gi in range(G)])

    @pl.when(c == nc - 1)
    def _():
        _store_head_blocks(s_out, s_scr, transpose=True)


def _gla(proj, p, n_seq, t_len):
    L = CHUNK_ML
    G = SEQS_PER_STEP_ML
    nc = t_len // L
    proj3 = proj.reshape(n_seq, t_len, D_PROJ)
    y, s_new = pl.pallas_call(
        functools.partial(_gla_kernel, L=L, nc=nc, G=G),
        grid=(n_seq // G, nc),
        in_specs=[
            _seq_spec(G, L, BLK_GQ), _seq_spec(G, L, BLK_GK), _seq_spec(G, L, BLK_GV),
            _seq_spec(G, L, BLK_GG), _seq_spec(G, L, BLK_SMALL, SMALL_W),
            p["gla_wa"].spec(), p["gla_ba"].spec(), p["gla_norm"].spec(),
        ],
        out_specs=[
            pl.BlockSpec((G, L, W_GROUP), lambda b, c: (b, c, 0)),
            _HEAD_BLOCKS_SPEC(G),
        ],
        out_shape=[
            jax.ShapeDtypeStruct((n_seq, t_len, W_GROUP), BF16),
            jax.ShapeDtypeStruct((n_seq, N_HEADS, HEAD_DIM, HEAD_DIM), F32),
        ],
        scratch_shapes=[pltpu.VMEM((G, W_GROUP, W_GROUP), F32)],
        compiler_params=_cparams("arbitrary", "arbitrary"),
        name="gla_chunk",
    )(proj3, proj3, proj3, proj3, proj3, p["gla_wa"].arr, p["gla_ba"].arr, p["gla_norm"].arr)
    return y.reshape(n_seq * t_len, W_GROUP), s_new


def _rwkv_vectors(rc, prev, p_mu, p_w0, p_a0, p_kk, p_ka, p_rk, w2, a2, g2, ones):
    xm = rc + p_mu * (prev - rc)
    rr = xm[:, 0:W_GROUP]
    rk = xm[:, W_GROUP:2 * W_GROUP]
    rv = xm[:, 2 * W_GROUP:3 * W_GROUP]
    tail = xm[:, 3 * W_GROUP:RWKV_COLS]
    lw = -RWKV_DECAY_SCALE * _sigmoid(p_w0 + _dot(jnp.tanh(tail), w2))
    a = _sigmoid(p_a0 + _dot(tail, a2))
    g = _dot(_sigmoid(tail), g2)
    kk = rk * p_kk
    kk = kk * lax.rsqrt(jnp.maximum(_head_sum(kk * kk, ones), 1e-24))
    kt = rk * (1.0 + (a - 1.0) * p_ka)
    bonus = _head_sum(rr * kt * p_rk, ones) * rv
    return rr, lw, kt, rv, kk, a, g, bonus


def _head_layernorm(o, g, ones):
    mu = _head_sum(o, ones) * (1.0 / HEAD_DIM)
    oc = o - mu
    var = _head_sum(oc * oc, ones) * (1.0 / HEAD_DIM)
    return oc * lax.rsqrt(var + GN_EPS) * g


def _stack_heads(x):
    xb = x.astype(BF16)
    lane_head = _iota((1, W_GROUP), 1) >> HEAD_SHIFT
    return jnp.concatenate([jnp.where(lane_head == h, xb, jnp.zeros_like(xb)) for h in range(N_HEADS)],
                           axis=0)


def _each(f, *seqs):
    return [f(*args) for args in zip(*seqs)]


def _block_mm(x_list, y_list):
    return _each(lambda x, y: _dot(x, _stack_heads(y)), x_list, y_list)


def _unit_lower_inverse(a_list, L):
    t_idx = _iota((L, N_HEADS * L), 0)
    s_idx = _iota((L, N_HEADS * L), 1) & (L - 1)
    eye = jnp.where(t_idx == s_idx, 1.0, 0.0).astype(F32)
    in16 = (t_idx >> 4) == (s_idx >> 4)
    in32 = ((t_idx >> 5) == (s_idx >> 5)) & ((t_idx >> 4) > (s_idx >> 4))
    in64 = (t_idx >> 5) > (s_idx >> 5)
    x16 = _each(lambda a: jnp.where(in16, -a, 0.0), a_list)
    x2 = _block_mm(x16, x16)
    x4 = _block_mm(x2, x2)
    x8 = _block_mm(x4, x4)
    t = _each(lambda x: eye + x, x16)
    for xp in (x2, x4, x8):
        t = _each(lambda t_, d: t_ + d, t, _block_mm(t, xp))
    for sel in (in32, in64):
        w = _block_mm(_each(lambda a: jnp.where(sel, a, 0.0), a_list), t)
        t = _each(lambda t_, d: t_ - d, t, _block_mm(t, w))
    return t


def _rwkv_chunks(rcs, prev_rows, states, prm, ones, L):
    (mu, w0, a0, pkk, pka, prk, w2, a2, g2, norm) = prm
    first_row = _iota((L, RWKV_COLS), 0) == 0
    prevs = _each(lambda rc, pr: jnp.where(first_row, pr, pltpu.roll(rc, 1, 0)), rcs, prev_rows)
    n_seq = len(rcs)
    stacked = _rwkv_vectors(jnp.concatenate(rcs, axis=0), jnp.concatenate(prevs, axis=0),
                            mu, w0, a0, pkk, pka, prk, w2, a2, g2, ones)
    r, lw, kx, v, kk, a, g, bonus = [[x[i * L:(i + 1) * L] for i in range(n_seq)] for x in stacked]

    tril = _tril_ones(L)
    lc = _each(lambda x: _dot_exact_rhs(tril, x), lw)
    akk = _each(lambda a_, k_: a_ * k_, a, kk)
    e_neg = _each(lambda x: jnp.exp(-x), lc)
    l_last = _each(lambda x: x[L - 1:L, :], lc)
    e_end = _each(lambda ll, x: jnp.exp(ll - x), l_last, lc)
    kr = _each(lambda k_, r_, c_, w_: jnp.concatenate([k_ * jnp.exp(c_ - w_), r_ * jnp.exp(c_)], axis=0),
               kk, r, lc, lw)
    a_hat = _each(lambda x, e: _stack_heads(x * e), akk, e_neg)
    k_hat = _each(lambda x, e: _stack_heads(x * e), kx, e_neg)
    v_bd = _each(_stack_heads, v)

    t_idx = _iota((L, N_HEADS * L), 0)
    s_idx = _iota((L, N_HEADS * L), 1) & (L - 1)
    strict = t_idx > s_idx
    incl = t_idx >= s_idx
    pa = _each(_dot_nt, kr, a_hat)
    pk = _each(_dot_nt, kr, k_hat)
    a_ua = _each(lambda p_: jnp.where(strict, p_[0:L], 0.0), pa)
    b_ra = _each(lambda p_: jnp.where(incl, p_[L:2 * L], 0.0), pa)
    ab_k = _each(lambda p_: jnp.concatenate([jnp.where(strict, p_[0:L], 0.0),
                                             jnp.where(incl, p_[L:2 * L], 0.0)], axis=0), pk)
    T = _unit_lower_inverse(a_ua, L)

    from_state = _each(_dot_nt, kr, states)
    from_v = _each(_dot, ab_k, v_bd)
    rhs = _each(lambda s_, v_: s_[0:L] + v_[0:L], from_state, from_v)
    U = _block_mm(T, rhs)
    corr = _block_mm(b_ra, U)
    o = _each(lambda s_, v_, c_: s_[L:2 * L] + v_[L:2 * L] - c_, from_state, from_v, corr)
    upd = _each(lambda v_, u_, k_, a_, e: _dot_tn(jnp.concatenate([v_, u_], axis=0),
                                                  jnp.concatenate([k_ * e, -(a_ * e)], axis=0)),
                v, U, kx, akk, e_end)
    s_new = _each(lambda s_, ll, d: s_ * jnp.exp(ll) + ones.astype(F32) * d, states, l_last, upd)
    y_all = ((_head_layernorm(jnp.concatenate(o, axis=0), norm, ones) + stacked[7]) * stacked[6])
    return [y_all[i * L:(i + 1) * L] for i in range(n_seq)], s_new


def _rwkv_kernel(rc_ref, mu_ref, w0_ref, a0_ref, kk_ref, ka_ref, rk_ref, w2_ref, a2_ref, g2_ref,
                 norm_ref, y_ref, s_out, shift_out, s_scr, prev_scr, *, L, nc, G):
    c = pl.program_id(1)

    @pl.when(c == 0)
    def _():
        s_scr[...] = jnp.zeros_like(s_scr)
        prev_scr[...] = jnp.zeros_like(prev_scr)

    ones = _head_ones()
    prm = (mu_ref[...], w0_ref[...], a0_ref[...], kk_ref[...], ka_ref[...], rk_ref[...],
           w2_ref[...], a2_ref[...], g2_ref[...], norm_ref[...])
    rcs = [rc_ref[gi] for gi in range(G)]
    ys, s_new = _rwkv_chunks(rcs, [prev_scr[gi, 0:1, :] for gi in range(G)],
                             [s_scr[gi] for gi in range(G)], prm, ones, L)
    for gi in range(G):
        prev_scr[gi, 0:1, :] = rcs[gi][L - 1:L, :]
        s_scr[gi] = s_new[gi]
        y_ref[gi] = ys[gi].astype(y_ref.dtype)

    @pl.when(c == nc - 1)
    def _():
        _store_head_blocks(s_out, s_scr)
        shift_out[...] = jnp.broadcast_to(prev_scr[:, 0:1, :], shift_out.shape)


def _rwkv(proj, p, n_seq, t_len):
    L = CHUNK_RW
    G = SEQS_PER_STEP
    nc = t_len // L
    names = ("rw_mu", "rw_w0", "rw_a0", "rw_kk", "rw_ka", "rw_rk", "rw_w2", "rw_a2", "rw_g2", "rw_norm")
    y, s_new, shift = pl.pallas_call(
        functools.partial(_rwkv_kernel, L=L, nc=nc, G=G),
        grid=(n_seq // G, nc),
        in_specs=[
            pl.BlockSpec((G, L, RWKV_COLS), lambda b, c: (b, c, 0)),
        ] + [p[k].spec() for k in names],
        out_specs=[
            pl.BlockSpec((G, L, W_GROUP), lambda b, c: (b, c, 0)),
            _HEAD_BLOCKS_SPEC(G),
            pl.BlockSpec((G, 8, RWKV_COLS), lambda b, c: (b, 0, 0)),
        ],
        out_shape=[
            jax.ShapeDtypeStruct((n_seq, t_len, W_GROUP), BF16),
            jax.ShapeDtypeStruct((n_seq, N_HEADS, HEAD_DIM, HEAD_DIM), F32),
            jax.ShapeDtypeStruct((n_seq, 8, RWKV_COLS), F32),
        ],
        scratch_shapes=[
            pltpu.VMEM((G, W_GROUP, W_GROUP), F32),
            pltpu.VMEM((G, 8, RWKV_COLS), F32),
        ],
        compiler_params=_cparams("arbitrary", "arbitrary"),
        name="rwkv_chunk",
    )(proj.reshape(n_seq, t_len, D_PROJ), *[p[k].arr for k in names])
    return y.reshape(n_seq * t_len, W_GROUP), s_new, shift


FF_CHUNK = 256


def _mix_residual(x_ref, ys, wout_ref):
    acc = x_ref[...]
    for j, y in enumerate(ys):
        acc = acc + jnp.dot(y.astype(BF16), wout_ref[j * W_GROUP:(j + 1) * W_GROUP, :],
                            preferred_element_type=F32)
    return acc


def _ffn_body(x1, nrm_ref, wup_ref, cw_ref, cb_ref, wdn_ref, prev_rows, d_ff, act_scr):
    h2 = _rmsnorm_rows(x1, nrm_ref[...]).astype(BF16)
    for j in range(d_ff // FF_CHUNK):
        lo, hi = j * FF_CHUNK, (j + 1) * FF_CHUNK
        ug = jnp.dot(h2, wup_ref[:, lo:hi], preferred_element_type=F32)
        uv = jnp.dot(h2, wup_ref[:, d_ff + lo:d_ff + hi], preferred_element_type=F32)
        p2, p1 = prev_rows(j, ug)
        conv = (cb_ref[:, lo:hi] + cw_ref[0:1, lo:hi] * p2 + cw_ref[1:2, lo:hi] * p1
                + cw_ref[2:3, lo:hi] * ug)
        act_scr[:, lo:hi] = (_gelu(conv) * uv).astype(BF16)
    return x1 + jnp.dot(act_scr[...], wdn_ref[...], preferred_element_type=F32)


def _ffn_seq_kernel(x_ref, y0_ref, y1_ref, y2_ref, y3_ref, wout_ref, nrm_ref, wup_ref, cw_ref, cb_ref,
                    wdn_ref, nf_ref, o_ref, buf_out, carry_scr, act_scr, *, tm, d_ff, final):
    t = pl.program_id(1)

    @pl.when(t == 0)
    def _():
        carry_scr[...] = jnp.zeros_like(carry_scr)

    x1 = _mix_residual(x_ref, (y0_ref[...], y1_ref[...], y2_ref[...], y3_ref[...]), wout_ref)
    row = _iota((tm, FF_CHUNK), 0)

    def prev_rows(j, ug):
        lo, hi = j * FF_CHUNK, (j + 1) * FF_CHUNK
        c0 = carry_scr[0:1, lo:hi]
        c1 = carry_scr[1:2, lo:hi]
        p1 = jnp.where(row == 0, c1, pltpu.roll(ug, 1, 0))
        p2 = jnp.where(row == 0, c0, jnp.where(row == 1, c1, pltpu.roll(ug, 2, 0)))
        carry_scr[0:2, lo:hi] = ug[tm - 2:tm, :]
        return p2, p1

    out = _ffn_body(x1, nrm_ref, wup_ref, cw_ref, cb_ref, wdn_ref, prev_rows, d_ff, act_scr)
    if final:
        out = _rmsnorm_rows(out, nf_ref[...])
    o_ref[...] = out
    buf_out[0] = carry_scr[...]


def _ffn_step_kernel(x_ref, y0_ref, y1_ref, y2_ref, y3_ref, wout_ref, nrm_ref, wup_ref, cw_ref, cb_ref,
                     wdn_ref, nf_ref, p2_ref, p1_ref, o_ref, ug_out, act_scr, *, d_ff, final):
    x1 = _mix_residual(x_ref, (y0_ref[...], y1_ref[...], y2_ref[...], y3_ref[...]), wout_ref)

    def prev_rows(j, ug):
        lo, hi = j * FF_CHUNK, (j + 1) * FF_CHUNK
        ug_out[:, lo:hi] = ug
        return p2_ref[:, lo:hi], p1_ref[:, lo:hi]

    out = _ffn_body(x1, nrm_ref, wup_ref, cw_ref, cb_ref, wdn_ref, prev_rows, d_ff, act_scr)
    if final:
        out = _rmsnorm_rows(out, nf_ref[...])
    o_ref[...] = out


def _ffn_seq(x2d, ys, p, norm_final, n_seq, t_len, tm, final):
    n, d = x2d.shape
    d_ff = p["ffn_wdn"].shape[0]
    nt = t_len // tm
    rows = lambda w: pl.BlockSpec((tm, w), lambda b, t: (b * nt + t, 0))
    once = dict(pipeline_mode=pl.Buffered(1))
    return pl.pallas_call(
        functools.partial(_ffn_seq_kernel, tm=tm, d_ff=d_ff, final=final),
        grid=(n_seq, nt),
        in_specs=[rows(d)] + [rows(W_GROUP)] * 4 + [
            p["w_out"].spec(**once), p["norm_ffn"].spec(), p["ffn_wup"].spec(**once), p["ffn_cw"].spec(),
            p["ffn_cb"].spec(), p["ffn_wdn"].spec(**once), norm_final.spec()],
        out_specs=[rows(d), pl.BlockSpec((1, CONV_W - 1, d_ff), lambda b, t: (b, 0, 0))],
        out_shape=[jax.ShapeDtypeStruct((n, d), F32),
                   jax.ShapeDtypeStruct((n_seq, CONV_W - 1, d_ff), F32)],
        scratch_shapes=[pltpu.VMEM((CONV_W - 1, d_ff), F32), pltpu.VMEM((tm, d_ff), BF16)],
        compiler_params=_cparams("arbitrary", "arbitrary"),
        name="wout_ffn_seq",
    )(x2d, *ys, p["w_out"].arr, p["norm_ffn"].arr, p["ffn_wup"].arr, p["ffn_cw"].arr, p["ffn_cb"].arr,
      p["ffn_wdn"].arr, norm_final.arr)


def _ffn_step(x2d, ys, p, norm_final, prev2, prev1, final):
    n, d = x2d.shape
    d_ff = p["ffn_wdn"].shape[0]
    full = lambda shape: pl.BlockSpec(shape, lambda i: (0,) * len(shape))
    return pl.pallas_call(
        functools.partial(_ffn_step_kernel, d_ff=d_ff, final=final),
        grid=(1,),
        in_specs=[full((n, d))] + [full((n, W_GROUP))] * 4 + [
            p["w_out"].spec(), p["norm_ffn"].spec(), p["ffn_wup"].spec(), p["ffn_cw"].spec(),
            p["ffn_cb"].spec(), p["ffn_wdn"].spec(), norm_final.spec(), full((n, d_ff)), full((n, d_ff))],
        out_specs=[full((n, d)), full((n, d_ff))],
        out_shape=[jax.ShapeDtypeStruct((n, d), F32), jax.ShapeDtypeStruct((n, d_ff), F32)],
        scratch_shapes=[pltpu.VMEM((n, d_ff), BF16)],
        compiler_params=_cparams("arbitrary"),
        name="wout_ffn_step",
    )(x2d, *ys, p["w_out"].arr, p["norm_ffn"].arr, p["ffn_wup"].arr, p["ffn_cw"].arr, p["ffn_cb"].arr,
      p["ffn_wdn"].arr, norm_final.arr, prev2, prev1)


def _step_vec_kernel(proj_ref, shift_ref, m_ref, brow_ref, wa_ref, ba_ref, mu_ref, w0_ref, a0_ref,
                     kk_ref, ka_ref, rk_ref, w2_ref, a2_ref, g2_ref,
                     pack_out, m_out, g_out, bonus_out):
    ones = _head_ones()
    sm = proj_ref[:, OFF_SMALL:OFF_SMALL + SMALL_W]
    pre = sm + brow_ref[...]
    logf = pltpu.roll(_log_sigmoid(pre), LANES - N_HEADS, 1)
    m_old = m_ref[...]
    m_new = jnp.maximum(logf + m_old, pre)
    m_out[...] = m_new
    lane = _iota((SMALL_W, W_GROUP), 0)
    spread_mat = jnp.where(lane == LANE_MI + (_iota((SMALL_W, W_GROUP), 1) >> HEAD_SHIFT), 1.0, 0.0).astype(BF16)
    gate_lanes = (_iota(pre.shape, 1) >= LANE_MI) & (_iota(pre.shape, 1) < LANE_MI + N_HEADS)
    spread = lambda x: _dot_exact_lhs(jnp.where(gate_lanes, x, 0.0), spread_mat)
    la = _log_sigmoid(_dot(sm, wa_ref[...]) + ba_ref[...]) * (1.0 / GLA_GATE_TEMP)
    rc = proj_ref[:, 0:RWKV_COLS]
    rr, lw, kt, rv, kk, a, g, bonus = _rwkv_vectors(
        rc, shift_ref[...], mu_ref[...], w0_ref[...], a0_ref[...], kk_ref[...], ka_ref[...],
        rk_ref[...], w2_ref[...], a2_ref[...], g2_ref[...], ones)
    main = lambda blk: proj_ref[:, blk * W_GROUP:(blk + 1) * W_GROUP]
    rows = {
        ROW_MQ: main(BLK_MQ), ROW_MK: main(BLK_MK), ROW_MV: main(BLK_MV),
        ROW_GQ: main(BLK_GQ), ROW_GK: main(BLK_GK), ROW_GV: main(BLK_GV), ROW_GA: jnp.exp(la),
        ROW_KK: kk, ROW_AKK: a * kk, ROW_W: jnp.exp(lw), ROW_KT: kt, ROW_RR: rr, ROW_RV: rv,
        ROW_IW: spread(jnp.exp(pre - m_new)), ROW_FW: spread(jnp.exp(logf + m_old - m_new)),
        ROW_M: spread(m_new),
    }
    for i, val in rows.items():
        pack_out[i] = val.T
    g_out[...] = g
    bonus_out[...] = bonus


def _step_vectors(proj, shift, m_pad, p):
    n = proj.shape[0]
    full = lambda a: pl.BlockSpec(a.shape, lambda i: (0,) * a.ndim)
    data = (proj, shift, m_pad)
    prms = [p[k] for k in ("ml_brow", "gla_wa", "gla_ba", "rw_mu", "rw_w0", "rw_a0", "rw_kk", "rw_ka",
                           "rw_rk", "rw_w2", "rw_a2", "rw_g2")]
    wide = jax.ShapeDtypeStruct((n, W_GROUP), F32)
    outs = [jax.ShapeDtypeStruct((N_PACK, W_GROUP, n), F32), jax.ShapeDtypeStruct((n, SMALL_W), F32),
            wide, wide]
    return pl.pallas_call(
        _step_vec_kernel,
        grid=(1,),
        in_specs=[full(a) for a in data] + [q.spec() for q in prms],
        out_specs=[pl.BlockSpec(o.shape, lambda i, nd=len(o.shape): (0,) * nd) for o in outs],
        out_shape=outs,
        compiler_params=_cparams("arbitrary"),
        name="step_vectors",
    )(*data, *[q.arr for q in prms])


(ROW_MQ, ROW_MK, ROW_MV, ROW_GQ, ROW_GK, ROW_GV, ROW_GA, ROW_KK, ROW_AKK, ROW_W, ROW_KT, ROW_RR, ROW_RV,
 ROW_IW, ROW_FW, ROW_M) = range(16)
N_PACK = 16
(OROW_ML, OROW_GLA, OROW_RW, OROW_N) = range(4)
N_OPACK = 4
STEP_UNROLL = 4


def _heads_step_kernel(pack_ref, c_ref, n_ref, gs_ref, rs_ref, c_out, gs_out, rs_out, opack_out):
    vec = lambda i: pack_ref[i]
    one = lambda i, j: pack_ref[i, pl.ds(j, 1), :]
    colsum = lambda x: jnp.sum(x, axis=0, keepdims=True)

    k_ml = vec(ROW_MK) * (HEAD_DIM ** -0.5)
    q_ml = vec(ROW_MQ)
    iw = pack_ref[ROW_IW, 0:1, :]
    fw = pack_ref[ROW_FW, 0:1, :]
    n_new = fw * n_ref[...] + iw * k_ml
    opack_out[OROW_N] = n_new
    r_den = 1.0 / jnp.maximum(jnp.abs(colsum(n_new * q_ml)), jnp.exp(-pack_ref[ROW_M, 0:1, :]))

    def ml_body(v, carry):
        c_new = fw * c_ref[v] + (iw * one(ROW_MV, v)) * k_ml
        c_out[v] = c_new
        opack_out[OROW_ML, pl.ds(v, 1), :] = colsum(c_new * q_ml) * r_den
        return carry

    lax.fori_loop(0, HEAD_DIM, ml_body, 0, unroll=STEP_UNROLL)

    v_gla = vec(ROW_GV)

    def gla_body(k, acc):
        s_new = one(ROW_GA, k) * gs_ref[k] + one(ROW_GK, k) * v_gla
        gs_out[k] = s_new
        return acc + (one(ROW_GQ, k) * (HEAD_DIM ** -0.5)) * s_new

    opack_out[OROW_GLA] = lax.fori_loop(0, HEAD_DIM, gla_body, jnp.zeros_like(v_gla), unroll=STEP_UNROLL)

    kk, akk, w, kt, rr = vec(ROW_KK), vec(ROW_AKK), vec(ROW_W), vec(ROW_KT), vec(ROW_RR)

    def rw_body(v, carry):
        s = rs_ref[v]
        s_new = s * w - colsum(s * kk) * akk + one(ROW_RV, v) * kt
        rs_out[v] = s_new
        opack_out[OROW_RW, pl.ds(v, 1), :] = colsum(s_new * rr)
        return carry

    lax.fori_loop(0, HEAD_DIM, rw_body, 0, unroll=STEP_UNROLL)


def _heads_step(pack, ml_c, ml_n, gla_s, rw_s, layer):
    n = pack.shape[-1]
    sq = (HEAD_DIM, HEAD_DIM, n)
    head = lambda lead: pl.BlockSpec((lead, HEAD_DIM, n), lambda h: (0, h, 0))
    at_layer = lambda shape: pl.BlockSpec((None,) + shape, lambda h: (layer * N_HEADS + h,) + (0,) * len(shape))
    out_state = pl.BlockSpec((None,) + sq, lambda h: (h, 0, 0, 0))
    state = jax.ShapeDtypeStruct((N_HEADS,) + sq, F32)
    return pl.pallas_call(
        _heads_step_kernel,
        grid=(N_HEADS,),
        in_specs=[head(N_PACK), at_layer(sq), at_layer((HEAD_DIM, n)), at_layer(sq), at_layer(sq)],
        out_specs=[out_state, out_state, out_state, head(N_OPACK)],
        out_shape=[state, state, state, jax.ShapeDtypeStruct((N_OPACK, W_GROUP, n), F32)],
        compiler_params=_cparams("arbitrary"),
        name="heads_step",
    )(pack, ml_c, ml_n, gla_s, rw_s)


def _step_post_kernel(hml_ref, mo_ref, ogla_ref, gg_ref, orw_ref, g_ref, bonus_ref,
                      mln_ref, gln_ref, rwn_ref, yml_out, ygla_out, yrw_out):
    ones = _head_ones()
    hml = hml_ref[...].T
    ms = _head_sum(hml * hml, ones) * (1.0 / HEAD_DIM)
    yml_out[...] = (hml * lax.rsqrt(ms + EPS) * mln_ref[...] * _sigmoid(mo_ref[...])).astype(BF16)
    og = ogla_ref[...].T
    ms = _head_sum(og * og, ones) * (1.0 / HEAD_DIM)
    gg = gg_ref[...]
    ygla_out[...] = (og * lax.rsqrt(ms + EPS) * gln_ref[...] * (gg * _sigmoid(gg))).astype(BF16)
    yrw_out[...] = ((_head_layernorm(orw_ref[...].T, rwn_ref[...], ones) + bonus_ref[...])
                    * g_ref[...]).astype(BF16)


def _step_post(hml, mo, ogla, gg, orw, g, bonus, p):
    n = mo.shape[0]
    data = (hml, mo, ogla, gg, orw, g, bonus)
    prms = [p[k] for k in ("ml_norm", "gla_norm", "rw_norm")]
    full = lambda a: pl.BlockSpec(a.shape, lambda i: (0,) * a.ndim)
    out = jax.ShapeDtypeStruct((n, W_GROUP), BF16)
    return pl.pallas_call(
        _step_post_kernel,
        grid=(1,),
        in_specs=[full(a) for a in data] + [q.spec() for q in prms],
        out_specs=[pl.BlockSpec((n, W_GROUP), lambda i: (0, 0))] * 3,
        out_shape=[out] * 3,
        compiler_params=_cparams("arbitrary"),
        name="step_post",
    )(*data, *[q.arr for q in prms])


def _prepare_params(prm):
    P = {}
    w_in = prm["w_in"].astype(BF16)
    depth, d = w_in.shape[0], w_in.shape[1]
    sizes = (W_GROUP, W_GROUP, W_GROUP, W_GROUP, N_HEADS, N_HEADS, W_GROUP, W_GROUP, W_GROUP, W_GROUP,
             GLA_RANK, W_GROUP, RWKV_COLS)
    offs = [0]
    for sz in sizes:
        offs.append(offs[-1] + sz)
    col = lambda i: w_in[:, :, offs[i]:offs[i + 1]]
    (u, mq, mk, mv, mi, mf, mo, gq, gk, gv, ga, gg, rcols) = [col(i) for i in range(13)]
    small = jnp.concatenate([ga, mi, mf, jnp.zeros((depth, d, SMALL_W - GLA_RANK - 2 * N_HEADS), BF16)], axis=2)
    P["w_in"] = jnp.concatenate([rcols, small, u, mq, mk, mv, mo, gq, gk, gv, gg], axis=2)
    P["w_gate_t"] = jnp.swapaxes(jnp.concatenate([mi, mf], axis=2), 1, 2)

    lam = lax.complex(prm["s5_lam_re"], prm["s5_lam_im"])
    dt = jnp.exp(prm["s5_log_dt"])[..., None]
    lam_bar = jnp.exp(lam * dt)
    b_bar = ((lam_bar - 1.0) / lam)[..., None] * lax.complex(prm["s5_b_re"], prm["s5_b_im"])
    eye = jnp.eye(S5_GROUPS, dtype=F32)
    bm = lambda b: jnp.einsum("lgph,gk->lghkp", b, eye).reshape(depth, W_GROUP, S5_WIDTH)
    P["bmat"] = jnp.concatenate([bm(b_bar.real), bm(b_bar.imag)], axis=2).astype(BF16)
    cm = lambda c: jnp.einsum("lghp,gk->lkpgh", c, eye).reshape(depth, S5_WIDTH, W_GROUP)
    P["cmat"] = jnp.concatenate([cm(prm["s5_c_re"]), -cm(prm["s5_c_im"])], axis=1).astype(BF16)
    P["lam"] = jnp.stack([lam_bar.real.reshape(depth, S5_WIDTH), lam_bar.imag.reshape(depth, S5_WIDTH)],
                         axis=1)[:, :, None, :]
    P["wglu"] = prm["s5_w_glu"].astype(BF16)

    bias = prm["ml_gate_bias"]
    P["ml_brow"] = jnp.pad(bias, ((0, 0), (LANE_MI, SMALL_W - LANE_MI - 2 * N_HEADS)))[:, None, :]
    P["ml_bcol"] = jnp.broadcast_to(bias[:, :, None], (depth, 2 * N_HEADS, CHUNK_ML))

    P["rows"] = jnp.stack([prm["s5_d"], prm["ml_norm"], prm["gla_b_alpha"], prm["gla_norm"], prm["rw_w0"],
                           prm["rw_a0"], prm["rw_k_k"], prm["rw_k_a"], prm["rw_r_k"].reshape(depth, W_GROUP),
                           prm["rw_norm"]], axis=1)[:, :, None, :]
    pad_rows = lambda w, lo: jnp.pad(w, ((0, 0), (lo, LANES - lo - w.shape[1]), (0, 0)))
    P["pads"] = jnp.stack([pad_rows(prm["gla_w_alpha"], 0), pad_rows(prm["rw_w2"], 0),
                           pad_rows(prm["rw_a2"], RWKV_W_RANK),
                           pad_rows(prm["rw_g2"], RWKV_W_RANK + RWKV_A_RANK)], axis=1).astype(BF16)
    P["rw_mu"] = prm["rw_mu"][:, None, :]
    P["norms"] = jnp.stack([prm["norm_mix"], prm["norm_ffn"]], axis=1)[:, :, None, :]
    P["w_out"] = prm["w_out"].astype(BF16)
    P["ffn_wup"] = prm["ffn_w_up"].astype(BF16)
    P["ffn_cw"] = prm["ffn_conv_w"]
    P["ffn_cb"] = prm["ffn_conv_b"][:, None, :]
    P["ffn_wdn"] = prm["ffn_w_down"].astype(BF16)
    return P


ROW_NAMES = ("d", "ml_norm", "gla_ba", "gla_norm", "rw_w0", "rw_a0", "rw_kk", "rw_ka", "rw_rk", "rw_norm")
PAD_NAMES = ("gla_wa", "rw_w2", "rw_a2", "rw_g2")


def _layer_params(P, l):
    p = {k: _Param(P[k], l) for k in ("w_in", "w_gate_t", "bmat", "cmat", "wglu", "ml_brow", "ml_bcol",
                                      "rw_mu", "w_out", "ffn_wup", "ffn_cw", "ffn_cb", "ffn_wdn")}
    p.update({k: _Param(P["rows"], l, i) for i, k in enumerate(ROW_NAMES)})
    p.update({k: _Param(P["pads"], l, i) for i, k in enumerate(PAD_NAMES)})
    p["lre"], p["lim"] = _Param(P["lam"], l, 0), _Param(P["lam"], l, 1)
    p["norm_mix"], p["norm_ffn"] = _Param(P["norms"], l, 0), _Param(P["norms"], l, 1)
    return p


def _prompt_layer(x2d, p, norm_final, n_seq, t_len, final):
    proj, gt, u = _inproj(x2d, p["norm_mix"], p["w_in"], p["w_gate_t"], n_seq, t_len, tm=1024)

    u_tb = u.reshape(n_seq, t_len, W_GROUP).transpose(1, 0, 2)
    zeros = jnp.zeros((n_seq, S5_WIDTH), F32)
    y_s5_tb, s5_re, s5_im = _s5(u_tb.reshape(t_len * n_seq, W_GROUP), zeros, zeros, p, n_seq, t_len, tt=128)
    y_s5 = y_s5_tb.reshape(t_len, n_seq, W_GROUP).transpose(1, 0, 2).reshape(n_seq * t_len, W_GROUP)

    y_ml, ml_c, ml_n, ml_m = _mlstm(proj, gt, p, n_seq, t_len)
    y_gla, gla_st = _gla(proj, p, n_seq, t_len)
    y_rw, rw_s, rw_shift = _rwkv(proj, p, n_seq, t_len)

    x_new, ffn_buf = _ffn_seq(x2d, (y_s5, y_ml, y_gla, y_rw), p, norm_final, n_seq, t_len, tm=512,
                              final=final)
    states = (
        s5_re.reshape(n_seq, S5_GROUPS, S5_STATE),
        s5_im.reshape(n_seq, S5_GROUPS, S5_STATE),
        ml_c,
        ml_n[:, 0, :].reshape(n_seq, N_HEADS, HEAD_DIM),
        ml_m[:, 0, LANE_MI:LANE_MI + N_HEADS],
        gla_st,
        rw_s,
        rw_shift[:, 0, :],
        ffn_buf,
    )
    return x_new, states


def _sample_layer(x2d, st, pair_states, layer, p, norm_final, final):
    (s5_re, s5_im, ml_m, rw_shift, ffn_buf) = st
    n = x2d.shape[0]
    npairs = n * N_HEADS
    proj, _, u = _inproj(x2d, p["norm_mix"], p["w_in"], p["w_gate_t"], 1, n, tm=n)

    y_s5, s5_re_new, s5_im_new = _s5(u,
                                     s5_re.reshape(n, S5_WIDTH), s5_im.reshape(n, S5_WIDTH),
                                     p, n, 1, tt=1)

    m_pad = jnp.zeros((n, SMALL_W), F32).at[:, LANE_MI:LANE_MI + N_HEADS].set(ml_m)
    pack, m_new, g, bonus = _step_vectors(proj, rw_shift, m_pad, p)
    ml_c_new, gla_s_new, rw_s_new, opack = _heads_step(pack, *pair_states, layer)

    main = lambda blk: proj[:, blk * W_GROUP:(blk + 1) * W_GROUP]
    y_ml, y_gla, y_rw = _step_post(opack[OROW_ML], main(BLK_MO), opack[OROW_GLA], main(BLK_GG),
                                   opack[OROW_RW], g, bonus, p)
    x_new, ug = _ffn_step(x2d, (y_s5, y_ml, y_gla, y_rw), p, norm_final, ffn_buf[:, 0, :], ffn_buf[:, 1, :],
                          final)
    states = (
        s5_re_new.reshape(n, S5_GROUPS, S5_STATE),
        s5_im_new.reshape(n, S5_GROUPS, S5_STATE),
        ml_c_new.transpose(3, 0, 1, 2),
        opack[OROW_N].reshape(N_HEADS, HEAD_DIM, n).transpose(2, 0, 1),
        m_new[:, LANE_MI:LANE_MI + N_HEADS],
        gla_s_new.transpose(3, 0, 1, 2),
        rw_s_new.transpose(3, 0, 1, 2),
        proj[:, 0:RWKV_COLS],
        jnp.stack([ffn_buf[:, 1, :], ug], axis=1),
    )
    return x_new, states


def kernel(x_prompt, x_sample, state_s5_re, state_s5_im, state_mlstm_C, state_mlstm_n, state_mlstm_m, state_gla_S, state_rwkv_S, state_rwkv_shift, state_ffn_conv, norm_mix, w_in, s5_lam_re, s5_lam_im, s5_log_dt, s5_b_re, s5_b_im, s5_c_re, s5_c_im, s5_d, s5_w_glu, ml_gate_bias, ml_norm, gla_w_alpha, gla_b_alpha, gla_norm, rw_mu, rw_w0, rw_w2, rw_a0, rw_a2, rw_g2, rw_k_k, rw_k_a, rw_r_k, rw_norm, w_out, norm_ffn, ffn_w_up, ffn_conv_w, ffn_conv_b, ffn_w_down, norm_final):
    prm = dict(norm_mix=norm_mix, w_in=w_in, s5_lam_re=s5_lam_re, s5_lam_im=s5_lam_im, s5_log_dt=s5_log_dt,
               s5_b_re=s5_b_re, s5_b_im=s5_b_im, s5_c_re=s5_c_re, s5_c_im=s5_c_im, s5_d=s5_d,
               s5_w_glu=s5_w_glu, ml_gate_bias=ml_gate_bias, ml_norm=ml_norm, gla_w_alpha=gla_w_alpha,
               gla_b_alpha=gla_b_alpha, gla_norm=gla_norm, rw_mu=rw_mu, rw_w0=rw_w0, rw_w2=rw_w2,
               rw_a0=rw_a0, rw_a2=rw_a2, rw_g2=rw_g2, rw_k_k=rw_k_k, rw_k_a=rw_k_a, rw_r_k=rw_r_k,
               rw_norm=rw_norm, w_out=w_out, norm_ffn=norm_ffn, ffn_w_up=ffn_w_up, ffn_conv_w=ffn_conv_w,
               ffn_conv_b=ffn_conv_b, ffn_w_down=ffn_w_down)
    depth = w_in.shape[0]
    n_seq, t_len, d = x_prompt.shape
    n_smp = x_sample.shape[0]
    assert t_len % 1024 == 0 and n_seq % SEQS_PER_STEP == 0 and x_sample.shape[1] == 1
    nf = _Param(norm_final[None, :])
    small_states = (state_s5_re, state_s5_im, state_mlstm_m, state_rwkv_shift, state_ffn_conv)
    lead = depth * N_HEADS
    to_lanes = lambda st: jnp.moveaxis(st, 1, -1)
    pair_states = (to_lanes(state_mlstm_C).reshape(lead, HEAD_DIM, HEAD_DIM, n_smp),
                   to_lanes(state_mlstm_n).reshape(lead, HEAD_DIM, n_smp),
                   to_lanes(state_gla_S).reshape(lead, HEAD_DIM, HEAD_DIM, n_smp),
                   to_lanes(state_rwkv_S).reshape(lead, HEAD_DIM, HEAD_DIM, n_smp))

    P = _prepare_params(prm)
    xp = x_prompt.reshape(n_seq * t_len, d)
    xs = x_sample.reshape(n_smp, d)
    p_states, s_states = [], []
    for l in range(depth):
        p = _layer_params(P, l)
        final = l == depth - 1
        xp, ps = _prompt_layer(xp, p, nf, n_seq, t_len, final)
        xs, ss = _sample_layer(xs, tuple(s[l] for s in small_states), pair_states, l, p, nf, final)
        p_states.append(ps)
        s_states.append(ss)
    new_p = tuple(jnp.stack([st[i] for st in p_states]) for i in range(9))
    new_s = tuple(jnp.stack([st[i] for st in s_states]) for i in range(9))
    return (xp.reshape(n_seq, t_len, d), xs.reshape(n_smp, 1, d)) + new_p + new_s
```

```python
import functools
import math

import jax
import jax.numpy as jnp
from jax import lax
from jax.experimental import pallas as pl
from jax.experimental.pallas import tpu as pltpu

F32 = jnp.float32
BF16 = jnp.bfloat16

LANES = 128
W_GROUP = 256
HEAD_DIM = 64
HEAD_SHIFT = 6
N_HEADS = 4
S5_GROUPS = 16
S5_STATE = 64
S5_WIDTH = S5_GROUPS * S5_STATE
GLA_RANK = 16
GLA_GATE_TEMP = 16.0
RWKV_W_RANK = 32
RWKV_A_RANK = 32
RWKV_G_RANK = 64
RWKV_COLS = 3 * W_GROUP + RWKV_W_RANK + RWKV_A_RANK + RWKV_G_RANK
RWKV_DECAY_SCALE = 0.6065306597126334
CONV_W = 3
EPS = 1e-6
GN_EPS = 64e-5
NEG_BIG = -1e30
EXP_CLAMP = 80.0

SMALL_W = LANES
OFF_SMALL = RWKV_COLS
OFF_MAIN = RWKV_COLS + SMALL_W
D_PROJ = OFF_MAIN + 9 * W_GROUP
LANE_MI = GLA_RANK
LANE_MF = GLA_RANK + N_HEADS
(BLK_U, BLK_MQ, BLK_MK, BLK_MV, BLK_MO, BLK_GQ, BLK_GK, BLK_GV, BLK_GG) = range(
    OFF_MAIN // W_GROUP, OFF_MAIN // W_GROUP + 9)
BLK_SMALL = OFF_SMALL // SMALL_W

CHUNK_ML = 128
CHUNK_RW = 64
SEQS_PER_STEP = 8
SEQS_PER_STEP_ML = 8
INPROJ_ROWS = 1024
FFN_ROWS = 512
S5_TILE_STEPS = 128
VMEM_LIMIT = 56 * 1024 * 1024


def _cparams(*sem):
    return pltpu.CompilerParams(dimension_semantics=sem, vmem_limit_bytes=VMEM_LIMIT)


class _Param:
    def __init__(self, arr, *lead):
        self.arr, self.lead = arr, tuple(lead)

    @property
    def shape(self):
        return tuple(self.arr.shape[len(self.lead):])

    def spec(self, **kw):
        lead, tail = self.lead, (0,) * len(self.shape)
        return pl.BlockSpec((None,) * len(lead) + self.shape, lambda *grid_idx: lead + tail, **kw)


def _dot(a, b):
    return jnp.dot(a.astype(BF16), b.astype(BF16), preferred_element_type=F32)


def _dot_nt(a, b):
    return lax.dot_general(a.astype(BF16), b.astype(BF16), (((1,), (1,)), ((), ())),
                           preferred_element_type=F32)


def _dot_tn(a, b):
    return lax.dot_general(a.astype(BF16), b.astype(BF16), (((0,), (0,)), ((), ())),
                           preferred_element_type=F32)


def _split2(x):
    hi = x.astype(BF16)
    lo = (x - hi.astype(F32)).astype(BF16)
    return hi, lo


def _dot_exact_rhs(a01, x):
    hi, lo = _split2(x)
    f = lambda p: jnp.dot(a01, p, preferred_element_type=F32)
    return f(hi) + f(lo)


def _dot_exact_lhs(x, b01):
    hi, lo = _split2(x)
    f = lambda p: jnp.dot(p, b01, preferred_element_type=F32)
    return f(hi) + f(lo)


def _sigmoid(x):
    return 1.0 / (1.0 + jnp.exp(-x))


def _log_sigmoid(x):
    return jnp.minimum(x, 0.0) - jnp.log(1.0 + jnp.exp(-jnp.abs(x)))


def _gelu(x):
    return 0.5 * x * (1.0 + jnp.tanh(math.sqrt(2.0 / math.pi) * (x + 0.044715 * (x * x * x))))


def _iota(shape, dim):
    return lax.broadcasted_iota(jnp.int32, shape, dim)


def _head_ones():
    r = _iota((W_GROUP, W_GROUP), 0) >> HEAD_SHIFT
    c = _iota((W_GROUP, W_GROUP), 1) >> HEAD_SHIFT
    return jnp.where(r == c, 1.0, 0.0).astype(BF16)


def _head_sum(x, ones):
    return _dot_exact_lhs(x, ones)


def _rmsnorm_rows(x, g):
    ms = jnp.mean(x * x, axis=-1, keepdims=True)
    return x * lax.rsqrt(ms + EPS) * g


def _tril_ones(n, strict=False):
    r = _iota((n, n), 0)
    c = _iota((n, n), 1)
    return jnp.where((r > c) if strict else (r >= c), 1.0, 0.0).astype(BF16)


def _triu_ones(n):
    r = _iota((n, n), 0)
    c = _iota((n, n), 1)
    return jnp.where(r <= c, 1.0, 0.0).astype(BF16)


def _store_head_blocks(out_ref, scr_ref, transpose=False):
    for g in range(scr_ref.shape[0]):
        m = scr_ref[g].T if transpose else scr_ref[g]
        for h in range(N_HEADS):
            lo, hi = h * HEAD_DIM, (h + 1) * HEAD_DIM
            out_ref[g, h] = m[lo:hi, lo:hi]


def _HEAD_BLOCKS_SPEC(G):
    return pl.BlockSpec((G, N_HEADS, HEAD_DIM, HEAD_DIM), lambda b, c: (b, 0, 0, 0))


def _inproj_kernel(x_ref, g_ref, w_ref, wg_ref, proj_ref, gt_ref, u_ref):
    hn = _rmsnorm_rows(x_ref[...], g_ref[...]).astype(BF16)
    proj = jnp.dot(hn, w_ref[...], preferred_element_type=F32)
    proj_ref[...] = proj
    u_ref[...] = proj[:, OFF_MAIN:OFF_MAIN + W_GROUP]
    gt_ref[0] = lax.dot_general(wg_ref[...], hn, (((1,), (1,)), ((), ())),
                                preferred_element_type=F32)


def _inproj(x2d, g, w, wg, n_seq, t_len, tm):
    n, d = x2d.shape
    tiles_per_seq = t_len // tm
    return pl.pallas_call(
        _inproj_kernel,
        grid=(n // tm,),
        in_specs=[
            pl.BlockSpec((tm, d), lambda i: (i, 0)),
            g.spec(), w.spec(pipeline_mode=pl.Buffered(1)), wg.spec(),
        ],
        out_specs=[
            pl.BlockSpec((tm, D_PROJ), lambda i: (i, 0)),
            pl.BlockSpec((1, 8, tm), lambda i: (i // tiles_per_seq, 0, i % tiles_per_seq)),
            pl.BlockSpec((tm, W_GROUP), lambda i: (i, 0)),
        ],
        out_shape=[
            jax.ShapeDtypeStruct((n, D_PROJ), F32),
            jax.ShapeDtypeStruct((n_seq, 8, t_len), F32),
            jax.ShapeDtypeStruct((n, W_GROUP), F32),
        ],
        compiler_params=_cparams("arbitrary"),
        name="inproj",
    )(x2d, g.arr, w.arr, wg.arr)


def _s5_kernel(u_ref, unext_ref, h0re_ref, h0im_ref, bmat_ref, lre_ref, lim_ref, cmat_ref, d_ref, wglu_ref,
               y_ref, hre_out, him_out, bu_scr, hs_scr, hre_scr, him_scr, *, bp, tt):
    i = pl.program_id(0)
    nt = S5_WIDTH // LANES

    def project(u_tile, slot):
        bu = jnp.dot(u_tile.astype(BF16), bmat_ref[...], preferred_element_type=F32)
        for j in range(2 * nt):
            bu_scr[slot, j] = bu[:, j * LANES:(j + 1) * LANES]

    @pl.when(i == 0)
    def _():
        hre_scr[...] = h0re_ref[...]
        him_scr[...] = h0im_ref[...]
        project(u_ref[...], 0)

    slot = lax.rem(i, 2)
    lre = [lre_ref[:, j * LANES:(j + 1) * LANES] for j in range(nt)]
    lim = [lim_ref[:, j * LANES:(j + 1) * LANES] for j in range(nt)]

    def step(t, carry):
        rows = pl.ds(pl.multiple_of(t * bp, bp), bp)
        new = []
        for j in range(nt):
            hre, him = carry[j], carry[nt + j]
            new.append((lre[j] * hre - lim[j] * him + bu_scr[slot, j, rows, :],
                        lre[j] * him + lim[j] * hre + bu_scr[slot, nt + j, rows, :]))
        out = tuple(x[0] for x in new) + tuple(x[1] for x in new)
        for j in range(2 * nt):
            hs_scr[j, rows, :] = out[j]
        return out

    carry = tuple(hre_scr[:, j * LANES:(j + 1) * LANES] for j in range(nt)) + tuple(
        him_scr[:, j * LANES:(j + 1) * LANES] for j in range(nt))
    carry = step(0, carry) if tt == 1 else lax.fori_loop(0, tt, step, carry)
    hre = jnp.concatenate(carry[0:nt], axis=1)
    him = jnp.concatenate(carry[nt:2 * nt], axis=1)
    hre_scr[...] = hre
    him_scr[...] = him
    hre_out[...] = hre
    him_out[...] = him

    project(unext_ref[...], lax.rem(i + 1, 2))
    u = u_ref[...]
    hs = jnp.concatenate([hs_scr[j] for j in range(2 * nt)], axis=1)
    y = jnp.dot(hs.astype(BF16), cmat_ref[...], preferred_element_type=F32) + d_ref[...] * u
    z = _gelu(y)
    y_ref[...] = (z * _sigmoid(_dot(z, wglu_ref[...]))).astype(y_ref.dtype)


def _s5(u_tb, h0re, h0im, sp, bp, t_len, tt):
    n = u_tb.shape[0]
    nsteps = t_len // tt
    full = lambda shape: pl.BlockSpec(shape, lambda i: (0,) * len(shape))
    tile = pl.BlockSpec((tt * bp, W_GROUP), lambda i: (i, 0))
    next_tile = pl.BlockSpec((tt * bp, W_GROUP), lambda i: (jnp.minimum(i + 1, nsteps - 1), 0))
    return pl.pallas_call(
        functools.partial(_s5_kernel, bp=bp, tt=tt),
        grid=(nsteps,),
        in_specs=[
            tile, next_tile,
            full((bp, S5_WIDTH)), full((bp, S5_WIDTH)),
            sp["bmat"].spec(), sp["lre"].spec(), sp["lim"].spec(), sp["cmat"].spec(), sp["d"].spec(),
            sp["wglu"].spec(),
        ],
        out_specs=[
            tile,
            full((bp, S5_WIDTH)), full((bp, S5_WIDTH)),
        ],
        out_shape=[
            jax.ShapeDtypeStruct((n, W_GROUP), BF16),
            jax.ShapeDtypeStruct((bp, S5_WIDTH), F32),
            jax.ShapeDtypeStruct((bp, S5_WIDTH), F32),
        ],
        scratch_shapes=[
            pltpu.VMEM((2, 2 * S5_WIDTH // LANES, tt * bp, LANES), F32),
            pltpu.VMEM((2 * S5_WIDTH // LANES, tt * bp, LANES), F32),
            pltpu.VMEM((bp, S5_WIDTH), F32),
            pltpu.VMEM((bp, S5_WIDTH), F32),
        ],
        compiler_params=_cparams("arbitrary"),
        name="s5_scan",
    )(u_tb, u_tb, h0re, h0im, sp["bmat"].arr, sp["lre"].arr, sp["lim"].arr, sp["cmat"].arr, sp["d"].arr,
      sp["wglu"].arr)


def _run_interleaved(chains):
    live = list(chains)
    while live:
        still = []
        for ch in live:
            try:
                next(ch)
                still.append(ch)
            except StopIteration:
                pass
        live = still


def _mlstm_kernel(q_ref, k_ref, v_ref, og_ref, sm_ref, gt_ref, brow_ref, bcol_ref, norm_ref,
                  y_ref, c_out, n_out, m_out, c_scr, n_scr, m_scr, *, L, nc, G):
    c = pl.program_id(1)

    @pl.when(c == 0)
    def _():
        c_scr[...] = jnp.zeros_like(c_scr)
        n_scr[...] = jnp.zeros_like(n_scr)
        m_scr[...] = jnp.zeros_like(m_scr)

    ones = _head_ones()
    tril = _tril_ones(L)
    triu = _triu_ones(L)
    l_shift = L.bit_length() - 1
    src = _iota((LANES, N_HEADS * L), 0)
    to_scores = jnp.where(src == LANE_MI + (_iota((LANES, N_HEADS * L), 1) >> l_shift), 1.0, 0.0).astype(BF16)
    src = _iota((LANES, W_GROUP), 0)
    to_feats = jnp.where(src == LANE_MI + (_iota((LANES, W_GROUP), 1) >> HEAD_SHIFT), 1.0, 0.0).astype(BF16)
    seg = _iota((N_HEADS * L, W_GROUP), 0) >> l_shift
    score_sum = jnp.where(seg == (_iota((N_HEADS * L, W_GROUP), 1) >> HEAD_SHIFT), 1.0, 0.0).astype(BF16)
    causal = _iota((L, N_HEADS * L), 0) >= (_iota((L, N_HEADS * L), 1) & (L - 1))
    t_idx = _iota((L, LANES), 0)

    seqs = list(range(G))
    cat = lambda xs: jnp.concatenate(xs, axis=0)
    split = lambda x: [x[i * L:(i + 1) * L] for i in seqs]

    def running_max(x):
        shift = 1
        while shift < L:
            x = jnp.maximum(x, jnp.where(t_idx >= shift, pltpu.roll(x, shift, 0), NEG_BIG))
            shift *= 2
        return x

    q = [q_ref[i] for i in seqs]
    k = [k_ref[i] * (HEAD_DIM ** -0.5) for i in seqs]
    v = [v_ref[i] for i in seqs]
    C = [c_scr[i] for i in seqs]
    n = [n_scr[i, 0:1, :] for i in seqs]
    m0 = [m_scr[i, 0:1, :] for i in seqs]
    pre = [sm_ref[i] + brow_ref[...] for i in seqs]
    gtb = [gt_ref[i] + bcol_ref[...] for i in seqs]
    bc_col = _each(lambda x: _dot_exact_rhs(tril, _log_sigmoid(x)), pre)
    bc_rows = _dot_exact_lhs(_log_sigmoid(cat(gtb)), triu)
    qC = _each(_dot_nt, q, C)
    qk = _each(lambda q_, k_: _dot_nt(q_, _stack_heads(k_)), q, k)
    qn = _head_sum(cat(_each(lambda q_, n_: q_ * n_, q, n)), ones)

    b_col = _each(lambda x: pltpu.roll(x, LANES - N_HEADS, 1), bc_col)
    r_col = _each(lambda p_, b_: p_ - b_, pre, b_col)
    mu = _each(lambda r_, m_: jnp.maximum(running_max(r_), m_), r_col, m0)
    mu_last = _each(lambda x: x[L - 1:L, :], mu)
    mu_scores = split(_dot_exact_lhs(cat(mu), to_scores))

    def weights(i):
        r_rows = gtb[i][0:N_HEADS, :] - bc_rows[8 * i + N_HEADS:8 * i + 2 * N_HEADS, :]
        r_all = jnp.concatenate([r_rows[h:h + 1, :] for h in range(N_HEADS)], axis=1)
        return jnp.exp(jnp.where(causal, r_all - mu_scores[i], NEG_BIG)) * qk[i]

    ws = [weights(i) for i in seqs]
    wsv = _each(lambda w_, v_: _dot(w_, _stack_heads(v_)), ws, v)
    w0 = _dot(cat(_each(lambda m_, u_: jnp.exp(m_ - u_), m0, mu)), to_feats)
    m_row = _dot_exact_lhs(cat(_each(lambda b_, u_: b_ + u_, b_col, mu)), to_feats)
    den = w0 * qn + _dot(cat(ws), score_sum)
    hc = (w0 * cat(qC) + cat(wsv)) / jnp.maximum(jnp.abs(den), jnp.exp(-m_row))

    wl = split(_dot(cat(_each(lambda r_, u_: jnp.exp(r_ - u_), r_col, mu_last)), to_feats))
    w0l_rows = _each(lambda m_, u_: jnp.exp(m_ - u_), m0, mu_last)
    w0l_rows = w0l_rows + [jnp.zeros((1, LANES), F32)] * (-G % 8)
    w0l = _dot_exact_lhs(cat(w0l_rows), to_feats)
    kw = _each(lambda k_, w_: k_ * w_, k, wl)
    upd = _each(_dot_tn, v, kw)
    for i in seqs:
        c_scr[i] = w0l[i:i + 1, :] * C[i] + ones.astype(F32) * upd[i]
        n_scr[i, 0:1, :] = w0l[i:i + 1, :] * n[i] + jnp.sum(kw[i], axis=0, keepdims=True)
        m_scr[i, 0:1, :] = b_col[i][L - 1:L, :] + mu_last[i]

    ms = _head_sum(hc * hc, ones) * (1.0 / HEAD_DIM)
    y = split(hc * lax.rsqrt(ms + EPS) * norm_ref[...])
    for i in seqs:
        y_ref[i] = (y[i] * _sigmoid(og_ref[i])).astype(y_ref.dtype)

    @pl.when(c == nc - 1)
    def _():
        _store_head_blocks(c_out, c_scr)
        n_out[...] = jnp.broadcast_to(n_scr[:, 0:1, :], n_out.shape)
        m_out[...] = m_scr[...]


def _seq_spec(G, L, blk, width=W_GROUP):
    return pl.BlockSpec((G, L, width), lambda b, c: (b, c, blk))


def _mlstm(proj, gt, p, n_seq, t_len):
    L = CHUNK_ML
    G = SEQS_PER_STEP_ML
    nc = t_len // L
    per_seq = lambda rows, width: pl.BlockSpec((G, rows, width), lambda b, c: (b, 0, 0))
    proj3 = proj.reshape(n_seq, t_len, D_PROJ)
    y, c_new, n_new, m_new = pl.pallas_call(
        functools.partial(_mlstm_kernel, L=L, nc=nc, G=G),
        grid=(n_seq // G, nc),
        in_specs=[
            _seq_spec(G, L, BLK_MQ), _seq_spec(G, L, BLK_MK), _seq_spec(G, L, BLK_MV),
            _seq_spec(G, L, BLK_MO), _seq_spec(G, L, BLK_SMALL, SMALL_W),
            pl.BlockSpec((G, 8, L), lambda b, c: (b, 0, c)),
            p["ml_brow"].spec(), p["ml_bcol"].spec(), p["ml_norm"].spec(),
        ],
        out_specs=[
            pl.BlockSpec((G, L, W_GROUP), lambda b, c: (b, c, 0)),
            _HEAD_BLOCKS_SPEC(G), per_seq(8, W_GROUP), per_seq(8, LANES),
        ],
        out_shape=[
            jax.ShapeDtypeStruct((n_seq, t_len, W_GROUP), BF16),
            jax.ShapeDtypeStruct((n_seq, N_HEADS, HEAD_DIM, HEAD_DIM), F32),
            jax.ShapeDtypeStruct((n_seq, 8, W_GROUP), F32),
            jax.ShapeDtypeStruct((n_seq, 8, LANES), F32),
        ],
        scratch_shapes=[
            pltpu.VMEM((G, W_GROUP, W_GROUP), F32),
            pltpu.VMEM((G, 8, W_GROUP), F32),
            pltpu.VMEM((G, 8, LANES), F32),
        ],
        compiler_params=_cparams("arbitrary", "arbitrary"),
        name="mlstm_chunk",
    )(proj3, proj3, proj3, proj3, proj3, gt, p["ml_brow"].arr, p["ml_bcol"].arr, p["ml_norm"].arr)
    return y.reshape(n_seq * t_len, W_GROUP), c_new, n_new, m_new


def _gla_kernel(q_ref, k_ref, v_ref, gg_ref, sm_ref, wa_ref, ba_ref, norm_ref,
                y_ref, s_out, s_scr, *, L, nc, G):
    c = pl.program_id(1)

    @pl.when(c == 0)
    def _():
        s_scr[...] = jnp.zeros_like(s_scr)

    ones = _head_ones()
    tril = _tril_ones(L)
    causal = _iota((L, N_HEADS * L), 0) >= (_iota((L, N_HEADS * L), 1) & (L - 1))

    def chain(gi):
        q = q_ref[gi] * (HEAD_DIM ** -0.5)
        k = k_ref[gi]
        v = v_ref[gi]
        la = _log_sigmoid(_dot(sm_ref[gi], wa_ref[...]) + ba_ref[...]) * (1.0 / GLA_GATE_TEMP)
        yield
        bc = _dot_exact_rhs(tril, la)
        yield
        ref = bc[L // 2 - 1:L // 2, :]
        qr = q * jnp.exp(jnp.minimum(bc - ref, EXP_CLAMP))
        kh = k * jnp.exp(jnp.minimum(ref - bc, EXP_CLAMP))
        ST = s_scr[gi]
        att = jnp.where(causal, _dot_nt(qr, _stack_heads(kh)), 0.0)
        from_state = _dot_nt(q * jnp.exp(bc), ST)
        yield
        o = from_state + _dot(att, _stack_heads(v))
        b_last = bc[L - 1:L, :]
        kbar = k * jnp.exp(b_last - bc)
        s_scr[gi] = ST * jnp.exp(b_last) + ones.astype(F32) * _dot_tn(v, kbar)
        yield
        ms = _head_sum(o * o, ones) * (1.0 / HEAD_DIM)
        gg = gg_ref[gi]
        y = o * lax.rsqrt(ms + EPS) * norm_ref[...] * (gg * _sigmoid(gg))
        y_ref[gi] = y.astype(y_ref.dtype)

    _run_interleaved([chain(gi) for gi in range(G)])

    @pl.when(c == nc - 1)
    def _():
        _store_head_blocks(s_out, s_scr, transpose=True)


def _gla(proj, p, n_seq, t_len):
    L = CHUNK_ML
    G = SEQS_PER_STEP_ML
    nc = t_len // L
    proj3 = proj.reshape(n_seq, t_len, D_PROJ)
    y, s_new = pl.pallas_call(
        functools.partial(_gla_kernel, L=L, nc=nc, G=G),
        grid=(n_seq // G, nc),
        in_specs=[
            _seq_spec(G, L, BLK_GQ), _seq_spec(G, L, BLK_GK), _seq_spec(G, L, BLK_GV),
            _seq_spec(G, L, BLK_GG), _seq_spec(G, L, BLK_SMALL, SMALL_W),
            p["gla_wa"].spec(), p["gla_ba"].spec(), p["gla_norm"].spec(),
        ],
        out_specs=[
            pl.BlockSpec((G, L, W_GROUP), lambda b, c: (b, c, 0)),
            _HEAD_BLOCKS_SPEC(G),
        ],
        out_shape=[
            jax.ShapeDtypeStruct((n_seq, t_len, W_GROUP), BF16),
            jax.ShapeDtypeStruct((n_seq, N_HEADS, HEAD_DIM, HEAD_DIM), F32),
        ],
        scratch_shapes=[pltpu.VMEM((G, W_GROUP, W_GROUP), F32)],
        compiler_params=_cparams("arbitrary", "arbitrary"),
        name="gla_chunk",
    )(proj3, proj3, proj3, proj3, proj3, p["gla_wa"].arr, p["gla_ba"].arr, p["gla_norm"].arr)
    return y.reshape(n_seq * t_len, W_GROUP), s_new


def _rwkv_vectors(rc, prev, p_mu, p_w0, p_a0, p_kk, p_ka, p_rk, w2, a2, g2, ones):
    xm = rc + p_mu * (prev - rc)
    rr = xm[:, 0:W_GROUP]
    rk = xm[:, W_GROUP:2 * W_GROUP]
    rv = xm[:, 2 * W_GROUP:3 * W_GROUP]
    tail = xm[:, 3 * W_GROUP:RWKV_COLS]
    lw = -RWKV_DECAY_SCALE * _sigmoid(p_w0 + _dot(jnp.tanh(tail), w2))
    a = _sigmoid(p_a0 + _dot(tail, a2))
    g = _dot(_sigmoid(tail), g2)
    kk = rk * p_kk
    kk = kk * lax.rsqrt(jnp.maximum(_head_sum(kk * kk, ones), 1e-24))
    kt = rk * (1.0 + (a - 1.0) * p_ka)
    bonus = _head_sum(rr * kt * p_rk, ones) * rv
    return rr, lw, kt, rv, kk, a, g, bonus


def _head_layernorm(o, g, ones):
    mu = _head_sum(o, ones) * (1.0 / HEAD_DIM)
    oc = o - mu
    var = _head_sum(oc * oc, ones) * (1.0 / HEAD_DIM)
    return oc * lax.rsqrt(var + GN_EPS) * g


def _stack_heads(x):
    xb = x.astype(BF16)
    lane_head = _iota((1, W_GROUP), 1) >> HEAD_SHIFT
    return jnp.concatenate([jnp.where(lane_head == h, xb, jnp.zeros_like(xb)) for h in range(N_HEADS)],
                           axis=0)


def _each(f, *seqs):
    return [f(*args) for args in zip(*seqs)]


def _block_mm(x_list, y_list):
    return _each(lambda x, y: _dot(x, _stack_heads(y)), x_list, y_list)


def _unit_lower_inverse(a_list, L):
    t_idx = _iota((L, N_HEADS * L), 0)
    s_idx = _iota((L, N_HEADS * L), 1) & (L - 1)
    eye = jnp.where(t_idx == s_idx, 1.0, 0.0).astype(F32)
    in16 = (t_idx >> 4) == (s_idx >> 4)
    in32 = ((t_idx >> 5) == (s_idx >> 5)) & ((t_idx >> 4) > (s_idx >> 4))
    in64 = (t_idx >> 5) > (s_idx >> 5)
    x16 = _each(lambda a: jnp.where(in16, -a, 0.0), a_list)
    x2 = _block_mm(x16, x16)
    x4 = _block_mm(x2, x2)
    x8 = _block_mm(x4, x4)
    t = _each(lambda x: eye + x, x16)
    for xp in (x2, x4, x8):
        t = _each(lambda t_, d: t_ + d, t, _block_mm(t, xp))
    for sel in (in32, in64):
        w = _block_mm(_each(lambda a: jnp.where(sel, a, 0.0), a_list), t)
        t = _each(lambda t_, d: t_ - d, t, _block_mm(t, w))
    return t


def _rwkv_chunks(rcs, prev_rows, states, prm, ones, L):
    (mu, w0, a0, pkk, pka, prk, w2, a2, g2, norm) = prm
    first_row = _iota((L, RWKV_COLS), 0) == 0
    prevs = _each(lambda rc, pr: jnp.where(first_row, pr, pltpu.roll(rc, 1, 0)), rcs, prev_rows)
    n_seq = len(rcs)
    stacked = _rwkv_vectors(jnp.concatenate(rcs, axis=0), jnp.concatenate(prevs, axis=0),
                            mu, w0, a0, pkk, pka, prk, w2, a2, g2, ones)
    r, lw, kx, v, kk, a, g, bonus = [[x[i * L:(i + 1) * L] for i in range(n_seq)] for x in stacked]

    tril = _tril_ones(L)
    t_idx = _iota((L, N_HEADS * L), 0)
    s_idx = _iota((L, N_HEADS * L), 1) & (L - 1)
    strict = t_idx > s_idx
    incl = t_idx >= s_idx

    def core(r, lw, kx, v, kk, a, states):
        lc = _each(lambda x: _dot_exact_rhs(tril, x), lw)
        akk = _each(lambda a_, k_: a_ * k_, a, kk)
        e_neg = _each(lambda x: jnp.exp(-x), lc)
        l_last = _each(lambda x: x[L - 1:L, :], lc)
        e_end = _each(lambda ll, x: jnp.exp(ll - x), l_last, lc)
        kr = _each(lambda k_, r_, c_, w_: jnp.concatenate([k_ * jnp.exp(c_ - w_), r_ * jnp.exp(c_)], axis=0),
                   kk, r, lc, lw)
        a_hat = _each(lambda x, e: _stack_heads(x * e), akk, e_neg)
        k_hat = _each(lambda x, e: _stack_heads(x * e), kx, e_neg)
        v_bd = _each(_stack_heads, v)

        pa = _each(_dot_nt, kr, a_hat)
        pk = _each(_dot_nt, kr, k_hat)
        a_ua = _each(lambda p_: jnp.where(strict, p_[0:L], 0.0), pa)
        b_ra = _each(lambda p_: jnp.where(incl, p_[L:2 * L], 0.0), pa)
        ab_k = _each(lambda p_: jnp.concatenate([jnp.where(strict, p_[0:L], 0.0),
                                                 jnp.where(incl, p_[L:2 * L], 0.0)], axis=0), pk)
        T = _unit_lower_inverse(a_ua, L)

        from_state = _each(_dot_nt, kr, states)
        from_v = _each(_dot, ab_k, v_bd)
        rhs = _each(lambda s_, v_: s_[0:L] + v_[0:L], from_state, from_v)
        U = _block_mm(T, rhs)
        corr = _block_mm(b_ra, U)
        o = _each(lambda s_, v_, c_: s_[L:2 * L] + v_[L:2 * L] - c_, from_state, from_v, corr)
        upd = _each(lambda v_, u_, k_, a_, e: _dot_tn(jnp.concatenate([v_, u_], axis=0),
                                                      jnp.concatenate([k_ * e, -(a_ * e)], axis=0)),
                    v, U, kx, akk, e_end)
        s_new = _each(lambda s_, ll, d: s_ * jnp.exp(ll) + ones.astype(F32) * d, states, l_last, upd)
        return o, s_new

    o, s_new = core(r, lw, kx, v, kk, a, states)
    y_all = ((_head_layernorm(jnp.concatenate(o, axis=0), norm, ones) + stacked[7]) * stacked[6])
    return [y_all[i * L:(i + 1) * L] for i in range(n_seq)], s_new


def _rwkv_kernel(rc_ref, mu_ref, w0_ref, a0_ref, kk_ref, ka_ref, rk_ref, w2_ref, a2_ref, g2_ref,
                 norm_ref, y_ref, s_out, shift_out, s_scr, prev_scr, *, L, nc, G):
    c = pl.program_id(1)

    @pl.when(c == 0)
    def _():
        s_scr[...] = jnp.zeros_like(s_scr)
        prev_scr[...] = jnp.zeros_like(prev_scr)

    ones = _head_ones()
    prm = (mu_ref[...], w0_ref[...], a0_ref[...], kk_ref[...], ka_ref[...], rk_ref[...],
           w2_ref[...], a2_ref[...], g2_ref[...], norm_ref[...])
    rcs = [rc_ref[gi] for gi in range(G)]
    ys, s_new = _rwkv_chunks(rcs, [prev_scr[gi, 0:1, :] for gi in range(G)],
                             [s_scr[gi] for gi in range(G)], prm, ones, L)
    for gi in range(G):
        prev_scr[gi, 0:1, :] = rcs[gi][L - 1:L, :]
        s_scr[gi] = s_new[gi]
        y_ref[gi] = ys[gi].astype(y_ref.dtype)

    @pl.when(c == nc - 1)
    def _():
        _store_head_blocks(s_out, s_scr)
        shift_out[...] = jnp.broadcast_to(prev_scr[:, 0:1, :], shift_out.shape)


def _rwkv(proj, p, n_seq, t_len):
    L = CHUNK_RW
    G = SEQS_PER_STEP
    nc = t_len // L
    names = ("rw_mu", "rw_w0", "rw_a0", "rw_kk", "rw_ka", "rw_rk", "rw_w2", "rw_a2", "rw_g2", "rw_norm")
    y, s_new, shift = pl.pallas_call(
        functools.partial(_rwkv_kernel, L=L, nc=nc, G=G),
        grid=(n_seq // G, nc),
        in_specs=[
            pl.BlockSpec((G, L, RWKV_COLS), lambda b, c: (b, c, 0)),
        ] + [p[k].spec() for k in names],
        out_specs=[
            pl.BlockSpec((G, L, W_GROUP), lambda b, c: (b, c, 0)),
            _HEAD_BLOCKS_SPEC(G),
            pl.BlockSpec((G, 8, RWKV_COLS), lambda b, c: (b, 0, 0)),
        ],
        out_shape=[
            jax.ShapeDtypeStruct((n_seq, t_len, W_GROUP), BF16),
            jax.ShapeDtypeStruct((n_seq, N_HEADS, HEAD_DIM, HEAD_DIM), F32),
            jax.ShapeDtypeStruct((n_seq, 8, RWKV_COLS), F32),
        ],
        scratch_shapes=[
            pltpu.VMEM((G, W_GROUP, W_GROUP), F32),
            pltpu.VMEM((G, 8, RWKV_COLS), F32),
        ],
        compiler_params=_cparams("arbitrary", "arbitrary"),
        name="rwkv_chunk",
    )(proj.reshape(n_seq, t_len, D_PROJ), *[p[k].arr for k in names])
    return y.reshape(n_seq * t_len, W_GROUP), s_new, shift


FF_CHUNK = 256


def _mix_residual(x_ref, ys, wout_ref):
    acc = x_ref[...]
    for j, y in enumerate(ys):
        acc = acc + jnp.dot(y.astype(BF16), wout_ref[j * W_GROUP:(j + 1) * W_GROUP, :],
                            preferred_element_type=F32)
    return acc


def _ffn_body(x1, nrm_ref, wup_ref, cw_ref, cb_ref, wdn_ref, prev_rows, d_ff, act_scr):
    h2 = _rmsnorm_rows(x1, nrm_ref[...]).astype(BF16)
    for j in range(d_ff // FF_CHUNK):
        lo, hi = j * FF_CHUNK, (j + 1) * FF_CHUNK
        ug = jnp.dot(h2, wup_ref[:, lo:hi], preferred_element_type=F32)
        uv = jnp.dot(h2, wup_ref[:, d_ff + lo:d_ff + hi], preferred_element_type=F32)
        p2, p1 = prev_rows(j, ug)
        conv = (cb_ref[:, lo:hi] + cw_ref[0:1, lo:hi] * p2 + cw_ref[1:2, lo:hi] * p1
                + cw_ref[2:3, lo:hi] * ug)
        act_scr[:, lo:hi] = (_gelu(conv) * uv).astype(BF16)
    return x1 + jnp.dot(act_scr[...], wdn_ref[...], preferred_element_type=F32)


def _ffn_seq_kernel(x_ref, y0_ref, y1_ref, y2_ref, y3_ref, wout_ref, nrm_ref, wup_ref, cw_ref, cb_ref,
                    wdn_ref, nf_ref, o_ref, buf_out, carry_scr, act_scr, *, tm, d_ff, final):
    t = pl.program_id(1)

    @pl.when(t == 0)
    def _():
        carry_scr[...] = jnp.zeros_like(carry_scr)

    x1 = _mix_residual(x_ref, (y0_ref[...], y1_ref[...], y2_ref[...], y3_ref[...]), wout_ref)
    row = _iota((tm, FF_CHUNK), 0)

    def prev_rows(j, ug):
        lo, hi = j * FF_CHUNK, (j + 1) * FF_CHUNK
        c0 = carry_scr[0:1, lo:hi]
        c1 = carry_scr[1:2, lo:hi]
        p1 = jnp.where(row == 0, c1, pltpu.roll(ug, 1, 0))
        p2 = jnp.where(row == 0, c0, jnp.where(row == 1, c1, pltpu.roll(ug, 2, 0)))
        carry_scr[0:2, lo:hi] = ug[tm - 2:tm, :]
        return p2, p1

    out = _ffn_body(x1, nrm_ref, wup_ref, cw_ref, cb_ref, wdn_ref, prev_rows, d_ff, act_scr)
    if final:
        out = _rmsnorm_rows(out, nf_ref[...])
    o_ref[...] = out
    buf_out[0] = carry_scr[...]


def _ffn_step_kernel(x_ref, y0_ref, y1_ref, y2_ref, y3_ref, wout_ref, nrm_ref, wup_ref, cw_ref, cb_ref,
                     wdn_ref, nf_ref, p2_ref, p1_ref, o_ref, ug_out, act_scr, *, d_ff, final):
    x1 = _mix_residual(x_ref, (y0_ref[...], y1_ref[...], y2_ref[...], y3_ref[...]), wout_ref)

    def prev_rows(j, ug):
        lo, hi = j * FF_CHUNK, (j + 1) * FF_CHUNK
        ug_out[:, lo:hi] = ug
        return p2_ref[:, lo:hi], p1_ref[:, lo:hi]

    out = _ffn_body(x1, nrm_ref, wup_ref, cw_ref, cb_ref, wdn_ref, prev_rows, d_ff, act_scr)
    if final:
        out = _rmsnorm_rows(out, nf_ref[...])
    o_ref[...] = out


def _ffn_seq(x2d, ys, p, norm_final, n_seq, t_len, tm, final):
    n, d = x2d.shape
    d_ff = p["ffn_wdn"].shape[0]
    nt = t_len // tm
    rows = lambda w: pl.BlockSpec((tm, w), lambda b, t: (b * nt + t, 0))
    once = dict(pipeline_mode=pl.Buffered(1))
    return pl.pallas_call(
        functools.partial(_ffn_seq_kernel, tm=tm, d_ff=d_ff, final=final),
        grid=(n_seq, nt),
        in_specs=[rows(d)] + [rows(W_GROUP)] * 4 + [
            p["w_out"].spec(**once), p["norm_ffn"].spec(), p["ffn_wup"].spec(**once), p["ffn_cw"].spec(),
            p["ffn_cb"].spec(), p["ffn_wdn"].spec(**once), norm_final.spec()],
        out_specs=[rows(d), pl.BlockSpec((1, CONV_W - 1, d_ff), lambda b, t: (b, 0, 0))],
        out_shape=[jax.ShapeDtypeStruct((n, d), F32),
                   jax.ShapeDtypeStruct((n_seq, CONV_W - 1, d_ff), F32)],
        scratch_shapes=[pltpu.VMEM((CONV_W - 1, d_ff), F32), pltpu.VMEM((tm, d_ff), BF16)],
        compiler_params=_cparams("arbitrary", "arbitrary"),
        name="wout_ffn_seq",
    )(x2d, *ys, p["w_out"].arr, p["norm_ffn"].arr, p["ffn_wup"].arr, p["ffn_cw"].arr, p["ffn_cb"].arr,
      p["ffn_wdn"].arr, norm_final.arr)


def _ffn_step(x2d, ys, p, norm_final, prev2, prev1, final):
    n, d = x2d.shape
    d_ff = p["ffn_wdn"].shape[0]
    full = lambda shape: pl.BlockSpec(shape, lambda i: (0,) * len(shape))
    return pl.pallas_call(
        functools.partial(_ffn_step_kernel, d_ff=d_ff, final=final),
        grid=(1,),
        in_specs=[full((n, d))] + [full((n, W_GROUP))] * 4 + [
            p["w_out"].spec(), p["norm_ffn"].spec(), p["ffn_wup"].spec(), p["ffn_cw"].spec(),
            p["ffn_cb"].spec(), p["ffn_wdn"].spec(), norm_final.spec(), full((n, d_ff)), full((n, d_ff))],
        out_specs=[full((n, d)), full((n, d_ff))],
        out_shape=[jax.ShapeDtypeStruct((n, d), F32), jax.ShapeDtypeStruct((n, d_ff), F32)],
        scratch_shapes=[pltpu.VMEM((n, d_ff), BF16)],
        compiler_params=_cparams("arbitrary"),
        name="wout_ffn_step",
    )(x2d, *ys, p["w_out"].arr, p["norm_ffn"].arr, p["ffn_wup"].arr, p["ffn_cw"].arr, p["ffn_cb"].arr,
      p["ffn_wdn"].arr, norm_final.arr, prev2, prev1)


def _step_vec_kernel(proj_ref, shift_ref, m_ref, brow_ref, wa_ref, ba_ref, mu_ref, w0_ref, a0_ref,
                     kk_ref, ka_ref, rk_ref, w2_ref, a2_ref, g2_ref,
                     pack_out, m_out, g_out, bonus_out):
    ones = _head_ones()
    sm = proj_ref[:, OFF_SMALL:OFF_SMALL + SMALL_W]
    pre = sm + brow_ref[...]
    logf = pltpu.roll(_log_sigmoid(pre), LANES - N_HEADS, 1)
    m_old = m_ref[...]
    m_new = jnp.maximum(logf + m_old, pre)
    m_out[...] = m_new
    lane = _iota((SMALL_W, W_GROUP), 0)
    spread_mat = jnp.where(lane == LANE_MI + (_iota((SMALL_W, W_GROUP), 1) >> HEAD_SHIFT), 1.0, 0.0).astype(BF16)
    gate_lanes = (_iota(pre.shape, 1) >= LANE_MI) & (_iota(pre.shape, 1) < LANE_MI + N_HEADS)
    spread = lambda x: _dot_exact_lhs(jnp.where(gate_lanes, x, 0.0), spread_mat)
    la = _log_sigmoid(_dot(sm, wa_ref[...]) + ba_ref[...]) * (1.0 / GLA_GATE_TEMP)
    rc = proj_ref[:, 0:RWKV_COLS]
    rr, lw, kt, rv, kk, a, g, bonus = _rwkv_vectors(
        rc, shift_ref[...], mu_ref[...], w0_ref[...], a0_ref[...], kk_ref[...], ka_ref[...],
        rk_ref[...], w2_ref[...], a2_ref[...], g2_ref[...], ones)
    main = lambda blk: proj_ref[:, blk * W_GROUP:(blk + 1) * W_GROUP]
    rows = {
        ROW_MQ: main(BLK_MQ), ROW_MK: main(BLK_MK), ROW_MV: main(BLK_MV),
        ROW_GQ: main(BLK_GQ), ROW_GK: main(BLK_GK), ROW_GV: main(BLK_GV), ROW_GA: jnp.exp(la),
        ROW_KK: kk, ROW_AKK: a * kk, ROW_W: jnp.exp(lw), ROW_KT: kt, ROW_RR: rr, ROW_RV: rv,
        ROW_IW: spread(jnp.exp(pre - m_new)), ROW_FW: spread(jnp.exp(logf + m_old - m_new)),
        ROW_M: spread(m_new),
    }
    for i, val in rows.items():
        pack_out[i] = val.T
    g_out[...] = g
    bonus_out[...] = bonus


def _step_vectors(proj, shift, m_pad, p):
    n = proj.shape[0]
    full = lambda a: pl.BlockSpec(a.shape, lambda i: (0,) * a.ndim)
    data = (proj, shift, m_pad)
    prms = [p[k] for k in ("ml_brow", "gla_wa", "gla_ba", "rw_mu", "rw_w0", "rw_a0", "rw_kk", "rw_ka",
                           "rw_rk", "rw_w2", "rw_a2", "rw_g2")]
    wide = jax.ShapeDtypeStruct((n, W_GROUP), F32)
    outs = [jax.ShapeDtypeStruct((N_PACK, W_GROUP, n), F32), jax.ShapeDtypeStruct((n, SMALL_W), F32),
            wide, wide]
    return pl.pallas_call(
        _step_vec_kernel,
        grid=(1,),
        in_specs=[full(a) for a in data] + [q.spec() for q in prms],
        out_specs=[pl.BlockSpec(o.shape, lambda i, nd=len(o.shape): (0,) * nd) for o in outs],
        out_shape=outs,
        compiler_params=_cparams("arbitrary"),
        name="step_vectors",
    )(*data, *[q.arr for q in prms])


(ROW_MQ, ROW_MK, ROW_MV, ROW_GQ, ROW_GK, ROW_GV, ROW_GA, ROW_KK, ROW_AKK, ROW_W, ROW_KT, ROW_RR, ROW_RV,
 ROW_IW, ROW_FW, ROW_M) = range(16)
N_PACK = 16
(OROW_ML, OROW_GLA, OROW_RW, OROW_N) = range(4)
N_OPACK = 4
STEP_UNROLL = 4


def _heads_step_kernel(pack_ref, c_ref, n_ref, gs_ref, rs_ref, c_out, gs_out, rs_out, opack_out):
    vec = lambda i: pack_ref[i]
    one = lambda i, j: pack_ref[i, pl.ds(j, 1), :]
    colsum = lambda x: jnp.sum(x, axis=0, keepdims=True)

    k_ml = vec(ROW_MK) * (HEAD_DIM ** -0.5)
    q_ml = vec(ROW_MQ)
    iw = pack_ref[ROW_IW, 0:1, :]
    fw = pack_ref[ROW_FW, 0:1, :]
    n_new = fw * n_ref[...] + iw * k_ml
    opack_out[OROW_N] = n_new
    r_den = 1.0 / jnp.maximum(jnp.abs(colsum(n_new * q_ml)), jnp.exp(-pack_ref[ROW_M, 0:1, :]))

    def ml_body(v, carry):
        c_new = fw * c_ref[v] + (iw * one(ROW_MV, v)) * k_ml
        c_out[v] = c_new
        opack_out[OROW_ML, pl.ds(v, 1), :] = colsum(c_new * q_ml) * r_den
        return carry

    lax.fori_loop(0, HEAD_DIM, ml_body, 0, unroll=STEP_UNROLL)

    v_gla = vec(ROW_GV)

    def gla_body(k, acc):
        s_new = one(ROW_GA, k) * gs_ref[k] + one(ROW_GK, k) * v_gla
        gs_out[k] = s_new
        return acc + (one(ROW_GQ, k) * (HEAD_DIM ** -0.5)) * s_new

    opack_out[OROW_GLA] = lax.fori_loop(0, HEAD_DIM, gla_body, jnp.zeros_like(v_gla), unroll=STEP_UNROLL)

    kk, akk, w, kt, rr = vec(ROW_KK), vec(ROW_AKK), vec(ROW_W), vec(ROW_KT), vec(ROW_RR)

    def rw_body(v, carry):
        s = rs_ref[v]
        s_new = s * w - colsum(s * kk) * akk + one(ROW_RV, v) * kt
        rs_out[v] = s_new
        opack_out[OROW_RW, pl.ds(v, 1), :] = colsum(s_new * rr)
        return carry

    lax.fori_loop(0, HEAD_DIM, rw_body, 0, unroll=STEP_UNROLL)


def _heads_step(pack, ml_c, ml_n, gla_s, rw_s, layer):
    n = pack.shape[-1]
    sq = (HEAD_DIM, HEAD_DIM, n)
    head = lambda lead: pl.BlockSpec((lead, HEAD_DIM, n), lambda h: (0, h, 0))
    at_layer = lambda shape: pl.BlockSpec((None,) + shape, lambda h: (layer * N_HEADS + h,) + (0,) * len(shape))
    out_state = pl.BlockSpec((None,) + sq, lambda h: (h, 0, 0, 0))
    state = jax.ShapeDtypeStruct((N_HEADS,) + sq, F32)
    return pl.pallas_call(
        _heads_step_kernel,
        grid=(N_HEADS,),
        in_specs=[head(N_PACK), at_layer(sq), at_layer((HEAD_DIM, n)), at_layer(sq), at_layer(sq)],
        out_specs=[out_state, out_state, out_state, head(N_OPACK)],
        out_shape=[state, state, state, jax.ShapeDtypeStruct((N_OPACK, W_GROUP, n), F32)],
        compiler_params=_cparams("arbitrary"),
        name="heads_step",
    )(pack, ml_c, ml_n, gla_s, rw_s)


def _step_post_kernel(hml_ref, mo_ref, ogla_ref, gg_ref, orw_ref, g_ref, bonus_ref,
                      mln_ref, gln_ref, rwn_ref, yml_out, ygla_out, yrw_out):
    ones = _head_ones()
    hml = hml_ref[...].T
    ms = _head_sum(hml * hml, ones) * (1.0 / HEAD_DIM)
    yml_out[...] = (hml * lax.rsqrt(ms + EPS) * mln_ref[...] * _sigmoid(mo_ref[...])).astype(BF16)
    og = ogla_ref[...].T
    ms = _head_sum(og * og, ones) * (1.0 / HEAD_DIM)
    gg = gg_ref[...]
    ygla_out[...] = (og * lax.rsqrt(ms + EPS) * gln_ref[...] * (gg * _sigmoid(gg))).astype(BF16)
    yrw_out[...] = ((_head_layernorm(orw_ref[...].T, rwn_ref[...], ones) + bonus_ref[...])
                    * g_ref[...]).astype(BF16)


def _step_post(hml, mo, ogla, gg, orw, g, bonus, p):
    n = mo.shape[0]
    data = (hml, mo, ogla, gg, orw, g, bonus)
    prms = [p[k] for k in ("ml_norm", "gla_norm", "rw_norm")]
    full = lambda a: pl.BlockSpec(a.shape, lambda i: (0,) * a.ndim)
    out = jax.ShapeDtypeStruct((n, W_GROUP), BF16)
    return pl.pallas_call(
        _step_post_kernel,
        grid=(1,),
        in_specs=[full(a) for a in data] + [q.spec() for q in prms],
        out_specs=[pl.BlockSpec((n, W_GROUP), lambda i: (0, 0))] * 3,
        out_shape=[out] * 3,
        compiler_params=_cparams("arbitrary"),
        name="step_post",
    )(*data, *[q.arr for q in prms])


def _prepare_params(prm):
    P = {}
    w_in = prm["w_in"].astype(BF16)
    depth, d = w_in.shape[0], w_in.shape[1]
    sizes = (W_GROUP, W_GROUP, W_GROUP, W_GROUP, N_HEADS, N_HEADS, W_GROUP, W_GROUP, W_GROUP, W_GROUP,
             GLA_RANK, W_GROUP, RWKV_COLS)
    offs = [0]
    for sz in sizes:
        offs.append(offs[-1] + sz)
    col = lambda i: w_in[:, :, offs[i]:offs[i + 1]]
    (u, mq, mk, mv, mi, mf, mo, gq, gk, gv, ga, gg, rcols) = [col(i) for i in range(13)]
    small = jnp.concatenate([ga, mi, mf, jnp.zeros((depth, d, SMALL_W - GLA_RANK - 2 * N_HEADS), BF16)], axis=2)
    P["w_in"] = jnp.concatenate([rcols, small, u, mq, mk, mv, mo, gq, gk, gv, gg], axis=2)
    P["w_gate_t"] = jnp.swapaxes(jnp.concatenate([mi, mf], axis=2), 1, 2)

    lam = lax.complex(prm["s5_lam_re"], prm["s5_lam_im"])
    dt = jnp.exp(prm["s5_log_dt"])[..., None]
    lam_bar = jnp.exp(lam * dt)
    b_bar = ((lam_bar - 1.0) / lam)[..., None] * lax.complex(prm["s5_b_re"], prm["s5_b_im"])
    eye = jnp.eye(S5_GROUPS, dtype=F32)
    bm = lambda b: jnp.einsum("lgph,gk->lghkp", b, eye).reshape(depth, W_GROUP, S5_WIDTH)
    P["bmat"] = jnp.concatenate([bm(b_bar.real), bm(b_bar.imag)], axis=2).astype(BF16)
    cm = lambda c: jnp.einsum("lghp,gk->lkpgh", c, eye).reshape(depth, S5_WIDTH, W_GROUP)
    P["cmat"] = jnp.concatenate([cm(prm["s5_c_re"]), -cm(prm["s5_c_im"])], axis=1).astype(BF16)
    P["lam"] = jnp.stack([lam_bar.real.reshape(depth, S5_WIDTH), lam_bar.imag.reshape(depth, S5_WIDTH)],
                         axis=1)[:, :, None, :]
    P["wglu"] = prm["s5_w_glu"].astype(BF16)

    bias = prm["ml_gate_bias"]
    P["ml_brow"] = jnp.pad(bias, ((0, 0), (LANE_MI, SMALL_W - LANE_MI - 2 * N_HEADS)))[:, None, :]
    P["ml_bcol"] = jnp.broadcast_to(bias[:, :, None], (depth, 2 * N_HEADS, CHUNK_ML))

    P["rows"] = jnp.stack([prm["s5_d"], prm["ml_norm"], prm["gla_b_alpha"], prm["gla_norm"], prm["rw_w0"],
                           prm["rw_a0"], prm["rw_k_k"], prm["rw_k_a"], prm["rw_r_k"].reshape(depth, W_GROUP),
                           prm["rw_norm"]], axis=1)[:, :, None, :]
    pad_rows = lambda w, lo: jnp.pad(w, ((0, 0), (lo, LANES - lo - w.shape[1]), (0, 0)))
    P["pads"] = jnp.stack([pad_rows(prm["gla_w_alpha"], 0), pad_rows(prm["rw_w2"], 0),
                           pad_rows(prm["rw_a2"], RWKV_W_RANK),
                           pad_rows(prm["rw_g2"], RWKV_W_RANK + RWKV_A_RANK)], axis=1).astype(BF16)
    P["rw_mu"] = prm["rw_mu"][:, None, :]
    P["norms"] = jnp.stack([prm["norm_mix"], prm["norm_ffn"]], axis=1)[:, :, None, :]
    P["w_out"] = prm["w_out"].astype(BF16)
    P["ffn_wup"] = prm["ffn_w_up"].astype(BF16)
    P["ffn_cw"] = prm["ffn_conv_w"]
    P["ffn_cb"] = prm["ffn_conv_b"][:, None, :]
    P["ffn_wdn"] = prm["ffn_w_down"].astype(BF16)
    return P


ROW_NAMES = ("d", "ml_norm", "gla_ba", "gla_norm", "rw_w0", "rw_a0", "rw_kk", "rw_ka", "rw_rk", "rw_norm")
PAD_NAMES = ("gla_wa", "rw_w2", "rw_a2", "rw_g2")


def _layer_params(P, l):
    p = {k: _Param(P[k], l) for k in ("w_in", "w_gate_t", "bmat", "cmat", "wglu", "ml_brow", "ml_bcol",
                                      "rw_mu", "w_out", "ffn_wup", "ffn_cw", "ffn_cb", "ffn_wdn")}
    p.update({k: _Param(P["rows"], l, i) for i, k in enumerate(ROW_NAMES)})
    p.update({k: _Param(P["pads"], l, i) for i, k in enumerate(PAD_NAMES)})
    p["lre"], p["lim"] = _Param(P["lam"], l, 0), _Param(P["lam"], l, 1)
    p["norm_mix"], p["norm_ffn"] = _Param(P["norms"], l, 0), _Param(P["norms"], l, 1)
    return p


def _prompt_layer(x2d, p, norm_final, n_seq, t_len, final):
    proj, gt, u = _inproj(x2d, p["norm_mix"], p["w_in"], p["w_gate_t"], n_seq, t_len, tm=INPROJ_ROWS)

    u_tb = u.reshape(n_seq, t_len, W_GROUP).transpose(1, 0, 2)
    zeros = jnp.zeros((n_seq, S5_WIDTH), F32)
    y_s5_tb, s5_re, s5_im = _s5(u_tb.reshape(t_len * n_seq, W_GROUP), zeros, zeros, p, n_seq, t_len, tt=S5_TILE_STEPS)
    y_s5 = y_s5_tb.reshape(t_len, n_seq, W_GROUP).transpose(1, 0, 2).reshape(n_seq * t_len, W_GROUP)

    y_ml, ml_c, ml_n, ml_m = _mlstm(proj, gt, p, n_seq, t_len)
    y_gla, gla_st = _gla(proj, p, n_seq, t_len)
    y_rw, rw_s, rw_shift = _rwkv(proj, p, n_seq, t_len)

    x_new, ffn_buf = _ffn_seq(x2d, (y_s5, y_ml, y_gla, y_rw), p, norm_final, n_seq, t_len, tm=FFN_ROWS,
                              final=final)
    states = (
        s5_re.reshape(n_seq, S5_GROUPS, S5_STATE),
        s5_im.reshape(n_seq, S5_GROUPS, S5_STATE),
        ml_c,
        ml_n[:, 0, :].reshape(n_seq, N_HEADS, HEAD_DIM),
        ml_m[:, 0, LANE_MI:LANE_MI + N_HEADS],
        gla_st,
        rw_s,
        rw_shift[:, 0, :],
        ffn_buf,
    )
    return x_new, states


def _sample_layer(x2d, st, pair_states, layer, p, norm_final, final):
    (s5_re, s5_im, ml_m, rw_shift, ffn_buf) = st
    n = x2d.shape[0]
    npairs = n * N_HEADS
    proj, _, u = _inproj(x2d, p["norm_mix"], p["w_in"], p["w_gate_t"], 1, n, tm=n)

    y_s5, s5_re_new, s5_im_new = _s5(u,
                                     s5_re.reshape(n, S5_WIDTH), s5_im.reshape(n, S5_WIDTH),
                                     p, n, 1, tt=1)

    m_pad = jnp.zeros((n, SMALL_W), F32).at[:, LANE_MI:LANE_MI + N_HEADS].set(ml_m)
    pack, m_new, g, bonus = _step_vectors(proj, rw_shift, m_pad, p)
    ml_c_new, gla_s_new, rw_s_new, opack = _heads_step(pack, *pair_states, layer)

    main = lambda blk: proj[:, blk * W_GROUP:(blk + 1) * W_GROUP]
    y_ml, y_gla, y_rw = _step_post(opack[OROW_ML], main(BLK_MO), opack[OROW_GLA], main(BLK_GG),
                                   opack[OROW_RW], g, bonus, p)
    x_new, ug = _ffn_step(x2d, (y_s5, y_ml, y_gla, y_rw), p, norm_final, ffn_buf[:, 0, :], ffn_buf[:, 1, :],
                          final)
    states = (
        s5_re_new.reshape(n, S5_GROUPS, S5_STATE),
        s5_im_new.reshape(n, S5_GROUPS, S5_STATE),
        ml_c_new.transpose(3, 0, 1, 2),
        opack[OROW_N].reshape(N_HEADS, HEAD_DIM, n).transpose(2, 0, 1),
        m_new[:, LANE_MI:LANE_MI + N_HEADS],
        gla_s_new.transpose(3, 0, 1, 2),
        rw_s_new.transpose(3, 0, 1, 2),
        proj[:, 0:RWKV_COLS],
        jnp.stack([ffn_buf[:, 1, :], ug], axis=1),
    )
    return x_new, states


def kernel(x_prompt, x_sample, state_s5_re, state_s5_im, state_mlstm_C, state_mlstm_n, state_mlstm_m, state_gla_S, state_rwkv_S, state_rwkv_shift, state_ffn_conv, norm_mix, w_in, s5_lam_re, s5_lam_im, s5_log_dt, s5_b_re, s5_b_im, s5_c_re, s5_c_im, s5_d, s5_w_glu, ml_gate_bias, ml_norm, gla_w_alpha, gla_b_alpha, gla_norm, rw_mu, rw_w0, rw_w2, rw_a0, rw_a2, rw_g2, rw_k_k, rw_k_a, rw_r_k, rw_norm, w_out, norm_ffn, ffn_w_up, ffn_conv_w, ffn_conv_b, ffn_w_down, norm_final):
    prm = dict(norm_mix=norm_mix, w_in=w_in, s5_lam_re=s5_lam_re, s5_lam_im=s5_lam_im, s5_log_dt=s5_log_dt,
               s5_b_re=s5_b_re, s5_b_im=s5_b_im, s5_c_re=s5_c_re, s5_c_im=s5_c_im, s5_d=s5_d,
               s5_w_glu=s5_w_glu, ml_gate_bias=ml_gate_bias, ml_norm=ml_norm, gla_w_alpha=gla_w_alpha,
               gla_b_alpha=gla_b_alpha, gla_norm=gla_norm, rw_mu=rw_mu, rw_w0=rw_w0, rw_w2=rw_w2,
               rw_a0=rw_a0, rw_a2=rw_a2, rw_g2=rw_g2, rw_k_k=rw_k_k, rw_k_a=rw_k_a, rw_r_k=rw_r_k,
               rw_norm=rw_norm, w_out=w_out, norm_ffn=norm_ffn, ffn_w_up=ffn_w_up, ffn_conv_w=ffn_conv_w,
               ffn_conv_b=ffn_conv_b, ffn_w_down=ffn_w_down)
    depth = w_in.shape[0]
    n_seq, t_len, d = x_prompt.shape
    n_smp = x_sample.shape[0]
    assert t_len % max(INPROJ_ROWS, FFN_ROWS, S5_TILE_STEPS, CHUNK_ML) == 0
    assert n_seq % max(SEQS_PER_STEP, SEQS_PER_STEP_ML) == 0 and x_sample.shape[1] == 1
    nf = _Param(norm_final[None, :])
    small_states = (state_s5_re, state_s5_im, state_mlstm_m, state_rwkv_shift, state_ffn_conv)
    lead = depth * N_HEADS
    to_lanes = lambda st: jnp.moveaxis(st, 1, -1)
    pair_states = (to_lanes(state_mlstm_C).reshape(lead, HEAD_DIM, HEAD_DIM, n_smp),
                   to_lanes(state_mlstm_n).reshape(lead, HEAD_DIM, n_smp),
                   to_lanes(state_gla_S).reshape(lead, HEAD_DIM, HEAD_DIM, n_smp),
                   to_lanes(state_rwkv_S).reshape(lead, HEAD_DIM, HEAD_DIM, n_smp))

    P = _prepare_params(prm)
    xp = x_prompt.reshape(n_seq * t_len, d)
    xs = x_sample.reshape(n_smp, d)
    p_states, s_states = [], []
    for l in range(depth):
        p = _layer_params(P, l)
        final = l == depth - 1
        xp, ps = _prompt_layer(xp, p, nf, n_seq, t_len, final)
        xs, ss = _sample_layer(xs, tuple(s[l] for s in small_states), pair_states, l, p, nf, final)
        p_states.append(ps)
        s_states.append(ss)
    new_p = tuple(jnp.stack([st[i] for st in p_states]) for i in range(9))
    new_s = tuple(jnp.stack([st[i] for st in s_states]) for i in range(9))
    return (xp.reshape(n_seq, t_len, d), xs.reshape(n_smp, 1, d)) + new_p + new_s
```

```python
import functools
import math

import jax
import jax.numpy as jnp
from jax import lax
from jax.experimental import pallas as pl
from jax.experimental.pallas import tpu as pltpu

F32 = jnp.float32
BF16 = jnp.bfloat16

LANES = 128
W_GROUP = 256
HEAD_DIM = 64
HEAD_SHIFT = 6
N_HEADS = 4
S5_GROUPS = 16
S5_STATE = 64
S5_WIDTH = S5_GROUPS * S5_STATE
GLA_RANK = 16
GLA_GATE_TEMP = 16.0
RWKV_W_RANK = 32
RWKV_A_RANK = 32
RWKV_G_RANK = 64
RWKV_COLS = 3 * W_GROUP + RWKV_W_RANK + RWKV_A_RANK + RWKV_G_RANK
RWKV_DECAY_SCALE = 0.6065306597126334
CONV_W = 3
EPS = 1e-6
GN_EPS = 64e-5
NEG_BIG = -1e30
EXP_CLAMP = 80.0

SMALL_W = LANES
OFF_SMALL = RWKV_COLS
OFF_MAIN = RWKV_COLS + SMALL_W
D_PROJ = OFF_MAIN + 9 * W_GROUP
LANE_MI = GLA_RANK
LANE_MF = GLA_RANK + N_HEADS
(BLK_U, BLK_MQ, BLK_MK, BLK_MV, BLK_MO, BLK_GQ, BLK_GK, BLK_GV, BLK_GG) = range(
    OFF_MAIN // W_GROUP, OFF_MAIN // W_GROUP + 9)
BLK_SMALL = OFF_SMALL // SMALL_W

CHUNK_ML = 128
CHUNK_RW = 64
SEQS_PER_STEP = 8
SEQS_PER_STEP_ML = 8
INPROJ_ROWS = 1024
FFN_ROWS = 512
S5_TILE_STEPS = 128
VMEM_LIMIT = 56 * 1024 * 1024


def _cparams(*sem):
    return pltpu.CompilerParams(dimension_semantics=sem, vmem_limit_bytes=VMEM_LIMIT)


class _Param:
    def __init__(self, arr, *lead):
        self.arr, self.lead = arr, tuple(lead)

    @property
    def shape(self):
        return tuple(self.arr.shape[len(self.lead):])

    def spec(self, **kw):
        lead, tail = self.lead, (0,) * len(self.shape)
        return pl.BlockSpec((None,) * len(lead) + self.shape, lambda *grid_idx: lead + tail, **kw)


def _dot(a, b):
    return jnp.dot(a.astype(BF16), b.astype(BF16), preferred_element_type=F32)


def _dot_nt(a, b):
    return lax.dot_general(a.astype(BF16), b.astype(BF16), (((1,), (1,)), ((), ())),
                           preferred_element_type=F32)


def _dot_tn(a, b):
    return lax.dot_general(a.astype(BF16), b.astype(BF16), (((0,), (0,)), ((), ())),
                           preferred_element_type=F32)


def _split2(x):
    hi = x.astype(BF16)
    lo = (x - hi.astype(F32)).astype(BF16)
    return hi, lo


def _dot_exact_rhs(a01, x):
    hi, lo = _split2(x)
    f = lambda p: jnp.dot(a01, p, preferred_element_type=F32)
    return f(hi) + f(lo)


def _dot_exact_lhs(x, b01):
    hi, lo = _split2(x)
    f = lambda p: jnp.dot(p, b01, preferred_element_type=F32)
    return f(hi) + f(lo)


def _sigmoid(x):
    return 1.0 / (1.0 + jnp.exp(-x))


def _log_sigmoid(x):
    return jnp.minimum(x, 0.0) - jnp.log(1.0 + jnp.exp(-jnp.abs(x)))


def _gelu(x):
    return 0.5 * x * (1.0 + jnp.tanh(math.sqrt(2.0 / math.pi) * (x + 0.044715 * (x * x * x))))


def _iota(shape, dim):
    return lax.broadcasted_iota(jnp.int32, shape, dim)


def _head_ones():
    r = _iota((W_GROUP, W_GROUP), 0) >> HEAD_SHIFT
    c = _iota((W_GROUP, W_GROUP), 1) >> HEAD_SHIFT
    return jnp.where(r == c, 1.0, 0.0).astype(BF16)


def _head_sum(x, ones):
    return _dot_exact_lhs(x, ones)


def _rmsnorm_rows(x, g):
    ms = jnp.mean(x * x, axis=-1, keepdims=True)
    return x * lax.rsqrt(ms + EPS) * g


def _tril_ones(n, strict=False):
    r = _iota((n, n), 0)
    c = _iota((n, n), 1)
    return jnp.where((r > c) if strict else (r >= c), 1.0, 0.0).astype(BF16)


def _triu_ones(n):
    r = _iota((n, n), 0)
    c = _iota((n, n), 1)
    return jnp.where(r <= c, 1.0, 0.0).astype(BF16)


def _store_head_blocks(out_ref, scr_ref, transpose=False):
    for g in range(scr_ref.shape[0]):
        m = scr_ref[g].T if transpose else scr_ref[g]
        for h in range(N_HEADS):
            lo, hi = h * HEAD_DIM, (h + 1) * HEAD_DIM
            out_ref[g, h] = m[lo:hi, lo:hi]


def _HEAD_BLOCKS_SPEC(G):
    return pl.BlockSpec((G, N_HEADS, HEAD_DIM, HEAD_DIM), lambda b, c: (b, 0, 0, 0))


def _inproj_kernel(x_ref, g_ref, w_ref, wg_ref, proj_ref, gt_ref, u_ref):
    hn = _rmsnorm_rows(x_ref[...], g_ref[...]).astype(BF16)
    proj = jnp.dot(hn, w_ref[...], preferred_element_type=F32)
    proj_ref[...] = proj
    u_ref[...] = proj[:, OFF_MAIN:OFF_MAIN + W_GROUP]
    gt_ref[0] = lax.dot_general(wg_ref[...], hn, (((1,), (1,)), ((), ())),
                                preferred_element_type=F32)


def _inproj(x2d, g, w, wg, n_seq, t_len, tm):
    n, d = x2d.shape
    tiles_per_seq = t_len // tm
    return pl.pallas_call(
        _inproj_kernel,
        grid=(n // tm,),
        in_specs=[
            pl.BlockSpec((tm, d), lambda i: (i, 0)),
            g.spec(), w.spec(pipeline_mode=pl.Buffered(1)), wg.spec(),
        ],
        out_specs=[
            pl.BlockSpec((tm, D_PROJ), lambda i: (i, 0)),
            pl.BlockSpec((1, 8, tm), lambda i: (i // tiles_per_seq, 0, i % tiles_per_seq)),
            pl.BlockSpec((tm, W_GROUP), lambda i: (i % tiles_per_seq, i // tiles_per_seq)),
        ],
        out_shape=[
            jax.ShapeDtypeStruct((n, D_PROJ), F32),
            jax.ShapeDtypeStruct((n_seq, 8, t_len), F32),
            jax.ShapeDtypeStruct((t_len, n_seq * W_GROUP), F32),
        ],
        compiler_params=_cparams("arbitrary"),
        name="inproj",
    )(x2d, g.arr, w.arr, wg.arr)


def _s5_kernel(u_ref, unext_ref, h0re_ref, h0im_ref, bmat_ref, lre_ref, lim_ref, cmat_ref, d_ref, wglu_ref,
               y_ref, hre_out, him_out, bu_scr, hs_scr, hre_scr, him_scr, *, bp, tt, wide):
    i = pl.program_id(0)
    nt = S5_WIDTH // LANES

    rows_of = (lambda ref: ref[...].reshape(tt, bp, W_GROUP).reshape(tt * bp, W_GROUP)) if wide else (
        lambda ref: ref[...])

    def project(u_tile, slot):
        bu = jnp.dot(u_tile.astype(BF16), bmat_ref[...], preferred_element_type=F32)
        for j in range(2 * nt):
            bu_scr[slot, j] = bu[:, j * LANES:(j + 1) * LANES]

    @pl.when(i == 0)
    def _():
        hre_scr[...] = h0re_ref[...]
        him_scr[...] = h0im_ref[...]
        project(rows_of(u_ref), 0)

    slot = lax.rem(i, 2)
    lre = [lre_ref[:, j * LANES:(j + 1) * LANES] for j in range(nt)]
    lim = [lim_ref[:, j * LANES:(j + 1) * LANES] for j in range(nt)]

    def step(t, carry):
        rows = pl.ds(pl.multiple_of(t * bp, bp), bp)
        new = []
        for j in range(nt):
            hre, him = carry[j], carry[nt + j]
            new.append((lre[j] * hre - lim[j] * him + bu_scr[slot, j, rows, :],
                        lre[j] * him + lim[j] * hre + bu_scr[slot, nt + j, rows, :]))
        out = tuple(x[0] for x in new) + tuple(x[1] for x in new)
        for j in range(2 * nt):
            hs_scr[j, rows, :] = out[j]
        return out

    carry = tuple(hre_scr[:, j * LANES:(j + 1) * LANES] for j in range(nt)) + tuple(
        him_scr[:, j * LANES:(j + 1) * LANES] for j in range(nt))
    carry = step(0, carry) if tt == 1 else lax.fori_loop(0, tt, step, carry)
    hre = jnp.concatenate(carry[0:nt], axis=1)
    him = jnp.concatenate(carry[nt:2 * nt], axis=1)
    hre_scr[...] = hre
    him_scr[...] = him
    hre_out[...] = hre
    him_out[...] = him

    project(rows_of(unext_ref), lax.rem(i + 1, 2))
    u = rows_of(u_ref)
    hs = jnp.concatenate([hs_scr[j] for j in range(2 * nt)], axis=1)
    y = jnp.dot(hs.astype(BF16), cmat_ref[...], preferred_element_type=F32) + d_ref[...] * u
    z = _gelu(y)
    y = z * _sigmoid(_dot(z, wglu_ref[...]))
    if wide:
        y = y.reshape(tt, bp, W_GROUP).reshape(tt, bp * W_GROUP)
    y_ref[...] = y.astype(y_ref.dtype)


def _s5(u, h0re, h0im, sp, bp, t_len, tt, wide):
    nsteps = t_len // tt
    full = lambda shape: pl.BlockSpec(shape, lambda i: (0,) * len(shape))
    blk = (tt, bp * W_GROUP) if wide else (tt * bp, W_GROUP)
    tile = pl.BlockSpec(blk, lambda i: (i, 0))
    next_tile = pl.BlockSpec(blk, lambda i: (jnp.minimum(i + 1, nsteps - 1), 0))
    return pl.pallas_call(
        functools.partial(_s5_kernel, bp=bp, tt=tt, wide=wide),
        grid=(nsteps,),
        in_specs=[
            tile, next_tile,
            full((bp, S5_WIDTH)), full((bp, S5_WIDTH)),
            sp["bmat"].spec(), sp["lre"].spec(), sp["lim"].spec(), sp["cmat"].spec(), sp["d"].spec(),
            sp["wglu"].spec(),
        ],
        out_specs=[
            tile,
            full((bp, S5_WIDTH)), full((bp, S5_WIDTH)),
        ],
        out_shape=[
            jax.ShapeDtypeStruct(u.shape, BF16),
            jax.ShapeDtypeStruct((bp, S5_WIDTH), F32),
            jax.ShapeDtypeStruct((bp, S5_WIDTH), F32),
        ],
        scratch_shapes=[
            pltpu.VMEM((2, 2 * S5_WIDTH // LANES, tt * bp, LANES), F32),
            pltpu.VMEM((2 * S5_WIDTH // LANES, tt * bp, LANES), F32),
            pltpu.VMEM((bp, S5_WIDTH), F32),
            pltpu.VMEM((bp, S5_WIDTH), F32),
        ],
        compiler_params=_cparams("arbitrary"),
        name="s5_scan",
    )(u, u, h0re, h0im, sp["bmat"].arr, sp["lre"].arr, sp["lim"].arr, sp["cmat"].arr, sp["d"].arr,
      sp["wglu"].arr)


def _run_interleaved(chains):
    live = list(chains)
    while live:
        still = []
        for ch in live:
            try:
                next(ch)
                still.append(ch)
            except StopIteration:
                pass
        live = still


def _mlstm_kernel(q_ref, k_ref, v_ref, og_ref, sm_ref, gt_ref, brow_ref, bcol_ref, norm_ref,
                  y_ref, c_out, n_out, m_out, c_scr, n_scr, m_scr, *, L, nc, G):
    c = pl.program_id(1)

    @pl.when(c == 0)
    def _():
        c_scr[...] = jnp.zeros_like(c_scr)
        n_scr[...] = jnp.zeros_like(n_scr)
        m_scr[...] = jnp.zeros_like(m_scr)

    ones = _head_ones()
    tril = _tril_ones(L)
    triu = _triu_ones(L)
    l_shift = L.bit_length() - 1
    src = _iota((LANES, N_HEADS * L), 0)
    to_scores = jnp.where(src == LANE_MI + (_iota((LANES, N_HEADS * L), 1) >> l_shift), 1.0, 0.0).astype(BF16)
    src = _iota((LANES, W_GROUP), 0)
    to_feats = jnp.where(src == LANE_MI + (_iota((LANES, W_GROUP), 1) >> HEAD_SHIFT), 1.0, 0.0).astype(BF16)
    seg = _iota((N_HEADS * L, W_GROUP), 0) >> l_shift
    score_sum = jnp.where(seg == (_iota((N_HEADS * L, W_GROUP), 1) >> HEAD_SHIFT), 1.0, 0.0).astype(BF16)
    causal = _iota((L, N_HEADS * L), 0) >= (_iota((L, N_HEADS * L), 1) & (L - 1))
    t_idx = _iota((L, LANES), 0)

    seqs = list(range(G))
    cat = lambda xs: jnp.concatenate(xs, axis=0)
    split = lambda x: [x[i * L:(i + 1) * L] for i in seqs]

    def running_max(x):
        shift = 1
        while shift < L:
            x = jnp.maximum(x, jnp.where(t_idx >= shift, pltpu.roll(x, shift, 0), NEG_BIG))
            shift *= 2
        return x

    q = [q_ref[i] for i in seqs]
    k = [k_ref[i] * (HEAD_DIM ** -0.5) for i in seqs]
    v = [v_ref[i] for i in seqs]
    C = [c_scr[i] for i in seqs]
    n = [n_scr[i, 0:1, :] for i in seqs]
    m0 = [m_scr[i, 0:1, :] for i in seqs]
    pre = [sm_ref[i] + brow_ref[...] for i in seqs]
    gtb = [gt_ref[i] + bcol_ref[...] for i in seqs]
    bc_col = _each(lambda x: _dot_exact_rhs(tril, _log_sigmoid(x)), pre)
    bc_rows = _dot_exact_lhs(_log_sigmoid(cat(gtb)), triu)
    qC = _each(_dot_nt, q, C)
    qk = _each(lambda q_, k_: _dot_nt(q_, _stack_heads(k_)), q, k)
    qn = _head_sum(cat(_each(lambda q_, n_: q_ * n_, q, n)), ones)

    b_col = _each(lambda x: pltpu.roll(x, LANES - N_HEADS, 1), bc_col)
    r_col = _each(lambda p_, b_: p_ - b_, pre, b_col)
    mu = _each(lambda r_, m_: jnp.maximum(running_max(r_), m_), r_col, m0)
    mu_last = _each(lambda x: x[L - 1:L, :], mu)
    mu_scores = split(_dot_exact_lhs(cat(mu), to_scores))

    def weights(i):
        r_rows = gtb[i][0:N_HEADS, :] - bc_rows[8 * i + N_HEADS:8 * i + 2 * N_HEADS, :]
        r_all = jnp.concatenate([r_rows[h:h + 1, :] for h in range(N_HEADS)], axis=1)
        return jnp.exp(jnp.where(causal, r_all - mu_scores[i], NEG_BIG)) * qk[i]

    ws = [weights(i) for i in seqs]
    wsv = _each(lambda w_, v_: _dot(w_, _stack_heads(v_)), ws, v)
    w0 = _dot(cat(_each(lambda m_, u_: jnp.exp(m_ - u_), m0, mu)), to_feats)
    m_row = _dot_exact_lhs(cat(_each(lambda b_, u_: b_ + u_, b_col, mu)), to_feats)
    den = w0 * qn + _dot(cat(ws), score_sum)
    hc = (w0 * cat(qC) + cat(wsv)) / jnp.maximum(jnp.abs(den), jnp.exp(-m_row))

    wl = split(_dot(cat(_each(lambda r_, u_: jnp.exp(r_ - u_), r_col, mu_last)), to_feats))
    w0l_rows = _each(lambda m_, u_: jnp.exp(m_ - u_), m0, mu_last)
    w0l_rows = w0l_rows + [jnp.zeros((1, LANES), F32)] * (-G % 8)
    w0l = _dot_exact_lhs(cat(w0l_rows), to_feats)
    kw = _each(lambda k_, w_: k_ * w_, k, wl)
    upd = _each(_dot_tn, v, kw)
    for i in seqs:
        c_scr[i] = w0l[i:i + 1, :] * C[i] + ones.astype(F32) * upd[i]
        n_scr[i, 0:1, :] = w0l[i:i + 1, :] * n[i] + jnp.sum(kw[i], axis=0, keepdims=True)
        m_scr[i, 0:1, :] = b_col[i][L - 1:L, :] + mu_last[i]

    ms = _head_sum(hc * hc, ones) * (1.0 / HEAD_DIM)
    y = split(hc * lax.rsqrt(ms + EPS) * norm_ref[...])
    for i in seqs:
        y_ref[i] = (y[i] * _sigmoid(og_ref[i])).astype(y_ref.dtype)

    @pl.when(c == nc - 1)
    def _():
        _store_head_blocks(c_out, c_scr)
        n_out[...] = jnp.broadcast_to(n_scr[:, 0:1, :], n_out.shape)
        m_out[...] = m_scr[...]


def _seq_spec(G, L, blk, width=W_GROUP):
    return pl.BlockSpec((G, L, width), lambda b, c: (b, c, blk))


def _mlstm(proj, gt, p, n_seq, t_len):
    L = CHUNK_ML
    G = SEQS_PER_STEP_ML
    nc = t_len // L
    per_seq = lambda rows, width: pl.BlockSpec((G, rows, width), lambda b, c: (b, 0, 0))
    proj3 = proj.reshape(n_seq, t_len, D_PROJ)
    y, c_new, n_new, m_new = pl.pallas_call(
        functools.partial(_mlstm_kernel, L=L, nc=nc, G=G),
        grid=(n_seq // G, nc),
        in_specs=[
            _seq_spec(G, L, BLK_MQ), _seq_spec(G, L, BLK_MK), _seq_spec(G, L, BLK_MV),
            _seq_spec(G, L, BLK_MO), _seq_spec(G, L, BLK_SMALL, SMALL_W),
            pl.BlockSpec((G, 8, L), lambda b, c: (b, 0, c)),
            p["ml_brow"].spec(), p["ml_bcol"].spec(), p["ml_norm"].spec(),
        ],
        out_specs=[
            pl.BlockSpec((G, L, W_GROUP), lambda b, c: (b, c, 0)),
            _HEAD_BLOCKS_SPEC(G), per_seq(8, W_GROUP), per_seq(8, LANES),
        ],
        out_shape=[
            jax.ShapeDtypeStruct((n_seq, t_len, W_GROUP), BF16),
            jax.ShapeDtypeStruct((n_seq, N_HEADS, HEAD_DIM, HEAD_DIM), F32),
            jax.ShapeDtypeStruct((n_seq, 8, W_GROUP), F32),
            jax.ShapeDtypeStruct((n_seq, 8, LANES), F32),
        ],
        scratch_shapes=[
            pltpu.VMEM((G, W_GROUP, W_GROUP), F32),
            pltpu.VMEM((G, 8, W_GROUP), F32),
            pltpu.VMEM((G, 8, LANES), F32),
        ],
        compiler_params=_cparams("arbitrary", "arbitrary"),
        name="mlstm_chunk",
    )(proj3, proj3, proj3, proj3, proj3, gt, p["ml_brow"].arr, p["ml_bcol"].arr, p["ml_norm"].arr)
    return y.reshape(n_seq * t_len, W_GROUP), c_new, n_new, m_new


def _gla_kernel(q_ref, k_ref, v_ref, gg_ref, sm_ref, wa_ref, ba_ref, norm_ref,
                y_ref, s_out, s_scr, *, L, nc, G):
    c = pl.program_id(1)

    @pl.when(c == 0)
    def _():
        s_scr[...] = jnp.zeros_like(s_scr)

    ones = _head_ones()
    tril = _tril_ones(L)
    causal = _iota((L, N_HEADS * L), 0) >= (_iota((L, N_HEADS * L), 1) & (L - 1))

    def chain(gi):
        q = q_ref[gi] * (HEAD_DIM ** -0.5)
        k = k_ref[gi]
        v = v_ref[gi]
        la = _log_sigmoid(_dot(sm_ref[gi], wa_ref[...]) + ba_ref[...]) * (1.0 / GLA_GATE_TEMP)
        yield
        bc = _dot_exact_rhs(tril, la)
        yield
        ref = bc[L // 2 - 1:L // 2, :]
        qr = q * jnp.exp(jnp.minimum(bc - ref, EXP_CLAMP))
        kh = k * jnp.exp(jnp.minimum(ref - bc, EXP_CLAMP))
        ST = s_scr[gi]
        att = jnp.where(causal, _dot_nt(qr, _stack_heads(kh)), 0.0)
        from_state = _dot_nt(q * jnp.exp(bc), ST)
        yield
        o = from_state + _dot(att, _stack_heads(v))
        b_last = bc[L - 1:L, :]
        kbar = k * jnp.exp(b_last - bc)
        s_scr[gi] = ST * jnp.exp(b_last) + ones.astype(F32) * _dot_tn(v, kbar)
        yield
        ms = _head_sum(o * o, ones) * (1.0 / HEAD_DIM)
        gg = gg_ref[gi]
        y = o * lax.rsqrt(ms + EPS) * norm_ref[...] * (gg * _sigmoid(gg))
        y_ref[gi] = y.astype(y_ref.dtype)

    _run_interleaved([chain(gi) for gi in range(G)])

    @pl.when(c == nc - 1)
    def _():
        _store_head_blocks(s_out, s_scr, transpose=True)


def _gla(proj, p, n_seq, t_len):
    L = CHUNK_ML
    G = SEQS_PER_STEP_ML
    nc = t_len // L
    proj3 = proj.reshape(n_seq, t_len, D_PROJ)
    y, s_new = pl.pallas_call(
        functools.partial(_gla_kernel, L=L, nc=nc, G=G),
        grid=(n_seq // G, nc),
        in_specs=[
            _seq_spec(G, L, BLK_GQ), _seq_spec(G, L, BLK_GK), _seq_spec(G, L, BLK_GV),
            _seq_spec(G, L, BLK_GG), _seq_spec(G, L, BLK_SMALL, SMALL_W),
            p["gla_wa"].spec(), p["gla_ba"].spec(), p["gla_norm"].spec(),
        ],
        out_specs=[
            pl.BlockSpec((G, L, W_GROUP), lambda b, c: (b, c, 0)),
            _HEAD_BLOCKS_SPEC(G),
        ],
        out_shape=[
            jax.ShapeDtypeStruct((n_seq, t_len, W_GROUP), BF16),
            jax.ShapeDtypeStruct((n_seq, N_HEADS, HEAD_DIM, HEAD_DIM), F32),
        ],
        scratch_shapes=[pltpu.VMEM((G, W_GROUP, W_GROUP), F32)],
        compiler_params=_cparams("arbitrary", "arbitrary"),
        name="gla_chunk",
    )(proj3, proj3, proj3, proj3, proj3, p["gla_wa"].arr, p["gla_ba"].arr, p["gla_norm"].arr)
    return y.reshape(n_seq * t_len, W_GROUP), s_new


def _rwkv_vectors(rc, prev, p_mu, p_w0, p_a0, p_kk, p_ka, p_rk, w2, a2, g2, ones):
    xm = rc + p_mu * (prev - rc)
    rr = xm[:, 0:W_GROUP]
    rk = xm[:, W_GROUP:2 * W_GROUP]
    rv = xm[:, 2 * W_GROUP:3 * W_GROUP]
    tail = xm[:, 3 * W_GROUP:RWKV_COLS]
    lw = -RWKV_DECAY_SCALE * _sigmoid(p_w0 + _dot(jnp.tanh(tail), w2))
    a = _sigmoid(p_a0 + _dot(tail, a2))
    g = _dot(_sigmoid(tail), g2)
    kk = rk * p_kk
    kk = kk * lax.rsqrt(jnp.maximum(_head_sum(kk * kk, ones), 1e-24))
    kt = rk * (1.0 + (a - 1.0) * p_ka)
    bonus = _head_sum(rr * kt * p_rk, ones) * rv
    return rr, lw, kt, rv, kk, a, g, bonus


def _head_layernorm(o, g, ones):
    mu = _head_sum(o, ones) * (1.0 / HEAD_DIM)
    oc = o - mu
    var = _head_sum(oc * oc, ones) * (1.0 / HEAD_DIM)
    return oc * lax.rsqrt(var + GN_EPS) * g


def _stack_heads(x):
    xb = x.astype(BF16)
    lane_head = _iota((1, W_GROUP), 1) >> HEAD_SHIFT
    return jnp.concatenate([jnp.where(lane_head == h, xb, jnp.zeros_like(xb)) for h in range(N_HEADS)],
                           axis=0)


def _each(f, *seqs):
    return [f(*args) for args in zip(*seqs)]


def _block_mm(x_list, y_list):
    return _each(lambda x, y: _dot(x, _stack_heads(y)), x_list, y_list)


def _unit_lower_inverse(a_list, L):
    t_idx = _iota((L, N_HEADS * L), 0)
    s_idx = _iota((L, N_HEADS * L), 1) & (L - 1)
    eye = jnp.where(t_idx == s_idx, 1.0, 0.0).astype(F32)
    in16 = (t_idx >> 4) == (s_idx >> 4)
    in32 = ((t_idx >> 5) == (s_idx >> 5)) & ((t_idx >> 4) > (s_idx >> 4))
    in64 = (t_idx >> 5) > (s_idx >> 5)
    x16 = _each(lambda a: jnp.where(in16, -a, 0.0), a_list)
    x2 = _block_mm(x16, x16)
    x4 = _block_mm(x2, x2)
    x8 = _block_mm(x4, x4)
    t = _each(lambda x: eye + x, x16)
    for xp in (x2, x4, x8):
        t = _each(lambda t_, d: t_ + d, t, _block_mm(t, xp))
    for sel in (in32, in64):
        w = _block_mm(_each(lambda a: jnp.where(sel, a, 0.0), a_list), t)
        t = _each(lambda t_, d: t_ - d, t, _block_mm(t, w))
    return t


def _rwkv_chunks(rcs, prev_rows, states, prm, ones, L):
    (mu, w0, a0, pkk, pka, prk, w2, a2, g2, norm) = prm
    first_row = _iota((L, RWKV_COLS), 0) == 0
    prevs = _each(lambda rc, pr: jnp.where(first_row, pr, pltpu.roll(rc, 1, 0)), rcs, prev_rows)
    n_seq = len(rcs)
    stacked = _rwkv_vectors(jnp.concatenate(rcs, axis=0), jnp.concatenate(prevs, axis=0),
                            mu, w0, a0, pkk, pka, prk, w2, a2, g2, ones)
    r, lw, kx, v, kk, a, g, bonus = [[x[i * L:(i + 1) * L] for i in range(n_seq)] for x in stacked]

    tril = _tril_ones(L)
    t_idx = _iota((L, N_HEADS * L), 0)
    s_idx = _iota((L, N_HEADS * L), 1) & (L - 1)
    strict = t_idx > s_idx
    incl = t_idx >= s_idx

    def core(r, lw, kx, v, kk, a, states):
        lc = _each(lambda x: _dot_exact_rhs(tril, x), lw)
        akk = _each(lambda a_, k_: a_ * k_, a, kk)
        e_neg = _each(lambda x: jnp.exp(-x), lc)
        l_last = _each(lambda x: x[L - 1:L, :], lc)
        e_end = _each(lambda ll, x: jnp.exp(ll - x), l_last, lc)
        kr = _each(lambda k_, r_, c_, w_: jnp.concatenate([k_ * jnp.exp(c_ - w_), r_ * jnp.exp(c_)], axis=0),
                   kk, r, lc, lw)
        a_hat = _each(lambda x, e: _stack_heads(x * e), akk, e_neg)
        k_hat = _each(lambda x, e: _stack_heads(x * e), kx, e_neg)
        v_bd = _each(_stack_heads, v)

        pa = _each(_dot_nt, kr, a_hat)
        pk = _each(_dot_nt, kr, k_hat)
        a_ua = _each(lambda p_: jnp.where(strict, p_[0:L], 0.0), pa)
        b_ra = _each(lambda p_: jnp.where(incl, p_[L:2 * L], 0.0), pa)
        ab_k = _each(lambda p_: jnp.concatenate([jnp.where(strict, p_[0:L], 0.0),
                                                 jnp.where(incl, p_[L:2 * L], 0.0)], axis=0), pk)
        T = _unit_lower_inverse(a_ua, L)

        from_state = _each(_dot_nt, kr, states)
        from_v = _each(_dot, ab_k, v_bd)
        rhs = _each(lambda s_, v_: s_[0:L] + v_[0:L], from_state, from_v)
        U = _block_mm(T, rhs)
        corr = _block_mm(b_ra, U)
        o = _each(lambda s_, v_, c_: s_[L:2 * L] + v_[L:2 * L] - c_, from_state, from_v, corr)
        upd = _each(lambda v_, u_, k_, a_, e: _dot_tn(jnp.concatenate([v_, u_], axis=0),
                                                      jnp.concatenate([k_ * e, -(a_ * e)], axis=0)),
                    v, U, kx, akk, e_end)
        s_new = _each(lambda s_, ll, d: s_ * jnp.exp(ll) + ones.astype(F32) * d, states, l_last, upd)
        return o, s_new

    o, s_new = core(r, lw, kx, v, kk, a, states)
    y_all = ((_head_layernorm(jnp.concatenate(o, axis=0), norm, ones) + stacked[7]) * stacked[6])
    return [y_all[i * L:(i + 1) * L] for i in range(n_seq)], s_new


def _rwkv_kernel(rc_ref, mu_ref, w0_ref, a0_ref, kk_ref, ka_ref, rk_ref, w2_ref, a2_ref, g2_ref,
                 norm_ref, y_ref, s_out, shift_out, s_scr, prev_scr, *, L, nc, G):
    c = pl.program_id(1)

    @pl.when(c == 0)
    def _():
        s_scr[...] = jnp.zeros_like(s_scr)
        prev_scr[...] = jnp.zeros_like(prev_scr)

    ones = _head_ones()
    prm = (mu_ref[...], w0_ref[...], a0_ref[...], kk_ref[...], ka_ref[...], rk_ref[...],
           w2_ref[...], a2_ref[...], g2_ref[...], norm_ref[...])
    rcs = [rc_ref[gi] for gi in range(G)]
    ys, s_new = _rwkv_chunks(rcs, [prev_scr[gi, 0:1, :] for gi in range(G)],
                             [s_scr[gi] for gi in range(G)], prm, ones, L)
    for gi in range(G):
        prev_scr[gi, 0:1, :] = rcs[gi][L - 1:L, :]
        s_scr[gi] = s_new[gi]
        y_ref[gi] = ys[gi].astype(y_ref.dtype)

    @pl.when(c == nc - 1)
    def _():
        _store_head_blocks(s_out, s_scr)
        shift_out[...] = jnp.broadcast_to(prev_scr[:, 0:1, :], shift_out.shape)


def _rwkv(proj, p, n_seq, t_len):
    L = CHUNK_RW
    G = SEQS_PER_STEP
    nc = t_len // L
    names = ("rw_mu", "rw_w0", "rw_a0", "rw_kk", "rw_ka", "rw_rk", "rw_w2", "rw_a2", "rw_g2", "rw_norm")
    y, s_new, shift = pl.pallas_call(
        functools.partial(_rwkv_kernel, L=L, nc=nc, G=G),
        grid=(n_seq // G, nc),
        in_specs=[
            pl.BlockSpec((G, L, RWKV_COLS), lambda b, c: (b, c, 0)),
        ] + [p[k].spec() for k in names],
        out_specs=[
            pl.BlockSpec((G, L, W_GROUP), lambda b, c: (b, c, 0)),
            _HEAD_BLOCKS_SPEC(G),
            pl.BlockSpec((G, 8, RWKV_COLS), lambda b, c: (b, 0, 0)),
        ],
        out_shape=[
            jax.ShapeDtypeStruct((n_seq, t_len, W_GROUP), BF16),
            jax.ShapeDtypeStruct((n_seq, N_HEADS, HEAD_DIM, HEAD_DIM), F32),
            jax.ShapeDtypeStruct((n_seq, 8, RWKV_COLS), F32),
        ],
        scratch_shapes=[
            pltpu.VMEM((G, W_GROUP, W_GROUP), F32),
            pltpu.VMEM((G, 8, RWKV_COLS), F32),
        ],
        compiler_params=_cparams("arbitrary", "arbitrary"),
        name="rwkv_chunk",
    )(proj.reshape(n_seq, t_len, D_PROJ), *[p[k].arr for k in names])
    return y.reshape(n_seq * t_len, W_GROUP), s_new, shift


FF_CHUNK = 256


def _mix_residual(x_ref, ys, wout_ref):
    acc = x_ref[...]
    for j, y in enumerate(ys):
        acc = acc + jnp.dot(y.astype(BF16), wout_ref[j * W_GROUP:(j + 1) * W_GROUP, :],
                            preferred_element_type=F32)
    return acc


def _ffn_body(x1, nrm_ref, wup_ref, cw_ref, cb_ref, wdn_ref, prev_rows, d_ff, act_scr):
    h2 = _rmsnorm_rows(x1, nrm_ref[...]).astype(BF16)
    for j in range(d_ff // FF_CHUNK):
        lo, hi = j * FF_CHUNK, (j + 1) * FF_CHUNK
        ug = jnp.dot(h2, wup_ref[:, lo:hi], preferred_element_type=F32)
        uv = jnp.dot(h2, wup_ref[:, d_ff + lo:d_ff + hi], preferred_element_type=F32)
        p2, p1 = prev_rows(j, ug)
        conv = (cb_ref[:, lo:hi] + cw_ref[0:1, lo:hi] * p2 + cw_ref[1:2, lo:hi] * p1
                + cw_ref[2:3, lo:hi] * ug)
        act_scr[:, lo:hi] = (_gelu(conv) * uv).astype(BF16)
    return x1 + jnp.dot(act_scr[...], wdn_ref[...], preferred_element_type=F32)


def _ffn_seq_kernel(x_ref, y0_ref, y1_ref, y2_ref, y3_ref, wout_ref, nrm_ref, wup_ref, cw_ref, cb_ref,
                    wdn_ref, nf_ref, o_ref, buf_out, carry_scr, act_scr, *, tm, d_ff, final):
    t = pl.program_id(1)

    @pl.when(t == 0)
    def _():
        carry_scr[...] = jnp.zeros_like(carry_scr)

    x1 = _mix_residual(x_ref, (y0_ref[...], y1_ref[...], y2_ref[...], y3_ref[...]), wout_ref)
    row = _iota((tm, FF_CHUNK), 0)

    def prev_rows(j, ug):
        lo, hi = j * FF_CHUNK, (j + 1) * FF_CHUNK
        c0 = carry_scr[0:1, lo:hi]
        c1 = carry_scr[1:2, lo:hi]
        p1 = jnp.where(row == 0, c1, pltpu.roll(ug, 1, 0))
        p2 = jnp.where(row == 0, c0, jnp.where(row == 1, c1, pltpu.roll(ug, 2, 0)))
        carry_scr[0:2, lo:hi] = ug[tm - 2:tm, :]
        return p2, p1

    out = _ffn_body(x1, nrm_ref, wup_ref, cw_ref, cb_ref, wdn_ref, prev_rows, d_ff, act_scr)
    if final:
        out = _rmsnorm_rows(out, nf_ref[...])
    o_ref[...] = out
    buf_out[0] = carry_scr[...]


def _ffn_step_kernel(x_ref, y0_ref, y1_ref, y2_ref, y3_ref, wout_ref, nrm_ref, wup_ref, cw_ref, cb_ref,
                     wdn_ref, nf_ref, p2_ref, p1_ref, o_ref, ug_out, act_scr, *, d_ff, final):
    x1 = _mix_residual(x_ref, (y0_ref[...], y1_ref[...], y2_ref[...], y3_ref[...]), wout_ref)

    def prev_rows(j, ug):
        lo, hi = j * FF_CHUNK, (j + 1) * FF_CHUNK
        ug_out[:, lo:hi] = ug
        return p2_ref[:, lo:hi], p1_ref[:, lo:hi]

    out = _ffn_body(x1, nrm_ref, wup_ref, cw_ref, cb_ref, wdn_ref, prev_rows, d_ff, act_scr)
    if final:
        out = _rmsnorm_rows(out, nf_ref[...])
    o_ref[...] = out


def _ffn_seq(x2d, ys, p, norm_final, n_seq, t_len, tm, final):
    n, d = x2d.shape
    d_ff = p["ffn_wdn"].shape[0]
    nt = t_len // tm
    rows = lambda w: pl.BlockSpec((tm, w), lambda b, t: (b * nt + t, 0))
    once = dict(pipeline_mode=pl.Buffered(1))
    return pl.pallas_call(
        functools.partial(_ffn_seq_kernel, tm=tm, d_ff=d_ff, final=final),
        grid=(n_seq, nt),
        in_specs=[rows(d), pl.BlockSpec((tm, W_GROUP), lambda b, t: (t, b))] + [rows(W_GROUP)] * 3 + [
            p["w_out"].spec(**once), p["norm_ffn"].spec(), p["ffn_wup"].spec(**once), p["ffn_cw"].spec(),
            p["ffn_cb"].spec(), p["ffn_wdn"].spec(**once), norm_final.spec()],
        out_specs=[rows(d), pl.BlockSpec((1, CONV_W - 1, d_ff), lambda b, t: (b, 0, 0))],
        out_shape=[jax.ShapeDtypeStruct((n, d), F32),
                   jax.ShapeDtypeStruct((n_seq, CONV_W - 1, d_ff), F32)],
        scratch_shapes=[pltpu.VMEM((CONV_W - 1, d_ff), F32), pltpu.VMEM((tm, d_ff), BF16)],
        compiler_params=_cparams("arbitrary", "arbitrary"),
        name="wout_ffn_seq",
    )(x2d, *ys, p["w_out"].arr, p["norm_ffn"].arr, p["ffn_wup"].arr, p["ffn_cw"].arr, p["ffn_cb"].arr,
      p["ffn_wdn"].arr, norm_final.arr)


def _ffn_step(x2d, ys, p, norm_final, prev2, prev1, final):
    n, d = x2d.shape
    d_ff = p["ffn_wdn"].shape[0]
    full = lambda shape: pl.BlockSpec(shape, lambda i: (0,) * len(shape))
    return pl.pallas_call(
        functools.partial(_ffn_step_kernel, d_ff=d_ff, final=final),
        grid=(1,),
        in_specs=[full((n, d))] + [full((n, W_GROUP))] * 4 + [
            p["w_out"].spec(), p["norm_ffn"].spec(), p["ffn_wup"].spec(), p["ffn_cw"].spec(),
            p["ffn_cb"].spec(), p["ffn_wdn"].spec(), norm_final.spec(), full((n, d_ff)), full((n, d_ff))],
        out_specs=[full((n, d)), full((n, d_ff))],
        out_shape=[jax.ShapeDtypeStruct((n, d), F32), jax.ShapeDtypeStruct((n, d_ff), F32)],
        scratch_shapes=[pltpu.VMEM((n, d_ff), BF16)],
        compiler_params=_cparams("arbitrary"),
        name="wout_ffn_step",
    )(x2d, *ys, p["w_out"].arr, p["norm_ffn"].arr, p["ffn_wup"].arr, p["ffn_cw"].arr, p["ffn_cb"].arr,
      p["ffn_wdn"].arr, norm_final.arr, prev2, prev1)


def _step_vec_kernel(proj_ref, shift_ref, m_ref, brow_ref, wa_ref, ba_ref, mu_ref, w0_ref, a0_ref,
                     kk_ref, ka_ref, rk_ref, w2_ref, a2_ref, g2_ref,
                     pack_out, m_out, g_out, bonus_out):
    ones = _head_ones()
    sm = proj_ref[:, OFF_SMALL:OFF_SMALL + SMALL_W]
    pre = sm + brow_ref[...]
    logf = pltpu.roll(_log_sigmoid(pre), LANES - N_HEADS, 1)
    m_old = m_ref[...]
    m_new = jnp.maximum(logf + m_old, pre)
    m_out[...] = m_new
    lane = _iota((SMALL_W, W_GROUP), 0)
    spread_mat = jnp.where(lane == LANE_MI + (_iota((SMALL_W, W_GROUP), 1) >> HEAD_SHIFT), 1.0, 0.0).astype(BF16)
    gate_lanes = (_iota(pre.shape, 1) >= LANE_MI) & (_iota(pre.shape, 1) < LANE_MI + N_HEADS)
    spread = lambda x: _dot_exact_lhs(jnp.where(gate_lanes, x, 0.0), spread_mat)
    la = _log_sigmoid(_dot(sm, wa_ref[...]) + ba_ref[...]) * (1.0 / GLA_GATE_TEMP)
    rc = proj_ref[:, 0:RWKV_COLS]
    rr, lw, kt, rv, kk, a, g, bonus = _rwkv_vectors(
        rc, shift_ref[...], mu_ref[...], w0_ref[...], a0_ref[...], kk_ref[...], ka_ref[...],
        rk_ref[...], w2_ref[...], a2_ref[...], g2_ref[...], ones)
    main = lambda blk: proj_ref[:, blk * W_GROUP:(blk + 1) * W_GROUP]
    rows = {
        ROW_MQ: main(BLK_MQ), ROW_MK: main(BLK_MK), ROW_MV: main(BLK_MV),
        ROW_GQ: main(BLK_GQ), ROW_GK: main(BLK_GK), ROW_GV: main(BLK_GV), ROW_GA: jnp.exp(la),
        ROW_KK: kk, ROW_AKK: a * kk, ROW_W: jnp.exp(lw), ROW_KT: kt, ROW_RR: rr, ROW_RV: rv,
        ROW_IW: spread(jnp.exp(pre - m_new)), ROW_FW: spread(jnp.exp(logf + m_old - m_new)),
        ROW_M: spread(m_new),
    }
    for i, val in rows.items():
        pack_out[i] = val.T
    g_out[...] = g
    bonus_out[...] = bonus


def _step_vectors(proj, shift, m_pad, p):
    n = proj.shape[0]
    full = lambda a: pl.BlockSpec(a.shape, lambda i: (0,) * a.ndim)
    data = (proj, shift, m_pad)
    prms = [p[k] for k in ("ml_brow", "gla_wa", "gla_ba", "rw_mu", "rw_w0", "rw_a0", "rw_kk", "rw_ka",
                           "rw_rk", "rw_w2", "rw_a2", "rw_g2")]
    wide = jax.ShapeDtypeStruct((n, W_GROUP), F32)
    outs = [jax.ShapeDtypeStruct((N_PACK, W_GROUP, n), F32), jax.ShapeDtypeStruct((n, SMALL_W), F32),
            wide, wide]
    return pl.pallas_call(
        _step_vec_kernel,
        grid=(1,),
        in_specs=[full(a) for a in data] + [q.spec() for q in prms],
        out_specs=[pl.BlockSpec(o.shape, lambda i, nd=len(o.shape): (0,) * nd) for o in outs],
        out_shape=outs,
        compiler_params=_cparams("arbitrary"),
        name="step_vectors",
    )(*data, *[q.arr for q in prms])


(ROW_MQ, ROW_MK, ROW_MV, ROW_GQ, ROW_GK, ROW_GV, ROW_GA, ROW_KK, ROW_AKK, ROW_W, ROW_KT, ROW_RR, ROW_RV,
 ROW_IW, ROW_FW, ROW_M) = range(16)
N_PACK = 16
(OROW_ML, OROW_GLA, OROW_RW, OROW_N) = range(4)
N_OPACK = 4
STEP_UNROLL = 4


def _heads_step_kernel(pack_ref, c_ref, n_ref, gs_ref, rs_ref, c_out, gs_out, rs_out, opack_out):
    vec = lambda i: pack_ref[i]
    one = lambda i, j: pack_ref[i, pl.ds(j, 1), :]
    colsum = lambda x: jnp.sum(x, axis=0, keepdims=True)

    k_ml = vec(ROW_MK) * (HEAD_DIM ** -0.5)
    q_ml = vec(ROW_MQ)
    iw = pack_ref[ROW_IW, 0:1, :]
    fw = pack_ref[ROW_FW, 0:1, :]
    n_new = fw * n_ref[...] + iw * k_ml
    opack_out[OROW_N] = n_new
    r_den = 1.0 / jnp.maximum(jnp.abs(colsum(n_new * q_ml)), jnp.exp(-pack_ref[ROW_M, 0:1, :]))

    def ml_body(v, carry):
        c_new = fw * c_ref[v] + (iw * one(ROW_MV, v)) * k_ml
        c_out[v] = c_new
        opack_out[OROW_ML, pl.ds(v, 1), :] = colsum(c_new * q_ml) * r_den
        return carry

    lax.fori_loop(0, HEAD_DIM, ml_body, 0, unroll=STEP_UNROLL)

    v_gla = vec(ROW_GV)

    def gla_body(k, acc):
        s_new = one(ROW_GA, k) * gs_ref[k] + one(ROW_GK, k) * v_gla
        gs_out[k] = s_new
        return acc + (one(ROW_GQ, k) * (HEAD_DIM ** -0.5)) * s_new

    opack_out[OROW_GLA] = lax.fori_loop(0, HEAD_DIM, gla_body, jnp.zeros_like(v_gla), unroll=STEP_UNROLL)

    kk, akk, w, kt, rr = vec(ROW_KK), vec(ROW_AKK), vec(ROW_W), vec(ROW_KT), vec(ROW_RR)

    def rw_body(v, carry):
        s = rs_ref[v]
        s_new = s * w - colsum(s * kk) * akk + one(ROW_RV, v) * kt
        rs_out[v] = s_new
        opack_out[OROW_RW, pl.ds(v, 1), :] = colsum(s_new * rr)
        return carry

    lax.fori_loop(0, HEAD_DIM, rw_body, 0, unroll=STEP_UNROLL)


def _heads_step(pack, ml_c, ml_n, gla_s, rw_s, layer):
    n = pack.shape[-1]
    sq = (HEAD_DIM, HEAD_DIM, n)
    head = lambda lead: pl.BlockSpec((lead, HEAD_DIM, n), lambda h: (0, h, 0))
    at_layer = lambda shape: pl.BlockSpec((None,) + shape, lambda h: (layer * N_HEADS + h,) + (0,) * len(shape))
    out_state = pl.BlockSpec((None,) + sq, lambda h: (h, 0, 0, 0))
    state = jax.ShapeDtypeStruct((N_HEADS,) + sq, F32)
    return pl.pallas_call(
        _heads_step_kernel,
        grid=(N_HEADS,),
        in_specs=[head(N_PACK), at_layer(sq), at_layer((HEAD_DIM, n)), at_layer(sq), at_layer(sq)],
        out_specs=[out_state, out_state, out_state, head(N_OPACK)],
        out_shape=[state, state, state, jax.ShapeDtypeStruct((N_OPACK, W_GROUP, n), F32)],
        compiler_params=_cparams("arbitrary"),
        name="heads_step",
    )(pack, ml_c, ml_n, gla_s, rw_s)


def _step_post_kernel(hml_ref, mo_ref, ogla_ref, gg_ref, orw_ref, g_ref, bonus_ref,
                      mln_ref, gln_ref, rwn_ref, yml_out, ygla_out, yrw_out):
    ones = _head_ones()
    hml = hml_ref[...].T
    ms = _head_sum(hml * hml, ones) * (1.0 / HEAD_DIM)
    yml_out[...] = (hml * lax.rsqrt(ms + EPS) * mln_ref[...] * _sigmoid(mo_ref[...])).astype(BF16)
    og = ogla_ref[...].T
    ms = _head_sum(og * og, ones) * (1.0 / HEAD_DIM)
    gg = gg_ref[...]
    ygla_out[...] = (og * lax.rsqrt(ms + EPS) * gln_ref[...] * (gg * _sigmoid(gg))).astype(BF16)
    yrw_out[...] = ((_head_layernorm(orw_ref[...].T, rwn_ref[...], ones) + bonus_ref[...])
                    * g_ref[...]).astype(BF16)


def _step_post(hml, mo, ogla, gg, orw, g, bonus, p):
    n = mo.shape[0]
    data = (hml, mo, ogla, gg, orw, g, bonus)
    prms = [p[k] for k in ("ml_norm", "gla_norm", "rw_norm")]
    full = lambda a: pl.BlockSpec(a.shape, lambda i: (0,) * a.ndim)
    out = jax.ShapeDtypeStruct((n, W_GROUP), BF16)
    return pl.pallas_call(
        _step_post_kernel,
        grid=(1,),
        in_specs=[full(a) for a in data] + [q.spec() for q in prms],
        out_specs=[pl.BlockSpec((n, W_GROUP), lambda i: (0, 0))] * 3,
        out_shape=[out] * 3,
        compiler_params=_cparams("arbitrary"),
        name="step_post",
    )(*data, *[q.arr for q in prms])


def _prepare_params(prm):
    P = {}
    w_in = prm["w_in"].astype(BF16)
    depth, d = w_in.shape[0], w_in.shape[1]
    sizes = (W_GROUP, W_GROUP, W_GROUP, W_GROUP, N_HEADS, N_HEADS, W_GROUP, W_GROUP, W_GROUP, W_GROUP,
             GLA_RANK, W_GROUP, RWKV_COLS)
    offs = [0]
    for sz in sizes:
        offs.append(offs[-1] + sz)
    col = lambda i: w_in[:, :, offs[i]:offs[i + 1]]
    (u, mq, mk, mv, mi, mf, mo, gq, gk, gv, ga, gg, rcols) = [col(i) for i in range(13)]
    small = jnp.concatenate([ga, mi, mf, jnp.zeros((depth, d, SMALL_W - GLA_RANK - 2 * N_HEADS), BF16)], axis=2)
    P["w_in"] = jnp.concatenate([rcols, small, u, mq, mk, mv, mo, gq, gk, gv, gg], axis=2)
    P["w_gate_t"] = jnp.swapaxes(jnp.concatenate([mi, mf], axis=2), 1, 2)

    lam = lax.complex(prm["s5_lam_re"], prm["s5_lam_im"])
    dt = jnp.exp(prm["s5_log_dt"])[..., None]
    lam_bar = jnp.exp(lam * dt)
    b_bar = ((lam_bar - 1.0) / lam)[..., None] * lax.complex(prm["s5_b_re"], prm["s5_b_im"])
    eye = jnp.eye(S5_GROUPS, dtype=F32)
    bm = lambda b: jnp.einsum("lgph,gk->lghkp", b, eye).reshape(depth, W_GROUP, S5_WIDTH)
    P["bmat"] = jnp.concatenate([bm(b_bar.real), bm(b_bar.imag)], axis=2).astype(BF16)
    cm = lambda c: jnp.einsum("lghp,gk->lkpgh", c, eye).reshape(depth, S5_WIDTH, W_GROUP)
    P["cmat"] = jnp.concatenate([cm(prm["s5_c_re"]), -cm(prm["s5_c_im"])], axis=1).astype(BF16)
    P["lam"] = jnp.stack([lam_bar.real.reshape(depth, S5_WIDTH), lam_bar.imag.reshape(depth, S5_WIDTH)],
                         axis=1)[:, :, None, :]
    P["wglu"] = prm["s5_w_glu"].astype(BF16)

    bias = prm["ml_gate_bias"]
    P["ml_brow"] = jnp.pad(bias, ((0, 0), (LANE_MI, SMALL_W - LANE_MI - 2 * N_HEADS)))[:, None, :]
    P["ml_bcol"] = jnp.broadcast_to(bias[:, :, None], (depth, 2 * N_HEADS, CHUNK_ML))

    P["rows"] = jnp.stack([prm["s5_d"], prm["ml_norm"], prm["gla_b_alpha"], prm["gla_norm"], prm["rw_w0"],
                           prm["rw_a0"], prm["rw_k_k"], prm["rw_k_a"], prm["rw_r_k"].reshape(depth, W_GROUP),
                           prm["rw_norm"]], axis=1)[:, :, None, :]
    pad_rows = lambda w, lo: jnp.pad(w, ((0, 0), (lo, LANES - lo - w.shape[1]), (0, 0)))
    P["pads"] = jnp.stack([pad_rows(prm["gla_w_alpha"], 0), pad_rows(prm["rw_w2"], 0),
                           pad_rows(prm["rw_a2"], RWKV_W_RANK),
                           pad_rows(prm["rw_g2"], RWKV_W_RANK + RWKV_A_RANK)], axis=1).astype(BF16)
    P["rw_mu"] = prm["rw_mu"][:, None, :]
    P["norms"] = jnp.stack([prm["norm_mix"], prm["norm_ffn"]], axis=1)[:, :, None, :]
    P["w_out"] = prm["w_out"].astype(BF16)
    P["ffn_wup"] = prm["ffn_w_up"].astype(BF16)
    P["ffn_cw"] = prm["ffn_conv_w"]
    P["ffn_cb"] = prm["ffn_conv_b"][:, None, :]
    P["ffn_wdn"] = prm["ffn_w_down"].astype(BF16)
    return P


ROW_NAMES = ("d", "ml_norm", "gla_ba", "gla_norm", "rw_w0", "rw_a0", "rw_kk", "rw_ka", "rw_rk", "rw_norm")
PAD_NAMES = ("gla_wa", "rw_w2", "rw_a2", "rw_g2")


def _layer_params(P, l):
    p = {k: _Param(P[k], l) for k in ("w_in", "w_gate_t", "bmat", "cmat", "wglu", "ml_brow", "ml_bcol",
                                      "rw_mu", "w_out", "ffn_wup", "ffn_cw", "ffn_cb", "ffn_wdn")}
    p.update({k: _Param(P["rows"], l, i) for i, k in enumerate(ROW_NAMES)})
    p.update({k: _Param(P["pads"], l, i) for i, k in enumerate(PAD_NAMES)})
    p["lre"], p["lim"] = _Param(P["lam"], l, 0), _Param(P["lam"], l, 1)
    p["norm_mix"], p["norm_ffn"] = _Param(P["norms"], l, 0), _Param(P["norms"], l, 1)
    return p


def _prompt_layer(x2d, p, norm_final, n_seq, t_len, final):
    proj, gt, u = _inproj(x2d, p["norm_mix"], p["w_in"], p["w_gate_t"], n_seq, t_len, tm=INPROJ_ROWS)

    zeros = jnp.zeros((n_seq, S5_WIDTH), F32)
    y_s5, s5_re, s5_im = _s5(u, zeros, zeros, p, n_seq, t_len, tt=S5_TILE_STEPS, wide=True)

    y_ml, ml_c, ml_n, ml_m = _mlstm(proj, gt, p, n_seq, t_len)
    y_gla, gla_st = _gla(proj, p, n_seq, t_len)
    y_rw, rw_s, rw_shift = _rwkv(proj, p, n_seq, t_len)

    x_new, ffn_buf = _ffn_seq(x2d, (y_s5, y_ml, y_gla, y_rw), p, norm_final, n_seq, t_len, tm=FFN_ROWS,
                              final=final)
    states = (
        s5_re.reshape(n_seq, S5_GROUPS, S5_STATE),
        s5_im.reshape(n_seq, S5_GROUPS, S5_STATE),
        ml_c,
        ml_n[:, 0, :].reshape(n_seq, N_HEADS, HEAD_DIM),
        ml_m[:, 0, LANE_MI:LANE_MI + N_HEADS],
        gla_st,
        rw_s,
        rw_shift[:, 0, :],
        ffn_buf,
    )
    return x_new, states


def _sample_layer(x2d, st, pair_states, layer, p, norm_final, final):
    (s5_re, s5_im, ml_m, rw_shift, ffn_buf) = st
    n = x2d.shape[0]
    npairs = n * N_HEADS
    proj, _, u = _inproj(x2d, p["norm_mix"], p["w_in"], p["w_gate_t"], 1, n, tm=n)

    y_s5, s5_re_new, s5_im_new = _s5(u,
                                     s5_re.reshape(n, S5_WIDTH), s5_im.reshape(n, S5_WIDTH),
                                     p, n, 1, tt=1, wide=False)

    m_pad = jnp.zeros((n, SMALL_W), F32).at[:, LANE_MI:LANE_MI + N_HEADS].set(ml_m)
    pack, m_new, g, bonus = _step_vectors(proj, rw_shift, m_pad, p)
    ml_c_new, gla_s_new, rw_s_new, opack = _heads_step(pack, *pair_states, layer)

    main = lambda blk: proj[:, blk * W_GROUP:(blk + 1) * W_GROUP]
    y_ml, y_gla, y_rw = _step_post(opack[OROW_ML], main(BLK_MO), opack[OROW_GLA], main(BLK_GG),
                                   opack[OROW_RW], g, bonus, p)
    x_new, ug = _ffn_step(x2d, (y_s5, y_ml, y_gla, y_rw), p, norm_final, ffn_buf[:, 0, :], ffn_buf[:, 1, :],
                          final)
    states = (
        s5_re_new.reshape(n, S5_GROUPS, S5_STATE),
        s5_im_new.reshape(n, S5_GROUPS, S5_STATE),
        ml_c_new.transpose(3, 0, 1, 2),
        opack[OROW_N].reshape(N_HEADS, HEAD_DIM, n).transpose(2, 0, 1),
        m_new[:, LANE_MI:LANE_MI + N_HEADS],
        gla_s_new.transpose(3, 0, 1, 2),
        rw_s_new.transpose(3, 0, 1, 2),
        proj[:, 0:RWKV_COLS],
        jnp.stack([ffn_buf[:, 1, :], ug], axis=1),
    )
    return x_new, states


def kernel(x_prompt, x_sample, state_s5_re, state_s5_im, state_mlstm_C, state_mlstm_n, state_mlstm_m, state_gla_S, state_rwkv_S, state_rwkv_shift, state_ffn_conv, norm_mix, w_in, s5_lam_re, s5_lam_im, s5_log_dt, s5_b_re, s5_b_im, s5_c_re, s5_c_im, s5_d, s5_w_glu, ml_gate_bias, ml_norm, gla_w_alpha, gla_b_alpha, gla_norm, rw_mu, rw_w0, rw_w2, rw_a0, rw_a2, rw_g2, rw_k_k, rw_k_a, rw_r_k, rw_norm, w_out, norm_ffn, ffn_w_up, ffn_conv_w, ffn_conv_b, ffn_w_down, norm_final):
    prm = dict(norm_mix=norm_mix, w_in=w_in, s5_lam_re=s5_lam_re, s5_lam_im=s5_lam_im, s5_log_dt=s5_log_dt,
               s5_b_re=s5_b_re, s5_b_im=s5_b_im, s5_c_re=s5_c_re, s5_c_im=s5_c_im, s5_d=s5_d,
               s5_w_glu=s5_w_glu, ml_gate_bias=ml_gate_bias, ml_norm=ml_norm, gla_w_alpha=gla_w_alpha,
               gla_b_alpha=gla_b_alpha, gla_norm=gla_norm, rw_mu=rw_mu, rw_w0=rw_w0, rw_w2=rw_w2,
               rw_a0=rw_a0, rw_a2=rw_a2, rw_g2=rw_g2, rw_k_k=rw_k_k, rw_k_a=rw_k_a, rw_r_k=rw_r_k,
               rw_norm=rw_norm, w_out=w_out, norm_ffn=norm_ffn, ffn_w_up=ffn_w_up, ffn_conv_w=ffn_conv_w,
               ffn_conv_b=ffn_conv_b, ffn_w_down=ffn_w_down)
    depth = w_in.shape[0]
    n_seq, t_len, d = x_prompt.shape
    n_smp = x_sample.shape[0]
    assert t_len % max(INPROJ_ROWS, FFN_ROWS, S5_TILE_STEPS, CHUNK_ML) == 0
    assert n_seq % max(SEQS_PER_STEP, SEQS_PER_STEP_ML) == 0 and x_sample.shape[1] == 1
    nf = _Param(norm_final[None, :])
    small_states = (state_s5_re, state_s5_im, state_mlstm_m, state_rwkv_shift, state_ffn_conv)
    lead = depth * N_HEADS
    to_lanes = lambda st: jnp.moveaxis(st, 1, -1)
    pair_states = (to_lanes(state_mlstm_C).reshape(lead, HEAD_DIM, HEAD_DIM, n_smp),
                   to_lanes(state_mlstm_n).reshape(lead, HEAD_DIM, n_smp),
                   to_lanes(state_gla_S).reshape(lead, HEAD_DIM, HEAD_DIM, n_smp),
                   to_lanes(state_rwkv_S).reshape(lead, HEAD_DIM, HEAD_DIM, n_smp))

    P = _prepare_params(prm)
    xp = x_prompt.reshape(n_seq * t_len, d)
    xs = x_sample.reshape(n_smp, d)
    p_states, s_states = [], []
    for l in range(depth):
        p = _layer_params(P, l)
        final = l == depth - 1
        xp, ps = _prompt_layer(xp, p, nf, n_seq, t_len, final)
        xs, ss = _sample_layer(xs, tuple(s[l] for s in small_states), pair_states, l, p, nf, final)
        p_states.append(ps)
        s_states.append(ss)
    new_p = tuple(jnp.stack([st[i] for st in p_states]) for i in range(9))
    new_s = tuple(jnp.stack([st[i] for st in s_states]) for i in range(9))
    return (xp.reshape(n_seq, t_len, d), xs.reshape(n_smp, 1, d)) + new_p + new_s
```

```python
import functools
import math

import jax
import jax.numpy as jnp
from jax import lax
from jax.experimental import pallas as pl
from jax.experimental.pallas import tpu as pltpu

F32 = jnp.float32
BF16 = jnp.bfloat16

LANES = 128
W_GROUP = 256
HEAD_DIM = 64
HEAD_SHIFT = 6
N_HEADS = 4
S5_GROUPS = 16
S5_STATE = 64
S5_WIDTH = S5_GROUPS * S5_STATE
GLA_RANK = 16
GLA_GATE_TEMP = 16.0
RWKV_W_RANK = 32
RWKV_A_RANK = 32
RWKV_G_RANK = 64
RWKV_COLS = 3 * W_GROUP + RWKV_W_RANK + RWKV_A_RANK + RWKV_G_RANK
RWKV_DECAY_SCALE = 0.6065306597126334
CONV_W = 3
EPS = 1e-6
GN_EPS = 64e-5
NEG_BIG = -1e30
EXP_CLAMP = 80.0

SMALL_W = LANES
OFF_SMALL = RWKV_COLS
OFF_MAIN = RWKV_COLS + SMALL_W
D_PROJ = OFF_MAIN + 9 * W_GROUP
LANE_MI = GLA_RANK
LANE_MF = GLA_RANK + N_HEADS
(BLK_U, BLK_MQ, BLK_MK, BLK_MV, BLK_MO, BLK_GQ, BLK_GK, BLK_GV, BLK_GG) = range(
    OFF_MAIN // W_GROUP, OFF_MAIN // W_GROUP + 9)
BLK_SMALL = OFF_SMALL // SMALL_W

CHUNK_ML = 128
CHUNK_RW = 64
SEQS_PER_STEP = 8
SEQS_PER_STEP_ML = 8
INPROJ_ROWS = 1024
FFN_ROWS = 1024
S5_TILE_STEPS = 128
VMEM_LIMIT = 56 * 1024 * 1024


def _cparams(*sem):
    return pltpu.CompilerParams(dimension_semantics=sem, vmem_limit_bytes=VMEM_LIMIT)


class _Param:
    def __init__(self, arr, *lead):
        self.arr, self.lead = arr, tuple(lead)

    @property
    def shape(self):
        return tuple(self.arr.shape[len(self.lead):])

    def spec(self, **kw):
        lead, tail = self.lead, (0,) * len(self.shape)
        return pl.BlockSpec((None,) * len(lead) + self.shape, lambda *grid_idx: lead + tail, **kw)


def _dot(a, b):
    return jnp.dot(a.astype(BF16), b.astype(BF16), preferred_element_type=F32)


def _dot_nt(a, b):
    return lax.dot_general(a.astype(BF16), b.astype(BF16), (((1,), (1,)), ((), ())),
                           preferred_element_type=F32)


def _dot_tn(a, b):
    return lax.dot_general(a.astype(BF16), b.astype(BF16), (((0,), (0,)), ((), ())),
                           preferred_element_type=F32)


def _split2(x):
    hi = x.astype(BF16)
    lo = (x - hi.astype(F32)).astype(BF16)
    return hi, lo


def _dot_exact_rhs(a01, x):
    hi, lo = _split2(x)
    f = lambda p: jnp.dot(a01, p, preferred_element_type=F32)
    return f(hi) + f(lo)


def _dot_exact_lhs(x, b01):
    hi, lo = _split2(x)
    f = lambda p: jnp.dot(p, b01, preferred_element_type=F32)
    return f(hi) + f(lo)


def _sigmoid(x):
    return 1.0 / (1.0 + jnp.exp(-x))


def _log_sigmoid(x):
    return jnp.minimum(x, 0.0) - jnp.log(1.0 + jnp.exp(-jnp.abs(x)))


def _gelu(x):
    return 0.5 * x * (1.0 + jnp.tanh(math.sqrt(2.0 / math.pi) * (x + 0.044715 * (x * x * x))))


def _iota(shape, dim):
    return lax.broadcasted_iota(jnp.int32, shape, dim)


def _head_ones():
    r = _iota((W_GROUP, W_GROUP), 0) >> HEAD_SHIFT
    c = _iota((W_GROUP, W_GROUP), 1) >> HEAD_SHIFT
    return jnp.where(r == c, 1.0, 0.0).astype(BF16)


def _head_sum(x, ones):
    return _dot_exact_lhs(x, ones)


def _rmsnorm_rows(x, g):
    ms = jnp.mean(x * x, axis=-1, keepdims=True)
    return x * lax.rsqrt(ms + EPS) * g


def _tril_ones(n, strict=False):
    r = _iota((n, n), 0)
    c = _iota((n, n), 1)
    return jnp.where((r > c) if strict else (r >= c), 1.0, 0.0).astype(BF16)


def _triu_ones(n):
    r = _iota((n, n), 0)
    c = _iota((n, n), 1)
    return jnp.where(r <= c, 1.0, 0.0).astype(BF16)


def _store_head_blocks(out_ref, scr_ref, transpose=False):
    for g in range(scr_ref.shape[0]):
        m = scr_ref[g].T if transpose else scr_ref[g]
        for h in range(N_HEADS):
            lo, hi = h * HEAD_DIM, (h + 1) * HEAD_DIM
            out_ref[g, h] = m[lo:hi, lo:hi]


def _HEAD_BLOCKS_SPEC(G):
    return pl.BlockSpec((G, N_HEADS, HEAD_DIM, HEAD_DIM), lambda b, c: (b, 0, 0, 0))


def _inproj_kernel(x_ref, g_ref, w_ref, wg_ref, proj_ref, gt_ref, u_ref):
    hn = _rmsnorm_rows(x_ref[...], g_ref[...]).astype(BF16)
    proj = jnp.dot(hn, w_ref[...], preferred_element_type=F32)
    proj_ref[...] = proj
    u_ref[...] = proj[:, OFF_MAIN:OFF_MAIN + W_GROUP]
    gt_ref[0] = lax.dot_general(wg_ref[...], hn, (((1,), (1,)), ((), ())),
                                preferred_element_type=F32)


def _inproj(x2d, g, w, wg, n_seq, t_len, tm):
    n, d = x2d.shape
    tiles_per_seq = t_len // tm
    return pl.pallas_call(
        _inproj_kernel,
        grid=(n // tm,),
        in_specs=[
            pl.BlockSpec((tm, d), lambda i: (i, 0)),
            g.spec(), w.spec(pipeline_mode=pl.Buffered(1)), wg.spec(),
        ],
        out_specs=[
            pl.BlockSpec((tm, D_PROJ), lambda i: (i, 0)),
            pl.BlockSpec((1, 8, tm), lambda i: (i // tiles_per_seq, 0, i % tiles_per_seq)),
            pl.BlockSpec((tm, W_GROUP), lambda i: (i % tiles_per_seq, i // tiles_per_seq)),
        ],
        out_shape=[
            jax.ShapeDtypeStruct((n, D_PROJ), F32),
            jax.ShapeDtypeStruct((n_seq, 8, t_len), F32),
            jax.ShapeDtypeStruct((t_len, n_seq * W_GROUP), F32),
        ],
        compiler_params=_cparams("arbitrary"),
        name="inproj",
    )(x2d, g.arr, w.arr, wg.arr)


def _s5_kernel(u_ref, unext_ref, h0re_ref, h0im_ref, bmat_ref, lre_ref, lim_ref, cmat_ref, d_ref, wglu_ref,
               y_ref, hre_out, him_out, bu_scr, hs_scr, hre_scr, him_scr, *, bp, tt, wide):
    i = pl.program_id(0)
    nt = S5_WIDTH // LANES

    rows_of = (lambda ref: ref[...].reshape(tt, bp, W_GROUP).reshape(tt * bp, W_GROUP)) if wide else (
        lambda ref: ref[...])

    def project(u_tile, slot):
        bu = jnp.dot(u_tile.astype(BF16), bmat_ref[...], preferred_element_type=F32)
        for j in range(2 * nt):
            bu_scr[slot, j] = bu[:, j * LANES:(j + 1) * LANES]

    @pl.when(i == 0)
    def _():
        hre_scr[...] = h0re_ref[...]
        him_scr[...] = h0im_ref[...]
        project(rows_of(u_ref), 0)

    slot = lax.rem(i, 2)
    lre = [lre_ref[:, j * LANES:(j + 1) * LANES] for j in range(nt)]
    lim = [lim_ref[:, j * LANES:(j + 1) * LANES] for j in range(nt)]

    def step(t, carry):
        rows = pl.ds(pl.multiple_of(t * bp, bp), bp)
        new = []
        for j in range(nt):
            hre, him = carry[j], carry[nt + j]
            new.append((lre[j] * hre - lim[j] * him + bu_scr[slot, j, rows, :],
                        lre[j] * him + lim[j] * hre + bu_scr[slot, nt + j, rows, :]))
        out = tuple(x[0] for x in new) + tuple(x[1] for x in new)
        for j in range(2 * nt):
            hs_scr[j, rows, :] = out[j]
        return out

    carry = tuple(hre_scr[:, j * LANES:(j + 1) * LANES] for j in range(nt)) + tuple(
        him_scr[:, j * LANES:(j + 1) * LANES] for j in range(nt))
    carry = step(0, carry) if tt == 1 else lax.fori_loop(0, tt, step, carry)
    hre = jnp.concatenate(carry[0:nt], axis=1)
    him = jnp.concatenate(carry[nt:2 * nt], axis=1)
    hre_scr[...] = hre
    him_scr[...] = him
    hre_out[...] = hre
    him_out[...] = him

    project(rows_of(unext_ref), lax.rem(i + 1, 2))
    u = rows_of(u_ref)
    hs = jnp.concatenate([hs_scr[j] for j in range(2 * nt)], axis=1)
    y = jnp.dot(hs.astype(BF16), cmat_ref[...], preferred_element_type=F32) + d_ref[...] * u
    z = _gelu(y)
    y = z * _sigmoid(_dot(z, wglu_ref[...]))
    if wide:
        y = y.reshape(tt, bp, W_GROUP).reshape(tt, bp * W_GROUP)
    y_ref[...] = y.astype(y_ref.dtype)


def _s5(u, h0re, h0im, sp, bp, t_len, tt, wide):
    nsteps = t_len // tt
    full = lambda shape: pl.BlockSpec(shape, lambda i: (0,) * len(shape))
    blk = (tt, bp * W_GROUP) if wide else (tt * bp, W_GROUP)
    tile = pl.BlockSpec(blk, lambda i: (i, 0))
    next_tile = pl.BlockSpec(blk, lambda i: (jnp.minimum(i + 1, nsteps - 1), 0))
    return pl.pallas_call(
        functools.partial(_s5_kernel, bp=bp, tt=tt, wide=wide),
        grid=(nsteps,),
        in_specs=[
            tile, next_tile,
            full((bp, S5_WIDTH)), full((bp, S5_WIDTH)),
            sp["bmat"].spec(), sp["lre"].spec(), sp["lim"].spec(), sp["cmat"].spec(), sp["d"].spec(),
            sp["wglu"].spec(),
        ],
        out_specs=[
            tile,
            full((bp, S5_WIDTH)), full((bp, S5_WIDTH)),
        ],
        out_shape=[
            jax.ShapeDtypeStruct(u.shape, BF16),
            jax.ShapeDtypeStruct((bp, S5_WIDTH), F32),
            jax.ShapeDtypeStruct((bp, S5_WIDTH), F32),
        ],
        scratch_shapes=[
            pltpu.VMEM((2, 2 * S5_WIDTH // LANES, tt * bp, LANES), F32),
            pltpu.VMEM((2 * S5_WIDTH // LANES, tt * bp, LANES), F32),
            pltpu.VMEM((bp, S5_WIDTH), F32),
            pltpu.VMEM((bp, S5_WIDTH), F32),
        ],
        compiler_params=_cparams("arbitrary"),
        name="s5_scan",
    )(u, u, h0re, h0im, sp["bmat"].arr, sp["lre"].arr, sp["lim"].arr, sp["cmat"].arr, sp["d"].arr,
      sp["wglu"].arr)


def _run_interleaved(chains):
    live = list(chains)
    while live:
        still = []
        for ch in live:
            try:
                next(ch)
                still.append(ch)
            except StopIteration:
                pass
        live = still


def _mlstm_kernel(q_ref, k_ref, v_ref, og_ref, sm_ref, gt_ref, brow_ref, bcol_ref, norm_ref,
                  y_ref, c_out, n_out, m_out, c_scr, n_scr, m_scr, *, L, nc, G):
    c = pl.program_id(1)

    @pl.when(c == 0)
    def _():
        c_scr[...] = jnp.zeros_like(c_scr)
        n_scr[...] = jnp.zeros_like(n_scr)
        m_scr[...] = jnp.zeros_like(m_scr)

    ones = _head_ones()
    tril = _tril_ones(L)
    triu = _triu_ones(L)
    l_shift = L.bit_length() - 1
    src = _iota((LANES, N_HEADS * L), 0)
    to_scores = jnp.where(src == LANE_MI + (_iota((LANES, N_HEADS * L), 1) >> l_shift), 1.0, 0.0).astype(BF16)
    src = _iota((LANES, W_GROUP), 0)
    to_feats = jnp.where(src == LANE_MI + (_iota((LANES, W_GROUP), 1) >> HEAD_SHIFT), 1.0, 0.0).astype(BF16)
    seg = _iota((N_HEADS * L, W_GROUP), 0) >> l_shift
    score_sum = jnp.where(seg == (_iota((N_HEADS * L, W_GROUP), 1) >> HEAD_SHIFT), 1.0, 0.0).astype(BF16)
    causal = _iota((L, N_HEADS * L), 0) >= (_iota((L, N_HEADS * L), 1) & (L - 1))
    t_idx = _iota((L, LANES), 0)

    seqs = list(range(G))
    cat = lambda xs: jnp.concatenate(xs, axis=0)
    split = lambda x: [x[i * L:(i + 1) * L] for i in seqs]

    def running_max(x):
        shift = 1
        while shift < L:
            x = jnp.maximum(x, jnp.where(t_idx >= shift, pltpu.roll(x, shift, 0), NEG_BIG))
            shift *= 2
        return x

    q = [q_ref[i] for i in seqs]
    k = [k_ref[i] * (HEAD_DIM ** -0.5) for i in seqs]
    v = [v_ref[i] for i in seqs]
    C = [c_scr[i] for i in seqs]
    n = [n_scr[i, 0:1, :] for i in seqs]
    m0 = [m_scr[i, 0:1, :] for i in seqs]
    pre = [sm_ref[i] + brow_ref[...] for i in seqs]
    gtb = [gt_ref[i] + bcol_ref[...] for i in seqs]
    bc_col = _each(lambda x: _dot_exact_rhs(tril, _log_sigmoid(x)), pre)
    bc_rows = _dot_exact_lhs(_log_sigmoid(cat(gtb)), triu)
    qC = _each(_dot_nt, q, C)
    qk = _each(lambda q_, k_: _dot_nt(q_, _stack_heads(k_)), q, k)
    qn = _head_sum(cat(_each(lambda q_, n_: q_ * n_, q, n)), ones)

    b_col = _each(lambda x: pltpu.roll(x, LANES - N_HEADS, 1), bc_col)
    r_col = _each(lambda p_, b_: p_ - b_, pre, b_col)
    mu = _each(lambda r_, m_: jnp.maximum(running_max(r_), m_), r_col, m0)
    mu_last = _each(lambda x: x[L - 1:L, :], mu)
    mu_scores = split(_dot_exact_lhs(cat(mu), to_scores))

    def weights(i):
        r_rows = gtb[i][0:N_HEADS, :] - bc_rows[8 * i + N_HEADS:8 * i + 2 * N_HEADS, :]
        r_all = jnp.concatenate([r_rows[h:h + 1, :] for h in range(N_HEADS)], axis=1)
        return jnp.exp(jnp.where(causal, r_all - mu_scores[i], NEG_BIG)) * qk[i]

    ws = [weights(i) for i in seqs]
    wsv = _each(lambda w_, v_: _dot(w_, _stack_heads(v_)), ws, v)
    w0 = _dot(cat(_each(lambda m_, u_: jnp.exp(m_ - u_), m0, mu)), to_feats)
    m_row = _dot_exact_lhs(cat(_each(lambda b_, u_: b_ + u_, b_col, mu)), to_feats)
    den = w0 * qn + _dot(cat(ws), score_sum)
    hc = (w0 * cat(qC) + cat(wsv)) / jnp.maximum(jnp.abs(den), jnp.exp(-m_row))

    wl = split(_dot(cat(_each(lambda r_, u_: jnp.exp(r_ - u_), r_col, mu_last)), to_feats))
    w0l_rows = _each(lambda m_, u_: jnp.exp(m_ - u_), m0, mu_last)
    w0l_rows = w0l_rows + [jnp.zeros((1, LANES), F32)] * (-G % 8)
    w0l = _dot_exact_lhs(cat(w0l_rows), to_feats)
    kw = _each(lambda k_, w_: k_ * w_, k, wl)
    upd = _each(_dot_tn, v, kw)
    for i in seqs:
        c_scr[i] = w0l[i:i + 1, :] * C[i] + ones.astype(F32) * upd[i]
        n_scr[i, 0:1, :] = w0l[i:i + 1, :] * n[i] + jnp.sum(kw[i], axis=0, keepdims=True)
        m_scr[i, 0:1, :] = b_col[i][L - 1:L, :] + mu_last[i]

    ms = _head_sum(hc * hc, ones) * (1.0 / HEAD_DIM)
    y = split(hc * lax.rsqrt(ms + EPS) * norm_ref[...])
    for i in seqs:
        y_ref[i] = (y[i] * _sigmoid(og_ref[i])).astype(y_ref.dtype)

    @pl.when(c == nc - 1)
    def _():
        _store_head_blocks(c_out, c_scr)
        n_out[...] = jnp.broadcast_to(n_scr[:, 0:1, :], n_out.shape)
        m_out[...] = m_scr[...]


def _seq_spec(G, L, blk, width=W_GROUP):
    return pl.BlockSpec((G, L, width), lambda b, c: (b, c, blk))


def _mlstm(proj, gt, p, n_seq, t_len):
    L = CHUNK_ML
    G = SEQS_PER_STEP_ML
    nc = t_len // L
    per_seq = lambda rows, width: pl.BlockSpec((G, rows, width), lambda b, c: (b, 0, 0))
    proj3 = proj.reshape(n_seq, t_len, D_PROJ)
    y, c_new, n_new, m_new = pl.pallas_call(
        functools.partial(_mlstm_kernel, L=L, nc=nc, G=G),
        grid=(n_seq // G, nc),
        in_specs=[
            _seq_spec(G, L, BLK_MQ), _seq_spec(G, L, BLK_MK), _seq_spec(G, L, BLK_MV),
            _seq_spec(G, L, BLK_MO), _seq_spec(G, L, BLK_SMALL, SMALL_W),
            pl.BlockSpec((G, 8, L), lambda b, c: (b, 0, c)),
            p["ml_brow"].spec(), p["ml_bcol"].spec(), p["ml_norm"].spec(),
        ],
        out_specs=[
            pl.BlockSpec((G, L, W_GROUP), lambda b, c: (b, c, 0)),
            _HEAD_BLOCKS_SPEC(G), per_seq(8, W_GROUP), per_seq(8, LANES),
        ],
        out_shape=[
            jax.ShapeDtypeStruct((n_seq, t_len, W_GROUP), BF16),
            jax.ShapeDtypeStruct((n_seq, N_HEADS, HEAD_DIM, HEAD_DIM), F32),
            jax.ShapeDtypeStruct((n_seq, 8, W_GROUP), F32),
            jax.ShapeDtypeStruct((n_seq, 8, LANES), F32),
        ],
        scratch_shapes=[
            pltpu.VMEM((G, W_GROUP, W_GROUP), F32),
            pltpu.VMEM((G, 8, W_GROUP), F32),
            pltpu.VMEM((G, 8, LANES), F32),
        ],
        compiler_params=_cparams("arbitrary", "arbitrary"),
        name="mlstm_chunk",
    )(proj3, proj3, proj3, proj3, proj3, gt, p["ml_brow"].arr, p["ml_bcol"].arr, p["ml_norm"].arr)
    return y.reshape(n_seq * t_len, W_GROUP), c_new, n_new, m_new


def _gla_kernel(q_ref, k_ref, v_ref, gg_ref, sm_ref, wa_ref, ba_ref, norm_ref,
                y_ref, s_out, s_scr, *, L, nc, G):
    c = pl.program_id(1)

    @pl.when(c == 0)
    def _():
        s_scr[...] = jnp.zeros_like(s_scr)

    ones = _head_ones()
    tril = _tril_ones(L)
    causal = _iota((L, N_HEADS * L), 0) >= (_iota((L, N_HEADS * L), 1) & (L - 1))

    def chain(gi):
        q = q_ref[gi] * (HEAD_DIM ** -0.5)
        k = k_ref[gi]
        v = v_ref[gi]
        la = _log_sigmoid(_dot(sm_ref[gi], wa_ref[...]) + ba_ref[...]) * (1.0 / GLA_GATE_TEMP)
        yield
        bc = _dot_exact_rhs(tril, la)
        yield
        ref = bc[L // 2 - 1:L // 2, :]
        qr = q * jnp.exp(jnp.minimum(bc - ref, EXP_CLAMP))
        kh = k * jnp.exp(jnp.minimum(ref - bc, EXP_CLAMP))
        ST = s_scr[gi]
        att = jnp.where(causal, _dot_nt(qr, _stack_heads(kh)), 0.0)
        from_state = _dot_nt(q * jnp.exp(bc), ST)
        yield
        o = from_state + _dot(att, _stack_heads(v))
        b_last = bc[L - 1:L, :]
        kbar = k * jnp.exp(b_last - bc)
        s_scr[gi] = ST * jnp.exp(b_last) + ones.astype(F32) * _dot_tn(v, kbar)
        yield
        ms = _head_sum(o * o, ones) * (1.0 / HEAD_DIM)
        gg = gg_ref[gi]
        y = o * lax.rsqrt(ms + EPS) * norm_ref[...] * (gg * _sigmoid(gg))
        y_ref[gi] = y.astype(y_ref.dtype)

    _run_interleaved([chain(gi) for gi in range(G)])

    @pl.when(c == nc - 1)
    def _():
        _store_head_blocks(s_out, s_scr, transpose=True)


def _gla(proj, p, n_seq, t_len):
    L = CHUNK_ML
    G = SEQS_PER_STEP_ML
    nc = t_len // L
    proj3 = proj.reshape(n_seq, t_len, D_PROJ)
    y, s_new = pl.pallas_call(
        functools.partial(_gla_kernel, L=L, nc=nc, G=G),
        grid=(n_seq // G, nc),
        in_specs=[
            _seq_spec(G, L, BLK_GQ), _seq_spec(G, L, BLK_GK), _seq_spec(G, L, BLK_GV),
            _seq_spec(G, L, BLK_GG), _seq_spec(G, L, BLK_SMALL, SMALL_W),
            p["gla_wa"].spec(), p["gla_ba"].spec(), p["gla_norm"].spec(),
        ],
        out_specs=[
            pl.BlockSpec((G, L, W_GROUP), lambda b, c: (b, c, 0)),
            _HEAD_BLOCKS_SPEC(G),
        ],
        out_shape=[
            jax.ShapeDtypeStruct((n_seq, t_len, W_GROUP), BF16),
            jax.ShapeDtypeStruct((n_seq, N_HEADS, HEAD_DIM, HEAD_DIM), F32),
        ],
        scratch_shapes=[pltpu.VMEM((G, W_GROUP, W_GROUP), F32)],
        compiler_params=_cparams("arbitrary", "arbitrary"),
        name="gla_chunk",
    )(proj3, proj3, proj3, proj3, proj3, p["gla_wa"].arr, p["gla_ba"].arr, p["gla_norm"].arr)
    return y.reshape(n_seq * t_len, W_GROUP), s_new


def _rwkv_vectors(rc, prev, p_mu, p_w0, p_a0, p_kk, p_ka, p_rk, w2, a2, g2, ones):
    xm = rc + p_mu * (prev - rc)
    rr = xm[:, 0:W_GROUP]
    rk = xm[:, W_GROUP:2 * W_GROUP]
    rv = xm[:, 2 * W_GROUP:3 * W_GROUP]
    tail = xm[:, 3 * W_GROUP:RWKV_COLS]
    lw = -RWKV_DECAY_SCALE * _sigmoid(p_w0 + _dot(jnp.tanh(tail), w2))
    a = _sigmoid(p_a0 + _dot(tail, a2))
    g = _dot(_sigmoid(tail), g2)
    kk = rk * p_kk
    kk = kk * lax.rsqrt(jnp.maximum(_head_sum(kk * kk, ones), 1e-24))
    kt = rk * (1.0 + (a - 1.0) * p_ka)
    bonus = _head_sum(rr * kt * p_rk, ones) * rv
    return rr, lw, kt, rv, kk, a, g, bonus


def _head_layernorm(o, g, ones):
    mu = _head_sum(o, ones) * (1.0 / HEAD_DIM)
    oc = o - mu
    var = _head_sum(oc * oc, ones) * (1.0 / HEAD_DIM)
    return oc * lax.rsqrt(var + GN_EPS) * g


def _stack_heads(x):
    xb = x.astype(BF16)
    lane_head = _iota((1, W_GROUP), 1) >> HEAD_SHIFT
    return jnp.concatenate([jnp.where(lane_head == h, xb, jnp.zeros_like(xb)) for h in range(N_HEADS)],
                           axis=0)


def _each(f, *seqs):
    return [f(*args) for args in zip(*seqs)]


def _block_mm(x_list, y_list):
    return _each(lambda x, y: _dot(x, _stack_heads(y)), x_list, y_list)


def _unit_lower_inverse(a_list, L):
    t_idx = _iota((L, N_HEADS * L), 0)
    s_idx = _iota((L, N_HEADS * L), 1) & (L - 1)
    eye = jnp.where(t_idx == s_idx, 1.0, 0.0).astype(F32)
    in16 = (t_idx >> 4) == (s_idx >> 4)
    in32 = ((t_idx >> 5) == (s_idx >> 5)) & ((t_idx >> 4) > (s_idx >> 4))
    in64 = (t_idx >> 5) > (s_idx >> 5)
    x16 = _each(lambda a: jnp.where(in16, -a, 0.0), a_list)
    x2 = _block_mm(x16, x16)
    x4 = _block_mm(x2, x2)
    x8 = _block_mm(x4, x4)
    t = _each(lambda x: eye + x, x16)
    for xp in (x2, x4, x8):
        t = _each(lambda t_, d: t_ + d, t, _block_mm(t, xp))
    for sel in (in32, in64):
        w = _block_mm(_each(lambda a: jnp.where(sel, a, 0.0), a_list), t)
        t = _each(lambda t_, d: t_ - d, t, _block_mm(t, w))
    return t


def _rwkv_chunks(rcs, prev_rows, states, prm, ones, L):
    (mu, w0, a0, pkk, pka, prk, w2, a2, g2, norm) = prm
    first_row = _iota((L, RWKV_COLS), 0) == 0
    prevs = _each(lambda rc, pr: jnp.where(first_row, pr, pltpu.roll(rc, 1, 0)), rcs, prev_rows)
    n_seq = len(rcs)
    stacked = _rwkv_vectors(jnp.concatenate(rcs, axis=0), jnp.concatenate(prevs, axis=0),
                            mu, w0, a0, pkk, pka, prk, w2, a2, g2, ones)
    r, lw, kx, v, kk, a, g, bonus = [[x[i * L:(i + 1) * L] for i in range(n_seq)] for x in stacked]

    tril = _tril_ones(L)
    t_idx = _iota((L, N_HEADS * L), 0)
    s_idx = _iota((L, N_HEADS * L), 1) & (L - 1)
    strict = t_idx > s_idx
    incl = t_idx >= s_idx

    def core(r, lw, kx, v, kk, a, states):
        lc = _each(lambda x: _dot_exact_rhs(tril, x), lw)
        akk = _each(lambda a_, k_: a_ * k_, a, kk)
        e_neg = _each(lambda x: jnp.exp(-x), lc)
        l_last = _each(lambda x: x[L - 1:L, :], lc)
        e_end = _each(lambda ll, x: jnp.exp(ll - x), l_last, lc)
        kr = _each(lambda k_, r_, c_, w_: jnp.concatenate([k_ * jnp.exp(c_ - w_), r_ * jnp.exp(c_)], axis=0),
                   kk, r, lc, lw)
        a_hat = _each(lambda x, e: _stack_heads(x * e), akk, e_neg)
        k_hat = _each(lambda x, e: _stack_heads(x * e), kx, e_neg)
        v_bd = _each(_stack_heads, v)

        pa = _each(_dot_nt, kr, a_hat)
        pk = _each(_dot_nt, kr, k_hat)
        a_ua = _each(lambda p_: jnp.where(strict, p_[0:L], 0.0), pa)
        b_ra = _each(lambda p_: jnp.where(incl, p_[L:2 * L], 0.0), pa)
        ab_k = _each(lambda p_: jnp.concatenate([jnp.where(strict, p_[0:L], 0.0),
                                                 jnp.where(incl, p_[L:2 * L], 0.0)], axis=0), pk)
        T = _unit_lower_inverse(a_ua, L)

        from_state = _each(_dot_nt, kr, states)
        from_v = _each(_dot, ab_k, v_bd)
        rhs = _each(lambda s_, v_: s_[0:L] + v_[0:L], from_state, from_v)
        U = _block_mm(T, rhs)
        corr = _block_mm(b_ra, U)
        o = _each(lambda s_, v_, c_: s_[L:2 * L] + v_[L:2 * L] - c_, from_state, from_v, corr)
        upd = _each(lambda v_, u_, k_, a_, e: _dot_tn(jnp.concatenate([v_, u_], axis=0),
                                                      jnp.concatenate([k_ * e, -(a_ * e)], axis=0)),
                    v, U, kx, akk, e_end)
        s_new = _each(lambda s_, ll, d: s_ * jnp.exp(ll) + ones.astype(F32) * d, states, l_last, upd)
        return o, s_new

    o, s_new = core(r, lw, kx, v, kk, a, states)
    y_all = ((_head_layernorm(jnp.concatenate(o, axis=0), norm, ones) + stacked[7]) * stacked[6])
    return [y_all[i * L:(i + 1) * L] for i in range(n_seq)], s_new


def _rwkv_kernel(rc_ref, mu_ref, w0_ref, a0_ref, kk_ref, ka_ref, rk_ref, w2_ref, a2_ref, g2_ref,
                 norm_ref, y_ref, s_out, shift_out, s_scr, prev_scr, *, L, nc, G):
    c = pl.program_id(1)

    @pl.when(c == 0)
    def _():
        s_scr[...] = jnp.zeros_like(s_scr)
        prev_scr[...] = jnp.zeros_like(prev_scr)

    ones = _head_ones()
    prm = (mu_ref[...], w0_ref[...], a0_ref[...], kk_ref[...], ka_ref[...], rk_ref[...],
           w2_ref[...], a2_ref[...], g2_ref[...], norm_ref[...])
    rcs = [rc_ref[gi] for gi in range(G)]
    ys, s_new = _rwkv_chunks(rcs, [prev_scr[gi, 0:1, :] for gi in range(G)],
                             [s_scr[gi] for gi in range(G)], prm, ones, L)
    for gi in range(G):
        prev_scr[gi, 0:1, :] = rcs[gi][L - 1:L, :]
        s_scr[gi] = s_new[gi]
        y_ref[gi] = ys[gi].astype(y_ref.dtype)

    @pl.when(c == nc - 1)
    def _():
        _store_head_blocks(s_out, s_scr)
        shift_out[...] = jnp.broadcast_to(prev_scr[:, 0:1, :], shift_out.shape)


def _rwkv(proj, p, n_seq, t_len):
    L = CHUNK_RW
    G = SEQS_PER_STEP
    nc = t_len // L
    names = ("rw_mu", "rw_w0", "rw_a0", "rw_kk", "rw_ka", "rw_rk", "rw_w2", "rw_a2", "rw_g2", "rw_norm")
    y, s_new, shift = pl.pallas_call(
        functools.partial(_rwkv_kernel, L=L, nc=nc, G=G),
        grid=(n_seq // G, nc),
        in_specs=[
            pl.BlockSpec((G, L, RWKV_COLS), lambda b, c: (b, c, 0)),
        ] + [p[k].spec() for k in names],
        out_specs=[
            pl.BlockSpec((G, L, W_GROUP), lambda b, c: (b, c, 0)),
            _HEAD_BLOCKS_SPEC(G),
            pl.BlockSpec((G, 8, RWKV_COLS), lambda b, c: (b, 0, 0)),
        ],
        out_shape=[
            jax.ShapeDtypeStruct((n_seq, t_len, W_GROUP), BF16),
            jax.ShapeDtypeStruct((n_seq, N_HEADS, HEAD_DIM, HEAD_DIM), F32),
            jax.ShapeDtypeStruct((n_seq, 8, RWKV_COLS), F32),
        ],
        scratch_shapes=[
            pltpu.VMEM((G, W_GROUP, W_GROUP), F32),
            pltpu.VMEM((G, 8, RWKV_COLS), F32),
        ],
        compiler_params=_cparams("arbitrary", "arbitrary"),
        name="rwkv_chunk",
    )(proj.reshape(n_seq, t_len, D_PROJ), *[p[k].arr for k in names])
    return y.reshape(n_seq * t_len, W_GROUP), s_new, shift


FF_CHUNK = 256


def _mix_residual(x_ref, ys, wout_ref):
    acc = x_ref[...]
    for j, y in enumerate(ys):
        acc = acc + jnp.dot(y.astype(BF16), wout_ref[j * W_GROUP:(j + 1) * W_GROUP, :],
                            preferred_element_type=F32)
    return acc


def _ffn_body(x1, nrm_ref, wup_ref, cw_ref, cb_ref, wdn_ref, prev_rows, d_ff, act_scr):
    h2 = _rmsnorm_rows(x1, nrm_ref[...]).astype(BF16)
    for j in range(d_ff // FF_CHUNK):
        lo, hi = j * FF_CHUNK, (j + 1) * FF_CHUNK
        ug = jnp.dot(h2, wup_ref[:, lo:hi], preferred_element_type=F32)
        uv = jnp.dot(h2, wup_ref[:, d_ff + lo:d_ff + hi], preferred_element_type=F32)
        p2, p1 = prev_rows(j, ug)
        conv = (cb_ref[:, lo:hi] + cw_ref[0:1, lo:hi] * p2 + cw_ref[1:2, lo:hi] * p1
                + cw_ref[2:3, lo:hi] * ug)
        act_scr[:, lo:hi] = (_gelu(conv) * uv).astype(BF16)
    return x1 + jnp.dot(act_scr[...], wdn_ref[...], preferred_element_type=F32)


def _ffn_seq_kernel(x_ref, y0_ref, y1_ref, y2_ref, y3_ref, wout_ref, nrm_ref, wup_ref, cw_ref, cb_ref,
                    wdn_ref, nf_ref, o_ref, buf_out, carry_scr, act_scr, *, tm, d_ff, final):
    t = pl.program_id(1)

    @pl.when(t == 0)
    def _():
        carry_scr[...] = jnp.zeros_like(carry_scr)

    x1 = _mix_residual(x_ref, (y0_ref[...], y1_ref[...], y2_ref[...], y3_ref[...]), wout_ref)
    row = _iota((tm, FF_CHUNK), 0)

    def prev_rows(j, ug):
        lo, hi = j * FF_CHUNK, (j + 1) * FF_CHUNK
        c0 = carry_scr[0:1, lo:hi]
        c1 = carry_scr[1:2, lo:hi]
        p1 = jnp.where(row == 0, c1, pltpu.roll(ug, 1, 0))
        p2 = jnp.where(row == 0, c0, jnp.where(row == 1, c1, pltpu.roll(ug, 2, 0)))
        carry_scr[0:2, lo:hi] = ug[tm - 2:tm, :]
        return p2, p1

    out = _ffn_body(x1, nrm_ref, wup_ref, cw_ref, cb_ref, wdn_ref, prev_rows, d_ff, act_scr)
    if final:
        out = _rmsnorm_rows(out, nf_ref[...])
    o_ref[...] = out
    buf_out[0] = carry_scr[...]


def _ffn_step_kernel(x_ref, y0_ref, y1_ref, y2_ref, y3_ref, wout_ref, nrm_ref, wup_ref, cw_ref, cb_ref,
                     wdn_ref, nf_ref, p2_ref, p1_ref, o_ref, ug_out, act_scr, *, d_ff, final):
    x1 = _mix_residual(x_ref, (y0_ref[...], y1_ref[...], y2_ref[...], y3_ref[...]), wout_ref)

    def prev_rows(j, ug):
        lo, hi = j * FF_CHUNK, (j + 1) * FF_CHUNK
        ug_out[:, lo:hi] = ug
        return p2_ref[:, lo:hi], p1_ref[:, lo:hi]

    out = _ffn_body(x1, nrm_ref, wup_ref, cw_ref, cb_ref, wdn_ref, prev_rows, d_ff, act_scr)
    if final:
        out = _rmsnorm_rows(out, nf_ref[...])
    o_ref[...] = out


def _ffn_seq(x2d, ys, p, norm_final, n_seq, t_len, tm, final):
    n, d = x2d.shape
    d_ff = p["ffn_wdn"].shape[0]
    nt = t_len // tm
    rows = lambda w: pl.BlockSpec((tm, w), lambda b, t: (b * nt + t, 0))
    once = dict(pipeline_mode=pl.Buffered(1))
    return pl.pallas_call(
        functools.partial(_ffn_seq_kernel, tm=tm, d_ff=d_ff, final=final),
        grid=(n_seq, nt),
        in_specs=[rows(d), pl.BlockSpec((tm, W_GROUP), lambda b, t: (t, b))] + [rows(W_GROUP)] * 3 + [
            p["w_out"].spec(**once), p["norm_ffn"].spec(), p["ffn_wup"].spec(**once), p["ffn_cw"].spec(),
            p["ffn_cb"].spec(), p["ffn_wdn"].spec(**once), norm_final.spec()],
        out_specs=[rows(d), pl.BlockSpec((1, CONV_W - 1, d_ff), lambda b, t: (b, 0, 0))],
        out_shape=[jax.ShapeDtypeStruct((n, d), F32),
                   jax.ShapeDtypeStruct((n_seq, CONV_W - 1, d_ff), F32)],
        scratch_shapes=[pltpu.VMEM((CONV_W - 1, d_ff), F32), pltpu.VMEM((tm, d_ff), BF16)],
        compiler_params=_cparams("arbitrary", "arbitrary"),
        name="wout_ffn_seq",
    )(x2d, *ys, p["w_out"].arr, p["norm_ffn"].arr, p["ffn_wup"].arr, p["ffn_cw"].arr, p["ffn_cb"].arr,
      p["ffn_wdn"].arr, norm_final.arr)


def _ffn_step(x2d, ys, p, norm_final, prev2, prev1, final):
    n, d = x2d.shape
    d_ff = p["ffn_wdn"].shape[0]
    full = lambda shape: pl.BlockSpec(shape, lambda i: (0,) * len(shape))
    return pl.pallas_call(
        functools.partial(_ffn_step_kernel, d_ff=d_ff, final=final),
        grid=(1,),
        in_specs=[full((n, d))] + [full((n, W_GROUP))] * 4 + [
            p["w_out"].spec(), p["norm_ffn"].spec(), p["ffn_wup"].spec(), p["ffn_cw"].spec(),
            p["ffn_cb"].spec(), p["ffn_wdn"].spec(), norm_final.spec(), full((n, d_ff)), full((n, d_ff))],
        out_specs=[full((n, d)), full((n, d_ff))],
        out_shape=[jax.ShapeDtypeStruct((n, d), F32), jax.ShapeDtypeStruct((n, d_ff), F32)],
        scratch_shapes=[pltpu.VMEM((n, d_ff), BF16)],
        compiler_params=_cparams("arbitrary"),
        name="wout_ffn_step",
    )(x2d, *ys, p["w_out"].arr, p["norm_ffn"].arr, p["ffn_wup"].arr, p["ffn_cw"].arr, p["ffn_cb"].arr,
      p["ffn_wdn"].arr, norm_final.arr, prev2, prev1)


def _step_vec_kernel(proj_ref, shift_ref, m_ref, brow_ref, wa_ref, ba_ref, mu_ref, w0_ref, a0_ref,
                     kk_ref, ka_ref, rk_ref, w2_ref, a2_ref, g2_ref,
                     pack_out, m_out, g_out, bonus_out):
    ones = _head_ones()
    sm = proj_ref[:, OFF_SMALL:OFF_SMALL + SMALL_W]
    pre = sm + brow_ref[...]
    logf = pltpu.roll(_log_sigmoid(pre), LANES - N_HEADS, 1)
    m_old = m_ref[...]
    m_new = jnp.maximum(logf + m_old, pre)
    m_out[...] = m_new
    lane = _iota((SMALL_W, W_GROUP), 0)
    spread_mat = jnp.where(lane == LANE_MI + (_iota((SMALL_W, W_GROUP), 1) >> HEAD_SHIFT), 1.0, 0.0).astype(BF16)
    gate_lanes = (_iota(pre.shape, 1) >= LANE_MI) & (_iota(pre.shape, 1) < LANE_MI + N_HEADS)
    spread = lambda x: _dot_exact_lhs(jnp.where(gate_lanes, x, 0.0), spread_mat)
    la = _log_sigmoid(_dot(sm, wa_ref[...]) + ba_ref[...]) * (1.0 / GLA_GATE_TEMP)
    rc = proj_ref[:, 0:RWKV_COLS]
    rr, lw, kt, rv, kk, a, g, bonus = _rwkv_vectors(
        rc, shift_ref[...], mu_ref[...], w0_ref[...], a0_ref[...], kk_ref[...], ka_ref[...],
        rk_ref[...], w2_ref[...], a2_ref[...], g2_ref[...], ones)
    main = lambda blk: proj_ref[:, blk * W_GROUP:(blk + 1) * W_GROUP]
    rows = {
        ROW_MQ: main(BLK_MQ), ROW_MK: main(BLK_MK), ROW_MV: main(BLK_MV),
        ROW_GQ: main(BLK_GQ), ROW_GK: main(BLK_GK), ROW_GV: main(BLK_GV), ROW_GA: jnp.exp(la),
        ROW_KK: kk, ROW_AKK: a * kk, ROW_W: jnp.exp(lw), ROW_KT: kt, ROW_RR: rr, ROW_RV: rv,
        ROW_IW: spread(jnp.exp(pre - m_new)), ROW_FW: spread(jnp.exp(logf + m_old - m_new)),
        ROW_M: spread(m_new),
    }
    for i, val in rows.items():
        pack_out[i] = val.T
    g_out[...] = g
    bonus_out[...] = bonus


def _step_vectors(proj, shift, m_pad, p):
    n = proj.shape[0]
    full = lambda a: pl.BlockSpec(a.shape, lambda i: (0,) * a.ndim)
    data = (proj, shift, m_pad)
    prms = [p[k] for k in ("ml_brow", "gla_wa", "gla_ba", "rw_mu", "rw_w0", "rw_a0", "rw_kk", "rw_ka",
                           "rw_rk", "rw_w2", "rw_a2", "rw_g2")]
    wide = jax.ShapeDtypeStruct((n, W_GROUP), F32)
    outs = [jax.ShapeDtypeStruct((N_PACK, W_GROUP, n), F32), jax.ShapeDtypeStruct((n, SMALL_W), F32),
            wide, wide]
    return pl.pallas_call(
        _step_vec_kernel,
        grid=(1,),
        in_specs=[full(a) for a in data] + [q.spec() for q in prms],
        out_specs=[pl.BlockSpec(o.shape, lambda i, nd=len(o.shape): (0,) * nd) for o in outs],
        out_shape=outs,
        compiler_params=_cparams("arbitrary"),
        name="step_vectors",
    )(*data, *[q.arr for q in prms])


(ROW_MQ, ROW_MK, ROW_MV, ROW_GQ, ROW_GK, ROW_GV, ROW_GA, ROW_KK, ROW_AKK, ROW_W, ROW_KT, ROW_RR, ROW_RV,
 ROW_IW, ROW_FW, ROW_M) = range(16)
N_PACK = 16
(OROW_ML, OROW_GLA, OROW_RW, OROW_N) = range(4)
N_OPACK = 4
STEP_UNROLL = 4


def _heads_step_kernel(pack_ref, c_ref, n_ref, gs_ref, rs_ref, c_out, gs_out, rs_out, opack_out):
    vec = lambda i: pack_ref[i]
    one = lambda i, j: pack_ref[i, pl.ds(j, 1), :]
    colsum = lambda x: jnp.sum(x, axis=0, keepdims=True)

    k_ml = vec(ROW_MK) * (HEAD_DIM ** -0.5)
    q_ml = vec(ROW_MQ)
    iw = pack_ref[ROW_IW, 0:1, :]
    fw = pack_ref[ROW_FW, 0:1, :]
    n_new = fw * n_ref[...] + iw * k_ml
    opack_out[OROW_N] = n_new
    r_den = 1.0 / jnp.maximum(jnp.abs(colsum(n_new * q_ml)), jnp.exp(-pack_ref[ROW_M, 0:1, :]))

    def ml_body(v, carry):
        c_new = fw * c_ref[v] + (iw * one(ROW_MV, v)) * k_ml
        c_out[v] = c_new
        opack_out[OROW_ML, pl.ds(v, 1), :] = colsum(c_new * q_ml) * r_den
        return carry

    lax.fori_loop(0, HEAD_DIM, ml_body, 0, unroll=STEP_UNROLL)

    v_gla = vec(ROW_GV)

    def gla_body(k, acc):
        s_new = one(ROW_GA, k) * gs_ref[k] + one(ROW_GK, k) * v_gla
        gs_out[k] = s_new
        return acc + (one(ROW_GQ, k) * (HEAD_DIM ** -0.5)) * s_new

    opack_out[OROW_GLA] = lax.fori_loop(0, HEAD_DIM, gla_body, jnp.zeros_like(v_gla), unroll=STEP_UNROLL)

    kk, akk, w, kt, rr = vec(ROW_KK), vec(ROW_AKK), vec(ROW_W), vec(ROW_KT), vec(ROW_RR)

    def rw_body(v, carry):
        s = rs_ref[v]
        s_new = s * w - colsum(s * kk) * akk + one(ROW_RV, v) * kt
        rs_out[v] = s_new
        opack_out[OROW_RW, pl.ds(v, 1), :] = colsum(s_new * rr)
        return carry

    lax.fori_loop(0, HEAD_DIM, rw_body, 0, unroll=STEP_UNROLL)


def _heads_step(pack, ml_c, ml_n, gla_s, rw_s, layer):
    n = pack.shape[-1]
    sq = (HEAD_DIM, HEAD_DIM, n)
    head = lambda lead: pl.BlockSpec((lead, HEAD_DIM, n), lambda h: (0, h, 0))
    at_layer = lambda shape: pl.BlockSpec((None,) + shape, lambda h: (layer * N_HEADS + h,) + (0,) * len(shape))
    out_state = pl.BlockSpec((None,) + sq, lambda h: (h, 0, 0, 0))
    state = jax.ShapeDtypeStruct((N_HEADS,) + sq, F32)
    return pl.pallas_call(
        _heads_step_kernel,
        grid=(N_HEADS,),
        in_specs=[head(N_PACK), at_layer(sq), at_layer((HEAD_DIM, n)), at_layer(sq), at_layer(sq)],
        out_specs=[out_state, out_state, out_state, head(N_OPACK)],
        out_shape=[state, state, state, jax.ShapeDtypeStruct((N_OPACK, W_GROUP, n), F32)],
        compiler_params=_cparams("arbitrary"),
        name="heads_step",
    )(pack, ml_c, ml_n, gla_s, rw_s)


def _step_post_kernel(hml_ref, mo_ref, ogla_ref, gg_ref, orw_ref, g_ref, bonus_ref,
                      mln_ref, gln_ref, rwn_ref, yml_out, ygla_out, yrw_out):
    ones = _head_ones()
    hml = hml_ref[...].T
    ms = _head_sum(hml * hml, ones) * (1.0 / HEAD_DIM)
    yml_out[...] = (hml * lax.rsqrt(ms + EPS) * mln_ref[...] * _sigmoid(mo_ref[...])).astype(BF16)
    og = ogla_ref[...].T
    ms = _head_sum(og * og, ones) * (1.0 / HEAD_DIM)
    gg = gg_ref[...]
    ygla_out[...] = (og * lax.rsqrt(ms + EPS) * gln_ref[...] * (gg * _sigmoid(gg))).astype(BF16)
    yrw_out[...] = ((_head_layernorm(orw_ref[...].T, rwn_ref[...], ones) + bonus_ref[...])
                    * g_ref[...]).astype(BF16)


def _step_post(hml, mo, ogla, gg, orw, g, bonus, p):
    n = mo.shape[0]
    data = (hml, mo, ogla, gg, orw, g, bonus)
    prms = [p[k] for k in ("ml_norm", "gla_norm", "rw_norm")]
    full = lambda a: pl.BlockSpec(a.shape, lambda i: (0,) * a.ndim)
    out = jax.ShapeDtypeStruct((n, W_GROUP), BF16)
    return pl.pallas_call(
        _step_post_kernel,
        grid=(1,),
        in_specs=[full(a) for a in data] + [q.spec() for q in prms],
        out_specs=[pl.BlockSpec((n, W_GROUP), lambda i: (0, 0))] * 3,
        out_shape=[out] * 3,
        compiler_params=_cparams("arbitrary"),
        name="step_post",
    )(*data, *[q.arr for q in prms])


def _prepare_params(prm):
    P = {}
    w_in = prm["w_in"].astype(BF16)
    depth, d = w_in.shape[0], w_in.shape[1]
    sizes = (W_GROUP, W_GROUP, W_GROUP, W_GROUP, N_HEADS, N_HEADS, W_GROUP, W_GROUP, W_GROUP, W_GROUP,
             GLA_RANK, W_GROUP, RWKV_COLS)
    offs = [0]
    for sz in sizes:
        offs.append(offs[-1] + sz)
    col = lambda i: w_in[:, :, offs[i]:offs[i + 1]]
    (u, mq, mk, mv, mi, mf, mo, gq, gk, gv, ga, gg, rcols) = [col(i) for i in range(13)]
    small = jnp.concatenate([ga, mi, mf, jnp.zeros((depth, d, SMALL_W - GLA_RANK - 2 * N_HEADS), BF16)], axis=2)
    P["w_in"] = jnp.concatenate([rcols, small, u, mq, mk, mv, mo, gq, gk, gv, gg], axis=2)
    P["w_gate_t"] = jnp.swapaxes(jnp.concatenate([mi, mf], axis=2), 1, 2)

    lam = lax.complex(prm["s5_lam_re"], prm["s5_lam_im"])
    dt = jnp.exp(prm["s5_log_dt"])[..., None]
    lam_bar = jnp.exp(lam * dt)
    b_bar = ((lam_bar - 1.0) / lam)[..., None] * lax.complex(prm["s5_b_re"], prm["s5_b_im"])
    eye = jnp.eye(S5_GROUPS, dtype=F32)
    bm = lambda b: jnp.einsum("lgph,gk->lghkp", b, eye).reshape(depth, W_GROUP, S5_WIDTH)
    P["bmat"] = jnp.concatenate([bm(b_bar.real), bm(b_bar.imag)], axis=2).astype(BF16)
    cm = lambda c: jnp.einsum("lghp,gk->lkpgh", c, eye).reshape(depth, S5_WIDTH, W_GROUP)
    P["cmat"] = jnp.concatenate([cm(prm["s5_c_re"]), -cm(prm["s5_c_im"])], axis=1).astype(BF16)
    P["lam"] = jnp.stack([lam_bar.real.reshape(depth, S5_WIDTH), lam_bar.imag.reshape(depth, S5_WIDTH)],
                         axis=1)[:, :, None, :]
    P["wglu"] = prm["s5_w_glu"].astype(BF16)

    bias = prm["ml_gate_bias"]
    P["ml_brow"] = jnp.pad(bias, ((0, 0), (LANE_MI, SMALL_W - LANE_MI - 2 * N_HEADS)))[:, None, :]
    P["ml_bcol"] = jnp.broadcast_to(bias[:, :, None], (depth, 2 * N_HEADS, CHUNK_ML))

    P["rows"] = jnp.stack([prm["s5_d"], prm["ml_norm"], prm["gla_b_alpha"], prm["gla_norm"], prm["rw_w0"],
                           prm["rw_a0"], prm["rw_k_k"], prm["rw_k_a"], prm["rw_r_k"].reshape(depth, W_GROUP),
                           prm["rw_norm"]], axis=1)[:, :, None, :]
    pad_rows = lambda w, lo: jnp.pad(w, ((0, 0), (lo, LANES - lo - w.shape[1]), (0, 0)))
    P["pads"] = jnp.stack([pad_rows(prm["gla_w_alpha"], 0), pad_rows(prm["rw_w2"], 0),
                           pad_rows(prm["rw_a2"], RWKV_W_RANK),
                           pad_rows(prm["rw_g2"], RWKV_W_RANK + RWKV_A_RANK)], axis=1).astype(BF16)
    P["rw_mu"] = prm["rw_mu"][:, None, :]
    P["norms"] = jnp.stack([prm["norm_mix"], prm["norm_ffn"]], axis=1)[:, :, None, :]
    P["w_out"] = prm["w_out"].astype(BF16)
    P["ffn_wup"] = prm["ffn_w_up"].astype(BF16)
    P["ffn_cw"] = prm["ffn_conv_w"]
    P["ffn_cb"] = prm["ffn_conv_b"][:, None, :]
    P["ffn_wdn"] = prm["ffn_w_down"].astype(BF16)
    return P


ROW_NAMES = ("d", "ml_norm", "gla_ba", "gla_norm", "rw_w0", "rw_a0", "rw_kk", "rw_ka", "rw_rk", "rw_norm")
PAD_NAMES = ("gla_wa", "rw_w2", "rw_a2", "rw_g2")


def _layer_params(P, l):
    p = {k: _Param(P[k], l) for k in ("w_in", "w_gate_t", "bmat", "cmat", "wglu", "ml_brow", "ml_bcol",
                                      "rw_mu", "w_out", "ffn_wup", "ffn_cw", "ffn_cb", "ffn_wdn")}
    p.update({k: _Param(P["rows"], l, i) for i, k in enumerate(ROW_NAMES)})
    p.update({k: _Param(P["pads"], l, i) for i, k in enumerate(PAD_NAMES)})
    p["lre"], p["lim"] = _Param(P["lam"], l, 0), _Param(P["lam"], l, 1)
    p["norm_mix"], p["norm_ffn"] = _Param(P["norms"], l, 0), _Param(P["norms"], l, 1)
    return p


def _prompt_layer(x2d, p, norm_final, n_seq, t_len, final):
    proj, gt, u = _inproj(x2d, p["norm_mix"], p["w_in"], p["w_gate_t"], n_seq, t_len, tm=INPROJ_ROWS)

    zeros = jnp.zeros((n_seq, S5_WIDTH), F32)
    y_s5, s5_re, s5_im = _s5(u, zeros, zeros, p, n_seq, t_len, tt=S5_TILE_STEPS, wide=True)

    y_ml, ml_c, ml_n, ml_m = _mlstm(proj, gt, p, n_seq, t_len)
    y_gla, gla_st = _gla(proj, p, n_seq, t_len)
    y_rw, rw_s, rw_shift = _rwkv(proj, p, n_seq, t_len)

    x_new, ffn_buf = _ffn_seq(x2d, (y_s5, y_ml, y_gla, y_rw), p, norm_final, n_seq, t_len, tm=FFN_ROWS,
                              final=final)
    states = (
        s5_re.reshape(n_seq, S5_GROUPS, S5_STATE),
        s5_im.reshape(n_seq, S5_GROUPS, S5_STATE),
        ml_c,
        ml_n[:, 0, :].reshape(n_seq, N_HEADS, HEAD_DIM),
        ml_m[:, 0, LANE_MI:LANE_MI + N_HEADS],
        gla_st,
        rw_s,
        rw_shift[:, 0, :],
        ffn_buf,
    )
    return x_new, states


def _sample_layer(x2d, st, pair_states, layer, p, norm_final, final):
    (s5_re, s5_im, ml_m, rw_shift, ffn_buf) = st
    n = x2d.shape[0]
    npairs = n * N_HEADS
    proj, _, u = _inproj(x2d, p["norm_mix"], p["w_in"], p["w_gate_t"], 1, n, tm=n)

    y_s5, s5_re_new, s5_im_new = _s5(u,
                                     s5_re.reshape(n, S5_WIDTH), s5_im.reshape(n, S5_WIDTH),
                                     p, n, 1, tt=1, wide=False)

    m_pad = jnp.zeros((n, SMALL_W), F32).at[:, LANE_MI:LANE_MI + N_HEADS].set(ml_m)
    pack, m_new, g, bonus = _step_vectors(proj, rw_shift, m_pad, p)
    ml_c_new, gla_s_new, rw_s_new, opack = _heads_step(pack, *pair_states, layer)

    main = lambda blk: proj[:, blk * W_GROUP:(blk + 1) * W_GROUP]
    y_ml, y_gla, y_rw = _step_post(opack[OROW_ML], main(BLK_MO), opack[OROW_GLA], main(BLK_GG),
                                   opack[OROW_RW], g, bonus, p)
    x_new, ug = _ffn_step(x2d, (y_s5, y_ml, y_gla, y_rw), p, norm_final, ffn_buf[:, 0, :], ffn_buf[:, 1, :],
                          final)
    states = (
        s5_re_new.reshape(n, S5_GROUPS, S5_STATE),
        s5_im_new.reshape(n, S5_GROUPS, S5_STATE),
        ml_c_new.transpose(3, 0, 1, 2),
        opack[OROW_N].reshape(N_HEADS, HEAD_DIM, n).transpose(2, 0, 1),
        m_new[:, LANE_MI:LANE_MI + N_HEADS],
        gla_s_new.transpose(3, 0, 1, 2),
        rw_s_new.transpose(3, 0, 1, 2),
        proj[:, 0:RWKV_COLS],
        jnp.stack([ffn_buf[:, 1, :], ug], axis=1),
    )
    return x_new, states


def kernel(x_prompt, x_sample, state_s5_re, state_s5_im, state_mlstm_C, state_mlstm_n, state_mlstm_m, state_gla_S, state_rwkv_S, state_rwkv_shift, state_ffn_conv, norm_mix, w_in, s5_lam_re, s5_lam_im, s5_log_dt, s5_b_re, s5_b_im, s5_c_re, s5_c_im, s5_d, s5_w_glu, ml_gate_bias, ml_norm, gla_w_alpha, gla_b_alpha, gla_norm, rw_mu, rw_w0, rw_w2, rw_a0, rw_a2, rw_g2, rw_k_k, rw_k_a, rw_r_k, rw_norm, w_out, norm_ffn, ffn_w_up, ffn_conv_w, ffn_conv_b, ffn_w_down, norm_final):
    prm = dict(norm_mix=norm_mix, w_in=w_in, s5_lam_re=s5_lam_re, s5_lam_im=s5_lam_im, s5_log_dt=s5_log_dt,
               s5_b_re=s5_b_re, s5_b_im=s5_b_im, s5_c_re=s5_c_re, s5_c_im=s5_c_im, s5_d=s5_d,
               s5_w_glu=s5_w_glu, ml_gate_bias=ml_gate_bias, ml_norm=ml_norm, gla_w_alpha=gla_w_alpha,
               gla_b_alpha=gla_b_alpha, gla_norm=gla_norm, rw_mu=rw_mu, rw_w0=rw_w0, rw_w2=rw_w2,
               rw_a0=rw_a0, rw_a2=rw_a2, rw_g2=rw_g2, rw_k_k=rw_k_k, rw_k_a=rw_k_a, rw_r_k=rw_r_k,
               rw_norm=rw_norm, w_out=w_out, norm_ffn=norm_ffn, ffn_w_up=ffn_w_up, ffn_conv_w=ffn_conv_w,
               ffn_conv_b=ffn_conv_b, ffn_w_down=ffn_w_down)
    depth = w_in.shape[0]
    n_seq, t_len, d = x_prompt.shape
    n_smp = x_sample.shape[0]
    assert t_len % max(INPROJ_ROWS, FFN_ROWS, S5_TILE_STEPS, CHUNK_ML) == 0
    assert n_seq % max(SEQS_PER_STEP, SEQS_PER_STEP_ML) == 0 and x_sample.shape[1] == 1
    nf = _Param(norm_final[None, :])
    small_states = (state_s5_re, state_s5_im, state_mlstm_m, state_rwkv_shift, state_ffn_conv)
    lead = depth * N_HEADS
    to_lanes = lambda st: jnp.moveaxis(st, 1, -1)
    pair_states = (to_lanes(state_mlstm_C).reshape(lead, HEAD_DIM, HEAD_DIM, n_smp),
                   to_lanes(state_mlstm_n).reshape(lead, HEAD_DIM, n_smp),
                   to_lanes(state_gla_S).reshape(lead, HEAD_DIM, HEAD_DIM, n_smp),
                   to_lanes(state_rwkv_S).reshape(lead, HEAD_DIM, HEAD_DIM, n_smp))

    P = _prepare_params(prm)
    xp = x_prompt.reshape(n_seq * t_len, d)
    xs = x_sample.reshape(n_smp, d)
    p_states, s_states = [], []
    for l in range(depth):
        p = _layer_params(P, l)
        final = l == depth - 1
        xp, ps = _prompt_layer(xp, p, nf, n_seq, t_len, final)
        xs, ss = _sample_layer(xs, tuple(s[l] for s in small_states), pair_states, l, p, nf, final)
        p_states.append(ps)
        s_states.append(ss)
    new_p = tuple(jnp.stack([st[i] for st in p_states]) for i in range(9))
    new_s = tuple(jnp.stack([st[i] for st in s_states]) for i in range(9))
    return (xp.reshape(n_seq, t_len, d), xs.reshape(n_smp, 1, d)) + new_p + new_s
```

```python
import functools
import math

import jax
import jax.numpy as jnp
from jax import lax
from jax.experimental import pallas as pl
from jax.experimental.pallas import tpu as pltpu

F32 = jnp.float32
BF16 = jnp.bfloat16

LANES = 128
SUBLANES = 8
W_GROUP = 256
HEAD_DIM = 64
HEAD_SHIFT = 6
N_HEADS = 4
S5_GROUPS = 16
S5_STATE = 64
S5_WIDTH = S5_GROUPS * S5_STATE
GLA_RANK = 16
GLA_GATE_TEMP = 16.0
RWKV_W_RANK = 32
RWKV_A_RANK = 32
RWKV_G_RANK = 64
RWKV_COLS = 3 * W_GROUP + RWKV_W_RANK + RWKV_A_RANK + RWKV_G_RANK
RWKV_DECAY_SCALE = 0.6065306597126334
CONV_W = 3
EPS = 1e-6
GN_EPS = 64e-5
NEG_BIG = -1e30
EXP_CLAMP = 80.0

SMALL_W = LANES
OFF_SMALL = RWKV_COLS
OFF_MAIN = RWKV_COLS + SMALL_W
D_PROJ = OFF_MAIN + 9 * W_GROUP
LANE_MI = GLA_RANK
LANE_MF = GLA_RANK + N_HEADS
GATE_ROWS = 2 * N_HEADS
(BLK_U, BLK_MQ, BLK_MK, BLK_MV, BLK_MO, BLK_GQ, BLK_GK, BLK_GV, BLK_GG) = range(
    OFF_MAIN // W_GROUP, OFF_MAIN // W_GROUP + 9)
BLK_SMALL = OFF_SMALL // SMALL_W

CHUNK_ML = 128
CHUNK_RW = 64
SEQS_PER_STEP = 8
SEQS_PER_STEP_ML = 8
INPROJ_ROWS = 1024
FFN_ROWS = 1024
S5_TILE_STEPS = 128
VMEM_LIMIT = 56 * 1024 * 1024


def _cparams(*sem):
    return pltpu.CompilerParams(dimension_semantics=sem, vmem_limit_bytes=VMEM_LIMIT)


class _Param:
    def __init__(self, arr, *lead):
        self.arr, self.lead = arr, tuple(lead)

    @property
    def shape(self):
        return tuple(self.arr.shape[len(self.lead):])

    def spec(self, **kw):
        lead, tail = self.lead, (0,) * len(self.shape)
        return pl.BlockSpec((None,) * len(lead) + self.shape, lambda *grid_idx: lead + tail, **kw)


def _dot(a, b):
    return jnp.dot(a.astype(BF16), b.astype(BF16), preferred_element_type=F32)


def _dot_nt(a, b):
    return lax.dot_general(a.astype(BF16), b.astype(BF16), (((1,), (1,)), ((), ())),
                           preferred_element_type=F32)


def _dot_tn(a, b):
    return lax.dot_general(a.astype(BF16), b.astype(BF16), (((0,), (0,)), ((), ())),
                           preferred_element_type=F32)


def _split2(x):
    hi = x.astype(BF16)
    lo = (x - hi.astype(F32)).astype(BF16)
    return hi, lo


def _dot_exact_rhs(a01, x):
    hi, lo = _split2(x)
    f = lambda p: jnp.dot(a01, p, preferred_element_type=F32)
    return f(hi) + f(lo)


def _dot_exact_lhs(x, b01):
    hi, lo = _split2(x)
    f = lambda p: jnp.dot(p, b01, preferred_element_type=F32)
    return f(hi) + f(lo)


def _sigmoid(x):
    return 1.0 / (1.0 + jnp.exp(-x))


def _log_sigmoid(x):
    return jnp.minimum(x, 0.0) - jnp.log(1.0 + jnp.exp(-jnp.abs(x)))


def _gelu(x):
    return 0.5 * x * (1.0 + jnp.tanh(math.sqrt(2.0 / math.pi) * (x + 0.044715 * (x * x * x))))


def _iota(shape, dim):
    return lax.broadcasted_iota(jnp.int32, shape, dim)


def _head_ones():
    r = _iota((W_GROUP, W_GROUP), 0) >> HEAD_SHIFT
    c = _iota((W_GROUP, W_GROUP), 1) >> HEAD_SHIFT
    return jnp.where(r == c, 1.0, 0.0).astype(BF16)


def _head_sum(x, ones):
    return _dot_exact_lhs(x, ones)


def _rmsnorm_rows(x, g):
    ms = jnp.mean(x * x, axis=-1, keepdims=True)
    return x * lax.rsqrt(ms + EPS) * g


def _tril_ones(n, strict=False):
    r = _iota((n, n), 0)
    c = _iota((n, n), 1)
    return jnp.where((r > c) if strict else (r >= c), 1.0, 0.0).astype(BF16)


def _triu_ones(n):
    r = _iota((n, n), 0)
    c = _iota((n, n), 1)
    return jnp.where(r <= c, 1.0, 0.0).astype(BF16)


def _store_head_blocks(out_ref, scr_ref, transpose=False):
    for g in range(scr_ref.shape[0]):
        m = scr_ref[g].T if transpose else scr_ref[g]
        for h in range(N_HEADS):
            lo, hi = h * HEAD_DIM, (h + 1) * HEAD_DIM
            out_ref[g, h] = m[lo:hi, lo:hi]


def _HEAD_BLOCKS_SPEC(G):
    return pl.BlockSpec((G, N_HEADS, HEAD_DIM, HEAD_DIM), lambda b, c: (b, 0, 0, 0))


def _inproj_kernel(x_ref, g_ref, w_ref, wg_ref, proj_ref, gt_ref, u_ref):
    hn = _rmsnorm_rows(x_ref[...], g_ref[...]).astype(BF16)
    proj = jnp.dot(hn, w_ref[...], preferred_element_type=F32)
    proj_ref[...] = proj
    u_ref[...] = proj[:, OFF_MAIN:OFF_MAIN + W_GROUP]
    gt_ref[0] = lax.dot_general(wg_ref[...], hn, (((1,), (1,)), ((), ())),
                                preferred_element_type=F32)


def _inproj(x2d, g, w, wg, n_seq, t_len, tm):
    n, d = x2d.shape
    tiles_per_seq = t_len // tm
    return pl.pallas_call(
        _inproj_kernel,
        grid=(n // tm,),
        in_specs=[
            pl.BlockSpec((tm, d), lambda i: (i, 0)),
            g.spec(), w.spec(pipeline_mode=pl.Buffered(1)), wg.spec(),
        ],
        out_specs=[
            pl.BlockSpec((tm, D_PROJ), lambda i: (i, 0)),
            pl.BlockSpec((1, GATE_ROWS, tm), lambda i: (i // tiles_per_seq, 0, i % tiles_per_seq)),
            pl.BlockSpec((tm, W_GROUP), lambda i: (i % tiles_per_seq, i // tiles_per_seq)),
        ],
        out_shape=[
            jax.ShapeDtypeStruct((n, D_PROJ), F32),
            jax.ShapeDtypeStruct((n_seq, GATE_ROWS, t_len), F32),
            jax.ShapeDtypeStruct((t_len, n_seq * W_GROUP), F32),
        ],
        compiler_params=_cparams("arbitrary"),
        name="inproj",
    )(x2d, g.arr, w.arr, wg.arr)


def _s5_kernel(u_ref, unext_ref, h0re_ref, h0im_ref, bmat_ref, lre_ref, lim_ref, cmat_ref, d_ref, wglu_ref,
               y_ref, hre_out, him_out, bu_scr, hs_scr, hre_scr, him_scr, *, bp, tt, wide):
    i = pl.program_id(0)
    nt = S5_WIDTH // LANES

    rows_of = (lambda ref: ref[...].reshape(tt, bp, W_GROUP).reshape(tt * bp, W_GROUP)) if wide else (
        lambda ref: ref[...])

    def project(u_tile, slot):
        bu = jnp.dot(u_tile.astype(BF16), bmat_ref[...], preferred_element_type=F32)
        for j in range(2 * nt):
            bu_scr[slot, j] = bu[:, j * LANES:(j + 1) * LANES]

    @pl.when(i == 0)
    def _():
        hre_scr[...] = h0re_ref[...]
        him_scr[...] = h0im_ref[...]
        project(rows_of(u_ref), 0)

    slot = lax.rem(i, 2)
    lre = [lre_ref[:, j * LANES:(j + 1) * LANES] for j in range(nt)]
    lim = [lim_ref[:, j * LANES:(j + 1) * LANES] for j in range(nt)]

    def step(t, carry):
        rows = pl.ds(pl.multiple_of(t * bp, bp), bp)
        new = []
        for j in range(nt):
            hre, him = carry[j], carry[nt + j]
            new.append((lre[j] * hre - lim[j] * him + bu_scr[slot, j, rows, :],
                        lre[j] * him + lim[j] * hre + bu_scr[slot, nt + j, rows, :]))
        out = tuple(x[0] for x in new) + tuple(x[1] for x in new)
        for j in range(2 * nt):
            hs_scr[j, rows, :] = out[j]
        return out

    carry = tuple(hre_scr[:, j * LANES:(j + 1) * LANES] for j in range(nt)) + tuple(
        him_scr[:, j * LANES:(j + 1) * LANES] for j in range(nt))
    carry = step(0, carry) if tt == 1 else lax.fori_loop(0, tt, step, carry)
    hre = jnp.concatenate(carry[0:nt], axis=1)
    him = jnp.concatenate(carry[nt:2 * nt], axis=1)
    hre_scr[...] = hre
    him_scr[...] = him
    hre_out[...] = hre
    him_out[...] = him

    project(rows_of(unext_ref), lax.rem(i + 1, 2))
    u = rows_of(u_ref)
    hs = jnp.concatenate([hs_scr[j] for j in range(2 * nt)], axis=1)
    y = jnp.dot(hs.astype(BF16), cmat_ref[...], preferred_element_type=F32) + d_ref[...] * u
    z = _gelu(y)
    y = z * _sigmoid(_dot(z, wglu_ref[...]))
    if wide:
        y = y.reshape(tt, bp, W_GROUP).reshape(tt, bp * W_GROUP)
    y_ref[...] = y.astype(y_ref.dtype)


def _s5(u, h0re, h0im, sp, bp, t_len, tt, wide):
    nsteps = t_len // tt
    full = lambda shape: pl.BlockSpec(shape, lambda i: (0,) * len(shape))
    blk = (tt, bp * W_GROUP) if wide else (tt * bp, W_GROUP)
    tile = pl.BlockSpec(blk, lambda i: (i, 0))
    next_tile = pl.BlockSpec(blk, lambda i: (jnp.minimum(i + 1, nsteps - 1), 0))
    return pl.pallas_call(
        functools.partial(_s5_kernel, bp=bp, tt=tt, wide=wide),
        grid=(nsteps,),
        in_specs=[
            tile, next_tile,
            full((bp, S5_WIDTH)), full((bp, S5_WIDTH)),
            sp["bmat"].spec(), sp["lre"].spec(), sp["lim"].spec(), sp["cmat"].spec(), sp["d"].spec(),
            sp["wglu"].spec(),
        ],
        out_specs=[
            tile,
            full((bp, S5_WIDTH)), full((bp, S5_WIDTH)),
        ],
        out_shape=[
            jax.ShapeDtypeStruct(u.shape, BF16),
            jax.ShapeDtypeStruct((bp, S5_WIDTH), F32),
            jax.ShapeDtypeStruct((bp, S5_WIDTH), F32),
        ],
        scratch_shapes=[
            pltpu.VMEM((2, 2 * S5_WIDTH // LANES, tt * bp, LANES), F32),
            pltpu.VMEM((2 * S5_WIDTH // LANES, tt * bp, LANES), F32),
            pltpu.VMEM((bp, S5_WIDTH), F32),
            pltpu.VMEM((bp, S5_WIDTH), F32),
        ],
        compiler_params=_cparams("arbitrary"),
        name="s5_scan",
    )(u, u, h0re, h0im, sp["bmat"].arr, sp["lre"].arr, sp["lim"].arr, sp["cmat"].arr, sp["d"].arr,
      sp["wglu"].arr)


def _run_interleaved(chains):
    live = list(chains)
    while live:
        still = []
        for ch in live:
            try:
                next(ch)
                still.append(ch)
            except StopIteration:
                pass
        live = still


def _mlstm_kernel(q_ref, k_ref, v_ref, og_ref, sm_ref, gt_ref, brow_ref, bcol_ref, norm_ref,
                  y_ref, c_out, n_out, m_out, c_scr, n_scr, m_scr, *, L, nc, G):
    c = pl.program_id(1)

    @pl.when(c == 0)
    def _():
        c_scr[...] = jnp.zeros_like(c_scr)
        n_scr[...] = jnp.zeros_like(n_scr)
        m_scr[...] = jnp.zeros_like(m_scr)

    ones = _head_ones()
    tril = _tril_ones(L)
    triu = _triu_ones(L)
    l_shift = L.bit_length() - 1
    src = _iota((LANES, N_HEADS * L), 0)
    to_scores = jnp.where(src == LANE_MI + (_iota((LANES, N_HEADS * L), 1) >> l_shift), 1.0, 0.0).astype(BF16)
    src = _iota((LANES, W_GROUP), 0)
    to_feats = jnp.where(src == LANE_MI + (_iota((LANES, W_GROUP), 1) >> HEAD_SHIFT), 1.0, 0.0).astype(BF16)
    seg = _iota((N_HEADS * L, W_GROUP), 0) >> l_shift
    score_sum = jnp.where(seg == (_iota((N_HEADS * L, W_GROUP), 1) >> HEAD_SHIFT), 1.0, 0.0).astype(BF16)
    causal = _iota((L, N_HEADS * L), 0) >= (_iota((L, N_HEADS * L), 1) & (L - 1))
    t_idx = _iota((L, LANES), 0)

    seqs = list(range(G))
    cat = lambda xs: jnp.concatenate(xs, axis=0)
    split = lambda x: [x[i * L:(i + 1) * L] for i in seqs]

    def running_max(x):
        shift = 1
        while shift < L:
            x = jnp.maximum(x, jnp.where(t_idx >= shift, pltpu.roll(x, shift, 0), NEG_BIG))
            shift *= 2
        return x

    q = [q_ref[i] for i in seqs]
    k = [k_ref[i] * (HEAD_DIM ** -0.5) for i in seqs]
    v = [v_ref[i] for i in seqs]
    C = [c_scr[i] for i in seqs]
    n = [n_scr[i, 0:1, :] for i in seqs]
    m0 = [m_scr[i, 0:1, :] for i in seqs]
    pre = [sm_ref[i] + brow_ref[...] for i in seqs]
    gtb = [gt_ref[i] + bcol_ref[...] for i in seqs]
    bc_col = _each(lambda x: _dot_exact_rhs(tril, _log_sigmoid(x)), pre)
    bc_rows = _dot_exact_lhs(_log_sigmoid(cat(gtb)), triu)
    qC = _each(_dot_nt, q, C)
    qk = _each(lambda q_, k_: _dot_nt(q_, _stack_heads(k_)), q, k)
    qn = _head_sum(cat(_each(lambda q_, n_: q_ * n_, q, n)), ones)

    b_col = _each(lambda x: pltpu.roll(x, LANES - N_HEADS, 1), bc_col)
    r_col = _each(lambda p_, b_: p_ - b_, pre, b_col)
    mu = _each(lambda r_, m_: jnp.maximum(running_max(r_), m_), r_col, m0)
    mu_last = _each(lambda x: x[L - 1:L, :], mu)
    mu_scores = split(_dot_exact_lhs(cat(mu), to_scores))

    def weights(i):
        f0 = GATE_ROWS * i + N_HEADS
        r_rows = gtb[i][0:N_HEADS, :] - bc_rows[f0:f0 + N_HEADS, :]
        r_all = jnp.concatenate([r_rows[h:h + 1, :] for h in range(N_HEADS)], axis=1)
        return jnp.exp(jnp.where(causal, r_all - mu_scores[i], NEG_BIG)) * qk[i]

    ws = [weights(i) for i in seqs]
    wsv = _each(lambda w_, v_: _dot(w_, _stack_heads(v_)), ws, v)
    w0 = _dot(cat(_each(lambda m_, u_: jnp.exp(m_ - u_), m0, mu)), to_feats)
    m_row = _dot_exact_lhs(cat(_each(lambda b_, u_: b_ + u_, b_col, mu)), to_feats)
    den = w0 * qn + _dot(cat(ws), score_sum)
    hc = (w0 * cat(qC) + cat(wsv)) / jnp.maximum(jnp.abs(den), jnp.exp(-m_row))

    wl = split(_dot(cat(_each(lambda r_, u_: jnp.exp(r_ - u_), r_col, mu_last)), to_feats))
    w0l_rows = _each(lambda m_, u_: jnp.exp(m_ - u_), m0, mu_last)
    w0l_rows = w0l_rows + [jnp.zeros((1, LANES), F32)] * (-G % SUBLANES)
    w0l = _dot_exact_lhs(cat(w0l_rows), to_feats)
    kw = _each(lambda k_, w_: k_ * w_, k, wl)
    upd = _each(_dot_tn, v, kw)
    for i in seqs:
        c_scr[i] = w0l[i:i + 1, :] * C[i] + ones.astype(F32) * upd[i]
        n_scr[i, 0:1, :] = w0l[i:i + 1, :] * n[i] + jnp.sum(kw[i], axis=0, keepdims=True)
        m_scr[i, 0:1, :] = b_col[i][L - 1:L, :] + mu_last[i]

    ms = _head_sum(hc * hc, ones) * (1.0 / HEAD_DIM)
    y = split(hc * lax.rsqrt(ms + EPS) * norm_ref[...])
    for i in seqs:
        y_ref[i] = (y[i] * _sigmoid(og_ref[i])).astype(y_ref.dtype)

    @pl.when(c == nc - 1)
    def _():
        _store_head_blocks(c_out, c_scr)
        n_out[...] = jnp.broadcast_to(n_scr[:, 0:1, :], n_out.shape)
        m_out[...] = m_scr[...]


def _seq_spec(G, L, blk, width=W_GROUP):
    return pl.BlockSpec((G, L, width), lambda b, c: (b, c, blk))


def _mlstm(proj, gt, p, n_seq, t_len):
    L = CHUNK_ML
    G = SEQS_PER_STEP_ML
    nc = t_len // L
    per_seq = lambda rows, width: pl.BlockSpec((G, rows, width), lambda b, c: (b, 0, 0))
    proj3 = proj.reshape(n_seq, t_len, D_PROJ)
    y, c_new, n_new, m_new = pl.pallas_call(
        functools.partial(_mlstm_kernel, L=L, nc=nc, G=G),
        grid=(n_seq // G, nc),
        in_specs=[
            _seq_spec(G, L, BLK_MQ), _seq_spec(G, L, BLK_MK), _seq_spec(G, L, BLK_MV),
            _seq_spec(G, L, BLK_MO), _seq_spec(G, L, BLK_SMALL, SMALL_W),
            pl.BlockSpec((G, GATE_ROWS, L), lambda b, c: (b, 0, c)),
            p["ml_brow"].spec(), p["ml_bcol"].spec(), p["ml_norm"].spec(),
        ],
        out_specs=[
            pl.BlockSpec((G, L, W_GROUP), lambda b, c: (b, c, 0)),
            _HEAD_BLOCKS_SPEC(G), per_seq(SUBLANES, W_GROUP), per_seq(SUBLANES, LANES),
        ],
        out_shape=[
            jax.ShapeDtypeStruct((n_seq, t_len, W_GROUP), BF16),
            jax.ShapeDtypeStruct((n_seq, N_HEADS, HEAD_DIM, HEAD_DIM), F32),
            jax.ShapeDtypeStruct((n_seq, SUBLANES, W_GROUP), F32),
            jax.ShapeDtypeStruct((n_seq, SUBLANES, LANES), F32),
        ],
        scratch_shapes=[
            pltpu.VMEM((G, W_GROUP, W_GROUP), F32),
            pltpu.VMEM((G, SUBLANES, W_GROUP), F32),
            pltpu.VMEM((G, SUBLANES, LANES), F32),
        ],
        compiler_params=_cparams("arbitrary", "arbitrary"),
        name="mlstm_chunk",
    )(proj3, proj3, proj3, proj3, proj3, gt, p["ml_brow"].arr, p["ml_bcol"].arr, p["ml_norm"].arr)
    return y.reshape(n_seq * t_len, W_GROUP), c_new, n_new, m_new


def _gla_kernel(q_ref, k_ref, v_ref, gg_ref, sm_ref, wa_ref, ba_ref, norm_ref,
                y_ref, s_out, s_scr, *, L, nc, G):
    c = pl.program_id(1)

    @pl.when(c == 0)
    def _():
        s_scr[...] = jnp.zeros_like(s_scr)

    ones = _head_ones()
    tril = _tril_ones(L)
    causal = _iota((L, N_HEADS * L), 0) >= (_iota((L, N_HEADS * L), 1) & (L - 1))

    def chain(gi):
        q = q_ref[gi] * (HEAD_DIM ** -0.5)
        k = k_ref[gi]
        v = v_ref[gi]
        la = _log_sigmoid(_dot(sm_ref[gi], wa_ref[...]) + ba_ref[...]) * (1.0 / GLA_GATE_TEMP)
        yield
        bc = _dot_exact_rhs(tril, la)
        yield
        ref = bc[L // 2 - 1:L // 2, :]
        qr = q * jnp.exp(jnp.minimum(bc - ref, EXP_CLAMP))
        kh = k * jnp.exp(jnp.minimum(ref - bc, EXP_CLAMP))
        ST = s_scr[gi]
        att = jnp.where(causal, _dot_nt(qr, _stack_heads(kh)), 0.0)
        from_state = _dot_nt(q * jnp.exp(bc), ST)
        yield
        o = from_state + _dot(att, _stack_heads(v))
        b_last = bc[L - 1:L, :]
        kbar = k * jnp.exp(b_last - bc)
        s_scr[gi] = ST * jnp.exp(b_last) + ones.astype(F32) * _dot_tn(v, kbar)
        yield
        ms = _head_sum(o * o, ones) * (1.0 / HEAD_DIM)
        gg = gg_ref[gi]
        y = o * lax.rsqrt(ms + EPS) * norm_ref[...] * (gg * _sigmoid(gg))
        y_ref[gi] = y.astype(y_ref.dtype)

    _run_interleaved([chain(gi) for gi in range(G)])

    @pl.when(c == nc - 1)
    def _():
        _store_head_blocks(s_out, s_scr, transpose=True)


def _gla(proj, p, n_seq, t_len):
    L = CHUNK_ML
    G = SEQS_PER_STEP_ML
    nc = t_len // L
    proj3 = proj.reshape(n_seq, t_len, D_PROJ)
    y, s_new = pl.pallas_call(
        functools.partial(_gla_kernel, L=L, nc=nc, G=G),
        grid=(n_seq // G, nc),
        in_specs=[
            _seq_spec(G, L, BLK_GQ), _seq_spec(G, L, BLK_GK), _seq_spec(G, L, BLK_GV),
            _seq_spec(G, L, BLK_GG), _seq_spec(G, L, BLK_SMALL, SMALL_W),
            p["gla_wa"].spec(), p["gla_ba"].spec(), p["gla_norm"].spec(),
        ],
        out_specs=[
            pl.BlockSpec((G, L, W_GROUP), lambda b, c: (b, c, 0)),
            _HEAD_BLOCKS_SPEC(G),
        ],
        out_shape=[
            jax.ShapeDtypeStruct((n_seq, t_len, W_GROUP), BF16),
            jax.ShapeDtypeStruct((n_seq, N_HEADS, HEAD_DIM, HEAD_DIM), F32),
        ],
        scratch_shapes=[pltpu.VMEM((G, W_GROUP, W_GROUP), F32)],
        compiler_params=_cparams("arbitrary", "arbitrary"),
        name="gla_chunk",
    )(proj3, proj3, proj3, proj3, proj3, p["gla_wa"].arr, p["gla_ba"].arr, p["gla_norm"].arr)
    return y.reshape(n_seq * t_len, W_GROUP), s_new


def _rwkv_vectors(rc, prev, p_mu, p_w0, p_a0, p_kk, p_ka, p_rk, w2, a2, g2, ones):
    xm = rc + p_mu * (prev - rc)
    rr = xm[:, 0:W_GROUP]
    rk = xm[:, W_GROUP:2 * W_GROUP]
    rv = xm[:, 2 * W_GROUP:3 * W_GROUP]
    tail = xm[:, 3 * W_GROUP:RWKV_COLS]
    lw = -RWKV_DECAY_SCALE * _sigmoid(p_w0 + _dot(jnp.tanh(tail), w2))
    a = _sigmoid(p_a0 + _dot(tail, a2))
    g = _dot(_sigmoid(tail), g2)
    kk = rk * p_kk
    kk = kk * lax.rsqrt(jnp.maximum(_head_sum(kk * kk, ones), 1e-24))
    kt = rk * (1.0 + (a - 1.0) * p_ka)
    bonus = _head_sum(rr * kt * p_rk, ones) * rv
    return rr, lw, kt, rv, kk, a, g, bonus


def _head_layernorm(o, g, ones):
    mu = _head_sum(o, ones) * (1.0 / HEAD_DIM)
    oc = o - mu
    var = _head_sum(oc * oc, ones) * (1.0 / HEAD_DIM)
    return oc * lax.rsqrt(var + GN_EPS) * g


def _stack_heads(x):
    xb = x.astype(BF16)
    lane_head = _iota((1, W_GROUP), 1) >> HEAD_SHIFT
    return jnp.concatenate([jnp.where(lane_head == h, xb, jnp.zeros_like(xb)) for h in range(N_HEADS)],
                           axis=0)


def _each(f, *seqs):
    return [f(*args) for args in zip(*seqs)]


def _block_mm(x_list, y_list):
    return _each(lambda x, y: _dot(x, _stack_heads(y)), x_list, y_list)


def _unit_lower_inverse(a_list, L):
    t_idx = _iota((L, N_HEADS * L), 0)
    s_idx = _iota((L, N_HEADS * L), 1) & (L - 1)
    eye = jnp.where(t_idx == s_idx, 1.0, 0.0).astype(F32)
    in16 = (t_idx >> 4) == (s_idx >> 4)
    in32 = ((t_idx >> 5) == (s_idx >> 5)) & ((t_idx >> 4) > (s_idx >> 4))
    in64 = (t_idx >> 5) > (s_idx >> 5)
    x16 = _each(lambda a: jnp.where(in16, -a, 0.0), a_list)
    x2 = _block_mm(x16, x16)
    x4 = _block_mm(x2, x2)
    x8 = _block_mm(x4, x4)
    t = _each(lambda x: eye + x, x16)
    for xp in (x2, x4, x8):
        t = _each(lambda t_, d: t_ + d, t, _block_mm(t, xp))
    for sel in (in32, in64):
        w = _block_mm(_each(lambda a: jnp.where(sel, a, 0.0), a_list), t)
        t = _each(lambda t_, d: t_ - d, t, _block_mm(t, w))
    return t


def _rwkv_chunks(rcs, prev_rows, states, prm, ones, L):
    (mu, w0, a0, pkk, pka, prk, w2, a2, g2, norm) = prm
    first_row = _iota((L, RWKV_COLS), 0) == 0
    prevs = _each(lambda rc, pr: jnp.where(first_row, pr, pltpu.roll(rc, 1, 0)), rcs, prev_rows)
    n_seq = len(rcs)
    stacked = _rwkv_vectors(jnp.concatenate(rcs, axis=0), jnp.concatenate(prevs, axis=0),
                            mu, w0, a0, pkk, pka, prk, w2, a2, g2, ones)
    r, lw, kx, v, kk, a, g, bonus = [[x[i * L:(i + 1) * L] for i in range(n_seq)] for x in stacked]

    tril = _tril_ones(L)
    t_idx = _iota((L, N_HEADS * L), 0)
    s_idx = _iota((L, N_HEADS * L), 1) & (L - 1)
    strict = t_idx > s_idx
    incl = t_idx >= s_idx

    def core(r, lw, kx, v, kk, a, states):
        lc = _each(lambda x: _dot_exact_rhs(tril, x), lw)
        akk = _each(lambda a_, k_: a_ * k_, a, kk)
        e_neg = _each(lambda x: jnp.exp(-x), lc)
        l_last = _each(lambda x: x[L - 1:L, :], lc)
        e_end = _each(lambda ll, x: jnp.exp(ll - x), l_last, lc)
        kr = _each(lambda k_, r_, c_, w_: jnp.concatenate([k_ * jnp.exp(c_ - w_), r_ * jnp.exp(c_)], axis=0),
                   kk, r, lc, lw)
        a_hat = _each(lambda x, e: _stack_heads(x * e), akk, e_neg)
        k_hat = _each(lambda x, e: _stack_heads(x * e), kx, e_neg)
        v_bd = _each(_stack_heads, v)

        pa = _each(_dot_nt, kr, a_hat)
        pk = _each(_dot_nt, kr, k_hat)
        a_ua = _each(lambda p_: jnp.where(strict, p_[0:L], 0.0), pa)
        b_ra = _each(lambda p_: jnp.where(incl, p_[L:2 * L], 0.0), pa)
        ab_k = _each(lambda p_: jnp.concatenate([jnp.where(strict, p_[0:L], 0.0),
                                                 jnp.where(incl, p_[L:2 * L], 0.0)], axis=0), pk)
        T = _unit_lower_inverse(a_ua, L)

        from_state = _each(_dot_nt, kr, states)
        from_v = _each(_dot, ab_k, v_bd)
        rhs = _each(lambda s_, v_: s_[0:L] + v_[0:L], from_state, from_v)
        U = _block_mm(T, rhs)
        corr = _block_mm(b_ra, U)
        o = _each(lambda s_, v_, c_: s_[L:2 * L] + v_[L:2 * L] - c_, from_state, from_v, corr)
        upd = _each(lambda v_, u_, k_, a_, e: _dot_tn(jnp.concatenate([v_, u_], axis=0),
                                                      jnp.concatenate([k_ * e, -(a_ * e)], axis=0)),
                    v, U, kx, akk, e_end)
        s_new = _each(lambda s_, ll, d: s_ * jnp.exp(ll) + ones.astype(F32) * d, states, l_last, upd)
        return o, s_new

    o, s_new = core(r, lw, kx, v, kk, a, states)
    y_all = ((_head_layernorm(jnp.concatenate(o, axis=0), norm, ones) + stacked[7]) * stacked[6])
    return [y_all[i * L:(i + 1) * L] for i in range(n_seq)], s_new


def _rwkv_kernel(rc_ref, mu_ref, w0_ref, a0_ref, kk_ref, ka_ref, rk_ref, w2_ref, a2_ref, g2_ref,
                 norm_ref, y_ref, s_out, shift_out, s_scr, prev_scr, *, L, nc, G):
    c = pl.program_id(1)

    @pl.when(c == 0)
    def _():
        s_scr[...] = jnp.zeros_like(s_scr)
        prev_scr[...] = jnp.zeros_like(prev_scr)

    ones = _head_ones()
    prm = (mu_ref[...], w0_ref[...], a0_ref[...], kk_ref[...], ka_ref[...], rk_ref[...],
           w2_ref[...], a2_ref[...], g2_ref[...], norm_ref[...])
    rcs = [rc_ref[gi] for gi in range(G)]
    ys, s_new = _rwkv_chunks(rcs, [prev_scr[gi, 0:1, :] for gi in range(G)],
                             [s_scr[gi] for gi in range(G)], prm, ones, L)
    for gi in range(G):
        prev_scr[gi, 0:1, :] = rcs[gi][L - 1:L, :]
        s_scr[gi] = s_new[gi]
        y_ref[gi] = ys[gi].astype(y_ref.dtype)

    @pl.when(c == nc - 1)
    def _():
        _store_head_blocks(s_out, s_scr)
        shift_out[...] = jnp.broadcast_to(prev_scr[:, 0:1, :], shift_out.shape)


def _rwkv(proj, p, n_seq, t_len):
    L = CHUNK_RW
    G = SEQS_PER_STEP
    nc = t_len // L
    names = ("rw_mu", "rw_w0", "rw_a0", "rw_kk", "rw_ka", "rw_rk", "rw_w2", "rw_a2", "rw_g2", "rw_norm")
    y, s_new, shift = pl.pallas_call(
        functools.partial(_rwkv_kernel, L=L, nc=nc, G=G),
        grid=(n_seq // G, nc),
        in_specs=[
            pl.BlockSpec((G, L, RWKV_COLS), lambda b, c: (b, c, 0)),
        ] + [p[k].spec() for k in names],
        out_specs=[
            pl.BlockSpec((G, L, W_GROUP), lambda b, c: (b, c, 0)),
            _HEAD_BLOCKS_SPEC(G),
            pl.BlockSpec((G, SUBLANES, RWKV_COLS), lambda b, c: (b, 0, 0)),
        ],
        out_shape=[
            jax.ShapeDtypeStruct((n_seq, t_len, W_GROUP), BF16),
            jax.ShapeDtypeStruct((n_seq, N_HEADS, HEAD_DIM, HEAD_DIM), F32),
            jax.ShapeDtypeStruct((n_seq, SUBLANES, RWKV_COLS), F32),
        ],
        scratch_shapes=[
            pltpu.VMEM((G, W_GROUP, W_GROUP), F32),
            pltpu.VMEM((G, SUBLANES, RWKV_COLS), F32),
        ],
        compiler_params=_cparams("arbitrary", "arbitrary"),
        name="rwkv_chunk",
    )(proj.reshape(n_seq, t_len, D_PROJ), *[p[k].arr for k in names])
    return y.reshape(n_seq * t_len, W_GROUP), s_new, shift


FF_CHUNK = 256


def _mix_residual(x_ref, ys, wout_ref):
    acc = x_ref[...]
    for j, y in enumerate(ys):
        acc = acc + jnp.dot(y.astype(BF16), wout_ref[j * W_GROUP:(j + 1) * W_GROUP, :],
                            preferred_element_type=F32)
    return acc


def _ffn_body(x1, nrm_ref, wup_ref, cw_ref, cb_ref, wdn_ref, prev_rows, d_ff, act_scr):
    h2 = _rmsnorm_rows(x1, nrm_ref[...]).astype(BF16)
    for j in range(d_ff // FF_CHUNK):
        lo, hi = j * FF_CHUNK, (j + 1) * FF_CHUNK
        ug = jnp.dot(h2, wup_ref[:, lo:hi], preferred_element_type=F32)
        uv = jnp.dot(h2, wup_ref[:, d_ff + lo:d_ff + hi], preferred_element_type=F32)
        p2, p1 = prev_rows(j, ug)
        conv = (cb_ref[:, lo:hi] + cw_ref[0:1, lo:hi] * p2 + cw_ref[1:2, lo:hi] * p1
                + cw_ref[2:3, lo:hi] * ug)
        act_scr[:, lo:hi] = (_gelu(conv) * uv).astype(BF16)
    return x1 + jnp.dot(act_scr[...], wdn_ref[...], preferred_element_type=F32)


def _ffn_seq_kernel(x_ref, y0_ref, y1_ref, y2_ref, y3_ref, wout_ref, nrm_ref, wup_ref, cw_ref, cb_ref,
                    wdn_ref, nf_ref, o_ref, buf_out, carry_scr, act_scr, *, tm, d_ff, final):
    t = pl.program_id(1)

    @pl.when(t == 0)
    def _():
        carry_scr[...] = jnp.zeros_like(carry_scr)

    x1 = _mix_residual(x_ref, (y0_ref[...], y1_ref[...], y2_ref[...], y3_ref[...]), wout_ref)
    row = _iota((tm, FF_CHUNK), 0)

    def prev_rows(j, ug):
        lo, hi = j * FF_CHUNK, (j + 1) * FF_CHUNK
        c0 = carry_scr[0:1, lo:hi]
        c1 = carry_scr[1:2, lo:hi]
        p1 = jnp.where(row == 0, c1, pltpu.roll(ug, 1, 0))
        p2 = jnp.where(row == 0, c0, jnp.where(row == 1, c1, pltpu.roll(ug, 2, 0)))
        carry_scr[0:2, lo:hi] = ug[tm - 2:tm, :]
        return p2, p1

    out = _ffn_body(x1, nrm_ref, wup_ref, cw_ref, cb_ref, wdn_ref, prev_rows, d_ff, act_scr)
    if final:
        out = _rmsnorm_rows(out, nf_ref[...])
    o_ref[...] = out
    buf_out[0] = carry_scr[...]


def _ffn_step_kernel(x_ref, y0_ref, y1_ref, y2_ref, y3_ref, wout_ref, nrm_ref, wup_ref, cw_ref, cb_ref,
                     wdn_ref, nf_ref, p2_ref, p1_ref, o_ref, ug_out, act_scr, *, d_ff, final):
    x1 = _mix_residual(x_ref, (y0_ref[...], y1_ref[...], y2_ref[...], y3_ref[...]), wout_ref)

    def prev_rows(j, ug):
        lo, hi = j * FF_CHUNK, (j + 1) * FF_CHUNK
        ug_out[:, lo:hi] = ug
        return p2_ref[:, lo:hi], p1_ref[:, lo:hi]

    out = _ffn_body(x1, nrm_ref, wup_ref, cw_ref, cb_ref, wdn_ref, prev_rows, d_ff, act_scr)
    if final:
        out = _rmsnorm_rows(out, nf_ref[...])
    o_ref[...] = out


def _ffn_seq(x2d, ys, p, norm_final, n_seq, t_len, tm, final):
    n, d = x2d.shape
    d_ff = p["ffn_wdn"].shape[0]
    nt = t_len // tm
    rows = lambda w: pl.BlockSpec((tm, w), lambda b, t: (b * nt + t, 0))
    once = dict(pipeline_mode=pl.Buffered(1))
    return pl.pallas_call(
        functools.partial(_ffn_seq_kernel, tm=tm, d_ff=d_ff, final=final),
        grid=(n_seq, nt),
        in_specs=[rows(d), pl.BlockSpec((tm, W_GROUP), lambda b, t: (t, b))] + [rows(W_GROUP)] * 3 + [
            p["w_out"].spec(**once), p["norm_ffn"].spec(), p["ffn_wup"].spec(**once), p["ffn_cw"].spec(),
            p["ffn_cb"].spec(), p["ffn_wdn"].spec(**once), norm_final.spec()],
        out_specs=[rows(d), pl.BlockSpec((1, CONV_W - 1, d_ff), lambda b, t: (b, 0, 0))],
        out_shape=[jax.ShapeDtypeStruct((n, d), F32),
                   jax.ShapeDtypeStruct((n_seq, CONV_W - 1, d_ff), F32)],
        scratch_shapes=[pltpu.VMEM((CONV_W - 1, d_ff), F32), pltpu.VMEM((tm, d_ff), BF16)],
        compiler_params=_cparams("arbitrary", "arbitrary"),
        name="wout_ffn_seq",
    )(x2d, *ys, p["w_out"].arr, p["norm_ffn"].arr, p["ffn_wup"].arr, p["ffn_cw"].arr, p["ffn_cb"].arr,
      p["ffn_wdn"].arr, norm_final.arr)


def _ffn_step(x2d, ys, p, norm_final, prev2, prev1, final):
    n, d = x2d.shape
    d_ff = p["ffn_wdn"].shape[0]
    full = lambda shape: pl.BlockSpec(shape, lambda i: (0,) * len(shape))
    return pl.pallas_call(
        functools.partial(_ffn_step_kernel, d_ff=d_ff, final=final),
        grid=(1,),
        in_specs=[full((n, d))] + [full((n, W_GROUP))] * 4 + [
            p["w_out"].spec(), p["norm_ffn"].spec(), p["ffn_wup"].spec(), p["ffn_cw"].spec(),
            p["ffn_cb"].spec(), p["ffn_wdn"].spec(), norm_final.spec(), full((n, d_ff)), full((n, d_ff))],
        out_specs=[full((n, d)), full((n, d_ff))],
        out_shape=[jax.ShapeDtypeStruct((n, d), F32), jax.ShapeDtypeStruct((n, d_ff), F32)],
        scratch_shapes=[pltpu.VMEM((n, d_ff), BF16)],
        compiler_params=_cparams("arbitrary"),
        name="wout_ffn_step",
    )(x2d, *ys, p["w_out"].arr, p["norm_ffn"].arr, p["ffn_wup"].arr, p["ffn_cw"].arr, p["ffn_cb"].arr,
      p["ffn_wdn"].arr, norm_final.arr, prev2, prev1)


def _step_vec_kernel(proj_ref, shift_ref, m_ref, brow_ref, wa_ref, ba_ref, mu_ref, w0_ref, a0_ref,
                     kk_ref, ka_ref, rk_ref, w2_ref, a2_ref, g2_ref,
                     pack_out, m_out, g_out, bonus_out):
    ones = _head_ones()
    sm = proj_ref[:, OFF_SMALL:OFF_SMALL + SMALL_W]
    pre = sm + brow_ref[...]
    logf = pltpu.roll(_log_sigmoid(pre), LANES - N_HEADS, 1)
    m_old = m_ref[...]
    m_new = jnp.maximum(logf + m_old, pre)
    m_out[...] = m_new
    lane = _iota((SMALL_W, W_GROUP), 0)
    spread_mat = jnp.where(lane == LANE_MI + (_iota((SMALL_W, W_GROUP), 1) >> HEAD_SHIFT), 1.0, 0.0).astype(BF16)
    gate_lanes = (_iota(pre.shape, 1) >= LANE_MI) & (_iota(pre.shape, 1) < LANE_MI + N_HEADS)
    spread = lambda x: _dot_exact_lhs(jnp.where(gate_lanes, x, 0.0), spread_mat)
    la = _log_sigmoid(_dot(sm, wa_ref[...]) + ba_ref[...]) * (1.0 / GLA_GATE_TEMP)
    rc = proj_ref[:, 0:RWKV_COLS]
    rr, lw, kt, rv, kk, a, g, bonus = _rwkv_vectors(
        rc, shift_ref[...], mu_ref[...], w0_ref[...], a0_ref[...], kk_ref[...], ka_ref[...],
        rk_ref[...], w2_ref[...], a2_ref[...], g2_ref[...], ones)
    main = lambda blk: proj_ref[:, blk * W_GROUP:(blk + 1) * W_GROUP]
    rows = {
        ROW_MQ: main(BLK_MQ), ROW_MK: main(BLK_MK), ROW_MV: main(BLK_MV),
        ROW_GQ: main(BLK_GQ), ROW_GK: main(BLK_GK), ROW_GV: main(BLK_GV), ROW_GA: jnp.exp(la),
        ROW_KK: kk, ROW_AKK: a * kk, ROW_W: jnp.exp(lw), ROW_KT: kt, ROW_RR: rr, ROW_RV: rv,
        ROW_IW: spread(jnp.exp(pre - m_new)), ROW_FW: spread(jnp.exp(logf + m_old - m_new)),
        ROW_M: spread(m_new),
    }
    for i, val in rows.items():
        pack_out[i] = val.T
    g_out[...] = g
    bonus_out[...] = bonus


def _step_vectors(proj, shift, m_pad, p):
    n = proj.shape[0]
    full = lambda a: pl.BlockSpec(a.shape, lambda i: (0,) * a.ndim)
    data = (proj, shift, m_pad)
    prms = [p[k] for k in ("ml_brow", "gla_wa", "gla_ba", "rw_mu", "rw_w0", "rw_a0", "rw_kk", "rw_ka",
                           "rw_rk", "rw_w2", "rw_a2", "rw_g2")]
    wide = jax.ShapeDtypeStruct((n, W_GROUP), F32)
    outs = [jax.ShapeDtypeStruct((N_PACK, W_GROUP, n), F32), jax.ShapeDtypeStruct((n, SMALL_W), F32),
            wide, wide]
    return pl.pallas_call(
        _step_vec_kernel,
        grid=(1,),
        in_specs=[full(a) for a in data] + [q.spec() for q in prms],
        out_specs=[pl.BlockSpec(o.shape, lambda i, nd=len(o.shape): (0,) * nd) for o in outs],
        out_shape=outs,
        compiler_params=_cparams("arbitrary"),
        name="step_vectors",
    )(*data, *[q.arr for q in prms])


(ROW_MQ, ROW_MK, ROW_MV, ROW_GQ, ROW_GK, ROW_GV, ROW_GA, ROW_KK, ROW_AKK, ROW_W, ROW_KT, ROW_RR, ROW_RV,
 ROW_IW, ROW_FW, ROW_M) = range(16)
N_PACK = 16
(OROW_ML, OROW_GLA, OROW_RW, OROW_N) = range(4)
N_OPACK = 4
STEP_UNROLL = 4


def _heads_step_kernel(pack_ref, c_ref, n_ref, gs_ref, rs_ref, c_out, gs_out, rs_out, opack_out):
    vec = lambda i: pack_ref[i]
    one = lambda i, j: pack_ref[i, pl.ds(j, 1), :]
    colsum = lambda x: jnp.sum(x, axis=0, keepdims=True)

    k_ml = vec(ROW_MK) * (HEAD_DIM ** -0.5)
    q_ml = vec(ROW_MQ)
    iw = pack_ref[ROW_IW, 0:1, :]
    fw = pack_ref[ROW_FW, 0:1, :]
    n_new = fw * n_ref[...] + iw * k_ml
    opack_out[OROW_N] = n_new
    r_den = 1.0 / jnp.maximum(jnp.abs(colsum(n_new * q_ml)), jnp.exp(-pack_ref[ROW_M, 0:1, :]))

    def ml_body(v, carry):
        c_new = fw * c_ref[v] + (iw * one(ROW_MV, v)) * k_ml
        c_out[v] = c_new
        opack_out[OROW_ML, pl.ds(v, 1), :] = colsum(c_new * q_ml) * r_den
        return carry

    lax.fori_loop(0, HEAD_DIM, ml_body, 0, unroll=STEP_UNROLL)

    v_gla = vec(ROW_GV)

    def gla_body(k, acc):
        s_new = one(ROW_GA, k) * gs_ref[k] + one(ROW_GK, k) * v_gla
        gs_out[k] = s_new
        return acc + (one(ROW_GQ, k) * (HEAD_DIM ** -0.5)) * s_new

    opack_out[OROW_GLA] = lax.fori_loop(0, HEAD_DIM, gla_body, jnp.zeros_like(v_gla), unroll=STEP_UNROLL)

    kk, akk, w, kt, rr = vec(ROW_KK), vec(ROW_AKK), vec(ROW_W), vec(ROW_KT), vec(ROW_RR)

    def rw_body(v, carry):
        s = rs_ref[v]
        s_new = s * w - colsum(s * kk) * akk + one(ROW_RV, v) * kt
        rs_out[v] = s_new
        opack_out[OROW_RW, pl.ds(v, 1), :] = colsum(s_new * rr)
        return carry

    lax.fori_loop(0, HEAD_DIM, rw_body, 0, unroll=STEP_UNROLL)


def _heads_step(pack, ml_c, ml_n, gla_s, rw_s, layer):
    n = pack.shape[-1]
    sq = (HEAD_DIM, HEAD_DIM, n)
    head = lambda lead: pl.BlockSpec((lead, HEAD_DIM, n), lambda h: (0, h, 0))
    at_layer = lambda shape: pl.BlockSpec((None,) + shape, lambda h: (layer * N_HEADS + h,) + (0,) * len(shape))
    out_state = pl.BlockSpec((None,) + sq, lambda h: (h, 0, 0, 0))
    state = jax.ShapeDtypeStruct((N_HEADS,) + sq, F32)
    return pl.pallas_call(
        _heads_step_kernel,
        grid=(N_HEADS,),
        in_specs=[head(N_PACK), at_layer(sq), at_layer((HEAD_DIM, n)), at_layer(sq), at_layer(sq)],
        out_specs=[out_state, out_state, out_state, head(N_OPACK)],
        out_shape=[state, state, state, jax.ShapeDtypeStruct((N_OPACK, W_GROUP, n), F32)],
        compiler_params=_cparams("arbitrary"),
        name="heads_step",
    )(pack, ml_c, ml_n, gla_s, rw_s)


def _step_post_kernel(hml_ref, mo_ref, ogla_ref, gg_ref, orw_ref, g_ref, bonus_ref,
                      mln_ref, gln_ref, rwn_ref, yml_out, ygla_out, yrw_out):
    ones = _head_ones()
    hml = hml_ref[...].T
    ms = _head_sum(hml * hml, ones) * (1.0 / HEAD_DIM)
    yml_out[...] = (hml * lax.rsqrt(ms + EPS) * mln_ref[...] * _sigmoid(mo_ref[...])).astype(BF16)
    og = ogla_ref[...].T
    ms = _head_sum(og * og, ones) * (1.0 / HEAD_DIM)
    gg = gg_ref[...]
    ygla_out[...] = (og * lax.rsqrt(ms + EPS) * gln_ref[...] * (gg * _sigmoid(gg))).astype(BF16)
    yrw_out[...] = ((_head_layernorm(orw_ref[...].T, rwn_ref[...], ones) + bonus_ref[...])
                    * g_ref[...]).astype(BF16)


def _step_post(hml, mo, ogla, gg, orw, g, bonus, p):
    n = mo.shape[0]
    data = (hml, mo, ogla, gg, orw, g, bonus)
    prms = [p[k] for k in ("ml_norm", "gla_norm", "rw_norm")]
    full = lambda a: pl.BlockSpec(a.shape, lambda i: (0,) * a.ndim)
    out = jax.ShapeDtypeStruct((n, W_GROUP), BF16)
    return pl.pallas_call(
        _step_post_kernel,
        grid=(1,),
        in_specs=[full(a) for a in data] + [q.spec() for q in prms],
        out_specs=[pl.BlockSpec((n, W_GROUP), lambda i: (0, 0))] * 3,
        out_shape=[out] * 3,
        compiler_params=_cparams("arbitrary"),
        name="step_post",
    )(*data, *[q.arr for q in prms])


def _prepare_params(prm):
    P = {}
    w_in = prm["w_in"].astype(BF16)
    depth, d = w_in.shape[0], w_in.shape[1]
    sizes = (W_GROUP, W_GROUP, W_GROUP, W_GROUP, N_HEADS, N_HEADS, W_GROUP, W_GROUP, W_GROUP, W_GROUP,
             GLA_RANK, W_GROUP, RWKV_COLS)
    offs = [0]
    for sz in sizes:
        offs.append(offs[-1] + sz)
    col = lambda i: w_in[:, :, offs[i]:offs[i + 1]]
    (u, mq, mk, mv, mi, mf, mo, gq, gk, gv, ga, gg, rcols) = [col(i) for i in range(13)]
    small = jnp.concatenate([ga, mi, mf, jnp.zeros((depth, d, SMALL_W - GLA_RANK - 2 * N_HEADS), BF16)], axis=2)
    P["w_in"] = jnp.concatenate([rcols, small, u, mq, mk, mv, mo, gq, gk, gv, gg], axis=2)
    P["w_gate_t"] = jnp.swapaxes(jnp.concatenate([mi, mf], axis=2), 1, 2)

    lam = lax.complex(prm["s5_lam_re"], prm["s5_lam_im"])
    dt = jnp.exp(prm["s5_log_dt"])[..., None]
    lam_bar = jnp.exp(lam * dt)
    b_bar = ((lam_bar - 1.0) / lam)[..., None] * lax.complex(prm["s5_b_re"], prm["s5_b_im"])
    eye = jnp.eye(S5_GROUPS, dtype=F32)
    bm = lambda b: jnp.einsum("lgph,gk->lghkp", b, eye).reshape(depth, W_GROUP, S5_WIDTH)
    P["bmat"] = jnp.concatenate([bm(b_bar.real), bm(b_bar.imag)], axis=2).astype(BF16)
    cm = lambda c: jnp.einsum("lghp,gk->lkpgh", c, eye).reshape(depth, S5_WIDTH, W_GROUP)
    P["cmat"] = jnp.concatenate([cm(prm["s5_c_re"]), -cm(prm["s5_c_im"])], axis=1).astype(BF16)
    P["lam"] = jnp.stack([lam_bar.real.reshape(depth, S5_WIDTH), lam_bar.imag.reshape(depth, S5_WIDTH)],
                         axis=1)[:, :, None, :]
    P["wglu"] = prm["s5_w_glu"].astype(BF16)

    bias = prm["ml_gate_bias"]
    P["ml_brow"] = jnp.pad(bias, ((0, 0), (LANE_MI, SMALL_W - LANE_MI - 2 * N_HEADS)))[:, None, :]
    P["ml_bcol"] = jnp.broadcast_to(bias[:, :, None], (depth, 2 * N_HEADS, CHUNK_ML))

    P["rows"] = jnp.stack([prm["s5_d"], prm["ml_norm"], prm["gla_b_alpha"], prm["gla_norm"], prm["rw_w0"],
                           prm["rw_a0"], prm["rw_k_k"], prm["rw_k_a"], prm["rw_r_k"].reshape(depth, W_GROUP),
                           prm["rw_norm"]], axis=1)[:, :, None, :]
    pad_rows = lambda w, lo: jnp.pad(w, ((0, 0), (lo, LANES - lo - w.shape[1]), (0, 0)))
    P["pads"] = jnp.stack([pad_rows(prm["gla_w_alpha"], 0), pad_rows(prm["rw_w2"], 0),
                           pad_rows(prm["rw_a2"], RWKV_W_RANK),
                           pad_rows(prm["rw_g2"], RWKV_W_RANK + RWKV_A_RANK)], axis=1).astype(BF16)
    P["rw_mu"] = prm["rw_mu"][:, None, :]
    P["norms"] = jnp.stack([prm["norm_mix"], prm["norm_ffn"]], axis=1)[:, :, None, :]
    P["w_out"] = prm["w_out"].astype(BF16)
    P["ffn_wup"] = prm["ffn_w_up"].astype(BF16)
    P["ffn_cw"] = prm["ffn_conv_w"]
    P["ffn_cb"] = prm["ffn_conv_b"][:, None, :]
    P["ffn_wdn"] = prm["ffn_w_down"].astype(BF16)
    return P


ROW_NAMES = ("d", "ml_norm", "gla_ba", "gla_norm", "rw_w0", "rw_a0", "rw_kk", "rw_ka", "rw_rk", "rw_norm")
PAD_NAMES = ("gla_wa", "rw_w2", "rw_a2", "rw_g2")


def _layer_params(P, l):
    p = {k: _Param(P[k], l) for k in ("w_in", "w_gate_t", "bmat", "cmat", "wglu", "ml_brow", "ml_bcol",
                                      "rw_mu", "w_out", "ffn_wup", "ffn_cw", "ffn_cb", "ffn_wdn")}
    p.update({k: _Param(P["rows"], l, i) for i, k in enumerate(ROW_NAMES)})
    p.update({k: _Param(P["pads"], l, i) for i, k in enumerate(PAD_NAMES)})
    p["lre"], p["lim"] = _Param(P["lam"], l, 0), _Param(P["lam"], l, 1)
    p["norm_mix"], p["norm_ffn"] = _Param(P["norms"], l, 0), _Param(P["norms"], l, 1)
    return p


def _prompt_layer(x2d, p, norm_final, n_seq, t_len, final):
    proj, gt, u = _inproj(x2d, p["norm_mix"], p["w_in"], p["w_gate_t"], n_seq, t_len, tm=INPROJ_ROWS)

    zeros = jnp.zeros((n_seq, S5_WIDTH), F32)
    y_s5, s5_re, s5_im = _s5(u, zeros, zeros, p, n_seq, t_len, tt=S5_TILE_STEPS, wide=True)

    y_ml, ml_c, ml_n, ml_m = _mlstm(proj, gt, p, n_seq, t_len)
    y_gla, gla_st = _gla(proj, p, n_seq, t_len)
    y_rw, rw_s, rw_shift = _rwkv(proj, p, n_seq, t_len)

    x_new, ffn_buf = _ffn_seq(x2d, (y_s5, y_ml, y_gla, y_rw), p, norm_final, n_seq, t_len, tm=FFN_ROWS,
                              final=final)
    states = (
        s5_re.reshape(n_seq, S5_GROUPS, S5_STATE),
        s5_im.reshape(n_seq, S5_GROUPS, S5_STATE),
        ml_c,
        ml_n[:, 0, :].reshape(n_seq, N_HEADS, HEAD_DIM),
        ml_m[:, 0, LANE_MI:LANE_MI + N_HEADS],
        gla_st,
        rw_s,
        rw_shift[:, 0, :],
        ffn_buf,
    )
    return x_new, states


def _sample_layer(x2d, st, pair_states, layer, p, norm_final, final):
    (s5_re, s5_im, ml_m, rw_shift, ffn_buf) = st
    n = x2d.shape[0]
    npairs = n * N_HEADS
    proj, _, u = _inproj(x2d, p["norm_mix"], p["w_in"], p["w_gate_t"], 1, n, tm=n)

    y_s5, s5_re_new, s5_im_new = _s5(u,
                                     s5_re.reshape(n, S5_WIDTH), s5_im.reshape(n, S5_WIDTH),
                                     p, n, 1, tt=1, wide=False)

    m_pad = jnp.zeros((n, SMALL_W), F32).at[:, LANE_MI:LANE_MI + N_HEADS].set(ml_m)
    pack, m_new, g, bonus = _step_vectors(proj, rw_shift, m_pad, p)
    ml_c_new, gla_s_new, rw_s_new, opack = _heads_step(pack, *pair_states, layer)

    main = lambda blk: proj[:, blk * W_GROUP:(blk + 1) * W_GROUP]
    y_ml, y_gla, y_rw = _step_post(opack[OROW_ML], main(BLK_MO), opack[OROW_GLA], main(BLK_GG),
                                   opack[OROW_RW], g, bonus, p)
    x_new, ug = _ffn_step(x2d, (y_s5, y_ml, y_gla, y_rw), p, norm_final, ffn_buf[:, 0, :], ffn_buf[:, 1, :],
                          final)
    states = (
        s5_re_new.reshape(n, S5_GROUPS, S5_STATE),
        s5_im_new.reshape(n, S5_GROUPS, S5_STATE),
        ml_c_new.transpose(3, 0, 1, 2),
        opack[OROW_N].reshape(N_HEADS, HEAD_DIM, n).transpose(2, 0, 1),
        m_new[:, LANE_MI:LANE_MI + N_HEADS],
        gla_s_new.transpose(3, 0, 1, 2),
        rw_s_new.transpose(3, 0, 1, 2),
        proj[:, 0:RWKV_COLS],
        jnp.stack([ffn_buf[:, 1, :], ug], axis=1),
    )
    return x_new, states


def kernel(x_prompt, x_sample, state_s5_re, state_s5_im, state_mlstm_C, state_mlstm_n, state_mlstm_m, state_gla_S, state_rwkv_S, state_rwkv_shift, state_ffn_conv, norm_mix, w_in, s5_lam_re, s5_lam_im, s5_log_dt, s5_b_re, s5_b_im, s5_c_re, s5_c_im, s5_d, s5_w_glu, ml_gate_bias, ml_norm, gla_w_alpha, gla_b_alpha, gla_norm, rw_mu, rw_w0, rw_w2, rw_a0, rw_a2, rw_g2, rw_k_k, rw_k_a, rw_r_k, rw_norm, w_out, norm_ffn, ffn_w_up, ffn_conv_w, ffn_conv_b, ffn_w_down, norm_final):
    prm = dict(norm_mix=norm_mix, w_in=w_in, s5_lam_re=s5_lam_re, s5_lam_im=s5_lam_im, s5_log_dt=s5_log_dt,
               s5_b_re=s5_b_re, s5_b_im=s5_b_im, s5_c_re=s5_c_re, s5_c_im=s5_c_im, s5_d=s5_d,
               s5_w_glu=s5_w_glu, ml_gate_bias=ml_gate_bias, ml_norm=ml_norm, gla_w_alpha=gla_w_alpha,
               gla_b_alpha=gla_b_alpha, gla_norm=gla_norm, rw_mu=rw_mu, rw_w0=rw_w0, rw_w2=rw_w2,
               rw_a0=rw_a0, rw_a2=rw_a2, rw_g2=rw_g2, rw_k_k=rw_k_k, rw_k_a=rw_k_a, rw_r_k=rw_r_k,
               rw_norm=rw_norm, w_out=w_out, norm_ffn=norm_ffn, ffn_w_up=ffn_w_up, ffn_conv_w=ffn_conv_w,
               ffn_conv_b=ffn_conv_b, ffn_w_down=ffn_w_down)
    depth = w_in.shape[0]
    n_seq, t_len, d = x_prompt.shape
    n_smp = x_sample.shape[0]
    assert t_len % max(INPROJ_ROWS, FFN_ROWS, S5_TILE_STEPS, CHUNK_ML) == 0
    assert n_seq % max(SEQS_PER_STEP, SEQS_PER_STEP_ML) == 0 and x_sample.shape[1] == 1
    nf = _Param(norm_final[None, :])
    small_states = (state_s5_re, state_s5_im, state_mlstm_m, state_rwkv_shift, state_ffn_conv)
    lead = depth * N_HEADS
    to_lanes = lambda st: jnp.moveaxis(st, 1, -1)
    pair_states = (to_lanes(state_mlstm_C).reshape(lead, HEAD_DIM, HEAD_DIM, n_smp),
                   to_lanes(state_mlstm_n).reshape(lead, HEAD_DIM, n_smp),
                   to_lanes(state_gla_S).reshape(lead, HEAD_DIM, HEAD_DIM, n_smp),
                   to_lanes(state_rwkv_S).reshape(lead, HEAD_DIM, HEAD_DIM, n_smp))

    P = _prepare_params(prm)
    xp = x_prompt.reshape(n_seq * t_len, d)
    xs = x_sample.reshape(n_smp, d)
    p_states, s_states = [], []
    for l in range(depth):
        p = _layer_params(P, l)
        final = l == depth - 1
        xp, ps = _prompt_layer(xp, p, nf, n_seq, t_len, final)
        xs, ss = _sample_layer(xs, tuple(s[l] for s in small_states), pair_states, l, p, nf, final)
        p_states.append(ps)
        s_states.append(ss)
    new_p = tuple(jnp.stack([st[i] for st in p_states]) for i in range(9))
    new_s = tuple(jnp.stack([st[i] for st in s_states]) for i in range(9))
    return (xp.reshape(n_seq, t_len, d), xs.reshape(n_smp, 1, d)) + new_p + new_s
```

```python
import functools
import math

import jax
import jax.numpy as jnp
from jax import lax
from jax.experimental import pallas as pl
from jax.experimental.pallas import tpu as pltpu

F32 = jnp.float32
BF16 = jnp.bfloat16

LANES = 128
SUBLANES = 8
W_GROUP = 256
HEAD_DIM = 64
HEAD_SHIFT = 6
N_HEADS = 4
S5_GROUPS = 16
S5_STATE = 64
S5_WIDTH = S5_GROUPS * S5_STATE
GLA_RANK = 16
GLA_GATE_TEMP = 16.0
RWKV_W_RANK = 32
RWKV_A_RANK = 32
RWKV_G_RANK = 64
RWKV_COLS = 3 * W_GROUP + RWKV_W_RANK + RWKV_A_RANK + RWKV_G_RANK
RWKV_DECAY_SCALE = 0.6065306597126334
CONV_W = 3
EPS = 1e-6
GN_EPS = 64e-5
NEG_BIG = -1e30
EXP_CLAMP = 80.0

SMALL_W = LANES
OFF_SMALL = RWKV_COLS
OFF_MAIN = RWKV_COLS + SMALL_W
D_PROJ = OFF_MAIN + 9 * W_GROUP
LANE_MI = GLA_RANK
LANE_MF = GLA_RANK + N_HEADS
GATE_ROWS = 2 * N_HEADS
(BLK_U, BLK_MQ, BLK_MK, BLK_MV, BLK_MO, BLK_GQ, BLK_GK, BLK_GV, BLK_GG) = range(
    OFF_MAIN // W_GROUP, OFF_MAIN // W_GROUP + 9)
BLK_SMALL = OFF_SMALL // SMALL_W

CHUNK_ML = 128
CHUNK_RW = 64
SEQS_PER_STEP = 8
SEQS_PER_STEP_ML = 8
INPROJ_ROWS = 1024
FFN_ROWS = 1024
S5_TILE_STEPS = 128
VMEM_LIMIT = 56 * 1024 * 1024


def _cparams(*sem):
    return pltpu.CompilerParams(dimension_semantics=sem, vmem_limit_bytes=VMEM_LIMIT)


class _Param:
    def __init__(self, arr, *lead):
        self.arr, self.lead = arr, tuple(lead)

    @property
    def shape(self):
        return tuple(self.arr.shape[len(self.lead):])

    def spec(self, **kw):
        lead, tail = self.lead, (0,) * len(self.shape)
        return pl.BlockSpec((None,) * len(lead) + self.shape, lambda *grid_idx: lead + tail, **kw)


def _dot(a, b):
    return jnp.dot(a.astype(BF16), b.astype(BF16), preferred_element_type=F32)


def _dot_nt(a, b):
    return lax.dot_general(a.astype(BF16), b.astype(BF16), (((1,), (1,)), ((), ())),
                           preferred_element_type=F32)


def _dot_tn(a, b):
    return lax.dot_general(a.astype(BF16), b.astype(BF16), (((0,), (0,)), ((), ())),
                           preferred_element_type=F32)


def _split2(x):
    hi = x.astype(BF16)
    lo = (x - hi.astype(F32)).astype(BF16)
    return hi, lo


def _dot_exact_rhs(a01, x):
    hi, lo = _split2(x)
    f = lambda p: jnp.dot(a01, p, preferred_element_type=F32)
    return f(hi) + f(lo)


def _dot_exact_lhs(x, b01):
    hi, lo = _split2(x)
    f = lambda p: jnp.dot(p, b01, preferred_element_type=F32)
    return f(hi) + f(lo)


def _sigmoid(x):
    return 1.0 / (1.0 + jnp.exp(-x))


def _log_sigmoid(x):
    return jnp.minimum(x, 0.0) - jnp.log(1.0 + jnp.exp(-jnp.abs(x)))


def _gelu(x):
    return 0.5 * x * (1.0 + jnp.tanh(math.sqrt(2.0 / math.pi) * (x + 0.044715 * (x * x * x))))


def _iota(shape, dim):
    return lax.broadcasted_iota(jnp.int32, shape, dim)


def _head_ones():
    r = _iota((W_GROUP, W_GROUP), 0) >> HEAD_SHIFT
    c = _iota((W_GROUP, W_GROUP), 1) >> HEAD_SHIFT
    return jnp.where(r == c, 1.0, 0.0).astype(BF16)


def _head_sum(x, ones):
    return _dot_exact_lhs(x, ones)


def _head_mean(x, ones):
    return _dot(x, ones) * (1.0 / HEAD_DIM)


def _rmsnorm_rows(x, g):
    ms = jnp.mean(x * x, axis=-1, keepdims=True)
    return x * lax.rsqrt(ms + EPS) * g


def _tril_ones(n, strict=False):
    r = _iota((n, n), 0)
    c = _iota((n, n), 1)
    return jnp.where((r > c) if strict else (r >= c), 1.0, 0.0).astype(BF16)


def _triu_ones(n):
    r = _iota((n, n), 0)
    c = _iota((n, n), 1)
    return jnp.where(r <= c, 1.0, 0.0).astype(BF16)


def _store_head_blocks(out_ref, scr_ref, transpose=False):
    for g in range(scr_ref.shape[0]):
        m = scr_ref[g].T if transpose else scr_ref[g]
        for h in range(N_HEADS):
            lo, hi = h * HEAD_DIM, (h + 1) * HEAD_DIM
            out_ref[g, h] = m[lo:hi, lo:hi]


def _HEAD_BLOCKS_SPEC(G):
    return pl.BlockSpec((G, N_HEADS, HEAD_DIM, HEAD_DIM), lambda b, c: (b, 0, 0, 0))


def _inproj_kernel(x_ref, g_ref, w_ref, wg_ref, proj_ref, gt_ref, u_ref):
    hn = _rmsnorm_rows(x_ref[...], g_ref[...]).astype(BF16)
    proj = jnp.dot(hn, w_ref[...], preferred_element_type=F32)
    proj_ref[...] = proj
    u_ref[...] = proj[:, OFF_MAIN:OFF_MAIN + W_GROUP]
    gt_ref[0] = lax.dot_general(wg_ref[...], hn, (((1,), (1,)), ((), ())),
                                preferred_element_type=F32)


def _inproj(x2d, g, w, wg, n_seq, t_len, tm):
    n, d = x2d.shape
    tiles_per_seq = t_len // tm
    return pl.pallas_call(
        _inproj_kernel,
        grid=(n // tm,),
        in_specs=[
            pl.BlockSpec((tm, d), lambda i: (i, 0)),
            g.spec(), w.spec(pipeline_mode=pl.Buffered(1)), wg.spec(),
        ],
        out_specs=[
            pl.BlockSpec((tm, D_PROJ), lambda i: (i, 0)),
            pl.BlockSpec((1, GATE_ROWS, tm), lambda i: (i // tiles_per_seq, 0, i % tiles_per_seq)),
            pl.BlockSpec((tm, W_GROUP), lambda i: (i % tiles_per_seq, i // tiles_per_seq)),
        ],
        out_shape=[
            jax.ShapeDtypeStruct((n, D_PROJ), F32),
            jax.ShapeDtypeStruct((n_seq, GATE_ROWS, t_len), F32),
            jax.ShapeDtypeStruct((t_len, n_seq * W_GROUP), F32),
        ],
        compiler_params=_cparams("arbitrary"),
        name="inproj",
    )(x2d, g.arr, w.arr, wg.arr)


def _s5_kernel(u_ref, unext_ref, h0re_ref, h0im_ref, bmat_ref, lre_ref, lim_ref, cmat_ref, d_ref, wglu_ref,
               y_ref, hre_out, him_out, bu_scr, hs_scr, hre_scr, him_scr, *, bp, tt, wide):
    i = pl.program_id(0)
    nt = S5_WIDTH // LANES

    rows_of = (lambda ref: ref[...].reshape(tt, bp, W_GROUP).reshape(tt * bp, W_GROUP)) if wide else (
        lambda ref: ref[...])

    def project(u_tile, slot):
        bu = jnp.dot(u_tile.astype(BF16), bmat_ref[...], preferred_element_type=F32)
        for j in range(2 * nt):
            bu_scr[slot, j] = bu[:, j * LANES:(j + 1) * LANES]

    @pl.when(i == 0)
    def _():
        hre_scr[...] = h0re_ref[...]
        him_scr[...] = h0im_ref[...]
        project(rows_of(u_ref), 0)

    slot = lax.rem(i, 2)
    lre = [lre_ref[:, j * LANES:(j + 1) * LANES] for j in range(nt)]
    lim = [lim_ref[:, j * LANES:(j + 1) * LANES] for j in range(nt)]

    def step(t, carry):
        rows = pl.ds(pl.multiple_of(t * bp, bp), bp)
        new = []
        for j in range(nt):
            hre, him = carry[j], carry[nt + j]
            new.append((lre[j] * hre - lim[j] * him + bu_scr[slot, j, rows, :],
                        lre[j] * him + lim[j] * hre + bu_scr[slot, nt + j, rows, :]))
        out = tuple(x[0] for x in new) + tuple(x[1] for x in new)
        for j in range(2 * nt):
            hs_scr[j, rows, :] = out[j]
        return out

    carry = tuple(hre_scr[:, j * LANES:(j + 1) * LANES] for j in range(nt)) + tuple(
        him_scr[:, j * LANES:(j + 1) * LANES] for j in range(nt))
    carry = step(0, carry) if tt == 1 else lax.fori_loop(0, tt, step, carry)
    hre = jnp.concatenate(carry[0:nt], axis=1)
    him = jnp.concatenate(carry[nt:2 * nt], axis=1)
    hre_scr[...] = hre
    him_scr[...] = him
    hre_out[...] = hre
    him_out[...] = him

    project(rows_of(unext_ref), lax.rem(i + 1, 2))
    u = rows_of(u_ref)
    hs = jnp.concatenate([hs_scr[j] for j in range(2 * nt)], axis=1)
    y = jnp.dot(hs.astype(BF16), cmat_ref[...], preferred_element_type=F32) + d_ref[...] * u
    z = _gelu(y)
    y = z * _sigmoid(_dot(z, wglu_ref[...]))
    if wide:
        y = y.reshape(tt, bp, W_GROUP).reshape(tt, bp * W_GROUP)
    y_ref[...] = y.astype(y_ref.dtype)


def _s5(u, h0re, h0im, sp, bp, t_len, tt, wide):
    nsteps = t_len // tt
    full = lambda shape: pl.BlockSpec(shape, lambda i: (0,) * len(shape))
    blk = (tt, bp * W_GROUP) if wide else (tt * bp, W_GROUP)
    tile = pl.BlockSpec(blk, lambda i: (i, 0))
    next_tile = pl.BlockSpec(blk, lambda i: (jnp.minimum(i + 1, nsteps - 1), 0))
    return pl.pallas_call(
        functools.partial(_s5_kernel, bp=bp, tt=tt, wide=wide),
        grid=(nsteps,),
        in_specs=[
            tile, next_tile,
            full((bp, S5_WIDTH)), full((bp, S5_WIDTH)),
            sp["bmat"].spec(), sp["lre"].spec(), sp["lim"].spec(), sp["cmat"].spec(), sp["d"].spec(),
            sp["wglu"].spec(),
        ],
        out_specs=[
            tile,
            full((bp, S5_WIDTH)), full((bp, S5_WIDTH)),
        ],
        out_shape=[
            jax.ShapeDtypeStruct(u.shape, BF16),
            jax.ShapeDtypeStruct((bp, S5_WIDTH), F32),
            jax.ShapeDtypeStruct((bp, S5_WIDTH), F32),
        ],
        scratch_shapes=[
            pltpu.VMEM((2, 2 * S5_WIDTH // LANES, tt * bp, LANES), F32),
            pltpu.VMEM((2 * S5_WIDTH // LANES, tt * bp, LANES), F32),
            pltpu.VMEM((bp, S5_WIDTH), F32),
            pltpu.VMEM((bp, S5_WIDTH), F32),
        ],
        compiler_params=_cparams("arbitrary"),
        name="s5_scan",
    )(u, u, h0re, h0im, sp["bmat"].arr, sp["lre"].arr, sp["lim"].arr, sp["cmat"].arr, sp["d"].arr,
      sp["wglu"].arr)


def _run_interleaved(chains):
    live = list(chains)
    while live:
        still = []
        for ch in live:
            try:
                next(ch)
                still.append(ch)
            except StopIteration:
                pass
        live = still


def _mlstm_kernel(q_ref, k_ref, v_ref, og_ref, sm_ref, gt_ref, brow_ref, bcol_ref, norm_ref,
                  y_ref, c_out, n_out, m_out, c_scr, n_scr, m_scr, *, L, nc, G):
    c = pl.program_id(1)

    @pl.when(c == 0)
    def _():
        c_scr[...] = jnp.zeros_like(c_scr)
        n_scr[...] = jnp.zeros_like(n_scr)
        m_scr[...] = jnp.zeros_like(m_scr)

    ones = _head_ones()
    tril = _tril_ones(L)
    triu = _triu_ones(L)
    l_shift = L.bit_length() - 1
    src = _iota((LANES, N_HEADS * L), 0)
    to_scores = jnp.where(src == LANE_MI + (_iota((LANES, N_HEADS * L), 1) >> l_shift), 1.0, 0.0).astype(BF16)
    src = _iota((LANES, W_GROUP), 0)
    to_feats = jnp.where(src == LANE_MI + (_iota((LANES, W_GROUP), 1) >> HEAD_SHIFT), 1.0, 0.0).astype(BF16)
    seg = _iota((N_HEADS * L, W_GROUP), 0) >> l_shift
    score_sum = jnp.where(seg == (_iota((N_HEADS * L, W_GROUP), 1) >> HEAD_SHIFT), 1.0, 0.0).astype(BF16)
    causal = _iota((L, N_HEADS * L), 0) >= (_iota((L, N_HEADS * L), 1) & (L - 1))
    t_idx = _iota((L, LANES), 0)

    seqs = list(range(G))
    cat = lambda xs: jnp.concatenate(xs, axis=0)
    split = lambda x: [x[i * L:(i + 1) * L] for i in seqs]

    def running_max(x):
        shift = 1
        while shift < L:
            x = jnp.maximum(x, jnp.where(t_idx >= shift, pltpu.roll(x, shift, 0), NEG_BIG))
            shift *= 2
        return x

    q = [q_ref[i] for i in seqs]
    k = [k_ref[i] * (HEAD_DIM ** -0.5) for i in seqs]
    v = [v_ref[i] for i in seqs]
    C = [c_scr[i] for i in seqs]
    n = [n_scr[i, 0:1, :] for i in seqs]
    m0 = [m_scr[i, 0:1, :] for i in seqs]
    pre = [sm_ref[i] + brow_ref[...] for i in seqs]
    gtb = [gt_ref[i] + bcol_ref[...] for i in seqs]
    bc_col = _each(lambda x: _dot_exact_rhs(tril, _log_sigmoid(x)), pre)
    bc_rows = _dot_exact_lhs(_log_sigmoid(cat(gtb)), triu)
    qC = _each(_dot_nt, q, C)
    qk = _each(lambda q_, k_: _dot_nt(q_, _stack_heads(k_)), q, k)
    qn = _head_sum(cat(_each(lambda q_, n_: q_ * n_, q, n)), ones)

    b_col = _each(lambda x: pltpu.roll(x, LANES - N_HEADS, 1), bc_col)
    r_col = _each(lambda p_, b_: p_ - b_, pre, b_col)
    mu = _each(lambda r_, m_: jnp.maximum(running_max(r_), m_), r_col, m0)
    mu_last = _each(lambda x: x[L - 1:L, :], mu)
    mu_scores = split(_dot_exact_lhs(cat(mu), to_scores))

    def weights(i):
        f0 = GATE_ROWS * i + N_HEADS
        r_rows = gtb[i][0:N_HEADS, :] - bc_rows[f0:f0 + N_HEADS, :]
        r_all = jnp.concatenate([r_rows[h:h + 1, :] for h in range(N_HEADS)], axis=1)
        return jnp.exp(jnp.where(causal, r_all - mu_scores[i], NEG_BIG)) * qk[i]

    ws = [weights(i) for i in seqs]
    wsv = _each(lambda w_, v_: _dot(w_, _stack_heads(v_)), ws, v)
    w0 = _dot(cat(_each(lambda m_, u_: jnp.exp(m_ - u_), m0, mu)), to_feats)
    m_row = _dot_exact_lhs(cat(_each(lambda b_, u_: b_ + u_, b_col, mu)), to_feats)
    den = w0 * qn + _dot(cat(ws), score_sum)
    hc = (w0 * cat(qC) + cat(wsv)) / jnp.maximum(jnp.abs(den), jnp.exp(-m_row))

    wl = split(_dot(cat(_each(lambda r_, u_: jnp.exp(r_ - u_), r_col, mu_last)), to_feats))
    w0l_rows = _each(lambda m_, u_: jnp.exp(m_ - u_), m0, mu_last)
    w0l_rows = w0l_rows + [jnp.zeros((1, LANES), F32)] * (-G % SUBLANES)
    w0l = _dot_exact_lhs(cat(w0l_rows), to_feats)
    kw = _each(lambda k_, w_: k_ * w_, k, wl)
    upd = _each(_dot_tn, v, kw)
    for i in seqs:
        c_scr[i] = w0l[i:i + 1, :] * C[i] + ones.astype(F32) * upd[i]
        n_scr[i, 0:1, :] = w0l[i:i + 1, :] * n[i] + jnp.sum(kw[i], axis=0, keepdims=True)
        m_scr[i, 0:1, :] = b_col[i][L - 1:L, :] + mu_last[i]

    ms = _head_mean(hc * hc, ones)
    y = split(hc * lax.rsqrt(ms + EPS) * norm_ref[...])
    for i in seqs:
        y_ref[i] = (y[i] * _sigmoid(og_ref[i])).astype(y_ref.dtype)

    @pl.when(c == nc - 1)
    def _():
        _store_head_blocks(c_out, c_scr)
        n_out[...] = jnp.broadcast_to(n_scr[:, 0:1, :], n_out.shape)
        m_out[...] = m_scr[...]


def _seq_spec(G, L, blk, width=W_GROUP):
    return pl.BlockSpec((G, L, width), lambda b, c: (b, c, blk))


def _mlstm(proj, gt, p, n_seq, t_len):
    L = CHUNK_ML
    G = SEQS_PER_STEP_ML
    nc = t_len // L
    per_seq = lambda rows, width: pl.BlockSpec((G, rows, width), lambda b, c: (b, 0, 0))
    proj3 = proj.reshape(n_seq, t_len, D_PROJ)
    y, c_new, n_new, m_new = pl.pallas_call(
        functools.partial(_mlstm_kernel, L=L, nc=nc, G=G),
        grid=(n_seq // G, nc),
        in_specs=[
            _seq_spec(G, L, BLK_MQ), _seq_spec(G, L, BLK_MK), _seq_spec(G, L, BLK_MV),
            _seq_spec(G, L, BLK_MO), _seq_spec(G, L, BLK_SMALL, SMALL_W),
            pl.BlockSpec((G, GATE_ROWS, L), lambda b, c: (b, 0, c)),
            p["ml_brow"].spec(), p["ml_bcol"].spec(), p["ml_norm"].spec(),
        ],
        out_specs=[
            pl.BlockSpec((G, L, W_GROUP), lambda b, c: (b, c, 0)),
            _HEAD_BLOCKS_SPEC(G), per_seq(SUBLANES, W_GROUP), per_seq(SUBLANES, LANES),
        ],
        out_shape=[
            jax.ShapeDtypeStruct((n_seq, t_len, W_GROUP), BF16),
            jax.ShapeDtypeStruct((n_seq, N_HEADS, HEAD_DIM, HEAD_DIM), F32),
            jax.ShapeDtypeStruct((n_seq, SUBLANES, W_GROUP), F32),
            jax.ShapeDtypeStruct((n_seq, SUBLANES, LANES), F32),
        ],
        scratch_shapes=[
            pltpu.VMEM((G, W_GROUP, W_GROUP), F32),
            pltpu.VMEM((G, SUBLANES, W_GROUP), F32),
            pltpu.VMEM((G, SUBLANES, LANES), F32),
        ],
        compiler_params=_cparams("arbitrary", "arbitrary"),
        name="mlstm_chunk",
    )(proj3, proj3, proj3, proj3, proj3, gt, p["ml_brow"].arr, p["ml_bcol"].arr, p["ml_norm"].arr)
    return y.reshape(n_seq * t_len, W_GROUP), c_new, n_new, m_new


def _gla_kernel(q_ref, k_ref, v_ref, gg_ref, sm_ref, wa_ref, ba_ref, norm_ref,
                y_ref, s_out, s_scr, *, L, nc, G):
    c = pl.program_id(1)

    @pl.when(c == 0)
    def _():
        s_scr[...] = jnp.zeros_like(s_scr)

    ones = _head_ones()
    tril = _tril_ones(L)
    causal = _iota((L, N_HEADS * L), 0) >= (_iota((L, N_HEADS * L), 1) & (L - 1))

    def chain(gi):
        q = q_ref[gi] * (HEAD_DIM ** -0.5)
        k = k_ref[gi]
        v = v_ref[gi]
        la = _log_sigmoid(_dot(sm_ref[gi], wa_ref[...]) + ba_ref[...]) * (1.0 / GLA_GATE_TEMP)
        yield
        bc = _dot_exact_rhs(tril, la)
        yield
        ref = bc[L // 2 - 1:L // 2, :]
        qr = q * jnp.exp(jnp.minimum(bc - ref, EXP_CLAMP))
        kh = k * jnp.exp(jnp.minimum(ref - bc, EXP_CLAMP))
        ST = s_scr[gi]
        att = jnp.where(causal, _dot_nt(qr, _stack_heads(kh)), 0.0)
        from_state = _dot_nt(q * jnp.exp(bc), ST)
        yield
        o = from_state + _dot(att, _stack_heads(v))
        b_last = bc[L - 1:L, :]
        kbar = k * jnp.exp(b_last - bc)
        s_scr[gi] = ST * jnp.exp(b_last) + ones.astype(F32) * _dot_tn(v, kbar)
        yield
        ms = _head_mean(o * o, ones)
        gg = gg_ref[gi]
        y = o * lax.rsqrt(ms + EPS) * norm_ref[...] * (gg * _sigmoid(gg))
        y_ref[gi] = y.astype(y_ref.dtype)

    _run_interleaved([chain(gi) for gi in range(G)])

    @pl.when(c == nc - 1)
    def _():
        _store_head_blocks(s_out, s_scr, transpose=True)


def _gla(proj, p, n_seq, t_len):
    L = CHUNK_ML
    G = SEQS_PER_STEP_ML
    nc = t_len // L
    proj3 = proj.reshape(n_seq, t_len, D_PROJ)
    y, s_new = pl.pallas_call(
        functools.partial(_gla_kernel, L=L, nc=nc, G=G),
        grid=(n_seq // G, nc),
        in_specs=[
            _seq_spec(G, L, BLK_GQ), _seq_spec(G, L, BLK_GK), _seq_spec(G, L, BLK_GV),
            _seq_spec(G, L, BLK_GG), _seq_spec(G, L, BLK_SMALL, SMALL_W),
            p["gla_wa"].spec(), p["gla_ba"].spec(), p["gla_norm"].spec(),
        ],
        out_specs=[
            pl.BlockSpec((G, L, W_GROUP), lambda b, c: (b, c, 0)),
            _HEAD_BLOCKS_SPEC(G),
        ],
        out_shape=[
            jax.ShapeDtypeStruct((n_seq, t_len, W_GROUP), BF16),
            jax.ShapeDtypeStruct((n_seq, N_HEADS, HEAD_DIM, HEAD_DIM), F32),
        ],
        scratch_shapes=[pltpu.VMEM((G, W_GROUP, W_GROUP), F32)],
        compiler_params=_cparams("arbitrary", "arbitrary"),
        name="gla_chunk",
    )(proj3, proj3, proj3, proj3, proj3, p["gla_wa"].arr, p["gla_ba"].arr, p["gla_norm"].arr)
    return y.reshape(n_seq * t_len, W_GROUP), s_new


def _rwkv_vectors(rc, prev, p_mu, p_w0, p_a0, p_kk, p_ka, p_rk, w2, a2, g2, ones):
    xm = rc + p_mu * (prev - rc)
    rr = xm[:, 0:W_GROUP]
    rk = xm[:, W_GROUP:2 * W_GROUP]
    rv = xm[:, 2 * W_GROUP:3 * W_GROUP]
    tail = xm[:, 3 * W_GROUP:RWKV_COLS]
    lw = -RWKV_DECAY_SCALE * _sigmoid(p_w0 + _dot(jnp.tanh(tail), w2))
    a = _sigmoid(p_a0 + _dot(tail, a2))
    g = _dot(_sigmoid(tail), g2)
    kk = rk * p_kk
    kk = kk * lax.rsqrt(jnp.maximum(_head_sum(kk * kk, ones), 1e-24))
    kt = rk * (1.0 + (a - 1.0) * p_ka)
    bonus = _head_sum(rr * kt * p_rk, ones) * rv
    return rr, lw, kt, rv, kk, a, g, bonus


def _head_layernorm(o, g, ones):
    mu = _head_mean(o, ones)
    oc = o - mu
    var = _head_mean(oc * oc, ones)
    return oc * lax.rsqrt(var + GN_EPS) * g


def _stack_heads(x):
    xb = x.astype(BF16)
    lane_head = _iota((1, W_GROUP), 1) >> HEAD_SHIFT
    return jnp.concatenate([jnp.where(lane_head == h, xb, jnp.zeros_like(xb)) for h in range(N_HEADS)],
                           axis=0)


def _each(f, *seqs):
    return [f(*args) for args in zip(*seqs)]


def _block_mm(x_list, y_list):
    return _each(lambda x, y: _dot(x, _stack_heads(y)), x_list, y_list)


def _unit_lower_inverse(a_list, L):
    t_idx = _iota((L, N_HEADS * L), 0)
    s_idx = _iota((L, N_HEADS * L), 1) & (L - 1)
    eye = jnp.where(t_idx == s_idx, 1.0, 0.0).astype(F32)
    in16 = (t_idx >> 4) == (s_idx >> 4)
    in32 = ((t_idx >> 5) == (s_idx >> 5)) & ((t_idx >> 4) > (s_idx >> 4))
    in64 = (t_idx >> 5) > (s_idx >> 5)
    x16 = _each(lambda a: jnp.where(in16, -a, 0.0), a_list)
    x2 = _block_mm(x16, x16)
    x4 = _block_mm(x2, x2)
    x8 = _block_mm(x4, x4)
    t = _each(lambda x: eye + x, x16)
    for xp in (x2, x4, x8):
        t = _each(lambda t_, d: t_ + d, t, _block_mm(t, xp))
    for sel in (in32, in64):
        w = _block_mm(_each(lambda a: jnp.where(sel, a, 0.0), a_list), t)
        t = _each(lambda t_, d: t_ - d, t, _block_mm(t, w))
    return t


def _rwkv_chunks(rcs, prev_rows, states, prm, ones, L):
    (mu, w0, a0, pkk, pka, prk, w2, a2, g2, norm) = prm
    first_row = _iota((L, RWKV_COLS), 0) == 0
    prevs = _each(lambda rc, pr: jnp.where(first_row, pr, pltpu.roll(rc, 1, 0)), rcs, prev_rows)
    n_seq = len(rcs)
    stacked = _rwkv_vectors(jnp.concatenate(rcs, axis=0), jnp.concatenate(prevs, axis=0),
                            mu, w0, a0, pkk, pka, prk, w2, a2, g2, ones)
    r, lw, kx, v, kk, a, g, bonus = [[x[i * L:(i + 1) * L] for i in range(n_seq)] for x in stacked]

    tril = _tril_ones(L)
    t_idx = _iota((L, N_HEADS * L), 0)
    s_idx = _iota((L, N_HEADS * L), 1) & (L - 1)
    strict = t_idx > s_idx
    incl = t_idx >= s_idx

    def core(r, lw, kx, v, kk, a, states):
        lc = _each(lambda x: _dot_exact_rhs(tril, x), lw)
        akk = _each(lambda a_, k_: a_ * k_, a, kk)
        e_neg = _each(lambda x: jnp.exp(-x), lc)
        l_last = _each(lambda x: x[L - 1:L, :], lc)
        e_end = _each(lambda ll, x: jnp.exp(ll - x), l_last, lc)
        kr = _each(lambda k_, r_, c_, w_: jnp.concatenate([k_ * jnp.exp(c_ - w_), r_ * jnp.exp(c_)], axis=0),
                   kk, r, lc, lw)
        a_hat = _each(lambda x, e: _stack_heads(x * e), akk, e_neg)
        k_hat = _each(lambda x, e: _stack_heads(x * e), kx, e_neg)
        v_bd = _each(_stack_heads, v)

        pa = _each(_dot_nt, kr, a_hat)
        pk = _each(_dot_nt, kr, k_hat)
        a_ua = _each(lambda p_: jnp.where(strict, p_[0:L], 0.0), pa)
        b_ra = _each(lambda p_: jnp.where(incl, p_[L:2 * L], 0.0), pa)
        ab_k = _each(lambda p_: jnp.concatenate([jnp.where(strict, p_[0:L], 0.0),
                                                 jnp.where(incl, p_[L:2 * L], 0.0)], axis=0), pk)
        T = _unit_lower_inverse(a_ua, L)

        from_state = _each(_dot_nt, kr, states)
        from_v = _each(_dot, ab_k, v_bd)
        rhs = _each(lambda s_, v_: s_[0:L] + v_[0:L], from_state, from_v)
        U = _block_mm(T, rhs)
        corr = _block_mm(b_ra, U)
        o = _each(lambda s_, v_, c_: s_[L:2 * L] + v_[L:2 * L] - c_, from_state, from_v, corr)
        upd = _each(lambda v_, u_, k_, a_, e: _dot_tn(jnp.concatenate([v_, u_], axis=0),
                                                      jnp.concatenate([k_ * e, -(a_ * e)], axis=0)),
                    v, U, kx, akk, e_end)
        s_new = _each(lambda s_, ll, d: s_ * jnp.exp(ll) + ones.astype(F32) * d, states, l_last, upd)
        return o, s_new

    o, s_new = core(r, lw, kx, v, kk, a, states)
    y_all = ((_head_layernorm(jnp.concatenate(o, axis=0), norm, ones) + stacked[7]) * stacked[6])
    return [y_all[i * L:(i + 1) * L] for i in range(n_seq)], s_new


def _rwkv_kernel(rc_ref, mu_ref, w0_ref, a0_ref, kk_ref, ka_ref, rk_ref, w2_ref, a2_ref, g2_ref,
                 norm_ref, y_ref, s_out, shift_out, s_scr, prev_scr, *, L, nc, G):
    c = pl.program_id(1)

    @pl.when(c == 0)
    def _():
        s_scr[...] = jnp.zeros_like(s_scr)
        prev_scr[...] = jnp.zeros_like(prev_scr)

    ones = _head_ones()
    prm = (mu_ref[...], w0_ref[...], a0_ref[...], kk_ref[...], ka_ref[...], rk_ref[...],
           w2_ref[...], a2_ref[...], g2_ref[...], norm_ref[...])
    rcs = [rc_ref[gi] for gi in range(G)]
    ys, s_new = _rwkv_chunks(rcs, [prev_scr[gi, 0:1, :] for gi in range(G)],
                             [s_scr[gi] for gi in range(G)], prm, ones, L)
    for gi in range(G):
        prev_scr[gi, 0:1, :] = rcs[gi][L - 1:L, :]
        s_scr[gi] = s_new[gi]
        y_ref[gi] = ys[gi].astype(y_ref.dtype)

    @pl.when(c == nc - 1)
    def _():
        _store_head_blocks(s_out, s_scr)
        shift_out[...] = jnp.broadcast_to(prev_scr[:, 0:1, :], shift_out.shape)


def _rwkv(proj, p, n_seq, t_len):
    L = CHUNK_RW
    G = SEQS_PER_STEP
    nc = t_len // L
    names = ("rw_mu", "rw_w0", "rw_a0", "rw_kk", "rw_ka", "rw_rk", "rw_w2", "rw_a2", "rw_g2", "rw_norm")
    y, s_new, shift = pl.pallas_call(
        functools.partial(_rwkv_kernel, L=L, nc=nc, G=G),
        grid=(n_seq // G, nc),
        in_specs=[
            pl.BlockSpec((G, L, RWKV_COLS), lambda b, c: (b, c, 0)),
        ] + [p[k].spec() for k in names],
        out_specs=[
            pl.BlockSpec((G, L, W_GROUP), lambda b, c: (b, c, 0)),
            _HEAD_BLOCKS_SPEC(G),
            pl.BlockSpec((G, SUBLANES, RWKV_COLS), lambda b, c: (b, 0, 0)),
        ],
        out_shape=[
            jax.ShapeDtypeStruct((n_seq, t_len, W_GROUP), BF16),
            jax.ShapeDtypeStruct((n_seq, N_HEADS, HEAD_DIM, HEAD_DIM), F32),
            jax.ShapeDtypeStruct((n_seq, SUBLANES, RWKV_COLS), F32),
        ],
        scratch_shapes=[
            pltpu.VMEM((G, W_GROUP, W_GROUP), F32),
            pltpu.VMEM((G, SUBLANES, RWKV_COLS), F32),
        ],
        compiler_params=_cparams("arbitrary", "arbitrary"),
        name="rwkv_chunk",
    )(proj.reshape(n_seq, t_len, D_PROJ), *[p[k].arr for k in names])
    return y.reshape(n_seq * t_len, W_GROUP), s_new, shift


FF_CHUNK = 256


def _mix_residual(x_ref, ys, wout_ref):
    acc = x_ref[...]
    for j, y in enumerate(ys):
        acc = acc + jnp.dot(y.astype(BF16), wout_ref[j * W_GROUP:(j + 1) * W_GROUP, :],
                            preferred_element_type=F32)
    return acc


def _ffn_body(x1, nrm_ref, wup_ref, cw_ref, cb_ref, wdn_ref, prev_rows, d_ff, act_scr):
    h2 = _rmsnorm_rows(x1, nrm_ref[...]).astype(BF16)
    for j in range(d_ff // FF_CHUNK):
        lo, hi = j * FF_CHUNK, (j + 1) * FF_CHUNK
        ug = jnp.dot(h2, wup_ref[:, lo:hi], preferred_element_type=F32)
        uv = jnp.dot(h2, wup_ref[:, d_ff + lo:d_ff + hi], preferred_element_type=F32)
        p2, p1 = prev_rows(j, ug)
        conv = (cb_ref[:, lo:hi] + cw_ref[0:1, lo:hi] * p2 + cw_ref[1:2, lo:hi] * p1
                + cw_ref[2:3, lo:hi] * ug)
        act_scr[:, lo:hi] = (_gelu(conv) * uv).astype(BF16)
    return x1 + jnp.dot(act_scr[...], wdn_ref[...], preferred_element_type=F32)


def _ffn_seq_kernel(x_ref, y0_ref, y1_ref, y2_ref, y3_ref, wout_ref, nrm_ref, wup_ref, cw_ref, cb_ref,
                    wdn_ref, nf_ref, o_ref, buf_out, carry_scr, act_scr, *, tm, d_ff, final):
    t = pl.program_id(1)

    @pl.when(t == 0)
    def _():
        carry_scr[...] = jnp.zeros_like(carry_scr)

    x1 = _mix_residual(x_ref, (y0_ref[...], y1_ref[...], y2_ref[...], y3_ref[...]), wout_ref)
    row = _iota((tm, FF_CHUNK), 0)

    def prev_rows(j, ug):
        lo, hi = j * FF_CHUNK, (j + 1) * FF_CHUNK
        c0 = carry_scr[0:1, lo:hi]
        c1 = carry_scr[1:2, lo:hi]
        p1 = jnp.where(row == 0, c1, pltpu.roll(ug, 1, 0))
        p2 = jnp.where(row == 0, c0, jnp.where(row == 1, c1, pltpu.roll(ug, 2, 0)))
        carry_scr[0:2, lo:hi] = ug[tm - 2:tm, :]
        return p2, p1

    out = _ffn_body(x1, nrm_ref, wup_ref, cw_ref, cb_ref, wdn_ref, prev_rows, d_ff, act_scr)
    if final:
        out = _rmsnorm_rows(out, nf_ref[...])
    o_ref[...] = out
    buf_out[0] = carry_scr[...]


def _ffn_step_kernel(x_ref, y0_ref, y1_ref, y2_ref, y3_ref, wout_ref, nrm_ref, wup_ref, cw_ref, cb_ref,
                     wdn_ref, nf_ref, p2_ref, p1_ref, o_ref, ug_out, act_scr, *, d_ff, final):
    x1 = _mix_residual(x_ref, (y0_ref[...], y1_ref[...], y2_ref[...], y3_ref[...]), wout_ref)

    def prev_rows(j, ug):
        lo, hi = j * FF_CHUNK, (j + 1) * FF_CHUNK
        ug_out[:, lo:hi] = ug
        return p2_ref[:, lo:hi], p1_ref[:, lo:hi]

    out = _ffn_body(x1, nrm_ref, wup_ref, cw_ref, cb_ref, wdn_ref, prev_rows, d_ff, act_scr)
    if final:
        out = _rmsnorm_rows(out, nf_ref[...])
    o_ref[...] = out


def _ffn_seq(x2d, ys, p, norm_final, n_seq, t_len, tm, final):
    n, d = x2d.shape
    d_ff = p["ffn_wdn"].shape[0]
    nt = t_len // tm
    rows = lambda w: pl.BlockSpec((tm, w), lambda b, t: (b * nt + t, 0))
    once = dict(pipeline_mode=pl.Buffered(1))
    return pl.pallas_call(
        functools.partial(_ffn_seq_kernel, tm=tm, d_ff=d_ff, final=final),
        grid=(n_seq, nt),
        in_specs=[rows(d), pl.BlockSpec((tm, W_GROUP), lambda b, t: (t, b))] + [rows(W_GROUP)] * 3 + [
            p["w_out"].spec(**once), p["norm_ffn"].spec(), p["ffn_wup"].spec(**once), p["ffn_cw"].spec(),
            p["ffn_cb"].spec(), p["ffn_wdn"].spec(**once), norm_final.spec()],
        out_specs=[rows(d), pl.BlockSpec((1, CONV_W - 1, d_ff), lambda b, t: (b, 0, 0))],
        out_shape=[jax.ShapeDtypeStruct((n, d), F32),
                   jax.ShapeDtypeStruct((n_seq, CONV_W - 1, d_ff), F32)],
        scratch_shapes=[pltpu.VMEM((CONV_W - 1, d_ff), F32), pltpu.VMEM((tm, d_ff), BF16)],
        compiler_params=_cparams("arbitrary", "arbitrary"),
        name="wout_ffn_seq",
    )(x2d, *ys, p["w_out"].arr, p["norm_ffn"].arr, p["ffn_wup"].arr, p["ffn_cw"].arr, p["ffn_cb"].arr,
      p["ffn_wdn"].arr, norm_final.arr)


def _ffn_step(x2d, ys, p, norm_final, prev2, prev1, final):
    n, d = x2d.shape
    d_ff = p["ffn_wdn"].shape[0]
    full = lambda shape: pl.BlockSpec(shape, lambda i: (0,) * len(shape))
    return pl.pallas_call(
        functools.partial(_ffn_step_kernel, d_ff=d_ff, final=final),
        grid=(1,),
        in_specs=[full((n, d))] + [full((n, W_GROUP))] * 4 + [
            p["w_out"].spec(), p["norm_ffn"].spec(), p["ffn_wup"].spec(), p["ffn_cw"].spec(),
            p["ffn_cb"].spec(), p["ffn_wdn"].spec(), norm_final.spec(), full((n, d_ff)), full((n, d_ff))],
        out_specs=[full((n, d)), full((n, d_ff))],
        out_shape=[jax.ShapeDtypeStruct((n, d), F32), jax.ShapeDtypeStruct((n, d_ff), F32)],
        scratch_shapes=[pltpu.VMEM((n, d_ff), BF16)],
        compiler_params=_cparams("arbitrary"),
        name="wout_ffn_step",
    )(x2d, *ys, p["w_out"].arr, p["norm_ffn"].arr, p["ffn_wup"].arr, p["ffn_cw"].arr, p["ffn_cb"].arr,
      p["ffn_wdn"].arr, norm_final.arr, prev2, prev1)


def _step_vec_kernel(proj_ref, shift_ref, m_ref, brow_ref, wa_ref, ba_ref, mu_ref, w0_ref, a0_ref,
                     kk_ref, ka_ref, rk_ref, w2_ref, a2_ref, g2_ref,
                     pack_out, m_out, g_out, bonus_out):
    ones = _head_ones()
    sm = proj_ref[:, OFF_SMALL:OFF_SMALL + SMALL_W]
    pre = sm + brow_ref[...]
    logf = pltpu.roll(_log_sigmoid(pre), LANES - N_HEADS, 1)
    m_old = m_ref[...]
    m_new = jnp.maximum(logf + m_old, pre)
    m_out[...] = m_new
    lane = _iota((SMALL_W, W_GROUP), 0)
    spread_mat = jnp.where(lane == LANE_MI + (_iota((SMALL_W, W_GROUP), 1) >> HEAD_SHIFT), 1.0, 0.0).astype(BF16)
    gate_lanes = (_iota(pre.shape, 1) >= LANE_MI) & (_iota(pre.shape, 1) < LANE_MI + N_HEADS)
    spread = lambda x: _dot_exact_lhs(jnp.where(gate_lanes, x, 0.0), spread_mat)
    la = _log_sigmoid(_dot(sm, wa_ref[...]) + ba_ref[...]) * (1.0 / GLA_GATE_TEMP)
    rc = proj_ref[:, 0:RWKV_COLS]
    rr, lw, kt, rv, kk, a, g, bonus = _rwkv_vectors(
        rc, shift_ref[...], mu_ref[...], w0_ref[...], a0_ref[...], kk_ref[...], ka_ref[...],
        rk_ref[...], w2_ref[...], a2_ref[...], g2_ref[...], ones)
    main = lambda blk: proj_ref[:, blk * W_GROUP:(blk + 1) * W_GROUP]
    rows = {
        ROW_MQ: main(BLK_MQ), ROW_MK: main(BLK_MK), ROW_MV: main(BLK_MV),
        ROW_GQ: main(BLK_GQ), ROW_GK: main(BLK_GK), ROW_GV: main(BLK_GV), ROW_GA: jnp.exp(la),
        ROW_KK: kk, ROW_AKK: a * kk, ROW_W: jnp.exp(lw), ROW_KT: kt, ROW_RR: rr, ROW_RV: rv,
        ROW_IW: spread(jnp.exp(pre - m_new)), ROW_FW: spread(jnp.exp(logf + m_old - m_new)),
        ROW_M: spread(m_new),
    }
    for i, val in rows.items():
        pack_out[i] = val.T
    g_out[...] = g
    bonus_out[...] = bonus


def _step_vectors(proj, shift, m_pad, p):
    n = proj.shape[0]
    full = lambda a: pl.BlockSpec(a.shape, lambda i: (0,) * a.ndim)
    data = (proj, shift, m_pad)
    prms = [p[k] for k in ("ml_brow", "gla_wa", "gla_ba", "rw_mu", "rw_w0", "rw_a0", "rw_kk", "rw_ka",
                           "rw_rk", "rw_w2", "rw_a2", "rw_g2")]
    wide = jax.ShapeDtypeStruct((n, W_GROUP), F32)
    outs = [jax.ShapeDtypeStruct((N_PACK, W_GROUP, n), F32), jax.ShapeDtypeStruct((n, SMALL_W), F32),
            wide, wide]
    return pl.pallas_call(
        _step_vec_kernel,
        grid=(1,),
        in_specs=[full(a) for a in data] + [q.spec() for q in prms],
        out_specs=[pl.BlockSpec(o.shape, lambda i, nd=len(o.shape): (0,) * nd) for o in outs],
        out_shape=outs,
        compiler_params=_cparams("arbitrary"),
        name="step_vectors",
    )(*data, *[q.arr for q in prms])


(ROW_MQ, ROW_MK, ROW_MV, ROW_GQ, ROW_GK, ROW_GV, ROW_GA, ROW_KK, ROW_AKK, ROW_W, ROW_KT, ROW_RR, ROW_RV,
 ROW_IW, ROW_FW, ROW_M) = range(16)
N_PACK = 16
(OROW_ML, OROW_GLA, OROW_RW, OROW_N) = range(4)
N_OPACK = 4
STEP_UNROLL = 4


def _heads_step_kernel(pack_ref, c_ref, n_ref, gs_ref, rs_ref, c_out, gs_out, rs_out, opack_out):
    vec = lambda i: pack_ref[i]
    one = lambda i, j: pack_ref[i, pl.ds(j, 1), :]
    colsum = lambda x: jnp.sum(x, axis=0, keepdims=True)

    k_ml = vec(ROW_MK) * (HEAD_DIM ** -0.5)
    q_ml = vec(ROW_MQ)
    iw = pack_ref[ROW_IW, 0:1, :]
    fw = pack_ref[ROW_FW, 0:1, :]
    n_new = fw * n_ref[...] + iw * k_ml
    opack_out[OROW_N] = n_new
    r_den = 1.0 / jnp.maximum(jnp.abs(colsum(n_new * q_ml)), jnp.exp(-pack_ref[ROW_M, 0:1, :]))

    def ml_body(v, carry):
        c_new = fw * c_ref[v] + (iw * one(ROW_MV, v)) * k_ml
        c_out[v] = c_new
        opack_out[OROW_ML, pl.ds(v, 1), :] = colsum(c_new * q_ml) * r_den
        return carry

    lax.fori_loop(0, HEAD_DIM, ml_body, 0, unroll=STEP_UNROLL)

    v_gla = vec(ROW_GV)

    def gla_body(k, acc):
        s_new = one(ROW_GA, k) * gs_ref[k] + one(ROW_GK, k) * v_gla
        gs_out[k] = s_new
        return acc + (one(ROW_GQ, k) * (HEAD_DIM ** -0.5)) * s_new

    opack_out[OROW_GLA] = lax.fori_loop(0, HEAD_DIM, gla_body, jnp.zeros_like(v_gla), unroll=STEP_UNROLL)

    kk, akk, w, kt, rr = vec(ROW_KK), vec(ROW_AKK), vec(ROW_W), vec(ROW_KT), vec(ROW_RR)

    def rw_body(v, carry):
        s = rs_ref[v]
        s_new = s * w - colsum(s * kk) * akk + one(ROW_RV, v) * kt
        rs_out[v] = s_new
        opack_out[OROW_RW, pl.ds(v, 1), :] = colsum(s_new * rr)
        return carry

    lax.fori_loop(0, HEAD_DIM, rw_body, 0, unroll=STEP_UNROLL)


def _heads_step(pack, ml_c, ml_n, gla_s, rw_s, layer):
    n = pack.shape[-1]
    sq = (HEAD_DIM, HEAD_DIM, n)
    head = lambda lead: pl.BlockSpec((lead, HEAD_DIM, n), lambda h: (0, h, 0))
    at_layer = lambda shape: pl.BlockSpec((None,) + shape, lambda h: (layer * N_HEADS + h,) + (0,) * len(shape))
    out_state = pl.BlockSpec((None,) + sq, lambda h: (h, 0, 0, 0))
    state = jax.ShapeDtypeStruct((N_HEADS,) + sq, F32)
    return pl.pallas_call(
        _heads_step_kernel,
        grid=(N_HEADS,),
        in_specs=[head(N_PACK), at_layer(sq), at_layer((HEAD_DIM, n)), at_layer(sq), at_layer(sq)],
        out_specs=[out_state, out_state, out_state, head(N_OPACK)],
        out_shape=[state, state, state, jax.ShapeDtypeStruct((N_OPACK, W_GROUP, n), F32)],
        compiler_params=_cparams("arbitrary"),
        name="heads_step",
    )(pack, ml_c, ml_n, gla_s, rw_s)


def _step_post_kernel(hml_ref, mo_ref, ogla_ref, gg_ref, orw_ref, g_ref, bonus_ref,
                      mln_ref, gln_ref, rwn_ref, yml_out, ygla_out, yrw_out):
    ones = _head_ones()
    hml = hml_ref[...].T
    ms = _head_mean(hml * hml, ones)
    yml_out[...] = (hml * lax.rsqrt(ms + EPS) * mln_ref[...] * _sigmoid(mo_ref[...])).astype(BF16)
    og = ogla_ref[...].T
    ms = _head_mean(og * og, ones)
    gg = gg_ref[...]
    ygla_out[...] = (og * lax.rsqrt(ms + EPS) * gln_ref[...] * (gg * _sigmoid(gg))).astype(BF16)
    yrw_out[...] = ((_head_layernorm(orw_ref[...].T, rwn_ref[...], ones) + bonus_ref[...])
                    * g_ref[...]).astype(BF16)


def _step_post(hml, mo, ogla, gg, orw, g, bonus, p):
    n = mo.shape[0]
    data = (hml, mo, ogla, gg, orw, g, bonus)
    prms = [p[k] for k in ("ml_norm", "gla_norm", "rw_norm")]
    full = lambda a: pl.BlockSpec(a.shape, lambda i: (0,) * a.ndim)
    out = jax.ShapeDtypeStruct((n, W_GROUP), BF16)
    return pl.pallas_call(
        _step_post_kernel,
        grid=(1,),
        in_specs=[full(a) for a in data] + [q.spec() for q in prms],
        out_specs=[pl.BlockSpec((n, W_GROUP), lambda i: (0, 0))] * 3,
        out_shape=[out] * 3,
        compiler_params=_cparams("arbitrary"),
        name="step_post",
    )(*data, *[q.arr for q in prms])


def _prepare_params(prm):
    P = {}
    w_in = prm["w_in"].astype(BF16)
    depth, d = w_in.shape[0], w_in.shape[1]
    sizes = (W_GROUP, W_GROUP, W_GROUP, W_GROUP, N_HEADS, N_HEADS, W_GROUP, W_GROUP, W_GROUP, W_GROUP,
             GLA_RANK, W_GROUP, RWKV_COLS)
    offs = [0]
    for sz in sizes:
        offs.append(offs[-1] + sz)
    col = lambda i: w_in[:, :, offs[i]:offs[i + 1]]
    (u, mq, mk, mv, mi, mf, mo, gq, gk, gv, ga, gg, rcols) = [col(i) for i in range(13)]
    small = jnp.concatenate([ga, mi, mf, jnp.zeros((depth, d, SMALL_W - GLA_RANK - 2 * N_HEADS), BF16)], axis=2)
    P["w_in"] = jnp.concatenate([rcols, small, u, mq, mk, mv, mo, gq, gk, gv, gg], axis=2)
    P["w_gate_t"] = jnp.swapaxes(jnp.concatenate([mi, mf], axis=2), 1, 2)

    lam = lax.complex(prm["s5_lam_re"], prm["s5_lam_im"])
    dt = jnp.exp(prm["s5_log_dt"])[..., None]
    lam_bar = jnp.exp(lam * dt)
    b_bar = ((lam_bar - 1.0) / lam)[..., None] * lax.complex(prm["s5_b_re"], prm["s5_b_im"])
    eye = jnp.eye(S5_GROUPS, dtype=F32)
    bm = lambda b: jnp.einsum("lgph,gk->lghkp", b, eye).reshape(depth, W_GROUP, S5_WIDTH)
    P["bmat"] = jnp.concatenate([bm(b_bar.real), bm(b_bar.imag)], axis=2).astype(BF16)
    cm = lambda c: jnp.einsum("lghp,gk->lkpgh", c, eye).reshape(depth, S5_WIDTH, W_GROUP)
    P["cmat"] = jnp.concatenate([cm(prm["s5_c_re"]), -cm(prm["s5_c_im"])], axis=1).astype(BF16)
    P["lam"] = jnp.stack([lam_bar.real.reshape(depth, S5_WIDTH), lam_bar.imag.reshape(depth, S5_WIDTH)],
                         axis=1)[:, :, None, :]
    P["wglu"] = prm["s5_w_glu"].astype(BF16)

    bias = prm["ml_gate_bias"]
    P["ml_brow"] = jnp.pad(bias, ((0, 0), (LANE_MI, SMALL_W - LANE_MI - 2 * N_HEADS)))[:, None, :]
    P["ml_bcol"] = jnp.broadcast_to(bias[:, :, None], (depth, 2 * N_HEADS, CHUNK_ML))

    P["rows"] = jnp.stack([prm["s5_d"], prm["ml_norm"], prm["gla_b_alpha"], prm["gla_norm"], prm["rw_w0"],
                           prm["rw_a0"], prm["rw_k_k"], prm["rw_k_a"], prm["rw_r_k"].reshape(depth, W_GROUP),
                           prm["rw_norm"]], axis=1)[:, :, None, :]
    pad_rows = lambda w, lo: jnp.pad(w, ((0, 0), (lo, LANES - lo - w.shape[1]), (0, 0)))
    P["pads"] = jnp.stack([pad_rows(prm["gla_w_alpha"], 0), pad_rows(prm["rw_w2"], 0),
                           pad_rows(prm["rw_a2"], RWKV_W_RANK),
                           pad_rows(prm["rw_g2"], RWKV_W_RANK + RWKV_A_RANK)], axis=1).astype(BF16)
    P["rw_mu"] = prm["rw_mu"][:, None, :]
    P["norms"] = jnp.stack([prm["norm_mix"], prm["norm_ffn"]], axis=1)[:, :, None, :]
    P["w_out"] = prm["w_out"].astype(BF16)
    P["ffn_wup"] = prm["ffn_w_up"].astype(BF16)
    P["ffn_cw"] = prm["ffn_conv_w"]
    P["ffn_cb"] = prm["ffn_conv_b"][:, None, :]
    P["ffn_wdn"] = prm["ffn_w_down"].astype(BF16)
    return P


ROW_NAMES = ("d", "ml_norm", "gla_ba", "gla_norm", "rw_w0", "rw_a0", "rw_kk", "rw_ka", "rw_rk", "rw_norm")
PAD_NAMES = ("gla_wa", "rw_w2", "rw_a2", "rw_g2")


def _layer_params(P, l):
    p = {k: _Param(P[k], l) for k in ("w_in", "w_gate_t", "bmat", "cmat", "wglu", "ml_brow", "ml_bcol",
                                      "rw_mu", "w_out", "ffn_wup", "ffn_cw", "ffn_cb", "ffn_wdn")}
    p.update({k: _Param(P["rows"], l, i) for i, k in enumerate(ROW_NAMES)})
    p.update({k: _Param(P["pads"], l, i) for i, k in enumerate(PAD_NAMES)})
    p["lre"], p["lim"] = _Param(P["lam"], l, 0), _Param(P["lam"], l, 1)
    p["norm_mix"], p["norm_ffn"] = _Param(P["norms"], l, 0), _Param(P["norms"], l, 1)
    return p


def _prompt_layer(x2d, p, norm_final, n_seq, t_len, final):
    proj, gt, u = _inproj(x2d, p["norm_mix"], p["w_in"], p["w_gate_t"], n_seq, t_len, tm=INPROJ_ROWS)

    zeros = jnp.zeros((n_seq, S5_WIDTH), F32)
    y_s5, s5_re, s5_im = _s5(u, zeros, zeros, p, n_seq, t_len, tt=S5_TILE_STEPS, wide=True)

    y_ml, ml_c, ml_n, ml_m = _mlstm(proj, gt, p, n_seq, t_len)
    y_gla, gla_st = _gla(proj, p, n_seq, t_len)
    y_rw, rw_s, rw_shift = _rwkv(proj, p, n_seq, t_len)

    x_new, ffn_buf = _ffn_seq(x2d, (y_s5, y_ml, y_gla, y_rw), p, norm_final, n_seq, t_len, tm=FFN_ROWS,
                              final=final)
    states = (
        s5_re.reshape(n_seq, S5_GROUPS, S5_STATE),
        s5_im.reshape(n_seq, S5_GROUPS, S5_STATE),
        ml_c,
        ml_n[:, 0, :].reshape(n_seq, N_HEADS, HEAD_DIM),
        ml_m[:, 0, LANE_MI:LANE_MI + N_HEADS],
        gla_st,
        rw_s,
        rw_shift[:, 0, :],
        ffn_buf,
    )
    return x_new, states


def _sample_layer(x2d, st, pair_states, layer, p, norm_final, final):
    (s5_re, s5_im, ml_m, rw_shift, ffn_buf) = st
    n = x2d.shape[0]
    npairs = n * N_HEADS
    proj, _, u = _inproj(x2d, p["norm_mix"], p["w_in"], p["w_gate_t"], 1, n, tm=n)

    y_s5, s5_re_new, s5_im_new = _s5(u,
                                     s5_re.reshape(n, S5_WIDTH), s5_im.reshape(n, S5_WIDTH),
                                     p, n, 1, tt=1, wide=False)

    m_pad = jnp.zeros((n, SMALL_W), F32).at[:, LANE_MI:LANE_MI + N_HEADS].set(ml_m)
    pack, m_new, g, bonus = _step_vectors(proj, rw_shift, m_pad, p)
    ml_c_new, gla_s_new, rw_s_new, opack = _heads_step(pack, *pair_states, layer)

    main = lambda blk: proj[:, blk * W_GROUP:(blk + 1) * W_GROUP]
    y_ml, y_gla, y_rw = _step_post(opack[OROW_ML], main(BLK_MO), opack[OROW_GLA], main(BLK_GG),
                                   opack[OROW_RW], g, bonus, p)
    x_new, ug = _ffn_step(x2d, (y_s5, y_ml, y_gla, y_rw), p, norm_final, ffn_buf[:, 0, :], ffn_buf[:, 1, :],
                          final)
    states = (
        s5_re_new.reshape(n, S5_GROUPS, S5_STATE),
        s5_im_new.reshape(n, S5_GROUPS, S5_STATE),
        ml_c_new.transpose(3, 0, 1, 2),
        opack[OROW_N].reshape(N_HEADS, HEAD_DIM, n).transpose(2, 0, 1),
        m_new[:, LANE_MI:LANE_MI + N_HEADS],
        gla_s_new.transpose(3, 0, 1, 2),
        rw_s_new.transpose(3, 0, 1, 2),
        proj[:, 0:RWKV_COLS],
        jnp.stack([ffn_buf[:, 1, :], ug], axis=1),
    )
    return x_new, states


def kernel(x_prompt, x_sample, state_s5_re, state_s5_im, state_mlstm_C, state_mlstm_n, state_mlstm_m, state_gla_S, state_rwkv_S, state_rwkv_shift, state_ffn_conv, norm_mix, w_in, s5_lam_re, s5_lam_im, s5_log_dt, s5_b_re, s5_b_im, s5_c_re, s5_c_im, s5_d, s5_w_glu, ml_gate_bias, ml_norm, gla_w_alpha, gla_b_alpha, gla_norm, rw_mu, rw_w0, rw_w2, rw_a0, rw_a2, rw_g2, rw_k_k, rw_k_a, rw_r_k, rw_norm, w_out, norm_ffn, ffn_w_up, ffn_conv_w, ffn_conv_b, ffn_w_down, norm_final):
    prm = dict(norm_mix=norm_mix, w_in=w_in, s5_lam_re=s5_lam_re, s5_lam_im=s5_lam_im, s5_log_dt=s5_log_dt,
               s5_b_re=s5_b_re, s5_b_im=s5_b_im, s5_c_re=s5_c_re, s5_c_im=s5_c_im, s5_d=s5_d,
               s5_w_glu=s5_w_glu, ml_gate_bias=ml_gate_bias, ml_norm=ml_norm, gla_w_alpha=gla_w_alpha,
               gla_b_alpha=gla_b_alpha, gla_norm=gla_norm, rw_mu=rw_mu, rw_w0=rw_w0, rw_w2=rw_w2,
               rw_a0=rw_a0, rw_a2=rw_a2, rw_g2=rw_g2, rw_k_k=rw_k_k, rw_k_a=rw_k_a, rw_r_k=rw_r_k,
               rw_norm=rw_norm, w_out=w_out, norm_ffn=norm_ffn, ffn_w_up=ffn_w_up, ffn_conv_w=ffn_conv_w,
               ffn_conv_b=ffn_conv_b, ffn_w_down=ffn_w_down)
    depth = w_in.shape[0]
    n_seq, t_len, d = x_prompt.shape
    n_smp = x_sample.shape[0]
    assert t_len % max(INPROJ_ROWS, FFN_ROWS, S5_TILE_STEPS, CHUNK_ML) == 0
    assert n_seq % max(SEQS_PER_STEP, SEQS_PER_STEP_ML) == 0 and x_sample.shape[1] == 1
    nf = _Param(norm_final[None, :])
    small_states = (state_s5_re, state_s5_im, state_mlstm_m, state_rwkv_shift, state_ffn_conv)
    lead = depth * N_HEADS
    to_lanes = lambda st: jnp.moveaxis(st, 1, -1)
    pair_states = (to_lanes(state_mlstm_C).reshape(lead, HEAD_DIM, HEAD_DIM, n_smp),
                   to_lanes(state_mlstm_n).reshape(lead, HEAD_DIM, n_smp),
                   to_lanes(state_gla_S).reshape(lead, HEAD_DIM, HEAD_DIM, n_smp),
                   to_lanes(state_rwkv_S).reshape(lead, HEAD_DIM, HEAD_DIM, n_smp))

    P = _prepare_params(prm)
    xp = x_prompt.reshape(n_seq * t_len, d)
    xs = x_sample.reshape(n_smp, d)
    p_states, s_states = [], []
    for l in range(depth):
        p = _layer_params(P, l)
        final = l == depth - 1
        xp, ps = _prompt_layer(xp, p, nf, n_seq, t_len, final)
        xs, ss = _sample_layer(xs, tuple(s[l] for s in small_states), pair_states, l, p, nf, final)
        p_states.append(ps)
        s_states.append(ss)
    new_p = tuple(jnp.stack([st[i] for st in p_states]) for i in range(9))
    new_s = tuple(jnp.stack([st[i] for st in s_states]) for i in range(9))
    return (xp.reshape(n_seq, t_len, d), xs.reshape(n_smp, 1, d)) + new_p + new_s
```

```python
import functools
import math

import jax
import jax.numpy as jnp
from jax import lax
from jax.experimental import pallas as pl
from jax.experimental.pallas import tpu as pltpu

F32 = jnp.float32
BF16 = jnp.bfloat16

LANES = 128
SUBLANES = 8
W_GROUP = 256
HEAD_DIM = 64
HEAD_SHIFT = 6
N_HEADS = 4
S5_GROUPS = 16
S5_STATE = 64
S5_WIDTH = S5_GROUPS * S5_STATE
GLA_RANK = 16
GLA_GATE_TEMP = 16.0
RWKV_W_RANK = 32
RWKV_A_RANK = 32
RWKV_G_RANK = 64
RWKV_COLS = 3 * W_GROUP + RWKV_W_RANK + RWKV_A_RANK + RWKV_G_RANK
RWKV_DECAY_SCALE = 0.6065306597126334
CONV_W = 3
EPS = 1e-6
GN_EPS = 64e-5
NEG_BIG = -1e30
EXP_CLAMP = 80.0

SMALL_W = LANES
OFF_SMALL = RWKV_COLS
OFF_MAIN = RWKV_COLS + SMALL_W
D_PROJ = OFF_MAIN + 9 * W_GROUP
LANE_MI = GLA_RANK
LANE_MF = GLA_RANK + N_HEADS
GATE_ROWS = 2 * N_HEADS
(BLK_U, BLK_MQ, BLK_MK, BLK_MV, BLK_MO, BLK_GQ, BLK_GK, BLK_GV, BLK_GG) = range(
    OFF_MAIN // W_GROUP, OFF_MAIN // W_GROUP + 9)
BLK_SMALL = OFF_SMALL // SMALL_W

CHUNK_ML = 128
CHUNK_RW = 64
SEQS_PER_STEP = 8
SEQS_PER_STEP_ML = 8
INPROJ_ROWS = 1024
FFN_ROWS = 1024
S5_TILE_STEPS = 128
VMEM_LIMIT = 56 * 1024 * 1024


def _cparams(*sem):
    return pltpu.CompilerParams(dimension_semantics=sem, vmem_limit_bytes=VMEM_LIMIT)


class _Param:
    def __init__(self, arr, *lead):
        self.arr, self.lead = arr, tuple(lead)

    @property
    def shape(self):
        return tuple(self.arr.shape[len(self.lead):])

    def spec(self, **kw):
        lead, tail = self.lead, (0,) * len(self.shape)
        return pl.BlockSpec((None,) * len(lead) + self.shape, lambda *grid_idx: lead + tail, **kw)


def _dot(a, b):
    return jnp.dot(a.astype(BF16), b.astype(BF16), preferred_element_type=F32)


def _dot_nt(a, b):
    return lax.dot_general(a.astype(BF16), b.astype(BF16), (((1,), (1,)), ((), ())),
                           preferred_element_type=F32)


def _dot_tn(a, b):
    return lax.dot_general(a.astype(BF16), b.astype(BF16), (((0,), (0,)), ((), ())),
                           preferred_element_type=F32)


def _split2(x):
    hi = x.astype(BF16)
    lo = (x - hi.astype(F32)).astype(BF16)
    return hi, lo


def _dot_exact_rhs(a01, x):
    hi, lo = _split2(x)
    f = lambda p: jnp.dot(a01, p, preferred_element_type=F32)
    return f(hi) + f(lo)


def _dot_exact_lhs(x, b01):
    hi, lo = _split2(x)
    f = lambda p: jnp.dot(p, b01, preferred_element_type=F32)
    return f(hi) + f(lo)


def _sigmoid(x):
    return 1.0 / (1.0 + jnp.exp(-x))


def _log_sigmoid(x):
    return jnp.minimum(x, 0.0) - jnp.log(1.0 + jnp.exp(-jnp.abs(x)))


def _gelu(x):
    return 0.5 * x * (1.0 + jnp.tanh(math.sqrt(2.0 / math.pi) * (x + 0.044715 * (x * x * x))))


def _iota(shape, dim):
    return lax.broadcasted_iota(jnp.int32, shape, dim)


def _head_ones():
    r = _iota((W_GROUP, W_GROUP), 0) >> HEAD_SHIFT
    c = _iota((W_GROUP, W_GROUP), 1) >> HEAD_SHIFT
    return jnp.where(r == c, 1.0, 0.0).astype(BF16)


def _head_sum(x, ones):
    return _dot_exact_lhs(x, ones)


def _head_mean(x, ones):
    return _dot(x, ones) * (1.0 / HEAD_DIM)


def _rmsnorm_rows(x, g):
    ms = jnp.mean(x * x, axis=-1, keepdims=True)
    return x * lax.rsqrt(ms + EPS) * g


def _tril_ones(n, strict=False):
    r = _iota((n, n), 0)
    c = _iota((n, n), 1)
    return jnp.where((r > c) if strict else (r >= c), 1.0, 0.0).astype(BF16)


def _triu_ones(n):
    r = _iota((n, n), 0)
    c = _iota((n, n), 1)
    return jnp.where(r <= c, 1.0, 0.0).astype(BF16)


def _store_head_blocks(out_ref, scr_ref, transpose=False):
    for g in range(scr_ref.shape[0]):
        m = scr_ref[g].T if transpose else scr_ref[g]
        for h in range(N_HEADS):
            lo, hi = h * HEAD_DIM, (h + 1) * HEAD_DIM
            out_ref[g, h] = m[lo:hi, lo:hi]


def _HEAD_BLOCKS_SPEC(G):
    return pl.BlockSpec((G, N_HEADS, HEAD_DIM, HEAD_DIM), lambda b, c: (b, 0, 0, 0))


def _inproj_kernel(x_ref, g_ref, w_ref, wg_ref, proj_ref, gt_ref, u_ref):
    hn = _rmsnorm_rows(x_ref[...], g_ref[...]).astype(BF16)
    proj = jnp.dot(hn, w_ref[...], preferred_element_type=F32)
    proj_ref[...] = proj
    u_ref[...] = proj[:, OFF_MAIN:OFF_MAIN + W_GROUP]
    gt_ref[0] = lax.dot_general(wg_ref[...], hn, (((1,), (1,)), ((), ())),
                                preferred_element_type=F32)


def _inproj(x2d, g, w, wg, n_seq, t_len, tm):
    n, d = x2d.shape
    tiles_per_seq = t_len // tm
    return pl.pallas_call(
        _inproj_kernel,
        grid=(n // tm,),
        in_specs=[
            pl.BlockSpec((tm, d), lambda i: (i, 0)),
            g.spec(), w.spec(pipeline_mode=pl.Buffered(1)), wg.spec(),
        ],
        out_specs=[
            pl.BlockSpec((tm, D_PROJ), lambda i: (i, 0)),
            pl.BlockSpec((1, GATE_ROWS, tm), lambda i: (i // tiles_per_seq, 0, i % tiles_per_seq)),
            pl.BlockSpec((tm, W_GROUP), lambda i: (i % tiles_per_seq, i // tiles_per_seq)),
        ],
        out_shape=[
            jax.ShapeDtypeStruct((n, D_PROJ), F32),
            jax.ShapeDtypeStruct((n_seq, GATE_ROWS, t_len), F32),
            jax.ShapeDtypeStruct((t_len, n_seq * W_GROUP), F32),
        ],
        compiler_params=_cparams("arbitrary"),
        name="inproj",
    )(x2d, g.arr, w.arr, wg.arr)


def _s5_kernel(u_ref, unext_ref, h0re_ref, h0im_ref, bmat_ref, lre_ref, lim_ref, cmat_ref, d_ref, wglu_ref,
               y_ref, hre_out, him_out, bu_scr, hs_scr, hre_scr, him_scr, *, bp, tt, wide):
    i = pl.program_id(0)
    nt = S5_WIDTH // LANES

    rows_of = (lambda ref: ref[...].reshape(tt, bp, W_GROUP).reshape(tt * bp, W_GROUP)) if wide else (
        lambda ref: ref[...])

    def project(u_tile, slot):
        bu = jnp.dot(u_tile.astype(BF16), bmat_ref[...], preferred_element_type=F32)
        for j in range(2 * nt):
            bu_scr[slot, j] = bu[:, j * LANES:(j + 1) * LANES]

    @pl.when(i == 0)
    def _():
        hre_scr[...] = h0re_ref[...]
        him_scr[...] = h0im_ref[...]
        project(rows_of(u_ref), 0)

    slot = lax.rem(i, 2)
    lre = [lre_ref[:, j * LANES:(j + 1) * LANES] for j in range(nt)]
    lim = [lim_ref[:, j * LANES:(j + 1) * LANES] for j in range(nt)]

    def step(t, carry):
        rows = pl.ds(pl.multiple_of(t * bp, bp), bp)
        new = []
        for j in range(nt):
            hre, him = carry[j], carry[nt + j]
            new.append((lre[j] * hre - lim[j] * him + bu_scr[slot, j, rows, :],
                        lre[j] * him + lim[j] * hre + bu_scr[slot, nt + j, rows, :]))
        out = tuple(x[0] for x in new) + tuple(x[1] for x in new)
        for j in range(2 * nt):
            hs_scr[j, rows, :] = out[j]
        return out

    carry = tuple(hre_scr[:, j * LANES:(j + 1) * LANES] for j in range(nt)) + tuple(
        him_scr[:, j * LANES:(j + 1) * LANES] for j in range(nt))
    carry = step(0, carry) if tt == 1 else lax.fori_loop(0, tt, step, carry)
    hre = jnp.concatenate(carry[0:nt], axis=1)
    him = jnp.concatenate(carry[nt:2 * nt], axis=1)
    hre_scr[...] = hre
    him_scr[...] = him
    hre_out[...] = hre
    him_out[...] = him

    project(rows_of(unext_ref), lax.rem(i + 1, 2))
    u = rows_of(u_ref)
    hs = jnp.concatenate([hs_scr[j] for j in range(2 * nt)], axis=1)
    y = jnp.dot(hs.astype(BF16), cmat_ref[...], preferred_element_type=F32) + d_ref[...] * u
    z = _gelu(y)
    y = z * _sigmoid(_dot(z, wglu_ref[...]))
    if wide:
        y = y.reshape(tt, bp, W_GROUP).reshape(tt, bp * W_GROUP)
    y_ref[...] = y.astype(y_ref.dtype)


def _s5(u, h0re, h0im, sp, bp, t_len, tt, wide):
    nsteps = t_len // tt
    full = lambda shape: pl.BlockSpec(shape, lambda i: (0,) * len(shape))
    blk = (tt, bp * W_GROUP) if wide else (tt * bp, W_GROUP)
    tile = pl.BlockSpec(blk, lambda i: (i, 0))
    next_tile = pl.BlockSpec(blk, lambda i: (jnp.minimum(i + 1, nsteps - 1), 0))
    return pl.pallas_call(
        functools.partial(_s5_kernel, bp=bp, tt=tt, wide=wide),
        grid=(nsteps,),
        in_specs=[
            tile, next_tile,
            full((bp, S5_WIDTH)), full((bp, S5_WIDTH)),
            sp["bmat"].spec(), sp["lre"].spec(), sp["lim"].spec(), sp["cmat"].spec(), sp["d"].spec(),
            sp["wglu"].spec(),
        ],
        out_specs=[
            tile,
            full((bp, S5_WIDTH)), full((bp, S5_WIDTH)),
        ],
        out_shape=[
            jax.ShapeDtypeStruct(u.shape, BF16),
            jax.ShapeDtypeStruct((bp, S5_WIDTH), F32),
            jax.ShapeDtypeStruct((bp, S5_WIDTH), F32),
        ],
        scratch_shapes=[
            pltpu.VMEM((2, 2 * S5_WIDTH // LANES, tt * bp, LANES), F32),
            pltpu.VMEM((2 * S5_WIDTH // LANES, tt * bp, LANES), F32),
            pltpu.VMEM((bp, S5_WIDTH), F32),
            pltpu.VMEM((bp, S5_WIDTH), F32),
        ],
        compiler_params=_cparams("arbitrary"),
        name="s5_scan",
    )(u, u, h0re, h0im, sp["bmat"].arr, sp["lre"].arr, sp["lim"].arr, sp["cmat"].arr, sp["d"].arr,
      sp["wglu"].arr)


ALL_PHASES = ("init", "body", "final")


def _run_interleaved(chains):
    live = list(chains)
    while live:
        still = []
        for ch in live:
            try:
                next(ch)
                still.append(ch)
            except StopIteration:
                pass
        live = still


def _mlstm_kernel(q_ref, k_ref, v_ref, og_ref, sm_ref, gt_ref, brow_ref, bcol_ref, norm_ref,
                  y_ref, c_out, n_out, m_out, c_scr, n_scr, m_scr, *, L, nc, G, phases=ALL_PHASES):
    c = pl.program_id(1)

    def init():
        @pl.when(c == 0)
        def _():
            c_scr[...] = jnp.zeros_like(c_scr)
            n_scr[...] = jnp.zeros_like(n_scr)
            m_scr[...] = jnp.zeros_like(m_scr)

    def final():
        @pl.when(c == nc - 1)
        def _():
            _store_head_blocks(c_out, c_scr)
            n_out[...] = jnp.broadcast_to(n_scr[:, 0:1, :], n_out.shape)
            m_out[...] = m_scr[...]

    if "init" in phases:
        init()
    if "body" not in phases:
        if "final" in phases:
            final()
        return

    ones = _head_ones()
    tril = _tril_ones(L)
    triu = _triu_ones(L)
    l_shift = L.bit_length() - 1
    src = _iota((LANES, N_HEADS * L), 0)
    to_scores = jnp.where(src == LANE_MI + (_iota((LANES, N_HEADS * L), 1) >> l_shift), 1.0, 0.0).astype(BF16)
    src = _iota((LANES, W_GROUP), 0)
    to_feats = jnp.where(src == LANE_MI + (_iota((LANES, W_GROUP), 1) >> HEAD_SHIFT), 1.0, 0.0).astype(BF16)
    seg = _iota((N_HEADS * L, W_GROUP), 0) >> l_shift
    score_sum = jnp.where(seg == (_iota((N_HEADS * L, W_GROUP), 1) >> HEAD_SHIFT), 1.0, 0.0).astype(BF16)
    causal = _iota((L, N_HEADS * L), 0) >= (_iota((L, N_HEADS * L), 1) & (L - 1))
    t_idx = _iota((L, LANES), 0)

    seqs = list(range(G))
    cat = lambda xs: jnp.concatenate(xs, axis=0)
    split = lambda x: [x[i * L:(i + 1) * L] for i in seqs]

    def running_max(x):
        shift = 1
        while shift < L:
            x = jnp.maximum(x, jnp.where(t_idx >= shift, pltpu.roll(x, shift, 0), NEG_BIG))
            shift *= 2
        return x

    q = [q_ref[i] for i in seqs]
    k = [k_ref[i] * (HEAD_DIM ** -0.5) for i in seqs]
    v = [v_ref[i] for i in seqs]
    C = [c_scr[i] for i in seqs]
    n = [n_scr[i, 0:1, :] for i in seqs]
    m0 = [m_scr[i, 0:1, :] for i in seqs]
    pre = [sm_ref[i] + brow_ref[...] for i in seqs]
    gtb = [gt_ref[i] + bcol_ref[...] for i in seqs]
    bc_col = _each(lambda x: _dot_exact_rhs(tril, _log_sigmoid(x)), pre)
    bc_rows = _dot_exact_lhs(_log_sigmoid(cat(gtb)), triu)
    qC = _each(_dot_nt, q, C)
    qk = _each(lambda q_, k_: _dot_nt(q_, _stack_heads(k_)), q, k)
    qn = _head_sum(cat(_each(lambda q_, n_: q_ * n_, q, n)), ones)

    b_col = _each(lambda x: pltpu.roll(x, LANES - N_HEADS, 1), bc_col)
    r_col = _each(lambda p_, b_: p_ - b_, pre, b_col)
    mu = _each(lambda r_, m_: jnp.maximum(running_max(r_), m_), r_col, m0)
    mu_last = _each(lambda x: x[L - 1:L, :], mu)
    mu_scores = split(_dot_exact_lhs(cat(mu), to_scores))

    def weights(i):
        f0 = GATE_ROWS * i + N_HEADS
        r_rows = gtb[i][0:N_HEADS, :] - bc_rows[f0:f0 + N_HEADS, :]
        r_all = jnp.concatenate([r_rows[h:h + 1, :] for h in range(N_HEADS)], axis=1)
        return jnp.exp(jnp.where(causal, r_all - mu_scores[i], NEG_BIG)) * qk[i]

    ws = [weights(i) for i in seqs]
    wsv = _each(lambda w_, v_: _dot(w_, _stack_heads(v_)), ws, v)
    w0 = _dot(cat(_each(lambda m_, u_: jnp.exp(m_ - u_), m0, mu)), to_feats)
    m_row = _dot_exact_lhs(cat(_each(lambda b_, u_: b_ + u_, b_col, mu)), to_feats)
    den = w0 * qn + _dot(cat(ws), score_sum)
    hc = (w0 * cat(qC) + cat(wsv)) / jnp.maximum(jnp.abs(den), jnp.exp(-m_row))

    wl = split(_dot(cat(_each(lambda r_, u_: jnp.exp(r_ - u_), r_col, mu_last)), to_feats))
    w0l_rows = _each(lambda m_, u_: jnp.exp(m_ - u_), m0, mu_last)
    w0l_rows = w0l_rows + [jnp.zeros((1, LANES), F32)] * (-G % SUBLANES)
    w0l = _dot_exact_lhs(cat(w0l_rows), to_feats)
    kw = _each(lambda k_, w_: k_ * w_, k, wl)
    upd = _each(_dot_tn, v, kw)
    for i in seqs:
        c_scr[i] = w0l[i:i + 1, :] * C[i] + ones.astype(F32) * upd[i]
        n_scr[i, 0:1, :] = w0l[i:i + 1, :] * n[i] + jnp.sum(kw[i], axis=0, keepdims=True)
        m_scr[i, 0:1, :] = b_col[i][L - 1:L, :] + mu_last[i]

    ms = _head_mean(hc * hc, ones)
    y = split(hc * lax.rsqrt(ms + EPS) * norm_ref[...])
    for i in seqs:
        y_ref[i] = (y[i] * _sigmoid(og_ref[i])).astype(y_ref.dtype)

    if "final" in phases:
        final()


def _seq_spec(G, L, blk, width=W_GROUP):
    return pl.BlockSpec((G, L, width), lambda b, c: (b, c, blk))


N_ML_IN, N_GLA_IN, N_ML_OUT, N_GLA_OUT, N_ML_SCR = 9, 8, 4, 2, 3


def _mlstm_gla_kernel(*refs, L, nc, G):
    bounds = [0]
    for n in (N_ML_IN, N_GLA_IN, N_ML_OUT, N_GLA_OUT, N_ML_SCR):
        bounds.append(bounds[-1] + n)
    ml_in, gla_in, ml_out, gla_out, ml_scr = [refs[a:b] for a, b in zip(bounds[:-1], bounds[1:])]
    gla_scr = refs[bounds[-1]:]
    ml = functools.partial(_mlstm_kernel, *ml_in, *ml_out, *ml_scr, L=L, nc=nc, G=G)
    gla = functools.partial(_gla_kernel, *gla_in, *gla_out, *gla_scr, L=L, nc=nc, G=G)
    for phase in ALL_PHASES:
        ml(phases=(phase,))
        gla(phases=(phase,))


def _mlstm_gla(proj, gt, p, n_seq, t_len):
    L = CHUNK_ML
    G = SEQS_PER_STEP_ML
    nc = t_len // L
    per_seq = lambda rows, width: pl.BlockSpec((G, rows, width), lambda b, c: (b, 0, 0))
    y_spec = pl.BlockSpec((G, L, W_GROUP), lambda b, c: (b, c, 0))
    y_shape = jax.ShapeDtypeStruct((n_seq, t_len, W_GROUP), BF16)
    state_shape = jax.ShapeDtypeStruct((n_seq, N_HEADS, HEAD_DIM, HEAD_DIM), F32)
    proj3 = proj.reshape(n_seq, t_len, D_PROJ)
    y_ml, c_new, n_new, m_new, y_gla, s_new = pl.pallas_call(
        functools.partial(_mlstm_gla_kernel, L=L, nc=nc, G=G),
        grid=(n_seq // G, nc),
        in_specs=[
            _seq_spec(G, L, BLK_MQ), _seq_spec(G, L, BLK_MK), _seq_spec(G, L, BLK_MV),
            _seq_spec(G, L, BLK_MO), _seq_spec(G, L, BLK_SMALL, SMALL_W),
            pl.BlockSpec((G, GATE_ROWS, L), lambda b, c: (b, 0, c)),
            p["ml_brow"].spec(), p["ml_bcol"].spec(), p["ml_norm"].spec(),
            _seq_spec(G, L, BLK_GQ), _seq_spec(G, L, BLK_GK), _seq_spec(G, L, BLK_GV),
            _seq_spec(G, L, BLK_GG), _seq_spec(G, L, BLK_SMALL, SMALL_W),
            p["gla_wa"].spec(), p["gla_ba"].spec(), p["gla_norm"].spec(),
        ],
        out_specs=[
            y_spec, _HEAD_BLOCKS_SPEC(G), per_seq(SUBLANES, W_GROUP), per_seq(SUBLANES, LANES),
            y_spec, _HEAD_BLOCKS_SPEC(G),
        ],
        out_shape=[
            y_shape, state_shape,
            jax.ShapeDtypeStruct((n_seq, SUBLANES, W_GROUP), F32),
            jax.ShapeDtypeStruct((n_seq, SUBLANES, LANES), F32),
            y_shape, state_shape,
        ],
        scratch_shapes=[
            pltpu.VMEM((G, W_GROUP, W_GROUP), F32),
            pltpu.VMEM((G, SUBLANES, W_GROUP), F32),
            pltpu.VMEM((G, SUBLANES, LANES), F32),
            pltpu.VMEM((G, W_GROUP, W_GROUP), F32),
        ],
        compiler_params=_cparams("arbitrary", "arbitrary"),
        name="mlstm_gla_chunk",
    )(proj3, proj3, proj3, proj3, proj3, gt, p["ml_brow"].arr, p["ml_bcol"].arr, p["ml_norm"].arr,
      proj3, proj3, proj3, proj3, proj3, p["gla_wa"].arr, p["gla_ba"].arr, p["gla_norm"].arr)
    flat = lambda y: y.reshape(n_seq * t_len, W_GROUP)
    return (flat(y_ml), c_new, n_new, m_new), (flat(y_gla), s_new)


def _gla_kernel(q_ref, k_ref, v_ref, gg_ref, sm_ref, wa_ref, ba_ref, norm_ref,
                y_ref, s_out, s_scr, *, L, nc, G, phases=ALL_PHASES):
    c = pl.program_id(1)

    def init():
        @pl.when(c == 0)
        def _():
            s_scr[...] = jnp.zeros_like(s_scr)

    def final():
        @pl.when(c == nc - 1)
        def _():
            _store_head_blocks(s_out, s_scr, transpose=True)

    if "init" in phases:
        init()
    if "body" not in phases:
        if "final" in phases:
            final()
        return

    ones = _head_ones()
    tril = _tril_ones(L)
    causal = _iota((L, N_HEADS * L), 0) >= (_iota((L, N_HEADS * L), 1) & (L - 1))

    def chain(gi):
        q = q_ref[gi] * (HEAD_DIM ** -0.5)
        k = k_ref[gi]
        v = v_ref[gi]
        la = _log_sigmoid(_dot(sm_ref[gi], wa_ref[...]) + ba_ref[...]) * (1.0 / GLA_GATE_TEMP)
        yield
        bc = _dot_exact_rhs(tril, la)
        yield
        ref = bc[L // 2 - 1:L // 2, :]
        qr = q * jnp.exp(jnp.minimum(bc - ref, EXP_CLAMP))
        kh = k * jnp.exp(jnp.minimum(ref - bc, EXP_CLAMP))
        ST = s_scr[gi]
        att = jnp.where(causal, _dot_nt(qr, _stack_heads(kh)), 0.0)
        from_state = _dot_nt(q * jnp.exp(bc), ST)
        yield
        o = from_state + _dot(att, _stack_heads(v))
        b_last = bc[L - 1:L, :]
        kbar = k * jnp.exp(b_last - bc)
        s_scr[gi] = ST * jnp.exp(b_last) + ones.astype(F32) * _dot_tn(v, kbar)
        yield
        ms = _head_mean(o * o, ones)
        gg = gg_ref[gi]
        y = o * lax.rsqrt(ms + EPS) * norm_ref[...] * (gg * _sigmoid(gg))
        y_ref[gi] = y.astype(y_ref.dtype)

    _run_interleaved([chain(gi) for gi in range(G)])

    if "final" in phases:
        final()


def _rwkv_vectors(rc, prev, p_mu, p_w0, p_a0, p_kk, p_ka, p_rk, w2, a2, g2, ones):
    xm = rc + p_mu * (prev - rc)
    rr = xm[:, 0:W_GROUP]
    rk = xm[:, W_GROUP:2 * W_GROUP]
    rv = xm[:, 2 * W_GROUP:3 * W_GROUP]
    tail = xm[:, 3 * W_GROUP:RWKV_COLS]
    lw = -RWKV_DECAY_SCALE * _sigmoid(p_w0 + _dot(jnp.tanh(tail), w2))
    a = _sigmoid(p_a0 + _dot(tail, a2))
    g = _dot(_sigmoid(tail), g2)
    kk = rk * p_kk
    kk = kk * lax.rsqrt(jnp.maximum(_head_sum(kk * kk, ones), 1e-24))
    kt = rk * (1.0 + (a - 1.0) * p_ka)
    bonus = _head_sum(rr * kt * p_rk, ones) * rv
    return rr, lw, kt, rv, kk, a, g, bonus


def _head_layernorm(o, g, ones):
    mu = _head_mean(o, ones)
    oc = o - mu
    var = _head_mean(oc * oc, ones)
    return oc * lax.rsqrt(var + GN_EPS) * g


def _stack_heads(x):
    xb = x.astype(BF16)
    lane_head = _iota((1, W_GROUP), 1) >> HEAD_SHIFT
    return jnp.concatenate([jnp.where(lane_head == h, xb, jnp.zeros_like(xb)) for h in range(N_HEADS)],
                           axis=0)


def _each(f, *seqs):
    return [f(*args) for args in zip(*seqs)]


def _block_mm(x_list, y_list):
    return _each(lambda x, y: _dot(x, _stack_heads(y)), x_list, y_list)


def _unit_lower_inverse(a_list, L):
    t_idx = _iota((L, N_HEADS * L), 0)
    s_idx = _iota((L, N_HEADS * L), 1) & (L - 1)
    eye = jnp.where(t_idx == s_idx, 1.0, 0.0).astype(F32)
    in16 = (t_idx >> 4) == (s_idx >> 4)
    in32 = ((t_idx >> 5) == (s_idx >> 5)) & ((t_idx >> 4) > (s_idx >> 4))
    in64 = (t_idx >> 5) > (s_idx >> 5)
    x16 = _each(lambda a: jnp.where(in16, -a, 0.0), a_list)
    x2 = _block_mm(x16, x16)
    x4 = _block_mm(x2, x2)
    x8 = _block_mm(x4, x4)
    t = _each(lambda x: eye + x, x16)
    for xp in (x2, x4, x8):
        t = _each(lambda t_, d: t_ + d, t, _block_mm(t, xp))
    for sel in (in32, in64):
        w = _block_mm(_each(lambda a: jnp.where(sel, a, 0.0), a_list), t)
        t = _each(lambda t_, d: t_ - d, t, _block_mm(t, w))
    return t


def _rwkv_chunks(rcs, prev_rows, states, prm, ones, L):
    (mu, w0, a0, pkk, pka, prk, w2, a2, g2, norm) = prm
    first_row = _iota((L, RWKV_COLS), 0) == 0
    prevs = _each(lambda rc, pr: jnp.where(first_row, pr, pltpu.roll(rc, 1, 0)), rcs, prev_rows)
    n_seq = len(rcs)
    stacked = _rwkv_vectors(jnp.concatenate(rcs, axis=0), jnp.concatenate(prevs, axis=0),
                            mu, w0, a0, pkk, pka, prk, w2, a2, g2, ones)
    r, lw, kx, v, kk, a, g, bonus = [[x[i * L:(i + 1) * L] for i in range(n_seq)] for x in stacked]

    tril = _tril_ones(L)
    t_idx = _iota((L, N_HEADS * L), 0)
    s_idx = _iota((L, N_HEADS * L), 1) & (L - 1)
    strict = t_idx > s_idx
    incl = t_idx >= s_idx

    def core(r, lw, kx, v, kk, a, states):
        lc = _each(lambda x: _dot_exact_rhs(tril, x), lw)
        akk = _each(lambda a_, k_: a_ * k_, a, kk)
        e_neg = _each(lambda x: jnp.exp(-x), lc)
        l_last = _each(lambda x: x[L - 1:L, :], lc)
        e_end = _each(lambda ll, x: jnp.exp(ll - x), l_last, lc)
        kr = _each(lambda k_, r_, c_, w_: jnp.concatenate([k_ * jnp.exp(c_ - w_), r_ * jnp.exp(c_)], axis=0),
                   kk, r, lc, lw)
        a_hat = _each(lambda x, e: _stack_heads(x * e), akk, e_neg)
        k_hat = _each(lambda x, e: _stack_heads(x * e), kx, e_neg)
        v_bd = _each(_stack_heads, v)

        pa = _each(_dot_nt, kr, a_hat)
        pk = _each(_dot_nt, kr, k_hat)
        a_ua = _each(lambda p_: jnp.where(strict, p_[0:L], 0.0), pa)
        b_ra = _each(lambda p_: jnp.where(incl, p_[L:2 * L], 0.0), pa)
        ab_k = _each(lambda p_: jnp.concatenate([jnp.where(strict, p_[0:L], 0.0),
                                                 jnp.where(incl, p_[L:2 * L], 0.0)], axis=0), pk)
        T = _unit_lower_inverse(a_ua, L)

        from_state = _each(_dot_nt, kr, states)
        from_v = _each(_dot, ab_k, v_bd)
        rhs = _each(lambda s_, v_: s_[0:L] + v_[0:L], from_state, from_v)
        U = _block_mm(T, rhs)
        corr = _block_mm(b_ra, U)
        o = _each(lambda s_, v_, c_: s_[L:2 * L] + v_[L:2 * L] - c_, from_state, from_v, corr)
        upd = _each(lambda v_, u_, k_, a_, e: _dot_tn(jnp.concatenate([v_, u_], axis=0),
                                                      jnp.concatenate([k_ * e, -(a_ * e)], axis=0)),
                    v, U, kx, akk, e_end)
        s_new = _each(lambda s_, ll, d: s_ * jnp.exp(ll) + ones.astype(F32) * d, states, l_last, upd)
        return o, s_new

    o, s_new = core(r, lw, kx, v, kk, a, states)
    y_all = ((_head_layernorm(jnp.concatenate(o, axis=0), norm, ones) + stacked[7]) * stacked[6])
    return [y_all[i * L:(i + 1) * L] for i in range(n_seq)], s_new


def _rwkv_kernel(rc_ref, mu_ref, w0_ref, a0_ref, kk_ref, ka_ref, rk_ref, w2_ref, a2_ref, g2_ref,
                 norm_ref, y_ref, s_out, shift_out, s_scr, prev_scr, *, L, nc, G):
    c = pl.program_id(1)

    @pl.when(c == 0)
    def _():
        s_scr[...] = jnp.zeros_like(s_scr)
        prev_scr[...] = jnp.zeros_like(prev_scr)

    ones = _head_ones()
    prm = (mu_ref[...], w0_ref[...], a0_ref[...], kk_ref[...], ka_ref[...], rk_ref[...],
           w2_ref[...], a2_ref[...], g2_ref[...], norm_ref[...])
    rcs = [rc_ref[gi] for gi in range(G)]
    ys, s_new = _rwkv_chunks(rcs, [prev_scr[gi, 0:1, :] for gi in range(G)],
                             [s_scr[gi] for gi in range(G)], prm, ones, L)
    for gi in range(G):
        prev_scr[gi, 0:1, :] = rcs[gi][L - 1:L, :]
        s_scr[gi] = s_new[gi]
        y_ref[gi] = ys[gi].astype(y_ref.dtype)

    @pl.when(c == nc - 1)
    def _():
        _store_head_blocks(s_out, s_scr)
        shift_out[...] = jnp.broadcast_to(prev_scr[:, 0:1, :], shift_out.shape)


def _rwkv(proj, p, n_seq, t_len):
    L = CHUNK_RW
    G = SEQS_PER_STEP
    nc = t_len // L
    names = ("rw_mu", "rw_w0", "rw_a0", "rw_kk", "rw_ka", "rw_rk", "rw_w2", "rw_a2", "rw_g2", "rw_norm")
    y, s_new, shift = pl.pallas_call(
        functools.partial(_rwkv_kernel, L=L, nc=nc, G=G),
        grid=(n_seq // G, nc),
        in_specs=[
            pl.BlockSpec((G, L, RWKV_COLS), lambda b, c: (b, c, 0)),
        ] + [p[k].spec() for k in names],
        out_specs=[
            pl.BlockSpec((G, L, W_GROUP), lambda b, c: (b, c, 0)),
            _HEAD_BLOCKS_SPEC(G),
            pl.BlockSpec((G, SUBLANES, RWKV_COLS), lambda b, c: (b, 0, 0)),
        ],
        out_shape=[
            jax.ShapeDtypeStruct((n_seq, t_len, W_GROUP), BF16),
            jax.ShapeDtypeStruct((n_seq, N_HEADS, HEAD_DIM, HEAD_DIM), F32),
            jax.ShapeDtypeStruct((n_seq, SUBLANES, RWKV_COLS), F32),
        ],
        scratch_shapes=[
            pltpu.VMEM((G, W_GROUP, W_GROUP), F32),
            pltpu.VMEM((G, SUBLANES, RWKV_COLS), F32),
        ],
        compiler_params=_cparams("arbitrary", "arbitrary"),
        name="rwkv_chunk",
    )(proj.reshape(n_seq, t_len, D_PROJ), *[p[k].arr for k in names])
    return y.reshape(n_seq * t_len, W_GROUP), s_new, shift


FF_CHUNK = 256


def _mix_residual(x_ref, ys, wout_ref):
    acc = x_ref[...]
    for j, y in enumerate(ys):
        acc = acc + jnp.dot(y.astype(BF16), wout_ref[j * W_GROUP:(j + 1) * W_GROUP, :],
                            preferred_element_type=F32)
    return acc


def _ffn_body(x1, nrm_ref, wup_ref, cw_ref, cb_ref, wdn_ref, prev_rows, d_ff, act_scr):
    h2 = _rmsnorm_rows(x1, nrm_ref[...]).astype(BF16)
    for j in range(d_ff // FF_CHUNK):
        lo, hi = j * FF_CHUNK, (j + 1) * FF_CHUNK
        ug = jnp.dot(h2, wup_ref[:, lo:hi], preferred_element_type=F32)
        uv = jnp.dot(h2, wup_ref[:, d_ff + lo:d_ff + hi], preferred_element_type=F32)
        p2, p1 = prev_rows(j, ug)
        conv = (cb_ref[:, lo:hi] + cw_ref[0:1, lo:hi] * p2 + cw_ref[1:2, lo:hi] * p1
                + cw_ref[2:3, lo:hi] * ug)
        act_scr[:, lo:hi] = (_gelu(conv) * uv).astype(BF16)
    return x1 + jnp.dot(act_scr[...], wdn_ref[...], preferred_element_type=F32)


def _ffn_seq_kernel(x_ref, y0_ref, y1_ref, y2_ref, y3_ref, wout_ref, nrm_ref, wup_ref, cw_ref, cb_ref,
                    wdn_ref, nf_ref, o_ref, buf_out, carry_scr, act_scr, *, tm, d_ff, final):
    t = pl.program_id(1)

    @pl.when(t == 0)
    def _():
        carry_scr[...] = jnp.zeros_like(carry_scr)

    x1 = _mix_residual(x_ref, (y0_ref[...], y1_ref[...], y2_ref[...], y3_ref[...]), wout_ref)
    row = _iota((tm, FF_CHUNK), 0)

    def prev_rows(j, ug):
        lo, hi = j * FF_CHUNK, (j + 1) * FF_CHUNK
        c0 = carry_scr[0:1, lo:hi]
        c1 = carry_scr[1:2, lo:hi]
        p1 = jnp.where(row == 0, c1, pltpu.roll(ug, 1, 0))
        p2 = jnp.where(row == 0, c0, jnp.where(row == 1, c1, pltpu.roll(ug, 2, 0)))
        carry_scr[0:2, lo:hi] = ug[tm - 2:tm, :]
        return p2, p1

    out = _ffn_body(x1, nrm_ref, wup_ref, cw_ref, cb_ref, wdn_ref, prev_rows, d_ff, act_scr)
    if final:
        out = _rmsnorm_rows(out, nf_ref[...])
    o_ref[...] = out
    buf_out[0] = carry_scr[...]


def _ffn_step_kernel(x_ref, y0_ref, y1_ref, y2_ref, y3_ref, wout_ref, nrm_ref, wup_ref, cw_ref, cb_ref,
                     wdn_ref, nf_ref, p2_ref, p1_ref, o_ref, ug_out, act_scr, *, d_ff, final):
    x1 = _mix_residual(x_ref, (y0_ref[...], y1_ref[...], y2_ref[...], y3_ref[...]), wout_ref)

    def prev_rows(j, ug):
        lo, hi = j * FF_CHUNK, (j + 1) * FF_CHUNK
        ug_out[:, lo:hi] = ug
        return p2_ref[:, lo:hi], p1_ref[:, lo:hi]

    out = _ffn_body(x1, nrm_ref, wup_ref, cw_ref, cb_ref, wdn_ref, prev_rows, d_ff, act_scr)
    if final:
        out = _rmsnorm_rows(out, nf_ref[...])
    o_ref[...] = out


def _ffn_seq(x2d, ys, p, norm_final, n_seq, t_len, tm, final):
    n, d = x2d.shape
    d_ff = p["ffn_wdn"].shape[0]
    nt = t_len // tm
    rows = lambda w: pl.BlockSpec((tm, w), lambda b, t: (b * nt + t, 0))
    once = dict(pipeline_mode=pl.Buffered(1))
    return pl.pallas_call(
        functools.partial(_ffn_seq_kernel, tm=tm, d_ff=d_ff, final=final),
        grid=(n_seq, nt),
        in_specs=[rows(d), pl.BlockSpec((tm, W_GROUP), lambda b, t: (t, b))] + [rows(W_GROUP)] * 3 + [
            p["w_out"].spec(**once), p["norm_ffn"].spec(), p["ffn_wup"].spec(**once), p["ffn_cw"].spec(),
            p["ffn_cb"].spec(), p["ffn_wdn"].spec(**once), norm_final.spec()],
        out_specs=[rows(d), pl.BlockSpec((1, CONV_W - 1, d_ff), lambda b, t: (b, 0, 0))],
        out_shape=[jax.ShapeDtypeStruct((n, d), F32),
                   jax.ShapeDtypeStruct((n_seq, CONV_W - 1, d_ff), F32)],
        scratch_shapes=[pltpu.VMEM((CONV_W - 1, d_ff), F32), pltpu.VMEM((tm, d_ff), BF16)],
        compiler_params=_cparams("arbitrary", "arbitrary"),
        name="wout_ffn_seq",
    )(x2d, *ys, p["w_out"].arr, p["norm_ffn"].arr, p["ffn_wup"].arr, p["ffn_cw"].arr, p["ffn_cb"].arr,
      p["ffn_wdn"].arr, norm_final.arr)


def _ffn_step(x2d, ys, p, norm_final, prev2, prev1, final):
    n, d = x2d.shape
    d_ff = p["ffn_wdn"].shape[0]
    full = lambda shape: pl.BlockSpec(shape, lambda i: (0,) * len(shape))
    return pl.pallas_call(
        functools.partial(_ffn_step_kernel, d_ff=d_ff, final=final),
        grid=(1,),
        in_specs=[full((n, d))] + [full((n, W_GROUP))] * 4 + [
            p["w_out"].spec(), p["norm_ffn"].spec(), p["ffn_wup"].spec(), p["ffn_cw"].spec(),
            p["ffn_cb"].spec(), p["ffn_wdn"].spec(), norm_final.spec(), full((n, d_ff)), full((n, d_ff))],
        out_specs=[full((n, d)), full((n, d_ff))],
        out_shape=[jax.ShapeDtypeStruct((n, d), F32), jax.ShapeDtypeStruct((n, d_ff), F32)],
        scratch_shapes=[pltpu.VMEM((n, d_ff), BF16)],
        compiler_params=_cparams("arbitrary"),
        name="wout_ffn_step",
    )(x2d, *ys, p["w_out"].arr, p["norm_ffn"].arr, p["ffn_wup"].arr, p["ffn_cw"].arr, p["ffn_cb"].arr,
      p["ffn_wdn"].arr, norm_final.arr, prev2, prev1)


def _step_vec_kernel(proj_ref, shift_ref, m_ref, brow_ref, wa_ref, ba_ref, mu_ref, w0_ref, a0_ref,
                     kk_ref, ka_ref, rk_ref, w2_ref, a2_ref, g2_ref,
                     pack_out, m_out, g_out, bonus_out):
    ones = _head_ones()
    sm = proj_ref[:, OFF_SMALL:OFF_SMALL + SMALL_W]
    pre = sm + brow_ref[...]
    logf = pltpu.roll(_log_sigmoid(pre), LANES - N_HEADS, 1)
    m_old = m_ref[...]
    m_new = jnp.maximum(logf + m_old, pre)
    m_out[...] = m_new
    lane = _iota((SMALL_W, W_GROUP), 0)
    spread_mat = jnp.where(lane == LANE_MI + (_iota((SMALL_W, W_GROUP), 1) >> HEAD_SHIFT), 1.0, 0.0).astype(BF16)
    gate_lanes = (_iota(pre.shape, 1) >= LANE_MI) & (_iota(pre.shape, 1) < LANE_MI + N_HEADS)
    spread = lambda x: _dot_exact_lhs(jnp.where(gate_lanes, x, 0.0), spread_mat)
    la = _log_sigmoid(_dot(sm, wa_ref[...]) + ba_ref[...]) * (1.0 / GLA_GATE_TEMP)
    rc = proj_ref[:, 0:RWKV_COLS]
    rr, lw, kt, rv, kk, a, g, bonus = _rwkv_vectors(
        rc, shift_ref[...], mu_ref[...], w0_ref[...], a0_ref[...], kk_ref[...], ka_ref[...],
        rk_ref[...], w2_ref[...], a2_ref[...], g2_ref[...], ones)
    main = lambda blk: proj_ref[:, blk * W_GROUP:(blk + 1) * W_GROUP]
    rows = {
        ROW_MQ: main(BLK_MQ), ROW_MK: main(BLK_MK), ROW_MV: main(BLK_MV),
        ROW_GQ: main(BLK_GQ), ROW_GK: main(BLK_GK), ROW_GV: main(BLK_GV), ROW_GA: jnp.exp(la),
        ROW_KK: kk, ROW_AKK: a * kk, ROW_W: jnp.exp(lw), ROW_KT: kt, ROW_RR: rr, ROW_RV: rv,
        ROW_IW: spread(jnp.exp(pre - m_new)), ROW_FW: spread(jnp.exp(logf + m_old - m_new)),
        ROW_M: spread(m_new),
    }
    for i, val in rows.items():
        pack_out[i] = val.T
    g_out[...] = g
    bonus_out[...] = bonus


def _step_vectors(proj, shift, m_pad, p):
    n = proj.shape[0]
    full = lambda a: pl.BlockSpec(a.shape, lambda i: (0,) * a.ndim)
    data = (proj, shift, m_pad)
    prms = [p[k] for k in ("ml_brow", "gla_wa", "gla_ba", "rw_mu", "rw_w0", "rw_a0", "rw_kk", "rw_ka",
                           "rw_rk", "rw_w2", "rw_a2", "rw_g2")]
    wide = jax.ShapeDtypeStruct((n, W_GROUP), F32)
    outs = [jax.ShapeDtypeStruct((N_PACK, W_GROUP, n), F32), jax.ShapeDtypeStruct((n, SMALL_W), F32),
            wide, wide]
    return pl.pallas_call(
        _step_vec_kernel,
        grid=(1,),
        in_specs=[full(a) for a in data] + [q.spec() for q in prms],
        out_specs=[pl.BlockSpec(o.shape, lambda i, nd=len(o.shape): (0,) * nd) for o in outs],
        out_shape=outs,
        compiler_params=_cparams("arbitrary"),
        name="step_vectors",
    )(*data, *[q.arr for q in prms])


(ROW_MQ, ROW_MK, ROW_MV, ROW_GQ, ROW_GK, ROW_GV, ROW_GA, ROW_KK, ROW_AKK, ROW_W, ROW_KT, ROW_RR, ROW_RV,
 ROW_IW, ROW_FW, ROW_M) = range(16)
N_PACK = 16
(OROW_ML, OROW_GLA, OROW_RW, OROW_N) = range(4)
N_OPACK = 4
STEP_UNROLL = 4


def _heads_step_kernel(pack_ref, c_ref, n_ref, gs_ref, rs_ref, c_out, gs_out, rs_out, opack_out):
    vec = lambda i: pack_ref[i]
    one = lambda i, j: pack_ref[i, pl.ds(j, 1), :]
    colsum = lambda x: jnp.sum(x, axis=0, keepdims=True)

    k_ml = vec(ROW_MK) * (HEAD_DIM ** -0.5)
    q_ml = vec(ROW_MQ)
    iw = pack_ref[ROW_IW, 0:1, :]
    fw = pack_ref[ROW_FW, 0:1, :]
    n_new = fw * n_ref[...] + iw * k_ml
    opack_out[OROW_N] = n_new
    r_den = 1.0 / jnp.maximum(jnp.abs(colsum(n_new * q_ml)), jnp.exp(-pack_ref[ROW_M, 0:1, :]))

    def ml_body(v, carry):
        c_new = fw * c_ref[v] + (iw * one(ROW_MV, v)) * k_ml
        c_out[v] = c_new
        opack_out[OROW_ML, pl.ds(v, 1), :] = colsum(c_new * q_ml) * r_den
        return carry

    lax.fori_loop(0, HEAD_DIM, ml_body, 0, unroll=STEP_UNROLL)

    v_gla = vec(ROW_GV)

    def gla_body(k, acc):
        s_new = one(ROW_GA, k) * gs_ref[k] + one(ROW_GK, k) * v_gla
        gs_out[k] = s_new
        return acc + (one(ROW_GQ, k) * (HEAD_DIM ** -0.5)) * s_new

    opack_out[OROW_GLA] = lax.fori_loop(0, HEAD_DIM, gla_body, jnp.zeros_like(v_gla), unroll=STEP_UNROLL)

    kk, akk, w, kt, rr = vec(ROW_KK), vec(ROW_AKK), vec(ROW_W), vec(ROW_KT), vec(ROW_RR)

    def rw_body(v, carry):
        s = rs_ref[v]
        s_new = s * w - colsum(s * kk) * akk + one(ROW_RV, v) * kt
        rs_out[v] = s_new
        opack_out[OROW_RW, pl.ds(v, 1), :] = colsum(s_new * rr)
        return carry

    lax.fori_loop(0, HEAD_DIM, rw_body, 0, unroll=STEP_UNROLL)


def _heads_step(pack, ml_c, ml_n, gla_s, rw_s, layer):
    n = pack.shape[-1]
    sq = (HEAD_DIM, HEAD_DIM, n)
    head = lambda lead: pl.BlockSpec((lead, HEAD_DIM, n), lambda h: (0, h, 0))
    at_layer = lambda shape: pl.BlockSpec((None,) + shape, lambda h: (layer * N_HEADS + h,) + (0,) * len(shape))
    out_state = pl.BlockSpec((None,) + sq, lambda h: (h, 0, 0, 0))
    state = jax.ShapeDtypeStruct((N_HEADS,) + sq, F32)
    return pl.pallas_call(
        _heads_step_kernel,
        grid=(N_HEADS,),
        in_specs=[head(N_PACK), at_layer(sq), at_layer((HEAD_DIM, n)), at_layer(sq), at_layer(sq)],
        out_specs=[out_state, out_state, out_state, head(N_OPACK)],
        out_shape=[state, state, state, jax.ShapeDtypeStruct((N_OPACK, W_GROUP, n), F32)],
        compiler_params=_cparams("arbitrary"),
        name="heads_step",
    )(pack, ml_c, ml_n, gla_s, rw_s)


def _step_post_kernel(hml_ref, mo_ref, ogla_ref, gg_ref, orw_ref, g_ref, bonus_ref,
                      mln_ref, gln_ref, rwn_ref, yml_out, ygla_out, yrw_out):
    ones = _head_ones()
    hml = hml_ref[...].T
    ms = _head_mean(hml * hml, ones)
    yml_out[...] = (hml * lax.rsqrt(ms + EPS) * mln_ref[...] * _sigmoid(mo_ref[...])).astype(BF16)
    og = ogla_ref[...].T
    ms = _head_mean(og * og, ones)
    gg = gg_ref[...]
    ygla_out[...] = (og * lax.rsqrt(ms + EPS) * gln_ref[...] * (gg * _sigmoid(gg))).astype(BF16)
    yrw_out[...] = ((_head_layernorm(orw_ref[...].T, rwn_ref[...], ones) + bonus_ref[...])
                    * g_ref[...]).astype(BF16)


def _step_post(hml, mo, ogla, gg, orw, g, bonus, p):
    n = mo.shape[0]
    data = (hml, mo, ogla, gg, orw, g, bonus)
    prms = [p[k] for k in ("ml_norm", "gla_norm", "rw_norm")]
    full = lambda a: pl.BlockSpec(a.shape, lambda i: (0,) * a.ndim)
    out = jax.ShapeDtypeStruct((n, W_GROUP), BF16)
    return pl.pallas_call(
        _step_post_kernel,
        grid=(1,),
        in_specs=[full(a) for a in data] + [q.spec() for q in prms],
        out_specs=[pl.BlockSpec((n, W_GROUP), lambda i: (0, 0))] * 3,
        out_shape=[out] * 3,
        compiler_params=_cparams("arbitrary"),
        name="step_post",
    )(*data, *[q.arr for q in prms])


def _prepare_params(prm):
    P = {}
    w_in = prm["w_in"].astype(BF16)
    depth, d = w_in.shape[0], w_in.shape[1]
    sizes = (W_GROUP, W_GROUP, W_GROUP, W_GROUP, N_HEADS, N_HEADS, W_GROUP, W_GROUP, W_GROUP, W_GROUP,
             GLA_RANK, W_GROUP, RWKV_COLS)
    offs = [0]
    for sz in sizes:
        offs.append(offs[-1] + sz)
    col = lambda i: w_in[:, :, offs[i]:offs[i + 1]]
    (u, mq, mk, mv, mi, mf, mo, gq, gk, gv, ga, gg, rcols) = [col(i) for i in range(13)]
    small = jnp.concatenate([ga, mi, mf, jnp.zeros((depth, d, SMALL_W - GLA_RANK - 2 * N_HEADS), BF16)], axis=2)
    P["w_in"] = jnp.concatenate([rcols, small, u, mq, mk, mv, mo, gq, gk, gv, gg], axis=2)
    P["w_gate_t"] = jnp.swapaxes(jnp.concatenate([mi, mf], axis=2), 1, 2)

    lam = lax.complex(prm["s5_lam_re"], prm["s5_lam_im"])
    dt = jnp.exp(prm["s5_log_dt"])[..., None]
    lam_bar = jnp.exp(lam * dt)
    b_bar = ((lam_bar - 1.0) / lam)[..., None] * lax.complex(prm["s5_b_re"], prm["s5_b_im"])
    eye = jnp.eye(S5_GROUPS, dtype=F32)
    bm = lambda b: jnp.einsum("lgph,gk->lghkp", b, eye).reshape(depth, W_GROUP, S5_WIDTH)
    P["bmat"] = jnp.concatenate([bm(b_bar.real), bm(b_bar.imag)], axis=2).astype(BF16)
    cm = lambda c: jnp.einsum("lghp,gk->lkpgh", c, eye).reshape(depth, S5_WIDTH, W_GROUP)
    P["cmat"] = jnp.concatenate([cm(prm["s5_c_re"]), -cm(prm["s5_c_im"])], axis=1).astype(BF16)
    P["lam"] = jnp.stack([lam_bar.real.reshape(depth, S5_WIDTH), lam_bar.imag.reshape(depth, S5_WIDTH)],
                         axis=1)[:, :, None, :]
    P["wglu"] = prm["s5_w_glu"].astype(BF16)

    bias = prm["ml_gate_bias"]
    P["ml_brow"] = jnp.pad(bias, ((0, 0), (LANE_MI, SMALL_W - LANE_MI - 2 * N_HEADS)))[:, None, :]
    P["ml_bcol"] = jnp.broadcast_to(bias[:, :, None], (depth, 2 * N_HEADS, CHUNK_ML))

    P["rows"] = jnp.stack([prm["s5_d"], prm["ml_norm"], prm["gla_b_alpha"], prm["gla_norm"], prm["rw_w0"],
                           prm["rw_a0"], prm["rw_k_k"], prm["rw_k_a"], prm["rw_r_k"].reshape(depth, W_GROUP),
                           prm["rw_norm"]], axis=1)[:, :, None, :]
    pad_rows = lambda w, lo: jnp.pad(w, ((0, 0), (lo, LANES - lo - w.shape[1]), (0, 0)))
    P["pads"] = jnp.stack([pad_rows(prm["gla_w_alpha"], 0), pad_rows(prm["rw_w2"], 0),
                           pad_rows(prm["rw_a2"], RWKV_W_RANK),
                           pad_rows(prm["rw_g2"], RWKV_W_RANK + RWKV_A_RANK)], axis=1).astype(BF16)
    P["rw_mu"] = prm["rw_mu"][:, None, :]
    P["norms"] = jnp.stack([prm["norm_mix"], prm["norm_ffn"]], axis=1)[:, :, None, :]
    P["w_out"] = prm["w_out"].astype(BF16)
    P["ffn_wup"] = prm["ffn_w_up"].astype(BF16)
    P["ffn_cw"] = prm["ffn_conv_w"]
    P["ffn_cb"] = prm["ffn_conv_b"][:, None, :]
    P["ffn_wdn"] = prm["ffn_w_down"].astype(BF16)
    return P


ROW_NAMES = ("d", "ml_norm", "gla_ba", "gla_norm", "rw_w0", "rw_a0", "rw_kk", "rw_ka", "rw_rk", "rw_norm")
PAD_NAMES = ("gla_wa", "rw_w2", "rw_a2", "rw_g2")


def _layer_params(P, l):
    p = {k: _Param(P[k], l) for k in ("w_in", "w_gate_t", "bmat", "cmat", "wglu", "ml_brow", "ml_bcol",
                                      "rw_mu", "w_out", "ffn_wup", "ffn_cw", "ffn_cb", "ffn_wdn")}
    p.update({k: _Param(P["rows"], l, i) for i, k in enumerate(ROW_NAMES)})
    p.update({k: _Param(P["pads"], l, i) for i, k in enumerate(PAD_NAMES)})
    p["lre"], p["lim"] = _Param(P["lam"], l, 0), _Param(P["lam"], l, 1)
    p["norm_mix"], p["norm_ffn"] = _Param(P["norms"], l, 0), _Param(P["norms"], l, 1)
    return p


def _prompt_layer(x2d, p, norm_final, n_seq, t_len, final):
    proj, gt, u = _inproj(x2d, p["norm_mix"], p["w_in"], p["w_gate_t"], n_seq, t_len, tm=INPROJ_ROWS)

    zeros = jnp.zeros((n_seq, S5_WIDTH), F32)
    y_s5, s5_re, s5_im = _s5(u, zeros, zeros, p, n_seq, t_len, tt=S5_TILE_STEPS, wide=True)

    (y_ml, ml_c, ml_n, ml_m), (y_gla, gla_st) = _mlstm_gla(proj, gt, p, n_seq, t_len)
    y_rw, rw_s, rw_shift = _rwkv(proj, p, n_seq, t_len)

    x_new, ffn_buf = _ffn_seq(x2d, (y_s5, y_ml, y_gla, y_rw), p, norm_final, n_seq, t_len, tm=FFN_ROWS,
                              final=final)
    states = (
        s5_re.reshape(n_seq, S5_GROUPS, S5_STATE),
        s5_im.reshape(n_seq, S5_GROUPS, S5_STATE),
        ml_c,
        ml_n[:, 0, :].reshape(n_seq, N_HEADS, HEAD_DIM),
        ml_m[:, 0, LANE_MI:LANE_MI + N_HEADS],
        gla_st,
        rw_s,
        rw_shift[:, 0, :],
        ffn_buf,
    )
    return x_new, states


def _sample_layer(x2d, st, pair_states, layer, p, norm_final, final):
    (s5_re, s5_im, ml_m, rw_shift, ffn_buf) = st
    n = x2d.shape[0]
    npairs = n * N_HEADS
    proj, _, u = _inproj(x2d, p["norm_mix"], p["w_in"], p["w_gate_t"], 1, n, tm=n)

    y_s5, s5_re_new, s5_im_new = _s5(u,
                                     s5_re.reshape(n, S5_WIDTH), s5_im.reshape(n, S5_WIDTH),
                                     p, n, 1, tt=1, wide=False)

    m_pad = jnp.zeros((n, SMALL_W), F32).at[:, LANE_MI:LANE_MI + N_HEADS].set(ml_m)
    pack, m_new, g, bonus = _step_vectors(proj, rw_shift, m_pad, p)
    ml_c_new, gla_s_new, rw_s_new, opack = _heads_step(pack, *pair_states, layer)

    main = lambda blk: proj[:, blk * W_GROUP:(blk + 1) * W_GROUP]
    y_ml, y_gla, y_rw = _step_post(opack[OROW_ML], main(BLK_MO), opack[OROW_GLA], main(BLK_GG),
                                   opack[OROW_RW], g, bonus, p)
    x_new, ug = _ffn_step(x2d, (y_s5, y_ml, y_gla, y_rw), p, norm_final, ffn_buf[:, 0, :], ffn_buf[:, 1, :],
                          final)
    states = (
        s5_re_new.reshape(n, S5_GROUPS, S5_STATE),
        s5_im_new.reshape(n, S5_GROUPS, S5_STATE),
        ml_c_new.transpose(3, 0, 1, 2),
        opack[OROW_N].reshape(N_HEADS, HEAD_DIM, n).transpose(2, 0, 1),
        m_new[:, LANE_MI:LANE_MI + N_HEADS],
        gla_s_new.transpose(3, 0, 1, 2),
        rw_s_new.transpose(3, 0, 1, 2),
        proj[:, 0:RWKV_COLS],
        jnp.stack([ffn_buf[:, 1, :], ug], axis=1),
    )
    return x_new, states


def kernel(x_prompt, x_sample, state_s5_re, state_s5_im, state_mlstm_C, state_mlstm_n, state_mlstm_m, state_gla_S, state_rwkv_S, state_rwkv_shift, state_ffn_conv, norm_mix, w_in, s5_lam_re, s5_lam_im, s5_log_dt, s5_b_re, s5_b_im, s5_c_re, s5_c_im, s5_d, s5_w_glu, ml_gate_bias, ml_norm, gla_w_alpha, gla_b_alpha, gla_norm, rw_mu, rw_w0, rw_w2, rw_a0, rw_a2, rw_g2, rw_k_k, rw_k_a, rw_r_k, rw_norm, w_out, norm_ffn, ffn_w_up, ffn_conv_w, ffn_conv_b, ffn_w_down, norm_final):
    prm = dict(norm_mix=norm_mix, w_in=w_in, s5_lam_re=s5_lam_re, s5_lam_im=s5_lam_im, s5_log_dt=s5_log_dt,
               s5_b_re=s5_b_re, s5_b_im=s5_b_im, s5_c_re=s5_c_re, s5_c_im=s5_c_im, s5_d=s5_d,
               s5_w_glu=s5_w_glu, ml_gate_bias=ml_gate_bias, ml_norm=ml_norm, gla_w_alpha=gla_w_alpha,
               gla_b_alpha=gla_b_alpha, gla_norm=gla_norm, rw_mu=rw_mu, rw_w0=rw_w0, rw_w2=rw_w2,
               rw_a0=rw_a0, rw_a2=rw_a2, rw_g2=rw_g2, rw_k_k=rw_k_k, rw_k_a=rw_k_a, rw_r_k=rw_r_k,
               rw_norm=rw_norm, w_out=w_out, norm_ffn=norm_ffn, ffn_w_up=ffn_w_up, ffn_conv_w=ffn_conv_w,
               ffn_conv_b=ffn_conv_b, ffn_w_down=ffn_w_down)
    depth = w_in.shape[0]
    n_seq, t_len, d = x_prompt.shape
    n_smp = x_sample.shape[0]
    assert t_len % max(INPROJ_ROWS, FFN_ROWS, S5_TILE_STEPS, CHUNK_ML) == 0
    assert n_seq % max(SEQS_PER_STEP, SEQS_PER_STEP_ML) == 0 and x_sample.shape[1] == 1
    nf = _Param(norm_final[None, :])
    small_states = (state_s5_re, state_s5_im, state_mlstm_m, state_rwkv_shift, state_ffn_conv)
    lead = depth * N_HEADS
    to_lanes = lambda st: jnp.moveaxis(st, 1, -1)
    pair_states = (to_lanes(state_mlstm_C).reshape(lead, HEAD_DIM, HEAD_DIM, n_smp),
                   to_lanes(state_mlstm_n).reshape(lead, HEAD_DIM, n_smp),
                   to_lanes(state_gla_S).reshape(lead, HEAD_DIM, HEAD_DIM, n_smp),
                   to_lanes(state_rwkv_S).reshape(lead, HEAD_DIM, HEAD_DIM, n_smp))

    P = _prepare_params(prm)
    xp = x_prompt.reshape(n_seq * t_len, d)
    xs = x_sample.reshape(n_smp, d)
    p_states, s_states = [], []
    for l in range(depth):
        p = _layer_params(P, l)
        final = l == depth - 1
        xp, ps = _prompt_layer(xp, p, nf, n_seq, t_len, final)
        xs, ss = _sample_layer(xs, tuple(s[l] for s in small_states), pair_states, l, p, nf, final)
        p_states.append(ps)
        s_states.append(ss)
    new_p = tuple(jnp.stack([st[i] for st in p_states]) for i in range(9))
    new_s = tuple(jnp.stack([st[i] for st in s_states]) for i in range(9))
    return (xp.reshape(n_seq, t_len, d), xs.reshape(n_smp, 1, d)) + new_p + new_s
```

```python
import functools
import math

import jax
import jax.numpy as jnp
from jax import lax
from jax.experimental import pallas as pl
from jax.experimental.pallas import tpu as pltpu

F32 = jnp.float32
BF16 = jnp.bfloat16

LANES = 128
SUBLANES = 8
W_GROUP = 256
HEAD_DIM = 64
HEAD_SHIFT = 6
N_HEADS = 4
S5_GROUPS = 16
S5_STATE = 64
S5_WIDTH = S5_GROUPS * S5_STATE
GLA_RANK = 16
GLA_GATE_TEMP = 16.0
RWKV_W_RANK = 32
RWKV_A_RANK = 32
RWKV_G_RANK = 64
RWKV_COLS = 3 * W_GROUP + RWKV_W_RANK + RWKV_A_RANK + RWKV_G_RANK
RWKV_DECAY_SCALE = 0.6065306597126334
CONV_W = 3
EPS = 1e-6
GN_EPS = 64e-5
NEG_BIG = -1e30
EXP_CLAMP = 80.0

SMALL_W = LANES
OFF_SMALL = RWKV_COLS
OFF_MAIN = RWKV_COLS + SMALL_W
D_PROJ = OFF_MAIN + 9 * W_GROUP
LANE_MI = GLA_RANK
LANE_MF = GLA_RANK + N_HEADS
GATE_ROWS = 2 * N_HEADS
(BLK_U, BLK_MQ, BLK_MK, BLK_MV, BLK_MO, BLK_GQ, BLK_GK, BLK_GV, BLK_GG) = range(
    OFF_MAIN // W_GROUP, OFF_MAIN // W_GROUP + 9)
BLK_SMALL = OFF_SMALL // SMALL_W

CHUNK_ML = 128
CHUNK_RW = 64
SEQS_PER_STEP = 8
SEQS_PER_STEP_ML = 8
INPROJ_ROWS = 1024
FFN_ROWS = 1024
S5_TILE_STEPS = 128
VMEM_LIMIT = 56 * 1024 * 1024


def _cparams(*sem):
    return pltpu.CompilerParams(dimension_semantics=sem, vmem_limit_bytes=VMEM_LIMIT)


class _Param:
    def __init__(self, arr, *lead):
        self.arr, self.lead = arr, tuple(lead)

    @property
    def shape(self):
        return tuple(self.arr.shape[len(self.lead):])

    def spec(self, **kw):
        lead, tail = self.lead, (0,) * len(self.shape)
        return pl.BlockSpec((None,) * len(lead) + self.shape, lambda *grid_idx: lead + tail, **kw)


def _dot(a, b):
    return jnp.dot(a.astype(BF16), b.astype(BF16), preferred_element_type=F32)


def _dot_nt(a, b):
    return lax.dot_general(a.astype(BF16), b.astype(BF16), (((1,), (1,)), ((), ())),
                           preferred_element_type=F32)


def _dot_tn(a, b):
    return lax.dot_general(a.astype(BF16), b.astype(BF16), (((0,), (0,)), ((), ())),
                           preferred_element_type=F32)


def _split2(x):
    hi = x.astype(BF16)
    lo = (x - hi.astype(F32)).astype(BF16)
    return hi, lo


def _dot_exact_rhs(a01, x):
    hi, lo = _split2(x)
    f = lambda p: jnp.dot(a01, p, preferred_element_type=F32)
    return f(hi) + f(lo)


def _dot_exact_lhs(x, b01):
    hi, lo = _split2(x)
    f = lambda p: jnp.dot(p, b01, preferred_element_type=F32)
    return f(hi) + f(lo)


def _sigmoid(x):
    return 1.0 / (1.0 + jnp.exp(-x))


def _log_sigmoid(x):
    return jnp.minimum(x, 0.0) - jnp.log(1.0 + jnp.exp(-jnp.abs(x)))


def _gelu(x):
    return 0.5 * x * (1.0 + jnp.tanh(math.sqrt(2.0 / math.pi) * (x + 0.044715 * (x * x * x))))


def _iota(shape, dim):
    return lax.broadcasted_iota(jnp.int32, shape, dim)


def _head_ones():
    r = _iota((W_GROUP, W_GROUP), 0) >> HEAD_SHIFT
    c = _iota((W_GROUP, W_GROUP), 1) >> HEAD_SHIFT
    return jnp.where(r == c, 1.0, 0.0).astype(BF16)


def _head_sum(x, ones):
    return _dot_exact_lhs(x, ones)


def _head_mean(x, ones):
    return _dot(x, ones) * (1.0 / HEAD_DIM)


def _rmsnorm_rows(x, g):
    ms = jnp.mean(x * x, axis=-1, keepdims=True)
    return x * lax.rsqrt(ms + EPS) * g


def _tril_ones(n, strict=False):
    r = _iota((n, n), 0)
    c = _iota((n, n), 1)
    return jnp.where((r > c) if strict else (r >= c), 1.0, 0.0).astype(BF16)


def _triu_ones(n):
    r = _iota((n, n), 0)
    c = _iota((n, n), 1)
    return jnp.where(r <= c, 1.0, 0.0).astype(BF16)


def _store_head_blocks(out_ref, scr_ref, transpose=False):
    for g in range(scr_ref.shape[0]):
        m = scr_ref[g].T if transpose else scr_ref[g]
        for h in range(N_HEADS):
            lo, hi = h * HEAD_DIM, (h + 1) * HEAD_DIM
            out_ref[g, h] = m[lo:hi, lo:hi]


def _HEAD_BLOCKS_SPEC(G):
    return pl.BlockSpec((G, N_HEADS, HEAD_DIM, HEAD_DIM), lambda b, c: (b, 0, 0, 0))


def _inproj_kernel(x_ref, g_ref, w_ref, wg_ref, proj_ref, gt_ref, u_ref):
    hn = _rmsnorm_rows(x_ref[...], g_ref[...]).astype(BF16)
    proj = jnp.dot(hn, w_ref[...], preferred_element_type=F32)
    proj_ref[...] = proj
    u_ref[...] = proj[:, OFF_MAIN:OFF_MAIN + W_GROUP]
    gt_ref[0] = lax.dot_general(wg_ref[...], hn, (((1,), (1,)), ((), ())),
                                preferred_element_type=F32)


def _inproj(x2d, g, w, wg, n_seq, t_len, tm):
    n, d = x2d.shape
    tiles_per_seq = t_len // tm
    return pl.pallas_call(
        _inproj_kernel,
        grid=(n // tm,),
        in_specs=[
            pl.BlockSpec((tm, d), lambda i: (i, 0)),
            g.spec(), w.spec(pipeline_mode=pl.Buffered(1)), wg.spec(),
        ],
        out_specs=[
            pl.BlockSpec((tm, D_PROJ), lambda i: (i, 0)),
            pl.BlockSpec((1, GATE_ROWS, tm), lambda i: (i // tiles_per_seq, 0, i % tiles_per_seq)),
            pl.BlockSpec((tm, W_GROUP), lambda i: (i % tiles_per_seq, i // tiles_per_seq)),
        ],
        out_shape=[
            jax.ShapeDtypeStruct((n, D_PROJ), F32),
            jax.ShapeDtypeStruct((n_seq, GATE_ROWS, t_len), F32),
            jax.ShapeDtypeStruct((t_len, n_seq * W_GROUP), F32),
        ],
        compiler_params=_cparams("arbitrary"),
        name="inproj",
    )(x2d, g.arr, w.arr, wg.arr)


def _s5_kernel(u_ref, unext_ref, h0re_ref, h0im_ref, bmat_ref, lre_ref, lim_ref, cmat_ref, d_ref, wglu_ref,
               y_ref, hre_out, him_out, bu_scr, hs_scr, hre_scr, him_scr, *, bp, tt, wide):
    i = pl.program_id(0)
    nt = S5_WIDTH // LANES

    rows_of = (lambda ref: ref[...].reshape(tt, bp, W_GROUP).reshape(tt * bp, W_GROUP)) if wide else (
        lambda ref: ref[...])

    def project(u_tile, slot):
        bu = jnp.dot(u_tile.astype(BF16), bmat_ref[...], preferred_element_type=F32)
        for j in range(2 * nt):
            bu_scr[slot, j] = bu[:, j * LANES:(j + 1) * LANES]

    @pl.when(i == 0)
    def _():
        hre_scr[...] = h0re_ref[...]
        him_scr[...] = h0im_ref[...]
        project(rows_of(u_ref), 0)

    slot = lax.rem(i, 2)
    lre = [lre_ref[:, j * LANES:(j + 1) * LANES] for j in range(nt)]
    lim = [lim_ref[:, j * LANES:(j + 1) * LANES] for j in range(nt)]

    def step(t, carry):
        rows = pl.ds(pl.multiple_of(t * bp, bp), bp)
        new = []
        for j in range(nt):
            hre, him = carry[j], carry[nt + j]
            new.append((lre[j] * hre - lim[j] * him + bu_scr[slot, j, rows, :],
                        lre[j] * him + lim[j] * hre + bu_scr[slot, nt + j, rows, :]))
        out = tuple(x[0] for x in new) + tuple(x[1] for x in new)
        for j in range(2 * nt):
            hs_scr[j, rows, :] = out[j]
        return out

    carry = tuple(hre_scr[:, j * LANES:(j + 1) * LANES] for j in range(nt)) + tuple(
        him_scr[:, j * LANES:(j + 1) * LANES] for j in range(nt))
    carry = step(0, carry) if tt == 1 else lax.fori_loop(0, tt, step, carry)
    hre = jnp.concatenate(carry[0:nt], axis=1)
    him = jnp.concatenate(carry[nt:2 * nt], axis=1)
    hre_scr[...] = hre
    him_scr[...] = him
    hre_out[...] = hre
    him_out[...] = him

    project(rows_of(unext_ref), lax.rem(i + 1, 2))
    u = rows_of(u_ref)
    hs = jnp.concatenate([hs_scr[j] for j in range(2 * nt)], axis=1)
    y = jnp.dot(hs.astype(BF16), cmat_ref[...], preferred_element_type=F32) + d_ref[...] * u
    z = _gelu(y)
    y = z * _sigmoid(_dot(z, wglu_ref[...]))
    if wide:
        y = y.reshape(tt, bp, W_GROUP).reshape(tt, bp * W_GROUP)
    y_ref[...] = y.astype(y_ref.dtype)


def _s5(u, h0re, h0im, sp, bp, t_len, tt, wide):
    nsteps = t_len // tt
    full = lambda shape: pl.BlockSpec(shape, lambda i: (0,) * len(shape))
    blk = (tt, bp * W_GROUP) if wide else (tt * bp, W_GROUP)
    tile = pl.BlockSpec(blk, lambda i: (i, 0))
    next_tile = pl.BlockSpec(blk, lambda i: (jnp.minimum(i + 1, nsteps - 1), 0))
    return pl.pallas_call(
        functools.partial(_s5_kernel, bp=bp, tt=tt, wide=wide),
        grid=(nsteps,),
        in_specs=[
            tile, next_tile,
            full((bp, S5_WIDTH)), full((bp, S5_WIDTH)),
            sp["bmat"].spec(), sp["lre"].spec(), sp["lim"].spec(), sp["cmat"].spec(), sp["d"].spec(),
            sp["wglu"].spec(),
        ],
        out_specs=[
            tile,
            full((bp, S5_WIDTH)), full((bp, S5_WIDTH)),
        ],
        out_shape=[
            jax.ShapeDtypeStruct(u.shape, BF16),
            jax.ShapeDtypeStruct((bp, S5_WIDTH), F32),
            jax.ShapeDtypeStruct((bp, S5_WIDTH), F32),
        ],
        scratch_shapes=[
            pltpu.VMEM((2, 2 * S5_WIDTH // LANES, tt * bp, LANES), F32),
            pltpu.VMEM((2 * S5_WIDTH // LANES, tt * bp, LANES), F32),
            pltpu.VMEM((bp, S5_WIDTH), F32),
            pltpu.VMEM((bp, S5_WIDTH), F32),
        ],
        compiler_params=_cparams("arbitrary"),
        name="s5_scan",
    )(u, u, h0re, h0im, sp["bmat"].arr, sp["lre"].arr, sp["lim"].arr, sp["cmat"].arr, sp["d"].arr,
      sp["wglu"].arr)


ALL_PHASES = ("init", "body", "final")


def _run_interleaved(chains):
    live = list(chains)
    while live:
        still = []
        for ch in live:
            try:
                next(ch)
                still.append(ch)
            except StopIteration:
                pass
        live = still


def _mlstm_kernel(q_ref, k_ref, v_ref, og_ref, sm_ref, gt_ref, brow_ref, bcol_ref, norm_ref,
                  y_ref, c_out, n_out, m_out, c_scr, n_scr, m_scr, *, L, nc, G, phases=ALL_PHASES):
    c = pl.program_id(1)

    def init():
        @pl.when(c == 0)
        def _():
            c_scr[...] = jnp.zeros_like(c_scr)
            n_scr[...] = jnp.zeros_like(n_scr)
            m_scr[...] = jnp.zeros_like(m_scr)

    def final():
        @pl.when(c == nc - 1)
        def _():
            _store_head_blocks(c_out, c_scr)
            n_out[...] = jnp.broadcast_to(n_scr[:, 0:1, :], n_out.shape)
            m_out[...] = m_scr[...]

    if "init" in phases:
        init()
    if "body" not in phases:
        if "final" in phases:
            final()
        return

    ones = _head_ones()
    tril = _tril_ones(L)
    triu = _triu_ones(L)
    l_shift = L.bit_length() - 1
    src = _iota((LANES, N_HEADS * L), 0)
    to_scores = jnp.where(src == LANE_MI + (_iota((LANES, N_HEADS * L), 1) >> l_shift), 1.0, 0.0).astype(BF16)
    src = _iota((LANES, W_GROUP), 0)
    to_feats = jnp.where(src == LANE_MI + (_iota((LANES, W_GROUP), 1) >> HEAD_SHIFT), 1.0, 0.0).astype(BF16)
    seg = _iota((N_HEADS * L, W_GROUP), 0) >> l_shift
    score_sum = jnp.where(seg == (_iota((N_HEADS * L, W_GROUP), 1) >> HEAD_SHIFT), 1.0, 0.0).astype(BF16)
    causal = _iota((L, N_HEADS * L), 0) >= (_iota((L, N_HEADS * L), 1) & (L - 1))
    t_idx = _iota((L, LANES), 0)

    seqs = list(range(G))
    cat = lambda xs: jnp.concatenate(xs, axis=0)
    split = lambda x: [x[i * L:(i + 1) * L] for i in seqs]

    def running_max(x):
        shift = 1
        while shift < L:
            x = jnp.maximum(x, jnp.where(t_idx >= shift, pltpu.roll(x, shift, 0), NEG_BIG))
            shift *= 2
        return x

    q = [q_ref[i] for i in seqs]
    k = [k_ref[i] * (HEAD_DIM ** -0.5) for i in seqs]
    v = [v_ref[i] for i in seqs]
    C = [c_scr[i] for i in seqs]
    n = [n_scr[i, 0:1, :] for i in seqs]
    m0 = [m_scr[i, 0:1, :] for i in seqs]
    pre = [sm_ref[i] + brow_ref[...] for i in seqs]
    gtb = [gt_ref[i] + bcol_ref[...] for i in seqs]
    bc_col = _each(lambda x: _dot_exact_rhs(tril, _log_sigmoid(x)), pre)
    bc_rows = _dot_exact_lhs(_log_sigmoid(cat(gtb)), triu)
    qC = _each(_dot_nt, q, C)
    qk = _each(lambda q_, k_: _dot_nt(q_, _stack_heads(k_)), q, k)
    qn = _head_sum(cat(_each(lambda q_, n_: q_ * n_, q, n)), ones)

    b_col = _each(lambda x: pltpu.roll(x, LANES - N_HEADS, 1), bc_col)
    r_col = _each(lambda p_, b_: p_ - b_, pre, b_col)
    mu = _each(lambda r_, m_: jnp.maximum(running_max(r_), m_), r_col, m0)
    mu_last = _each(lambda x: x[L - 1:L, :], mu)
    mu_scores = split(_dot_exact_lhs(cat(mu), to_scores))

    def weights(i):
        f0 = GATE_ROWS * i + N_HEADS
        r_rows = gtb[i][0:N_HEADS, :] - bc_rows[f0:f0 + N_HEADS, :]
        r_all = jnp.concatenate([r_rows[h:h + 1, :] for h in range(N_HEADS)], axis=1)
        return jnp.exp(jnp.where(causal, r_all - mu_scores[i], NEG_BIG)) * qk[i]

    ws = [weights(i) for i in seqs]
    wsv = _each(lambda w_, v_: _dot(w_, _stack_heads(v_)), ws, v)
    w0 = _dot(cat(_each(lambda m_, u_: jnp.exp(m_ - u_), m0, mu)), to_feats)
    m_row = _dot_exact_lhs(cat(_each(lambda b_, u_: b_ + u_, b_col, mu)), to_feats)
    den = w0 * qn + _dot(cat(ws), score_sum)
    hc = (w0 * cat(qC) + cat(wsv)) / jnp.maximum(jnp.abs(den), jnp.exp(-m_row))

    wl = split(_dot(cat(_each(lambda r_, u_: jnp.exp(r_ - u_), r_col, mu_last)), to_feats))
    w0l_rows = _each(lambda m_, u_: jnp.exp(m_ - u_), m0, mu_last)
    w0l_rows = w0l_rows + [jnp.zeros((1, LANES), F32)] * (-G % SUBLANES)
    w0l = _dot_exact_lhs(cat(w0l_rows), to_feats)
    kw = _each(lambda k_, w_: k_ * w_, k, wl)
    upd = _each(_dot_tn, v, kw)
    for i in seqs:
        c_scr[i] = w0l[i:i + 1, :] * C[i] + ones.astype(F32) * upd[i]
        n_scr[i, 0:1, :] = w0l[i:i + 1, :] * n[i] + jnp.sum(kw[i], axis=0, keepdims=True)
        m_scr[i, 0:1, :] = b_col[i][L - 1:L, :] + mu_last[i]

    ms = _head_mean(hc * hc, ones)
    y = split(hc * lax.rsqrt(ms + EPS) * norm_ref[...])
    for i in seqs:
        y_ref[i] = (y[i] * _sigmoid(og_ref[i])).astype(y_ref.dtype)

    if "final" in phases:
        final()


def _seq_spec(G, L, blk, width=W_GROUP):
    return pl.BlockSpec((G, L, width), lambda b, c: (b, c, blk))


N_ML_IN, N_GLA_IN, N_ML_OUT, N_GLA_OUT, N_ML_SCR = 9, 8, 4, 2, 3


def _mlstm_gla_kernel(*refs, L, nc, G):
    bounds = [0]
    for n in (N_ML_IN, N_GLA_IN, N_ML_OUT, N_GLA_OUT, N_ML_SCR):
        bounds.append(bounds[-1] + n)
    ml_in, gla_in, ml_out, gla_out, ml_scr = [refs[a:b] for a, b in zip(bounds[:-1], bounds[1:])]
    gla_scr = refs[bounds[-1]:]
    ml = functools.partial(_mlstm_kernel, *ml_in, *ml_out, *ml_scr, L=L, nc=nc, G=G)
    gla = functools.partial(_gla_kernel, *gla_in, *gla_out, *gla_scr, L=L, nc=nc, G=G)
    for phase in ALL_PHASES:
        ml(phases=(phase,))
        gla(phases=(phase,))


def _mlstm_gla(proj, gt, p, n_seq, t_len):
    L = CHUNK_ML
    G = SEQS_PER_STEP_ML
    nc = t_len // L
    per_seq = lambda rows, width: pl.BlockSpec((G, rows, width), lambda b, c: (b, 0, 0))
    y_spec = pl.BlockSpec((G, L, W_GROUP), lambda b, c: (b, c, 0))
    y_shape = jax.ShapeDtypeStruct((n_seq, t_len, W_GROUP), BF16)
    state_shape = jax.ShapeDtypeStruct((n_seq, N_HEADS, HEAD_DIM, HEAD_DIM), F32)
    proj3 = proj.reshape(n_seq, t_len, D_PROJ)
    y_ml, c_new, n_new, m_new, y_gla, s_new = pl.pallas_call(
        functools.partial(_mlstm_gla_kernel, L=L, nc=nc, G=G),
        grid=(n_seq // G, nc),
        in_specs=[
            _seq_spec(G, L, BLK_MQ), _seq_spec(G, L, BLK_MK), _seq_spec(G, L, BLK_MV),
            _seq_spec(G, L, BLK_MO), _seq_spec(G, L, BLK_SMALL, SMALL_W),
            pl.BlockSpec((G, GATE_ROWS, L), lambda b, c: (b, 0, c)),
            p["ml_brow"].spec(), p["ml_bcol"].spec(), p["ml_norm"].spec(),
            _seq_spec(G, L, BLK_GQ), _seq_spec(G, L, BLK_GK), _seq_spec(G, L, BLK_GV),
            _seq_spec(G, L, BLK_GG), _seq_spec(G, L, BLK_SMALL, SMALL_W),
            p["gla_wa"].spec(), p["gla_ba"].spec(), p["gla_norm"].spec(),
        ],
        out_specs=[
            y_spec, _HEAD_BLOCKS_SPEC(G), per_seq(SUBLANES, W_GROUP), per_seq(SUBLANES, LANES),
            y_spec, _HEAD_BLOCKS_SPEC(G),
        ],
        out_shape=[
            y_shape, state_shape,
            jax.ShapeDtypeStruct((n_seq, SUBLANES, W_GROUP), F32),
            jax.ShapeDtypeStruct((n_seq, SUBLANES, LANES), F32),
            y_shape, state_shape,
        ],
        scratch_shapes=[
            pltpu.VMEM((G, W_GROUP, W_GROUP), F32),
            pltpu.VMEM((G, SUBLANES, W_GROUP), F32),
            pltpu.VMEM((G, SUBLANES, LANES), F32),
            pltpu.VMEM((G, W_GROUP, W_GROUP), F32),
        ],
        compiler_params=_cparams("arbitrary", "arbitrary"),
        name="mlstm_gla_chunk",
    )(proj3, proj3, proj3, proj3, proj3, gt, p["ml_brow"].arr, p["ml_bcol"].arr, p["ml_norm"].arr,
      proj3, proj3, proj3, proj3, proj3, p["gla_wa"].arr, p["gla_ba"].arr, p["gla_norm"].arr)
    flat = lambda y: y.reshape(n_seq * t_len, W_GROUP)
    return (flat(y_ml), c_new, n_new, m_new), (flat(y_gla), s_new)


def _gla_kernel(q_ref, k_ref, v_ref, gg_ref, sm_ref, wa_ref, ba_ref, norm_ref,
                y_ref, s_out, s_scr, *, L, nc, G, phases=ALL_PHASES):
    c = pl.program_id(1)

    def init():
        @pl.when(c == 0)
        def _():
            s_scr[...] = jnp.zeros_like(s_scr)

    def final():
        @pl.when(c == nc - 1)
        def _():
            _store_head_blocks(s_out, s_scr, transpose=True)

    if "init" in phases:
        init()
    if "body" not in phases:
        if "final" in phases:
            final()
        return

    ones = _head_ones()
    tril = _tril_ones(L)
    causal = _iota((L, N_HEADS * L), 0) >= (_iota((L, N_HEADS * L), 1) & (L - 1))

    def chain(gi):
        q = q_ref[gi] * (HEAD_DIM ** -0.5)
        k = k_ref[gi]
        v = v_ref[gi]
        la = _log_sigmoid(_dot(sm_ref[gi], wa_ref[...]) + ba_ref[...]) * (1.0 / GLA_GATE_TEMP)
        yield
        bc = _dot_exact_rhs(tril, la)
        yield
        ref = bc[L // 2 - 1:L // 2, :]
        qr = q * jnp.exp(jnp.minimum(bc - ref, EXP_CLAMP))
        kh = k * jnp.exp(jnp.minimum(ref - bc, EXP_CLAMP))
        ST = s_scr[gi]
        att = jnp.where(causal, _dot_nt(qr, _stack_heads(kh)), 0.0)
        from_state = _dot_nt(q * jnp.exp(bc), ST)
        yield
        o = from_state + _dot(att, _stack_heads(v))
        b_last = bc[L - 1:L, :]
        kbar = k * jnp.exp(b_last - bc)
        s_scr[gi] = ST * jnp.exp(b_last) + ones.astype(F32) * _dot_tn(v, kbar)
        yield
        ms = _head_mean(o * o, ones)
        gg = gg_ref[gi]
        y = o * lax.rsqrt(ms + EPS) * norm_ref[...] * (gg * _sigmoid(gg))
        y_ref[gi] = y.astype(y_ref.dtype)

    _run_interleaved([chain(gi) for gi in range(G)])

    if "final" in phases:
        final()


def _rwkv_vectors(rc, prev, p_mu, p_w0, p_a0, p_kk, p_ka, p_rk, w2, a2, g2, ones):
    xm = rc + p_mu * (prev - rc)
    rr = xm[:, 0:W_GROUP]
    rk = xm[:, W_GROUP:2 * W_GROUP]
    rv = xm[:, 2 * W_GROUP:3 * W_GROUP]
    tail = xm[:, 3 * W_GROUP:RWKV_COLS]
    lw = -RWKV_DECAY_SCALE * _sigmoid(p_w0 + _dot(jnp.tanh(tail), w2))
    a = _sigmoid(p_a0 + _dot(tail, a2))
    g = _dot(_sigmoid(tail), g2)
    kk = rk * p_kk
    kk = kk * lax.rsqrt(jnp.maximum(_head_sum(kk * kk, ones), 1e-24))
    kt = rk * (1.0 + (a - 1.0) * p_ka)
    bonus = _head_sum(rr * kt * p_rk, ones) * rv
    return rr, lw, kt, rv, kk, a, g, bonus


def _head_layernorm(o, g, ones):
    mu = _head_mean(o, ones)
    oc = o - mu
    var = _head_mean(oc * oc, ones)
    return oc * lax.rsqrt(var + GN_EPS) * g


def _stack_heads(x):
    xb = x.astype(BF16)
    lane_head = _iota((1, W_GROUP), 1) >> HEAD_SHIFT
    return jnp.concatenate([jnp.where(lane_head == h, xb, jnp.zeros_like(xb)) for h in range(N_HEADS)],
                           axis=0)


def _each(f, *seqs):
    return [f(*args) for args in zip(*seqs)]


def _block_mm(x_list, y_list):
    return _each(lambda x, y: _dot(x, _stack_heads(y)), x_list, y_list)


def _unit_lower_inverse(a_list, L):
    t_idx = _iota((L, N_HEADS * L), 0)
    s_idx = _iota((L, N_HEADS * L), 1) & (L - 1)
    eye = jnp.where(t_idx == s_idx, 1.0, 0.0).astype(F32)
    in16 = (t_idx >> 4) == (s_idx >> 4)
    in32 = ((t_idx >> 5) == (s_idx >> 5)) & ((t_idx >> 4) > (s_idx >> 4))
    in64 = (t_idx >> 5) > (s_idx >> 5)
    x16 = _each(lambda a: jnp.where(in16, -a, 0.0), a_list)
    x2 = _block_mm(x16, x16)
    x4 = _block_mm(x2, x2)
    x8 = _block_mm(x4, x4)
    t = _each(lambda x: eye + x, x16)
    for xp in (x2, x4, x8):
        t = _each(lambda t_, d: t_ + d, t, _block_mm(t, xp))
    for sel in (in32, in64):
        w = _block_mm(_each(lambda a: jnp.where(sel, a, 0.0), a_list), t)
        t = _each(lambda t_, d: t_ - d, t, _block_mm(t, w))
    return t


def _rwkv_chunks(rcs, prev_rows, states, prm, ones, L):
    (mu, w0, a0, pkk, pka, prk, w2, a2, g2, norm) = prm
    first_row = _iota((L, RWKV_COLS), 0) == 0
    prevs = _each(lambda rc, pr: jnp.where(first_row, pr, pltpu.roll(rc, 1, 0)), rcs, prev_rows)
    n_seq = len(rcs)
    stacked = _rwkv_vectors(jnp.concatenate(rcs, axis=0), jnp.concatenate(prevs, axis=0),
                            mu, w0, a0, pkk, pka, prk, w2, a2, g2, ones)
    r, lw, kx, v, kk, a, g, bonus = [[x[i * L:(i + 1) * L] for i in range(n_seq)] for x in stacked]

    tril = _tril_ones(L)
    t_idx = _iota((L, N_HEADS * L), 0)
    s_idx = _iota((L, N_HEADS * L), 1) & (L - 1)
    strict = t_idx > s_idx
    incl = t_idx >= s_idx

    def core(r, lw, kx, v, kk, a, states):
        lc = _each(lambda x: _dot_exact_rhs(tril, x), lw)
        akk = _each(lambda a_, k_: a_ * k_, a, kk)
        e_neg = _each(lambda x: jnp.exp(-x), lc)
        l_last = _each(lambda x: x[L - 1:L, :], lc)
        e_end = _each(lambda ll, x: jnp.exp(ll - x), l_last, lc)
        kr = _each(lambda k_, r_, c_, w_: jnp.concatenate([k_ * jnp.exp(c_ - w_), r_ * jnp.exp(c_)], axis=0),
                   kk, r, lc, lw)
        a_hat = _each(lambda x, e: _stack_heads(x * e), akk, e_neg)
        k_hat = _each(lambda x, e: _stack_heads(x * e), kx, e_neg)
        v_bd = _each(_stack_heads, v)

        pa = _each(_dot_nt, kr, a_hat)
        pk = _each(_dot_nt, kr, k_hat)
        a_ua = _each(lambda p_: jnp.where(strict, p_[0:L], 0.0), pa)
        b_ra = _each(lambda p_: jnp.where(incl, p_[L:2 * L], 0.0), pa)
        ab_k = _each(lambda p_: jnp.concatenate([jnp.where(strict, p_[0:L], 0.0),
                                                 jnp.where(incl, p_[L:2 * L], 0.0)], axis=0), pk)
        T = _unit_lower_inverse(a_ua, L)

        from_state = _each(_dot_nt, kr, states)
        from_v = _each(_dot, ab_k, v_bd)
        rhs = _each(lambda s_, v_: s_[0:L] + v_[0:L], from_state, from_v)
        U = _block_mm(T, rhs)
        corr = _block_mm(b_ra, U)
        o = _each(lambda s_, v_, c_: s_[L:2 * L] + v_[L:2 * L] - c_, from_state, from_v, corr)
        upd = _each(lambda v_, u_, k_, a_, e: _dot_tn(jnp.concatenate([v_, u_], axis=0),
                                                      jnp.concatenate([k_ * e, -(a_ * e)], axis=0)),
                    v, U, kx, akk, e_end)
        s_new = _each(lambda s_, ll, d: s_ * jnp.exp(ll) + ones.astype(F32) * d, states, l_last, upd)
        return o, s_new

    o, s_new = core(r, lw, kx, v, kk, a, states)
    y_all = ((_head_layernorm(jnp.concatenate(o, axis=0), norm, ones) + stacked[7]) * stacked[6])
    return [y_all[i * L:(i + 1) * L] for i in range(n_seq)], s_new


def _rwkv_kernel(rc_ref, mu_ref, w0_ref, a0_ref, kk_ref, ka_ref, rk_ref, w2_ref, a2_ref, g2_ref,
                 norm_ref, y_ref, s_out, shift_out, s_scr, prev_scr, *, L, nc, G):
    c = pl.program_id(1)

    @pl.when(c == 0)
    def _():
        s_scr[...] = jnp.zeros_like(s_scr)
        prev_scr[...] = jnp.zeros_like(prev_scr)

    ones = _head_ones()
    prm = (mu_ref[...], w0_ref[...], a0_ref[...], kk_ref[...], ka_ref[...], rk_ref[...],
           w2_ref[...], a2_ref[...], g2_ref[...], norm_ref[...])
    rcs = [rc_ref[gi] for gi in range(G)]
    ys, s_new = _rwkv_chunks(rcs, [prev_scr[gi, 0:1, :] for gi in range(G)],
                             [s_scr[gi] for gi in range(G)], prm, ones, L)
    for gi in range(G):
        prev_scr[gi, 0:1, :] = rcs[gi][L - 1:L, :]
        s_scr[gi] = s_new[gi]
        y_ref[gi] = ys[gi].astype(y_ref.dtype)

    @pl.when(c == nc - 1)
    def _():
        _store_head_blocks(s_out, s_scr)
        shift_out[...] = jnp.broadcast_to(prev_scr[:, 0:1, :], shift_out.shape)


def _rwkv(proj, p, n_seq, t_len):
    L = CHUNK_RW
    G = SEQS_PER_STEP
    nc = t_len // L
    names = ("rw_mu", "rw_w0", "rw_a0", "rw_kk", "rw_ka", "rw_rk", "rw_w2", "rw_a2", "rw_g2", "rw_norm")
    y, s_new, shift = pl.pallas_call(
        functools.partial(_rwkv_kernel, L=L, nc=nc, G=G),
        grid=(n_seq // G, nc),
        in_specs=[
            pl.BlockSpec((G, L, RWKV_COLS), lambda b, c: (b, c, 0)),
        ] + [p[k].spec() for k in names],
        out_specs=[
            pl.BlockSpec((G, L, W_GROUP), lambda b, c: (b, c, 0)),
            _HEAD_BLOCKS_SPEC(G),
            pl.BlockSpec((G, SUBLANES, RWKV_COLS), lambda b, c: (b, 0, 0)),
        ],
        out_shape=[
            jax.ShapeDtypeStruct((n_seq, t_len, W_GROUP), BF16),
            jax.ShapeDtypeStruct((n_seq, N_HEADS, HEAD_DIM, HEAD_DIM), F32),
            jax.ShapeDtypeStruct((n_seq, SUBLANES, RWKV_COLS), F32),
        ],
        scratch_shapes=[
            pltpu.VMEM((G, W_GROUP, W_GROUP), F32),
            pltpu.VMEM((G, SUBLANES, RWKV_COLS), F32),
        ],
        compiler_params=_cparams("arbitrary", "arbitrary"),
        name="rwkv_chunk",
    )(proj.reshape(n_seq, t_len, D_PROJ), *[p[k].arr for k in names])
    return y.reshape(n_seq * t_len, W_GROUP), s_new, shift


FF_CHUNK = 256


def _mix_residual(x_ref, ys, wout_ref):
    acc = x_ref[...]
    for j, y in enumerate(ys):
        acc = acc + jnp.dot(y.astype(BF16), wout_ref[j * W_GROUP:(j + 1) * W_GROUP, :],
                            preferred_element_type=F32)
    return acc


def _ffn_body(x1, nrm_ref, wup_ref, cw_ref, cb_ref, wdn_ref, prev_rows, d_ff, act_scr):
    h2 = _rmsnorm_rows(x1, nrm_ref[...]).astype(BF16)
    for j in range(d_ff // FF_CHUNK):
        lo, hi = j * FF_CHUNK, (j + 1) * FF_CHUNK
        ug = jnp.dot(h2, wup_ref[:, lo:hi], preferred_element_type=F32)
        uv = jnp.dot(h2, wup_ref[:, d_ff + lo:d_ff + hi], preferred_element_type=F32)
        p2, p1 = prev_rows(j, ug)
        conv = (cb_ref[:, lo:hi] + cw_ref[0:1, lo:hi] * p2 + cw_ref[1:2, lo:hi] * p1
                + cw_ref[2:3, lo:hi] * ug)
        act_scr[:, lo:hi] = (_gelu(conv) * uv).astype(BF16)
    return x1 + jnp.dot(act_scr[...], wdn_ref[...], preferred_element_type=F32)


def _ffn_seq_kernel(x_ref, y0_ref, y1_ref, y2_ref, y3_ref, wout_ref, nrm_ref, wup_ref, cw_ref, cb_ref,
                    wdn_ref, nf_ref, o_ref, buf_out, carry_scr, act_scr, *, tm, d_ff, final):
    t = pl.program_id(1)

    @pl.when(t == 0)
    def _():
        carry_scr[...] = jnp.zeros_like(carry_scr)

    x1 = _mix_residual(x_ref, (y0_ref[...], y1_ref[...], y2_ref[...], y3_ref[...]), wout_ref)
    row = _iota((tm, FF_CHUNK), 0)

    def prev_rows(j, ug):
        lo, hi = j * FF_CHUNK, (j + 1) * FF_CHUNK
        c0 = carry_scr[0:1, lo:hi]
        c1 = carry_scr[1:2, lo:hi]
        p1 = jnp.where(row == 0, c1, pltpu.roll(ug, 1, 0))
        p2 = jnp.where(row == 0, c0, jnp.where(row == 1, c1, pltpu.roll(ug, 2, 0)))
        carry_scr[0:2, lo:hi] = ug[tm - 2:tm, :]
        return p2, p1

    out = _ffn_body(x1, nrm_ref, wup_ref, cw_ref, cb_ref, wdn_ref, prev_rows, d_ff, act_scr)
    if final:
        out = _rmsnorm_rows(out, nf_ref[...])
    o_ref[...] = out
    buf_out[0] = carry_scr[...]


def _ffn_step_kernel(x_ref, y0_ref, y1_ref, y2_ref, y3_ref, wout_ref, nrm_ref, wup_ref, cw_ref, cb_ref,
                     wdn_ref, nf_ref, p2_ref, p1_ref, o_ref, ug_out, act_scr, *, d_ff, final):
    x1 = _mix_residual(x_ref, (y0_ref[...], y1_ref[...], y2_ref[...], y3_ref[...]), wout_ref)

    def prev_rows(j, ug):
        lo, hi = j * FF_CHUNK, (j + 1) * FF_CHUNK
        ug_out[:, lo:hi] = ug
        return p2_ref[:, lo:hi], p1_ref[:, lo:hi]

    out = _ffn_body(x1, nrm_ref, wup_ref, cw_ref, cb_ref, wdn_ref, prev_rows, d_ff, act_scr)
    if final:
        out = _rmsnorm_rows(out, nf_ref[...])
    o_ref[...] = out


def _ffn_seq(x2d, ys, p, norm_final, n_seq, t_len, tm, final):
    n, d = x2d.shape
    d_ff = p["ffn_wdn"].shape[0]
    nt = t_len // tm
    rows = lambda w: pl.BlockSpec((tm, w), lambda b, t: (b * nt + t, 0))
    once = dict(pipeline_mode=pl.Buffered(1))
    return pl.pallas_call(
        functools.partial(_ffn_seq_kernel, tm=tm, d_ff=d_ff, final=final),
        grid=(n_seq, nt),
        in_specs=[rows(d), pl.BlockSpec((tm, W_GROUP), lambda b, t: (t, b))] + [rows(W_GROUP)] * 3 + [
            p["w_out"].spec(**once), p["norm_ffn"].spec(), p["ffn_wup"].spec(**once), p["ffn_cw"].spec(),
            p["ffn_cb"].spec(), p["ffn_wdn"].spec(**once), norm_final.spec()],
        out_specs=[rows(d), pl.BlockSpec((1, CONV_W - 1, d_ff), lambda b, t: (b, 0, 0))],
        out_shape=[jax.ShapeDtypeStruct((n, d), F32),
                   jax.ShapeDtypeStruct((n_seq, CONV_W - 1, d_ff), F32)],
        scratch_shapes=[pltpu.VMEM((CONV_W - 1, d_ff), F32), pltpu.VMEM((tm, d_ff), BF16)],
        compiler_params=_cparams("arbitrary", "arbitrary"),
        name="wout_ffn_seq",
    )(x2d, *ys, p["w_out"].arr, p["norm_ffn"].arr, p["ffn_wup"].arr, p["ffn_cw"].arr, p["ffn_cb"].arr,
      p["ffn_wdn"].arr, norm_final.arr)


def _step_vec_kernel(proj_ref, shift_ref, m_ref, brow_ref, wa_ref, ba_ref, mu_ref, w0_ref, a0_ref,
                     kk_ref, ka_ref, rk_ref, w2_ref, a2_ref, g2_ref,
                     pack_out, m_out, g_out, bonus_out):
    ones = _head_ones()
    sm = proj_ref[:, OFF_SMALL:OFF_SMALL + SMALL_W]
    pre = sm + brow_ref[...]
    logf = pltpu.roll(_log_sigmoid(pre), LANES - N_HEADS, 1)
    m_old = m_ref[...]
    m_new = jnp.maximum(logf + m_old, pre)
    m_out[...] = m_new
    lane = _iota((SMALL_W, W_GROUP), 0)
    spread_mat = jnp.where(lane == LANE_MI + (_iota((SMALL_W, W_GROUP), 1) >> HEAD_SHIFT), 1.0, 0.0).astype(BF16)
    gate_lanes = (_iota(pre.shape, 1) >= LANE_MI) & (_iota(pre.shape, 1) < LANE_MI + N_HEADS)
    spread = lambda x: _dot_exact_lhs(jnp.where(gate_lanes, x, 0.0), spread_mat)
    la = _log_sigmoid(_dot(sm, wa_ref[...]) + ba_ref[...]) * (1.0 / GLA_GATE_TEMP)
    rc = proj_ref[:, 0:RWKV_COLS]
    rr, lw, kt, rv, kk, a, g, bonus = _rwkv_vectors(
        rc, shift_ref[...], mu_ref[...], w0_ref[...], a0_ref[...], kk_ref[...], ka_ref[...],
        rk_ref[...], w2_ref[...], a2_ref[...], g2_ref[...], ones)
    main = lambda blk: proj_ref[:, blk * W_GROUP:(blk + 1) * W_GROUP]
    rows = {
        ROW_MQ: main(BLK_MQ), ROW_MK: main(BLK_MK), ROW_MV: main(BLK_MV),
        ROW_GQ: main(BLK_GQ), ROW_GK: main(BLK_GK), ROW_GV: main(BLK_GV), ROW_GA: jnp.exp(la),
        ROW_KK: kk, ROW_AKK: a * kk, ROW_W: jnp.exp(lw), ROW_KT: kt, ROW_RR: rr, ROW_RV: rv,
        ROW_IW: spread(jnp.exp(pre - m_new)), ROW_FW: spread(jnp.exp(logf + m_old - m_new)),
        ROW_M: spread(m_new),
    }
    for i, val in rows.items():
        pack_out[i] = val.T
    g_out[...] = g
    bonus_out[...] = bonus


def _step_vectors(proj, shift, m_pad, p):
    n = proj.shape[0]
    full = lambda a: pl.BlockSpec(a.shape, lambda i: (0,) * a.ndim)
    data = (proj, shift, m_pad)
    prms = [p[k] for k in ("ml_brow", "gla_wa", "gla_ba", "rw_mu", "rw_w0", "rw_a0", "rw_kk", "rw_ka",
                           "rw_rk", "rw_w2", "rw_a2", "rw_g2")]
    wide = jax.ShapeDtypeStruct((n, W_GROUP), F32)
    outs = [jax.ShapeDtypeStruct((N_PACK, W_GROUP, n), F32), jax.ShapeDtypeStruct((n, SMALL_W), F32),
            wide, wide]
    return pl.pallas_call(
        _step_vec_kernel,
        grid=(1,),
        in_specs=[full(a) for a in data] + [q.spec() for q in prms],
        out_specs=[pl.BlockSpec(o.shape, lambda i, nd=len(o.shape): (0,) * nd) for o in outs],
        out_shape=outs,
        compiler_params=_cparams("arbitrary"),
        name="step_vectors",
    )(*data, *[q.arr for q in prms])


(ROW_MQ, ROW_MK, ROW_MV, ROW_GQ, ROW_GK, ROW_GV, ROW_GA, ROW_KK, ROW_AKK, ROW_W, ROW_KT, ROW_RR, ROW_RV,
 ROW_IW, ROW_FW, ROW_M) = range(16)
N_PACK = 16
(OROW_ML, OROW_GLA, OROW_RW, OROW_N) = range(4)
N_OPACK = 4
STEP_UNROLL = 4


def _heads_step_kernel(pack_ref, c_ref, n_ref, gs_ref, rs_ref, c_out, gs_out, rs_out, opack_out):
    vec = lambda i: pack_ref[i]
    one = lambda i, j: pack_ref[i, pl.ds(j, 1), :]
    colsum = lambda x: jnp.sum(x, axis=0, keepdims=True)

    k_ml = vec(ROW_MK) * (HEAD_DIM ** -0.5)
    q_ml = vec(ROW_MQ)
    iw = pack_ref[ROW_IW, 0:1, :]
    fw = pack_ref[ROW_FW, 0:1, :]
    n_new = fw * n_ref[...] + iw * k_ml
    opack_out[OROW_N] = n_new
    r_den = 1.0 / jnp.maximum(jnp.abs(colsum(n_new * q_ml)), jnp.exp(-pack_ref[ROW_M, 0:1, :]))

    def ml_body(v, carry):
        c_new = fw * c_ref[v] + (iw * one(ROW_MV, v)) * k_ml
        c_out[v] = c_new
        opack_out[OROW_ML, pl.ds(v, 1), :] = colsum(c_new * q_ml) * r_den
        return carry

    lax.fori_loop(0, HEAD_DIM, ml_body, 0, unroll=STEP_UNROLL)

    v_gla = vec(ROW_GV)

    def gla_body(k, acc):
        s_new = one(ROW_GA, k) * gs_ref[k] + one(ROW_GK, k) * v_gla
        gs_out[k] = s_new
        return acc + (one(ROW_GQ, k) * (HEAD_DIM ** -0.5)) * s_new

    opack_out[OROW_GLA] = lax.fori_loop(0, HEAD_DIM, gla_body, jnp.zeros_like(v_gla), unroll=STEP_UNROLL)

    kk, akk, w, kt, rr = vec(ROW_KK), vec(ROW_AKK), vec(ROW_W), vec(ROW_KT), vec(ROW_RR)

    def rw_body(v, carry):
        s = rs_ref[v]
        s_new = s * w - colsum(s * kk) * akk + one(ROW_RV, v) * kt
        rs_out[v] = s_new
        opack_out[OROW_RW, pl.ds(v, 1), :] = colsum(s_new * rr)
        return carry

    lax.fori_loop(0, HEAD_DIM, rw_body, 0, unroll=STEP_UNROLL)


def _heads_step(pack, ml_c, ml_n, gla_s, rw_s, layer):
    n = pack.shape[-1]
    sq = (HEAD_DIM, HEAD_DIM, n)
    head = lambda lead: pl.BlockSpec((lead, HEAD_DIM, n), lambda h: (0, h, 0))
    at_layer = lambda shape: pl.BlockSpec((None,) + shape, lambda h: (layer * N_HEADS + h,) + (0,) * len(shape))
    out_state = pl.BlockSpec((None,) + sq, lambda h: (h, 0, 0, 0))
    state = jax.ShapeDtypeStruct((N_HEADS,) + sq, F32)
    return pl.pallas_call(
        _heads_step_kernel,
        grid=(N_HEADS,),
        in_specs=[head(N_PACK), at_layer(sq), at_layer((HEAD_DIM, n)), at_layer(sq), at_layer(sq)],
        out_specs=[out_state, out_state, out_state, head(N_OPACK)],
        out_shape=[state, state, state, jax.ShapeDtypeStruct((N_OPACK, W_GROUP, n), F32)],
        compiler_params=_cparams("arbitrary"),
        name="heads_step",
    )(pack, ml_c, ml_n, gla_s, rw_s)


def _step_post_kernel(hml_ref, mo_ref, ogla_ref, gg_ref, orw_ref, g_ref, bonus_ref,
                      mln_ref, gln_ref, rwn_ref, yml_out, ygla_out, yrw_out):
    ones = _head_ones()
    hml = hml_ref[...].T
    ms = _head_mean(hml * hml, ones)
    yml_out[...] = (hml * lax.rsqrt(ms + EPS) * mln_ref[...] * _sigmoid(mo_ref[...])).astype(BF16)
    og = ogla_ref[...].T
    ms = _head_mean(og * og, ones)
    gg = gg_ref[...]
    ygla_out[...] = (og * lax.rsqrt(ms + EPS) * gln_ref[...] * (gg * _sigmoid(gg))).astype(BF16)
    yrw_out[...] = ((_head_layernorm(orw_ref[...].T, rwn_ref[...], ones) + bonus_ref[...])
                    * g_ref[...]).astype(BF16)


N_POST_IN = 10


def _post_ffn_step_kernel(*refs, d_ff, final):
    post_in, rest = refs[:N_POST_IN], refs[N_POST_IN:]
    (x_ref, y0_ref, wout_ref, nrm_ref, wup_ref, cw_ref, cb_ref, wdn_ref, nf_ref, p2_ref, p1_ref,
     o_ref, ug_out, yml_scr, ygla_scr, yrw_scr, act_scr) = rest
    _step_post_kernel(*post_in, yml_scr, ygla_scr, yrw_scr)
    _ffn_step_kernel(x_ref, y0_ref, yml_scr, ygla_scr, yrw_scr, wout_ref, nrm_ref, wup_ref, cw_ref, cb_ref,
                     wdn_ref, nf_ref, p2_ref, p1_ref, o_ref, ug_out, act_scr, d_ff=d_ff, final=final)


def _post_ffn_step(x2d, y_s5, post_data, p, norm_final, prev2, prev1, final):
    n, d = x2d.shape
    d_ff = p["ffn_wdn"].shape[0]
    full = lambda a: pl.BlockSpec(a.shape, lambda i: (0,) * a.ndim)
    prms = [p[k] for k in ("ml_norm", "gla_norm", "rw_norm")]
    weights = [p["w_out"], p["norm_ffn"], p["ffn_wup"], p["ffn_cw"], p["ffn_cb"], p["ffn_wdn"], norm_final]
    assert len(post_data) + len(prms) == N_POST_IN
    return pl.pallas_call(
        functools.partial(_post_ffn_step_kernel, d_ff=d_ff, final=final),
        grid=(1,),
        in_specs=([full(a) for a in post_data] + [q.spec() for q in prms] + [full(x2d), full(y_s5)]
                  + [q.spec() for q in weights] + [full(prev2), full(prev1)]),
        out_specs=[pl.BlockSpec((n, d), lambda i: (0, 0)), pl.BlockSpec((n, d_ff), lambda i: (0, 0))],
        out_shape=[jax.ShapeDtypeStruct((n, d), F32), jax.ShapeDtypeStruct((n, d_ff), F32)],
        scratch_shapes=[pltpu.VMEM((n, W_GROUP), BF16)] * 3 + [pltpu.VMEM((n, d_ff), BF16)],
        compiler_params=_cparams("arbitrary"),
        name="post_ffn_step",
    )(*post_data, *[q.arr for q in prms], x2d, y_s5, *[q.arr for q in weights], prev2, prev1)


def _prepare_params(prm):
    P = {}
    w_in = prm["w_in"].astype(BF16)
    depth, d = w_in.shape[0], w_in.shape[1]
    sizes = (W_GROUP, W_GROUP, W_GROUP, W_GROUP, N_HEADS, N_HEADS, W_GROUP, W_GROUP, W_GROUP, W_GROUP,
             GLA_RANK, W_GROUP, RWKV_COLS)
    offs = [0]
    for sz in sizes:
        offs.append(offs[-1] + sz)
    col = lambda i: w_in[:, :, offs[i]:offs[i + 1]]
    (u, mq, mk, mv, mi, mf, mo, gq, gk, gv, ga, gg, rcols) = [col(i) for i in range(13)]
    small = jnp.concatenate([ga, mi, mf, jnp.zeros((depth, d, SMALL_W - GLA_RANK - 2 * N_HEADS), BF16)], axis=2)
    P["w_in"] = jnp.concatenate([rcols, small, u, mq, mk, mv, mo, gq, gk, gv, gg], axis=2)
    P["w_gate_t"] = jnp.swapaxes(jnp.concatenate([mi, mf], axis=2), 1, 2)

    lam = lax.complex(prm["s5_lam_re"], prm["s5_lam_im"])
    dt = jnp.exp(prm["s5_log_dt"])[..., None]
    lam_bar = jnp.exp(lam * dt)
    b_bar = ((lam_bar - 1.0) / lam)[..., None] * lax.complex(prm["s5_b_re"], prm["s5_b_im"])
    eye = jnp.eye(S5_GROUPS, dtype=F32)
    bm = lambda b: jnp.einsum("lgph,gk->lghkp", b, eye).reshape(depth, W_GROUP, S5_WIDTH)
    P["bmat"] = jnp.concatenate([bm(b_bar.real), bm(b_bar.imag)], axis=2).astype(BF16)
    cm = lambda c: jnp.einsum("lghp,gk->lkpgh", c, eye).reshape(depth, S5_WIDTH, W_GROUP)
    P["cmat"] = jnp.concatenate([cm(prm["s5_c_re"]), -cm(prm["s5_c_im"])], axis=1).astype(BF16)
    P["lam"] = jnp.stack([lam_bar.real.reshape(depth, S5_WIDTH), lam_bar.imag.reshape(depth, S5_WIDTH)],
                         axis=1)[:, :, None, :]
    P["wglu"] = prm["s5_w_glu"].astype(BF16)

    bias = prm["ml_gate_bias"]
    P["ml_brow"] = jnp.pad(bias, ((0, 0), (LANE_MI, SMALL_W - LANE_MI - 2 * N_HEADS)))[:, None, :]
    P["ml_bcol"] = jnp.broadcast_to(bias[:, :, None], (depth, 2 * N_HEADS, CHUNK_ML))

    P["rows"] = jnp.stack([prm["s5_d"], prm["ml_norm"], prm["gla_b_alpha"], prm["gla_norm"], prm["rw_w0"],
                           prm["rw_a0"], prm["rw_k_k"], prm["rw_k_a"], prm["rw_r_k"].reshape(depth, W_GROUP),
                           prm["rw_norm"]], axis=1)[:, :, None, :]
    pad_rows = lambda w, lo: jnp.pad(w, ((0, 0), (lo, LANES - lo - w.shape[1]), (0, 0)))
    P["pads"] = jnp.stack([pad_rows(prm["gla_w_alpha"], 0), pad_rows(prm["rw_w2"], 0),
                           pad_rows(prm["rw_a2"], RWKV_W_RANK),
                           pad_rows(prm["rw_g2"], RWKV_W_RANK + RWKV_A_RANK)], axis=1).astype(BF16)
    P["rw_mu"] = prm["rw_mu"][:, None, :]
    P["norms"] = jnp.stack([prm["norm_mix"], prm["norm_ffn"]], axis=1)[:, :, None, :]
    P["w_out"] = prm["w_out"].astype(BF16)
    P["ffn_wup"] = prm["ffn_w_up"].astype(BF16)
    P["ffn_cw"] = prm["ffn_conv_w"]
    P["ffn_cb"] = prm["ffn_conv_b"][:, None, :]
    P["ffn_wdn"] = prm["ffn_w_down"].astype(BF16)
    return P


ROW_NAMES = ("d", "ml_norm", "gla_ba", "gla_norm", "rw_w0", "rw_a0", "rw_kk", "rw_ka", "rw_rk", "rw_norm")
PAD_NAMES = ("gla_wa", "rw_w2", "rw_a2", "rw_g2")


def _layer_params(P, l):
    p = {k: _Param(P[k], l) for k in ("w_in", "w_gate_t", "bmat", "cmat", "wglu", "ml_brow", "ml_bcol",
                                      "rw_mu", "w_out", "ffn_wup", "ffn_cw", "ffn_cb", "ffn_wdn")}
    p.update({k: _Param(P["rows"], l, i) for i, k in enumerate(ROW_NAMES)})
    p.update({k: _Param(P["pads"], l, i) for i, k in enumerate(PAD_NAMES)})
    p["lre"], p["lim"] = _Param(P["lam"], l, 0), _Param(P["lam"], l, 1)
    p["norm_mix"], p["norm_ffn"] = _Param(P["norms"], l, 0), _Param(P["norms"], l, 1)
    return p


def _prompt_layer(x2d, p, norm_final, n_seq, t_len, final):
    proj, gt, u = _inproj(x2d, p["norm_mix"], p["w_in"], p["w_gate_t"], n_seq, t_len, tm=INPROJ_ROWS)

    zeros = jnp.zeros((n_seq, S5_WIDTH), F32)
    y_s5, s5_re, s5_im = _s5(u, zeros, zeros, p, n_seq, t_len, tt=S5_TILE_STEPS, wide=True)

    (y_ml, ml_c, ml_n, ml_m), (y_gla, gla_st) = _mlstm_gla(proj, gt, p, n_seq, t_len)
    y_rw, rw_s, rw_shift = _rwkv(proj, p, n_seq, t_len)

    x_new, ffn_buf = _ffn_seq(x2d, (y_s5, y_ml, y_gla, y_rw), p, norm_final, n_seq, t_len, tm=FFN_ROWS,
                              final=final)
    states = (
        s5_re.reshape(n_seq, S5_GROUPS, S5_STATE),
        s5_im.reshape(n_seq, S5_GROUPS, S5_STATE),
        ml_c,
        ml_n[:, 0, :].reshape(n_seq, N_HEADS, HEAD_DIM),
        ml_m[:, 0, LANE_MI:LANE_MI + N_HEADS],
        gla_st,
        rw_s,
        rw_shift[:, 0, :],
        ffn_buf,
    )
    return x_new, states


def _sample_layer(x2d, st, pair_states, layer, p, norm_final, final):
    (s5_re, s5_im, ml_m, rw_shift, ffn_buf) = st
    n = x2d.shape[0]
    npairs = n * N_HEADS
    proj, _, u = _inproj(x2d, p["norm_mix"], p["w_in"], p["w_gate_t"], 1, n, tm=n)

    y_s5, s5_re_new, s5_im_new = _s5(u,
                                     s5_re.reshape(n, S5_WIDTH), s5_im.reshape(n, S5_WIDTH),
                                     p, n, 1, tt=1, wide=False)

    m_pad = jnp.zeros((n, SMALL_W), F32).at[:, LANE_MI:LANE_MI + N_HEADS].set(ml_m)
    pack, m_new, g, bonus = _step_vectors(proj, rw_shift, m_pad, p)
    ml_c_new, gla_s_new, rw_s_new, opack = _heads_step(pack, *pair_states, layer)

    main = lambda blk: proj[:, blk * W_GROUP:(blk + 1) * W_GROUP]
    post_data = (opack[OROW_ML], main(BLK_MO), opack[OROW_GLA], main(BLK_GG), opack[OROW_RW], g, bonus)
    x_new, ug = _post_ffn_step(x2d, y_s5, post_data, p, norm_final, ffn_buf[:, 0, :], ffn_buf[:, 1, :], final)
    states = (
        s5_re_new.reshape(n, S5_GROUPS, S5_STATE),
        s5_im_new.reshape(n, S5_GROUPS, S5_STATE),
        ml_c_new.transpose(3, 0, 1, 2),
        opack[OROW_N].reshape(N_HEADS, HEAD_DIM, n).transpose(2, 0, 1),
        m_new[:, LANE_MI:LANE_MI + N_HEADS],
        gla_s_new.transpose(3, 0, 1, 2),
        rw_s_new.transpose(3, 0, 1, 2),
        proj[:, 0:RWKV_COLS],
        jnp.stack([ffn_buf[:, 1, :], ug], axis=1),
    )
    return x_new, states


def kernel(x_prompt, x_sample, state_s5_re, state_s5_im, state_mlstm_C, state_mlstm_n, state_mlstm_m, state_gla_S, state_rwkv_S, state_rwkv_shift, state_ffn_conv, norm_mix, w_in, s5_lam_re, s5_lam_im, s5_log_dt, s5_b_re, s5_b_im, s5_c_re, s5_c_im, s5_d, s5_w_glu, ml_gate_bias, ml_norm, gla_w_alpha, gla_b_alpha, gla_norm, rw_mu, rw_w0, rw_w2, rw_a0, rw_a2, rw_g2, rw_k_k, rw_k_a, rw_r_k, rw_norm, w_out, norm_ffn, ffn_w_up, ffn_conv_w, ffn_conv_b, ffn_w_down, norm_final):
    prm = dict(norm_mix=norm_mix, w_in=w_in, s5_lam_re=s5_lam_re, s5_lam_im=s5_lam_im, s5_log_dt=s5_log_dt,
               s5_b_re=s5_b_re, s5_b_im=s5_b_im, s5_c_re=s5_c_re, s5_c_im=s5_c_im, s5_d=s5_d,
               s5_w_glu=s5_w_glu, ml_gate_bias=ml_gate_bias, ml_norm=ml_norm, gla_w_alpha=gla_w_alpha,
               gla_b_alpha=gla_b_alpha, gla_norm=gla_norm, rw_mu=rw_mu, rw_w0=rw_w0, rw_w2=rw_w2,
               rw_a0=rw_a0, rw_a2=rw_a2, rw_g2=rw_g2, rw_k_k=rw_k_k, rw_k_a=rw_k_a, rw_r_k=rw_r_k,
               rw_norm=rw_norm, w_out=w_out, norm_ffn=norm_ffn, ffn_w_up=ffn_w_up, ffn_conv_w=ffn_conv_w,
               ffn_conv_b=ffn_conv_b, ffn_w_down=ffn_w_down)
    depth = w_in.shape[0]
    n_seq, t_len, d = x_prompt.shape
    n_smp = x_sample.shape[0]
    assert t_len % max(INPROJ_ROWS, FFN_ROWS, S5_TILE_STEPS, CHUNK_ML) == 0
    assert n_seq % max(SEQS_PER_STEP, SEQS_PER_STEP_ML) == 0 and x_sample.shape[1] == 1
    nf = _Param(norm_final[None, :])
    small_states = (state_s5_re, state_s5_im, state_mlstm_m, state_rwkv_shift, state_ffn_conv)
    lead = depth * N_HEADS
    to_lanes = lambda st: jnp.moveaxis(st, 1, -1)
    pair_states = (to_lanes(state_mlstm_C).reshape(lead, HEAD_DIM, HEAD_DIM, n_smp),
                   to_lanes(state_mlstm_n).reshape(lead, HEAD_DIM, n_smp),
                   to_lanes(state_gla_S).reshape(lead, HEAD_DIM, HEAD_DIM, n_smp),
                   to_lanes(state_rwkv_S).reshape(lead, HEAD_DIM, HEAD_DIM, n_smp))

    P = _prepare_params(prm)
    xp = x_prompt.reshape(n_seq * t_len, d)
    xs = x_sample.reshape(n_smp, d)
    p_states, s_states = [], []
    for l in range(depth):
        p = _layer_params(P, l)
        final = l == depth - 1
        xp, ps = _prompt_layer(xp, p, nf, n_seq, t_len, final)
        xs, ss = _sample_layer(xs, tuple(s[l] for s in small_states), pair_states, l, p, nf, final)
        p_states.append(ps)
        s_states.append(ss)
    new_p = tuple(jnp.stack([st[i] for st in p_states]) for i in range(9))
    new_s = tuple(jnp.stack([st[i] for st in s_states]) for i in range(9))
    return (xp.reshape(n_seq, t_len, d), xs.reshape(n_smp, 1, d)) + new_p + new_s
```
